```python
import jax, jax.numpy as jnp
from jax import lax
import numpy as np


D_MODEL = 2048
BATCH = 8
SEQ = 4096
DEPTH = 2

D_A = D_MODEL // 2
A_HEAD = 128
A_HEADS = D_A // A_HEAD
A_CHUNK = 128
D_B = D_MODEL // 2
B_GROUPS = 4
B_GROUP = D_B // B_GROUPS
B_WINDOWS = (2, 4, 8, 16)
D_C = D_MODEL
C_HEAD = 128
C_HEADS = D_C // C_HEAD
C_CHUNK = 64
D_FF = 5632
CONV_W = 3
N_EVEN = (DEPTH + 1) // 2
N_ODD = DEPTH // 2
ALPHA = (2 * DEPTH) ** 0.25
BETA = (8 * DEPTH) ** -0.25
LN_EPS = 1e-5

kernel_name = 'hybrid_gmlp_pool_hgrn2_convffn_deepnorm'


def layer_norm(x, g, b):
    xf = x.astype(jnp.float32)
    mu = jnp.mean(xf, axis=-1, keepdims=True)
    var = jnp.mean(jnp.square(xf - mu), axis=-1, keepdims=True)
    return ((xf - mu) * lax.rsqrt(var + LN_EPS) * g + b).astype(x.dtype)


def rms_norm(x, g):
    xf = x.astype(jnp.float32)
    return xf * lax.rsqrt(jnp.mean(jnp.square(xf), axis=-1, keepdims=True) + LN_EPS) * g


def shift_right(x, s):
    pad = [(0, 0)] * x.ndim
    pad[1] = (s, 0)
    return jnp.pad(x, pad)[:, :x.shape[1]]


def spatial_gating(za, ln_g, ln_b, w_s, b_s):
    bn, t, _ = za.shape
    u, v = jnp.split(za, 2, axis=-1)
    v = layer_norm(v, ln_g, ln_b)
    v = v.reshape(bn, t // A_CHUNK, A_CHUNK, A_HEADS, A_HEAD)
    w = jnp.tril(w_s)
    s = jnp.einsum('hts,bnshc->bnthc', w, v) + b_s.T[None, None, :, :, None]
    return u * s.reshape(bn, t, D_A)


def multiscale_pool(xb, w_pool, scale):
    bn, t, _ = xb.shape
    xg = xb.reshape(bn, t, B_GROUPS, B_GROUP).astype(jnp.float32)
    csum = jnp.cumsum(xg, axis=1)
    pos = jnp.arange(1, t + 1, dtype=jnp.float32)
    outs = []
    for gi, win in enumerate(B_WINDOWS):
        c = csum[:, :, gi]
        wsum = c - shift_right(c, win)
        cnt = jnp.minimum(pos, float(win))[None, :, None]
        outs.append(wsum / cnt - xg[:, :, gi])
    p = jnp.stack(outs, axis=2).astype(xb.dtype)
    y = jnp.einsum('btgc,gcd->btgd', p, w_pool)
    return y.reshape(bn, t, D_B) * scale


def hgrn2(q, f_logit, inp, lb):
    bn, t, _ = q.shape
    n = t // C_CHUNK
    f32 = jnp.float32

    def heads(a):
        return a.astype(f32).reshape(bn, n, C_CHUNK, C_HEADS, C_HEAD).transpose(0, 3, 1, 2, 4)

    f = lb + (1.0 - lb) * jax.nn.sigmoid(f_logit.astype(f32))
    qh = heads(jax.nn.silu(q.astype(f32)))
    kh = heads(1.0 - f)
    vh = heads(inp)
    bcum = jnp.cumsum(heads(jnp.log(f)), axis=3)
    blast = bcum[:, :, :, -1:, :]
    q_dec = qh * jnp.exp(bcum)
    k_dec = kh * jnp.exp(-bcum)
    k_end = kh * jnp.exp(blast - bcum)
    mask = jnp.tril(jnp.ones((C_CHUNK, C_CHUNK), dtype=bool))
    att = jnp.where(mask, jnp.einsum('bhntd,bhnsd->bhnts', q_dec, k_dec), 0.0)
    o_intra = jnp.einsum('bhnts,bhnse->bhnte', att, vh)
    upd = jnp.einsum('bhnsd,bhnse->bhnde', k_end, vh)
    dec = jnp.exp(blast[:, :, :, 0, :])

    def step(state, xs):
        d_n, u_n = xs
        return d_n[..., None] * state + u_n, state

    s0 = jnp.zeros((bn, C_HEADS, C_HEAD, C_HEAD), f32)
    _, s_prev = lax.scan(step, s0, (jnp.moveaxis(dec, 2, 0), jnp.moveaxis(upd, 2, 0)))
    s_prev = jnp.moveaxis(s_prev, 0, 2)
    o = o_intra + jnp.einsum('bhntd,bhnde->bhnte', q_dec, s_prev)
    return o.transpose(0, 2, 3, 1, 4).reshape(bn, t, C_HEADS, C_HEAD)


def conv_ffn(x, w_up, conv_w, conv_b, w_down):
    h = x @ w_up
    hc = conv_b + conv_w[CONV_W - 1] * h
    for j in range(CONV_W - 1):
        hc = hc + conv_w[j] * shift_right(h, CONV_W - 1 - j)
    a, v = jnp.split(hc, 2, axis=-1)
    return (jax.nn.silu(a) * v) @ w_down


def _fwd_setup_inputs(seed: int = 0) -> dict:
    key = jax.random.key(seed)
    ks = jax.random.split(key, 24)
    f32 = jnp.float32

    def nrm(k, shape, scale):
        return jax.random.normal(k, shape, f32) * scale

    return {
        'x': nrm(ks[0], (BATCH, SEQ, D_MODEL), 1.0),
        'ev_w_in': nrm(ks[1], (N_EVEN, D_MODEL, 2 * D_A + D_B), D_MODEL ** -0.5),
        'ev_ln_v_g': 1.0 + nrm(ks[2], (N_EVEN, D_A), 0.02),
        'ev_ln_v_b': nrm(ks[3], (N_EVEN, D_A), 0.02),
        'ev_w_s': nrm(ks[4], (N_EVEN, A_HEADS, A_CHUNK, A_CHUNK), 0.5 * A_CHUNK ** -0.5),
        'ev_b_s': 1.0 + nrm(ks[5], (N_EVEN, A_HEADS, A_CHUNK), 0.02),
        'ev_w_pool': nrm(ks[6], (N_EVEN, B_GROUPS, B_GROUP, B_GROUP), B_GROUP ** -0.5),
        'ev_pool_scale': 1.0 + nrm(ks[7], (N_EVEN, D_B), 0.02),
        'ev_w_out': nrm(ks[8], (N_EVEN, D_A + D_B, D_MODEL), BETA * (D_A + D_B) ** -0.5),
        'od_w_in': nrm(ks[9], (N_ODD, D_MODEL, 4 * D_C), D_MODEL ** -0.5),
        'od_norm_g': 1.0 + nrm(ks[10], (N_ODD, D_C), 0.02),
        'od_w_out': nrm(ks[11], (N_ODD, D_C, D_MODEL), BETA * D_C ** -0.5),
        'lb_param': nrm(ks[12], (DEPTH, D_C), 0.1),
        'ffn_w_up': nrm(ks[13], (DEPTH, D_MODEL, 2 * D_FF), D_MODEL ** -0.5),
        'ffn_conv_w': nrm(ks[14], (DEPTH, CONV_W, 2 * D_FF), CONV_W ** -0.5),
        'ffn_conv_b': nrm(ks[15], (DEPTH, 2 * D_FF), 0.02),
        'ffn_w_down': nrm(ks[16], (DEPTH, D_FF, D_MODEL), BETA * D_FF ** -0.5),
        'ln1_g': 1.0 + nrm(ks[17], (DEPTH, D_MODEL), 0.02),
        'ln1_b': nrm(ks[18], (DEPTH, D_MODEL), 0.02),
        'ln2_g': 1.0 + nrm(ks[19], (DEPTH, D_MODEL), 0.02),
        'ln2_b': nrm(ks[20], (DEPTH, D_MODEL), 0.02),
    }


def _fwd_reference(x, ev_w_in, ev_ln_v_g, ev_ln_v_b, ev_w_s, ev_b_s, ev_w_pool, ev_pool_scale,
              ev_w_out, od_w_in, od_norm_g, od_w_out, lb_param, ffn_w_up, ffn_conv_w,
              ffn_conv_b, ffn_w_down, ln1_g, ln1_b, ln2_g, ln2_b):
    bn, t, _ = x.shape
    lb_all = jnp.cumsum(jax.nn.softmax(lb_param.astype(jnp.float32), axis=0), axis=0)
    lb_all = lb_all - lb_all[0]
    for l in range(DEPTH):
        if l % 2 == 0:
            e = l // 2
            h = x @ ev_w_in[e]
            za = jax.nn.gelu(h[..., :2 * D_A])
            xb = h[..., 2 * D_A:]
            ya = spatial_gating(za, ev_ln_v_g[e], ev_ln_v_b[e], ev_w_s[e], ev_b_s[e])
            yb = multiscale_pool(xb, ev_w_pool[e], ev_pool_scale[e])
            mix = jnp.concatenate([ya, yb], axis=-1) @ ev_w_out[e]
        else:
            o = l // 2
            h = x @ od_w_in[o]
            q, f_logit, inp, g = jnp.split(h, 4, axis=-1)
            y = hgrn2(q, f_logit, inp, lb_all[l])
            y = rms_norm(y, od_norm_g[o].reshape(C_HEADS, C_HEAD)).reshape(bn, t, D_C)
            y = (y * jax.nn.sigmoid(g.astype(jnp.float32))).astype(x.dtype)
            mix = y @ od_w_out[o]
        x = layer_norm(ALPHA * x + mix, ln1_g[l], ln1_b[l])
        x = layer_norm(ALPHA * x + conv_ffn(x, ffn_w_up[l], ffn_conv_w[l], ffn_conv_b[l], ffn_w_down[l]),
                       ln2_g[l], ln2_b[l])
    return x


import jax as _jax
import jax.numpy as _jnp

TWIN_FORMAT = 'train_step'
FWD_PARAMS = ['x', 'ev_w_in', 'ev_ln_v_g', 'ev_ln_v_b', 'ev_w_s', 'ev_b_s', 'ev_w_pool', 'ev_pool_scale', 'ev_w_out', 'od_w_in', 'od_norm_g', 'od_w_out', 'lb_param', 'ffn_w_up', 'ffn_conv_w', 'ffn_conv_b', 'ffn_w_down', 'ln1_g', 'ln1_b', 'ln2_g', 'ln2_b']
TWIN_WEIGHTS = ['ev_w_in', 'ev_ln_v_g', 'ev_ln_v_b', 'ev_w_s', 'ev_b_s', 'ev_w_pool', 'ev_pool_scale', 'ev_w_out', 'od_w_in', 'od_norm_g', 'od_w_out', 'lb_param', 'ffn_w_up', 'ffn_conv_w', 'ffn_conv_b', 'ffn_w_down', 'ln1_g', 'ln1_b', 'ln2_g', 'ln2_b']
TWIN_DIFF_INPUT = 'x'
TWIN_INPUTS = ['x', 'ev_w_in', 'ev_ln_v_g', 'ev_ln_v_b', 'ev_w_s', 'ev_b_s', 'ev_w_pool', 'ev_pool_scale', 'ev_w_out', 'od_w_in', 'od_norm_g', 'od_w_out', 'lb_param', 'ffn_w_up', 'ffn_conv_w', 'ffn_conv_b', 'ffn_w_down', 'ln1_g', 'ln1_b', 'ln2_g', 'ln2_b', 'loss_target', 'm_ev_w_in', 'm_ev_ln_v_g', 'm_ev_ln_v_b', 'm_ev_w_s', 'm_ev_b_s', 'm_ev_w_pool', 'm_ev_pool_scale', 'm_ev_w_out', 'm_od_w_in', 'm_od_norm_g', 'm_od_w_out', 'm_lb_param', 'm_ffn_w_up', 'm_ffn_conv_w', 'm_ffn_conv_b', 'm_ffn_w_down', 'm_ln1_g', 'm_ln1_b', 'm_ln2_g', 'm_ln2_b', 'v_ev_w_in', 'v_ev_ln_v_g', 'v_ev_ln_v_b', 'v_ev_w_s', 'v_ev_b_s', 'v_ev_w_pool', 'v_ev_pool_scale', 'v_ev_w_out', 'v_od_w_in', 'v_od_norm_g', 'v_od_w_out', 'v_lb_param', 'v_ffn_w_up', 'v_ffn_conv_w', 'v_ffn_conv_b', 'v_ffn_w_down', 'v_ln1_g', 'v_ln1_b', 'v_ln2_g', 'v_ln2_b']
TWIN_OUTPUTS = ['loss', 'grad_x', 'grad_ev_w_in', 'grad_ev_ln_v_g', 'grad_ev_ln_v_b', 'grad_ev_w_s', 'grad_ev_b_s', 'grad_ev_w_pool', 'grad_ev_pool_scale', 'grad_ev_w_out', 'grad_od_w_in', 'grad_od_norm_g', 'grad_od_w_out', 'grad_lb_param', 'grad_ffn_w_up', 'grad_ffn_conv_w', 'grad_ffn_conv_b', 'grad_ffn_w_down', 'grad_ln1_g', 'grad_ln1_b', 'grad_ln2_g', 'grad_ln2_b', 'delta_ev_w_in', 'delta_ev_ln_v_g', 'delta_ev_ln_v_b', 'delta_ev_w_s', 'delta_ev_b_s', 'delta_ev_w_pool', 'delta_ev_pool_scale', 'delta_ev_w_out', 'delta_od_w_in', 'delta_od_norm_g', 'delta_od_w_out', 'delta_lb_param', 'delta_ffn_w_up', 'delta_ffn_conv_w', 'delta_ffn_conv_b', 'delta_ffn_w_down', 'delta_ln1_g', 'delta_ln1_b', 'delta_ln2_g', 'delta_ln2_b', 'new_m_ev_w_in', 'new_m_ev_ln_v_g', 'new_m_ev_ln_v_b', 'new_m_ev_w_s', 'new_m_ev_b_s', 'new_m_ev_w_pool', 'new_m_ev_pool_scale', 'new_m_ev_w_out', 'new_m_od_w_in', 'new_m_od_norm_g', 'new_m_od_w_out', 'new_m_lb_param', 'new_m_ffn_w_up', 'new_m_ffn_conv_w', 'new_m_ffn_conv_b', 'new_m_ffn_w_down', 'new_m_ln1_g', 'new_m_ln1_b', 'new_m_ln2_g', 'new_m_ln2_b', 'new_v_ev_w_in', 'new_v_ev_ln_v_g', 'new_v_ev_ln_v_b', 'new_v_ev_w_s', 'new_v_ev_b_s', 'new_v_ev_w_pool', 'new_v_ev_pool_scale', 'new_v_ev_w_out', 'new_v_od_w_in', 'new_v_od_norm_g', 'new_v_od_w_out', 'new_v_lb_param', 'new_v_ffn_w_up', 'new_v_ffn_conv_w', 'new_v_ffn_conv_b', 'new_v_ffn_w_down', 'new_v_ln1_g', 'new_v_ln1_b', 'new_v_ln2_g', 'new_v_ln2_b']
TWIN_LEAF_KINDS = {'loss': 'loss', 'grad_x': 'grad_x', 'grad_ev_w_in': 'grad_w', 'grad_ev_ln_v_g': 'grad_w', 'grad_ev_ln_v_b': 'grad_w', 'grad_ev_w_s': 'grad_w', 'grad_ev_b_s': 'grad_w', 'grad_ev_w_pool': 'grad_w', 'grad_ev_pool_scale': 'grad_w', 'grad_ev_w_out': 'grad_w', 'grad_od_w_in': 'grad_w', 'grad_od_norm_g': 'grad_w', 'grad_od_w_out': 'grad_w', 'grad_lb_param': 'grad_w', 'grad_ffn_w_up': 'grad_w', 'grad_ffn_conv_w': 'grad_w', 'grad_ffn_conv_b': 'grad_w', 'grad_ffn_w_down': 'grad_w', 'grad_ln1_g': 'grad_w', 'grad_ln1_b': 'grad_w', 'grad_ln2_g': 'grad_w', 'grad_ln2_b': 'grad_w', 'delta_ev_w_in': 'delta_w', 'delta_ev_ln_v_g': 'delta_w', 'delta_ev_ln_v_b': 'delta_w', 'delta_ev_w_s': 'delta_w', 'delta_ev_b_s': 'delta_w', 'delta_ev_w_pool': 'delta_w', 'delta_ev_pool_scale': 'delta_w', 'delta_ev_w_out': 'delta_w', 'delta_od_w_in': 'delta_w', 'delta_od_norm_g': 'delta_w', 'delta_od_w_out': 'delta_w', 'delta_lb_param': 'delta_w', 'delta_ffn_w_up': 'delta_w', 'delta_ffn_conv_w': 'delta_w', 'delta_ffn_conv_b': 'delta_w', 'delta_ffn_w_down': 'delta_w', 'delta_ln1_g': 'delta_w', 'delta_ln1_b': 'delta_w', 'delta_ln2_g': 'delta_w', 'delta_ln2_b': 'delta_w', 'new_m_ev_w_in': 'new_m', 'new_m_ev_ln_v_g': 'new_m', 'new_m_ev_ln_v_b': 'new_m', 'new_m_ev_w_s': 'new_m', 'new_m_ev_b_s': 'new_m', 'new_m_ev_w_pool': 'new_m', 'new_m_ev_pool_scale': 'new_m', 'new_m_ev_w_out': 'new_m', 'new_m_od_w_in': 'new_m', 'new_m_od_norm_g': 'new_m', 'new_m_od_w_out': 'new_m', 'new_m_lb_param': 'new_m', 'new_m_ffn_w_up': 'new_m', 'new_m_ffn_conv_w': 'new_m', 'new_m_ffn_conv_b': 'new_m', 'new_m_ffn_w_down': 'new_m', 'new_m_ln1_g': 'new_m', 'new_m_ln1_b': 'new_m', 'new_m_ln2_g': 'new_m', 'new_m_ln2_b': 'new_m', 'new_v_ev_w_in': 'new_v', 'new_v_ev_ln_v_g': 'new_v', 'new_v_ev_ln_v_b': 'new_v', 'new_v_ev_w_s': 'new_v', 'new_v_ev_b_s': 'new_v', 'new_v_ev_w_pool': 'new_v', 'new_v_ev_pool_scale': 'new_v', 'new_v_ev_w_out': 'new_v', 'new_v_od_w_in': 'new_v', 'new_v_od_norm_g': 'new_v', 'new_v_od_w_out': 'new_v', 'new_v_lb_param': 'new_v', 'new_v_ffn_w_up': 'new_v', 'new_v_ffn_conv_w': 'new_v', 'new_v_ffn_conv_b': 'new_v', 'new_v_ffn_w_down': 'new_v', 'new_v_ln1_g': 'new_v', 'new_v_ln1_b': 'new_v', 'new_v_ln2_g': 'new_v', 'new_v_ln2_b': 'new_v'}


def _forward(args):
    return _fwd_reference(*[args[k] for k in FWD_PARAMS])


def _output_shape():
    def fwd():
        inp = _fwd_setup_inputs(0)
        return _fwd_reference(*[inp[k] for k in FWD_PARAMS])
    out = _jax.eval_shape(fwd)
    return out.shape, out.dtype

N_MICROBATCH = 1
ADAM_LR = 0.001
ADAM_B1 = 0.9
ADAM_B2 = 0.999
ADAM_EPS = 1e-08
ADAM_WD = 0.01
ADAM_STEP = 10
PER_EXAMPLE_BATCH_AXIS = {'x': 0, 'loss_target': 0}
SHARED_INPUTS = []
_WEIGHT_DTYPES = {'ev_w_in': _jnp.float32, 'ev_ln_v_g': _jnp.float32, 'ev_ln_v_b': _jnp.float32, 'ev_w_s': _jnp.float32, 'ev_b_s': _jnp.float32, 'ev_w_pool': _jnp.float32, 'ev_pool_scale': _jnp.float32, 'ev_w_out': _jnp.float32, 'od_w_in': _jnp.float32, 'od_norm_g': _jnp.float32, 'od_w_out': _jnp.float32, 'lb_param': _jnp.float32, 'ffn_w_up': _jnp.float32, 'ffn_conv_w': _jnp.float32, 'ffn_conv_b': _jnp.float32, 'ffn_w_down': _jnp.float32, 'ln1_g': _jnp.float32, 'ln1_b': _jnp.float32, 'ln2_g': _jnp.float32, 'ln2_b': _jnp.float32}
MOMENT_SCALE = {'ev_w_in': 2.121844e-02, 'ev_ln_v_g': 7.282069e-03, 'ev_ln_v_b': 7.207866e-03, 'ev_w_s': 1.458519e-02, 'ev_b_s': 2.113650e-02, 'ev_w_pool': 2.767317e-02, 'ev_pool_scale': 2.767933e-02, 'ev_w_out': 5.928119e-02, 'od_w_in': 9.971290e-03, 'od_norm_g': 1.859973e-02, 'od_w_out': 3.701814e-02, 'lb_param': 1.666958e-03, 'ffn_w_up': 1.157346e-02, 'ffn_conv_w': 1.185658e-02, 'ffn_conv_b': 1.411092e-02, 'ffn_w_down': 3.788737e-02, 'ln1_g': 5.500502e-01, 'ln1_b': 2.870598e-01, 'ln2_g': 1.133699e+01, 'ln2_b': 9.122005e-01}


def _to_microbatches(a, axis):
    t = _jnp.moveaxis(a, axis, 0)
    t = t.reshape((N_MICROBATCH, t.shape[0] // N_MICROBATCH) + t.shape[1:])
    return _jnp.moveaxis(t, 1, axis + 1)


def setup_inputs(seed: int = 0) -> dict:
    inp = _fwd_setup_inputs(seed)
    key = _jax.random.fold_in(_jax.random.key(seed), 7919)
    shape, _ = _output_shape()
    out = dict(inp)
    out["loss_target"] = _jax.random.normal(_jax.random.fold_in(key, 0), shape, _jnp.float32)
    for i, name in enumerate(TWIN_WEIGHTS):
        w = inp[name].astype(_jnp.float32)
        if MOMENT_SCALE is None:
            s = _jnp.sqrt(_jnp.mean(_jnp.square(w)) + 1e-30)
        else:
            s = MOMENT_SCALE[name]
        km, kv = _jax.random.split(_jax.random.fold_in(key, i + 1))
        out[name] = w
        out["m_" + name] = s * _jax.random.normal(km, w.shape, _jnp.float32)
        out["v_" + name] = (s * s) * _jax.random.uniform(kv, w.shape, _jnp.float32, 0.5, 1.5)
    if N_MICROBATCH > 1:
        for name, axis in PER_EXAMPLE_BATCH_AXIS.items():
            out[name] = _to_microbatches(out[name], axis)
    return {'x': out['x'], 'ev_w_in': out['ev_w_in'], 'ev_ln_v_g': out['ev_ln_v_g'], 'ev_ln_v_b': out['ev_ln_v_b'], 'ev_w_s': out['ev_w_s'], 'ev_b_s': out['ev_b_s'], 'ev_w_pool': out['ev_w_pool'], 'ev_pool_scale': out['ev_pool_scale'], 'ev_w_out': out['ev_w_out'], 'od_w_in': out['od_w_in'], 'od_norm_g': out['od_norm_g'], 'od_w_out': out['od_w_out'], 'lb_param': out['lb_param'], 'ffn_w_up': out['ffn_w_up'], 'ffn_conv_w': out['ffn_conv_w'], 'ffn_conv_b': out['ffn_conv_b'], 'ffn_w_down': out['ffn_w_down'], 'ln1_g': out['ln1_g'], 'ln1_b': out['ln1_b'], 'ln2_g': out['ln2_g'], 'ln2_b': out['ln2_b'], 'loss_target': out['loss_target'], 'm_ev_w_in': out['m_ev_w_in'], 'm_ev_ln_v_g': out['m_ev_ln_v_g'], 'm_ev_ln_v_b': out['m_ev_ln_v_b'], 'm_ev_w_s': out['m_ev_w_s'], 'm_ev_b_s': out['m_ev_b_s'], 'm_ev_w_pool': out['m_ev_w_pool'], 'm_ev_pool_scale': out['m_ev_pool_scale'], 'm_ev_w_out': out['m_ev_w_out'], 'm_od_w_in': out['m_od_w_in'], 'm_od_norm_g': out['m_od_norm_g'], 'm_od_w_out': out['m_od_w_out'], 'm_lb_param': out['m_lb_param'], 'm_ffn_w_up': out['m_ffn_w_up'], 'm_ffn_conv_w': out['m_ffn_conv_w'], 'm_ffn_conv_b': out['m_ffn_conv_b'], 'm_ffn_w_down': out['m_ffn_w_down'], 'm_ln1_g': out['m_ln1_g'], 'm_ln1_b': out['m_ln1_b'], 'm_ln2_g': out['m_ln2_g'], 'm_ln2_b': out['m_ln2_b'], 'v_ev_w_in': out['v_ev_w_in'], 'v_ev_ln_v_g': out['v_ev_ln_v_g'], 'v_ev_ln_v_b': out['v_ev_ln_v_b'], 'v_ev_w_s': out['v_ev_w_s'], 'v_ev_b_s': out['v_ev_b_s'], 'v_ev_w_pool': out['v_ev_w_pool'], 'v_ev_pool_scale': out['v_ev_pool_scale'], 'v_ev_w_out': out['v_ev_w_out'], 'v_od_w_in': out['v_od_w_in'], 'v_od_norm_g': out['v_od_norm_g'], 'v_od_w_out': out['v_od_w_out'], 'v_lb_param': out['v_lb_param'], 'v_ffn_w_up': out['v_ffn_w_up'], 'v_ffn_conv_w': out['v_ffn_conv_w'], 'v_ffn_conv_b': out['v_ffn_conv_b'], 'v_ffn_w_down': out['v_ffn_w_down'], 'v_ln1_g': out['v_ln1_g'], 'v_ln1_b': out['v_ln1_b'], 'v_ln2_g': out['v_ln2_g'], 'v_ln2_b': out['v_ln2_b']}


def _loss(weights, diff, rest, loss_target):
    with _jax.named_scope("forward"):
        args = {**rest, TWIN_DIFF_INPUT: diff, **{k: w.astype(_WEIGHT_DTYPES[k]) for k, w in weights.items()}}
        y = _forward(args)
    with _jax.named_scope("loss_head"):
        err = _jnp.square(y.astype(_jnp.float32) - loss_target)
        return 0.5 * _jnp.sum(_jnp.mean(err, axis=-1)) if err.ndim else 0.5 * err


def _adamw(w, g, m, v):
    m = ADAM_B1 * m + (1.0 - ADAM_B1) * g
    v = ADAM_B2 * v + (1.0 - ADAM_B2) * _jnp.square(g)
    m_hat = m / (1.0 - ADAM_B1 ** ADAM_STEP)
    v_hat = v / (1.0 - ADAM_B2 ** ADAM_STEP)
    delta = -ADAM_LR * (m_hat / (_jnp.sqrt(v_hat) + ADAM_EPS) + ADAM_WD * w)
    return delta, m, v


def reference(x, ev_w_in, ev_ln_v_g, ev_ln_v_b, ev_w_s, ev_b_s, ev_w_pool, ev_pool_scale, ev_w_out, od_w_in, od_norm_g, od_w_out, lb_param, ffn_w_up, ffn_conv_w, ffn_conv_b, ffn_w_down, ln1_g, ln1_b, ln2_g, ln2_b, loss_target, m_ev_w_in, m_ev_ln_v_g, m_ev_ln_v_b, m_ev_w_s, m_ev_b_s, m_ev_w_pool, m_ev_pool_scale, m_ev_w_out, m_od_w_in, m_od_norm_g, m_od_w_out, m_lb_param, m_ffn_w_up, m_ffn_conv_w, m_ffn_conv_b, m_ffn_w_down, m_ln1_g, m_ln1_b, m_ln2_g, m_ln2_b, v_ev_w_in, v_ev_ln_v_g, v_ev_ln_v_b, v_ev_w_s, v_ev_b_s, v_ev_w_pool, v_ev_pool_scale, v_ev_w_out, v_od_w_in, v_od_norm_g, v_od_w_out, v_lb_param, v_ffn_w_up, v_ffn_conv_w, v_ffn_conv_b, v_ffn_w_down, v_ln1_g, v_ln1_b, v_ln2_g, v_ln2_b):
    given = dict(x=x, ev_w_in=ev_w_in, ev_ln_v_g=ev_ln_v_g, ev_ln_v_b=ev_ln_v_b, ev_w_s=ev_w_s, ev_b_s=ev_b_s, ev_w_pool=ev_w_pool, ev_pool_scale=ev_pool_scale, ev_w_out=ev_w_out, od_w_in=od_w_in, od_norm_g=od_norm_g, od_w_out=od_w_out, lb_param=lb_param, ffn_w_up=ffn_w_up, ffn_conv_w=ffn_conv_w, ffn_conv_b=ffn_conv_b, ffn_w_down=ffn_w_down, ln1_g=ln1_g, ln1_b=ln1_b, ln2_g=ln2_g, ln2_b=ln2_b, loss_target=loss_target, m_ev_w_in=m_ev_w_in, m_ev_ln_v_g=m_ev_ln_v_g, m_ev_ln_v_b=m_ev_ln_v_b, m_ev_w_s=m_ev_w_s, m_ev_b_s=m_ev_b_s, m_ev_w_pool=m_ev_w_pool, m_ev_pool_scale=m_ev_pool_scale, m_ev_w_out=m_ev_w_out, m_od_w_in=m_od_w_in, m_od_norm_g=m_od_norm_g, m_od_w_out=m_od_w_out, m_lb_param=m_lb_param, m_ffn_w_up=m_ffn_w_up, m_ffn_conv_w=m_ffn_conv_w, m_ffn_conv_b=m_ffn_conv_b, m_ffn_w_down=m_ffn_w_down, m_ln1_g=m_ln1_g, m_ln1_b=m_ln1_b, m_ln2_g=m_ln2_g, m_ln2_b=m_ln2_b, v_ev_w_in=v_ev_w_in, v_ev_ln_v_g=v_ev_ln_v_g, v_ev_ln_v_b=v_ev_ln_v_b, v_ev_w_s=v_ev_w_s, v_ev_b_s=v_ev_b_s, v_ev_w_pool=v_ev_w_pool, v_ev_pool_scale=v_ev_pool_scale, v_ev_w_out=v_ev_w_out, v_od_w_in=v_od_w_in, v_od_norm_g=v_od_norm_g, v_od_w_out=v_od_w_out, v_lb_param=v_lb_param, v_ffn_w_up=v_ffn_w_up, v_ffn_conv_w=v_ffn_conv_w, v_ffn_conv_b=v_ffn_conv_b, v_ffn_w_down=v_ffn_w_down, v_ln1_g=v_ln1_g, v_ln1_b=v_ln1_b, v_ln2_g=v_ln2_g, v_ln2_b=v_ln2_b)
    weights = {n: given[n] for n in TWIN_WEIGHTS}
    shared = {n: given[n] for n in SHARED_INPUTS}
    per_example = {n: given[n] for n in ['x']}
    grad_fn = _jax.value_and_grad(_loss, argnums=(0, 1))

    def one_microbatch(ex, loss_target):
        ex = dict(ex)
        diff = ex.pop(TWIN_DIFF_INPUT)
        return grad_fn(weights, diff, {**shared, **ex}, loss_target)

    if N_MICROBATCH == 1:
        loss, (grad_w, grad_x) = one_microbatch(per_example, given["loss_target"])
    else:
        def body(carry, xs):
            loss_sum, grad_sum = carry
            l_k, (gw_k, gx_k) = one_microbatch(xs[0], xs[1])
            with _jax.named_scope("update"):
                return (loss_sum + l_k, _jax.tree.map(_jnp.add, grad_sum, gw_k)), gx_k

        init = (_jnp.zeros((), _jnp.float32), _jax.tree.map(_jnp.zeros_like, weights))
        (loss, grad_w), grad_x = _jax.lax.scan(body, init, (per_example, given["loss_target"]))
    with _jax.named_scope("update"):
        delta_w, new_m, new_v = {}, {}, {}
        for n in TWIN_WEIGHTS:
            delta_w[n], new_m[n], new_v[n] = _adamw(weights[n], grad_w[n], given["m_" + n], given["v_" + n])
    return (loss, grad_x, *[grad_w[n] for n in TWIN_WEIGHTS], *[delta_w[n] for n in TWIN_WEIGHTS],
            *[new_m[n] for n in TWIN_WEIGHTS], *[new_v[n] for n in TWIN_WEIGHTS])
```

```python
import math

import jax
import jax.numpy as jnp
from jax import lax
from jax.experimental import pallas as pl
from jax.experimental.pallas import tpu as pltpu

F32 = jnp.float32
BF16 = jnp.bfloat16
MESH = pl.DeviceIdType.MESH
ANY = pl.BlockSpec(memory_space=pl.ANY)
VMEM_SPEC = pl.BlockSpec(memory_space=pltpu.VMEM)

DEPTH = 2
ALPHA = (2 * DEPTH) ** 0.25
LN_EPS = 1e-5
A_HEAD = 128
A_CHUNK = 128
B_WINDOWS = (2, 4, 8, 16)
POOL_HALO = 16
C_HEAD = 128
C_CHUNK = 64
CONV_HALO = 8
ADAM_LR = 0.001
ADAM_B1 = 0.9
ADAM_B2 = 0.999
ADAM_EPS = 1e-08
ADAM_WD = 0.01
ADAM_STEP = 10
V7X_VMEM_LIMIT_BYTES = 56 * 1024 * 1024
LANES = 128
PACK_ALIGN = 8 * LANES


def _call(body, *, name, out_shape, grid=(), in_specs=None, out_specs=None, scratch=()):
    return pl.pallas_call(
        body, name=name, grid=grid, in_specs=in_specs, out_specs=out_specs, out_shape=out_shape,
        scratch_shapes=list(scratch),
        compiler_params=pltpu.CompilerParams(vmem_limit_bytes=V7X_VMEM_LIMIT_BYTES))


def _tile(n, pref, unit=LANES):
    if n <= pref:
        return n
    t = (pref // unit) * unit
    while t > unit and n % t:
        t -= unit
    assert n % t == 0, (n, pref, unit)
    return t


def _dot(a, b):
    return jnp.dot(a, b, preferred_element_type=F32)


def _dot_nt(a, b):
    return lax.dot_general(a, b, (((1,), (1,)), ((), ())), preferred_element_type=F32)


def _dot_tn(a, b):
    return lax.dot_general(a, b, (((0,), (0,)), ((), ())), preferred_element_type=F32)


def _sigmoid(x):
    return jax.nn.sigmoid(x)


_GELU_C = math.sqrt(2.0 / math.pi)


def _gelu(x):
    return 0.5 * x * (1.0 + jnp.tanh(_GELU_C * (x + 0.044715 * x * x * x)))


def _gelu_grad(x):
    th = jnp.tanh(_GELU_C * (x + 0.044715 * x * x * x))
    return 0.5 * (1.0 + th) + 0.5 * x * (1.0 - th * th) * _GELU_C * (1.0 + 3.0 * 0.044715 * x * x)


def _ln_fwd(r, g, b):
    mu = jnp.mean(r, axis=-1, keepdims=True)
    xc = r - mu
    var = jnp.mean(xc * xc, axis=-1, keepdims=True)
    rstd = lax.rsqrt(var + LN_EPS)
    xh = xc * rstd
    return xh * g + b, xh, rstd


def _ln_bwd(dy, xh, rstd, g):
    dxh = dy * g
    m1 = jnp.mean(dxh, axis=-1, keepdims=True)
    m2 = jnp.mean(dxh * xh, axis=-1, keepdims=True)
    dr = rstd * (dxh - m1 - xh * m2)
    return dr, jnp.sum(dy * xh, axis=0, keepdims=True), jnp.sum(dy, axis=0, keepdims=True)


def _exact_tri_dot(tri, x):
    hi = x.astype(BF16)
    r1 = x - hi.astype(F32)
    mid = r1.astype(BF16)
    lo = (r1 - mid.astype(F32)).astype(BF16)
    return _dot(tri, hi) + _dot(tri, mid) + _dot(tri, lo)


def cast_bf16(a3, name):
    L, R, C = a3.shape
    br = _tile(R, max(8, (1 << 20) // C), 8)

    def body(a_ref, o_ref):
        o_ref[...] = a_ref[...].astype(BF16)

    return _call(body, name=name, grid=(L, R // br),
                 in_specs=[pl.BlockSpec((None, br, C), lambda l, r: (l, r, 0))],
                 out_specs=pl.BlockSpec((None, br, C), lambda l, r: (l, r, 0)),
                 out_shape=jax.ShapeDtypeStruct((L, R, C), BF16))(a3)


def adamw(w, g, m, v, name):
    L, R, C = w.shape
    br = _tile(R, max(8, (1 << 19) // C), 8)
    c1 = 1.0 - ADAM_B1 ** ADAM_STEP
    c2 = 1.0 - ADAM_B2 ** ADAM_STEP

    def body(w_ref, g_ref, m_ref, v_ref, d_ref, nm_ref, nv_ref):
        gg = g_ref[...]
        nm = ADAM_B1 * m_ref[...] + (1.0 - ADAM_B1) * gg
        nv = ADAM_B2 * v_ref[...] + (1.0 - ADAM_B2) * (gg * gg)
        d_ref[...] = -ADAM_LR * ((nm / c1) / (jnp.sqrt(nv / c2) + ADAM_EPS) + ADAM_WD * w_ref[...])
        nm_ref[...] = nm
        nv_ref[...] = nv

    spec = pl.BlockSpec((None, br, C), lambda l, r: (l, r, 0))
    sds = jax.ShapeDtypeStruct((L, R, C), F32)
    return _call(body, name=name, grid=(L, R // br), in_specs=[spec] * 4, out_specs=[spec] * 3,
                 out_shape=[sds] * 3)(w, g, m, v)


def mm_nn(a, w, *, w_spec, P, tnw, tm, n_j, name):
    T, K = a.shape
    bw = P * tnw

    def body(a_ref, w_ref, o_ref):
        av = a_ref[...]
        for p in range(P):
            o_ref[:, p * tnw:(p + 1) * tnw] = _dot(av, w_ref[p]).astype(BF16)

    return _call(body, name=name, grid=(T // tm, n_j),
                 in_specs=[pl.BlockSpec((tm, K), lambda i, j: (i, 0)), w_spec],
                 out_specs=pl.BlockSpec((tm, bw), lambda i, j: (i, j)),
                 out_shape=jax.ShapeDtypeStruct((T, n_j * bw), BF16))(a, w)


def mm_ln(a, w, res, rg, rb, g, b, *, w_spec, K, tk, tm, name):
    T, N = res.shape
    n_k = K // tk

    def body(a_ref, w_ref, res_ref, rg_ref, rb_ref, g_ref, b_ref, xh_ref, y_ref, rs_ref, acc):
        k = pl.program_id(1)

        @pl.when(k == 0)
        def _():
            acc[...] = jnp.zeros_like(acc)

        acc[...] += _dot(a_ref[...], w_ref[...])

        @pl.when(k == n_k - 1)
        def _():
            r = ALPHA * (res_ref[...] * rg_ref[...] + rb_ref[...]) + acc[...]
            y, xh, rstd = _ln_fwd(r, g_ref[...], b_ref[...])
            xh_ref[...] = xh
            y_ref[...] = y.astype(BF16)
            rs_ref[...] = rstd

    row = pl.BlockSpec((tm, N), lambda i, k: (i, 0))
    vec = pl.BlockSpec((1, N), lambda i, k: (0, 0))
    return _call(body, name=name, grid=(T // tm, n_k),
                 in_specs=[pl.BlockSpec((tm, tk), lambda i, k: (i, k)), w_spec, row, vec, vec, vec, vec],
                 out_specs=[row, row, pl.BlockSpec((tm, 1), lambda i, k: (i, 0))],
                 out_shape=[jax.ShapeDtypeStruct((T, N), F32), jax.ShapeDtypeStruct((T, N), BF16),
                            jax.ShapeDtypeStruct((T, 1), F32)],
                 scratch=[pltpu.VMEM((tm, N), F32)])(a, w, res, rg, rb, g, b)


def mm_nt_plain(a, w, *, tm, tn, name):
    T, K = a.shape
    N = w.shape[0]

    def body(a_ref, w_ref, o_ref):
        o_ref[...] = _dot_nt(a_ref[...], w_ref[...]).astype(BF16)

    return _call(body, name=name, grid=(T // tm, N // tn),
                 in_specs=[pl.BlockSpec((tm, K), lambda i, j: (i, 0)), pl.BlockSpec((tn, K), lambda i, j: (j, 0))],
                 out_specs=pl.BlockSpec((tm, tn), lambda i, j: (i, j)),
                 out_shape=jax.ShapeDtypeStruct((T, N), BF16))(a, w)


def mm_nt_res(a, w, res, ln, *, a_spec, w_spec, P, tnw, n_k, tm, name):
    T, N = res.shape
    n_i = T // tm

    def body(*refs):
        if ln is None:
            a_ref, w_ref, res_ref, o_ref, acc = refs
        else:
            a_ref, w_ref, res_ref, xh_ref, rs_ref, g_ref, dr_ref, drb_ref, dg_ref, db_ref, acc = refs
        i = pl.program_id(0)
        k = pl.program_id(1)

        @pl.when(k == 0)
        def _():
            acc[...] = jnp.zeros_like(acc)

        s = acc[...]
        for p in range(P):
            s = s + _dot_nt(a_ref[:, p * tnw:(p + 1) * tnw], w_ref[p])
        acc[...] = s

        @pl.when(k == n_k - 1)
        def _():
            d = ALPHA * res_ref[...] + acc[...]
            if ln is None:
                o_ref[...] = d
            else:
                dr, dg, db = _ln_bwd(d, xh_ref[...], rs_ref[...], g_ref[...])
                dr_ref[...] = dr
                drb_ref[...] = dr.astype(BF16)

                @pl.when(i == 0)
                def _():
                    dg_ref[...] = jnp.zeros_like(dg_ref)
                    db_ref[...] = jnp.zeros_like(db_ref)

                dg_ref[...] += dg
                db_ref[...] += db

    row = pl.BlockSpec((tm, N), lambda i, k: (i, 0))
    vec = pl.BlockSpec((1, N), lambda i, k: (0, 0))
    scratch = [pltpu.VMEM((tm, N), F32)]
    if ln is None:
        return _call(body, name=name, grid=(n_i, n_k), in_specs=[a_spec, w_spec, row], out_specs=row,
                     out_shape=jax.ShapeDtypeStruct((T, N), F32), scratch=scratch)(a, w, res)
    xh, rstd, g = ln
    return _call(body, name=name, grid=(n_i, n_k),
                 in_specs=[a_spec, w_spec, row, row, pl.BlockSpec((tm, 1), lambda i, k: (i, 0)), vec],
                 out_specs=[row, row, vec, vec],
                 out_shape=[jax.ShapeDtypeStruct((T, N), F32), jax.ShapeDtypeStruct((T, N), BF16),
                            jax.ShapeDtypeStruct((1, N), F32), jax.ShapeDtypeStruct((1, N), F32)],
                 scratch=scratch)(a, w, res, xh, rstd, g)


def mm_tn(a, g, *, a_spec, g_spec, o_spec, out_shape, grid, acc_shape, P, tnw, name):
    n_t = grid[2]

    def body(a_ref, g_ref, o_ref, acc):
        t = pl.program_id(2)

        @pl.when(t == 0)
        def _():
            acc[...] = jnp.zeros_like(acc)

        acc[...] += _dot_tn(a_ref[...], g_ref[...])

        @pl.when(t == n_t - 1)
        def _():
            for p in range(P):
                o_ref[p] = acc[:, p * tnw:(p + 1) * tnw].astype(BF16)

    return _call(body, name=name, grid=grid, in_specs=[a_spec, g_spec], out_specs=o_spec,
                 out_shape=jax.ShapeDtypeStruct(out_shape, BF16),
                 scratch=[pltpu.VMEM(acc_shape, F32)])(a, g)


def _causal_conv(ext, halo, w, b):
    s1 = pltpu.roll(ext, 1, 0)[halo:]
    s2 = pltpu.roll(ext, 2, 0)[halo:]
    return b + w[2:3] * ext[halo:] + w[1:2] * s1 + w[0:1] * s2, s1, s2


def ffn_up(xb, wup, cw, cb, l, *, name):
    T, D = xb.shape
    Ns = wup.shape[-1]
    F = 2 * Ns
    tn = _tile(Ns, 256)
    nps = Ns // tn
    n_j = F // tn
    tm = _tile(T, 1024, 8)

    def body(x_ref, wa_ref, wv_ref, cwa_ref, cwv_ref, cba_ref, cbv_ref, h_ref, f_ref, carry):
        i = pl.program_id(1)

        @pl.when(i == 0)
        def _():
            carry[...] = jnp.zeros_like(carry)

        xv = x_ref[...]
        ha = _dot(xv, wa_ref[...])
        hv = _dot(xv, wv_ref[...])
        ca, _, _ = _causal_conv(jnp.concatenate([carry[0], ha], axis=0), CONV_HALO, cwa_ref[...], cba_ref[...])
        cv, _, _ = _causal_conv(jnp.concatenate([carry[1], hv], axis=0), CONV_HALO, cwv_ref[...], cbv_ref[...])
        carry[0] = ha[tm - CONV_HALO:]
        carry[1] = hv[tm - CONV_HALO:]
        h_ref[0] = ha.astype(BF16)
        h_ref[1] = hv.astype(BF16)
        f_ref[...] = (ca * _sigmoid(ca) * cv).astype(BF16)

    wspec_a = pl.BlockSpec((None, None, D, tn), lambda j, i: (l, j // nps, 0, j % nps))
    wspec_v = pl.BlockSpec((None, None, D, tn), lambda j, i: (l, 2 + j // nps, 0, j % nps))
    return _call(
        body, name=name, grid=(n_j, T // tm),
        in_specs=[pl.BlockSpec((tm, D), lambda j, i: (i, 0)), wspec_a, wspec_v,
                  pl.BlockSpec((None, 3, tn), lambda j, i: (l, 0, j)),
                  pl.BlockSpec((None, 3, tn), lambda j, i: (l, 0, n_j + j)),
                  pl.BlockSpec((None, 1, tn), lambda j, i: (l, 0, j)),
                  pl.BlockSpec((None, 1, tn), lambda j, i: (l, 0, n_j + j))],
        out_specs=[pl.BlockSpec((2, tm, tn), lambda j, i: (0, i, j)), pl.BlockSpec((tm, tn), lambda j, i: (i, j))],
        out_shape=[jax.ShapeDtypeStruct((2, T, F), BF16), jax.ShapeDtypeStruct((T, F), BF16)],
        scratch=[pltpu.VMEM((2, CONV_HALO, tn), F32)])(xb, wup, wup, cw, cw, cb, cb)


def ffn_dgate(db16, wdn, h, cw, cb, l, *, name):
    T, D = db16.shape
    F = h.shape[-1]
    tn = _tile(F, 512)
    n_j = F // tn
    tm = _tile(T, 512, 16)
    n_i = T // tm
    hb = 16
    n_ext = tm + CONV_HALO

    def body(d_ref, w_ref, h_ref, hh_ref, cwa_ref, cwv_ref, cba_ref, cbv_ref, dh_ref, dcw_ref, dcb_ref, carry):
        ip = pl.program_id(1)
        i = n_i - 1 - ip

        @pl.when(ip == 0)
        def _():
            carry[...] = jnp.zeros_like(carry)
            dcw_ref[...] = jnp.zeros_like(dcw_ref)
            dcb_ref[...] = jnp.zeros_like(dcb_ref)

        df = _dot_nt(d_ref[...], w_ref[...])
        halo = jnp.where(i > 0, hh_ref[...].astype(F32), 0.0)
        ha = h_ref[0].astype(F32)
        hv = h_ref[1].astype(F32)
        cwa, cwv = cwa_ref[...], cwv_ref[...]
        ca, a1, a2 = _causal_conv(jnp.concatenate([halo[0], ha], axis=0), hb, cwa, cba_ref[...])
        cv, v1, v2 = _causal_conv(jnp.concatenate([halo[1], hv], axis=0), hb, cwv, cbv_ref[...])
        sig = _sigmoid(ca)
        da = df * cv * sig * (1.0 + ca * (1.0 - sig))
        dv = df * ca * sig
        for half, (dc, h0, h1, h2, w) in enumerate(((da, ha, a1, a2, cwa), (dv, hv, v1, v2, cwv))):
            dcb_ref[half] += jnp.sum(dc, axis=0, keepdims=True)
            dcw_ref[half] += jnp.concatenate(
                [jnp.sum(dc * h2, axis=0, keepdims=True), jnp.sum(dc * h1, axis=0, keepdims=True),
                 jnp.sum(dc * h0, axis=0, keepdims=True)], axis=0)
            ext = jnp.concatenate([dc, carry[half]], axis=0)
            n1 = pltpu.roll(ext, n_ext - 1, 0)[:tm]
            n2 = pltpu.roll(ext, n_ext - 2, 0)[:tm]
            dh_ref[half] = (w[2:3] * dc + w[1:2] * n1 + w[0:1] * n2).astype(BF16)
            carry[half] = dc[:CONV_HALO]

    rev = lambda ip: n_i - 1 - ip
    return _call(
        body, name=name, grid=(n_j, n_i),
        in_specs=[pl.BlockSpec((tm, D), lambda j, ip: (rev(ip), 0)),
                  pl.BlockSpec((None, tn, D), lambda j, ip: (l, j, 0)),
                  pl.BlockSpec((2, tm, tn), lambda j, ip: (0, rev(ip), j)),
                  pl.BlockSpec((2, hb, tn), lambda j, ip: (0, jnp.maximum(rev(ip) * (tm // hb) - 1, 0), j)),
                  pl.BlockSpec((None, 3, tn), lambda j, ip: (l, 0, j)),
                  pl.BlockSpec((None, 3, tn), lambda j, ip: (l, 0, n_j + j)),
                  pl.BlockSpec((None, 1, tn), lambda j, ip: (l, 0, j)),
                  pl.BlockSpec((None, 1, tn), lambda j, ip: (l, 0, n_j + j))],
        out_specs=[pl.BlockSpec((2, tm, tn), lambda j, ip: (0, rev(ip), j)),
                   pl.BlockSpec((2, 3, tn), lambda j, ip: (0, 0, j)),
                   pl.BlockSpec((2, 1, tn), lambda j, ip: (0, 0, j))],
        out_shape=[jax.ShapeDtypeStruct((2, T, F), BF16), jax.ShapeDtypeStruct((2, 3, F), F32),
                   jax.ShapeDtypeStruct((2, 1, F), F32)],
        scratch=[pltpu.VMEM((2, CONV_HALO, tn), F32)])(db16, wdn, h, h, cw, cw, cb, cb)


def _pool_fwd(ext, xb_g, t_glob, win):
    e = ext
    sft = 1
    while sft < win:
        e = e + pltpu.roll(e, sft, 0)
        sft *= 2
    cnt = jnp.minimum(t_glob + 1.0, float(win))
    return e[POOL_HALO:] / cnt - xb_g


def gating_fwd(h0, lg, lb, ws, bsT, wp, sc, *, name):
    T = h0.shape[0]
    DA = lg.shape[-1]
    DB = sc.shape[-1]
    HA = DA // A_HEAD
    G = len(B_WINDOWS)
    CG = DB // G
    tm = _tile(T, 512, A_CHUNK)
    n_c = tm // A_CHUNK

    def body(h_ref, halo_ref, lg_ref, lb_ref, ws_ref, bsT_ref, wp_ref, sc_ref, cat_ref):
        i = pl.program_id(0)
        hu = h_ref[:, 0:DA].astype(F32)
        hv = h_ref[:, DA:2 * DA].astype(F32)
        xb = h_ref[:, 2 * DA:].astype(F32)
        u = _gelu(hu)
        vn, _, _ = _ln_fwd(_gelu(hv), lg_ref[...], lb_ref[...])
        vnb = vn.astype(BF16)
        rr = lax.broadcasted_iota(jnp.int32, (A_CHUNK, A_CHUNK), 0)
        cc = lax.broadcasted_iota(jnp.int32, (A_CHUNK, A_CHUNK), 1)
        for hh in range(HA):
            wt = jnp.where(rr >= cc, ws_ref[hh], 0.0).astype(BF16)
            cs = slice(hh * A_HEAD, (hh + 1) * A_HEAD)
            for n in range(n_c):
                rs = slice(n * A_CHUNK, (n + 1) * A_CHUNK)
                s = _dot(wt, vnb[rs, cs]) + bsT_ref[:, hh:hh + 1]
                cat_ref[rs, cs] = (u[rs, cs] * s).astype(BF16)
        halo = jnp.where(i > 0, halo_ref[...].astype(F32), 0.0)
        ext = jnp.concatenate([halo, xb], axis=0)
        t_glob = (i * tm + lax.broadcasted_iota(jnp.int32, (tm, 1), 0)).astype(F32)
        for g, win in enumerate(B_WINDOWS):
            gs = slice(g * CG, (g + 1) * CG)
            p = _pool_fwd(ext[:, gs], xb[:, gs], t_glob, win)
            z = _dot(p.astype(BF16), wp_ref[g])
            cat_ref[:, DA + g * CG:DA + (g + 1) * CG] = (z * sc_ref[:, gs]).astype(BF16)

    full = lambda a: pl.BlockSpec(a.shape, lambda i: (0,) * a.ndim)
    hpb = tm // POOL_HALO
    return _call(
        body, name=name, grid=(T // tm,),
        in_specs=[pl.BlockSpec((tm, 2 * DA + DB), lambda i: (i, 0)),
                  pl.BlockSpec((POOL_HALO, DB), lambda i: (jnp.maximum(i * hpb - 1, 0), 2 * DA // DB)),
                  full(lg), full(lb), full(ws), full(bsT), full(wp), full(sc)],
        out_specs=pl.BlockSpec((tm, DA + DB), lambda i: (i, 0)),
        out_shape=jax.ShapeDtypeStruct((T, DA + DB), BF16))(h0, h0, lg, lb, ws, bsT, wp, sc)


def gating_bwd(h0, dcat, lg, lb, ws, wsT, bsT, wp, sc, *, name):
    T = h0.shape[0]
    DA = lg.shape[-1]
    DB = sc.shape[-1]
    HA = DA // A_HEAD
    G = len(B_WINDOWS)
    CG = DB // G
    tm = _tile(T, 512, A_CHUNK)
    n_i = T // tm
    n_c = tm // A_CHUNK
    n_ext = tm + POOL_HALO

    def body(h_ref, halo_ref, dc_ref, dhalo_ref, lg_ref, lb_ref, ws_ref, wsT_ref, bsT_ref, wp_ref, sc_ref,
             dh_ref, dws_ref, dbsT_ref, dlg_ref, dlb_ref, dsc_ref, dwp_ref, dvn_sc):
        i = pl.program_id(0)

        @pl.when(i == 0)
        def _():
            for r in (dws_ref, dbsT_ref, dlg_ref, dlb_ref, dsc_ref, dwp_ref):
                r[...] = jnp.zeros_like(r)

        hu = h_ref[:, 0:DA].astype(F32)
        hv = h_ref[:, DA:2 * DA].astype(F32)
        xb = h_ref[:, 2 * DA:].astype(F32)
        u = _gelu(hu)
        gu = _gelu_grad(hu)
        lgv = lg_ref[...]
        vn, vhat, rstd = _ln_fwd(_gelu(hv), lgv, lb_ref[...])
        vnb = vn.astype(BF16)
        rr = lax.broadcasted_iota(jnp.int32, (A_CHUNK, A_CHUNK), 0)
        cc = lax.broadcasted_iota(jnp.int32, (A_CHUNK, A_CHUNK), 1)
        for hh in range(HA):
            wt = jnp.where(rr >= cc, ws_ref[hh], 0.0).astype(BF16)
            wtT = jnp.where(rr <= cc, wsT_ref[hh], 0.0).astype(BF16)
            cs = slice(hh * A_HEAD, (hh + 1) * A_HEAD)
            dws = jnp.zeros((A_CHUNK, A_CHUNK), F32)
            dbs = jnp.zeros((A_CHUNK, 1), F32)
            for n in range(n_c):
                rs = slice(n * A_CHUNK, (n + 1) * A_CHUNK)
                vb = vnb[rs, cs]
                s = _dot(wt, vb) + bsT_ref[:, hh:hh + 1]
                dya = dc_ref[rs, cs].astype(F32)
                ds = dya * u[rs, cs]
                dh_ref[rs, cs] = (dya * s * gu[rs, cs]).astype(BF16)
                dsb = ds.astype(BF16)
                dbs = dbs + jnp.sum(ds, axis=1, keepdims=True)
                dws = dws + _dot_nt(dsb, vb)
                dvn_sc[rs, cs] = _dot(wtT, dsb)
            dws_ref[hh] += jnp.where(rr >= cc, dws, 0.0)
            dbsT_ref[:, hh:hh + 1] += dbs
        dvg, dlg, dlb = _ln_bwd(dvn_sc[...], vhat, rstd, lgv)
        dlg_ref[...] += dlg
        dlb_ref[...] += dlb
        dh_ref[:, DA:2 * DA] = (dvg * _gelu_grad(hv)).astype(BF16)

        halo = jnp.where(i > 0, halo_ref[...].astype(F32), 0.0)
        ext = jnp.concatenate([halo, xb], axis=0)
        t_glob = (i * tm + lax.broadcasted_iota(jnp.int32, (tm, 1), 0)).astype(F32)
        t_ext = (i * tm + lax.broadcasted_iota(jnp.int32, (n_ext, 1), 0)).astype(F32)
        dyb = dc_ref[:, DA:].astype(F32)
        dhalo = jnp.where(i < n_i - 1, dhalo_ref[...].astype(F32), 0.0)
        dyb_ext = jnp.concatenate([dyb, dhalo], axis=0)
        for g, win in enumerate(B_WINDOWS):
            gs = slice(g * CG, (g + 1) * CG)
            pb = _pool_fwd(ext[:, gs], xb[:, gs], t_glob, win).astype(BF16)
            wpg = wp_ref[g]
            z = _dot(pb, wpg)
            dsc_ref[:, gs] += jnp.sum(dyb[:, gs] * z, axis=0, keepdims=True)
            dzb = (dyb_ext[:, gs] * sc_ref[:, gs]).astype(BF16)
            dwp_ref[g] += _dot_tn(pb, dzb[:tm])
            dp = _dot_nt(dzb, wpg)
            e = dp / jnp.minimum(t_ext + 1.0, float(win))
            sft = 1
            while sft < win:
                e = e + pltpu.roll(e, n_ext - sft, 0)
                sft *= 2
            dh_ref[:, 2 * DA + g * CG:2 * DA + (g + 1) * CG] = (e[:tm] - dp[:tm]).astype(BF16)

    full = lambda a: pl.BlockSpec(a.shape, lambda i: (0,) * a.ndim)
    hpb = tm // POOL_HALO
    n_hb = T // POOL_HALO
    outs = [jax.ShapeDtypeStruct((T, 2 * DA + DB), BF16), jax.ShapeDtypeStruct(ws.shape, F32),
            jax.ShapeDtypeStruct(bsT.shape, F32), jax.ShapeDtypeStruct(lg.shape, F32),
            jax.ShapeDtypeStruct(lb.shape, F32), jax.ShapeDtypeStruct(sc.shape, F32),
            jax.ShapeDtypeStruct(wp.shape, F32)]
    return _call(
        body, name=name, grid=(n_i,),
        in_specs=[pl.BlockSpec((tm, 2 * DA + DB), lambda i: (i, 0)),
                  pl.BlockSpec((POOL_HALO, DB), lambda i: (jnp.maximum(i * hpb - 1, 0), 2 * DA // DB)),
                  pl.BlockSpec((tm, DA + DB), lambda i: (i, 0)),
                  pl.BlockSpec((POOL_HALO, DB), lambda i: (jnp.minimum((i + 1) * hpb, n_hb - 1), DA // DB)),
                  full(lg), full(lb), full(ws), full(wsT), full(bsT), full(wp), full(sc)],
        out_specs=[pl.BlockSpec((tm, 2 * DA + DB), lambda i: (i, 0))] + [full(o) for o in outs[1:]],
        out_shape=outs,
        scratch=[pltpu.VMEM((tm, DA), F32)])(h0, h0, dcat, dcat, lg, lb, ws, wsT, bsT, wp, sc)


HP = 2 * C_HEAD


def _hgrn_gates(blk, lbv, tri):
    q = blk[:, 0:HP]
    sg = _sigmoid(blk[:, HP:2 * HP])
    f = lbv + (1.0 - lbv) * sg
    bcum = _exact_tri_dot(tri, jnp.log(f))
    blast = bcum[C_CHUNK - 1:C_CHUNK]
    sq = _sigmoid(q)
    k = 1.0 - f
    e_in = jnp.exp(bcum)
    e_out = jnp.exp(-bcum)
    e_end = jnp.exp(blast - bcum)
    return dict(q=q, sq=sq, sg=sg, f=f, k=k, bcum=bcum, blast=blast, e_in=e_in, e_out=e_out, e_end=e_end,
                qd=q * sq * e_in, kd=k * e_out, ke=k * e_end, dec=jnp.exp(blast), v=blk[:, 2 * HP:3 * HP],
                gg=blk[:, 3 * HP:4 * HP])


def _lower_bound(lbp_ref):
    p0, p1 = lbp_ref[0:1], lbp_ref[1:2]
    mx = jnp.maximum(p0, p1)
    e0, e1 = jnp.exp(p0 - mx), jnp.exp(p1 - mx)
    return e1 / (e0 + e1)


def hgrn_fwd(h1p, lbp, gn, *, name):
    T = h1p.shape[0]
    DC = gn.shape[-1]
    n_p = DC // HP
    tt = _tile(T, 512, C_CHUNK)
    n_c = tt // C_CHUNK

    def body(h_ref, lbp_ref, gn_ref, y_ref, o_ref, sp_ref, st):
        i = pl.program_id(1)

        @pl.when(i == 0)
        def _():
            st[...] = jnp.zeros_like(st)

        lbv = _lower_bound(lbp_ref)
        gnv = gn_ref[...]
        rr = lax.broadcasted_iota(jnp.int32, (C_CHUNK, C_CHUNK), 0)
        cc = lax.broadcasted_iota(jnp.int32, (C_CHUNK, C_CHUNK), 1)
        causal = rr >= cc
        tri = jnp.where(causal, 1.0, 0.0).astype(BF16)

        def chunk(n, _):
            rows = pl.ds(pl.multiple_of(n * C_CHUNK, C_CHUNK), C_CHUNK)
            a = _hgrn_gates(h_ref[rows, :].astype(F32), lbv, tri)
            outs = []
            for hd in range(2):
                cs = slice(hd * C_HEAD, (hd + 1) * C_HEAD)
                qd, kd, ke = a["qd"][:, cs].astype(BF16), a["kd"][:, cs].astype(BF16), a["ke"][:, cs].astype(BF16)
                vb = a["v"][:, cs].astype(BF16)
                s_t = st[hd]
                att = jnp.where(causal, _dot_nt(qd, kd), 0.0).astype(BF16)
                outs.append(_dot(att, vb) + _dot_nt(qd, s_t.astype(BF16)))
                sp_ref[hd, n] = s_t
                st[hd] = a["dec"][:, cs] * s_t + _dot_tn(vb, ke)
            o = jnp.concatenate(outs, axis=1)
            o_ref[rows, :] = o.astype(BF16)
            ys = []
            for hd in range(2):
                oh = o[:, hd * C_HEAD:(hd + 1) * C_HEAD]
                ys.append(oh * lax.rsqrt(jnp.mean(oh * oh, axis=-1, keepdims=True) + LN_EPS))
            y_ref[rows, :] = (jnp.concatenate(ys, axis=1) * gnv * _sigmoid(a["gg"])).astype(BF16)
            return 0

        lax.fori_loop(0, n_c, chunk, 0)

    return _call(
        body, name=name, grid=(n_p, T // tt),
        in_specs=[pl.BlockSpec((tt, 4 * HP), lambda p, i: (i, p)), pl.BlockSpec((2, HP), lambda p, i: (0, p)),
                  pl.BlockSpec((1, HP), lambda p, i: (0, p))],
        out_specs=[pl.BlockSpec((tt, HP), lambda p, i: (i, p)), pl.BlockSpec((tt, HP), lambda p, i: (i, p)),
                   pl.BlockSpec((2, n_c, C_HEAD, C_HEAD), lambda p, i: (p, i, 0, 0))],
        out_shape=[jax.ShapeDtypeStruct((T, DC), BF16), jax.ShapeDtypeStruct((T, DC), BF16),
                   jax.ShapeDtypeStruct((2 * n_p, T // C_CHUNK, C_HEAD, C_HEAD), F32)],
        scratch=[pltpu.VMEM((2, C_HEAD, C_HEAD), F32)])(h1p, lbp, gn)


def hgrn_bwd(h1p, o_saved, dy, sp, lbp, gn, *, name):
    T = h1p.shape[0]
    DC = gn.shape[-1]
    n_p = DC // HP
    tt = _tile(T, 512, C_CHUNK)
    n_i = T // tt
    n_c = tt // C_CHUNK

    def body(h_ref, o_ref, dy_ref, sp_ref, lbp_ref, gn_ref, dh_ref, dgn_ref, dlbp_ref, dst, dlb_acc):
        ip = pl.program_id(1)

        @pl.when(ip == 0)
        def _():
            dst[...] = jnp.zeros_like(dst)
            dlb_acc[...] = jnp.zeros_like(dlb_acc)
            dgn_ref[...] = jnp.zeros_like(dgn_ref)

        lbv = _lower_bound(lbp_ref)
        gnv = gn_ref[...]
        rr = lax.broadcasted_iota(jnp.int32, (C_CHUNK, C_CHUNK), 0)
        cc = lax.broadcasted_iota(jnp.int32, (C_CHUNK, C_CHUNK), 1)
        causal = rr >= cc
        tri = jnp.where(causal, 1.0, 0.0).astype(BF16)
        tri_t = jnp.where(rr <= cc, 1.0, 0.0).astype(BF16)
        last_row = lax.broadcasted_iota(jnp.int32, (C_CHUNK, 1), 0) == C_CHUNK - 1

        def chunk(m, _):
            n = n_c - 1 - m
            rows = pl.ds(pl.multiple_of(n * C_CHUNK, C_CHUNK), C_CHUNK)
            a = _hgrn_gates(h_ref[rows, :].astype(F32), lbv, tri)
            o = o_ref[rows, :].astype(F32)
            dyv = dy_ref[rows, :].astype(F32)
            sgg = _sigmoid(a["gg"])
            ohs, rrs = [], []
            for hd in range(2):
                oh = o[:, hd * C_HEAD:(hd + 1) * C_HEAD]
                r = lax.rsqrt(jnp.mean(oh * oh, axis=-1, keepdims=True) + LN_EPS)
                ohs.append(oh * r)
                rrs.append(r)
            ohat = jnp.concatenate(ohs, axis=1)
            dyn = dyv * sgg
            dgg = dyv * ohat * gnv * sgg * (1.0 - sgg)
            dgn_ref[...] += jnp.sum(dyn * ohat, axis=0, keepdims=True)
            dxh = dyn * gnv
            dqd_l, dkd_l, dke_l, dv_l, ddec_l = [], [], [], [], []
            for hd in range(2):
                cs = slice(hd * C_HEAD, (hd + 1) * C_HEAD)
                dxh_h, oh_h = dxh[:, cs], ohat[:, cs]
                do = rrs[hd] * (dxh_h - oh_h * jnp.mean(dxh_h * oh_h, axis=-1, keepdims=True))
                dob = do.astype(BF16)
                qd, kd, ke = a["qd"][:, cs].astype(BF16), a["kd"][:, cs].astype(BF16), a["ke"][:, cs].astype(BF16)
                vb = a["v"][:, cs].astype(BF16)
                s_t = sp_ref[hd, n]
                ds_out = dst[hd]
                ds_outb = ds_out.astype(BF16)
                att = jnp.where(causal, _dot_nt(qd, kd), 0.0).astype(BF16)
                datt = jnp.where(causal, _dot_nt(dob, vb), 0.0).astype(BF16)
                dv_l.append(_dot_tn(att, dob) + _dot_nt(ke, ds_outb))
                dqd_l.append(_dot(datt, kd) + _dot(dob, s_t.astype(BF16)))
                dkd_l.append(_dot_tn(datt, qd))
                dke_l.append(_dot(vb, ds_outb))
                ddec_l.append(jnp.sum(ds_out * s_t, axis=0, keepdims=True))
                dst[hd] = a["dec"][:, cs] * ds_out + _dot_tn(dob, qd)
            dqd = jnp.concatenate(dqd_l, axis=1)
            dkd = jnp.concatenate(dkd_l, axis=1)
            dke = jnp.concatenate(dke_l, axis=1)
            dv = jnp.concatenate(dv_l, axis=1)
            ddec = jnp.concatenate(ddec_l, axis=1)
            dqs = dqd * a["e_in"]
            kek = dke * a["ke"]
            dbcum = dqd * a["qd"] - dkd * a["kd"] - kek
            dk = dkd * a["e_out"] + dke * a["e_end"]
            dblast = jnp.sum(kek, axis=0, keepdims=True) + ddec * a["dec"]
            dbcum = dbcum + jnp.where(last_row, dblast, 0.0)
            dlf = _exact_tri_dot(tri_t, dbcum)
            df = dlf / a["f"] - dk
            dlb_acc[...] += jnp.sum(df * (1.0 - a["sg"]), axis=0, keepdims=True)
            dfl = df * (1.0 - lbv) * a["sg"] * (1.0 - a["sg"])
            dq = dqs * a["sq"] * (1.0 + a["q"] * (1.0 - a["sq"]))
            dh_ref[rows, :] = jnp.concatenate([dq, dfl, dv, dgg], axis=1).astype(BF16)
            return 0

        lax.fori_loop(0, n_c, chunk, 0)

        @pl.when(ip == n_i - 1)
        def _():
            d1 = dlb_acc[...] * lbv * (1.0 - lbv)
            dlbp_ref[...] = jnp.concatenate([-d1, d1], axis=0)

    rev = lambda ip: n_i - 1 - ip
    return _call(
        body, name=name, grid=(n_p, n_i),
        in_specs=[pl.BlockSpec((tt, 4 * HP), lambda p, ip: (rev(ip), p)),
                  pl.BlockSpec((tt, HP), lambda p, ip: (rev(ip), p)),
                  pl.BlockSpec((tt, HP), lambda p, ip: (rev(ip), p)),
                  pl.BlockSpec((2, n_c, C_HEAD, C_HEAD), lambda p, ip: (p, rev(ip), 0, 0)),
                  pl.BlockSpec((2, HP), lambda p, ip: (0, p)), pl.BlockSpec((1, HP), lambda p, ip: (0, p))],
        out_specs=[pl.BlockSpec((tt, 4 * HP), lambda p, ip: (rev(ip), p)),
                   pl.BlockSpec((1, HP), lambda p, ip: (0, p)), pl.BlockSpec((2, HP), lambda p, ip: (0, p))],
        out_shape=[jax.ShapeDtypeStruct(h1p.shape, BF16), jax.ShapeDtypeStruct((1, DC), F32),
                   jax.ShapeDtypeStruct((2, DC), F32)],
        scratch=[pltpu.VMEM((2, C_HEAD, C_HEAD), F32), pltpu.VMEM((1, HP), F32)])(h1p, o_saved, dy, sp, lbp, gn)


def loss_bwd(xh, rstd, g, b, target, *, name):
    T, N = xh.shape
    tm = _tile(T, 512, 8)

    def body(xh_ref, rs_ref, g_ref, b_ref, t_ref, ls_ref, dr_ref, drb_ref, dg_ref, db_ref):
        i = pl.program_id(0)

        @pl.when(i == 0)
        def _():
            for r in (ls_ref, dg_ref, db_ref):
                r[...] = jnp.zeros_like(r)

        xhv, gv = xh_ref[...], g_ref[...]
        e = xhv * gv + b_ref[...] - t_ref[...]
        ls_ref[...] += 0.5 * jnp.sum(jnp.mean(e * e, axis=-1, keepdims=True), axis=0, keepdims=True)
        dr, dg, db = _ln_bwd(e / N, xhv, rs_ref[...], gv)
        dr_ref[...] = dr
        drb_ref[...] = dr.astype(BF16)
        dg_ref[...] += dg
        db_ref[...] += db

    row = pl.BlockSpec((tm, N), lambda i: (i, 0))
    vec = pl.BlockSpec((1, N), lambda i: (0, 0))
    return _call(body, name=name, grid=(T // tm,),
                 in_specs=[row, pl.BlockSpec((tm, 1), lambda i: (i, 0)), vec, vec, row],
                 out_specs=[pl.BlockSpec((1, LANES), lambda i: (0, 0)), row, row, vec, vec],
                 out_shape=[jax.ShapeDtypeStruct((1, LANES), F32), jax.ShapeDtypeStruct((T, N), F32),
                            jax.ShapeDtypeStruct((T, N), BF16), jax.ShapeDtypeStruct((1, N), F32),
                            jax.ShapeDtypeStruct((1, N), F32)])(xh, rstd, g, b, target)


def _mesh_pos():
    x, y, c = lax.axis_index("x"), lax.axis_index("y"), lax.axis_index("c")
    chips = [(1 - x, y), (x, 1 - y), (1 - x, 1 - y)]
    return x, y, c, chips


def _remote(src, dst, send, recv, j, dev):
    return pltpu.make_async_remote_copy(src_ref=src, dst_ref=dst, send_sem=send.at[j], recv_sem=recv.at[j],
                                        device_id=dev, device_id_type=MESH)


def all_gather_weights(big, small, *, name):
    nb, ns = len(big), len(small)
    layers = [(t, l) for t in range(nb) for l in range(big[t].shape[0])]
    n_first = 3 * len(layers)
    n_small = 3 * ns
    n_rem = 2 * n_first + n_small
    n_loc = len(layers) + ns

    def body(*refs):
        ins, outs = refs[:nb + ns], refs[nb + ns:2 * (nb + ns)]
        send, recv, loc = refs[2 * (nb + ns):]
        x, y, c, chips = _mesh_pos()
        me = 2 * x + y
        sib = (x, y, 1 - c)
        ids = [2 * cx + cy for cx, cy in chips]
        started = []
        for q, (t, l) in enumerate(layers):
            cp = pltpu.make_async_copy(ins[t].at[l], outs[t].at[l, me], loc.at[q])
            cp.start()
            started.append(cp)
        for t in range(ns):
            cp = pltpu.make_async_copy(ins[nb + t], outs[nb + t].at[me], loc.at[len(layers) + t])
            cp.start()
            started.append(cp)
        sends = []
        for q, (t, l) in enumerate(layers):
            for k, chip in enumerate(chips):
                cp = _remote(ins[t].at[l, c], outs[t].at[l, me, c], send, recv, 3 * q + k, (*chip, c))
                cp.start()
                sends.append(cp)
        for t in range(ns):
            for k, chip in enumerate(chips):
                cp = _remote(ins[nb + t], outs[nb + t].at[me], send, recv, 2 * n_first + 3 * t + k, (*chip, c))
                cp.start()
                sends.append(cp)
        for q, (t, l) in enumerate(layers):
            for k in range(3):
                blk = outs[t].at[l, ids[k], c]
                _remote(blk, blk, send, recv, 3 * q + k, sib).wait_recv()
                cp = _remote(blk, blk, send, recv, n_first + 3 * q + k, sib)
                cp.start()
                sends.append(cp)
        for t in range(ns):
            for k in range(3):
                blk = outs[nb + t].at[ids[k]]
                _remote(blk, blk, send, recv, 2 * n_first + 3 * t + k, sib).wait_recv()
        for q, (t, l) in enumerate(layers):
            for k in range(3):
                blk = outs[t].at[l, ids[k], 1 - c]
                _remote(blk, blk, send, recv, n_first + 3 * q + k, sib).wait_recv()
        for cp in sends:
            cp.wait_send()
        for cp in started:
            cp.wait()

    out_shape = [jax.ShapeDtypeStruct((a.shape[0], 4) + a.shape[1:], a.dtype) for a in big]
    out_shape += [jax.ShapeDtypeStruct((4,) + a.shape, a.dtype) for a in small]
    return _call(body, name=name, in_specs=[ANY] * (nb + ns), out_specs=[ANY] * (nb + ns), out_shape=out_shape,
                 scratch=[pltpu.SemaphoreType.DMA((n_rem,)), pltpu.SemaphoreType.DMA((n_rem,)),
                          pltpu.SemaphoreType.DMA((n_loc,))])(*big, *small)


def rs_exchange_halves(grads, *, name):
    n = len(grads)

    def body(*refs):
        ins, mine, theirs = refs[:n], refs[n:2 * n], refs[2 * n:3 * n]
        send, recv, loc = refs[3 * n:]
        x, y, c, chips = _mesh_pos()
        sib = (x, y, 1 - c)
        ids = [2 * cx + cy for cx, cy in chips] + [2 * x + y]
        cps = []
        for t in range(n):
            for k in range(4):
                lc = pltpu.make_async_copy(ins[t].at[ids[k], c], mine[t].at[k], loc.at[4 * t + k])
                lc.start()
                rc = _remote(ins[t].at[ids[k], 1 - c], theirs[t].at[k], send, recv, 4 * t + k, sib)
                rc.start()
                cps.append((lc, rc))
        for lc, rc in cps:
            rc.wait()
            lc.wait()

    out_shape = [jax.ShapeDtypeStruct((4,) + g.shape[2:], g.dtype) for g in grads] * 2
    return _call(body, name=name, in_specs=[ANY] * n, out_specs=[ANY] * (2 * n), out_shape=out_shape,
                 scratch=[pltpu.SemaphoreType.DMA((4 * n,)), pltpu.SemaphoreType.DMA((4 * n,)),
                          pltpu.SemaphoreType.DMA((4 * n,))])(*grads)


def rs_exchange_chips(parts, *, name):
    n = len(parts)

    def body(*refs):
        ins, outs = refs[:n], refs[n:2 * n]
        send, recv = refs[2 * n:]
        x, y, c, chips = _mesh_pos()
        cps = []
        for t in range(n):
            for k, chip in enumerate(chips):
                cp = _remote(ins[t].at[k], outs[t].at[k], send, recv, 3 * t + k, (*chip, c))
                cp.start()
                cps.append(cp)
        for cp in cps:
            cp.wait()

    return _call(body, name=name, in_specs=[ANY] * n, out_specs=[ANY] * n,
                 out_shape=[jax.ShapeDtypeStruct(p.shape, p.dtype) for p in parts],
                 scratch=[pltpu.SemaphoreType.DMA((3 * n,)), pltpu.SemaphoreType.DMA((3 * n,))])(*parts)


def rs_share_halves(halves, groups, *, name):
    n = len(halves)

    def body(*refs):
        ins, outs = refs[:n], refs[n:n + len(groups)]
        send, recv, loc = refs[n + len(groups):]
        x, y, c, _ = _mesh_pos()
        sib = (x, y, 1 - c)
        cps = []
        for w, members in enumerate(groups):
            for l, t in enumerate(members):
                lc = pltpu.make_async_copy(ins[t], outs[w].at[l, c], loc.at[t])
                lc.start()
                rc = _remote(ins[t], outs[w].at[l, c], send, recv, t, sib)
                rc.start()
                cps.append((lc, rc, outs[w].at[l, 1 - c], t))
        for lc, rc, other, t in cps:
            rc.wait_send()
            _remote(other, other, send, recv, t, sib).wait_recv()
            lc.wait()

    out_shape = [jax.ShapeDtypeStruct((len(m), 2) + halves[m[0]].shape, F32) for m in groups]
    return _call(body, name=name, in_specs=[ANY] * n, out_specs=[ANY] * len(groups), out_shape=out_shape,
                 scratch=[pltpu.SemaphoreType.DMA((n,)), pltpu.SemaphoreType.DMA((n,)),
                          pltpu.SemaphoreType.DMA((n,))])(*halves)


def add_halves(mine, theirs, *, name):
    _, Rh, C = mine.shape
    br = _tile(Rh, max(8, (1 << 18) // C), 16)

    def body(a_ref, b_ref, p_ref, own_ref):
        s = a_ref[...].astype(F32) + b_ref[...].astype(F32)
        p_ref[...] = s[0:3].astype(BF16)
        own_ref[...] = s[3]

    spec = pl.BlockSpec((4, br, C), lambda r: (0, r, 0))
    return _call(body, name=name, grid=(Rh // br,), in_specs=[spec, spec],
                 out_specs=[pl.BlockSpec((3, br, C), lambda r: (0, r, 0)), pl.BlockSpec((br, C), lambda r: (r, 0))],
                 out_shape=[jax.ShapeDtypeStruct((3, Rh, C), BF16), jax.ShapeDtypeStruct((Rh, C), F32)])(mine, theirs)


def add_chips(own, got, *, name):
    Rh, C = own.shape
    br = _tile(Rh, max(8, (1 << 18) // C), 16)

    def body(o_ref, g_ref, f_ref):
        s = o_ref[...]
        for k in range(3):
            s = s + g_ref[k].astype(F32)
        f_ref[...] = s

    return _call(body, name=name, grid=(Rh // br,),
                 in_specs=[pl.BlockSpec((br, C), lambda r: (r, 0)), pl.BlockSpec((3, br, C), lambda r: (0, r, 0))],
                 out_specs=pl.BlockSpec((br, C), lambda r: (r, 0)),
                 out_shape=jax.ShapeDtypeStruct((Rh, C), F32))(own, got)


def all_reduce_small(buf, *, name):
    rows = buf.shape[0]

    def body(x_ref, o_ref, rbuf, send, recv):
        x, y, c, _ = _mesh_pos()
        o_ref[...] = x_ref[...]
        for k, dev in enumerate(((x, y, 1 - c), (1 - x, y, c), (x, 1 - y, c))):
            cp = _remote(o_ref, rbuf.at[k], send, recv, k, dev)
            cp.start()
            cp.wait()
            o_ref[...] = o_ref[...] + rbuf[k]

    return _call(body, name=name, in_specs=[VMEM_SPEC], out_specs=VMEM_SPEC,
                 out_shape=jax.ShapeDtypeStruct(buf.shape, F32),
                 scratch=[pltpu.VMEM((3, rows, LANES), F32), pltpu.SemaphoreType.DMA((3,)),
                          pltpu.SemaphoreType.DMA((3,))])(buf)


def _pack(arrs):
    parts = []
    for a in arrs:
        f = a.reshape(-1).astype(F32)
        parts.append(jnp.pad(f, (0, (-f.shape[0]) % PACK_ALIGN)))
    return jnp.concatenate(parts).reshape(-1, LANES)


def _unpack(buf, shapes):
    flat = buf.reshape(-1)
    out, off = [], 0
    for s in shapes:
        n = math.prod(s)
        out.append(flat[off:off + n].reshape(s))
        off += n + (-n) % PACK_ALIGN
    return out


_WEIGHTS = ['ev_w_in', 'ev_ln_v_g', 'ev_ln_v_b', 'ev_w_s', 'ev_b_s', 'ev_w_pool', 'ev_pool_scale', 'ev_w_out',
            'od_w_in', 'od_norm_g', 'od_w_out', 'lb_param', 'ffn_w_up', 'ffn_conv_w', 'ffn_conv_b', 'ffn_w_down',
            'ln1_g', 'ln1_b', 'ln2_g', 'ln2_b']
_BIG = ['ev_w_in', 'ev_w_out', 'od_w_in', 'od_w_out', 'ffn_w_up', 'ffn_w_down']
_SMALL = [n for n in _WEIGHTS if n not in _BIG]


def kernel(x, ev_w_in, ev_ln_v_g, ev_ln_v_b, ev_w_s, ev_b_s, ev_w_pool, ev_pool_scale, ev_w_out, od_w_in, od_norm_g, od_w_out, lb_param, ffn_w_up, ffn_conv_w, ffn_conv_b, ffn_w_down, ln1_g, ln1_b, ln2_g, ln2_b, loss_target, m_ev_w_in, m_ev_ln_v_g, m_ev_ln_v_b, m_ev_w_s, m_ev_b_s, m_ev_w_pool, m_ev_pool_scale, m_ev_w_out, m_od_w_in, m_od_norm_g, m_od_w_out, m_lb_param, m_ffn_w_up, m_ffn_conv_w, m_ffn_conv_b, m_ffn_w_down, m_ln1_g, m_ln1_b, m_ln2_g, m_ln2_b, v_ev_w_in, v_ev_ln_v_g, v_ev_ln_v_b, v_ev_w_s, v_ev_b_s, v_ev_w_pool, v_ev_pool_scale, v_ev_w_out, v_od_w_in, v_od_norm_g, v_od_w_out, v_lb_param, v_ffn_w_up, v_ffn_conv_w, v_ffn_conv_b, v_ffn_w_down, v_ln1_g, v_ln1_b, v_ln2_g, v_ln2_b):
    given = dict(locals())
    w = {n: given[n] for n in _WEIGHTS}
    mom = {n: given["m_" + n] for n in _WEIGHTS}
    vel = {n: given["v_" + n] for n in _WEIGHTS}
    x2d = x[0]
    tgt = loss_target[0]
    T, D = x2d.shape
    DA = ev_ln_v_g.shape[-1]
    DB = ev_pool_scale.shape[-1]
    HA = ev_w_s.shape[1]
    G = len(B_WINDOWS)
    CG = DB // G
    DC = 4 * od_norm_g.shape[-1]
    F = ffn_conv_b.shape[-1] // 2
    chip = 2 * lax.axis_index("x") + lax.axis_index("y")

    big_sh = {n: cast_bf16(w[n], "cast_" + n) for n in _BIG}
    halves = lambda a: a.reshape(a.shape[0], 2, a.shape[1] // 2, a.shape[2])
    gathered = all_gather_weights([halves(big_sh[n]) for n in _BIG], [ev_w_pool[0], ffn_conv_w, od_norm_g],
                                  name="all_gather_weights")
    wg = {n: g.reshape(g.shape[0], 4, g.shape[3] * 2, g.shape[4]) for n, g in zip(_BIG, gathered[:6])}
    wpool_full = gathered[6].transpose(1, 0, 2, 3).reshape(G, CG, CG)
    cw_full = gathered[7].transpose(1, 2, 0, 3).reshape(DEPTH, 3, 2 * F)
    gn_full = gathered[8].reshape(1, DC)
    win0 = wg['ev_w_in'][0]
    wout0 = wg['ev_w_out'][0].reshape(DA + DB, D)
    win1 = wg['od_w_in'][0]
    wout1 = wg['od_w_out'][0].reshape(DC, D)
    wup = wg['ffn_w_up']
    wdn = wg['ffn_w_down'].reshape(DEPTH, F, D)
    cb3 = ffn_conv_b.reshape(DEPTH, 1, 2 * F)
    ws = ev_w_s[0]
    wsT = jnp.swapaxes(ws, 1, 2)
    bsT = ev_b_s[0].T
    wpb = wpool_full.astype(BF16)
    ones = jnp.ones((1, D), F32)
    zeros = jnp.zeros((1, D), F32)
    row = lambda a, l: a[l:l + 1]

    Ns0 = win0.shape[-1]
    Nu = wup.shape[-1]
    tm_big = _tile(T, 1024, 8)
    tm_ln = _tile(T, 512, 8)
    tk_ln = _tile(D, 512)
    n_p = DC // HP

    def nat_spec(Ns, tnw):
        nps = Ns // tnw
        return pl.BlockSpec((1, D, tnw), lambda i, j: (j // nps, 0, j % nps))

    perm_spec = pl.BlockSpec((4, D, HP), lambda i, j: (0, 0, j))

    xb16 = cast_bf16(x, "cast_x")[0]
    h0 = mm_nn(xb16, win0, w_spec=nat_spec(Ns0, Ns0), P=1, tnw=Ns0, tm=tm_big, n_j=4, name="ev_in")
    cat = gating_fwd(h0, ev_ln_v_g, ev_ln_v_b, ws, bsT, wpb, ev_pool_scale, name="gating_fwd")

    def mix_ln(a, wmat, res, l, name):
        K = a.shape[1]
        tk = _tile(K, 512)
        return mm_ln(a, wmat, *res, row(ln1_g, l), row(ln1_b, l), w_spec=pl.BlockSpec((tk, D), lambda i, k: (k, 0)),
                     K=K, tk=tk, tm=tm_ln, name=name)

    def ffn_fwd(y_in, res, l):
        h, f = ffn_up(y_in, wup, cw_full, cb3, l, name=f"ffn_up{l}")
        tk = _tile(F, 512)
        out = mm_ln(f, wdn, *res, row(ln2_g, l), row(ln2_b, l),
                    w_spec=pl.BlockSpec((None, tk, D), lambda i, k: (l, k, 0)), K=F, tk=tk, tm=tm_ln,
                    name=f"ffn_down{l}")
        return h, f, out

    xh1, y1, rs1 = mix_ln(cat, wout0, (x2d, ones, zeros), 0, "ev_out")
    res1 = (xh1, row(ln1_g, 0), row(ln1_b, 0))
    hf0, f0, (xh2, y2, rs2) = ffn_fwd(y1, res1, 0)
    res2 = (xh2, row(ln2_g, 0), row(ln2_b, 0))
    h1p = mm_nn(y2, win1, w_spec=perm_spec, P=4, tnw=HP, tm=tm_big, n_j=n_p, name="od_in")
    yh, o_saved, sp = hgrn_fwd(h1p, lb_param, gn_full, name="hgrn_fwd")
    xh3, y3, rs3 = mix_ln(yh, wout1, res2, 1, "od_out")
    res3 = (xh3, row(ln1_g, 1), row(ln1_b, 1))
    hf1, f1, (xh4, y4, rs4) = ffn_fwd(y3, res3, 1)
    loss_p, dr, drb, dg_ln2_1, db_ln2_1 = loss_bwd(xh4, rs4, row(ln2_g, 1), row(ln2_b, 1), tgt, name="loss_bwd")

    tt = _tile(T, 512, 16)
    n_t = T // tt
    tnu = Nu // 2 if (Nu // 2) % LANES == 0 else Nu
    upb = Nu // tnu
    tkd = _tile(D, 1024)

    def ffn_bwd(l, dr2, dr2b, f, hf, y_in, xh_in, rs_in):
        tkf = F // 4
        g_dn = mm_tn(f, dr2b, a_spec=pl.BlockSpec((tt, tkf), lambda kb, nb, t: (t, kb)),
                     g_spec=pl.BlockSpec((tt, D), lambda kb, nb, t: (t, 0)),
                     o_spec=pl.BlockSpec((1, tkf, D), lambda kb, nb, t: (0, kb, 0)), out_shape=(1, F, D),
                     grid=(4, 1, n_t), acc_shape=(tkf, D), P=1, tnw=D, name=f"g_ffn_down{l}")
        dh, dcw, dcb = ffn_dgate(dr2b, wdn, hf, cw_full, cb3, l, name=f"ffn_dgate{l}")
        g_up = mm_tn(y_in, dh, a_spec=pl.BlockSpec((tt, tkd), lambda kb, nb, t: (t, kb)),
                     g_spec=pl.BlockSpec((None, tt, tnu), lambda kb, nb, t: (nb // (2 * upb), t, nb % (2 * upb))),
                     o_spec=pl.BlockSpec((1, tkd, tnu), lambda kb, nb, t: (nb // upb, kb, nb % upb)),
                     out_shape=(4, D, Nu), grid=(D // tkd, 4 * upb, n_t), acc_shape=(tkd, tnu), P=1, tnw=tnu,
                     name=f"g_ffn_up{l}")
        tku = _tile(Nu, 256)
        kps = Nu // tku
        out = mm_nt_res(dh, wup, dr2, (xh_in, rs_in, row(ln1_g, l)),
                        a_spec=pl.BlockSpec((None, tm_ln, tku), lambda i, k: (k // (2 * kps), i, k % (2 * kps))),
                        w_spec=pl.BlockSpec((None, 1, D, tku), lambda i, k: (l, k // kps, 0, k % kps)),
                        P=1, tnw=tku, n_k=4 * kps, tm=tm_ln, name=f"d_ffn_in{l}")
        return g_dn, g_up, dcw, dcb, out

    def g_out(a, gb, name):
        K = a.shape[1]
        tkk = _tile(K, 1024)
        return mm_tn(a, gb, a_spec=pl.BlockSpec((tt, tkk), lambda kb, nb, t: (t, kb)),
                     g_spec=pl.BlockSpec((tt, tkd), lambda kb, nb, t: (t, nb)),
                     o_spec=pl.BlockSpec((1, tkk, tkd), lambda kb, nb, t: (0, kb, nb)), out_shape=(1, K, D),
                     grid=(K // tkk, D // tkd, n_t), acc_shape=(tkk, tkd), P=1, tnw=tkd, name=name)

    g_dn1, g_up1, dcw1, dcb1, (dr1, dr1b, dg_ln1_1, db_ln1_1) = ffn_bwd(1, dr, drb, f1, hf1, y3, xh3, rs3)
    g_wout1 = g_out(yh, dr1b, "g_od_out")
    dyh = mm_nt_plain(dr1b, wout1, tm=tm_big, tn=_tile(DC, 512), name="d_od_out")
    dh1p, d_gn, d_lbp = hgrn_bwd(h1p, o_saved, dyh, sp, lb_param, gn_full, name="hgrn_bwd")
    g_win1 = mm_tn(y2, dh1p, a_spec=pl.BlockSpec((tt, tkd), lambda kb, nb, t: (t, kb)),
                   g_spec=pl.BlockSpec((tt, 4 * HP), lambda kb, nb, t: (t, nb)),
                   o_spec=pl.BlockSpec((4, tkd, HP), lambda kb, nb, t: (0, kb, nb)), out_shape=(4, D, DC),
                   grid=(D // tkd, n_p, n_t), acc_shape=(tkd, 4 * HP), P=4, tnw=HP, name="g_od_in")
    dr, drb, dg_ln2_0, db_ln2_0 = mm_nt_res(
        dh1p, win1, dr1, (xh2, rs2, row(ln2_g, 0)), a_spec=pl.BlockSpec((tm_ln, 4 * HP), lambda i, k: (i, k)),
        w_spec=pl.BlockSpec((4, D, HP), lambda i, k: (0, 0, k)), P=4, tnw=HP, n_k=n_p, tm=tm_ln, name="d_od_in")
    g_dn0, g_up0, dcw0, dcb0, (dr1, dr1b, dg_ln1_0, db_ln1_0) = ffn_bwd(0, dr, drb, f0, hf0, y1, xh1, rs1)
    g_wout0 = g_out(cat, dr1b, "g_ev_out")
    dcat = mm_nt_plain(dr1b, wout0, tm=tm_big, tn=_tile(DA + DB, 512), name="d_ev_out")
    dh0, d_ws, d_bsT, d_lg, d_lb, d_sc, d_wp = gating_bwd(h0, dcat, ev_ln_v_g, ev_ln_v_b, ws, wsT, bsT, wpb,
                                                          ev_pool_scale, name="gating_bwd")
    g_win0 = mm_tn(xb16, dh0, a_spec=pl.BlockSpec((tt, tkd), lambda kb, nb, t: (t, kb)),
                   g_spec=pl.BlockSpec((tt, Ns0), lambda kb, nb, t: (t, nb)),
                   o_spec=pl.BlockSpec((1, tkd, Ns0), lambda kb, nb, t: (nb, kb, 0)), out_shape=(4, D, Ns0),
                   grid=(D // tkd, 4, n_t), acc_shape=(tkd, Ns0), P=1, tnw=Ns0, name="g_ev_in")
    grad_x = mm_nt_res(dh0, win0, dr1, None, a_spec=pl.BlockSpec((tm_ln, Ns0), lambda i, k: (i, k)),
                       w_spec=pl.BlockSpec((1, D, Ns0), lambda i, k: (k, 0, 0)), P=1, tnw=Ns0, n_k=4, tm=tm_ln,
                       name="d_ev_in")

    big_grads = [g_win0, g_wout0.reshape(4, (DA + DB) // 4, D), g_win1, g_wout1.reshape(4, DC // 4, D),
                 g_up0, g_up1, g_dn0.reshape(4, F // 4, D), g_dn1.reshape(4, F // 4, D)]
    groups = [[0], [1], [2], [3], [4, 5], [6, 7]]
    split = [g.reshape(4, 2, g.shape[1] // 2, g.shape[2]) for g in big_grads]
    n_g = len(split)
    ex = rs_exchange_halves(split, name="rs_exchange_halves")
    summed = [add_halves(ex[t], ex[n_g + t], name=f"rs_add_halves{t}") for t in range(n_g)]
    got = rs_exchange_chips([s[0] for s in summed], name="rs_exchange_chips")
    finished = [add_chips(summed[t][1], got[t], name=f"rs_add_chips{t}") for t in range(n_g)]
    shared = rs_share_halves(finished, groups, name="rs_share_halves")
    big_g = {n: s.reshape(w[n].shape) for n, s in zip(_BIG, shared)}

    small_full = {
        'ev_ln_v_g': d_lg, 'ev_ln_v_b': d_lb, 'ev_w_s': d_ws[None], 'ev_b_s': d_bsT.T[None], 'ev_w_pool': d_wp[None],
        'ev_pool_scale': d_sc, 'od_norm_g': d_gn, 'lb_param': d_lbp,
        'ffn_conv_w': jnp.stack([jnp.concatenate([dcw0[0], dcw0[1]], axis=-1),
                                 jnp.concatenate([dcw1[0], dcw1[1]], axis=-1)]),
        'ffn_conv_b': jnp.stack([jnp.concatenate([dcb0[0, 0], dcb0[1, 0]]), jnp.concatenate([dcb1[0, 0], dcb1[1, 0]])]),
        'ln1_g': jnp.concatenate([dg_ln1_0, dg_ln1_1]), 'ln1_b': jnp.concatenate([db_ln1_0, db_ln1_1]),
        'ln2_g': jnp.concatenate([dg_ln2_0, dg_ln2_1]), 'ln2_b': jnp.concatenate([db_ln2_0, db_ln2_1])}
    packed = _pack([small_full[n] for n in _SMALL] + [loss_p[0, 0:1]])
    reduced = _unpack(all_reduce_small(packed, name="all_reduce_small"),
                      [small_full[n].shape for n in _SMALL] + [(1,)])
    small_g = dict(zip(_SMALL, reduced[:-1]))
    loss = reduced[-1][0]
    small_g['ev_w_pool'] = lax.dynamic_slice_in_dim(small_g['ev_w_pool'], chip * (CG // 4), CG // 4, axis=2)
    small_g['ffn_conv_w'] = lax.dynamic_slice_in_dim(small_g['ffn_conv_w'], chip * (F // 2), F // 2, axis=2)
    small_g['od_norm_g'] = lax.dynamic_slice_in_dim(small_g['od_norm_g'], chip * (DC // 4), DC // 4, axis=1)

    grads = {**big_g, **small_g}
    delta, new_m, new_v = {}, {}, {}
    for n in _BIG:
        delta[n], new_m[n], new_v[n] = adamw(w[n], big_g[n], mom[n], vel[n], "adamw_" + n)
    ps = [_pack([d[n] for n in _SMALL]) for d in (w, small_g, mom, vel)]
    upd = adamw(*[p[None] for p in ps], "adamw_small")
    shapes = [w[n].shape for n in _SMALL]
    for d, buf in zip((delta, new_m, new_v), upd):
        d.update(zip(_SMALL, _unpack(buf[0], shapes)))

    return (loss, grad_x[None], *[grads[n] for n in _WEIGHTS], *[delta[n] for n in _WEIGHTS],
            *[new_m[n] for n in _WEIGHTS], *[new_v[n] for n in _WEIGHTS])
```

```python
import math

import jax
import jax.numpy as jnp
from jax import lax
from jax.experimental import pallas as pl
from jax.experimental.pallas import tpu as pltpu

F32 = jnp.float32
BF16 = jnp.bfloat16
MESH = pl.DeviceIdType.MESH
ANY = pl.BlockSpec(memory_space=pl.ANY)
VMEM_SPEC = pl.BlockSpec(memory_space=pltpu.VMEM)

DEPTH = 2
ALPHA = (2 * DEPTH) ** 0.25
LN_EPS = 1e-5
A_HEAD = 128
A_CHUNK = 128
B_WINDOWS = (2, 4, 8, 16)
POOL_HALO = 16
C_HEAD = 128
C_CHUNK = 64
CONV_HALO = 8
ADAM_LR = 0.001
ADAM_B1 = 0.9
ADAM_B2 = 0.999
ADAM_EPS = 1e-08
ADAM_WD = 0.01
ADAM_STEP = 10
V7X_VMEM_LIMIT_BYTES = 56 * 1024 * 1024
LANES = 128
PACK_ALIGN = 8 * LANES


def _call(body, *, name, out_shape, grid=(), in_specs=None, out_specs=None, scratch=(), prefetch=0, aliases=None):
    spec = pltpu.PrefetchScalarGridSpec(num_scalar_prefetch=prefetch, grid=grid, in_specs=in_specs,
                                        out_specs=out_specs, scratch_shapes=list(scratch))
    return pl.pallas_call(body, name=name, grid_spec=spec, out_shape=out_shape,
                          input_output_aliases=aliases or {},
                          compiler_params=pltpu.CompilerParams(vmem_limit_bytes=V7X_VMEM_LIMIT_BYTES))


def _tile(n, pref, unit=LANES):
    if n <= pref:
        return n
    t = (pref // unit) * unit
    while t > unit and n % t:
        t -= unit
    assert n % t == 0, (n, pref, unit)
    return t


def _dot(a, b):
    return jnp.dot(a, b, preferred_element_type=F32)


def _dot_nt(a, b):
    return lax.dot_general(a, b, (((1,), (1,)), ((), ())), preferred_element_type=F32)


def _dot_tn(a, b):
    return lax.dot_general(a, b, (((0,), (0,)), ((), ())), preferred_element_type=F32)


def _sigmoid(x):
    return jax.nn.sigmoid(x)


_GELU_C = math.sqrt(2.0 / math.pi)


def _gelu(x):
    return 0.5 * x * (1.0 + jnp.tanh(_GELU_C * (x + 0.044715 * x * x * x)))


def _gelu_grad(x):
    th = jnp.tanh(_GELU_C * (x + 0.044715 * x * x * x))
    return 0.5 * (1.0 + th) + 0.5 * x * (1.0 - th * th) * _GELU_C * (1.0 + 3.0 * 0.044715 * x * x)


def _ln_fwd(r, g, b):
    mu = jnp.mean(r, axis=-1, keepdims=True)
    xc = r - mu
    var = jnp.mean(xc * xc, axis=-1, keepdims=True)
    rstd = lax.rsqrt(var + LN_EPS)
    xh = xc * rstd
    return xh * g + b, xh, rstd


def _ln_bwd(dy, xh, rstd, g):
    dxh = dy * g
    m1 = jnp.mean(dxh, axis=-1, keepdims=True)
    m2 = jnp.mean(dxh * xh, axis=-1, keepdims=True)
    dr = rstd * (dxh - m1 - xh * m2)
    return dr, jnp.sum(dy * xh, axis=0, keepdims=True), jnp.sum(dy, axis=0, keepdims=True)


def _exact_tri_dot(tri, x):
    hi = x.astype(BF16)
    r1 = x - hi.astype(F32)
    mid = r1.astype(BF16)
    lo = (r1 - mid.astype(F32)).astype(BF16)
    return _dot(tri, hi) + _dot(tri, mid) + _dot(tri, lo)


def cast_bf16(a3, name):
    L, R, C = a3.shape
    br = _tile(R, max(8, (1 << 20) // C), 8)

    def body(a_ref, o_ref):
        o_ref[...] = a_ref[...].astype(BF16)

    return _call(body, name=name, grid=(L, R // br),
                 in_specs=[pl.BlockSpec((None, br, C), lambda l, r: (l, r, 0))],
                 out_specs=pl.BlockSpec((None, br, C), lambda l, r: (l, r, 0)),
                 out_shape=jax.ShapeDtypeStruct((L, R, C), BF16))(a3)


def adamw(w, g, m, v, name):
    L, R, C = w.shape
    br = _tile(R, max(8, (1 << 19) // C), 8)
    c1 = 1.0 - ADAM_B1 ** ADAM_STEP
    c2 = 1.0 - ADAM_B2 ** ADAM_STEP

    def body(w_ref, g_ref, m_ref, v_ref, d_ref, nm_ref, nv_ref):
        gg = g_ref[...]
        nm = ADAM_B1 * m_ref[...] + (1.0 - ADAM_B1) * gg
        nv = ADAM_B2 * v_ref[...] + (1.0 - ADAM_B2) * (gg * gg)
        d_ref[...] = -ADAM_LR * ((nm / c1) / (jnp.sqrt(nv / c2) + ADAM_EPS) + ADAM_WD * w_ref[...])
        nm_ref[...] = nm
        nv_ref[...] = nv

    spec = pl.BlockSpec((None, br, C), lambda l, r: (l, r, 0))
    sds = jax.ShapeDtypeStruct((L, R, C), F32)
    return _call(body, name=name, grid=(L, R // br), in_specs=[spec] * 4, out_specs=[spec] * 3,
                 out_shape=[sds] * 3)(w, g, m, v)


def mm_nn(a, w, *, w_spec, P, tnw, tm, n_j, name):
    T, K = a.shape
    bw = P * tnw

    def body(a_ref, w_ref, o_ref):
        av = a_ref[...]
        for p in range(P):
            o_ref[:, p * tnw:(p + 1) * tnw] = _dot(av, w_ref[p]).astype(BF16)

    return _call(body, name=name, grid=(T // tm, n_j),
                 in_specs=[pl.BlockSpec((tm, K), lambda i, j: (i, 0)), w_spec],
                 out_specs=pl.BlockSpec((tm, bw), lambda i, j: (i, j)),
                 out_shape=jax.ShapeDtypeStruct((T, n_j * bw), BF16))(a, w)


def mm_ln(a, w, res, rg, rb, g, b, *, w_spec, K, tk, tm, name):
    T, N = res.shape
    n_k = K // tk

    def body(a_ref, w_ref, res_ref, rg_ref, rb_ref, g_ref, b_ref, xh_ref, y_ref, rs_ref, acc):
        k = pl.program_id(1)

        @pl.when(k == 0)
        def _():
            acc[...] = jnp.zeros_like(acc)

        acc[...] += _dot(a_ref[...], w_ref[...])

        @pl.when(k == n_k - 1)
        def _():
            r = ALPHA * (res_ref[...] * rg_ref[...] + rb_ref[...]) + acc[...]
            y, xh, rstd = _ln_fwd(r, g_ref[...], b_ref[...])
            xh_ref[...] = xh
            y_ref[...] = y.astype(BF16)
            rs_ref[...] = rstd

    row = pl.BlockSpec((tm, N), lambda i, k: (i, 0))
    vec = pl.BlockSpec((1, N), lambda i, k: (0, 0))
    return _call(body, name=name, grid=(T // tm, n_k),
                 in_specs=[pl.BlockSpec((tm, tk), lambda i, k: (i, k)), w_spec, row, vec, vec, vec, vec],
                 out_specs=[row, row, pl.BlockSpec((tm, 1), lambda i, k: (i, 0))],
                 out_shape=[jax.ShapeDtypeStruct((T, N), F32), jax.ShapeDtypeStruct((T, N), BF16),
                            jax.ShapeDtypeStruct((T, 1), F32)],
                 scratch=[pltpu.VMEM((tm, N), F32)])(a, w, res, rg, rb, g, b)


def mm_nt_plain(a, w, *, tm, tn, name):
    T, K = a.shape
    N = w.shape[0]

    def body(a_ref, w_ref, o_ref):
        o_ref[...] = _dot_nt(a_ref[...], w_ref[...]).astype(BF16)

    return _call(body, name=name, grid=(T // tm, N // tn),
                 in_specs=[pl.BlockSpec((tm, K), lambda i, j: (i, 0)), pl.BlockSpec((tn, K), lambda i, j: (j, 0))],
                 out_specs=pl.BlockSpec((tm, tn), lambda i, j: (i, j)),
                 out_shape=jax.ShapeDtypeStruct((T, N), BF16))(a, w)


def mm_nt_res(a, w, res, ln, *, a_spec, w_spec, P, tnw, n_k, tm, name):
    T, N = res.shape
    n_i = T // tm

    def body(*refs):
        if ln is None:
            a_ref, w_ref, res_ref, o_ref, acc = refs
        else:
            a_ref, w_ref, res_ref, xh_ref, rs_ref, g_ref, dr_ref, drb_ref, dg_ref, db_ref, acc = refs
        i = pl.program_id(0)
        k = pl.program_id(1)

        @pl.when(k == 0)
        def _():
            acc[...] = jnp.zeros_like(acc)

        s = acc[...]
        for p in range(P):
            s = s + _dot_nt(a_ref[:, p * tnw:(p + 1) * tnw], w_ref[p])
        acc[...] = s

        @pl.when(k == n_k - 1)
        def _():
            d = ALPHA * res_ref[...] + acc[...]
            if ln is None:
                o_ref[...] = d
            else:
                dr, dg, db = _ln_bwd(d, xh_ref[...], rs_ref[...], g_ref[...])
                dr_ref[...] = dr
                drb_ref[...] = dr.astype(BF16)

                @pl.when(i == 0)
                def _():
                    dg_ref[...] = jnp.zeros_like(dg_ref)
                    db_ref[...] = jnp.zeros_like(db_ref)

                dg_ref[...] += dg
                db_ref[...] += db

    row = pl.BlockSpec((tm, N), lambda i, k: (i, 0))
    vec = pl.BlockSpec((1, N), lambda i, k: (0, 0))
    scratch = [pltpu.VMEM((tm, N), F32)]
    if ln is None:
        return _call(body, name=name, grid=(n_i, n_k), in_specs=[a_spec, w_spec, row], out_specs=row,
                     out_shape=jax.ShapeDtypeStruct((T, N), F32), scratch=scratch)(a, w, res)
    xh, rstd, g = ln
    return _call(body, name=name, grid=(n_i, n_k),
                 in_specs=[a_spec, w_spec, row, row, pl.BlockSpec((tm, 1), lambda i, k: (i, 0)), vec],
                 out_specs=[row, row, vec, vec],
                 out_shape=[jax.ShapeDtypeStruct((T, N), F32), jax.ShapeDtypeStruct((T, N), BF16),
                            jax.ShapeDtypeStruct((1, N), F32), jax.ShapeDtypeStruct((1, N), F32)],
                 scratch=scratch)(a, w, res, xh, rstd, g)


def mm_tn(a, g, *, a_spec, g_spec, o_spec, out_shape, grid, acc_shape, P, tnw, name):
    n_t = grid[2]

    def body(a_ref, g_ref, o_ref, acc):
        t = pl.program_id(2)

        @pl.when(t == 0)
        def _():
            acc[...] = jnp.zeros_like(acc)

        acc[...] += _dot_tn(a_ref[...], g_ref[...])

        @pl.when(t == n_t - 1)
        def _():
            for p in range(P):
                o_ref[p] = acc[:, p * tnw:(p + 1) * tnw].astype(BF16)

    return _call(body, name=name, grid=grid, in_specs=[a_spec, g_spec], out_specs=o_spec,
                 out_shape=jax.ShapeDtypeStruct(out_shape, BF16),
                 scratch=[pltpu.VMEM(acc_shape, F32)])(a, g)


def _causal_conv(ext, halo, w, b):
    s1 = pltpu.roll(ext, 1, 0)[halo:]
    s2 = pltpu.roll(ext, 2, 0)[halo:]
    return b + w[2:3] * ext[halo:] + w[1:2] * s1 + w[0:1] * s2, s1, s2


def ffn_up(xb, wup, cw, cb, l, *, name):
    T, D = xb.shape
    Ns = wup.shape[-1]
    F = 2 * Ns
    tn = _tile(Ns, 256)
    nps = Ns // tn
    n_j = F // tn
    tm = _tile(T, 1024, 8)

    def body(x_ref, wa_ref, wv_ref, cwa_ref, cwv_ref, cba_ref, cbv_ref, h_ref, f_ref, carry):
        i = pl.program_id(1)

        @pl.when(i == 0)
        def _():
            carry[...] = jnp.zeros_like(carry)

        xv = x_ref[...]
        ha = _dot(xv, wa_ref[...])
        hv = _dot(xv, wv_ref[...])
        ca, _, _ = _causal_conv(jnp.concatenate([carry[0], ha], axis=0), CONV_HALO, cwa_ref[...], cba_ref[...])
        cv, _, _ = _causal_conv(jnp.concatenate([carry[1], hv], axis=0), CONV_HALO, cwv_ref[...], cbv_ref[...])
        carry[0] = ha[tm - CONV_HALO:]
        carry[1] = hv[tm - CONV_HALO:]
        h_ref[0] = ha.astype(BF16)
        h_ref[1] = hv.astype(BF16)
        f_ref[...] = (ca * _sigmoid(ca) * cv).astype(BF16)

    wspec_a = pl.BlockSpec((None, None, D, tn), lambda j, i: (l, j // nps, 0, j % nps))
    wspec_v = pl.BlockSpec((None, None, D, tn), lambda j, i: (l, 2 + j // nps, 0, j % nps))
    return _call(
        body, name=name, grid=(n_j, T // tm),
        in_specs=[pl.BlockSpec((tm, D), lambda j, i: (i, 0)), wspec_a, wspec_v,
                  pl.BlockSpec((None, 3, tn), lambda j, i: (l, 0, j)),
                  pl.BlockSpec((None, 3, tn), lambda j, i: (l, 0, n_j + j)),
                  pl.BlockSpec((None, 1, tn), lambda j, i: (l, 0, j)),
                  pl.BlockSpec((None, 1, tn), lambda j, i: (l, 0, n_j + j))],
        out_specs=[pl.BlockSpec((2, tm, tn), lambda j, i: (0, i, j)), pl.BlockSpec((tm, tn), lambda j, i: (i, j))],
        out_shape=[jax.ShapeDtypeStruct((2, T, F), BF16), jax.ShapeDtypeStruct((T, F), BF16)],
        scratch=[pltpu.VMEM((2, CONV_HALO, tn), F32)])(xb, wup, wup, cw, cw, cb, cb)


def ffn_dgate(db16, wdn, h, cw, cb, l, *, name):
    T, D = db16.shape
    F = h.shape[-1]
    tn = _tile(F, 512)
    n_j = F // tn
    tm = _tile(T, 512, 16)
    n_i = T // tm
    hb = 16
    n_ext = tm + CONV_HALO

    def body(d_ref, w_ref, h_ref, hh_ref, cwa_ref, cwv_ref, cba_ref, cbv_ref, dh_ref, dcw_ref, dcb_ref, carry):
        ip = pl.program_id(1)
        i = n_i - 1 - ip

        @pl.when(ip == 0)
        def _():
            carry[...] = jnp.zeros_like(carry)
            dcw_ref[...] = jnp.zeros_like(dcw_ref)
            dcb_ref[...] = jnp.zeros_like(dcb_ref)

        df = _dot_nt(d_ref[...], w_ref[...])
        halo = jnp.where(i > 0, hh_ref[...].astype(F32), 0.0)
        ha = h_ref[0].astype(F32)
        hv = h_ref[1].astype(F32)
        cwa, cwv = cwa_ref[...], cwv_ref[...]
        ca, a1, a2 = _causal_conv(jnp.concatenate([halo[0], ha], axis=0), hb, cwa, cba_ref[...])
        cv, v1, v2 = _causal_conv(jnp.concatenate([halo[1], hv], axis=0), hb, cwv, cbv_ref[...])
        sig = _sigmoid(ca)
        da = df * cv * sig * (1.0 + ca * (1.0 - sig))
        dv = df * ca * sig
        for half, (dc, h0, h1, h2, w) in enumerate(((da, ha, a1, a2, cwa), (dv, hv, v1, v2, cwv))):
            dcb_ref[half] += jnp.sum(dc, axis=0, keepdims=True)
            dcw_ref[half] += jnp.concatenate(
                [jnp.sum(dc * h2, axis=0, keepdims=True), jnp.sum(dc * h1, axis=0, keepdims=True),
                 jnp.sum(dc * h0, axis=0, keepdims=True)], axis=0)
            ext = jnp.concatenate([dc, carry[half]], axis=0)
            n1 = pltpu.roll(ext, n_ext - 1, 0)[:tm]
            n2 = pltpu.roll(ext, n_ext - 2, 0)[:tm]
            dh_ref[half] = (w[2:3] * dc + w[1:2] * n1 + w[0:1] * n2).astype(BF16)
            carry[half] = dc[:CONV_HALO]

    rev = lambda ip: n_i - 1 - ip
    return _call(
        body, name=name, grid=(n_j, n_i),
        in_specs=[pl.BlockSpec((tm, D), lambda j, ip: (rev(ip), 0)),
                  pl.BlockSpec((None, tn, D), lambda j, ip: (l, j, 0)),
                  pl.BlockSpec((2, tm, tn), lambda j, ip: (0, rev(ip), j)),
                  pl.BlockSpec((2, hb, tn), lambda j, ip: (0, jnp.maximum(rev(ip) * (tm // hb) - 1, 0), j)),
                  pl.BlockSpec((None, 3, tn), lambda j, ip: (l, 0, j)),
                  pl.BlockSpec((None, 3, tn), lambda j, ip: (l, 0, n_j + j)),
                  pl.BlockSpec((None, 1, tn), lambda j, ip: (l, 0, j)),
                  pl.BlockSpec((None, 1, tn), lambda j, ip: (l, 0, n_j + j))],
        out_specs=[pl.BlockSpec((2, tm, tn), lambda j, ip: (0, rev(ip), j)),
                   pl.BlockSpec((2, 3, tn), lambda j, ip: (0, 0, j)),
                   pl.BlockSpec((2, 1, tn), lambda j, ip: (0, 0, j))],
        out_shape=[jax.ShapeDtypeStruct((2, T, F), BF16), jax.ShapeDtypeStruct((2, 3, F), F32),
                   jax.ShapeDtypeStruct((2, 1, F), F32)],
        scratch=[pltpu.VMEM((2, CONV_HALO, tn), F32)])(db16, wdn, h, h, cw, cw, cb, cb)


def _pool_fwd(ext, xb_g, t_glob, win):
    e = ext
    sft = 1
    while sft < win:
        e = e + pltpu.roll(e, sft, 0)
        sft *= 2
    cnt = jnp.minimum(t_glob + 1.0, float(win))
    return e[POOL_HALO:] / cnt - xb_g


def gating_fwd(h0, lg, lb, ws, bsT, wp, sc, *, name):
    T = h0.shape[0]
    DA = lg.shape[-1]
    DB = sc.shape[-1]
    HA = DA // A_HEAD
    G = len(B_WINDOWS)
    CG = DB // G
    tm = _tile(T, 512, A_CHUNK)
    n_c = tm // A_CHUNK

    def body(h_ref, halo_ref, lg_ref, lb_ref, ws_ref, bsT_ref, wp_ref, sc_ref, cat_ref):
        i = pl.program_id(0)
        hu = h_ref[:, 0:DA].astype(F32)
        hv = h_ref[:, DA:2 * DA].astype(F32)
        xb = h_ref[:, 2 * DA:].astype(F32)
        u = _gelu(hu)
        vn, _, _ = _ln_fwd(_gelu(hv), lg_ref[...], lb_ref[...])
        vnb = vn.astype(BF16)
        rr = lax.broadcasted_iota(jnp.int32, (A_CHUNK, A_CHUNK), 0)
        cc = lax.broadcasted_iota(jnp.int32, (A_CHUNK, A_CHUNK), 1)
        for hh in range(HA):
            wt = jnp.where(rr >= cc, ws_ref[hh], 0.0).astype(BF16)
            cs = slice(hh * A_HEAD, (hh + 1) * A_HEAD)
            for n in range(n_c):
                rs = slice(n * A_CHUNK, (n + 1) * A_CHUNK)
                s = _dot(wt, vnb[rs, cs]) + bsT_ref[:, hh:hh + 1]
                cat_ref[rs, cs] = (u[rs, cs] * s).astype(BF16)
        halo = jnp.where(i > 0, halo_ref[...].astype(F32), 0.0)
        ext = jnp.concatenate([halo, xb], axis=0)
        t_glob = (i * tm + lax.broadcasted_iota(jnp.int32, (tm, 1), 0)).astype(F32)
        for g, win in enumerate(B_WINDOWS):
            gs = slice(g * CG, (g + 1) * CG)
            p = _pool_fwd(ext[:, gs], xb[:, gs], t_glob, win)
            z = _dot(p.astype(BF16), wp_ref[g])
            cat_ref[:, DA + g * CG:DA + (g + 1) * CG] = (z * sc_ref[:, gs]).astype(BF16)

    full = lambda a: pl.BlockSpec(a.shape, lambda i: (0,) * a.ndim)
    hpb = tm // POOL_HALO
    return _call(
        body, name=name, grid=(T // tm,),
        in_specs=[pl.BlockSpec((tm, 2 * DA + DB), lambda i: (i, 0)),
                  pl.BlockSpec((POOL_HALO, DB), lambda i: (jnp.maximum(i * hpb - 1, 0), 2 * DA // DB)),
                  full(lg), full(lb), full(ws), full(bsT), full(wp), full(sc)],
        out_specs=pl.BlockSpec((tm, DA + DB), lambda i: (i, 0)),
        out_shape=jax.ShapeDtypeStruct((T, DA + DB), BF16))(h0, h0, lg, lb, ws, bsT, wp, sc)


def gating_bwd(h0, dcat, lg, lb, ws, wsT, bsT, wp, sc, *, name):
    T = h0.shape[0]
    DA = lg.shape[-1]
    DB = sc.shape[-1]
    HA = DA // A_HEAD
    G = len(B_WINDOWS)
    CG = DB // G
    tm = _tile(T, 512, A_CHUNK)
    n_i = T // tm
    n_c = tm // A_CHUNK
    n_ext = tm + POOL_HALO

    def body(h_ref, halo_ref, dc_ref, dhalo_ref, lg_ref, lb_ref, ws_ref, wsT_ref, bsT_ref, wp_ref, sc_ref,
             dh_ref, dws_ref, dbsT_ref, dlg_ref, dlb_ref, dsc_ref, dwp_ref, dvn_sc):
        i = pl.program_id(0)

        @pl.when(i == 0)
        def _():
            for r in (dws_ref, dbsT_ref, dlg_ref, dlb_ref, dsc_ref, dwp_ref):
                r[...] = jnp.zeros_like(r)

        hu = h_ref[:, 0:DA].astype(F32)
        hv = h_ref[:, DA:2 * DA].astype(F32)
        xb = h_ref[:, 2 * DA:].astype(F32)
        u = _gelu(hu)
        gu = _gelu_grad(hu)
        lgv = lg_ref[...]
        vn, vhat, rstd = _ln_fwd(_gelu(hv), lgv, lb_ref[...])
        vnb = vn.astype(BF16)
        rr = lax.broadcasted_iota(jnp.int32, (A_CHUNK, A_CHUNK), 0)
        cc = lax.broadcasted_iota(jnp.int32, (A_CHUNK, A_CHUNK), 1)
        for hh in range(HA):
            wt = jnp.where(rr >= cc, ws_ref[hh], 0.0).astype(BF16)
            wtT = jnp.where(rr <= cc, wsT_ref[hh], 0.0).astype(BF16)
            cs = slice(hh * A_HEAD, (hh + 1) * A_HEAD)
            dws = jnp.zeros((A_CHUNK, A_CHUNK), F32)
            dbs = jnp.zeros((A_CHUNK, 1), F32)
            for n in range(n_c):
                rs = slice(n * A_CHUNK, (n + 1) * A_CHUNK)
                vb = vnb[rs, cs]
                s = _dot(wt, vb) + bsT_ref[:, hh:hh + 1]
                dya = dc_ref[rs, cs].astype(F32)
                ds = dya * u[rs, cs]
                dh_ref[rs, cs] = (dya * s * gu[rs, cs]).astype(BF16)
                dsb = ds.astype(BF16)
                dbs = dbs + jnp.sum(ds, axis=1, keepdims=True)
                dws = dws + _dot_nt(dsb, vb)
                dvn_sc[rs, cs] = _dot(wtT, dsb)
            dws_ref[hh] += jnp.where(rr >= cc, dws, 0.0)
            dbsT_ref[:, hh:hh + 1] += dbs
        dvg, dlg, dlb = _ln_bwd(dvn_sc[...], vhat, rstd, lgv)
        dlg_ref[...] += dlg
        dlb_ref[...] += dlb
        dh_ref[:, DA:2 * DA] = (dvg * _gelu_grad(hv)).astype(BF16)

        halo = jnp.where(i > 0, halo_ref[...].astype(F32), 0.0)
        ext = jnp.concatenate([halo, xb], axis=0)
        t_glob = (i * tm + lax.broadcasted_iota(jnp.int32, (tm, 1), 0)).astype(F32)
        t_ext = (i * tm + lax.broadcasted_iota(jnp.int32, (n_ext, 1), 0)).astype(F32)
        dyb = dc_ref[:, DA:].astype(F32)
        dhalo = jnp.where(i < n_i - 1, dhalo_ref[...].astype(F32), 0.0)
        dyb_ext = jnp.concatenate([dyb, dhalo], axis=0)
        for g, win in enumerate(B_WINDOWS):
            gs = slice(g * CG, (g + 1) * CG)
            pb = _pool_fwd(ext[:, gs], xb[:, gs], t_glob, win).astype(BF16)
            wpg = wp_ref[g]
            z = _dot(pb, wpg)
            dsc_ref[:, gs] += jnp.sum(dyb[:, gs] * z, axis=0, keepdims=True)
            dzb = (dyb_ext[:, gs] * sc_ref[:, gs]).astype(BF16)
            dwp_ref[g] += _dot_tn(pb, dzb[:tm])
            dp = _dot_nt(dzb, wpg)
            e = dp / jnp.minimum(t_ext + 1.0, float(win))
            sft = 1
            while sft < win:
                e = e + pltpu.roll(e, n_ext - sft, 0)
                sft *= 2
            dh_ref[:, 2 * DA + g * CG:2 * DA + (g + 1) * CG] = (e[:tm] - dp[:tm]).astype(BF16)

    full = lambda a: pl.BlockSpec(a.shape, lambda i: (0,) * a.ndim)
    hpb = tm // POOL_HALO
    n_hb = T // POOL_HALO
    outs = [jax.ShapeDtypeStruct((T, 2 * DA + DB), BF16), jax.ShapeDtypeStruct(ws.shape, F32),
            jax.ShapeDtypeStruct(bsT.shape, F32), jax.ShapeDtypeStruct(lg.shape, F32),
            jax.ShapeDtypeStruct(lb.shape, F32), jax.ShapeDtypeStruct(sc.shape, F32),
            jax.ShapeDtypeStruct(wp.shape, F32)]
    return _call(
        body, name=name, grid=(n_i,),
        in_specs=[pl.BlockSpec((tm, 2 * DA + DB), lambda i: (i, 0)),
                  pl.BlockSpec((POOL_HALO, DB), lambda i: (jnp.maximum(i * hpb - 1, 0), 2 * DA // DB)),
                  pl.BlockSpec((tm, DA + DB), lambda i: (i, 0)),
                  pl.BlockSpec((POOL_HALO, DB), lambda i: (jnp.minimum((i + 1) * hpb, n_hb - 1), DA // DB)),
                  full(lg), full(lb), full(ws), full(wsT), full(bsT), full(wp), full(sc)],
        out_specs=[pl.BlockSpec((tm, 2 * DA + DB), lambda i: (i, 0))] + [full(o) for o in outs[1:]],
        out_shape=outs,
        scratch=[pltpu.VMEM((tm, DA), F32)])(h0, h0, dcat, dcat, lg, lb, ws, wsT, bsT, wp, sc)


HP = 2 * C_HEAD


def _hgrn_gates(blk, lbv, tri):
    q = blk[:, 0:HP]
    sg = _sigmoid(blk[:, HP:2 * HP])
    f = lbv + (1.0 - lbv) * sg
    bcum = _exact_tri_dot(tri, jnp.log(f))
    blast = bcum[C_CHUNK - 1:C_CHUNK]
    sq = _sigmoid(q)
    k = 1.0 - f
    e_in = jnp.exp(bcum)
    e_out = jnp.exp(-bcum)
    e_end = jnp.exp(blast - bcum)
    return dict(q=q, sq=sq, sg=sg, f=f, k=k, bcum=bcum, blast=blast, e_in=e_in, e_out=e_out, e_end=e_end,
                qd=q * sq * e_in, kd=k * e_out, ke=k * e_end, dec=jnp.exp(blast), v=blk[:, 2 * HP:3 * HP],
                gg=blk[:, 3 * HP:4 * HP])


def _lower_bound(lbp_ref):
    p0, p1 = lbp_ref[0:1], lbp_ref[1:2]
    mx = jnp.maximum(p0, p1)
    e0, e1 = jnp.exp(p0 - mx), jnp.exp(p1 - mx)
    return e1 / (e0 + e1)


def hgrn_fwd(h1p, lbp, gn, *, name):
    T = h1p.shape[0]
    DC = gn.shape[-1]
    n_p = DC // HP
    tt = _tile(T, 512, C_CHUNK)
    n_c = tt // C_CHUNK

    def body(h_ref, lbp_ref, gn_ref, y_ref, o_ref, sp_ref, st):
        i = pl.program_id(1)

        @pl.when(i == 0)
        def _():
            st[...] = jnp.zeros_like(st)

        lbv = _lower_bound(lbp_ref)
        gnv = gn_ref[...]
        rr = lax.broadcasted_iota(jnp.int32, (C_CHUNK, C_CHUNK), 0)
        cc = lax.broadcasted_iota(jnp.int32, (C_CHUNK, C_CHUNK), 1)
        causal = rr >= cc
        tri = jnp.where(causal, 1.0, 0.0).astype(BF16)

        def chunk(n, _):
            rows = pl.ds(pl.multiple_of(n * C_CHUNK, C_CHUNK), C_CHUNK)
            a = _hgrn_gates(h_ref[rows, :].astype(F32), lbv, tri)
            outs = []
            for hd in range(2):
                cs = slice(hd * C_HEAD, (hd + 1) * C_HEAD)
                qd, kd, ke = a["qd"][:, cs].astype(BF16), a["kd"][:, cs].astype(BF16), a["ke"][:, cs].astype(BF16)
                vb = a["v"][:, cs].astype(BF16)
                s_t = st[hd]
                att = jnp.where(causal, _dot_nt(qd, kd), 0.0).astype(BF16)
                outs.append(_dot(att, vb) + _dot_nt(qd, s_t.astype(BF16)))
                sp_ref[hd, n] = s_t
                st[hd] = a["dec"][:, cs] * s_t + _dot_tn(vb, ke)
            o = jnp.concatenate(outs, axis=1)
            o_ref[rows, :] = o.astype(BF16)
            ys = []
            for hd in range(2):
                oh = o[:, hd * C_HEAD:(hd + 1) * C_HEAD]
                ys.append(oh * lax.rsqrt(jnp.mean(oh * oh, axis=-1, keepdims=True) + LN_EPS))
            y_ref[rows, :] = (jnp.concatenate(ys, axis=1) * gnv * _sigmoid(a["gg"])).astype(BF16)
            return 0

        lax.fori_loop(0, n_c, chunk, 0)

    return _call(
        body, name=name, grid=(n_p, T // tt),
        in_specs=[pl.BlockSpec((tt, 4 * HP), lambda p, i: (i, p)), pl.BlockSpec((2, HP), lambda p, i: (0, p)),
                  pl.BlockSpec((1, HP), lambda p, i: (0, p))],
        out_specs=[pl.BlockSpec((tt, HP), lambda p, i: (i, p)), pl.BlockSpec((tt, HP), lambda p, i: (i, p)),
                   pl.BlockSpec((2, n_c, C_HEAD, C_HEAD), lambda p, i: (p, i, 0, 0))],
        out_shape=[jax.ShapeDtypeStruct((T, DC), BF16), jax.ShapeDtypeStruct((T, DC), BF16),
                   jax.ShapeDtypeStruct((2 * n_p, T // C_CHUNK, C_HEAD, C_HEAD), F32)],
        scratch=[pltpu.VMEM((2, C_HEAD, C_HEAD), F32)])(h1p, lbp, gn)


def hgrn_bwd(h1p, o_saved, dy, sp, lbp, gn, *, name):
    T = h1p.shape[0]
    DC = gn.shape[-1]
    n_p = DC // HP
    tt = _tile(T, 512, C_CHUNK)
    n_i = T // tt
    n_c = tt // C_CHUNK

    def body(h_ref, o_ref, dy_ref, sp_ref, lbp_ref, gn_ref, dh_ref, dgn_ref, dlbp_ref, dst, dlb_acc):
        ip = pl.program_id(1)

        @pl.when(ip == 0)
        def _():
            dst[...] = jnp.zeros_like(dst)
            dlb_acc[...] = jnp.zeros_like(dlb_acc)
            dgn_ref[...] = jnp.zeros_like(dgn_ref)

        lbv = _lower_bound(lbp_ref)
        gnv = gn_ref[...]
        rr = lax.broadcasted_iota(jnp.int32, (C_CHUNK, C_CHUNK), 0)
        cc = lax.broadcasted_iota(jnp.int32, (C_CHUNK, C_CHUNK), 1)
        causal = rr >= cc
        tri = jnp.where(causal, 1.0, 0.0).astype(BF16)
        tri_t = jnp.where(rr <= cc, 1.0, 0.0).astype(BF16)
        last_row = lax.broadcasted_iota(jnp.int32, (C_CHUNK, 1), 0) == C_CHUNK - 1

        def chunk(m, _):
            n = n_c - 1 - m
            rows = pl.ds(pl.multiple_of(n * C_CHUNK, C_CHUNK), C_CHUNK)
            a = _hgrn_gates(h_ref[rows, :].astype(F32), lbv, tri)
            o = o_ref[rows, :].astype(F32)
            dyv = dy_ref[rows, :].astype(F32)
            sgg = _sigmoid(a["gg"])
            ohs, rrs = [], []
            for hd in range(2):
                oh = o[:, hd * C_HEAD:(hd + 1) * C_HEAD]
                r = lax.rsqrt(jnp.mean(oh * oh, axis=-1, keepdims=True) + LN_EPS)
                ohs.append(oh * r)
                rrs.append(r)
            ohat = jnp.concatenate(ohs, axis=1)
            dyn = dyv * sgg
            dgg = dyv * ohat * gnv * sgg * (1.0 - sgg)
            dgn_ref[...] += jnp.sum(dyn * ohat, axis=0, keepdims=True)
            dxh = dyn * gnv
            dqd_l, dkd_l, dke_l, dv_l, ddec_l = [], [], [], [], []
            for hd in range(2):
                cs = slice(hd * C_HEAD, (hd + 1) * C_HEAD)
                dxh_h, oh_h = dxh[:, cs], ohat[:, cs]
                do = rrs[hd] * (dxh_h - oh_h * jnp.mean(dxh_h * oh_h, axis=-1, keepdims=True))
                dob = do.astype(BF16)
                qd, kd, ke = a["qd"][:, cs].astype(BF16), a["kd"][:, cs].astype(BF16), a["ke"][:, cs].astype(BF16)
                vb = a["v"][:, cs].astype(BF16)
                s_t = sp_ref[hd, n]
                ds_out = dst[hd]
                ds_outb = ds_out.astype(BF16)
                att = jnp.where(causal, _dot_nt(qd, kd), 0.0).astype(BF16)
                datt = jnp.where(causal, _dot_nt(dob, vb), 0.0).astype(BF16)
                dv_l.append(_dot_tn(att, dob) + _dot_nt(ke, ds_outb))
                dqd_l.append(_dot(datt, kd) + _dot(dob, s_t.astype(BF16)))
                dkd_l.append(_dot_tn(datt, qd))
                dke_l.append(_dot(vb, ds_outb))
                ddec_l.append(jnp.sum(ds_out * s_t, axis=0, keepdims=True))
                dst[hd] = a["dec"][:, cs] * ds_out + _dot_tn(dob, qd)
            dqd = jnp.concatenate(dqd_l, axis=1)
            dkd = jnp.concatenate(dkd_l, axis=1)
            dke = jnp.concatenate(dke_l, axis=1)
            dv = jnp.concatenate(dv_l, axis=1)
            ddec = jnp.concatenate(ddec_l, axis=1)
            dqs = dqd * a["e_in"]
            kek = dke * a["ke"]
            dbcum = dqd * a["qd"] - dkd * a["kd"] - kek
            dk = dkd * a["e_out"] + dke * a["e_end"]
            dblast = jnp.sum(kek, axis=0, keepdims=True) + ddec * a["dec"]
            dbcum = dbcum + jnp.where(last_row, dblast, 0.0)
            dlf = _exact_tri_dot(tri_t, dbcum)
            df = dlf / a["f"] - dk
            dlb_acc[...] += jnp.sum(df * (1.0 - a["sg"]), axis=0, keepdims=True)
            dfl = df * (1.0 - lbv) * a["sg"] * (1.0 - a["sg"])
            dq = dqs * a["sq"] * (1.0 + a["q"] * (1.0 - a["sq"]))
            dh_ref[rows, :] = jnp.concatenate([dq, dfl, dv, dgg], axis=1).astype(BF16)
            return 0

        lax.fori_loop(0, n_c, chunk, 0)

        @pl.when(ip == n_i - 1)
        def _():
            d1 = dlb_acc[...] * lbv * (1.0 - lbv)
            dlbp_ref[...] = jnp.concatenate([-d1, d1], axis=0)

    rev = lambda ip: n_i - 1 - ip
    return _call(
        body, name=name, grid=(n_p, n_i),
        in_specs=[pl.BlockSpec((tt, 4 * HP), lambda p, ip: (rev(ip), p)),
                  pl.BlockSpec((tt, HP), lambda p, ip: (rev(ip), p)),
                  pl.BlockSpec((tt, HP), lambda p, ip: (rev(ip), p)),
                  pl.BlockSpec((2, n_c, C_HEAD, C_HEAD), lambda p, ip: (p, rev(ip), 0, 0)),
                  pl.BlockSpec((2, HP), lambda p, ip: (0, p)), pl.BlockSpec((1, HP), lambda p, ip: (0, p))],
        out_specs=[pl.BlockSpec((tt, 4 * HP), lambda p, ip: (rev(ip), p)),
                   pl.BlockSpec((1, HP), lambda p, ip: (0, p)), pl.BlockSpec((2, HP), lambda p, ip: (0, p))],
        out_shape=[jax.ShapeDtypeStruct(h1p.shape, BF16), jax.ShapeDtypeStruct((1, DC), F32),
                   jax.ShapeDtypeStruct((2, DC), F32)],
        scratch=[pltpu.VMEM((2, C_HEAD, C_HEAD), F32), pltpu.VMEM((1, HP), F32)])(h1p, o_saved, dy, sp, lbp, gn)


def loss_bwd(xh, rstd, g, b, target, *, name):
    T, N = xh.shape
    tm = _tile(T, 512, 8)

    def body(xh_ref, rs_ref, g_ref, b_ref, t_ref, ls_ref, dr_ref, drb_ref, dg_ref, db_ref):
        i = pl.program_id(0)

        @pl.when(i == 0)
        def _():
            for r in (ls_ref, dg_ref, db_ref):
                r[...] = jnp.zeros_like(r)

        xhv, gv = xh_ref[...], g_ref[...]
        e = xhv * gv + b_ref[...] - t_ref[...]
        ls_ref[...] += 0.5 * jnp.sum(jnp.mean(e * e, axis=-1, keepdims=True), axis=0, keepdims=True)
        dr, dg, db = _ln_bwd(e / N, xhv, rs_ref[...], gv)
        dr_ref[...] = dr
        drb_ref[...] = dr.astype(BF16)
        dg_ref[...] += dg
        db_ref[...] += db

    row = pl.BlockSpec((tm, N), lambda i: (i, 0))
    vec = pl.BlockSpec((1, N), lambda i: (0, 0))
    return _call(body, name=name, grid=(T // tm,),
                 in_specs=[row, pl.BlockSpec((tm, 1), lambda i: (i, 0)), vec, vec, row],
                 out_specs=[pl.BlockSpec((1, LANES), lambda i: (0, 0)), row, row, vec, vec],
                 out_shape=[jax.ShapeDtypeStruct((1, LANES), F32), jax.ShapeDtypeStruct((T, N), F32),
                            jax.ShapeDtypeStruct((T, N), BF16), jax.ShapeDtypeStruct((1, N), F32),
                            jax.ShapeDtypeStruct((1, N), F32)])(xh, rstd, g, b, target)


def _mesh_pos():
    x, y, c = lax.axis_index("x"), lax.axis_index("y"), lax.axis_index("c")
    chips = [(1 - x, y), (x, 1 - y), (1 - x, 1 - y)]
    return x, y, c, chips


def _remote(src, dst, send, recv, j, dev):
    return pltpu.make_async_remote_copy(src_ref=src, dst_ref=dst, send_sem=send.at[j], recv_sem=recv.at[j],
                                        device_id=dev, device_id_type=MESH)


def mesh_ids():
    x, y, c, chips = _mesh_pos()
    return jnp.stack([c] + [2 * cx + cy for cx, cy in chips] + [2 * x + y]).astype(jnp.int32)


def _sibling():
    return (lax.axis_index("x"), lax.axis_index("y"), 1 - lax.axis_index("c"))


def _swap_with_sibling(src, recv, send_sem, recv_sem, step):
    slot = step % 2
    cp = pltpu.make_async_remote_copy(src_ref=src, dst_ref=recv.at[slot], send_sem=send_sem.at[slot],
                                      recv_sem=recv_sem.at[slot], device_id=_sibling(), device_id_type=MESH)
    cp.start()
    cp.wait_recv()
    return cp, slot


def _swap_scratch(br, C, dtype):
    return [pltpu.VMEM((2, br, C), dtype), pltpu.SemaphoreType.DMA((2,)), pltpu.SemaphoreType.DMA((2,))]


def _swap_rows(Rh, C, itemsize):
    return _tile(Rh, max(16, (2 << 20) // (C * itemsize)), 16)


def cast_to_slot(a3, me1, name):
    L, R, C = a3.shape
    br = _tile(R, max(8, (1 << 20) // C), 16)

    def body(me_ref, a_ref, o_ref):
        o_ref[...] = a_ref[...].astype(BF16)

    return _call(body, name=name, grid=(L, R // br), prefetch=1,
                 in_specs=[pl.BlockSpec((None, br, C), lambda l, r, me: (l, r, 0))],
                 out_specs=pl.BlockSpec((None, None, br, C), lambda l, r, me: (l, me[0], r, 0)),
                 out_shape=jax.ShapeDtypeStruct((L, 4, R, C), BF16))(me1, a3)


def all_gather_chips(big, small, *, name):
    nb, ns = len(big), len(small)
    layers = [(t, l) for t in range(nb) for l in range(big[t].shape[0])]
    n_big = 3 * len(layers)
    n_rem = n_big + 3 * ns

    def body(*refs):
        small_in = refs[nb:nb + ns]
        bufs, small_out = refs[nb + ns:2 * nb + ns], refs[2 * nb + ns:2 * (nb + ns)]
        send, recv, loc = refs[2 * (nb + ns):]
        x, y, c, chips = _mesh_pos()
        me = 2 * x + y
        ids = [2 * cx + cy for cx, cy in chips]
        started, sends = [], []
        for t in range(ns):
            cp = pltpu.make_async_copy(small_in[t], small_out[t].at[me], loc.at[t])
            cp.start()
            started.append(cp)
        for q, (t, l) in enumerate(layers):
            for k, chip in enumerate(chips):
                blk = bufs[t].at[l, me, c]
                cp = _remote(blk, blk, send, recv, 3 * q + k, (*chip, c))
                cp.start()
                sends.append(cp)
        for t in range(ns):
            for k, chip in enumerate(chips):
                cp = _remote(small_in[t], small_out[t].at[me], send, recv, n_big + 3 * t + k, (*chip, c))
                cp.start()
                sends.append(cp)
        for q, (t, l) in enumerate(layers):
            for k in range(3):
                blk = bufs[t].at[l, ids[k], c]
                _remote(blk, blk, send, recv, 3 * q + k, (x, y, c)).wait_recv()
        for t in range(ns):
            for k in range(3):
                blk = small_out[t].at[ids[k]]
                _remote(blk, blk, send, recv, n_big + 3 * t + k, (x, y, c)).wait_recv()
        for cp in sends:
            cp.wait_send()
        for cp in started:
            cp.wait()

    out_shape = [jax.ShapeDtypeStruct(a.shape, a.dtype) for a in big]
    out_shape += [jax.ShapeDtypeStruct((4,) + a.shape, a.dtype) for a in small]
    return _call(body, name=name, in_specs=[ANY] * (nb + ns), out_specs=[ANY] * (nb + ns), out_shape=out_shape,
                 aliases={t: t for t in range(nb)},
                 scratch=[pltpu.SemaphoreType.DMA((n_rem,)), pltpu.SemaphoreType.DMA((n_rem,)),
                          pltpu.SemaphoreType.DMA((max(ns, 1),))])(*big, *small)


def all_gather_pair(buf, ids, *, name):
    L, _, _, Rh, C = buf.shape
    br = _swap_rows(Rh, C, 2)
    n_r = Rh // br

    def body(ids_ref, in_ref, o_ref, recv, ssem, rsem):
        step = (pl.program_id(0) * 3 + pl.program_id(1)) * n_r + pl.program_id(2)
        cp, slot = _swap_with_sibling(in_ref, recv, ssem, rsem, step)
        o_ref[...] = recv[slot]
        cp.wait_send()

    at = lambda l, s, h, r: (((l * 4 + s) * 2 + h) * n_r + r, 0)
    out = _call(body, name=name, grid=(L, 3, n_r), prefetch=1,
                in_specs=[pl.BlockSpec((br, C), lambda l, k, r, ids: at(l, ids[1 + k], ids[0], r))],
                out_specs=pl.BlockSpec((br, C), lambda l, k, r, ids: at(l, ids[1 + k], 1 - ids[0], r)),
                out_shape=jax.ShapeDtypeStruct((L * 8 * Rh, C), buf.dtype), aliases={1: 0},
                scratch=_swap_scratch(br, C, BF16))(ids, buf.reshape(L * 8 * Rh, C))
    return out.reshape(buf.shape)


def rs_pair_add(grad, ids, *, name):
    _, _, Rh, C = grad.shape
    br = _swap_rows(Rh, C, 2)
    n_r = Rh // br

    def body(ids_ref, send_ref, keep_ref, pb_ref, own_ref, recv, ssem, rsem):
        ph = pl.program_id(0)
        cp, slot = _swap_with_sibling(send_ref, recv, ssem, rsem, ph * n_r + pl.program_id(1))
        s = keep_ref[...].astype(F32) + recv[slot].astype(F32)

        @pl.when(ph == 0)
        def _():
            own_ref[...] = s

        @pl.when(ph > 0)
        def _():
            pb_ref[...] = s.astype(BF16)

        cp.wait_send()

    rel = lambda ph: (ph + 3) % 4
    at = lambda s, h, r: ((s * 2 + h) * n_r + r, 0)
    grad = grad.reshape(8 * Rh, C)
    return _call(
        body, name=name, grid=(4, n_r), prefetch=1,
        in_specs=[pl.BlockSpec((br, C), lambda ph, r, ids: at(ids[1 + rel(ph)], 1 - ids[0], r)),
                  pl.BlockSpec((br, C), lambda ph, r, ids: at(ids[1 + rel(ph)], ids[0], r))],
        out_specs=[pl.BlockSpec((None, br, C), lambda ph, r, ids: (jnp.maximum(ph - 1, 0), jnp.where(ph == 0, 0, r), 0)),
                   pl.BlockSpec((br, C), lambda ph, r, ids: (jnp.where(ph == 0, r, n_r - 1), 0))],
        out_shape=[jax.ShapeDtypeStruct((3, Rh, C), BF16), jax.ShapeDtypeStruct((Rh, C), F32)],
        scratch=_swap_scratch(br, C, BF16))(ids, grad, grad)


def rs_exchange_chips(parts, *, name):
    n = len(parts)

    def body(*refs):
        ins, outs = refs[:n], refs[n:2 * n]
        send, recv = refs[2 * n:]
        x, y, c, chips = _mesh_pos()
        cps = []
        for t in range(n):
            for k, chip in enumerate(chips):
                cp = _remote(ins[t].at[k], outs[t].at[k], send, recv, 3 * t + k, (*chip, c))
                cp.start()
                cps.append(cp)
        for cp in cps:
            cp.wait()

    return _call(body, name=name, in_specs=[ANY] * n, out_specs=[ANY] * n,
                 out_shape=[jax.ShapeDtypeStruct(p.shape, p.dtype) for p in parts],
                 scratch=[pltpu.SemaphoreType.DMA((3 * n,)), pltpu.SemaphoreType.DMA((3 * n,))])(*parts)


def rs_finish(owns, gots, *, name):
    L = len(owns)
    Rh, C = owns[0].shape
    br = _swap_rows(Rh, C, 4)
    n_r = Rh // br

    def body(*refs):
        own_refs, got_refs, o_ref = refs[:L], refs[L:2 * L], refs[2 * L]
        recv, ssem, rsem = refs[2 * L + 1:]
        l = pl.program_id(0)
        c = lax.axis_index("c")
        for ll in range(L):
            @pl.when(l == ll)
            def _():
                s = own_refs[ll][...]
                for k in range(3):
                    s = s + got_refs[ll][k].astype(F32)
                o_ref[c] = s

        cp, slot = _swap_with_sibling(o_ref.at[c], recv, ssem, rsem, l * n_r + pl.program_id(1))
        o_ref[1 - c] = recv[slot]
        cp.wait_send()

    def at_layer(ll):
        return lambda l, r: jnp.where(l == ll, r, jnp.where(l < ll, 0, n_r - 1))

    in_specs = [pl.BlockSpec((br, C), lambda l, r, ll=ll: (at_layer(ll)(l, r), 0)) for ll in range(L)]
    in_specs += [pl.BlockSpec((3, br, C), lambda l, r, ll=ll: (0, at_layer(ll)(l, r), 0)) for ll in range(L)]
    out = _call(body, name=name, grid=(L, n_r), in_specs=in_specs,
                out_specs=pl.BlockSpec((2, br, C), lambda l, r: (l, r, 0)),
                out_shape=jax.ShapeDtypeStruct((L * 2, Rh, C), F32),
                scratch=_swap_scratch(br, C, F32))(*owns, *gots)
    return out.reshape(L, 2, Rh, C)


def all_reduce_small(buf, *, name):
    rows = buf.shape[0]

    def body(x_ref, o_ref, rbuf, send, recv):
        x, y, c, _ = _mesh_pos()
        o_ref[...] = x_ref[...]
        for k, dev in enumerate(((x, y, 1 - c), (1 - x, y, c), (x, 1 - y, c))):
            cp = _remote(o_ref, rbuf.at[k], send, recv, k, dev)
            cp.start()
            cp.wait()
            o_ref[...] = o_ref[...] + rbuf[k]

    return _call(body, name=name, in_specs=[VMEM_SPEC], out_specs=VMEM_SPEC,
                 out_shape=jax.ShapeDtypeStruct(buf.shape, F32),
                 scratch=[pltpu.VMEM((3, rows, LANES), F32), pltpu.SemaphoreType.DMA((3,)),
                          pltpu.SemaphoreType.DMA((3,))])(buf)


def _pack(arrs):
    parts = []
    for a in arrs:
        f = a.reshape(-1).astype(F32)
        parts.append(jnp.pad(f, (0, (-f.shape[0]) % PACK_ALIGN)))
    return jnp.concatenate(parts).reshape(-1, LANES)


def _unpack(buf, shapes):
    flat = buf.reshape(-1)
    out, off = [], 0
    for s in shapes:
        n = math.prod(s)
        out.append(flat[off:off + n].reshape(s))
        off += n + (-n) % PACK_ALIGN
    return out


_WEIGHTS = ['ev_w_in', 'ev_ln_v_g', 'ev_ln_v_b', 'ev_w_s', 'ev_b_s', 'ev_w_pool', 'ev_pool_scale', 'ev_w_out',
            'od_w_in', 'od_norm_g', 'od_w_out', 'lb_param', 'ffn_w_up', 'ffn_conv_w', 'ffn_conv_b', 'ffn_w_down',
            'ln1_g', 'ln1_b', 'ln2_g', 'ln2_b']
_BIG = ['ev_w_in', 'ev_w_out', 'od_w_in', 'od_w_out', 'ffn_w_up', 'ffn_w_down']
_SMALL = [n for n in _WEIGHTS if n not in _BIG]


def kernel(x, ev_w_in, ev_ln_v_g, ev_ln_v_b, ev_w_s, ev_b_s, ev_w_pool, ev_pool_scale, ev_w_out, od_w_in, od_norm_g, od_w_out, lb_param, ffn_w_up, ffn_conv_w, ffn_conv_b, ffn_w_down, ln1_g, ln1_b, ln2_g, ln2_b, loss_target, m_ev_w_in, m_ev_ln_v_g, m_ev_ln_v_b, m_ev_w_s, m_ev_b_s, m_ev_w_pool, m_ev_pool_scale, m_ev_w_out, m_od_w_in, m_od_norm_g, m_od_w_out, m_lb_param, m_ffn_w_up, m_ffn_conv_w, m_ffn_conv_b, m_ffn_w_down, m_ln1_g, m_ln1_b, m_ln2_g, m_ln2_b, v_ev_w_in, v_ev_ln_v_g, v_ev_ln_v_b, v_ev_w_s, v_ev_b_s, v_ev_w_pool, v_ev_pool_scale, v_ev_w_out, v_od_w_in, v_od_norm_g, v_od_w_out, v_lb_param, v_ffn_w_up, v_ffn_conv_w, v_ffn_conv_b, v_ffn_w_down, v_ln1_g, v_ln1_b, v_ln2_g, v_ln2_b):
    given = dict(locals())
    w = {n: given[n] for n in _WEIGHTS}
    mom = {n: given["m_" + n] for n in _WEIGHTS}
    vel = {n: given["v_" + n] for n in _WEIGHTS}
    x2d = x[0]
    tgt = loss_target[0]
    T, D = x2d.shape
    DA = ev_ln_v_g.shape[-1]
    DB = ev_pool_scale.shape[-1]
    HA = ev_w_s.shape[1]
    G = len(B_WINDOWS)
    CG = DB // G
    DC = 4 * od_norm_g.shape[-1]
    F = ffn_conv_b.shape[-1] // 2
    chip = 2 * lax.axis_index("x") + lax.axis_index("y")

    ids = mesh_ids()
    halves = lambda a: a.reshape(a.shape[0], 4, 2, a.shape[2] // 2, a.shape[3])
    slots = [halves(cast_to_slot(w[n], ids[4:5], "cast_" + n)) for n in _BIG]
    gathered = all_gather_chips(slots, [ev_w_pool[0], ffn_conv_w, od_norm_g], name="all_gather_chips")
    paired = [all_gather_pair(g, ids, name="all_gather_pair_" + n) for n, g in zip(_BIG, gathered[:6])]
    wg = {n: g.reshape(g.shape[0], 4, g.shape[3] * 2, g.shape[4]) for n, g in zip(_BIG, paired)}
    wpool_full = gathered[6].transpose(1, 0, 2, 3).reshape(G, CG, CG)
    cw_full = gathered[7].transpose(1, 2, 0, 3).reshape(DEPTH, 3, 2 * F)
    gn_full = gathered[8].reshape(1, DC)
    win0 = wg['ev_w_in'][0]
    wout0 = wg['ev_w_out'][0].reshape(DA + DB, D)
    win1 = wg['od_w_in'][0]
    wout1 = wg['od_w_out'][0].reshape(DC, D)
    wup = wg['ffn_w_up']
    wdn = wg['ffn_w_down'].reshape(DEPTH, F, D)
    cb3 = ffn_conv_b.reshape(DEPTH, 1, 2 * F)
    ws = ev_w_s[0]
    wsT = jnp.swapaxes(ws, 1, 2)
    bsT = ev_b_s[0].T
    wpb = wpool_full.astype(BF16)
    ones = jnp.ones((1, D), F32)
    zeros = jnp.zeros((1, D), F32)
    row = lambda a, l: a[l:l + 1]

    Ns0 = win0.shape[-1]
    Nu = wup.shape[-1]
    tm_big = _tile(T, 1024, 8)
    tm_ln = _tile(T, 512, 8)
    tk_ln = _tile(D, 512)
    n_p = DC // HP

    def nat_spec(Ns, tnw):
        nps = Ns // tnw
        return pl.BlockSpec((1, D, tnw), lambda i, j: (j // nps, 0, j % nps))

    perm_spec = pl.BlockSpec((4, D, HP), lambda i, j: (0, 0, j))

    xb16 = cast_bf16(x, "cast_x")[0]
    h0 = mm_nn(xb16, win0, w_spec=nat_spec(Ns0, Ns0), P=1, tnw=Ns0, tm=tm_big, n_j=4, name="ev_in")
    cat = gating_fwd(h0, ev_ln_v_g, ev_ln_v_b, ws, bsT, wpb, ev_pool_scale, name="gating_fwd")

    def mix_ln(a, wmat, res, l, name):
        K = a.shape[1]
        tk = _tile(K, 512)
        return mm_ln(a, wmat, *res, row(ln1_g, l), row(ln1_b, l), w_spec=pl.BlockSpec((tk, D), lambda i, k: (k, 0)),
                     K=K, tk=tk, tm=tm_ln, name=name)

    def ffn_fwd(y_in, res, l):
        h, f = ffn_up(y_in, wup, cw_full, cb3, l, name=f"ffn_up{l}")
        tk = _tile(F, 512)
        out = mm_ln(f, wdn, *res, row(ln2_g, l), row(ln2_b, l),
                    w_spec=pl.BlockSpec((None, tk, D), lambda i, k: (l, k, 0)), K=F, tk=tk, tm=tm_ln,
                    name=f"ffn_down{l}")
        return h, f, out

    xh1, y1, rs1 = mix_ln(cat, wout0, (x2d, ones, zeros), 0, "ev_out")
    res1 = (xh1, row(ln1_g, 0), row(ln1_b, 0))
    hf0, f0, (xh2, y2, rs2) = ffn_fwd(y1, res1, 0)
    res2 = (xh2, row(ln2_g, 0), row(ln2_b, 0))
    h1p = mm_nn(y2, win1, w_spec=perm_spec, P=4, tnw=HP, tm=tm_big, n_j=n_p, name="od_in")
    yh, o_saved, sp = hgrn_fwd(h1p, lb_param, gn_full, name="hgrn_fwd")
    xh3, y3, rs3 = mix_ln(yh, wout1, res2, 1, "od_out")
    res3 = (xh3, row(ln1_g, 1), row(ln1_b, 1))
    hf1, f1, (xh4, y4, rs4) = ffn_fwd(y3, res3, 1)
    loss_p, dr, drb, dg_ln2_1, db_ln2_1 = loss_bwd(xh4, rs4, row(ln2_g, 1), row(ln2_b, 1), tgt, name="loss_bwd")

    tt = _tile(T, 512, 16)
    n_t = T // tt
    tnu = Nu // 2 if (Nu // 2) % LANES == 0 else Nu
    upb = Nu // tnu
    tkd = _tile(D, 1024)

    def ffn_bwd(l, dr2, dr2b, f, hf, y_in, xh_in, rs_in):
        tkf = F // 4
        g_dn = mm_tn(f, dr2b, a_spec=pl.BlockSpec((tt, tkf), lambda kb, nb, t: (t, kb)),
                     g_spec=pl.BlockSpec((tt, D), lambda kb, nb, t: (t, 0)),
                     o_spec=pl.BlockSpec((1, tkf, D), lambda kb, nb, t: (0, kb, 0)), out_shape=(1, F, D),
                     grid=(4, 1, n_t), acc_shape=(tkf, D), P=1, tnw=D, name=f"g_ffn_down{l}")
        dh, dcw, dcb = ffn_dgate(dr2b, wdn, hf, cw_full, cb3, l, name=f"ffn_dgate{l}")
        g_up = mm_tn(y_in, dh, a_spec=pl.BlockSpec((tt, tkd), lambda kb, nb, t: (t, kb)),
                     g_spec=pl.BlockSpec((None, tt, tnu), lambda kb, nb, t: (nb // (2 * upb), t, nb % (2 * upb))),
                     o_spec=pl.BlockSpec((1, tkd, tnu), lambda kb, nb, t: (nb // upb, kb, nb % upb)),
                     out_shape=(4, D, Nu), grid=(D // tkd, 4 * upb, n_t), acc_shape=(tkd, tnu), P=1, tnw=tnu,
                     name=f"g_ffn_up{l}")
        tku = _tile(Nu, 256)
        kps = Nu // tku
        out = mm_nt_res(dh, wup, dr2, (xh_in, rs_in, row(ln1_g, l)),
                        a_spec=pl.BlockSpec((None, tm_ln, tku), lambda i, k: (k // (2 * kps), i, k % (2 * kps))),
                        w_spec=pl.BlockSpec((None, 1, D, tku), lambda i, k: (l, k // kps, 0, k % kps)),
                        P=1, tnw=tku, n_k=4 * kps, tm=tm_ln, name=f"d_ffn_in{l}")
        return g_dn, g_up, dcw, dcb, out

    def g_out(a, gb, name):
        K = a.shape[1]
        tkk = _tile(K, 1024)
        return mm_tn(a, gb, a_spec=pl.BlockSpec((tt, tkk), lambda kb, nb, t: (t, kb)),
                     g_spec=pl.BlockSpec((tt, tkd), lambda kb, nb, t: (t, nb)),
                     o_spec=pl.BlockSpec((1, tkk, tkd), lambda kb, nb, t: (0, kb, nb)), out_shape=(1, K, D),
                     grid=(K // tkk, D // tkd, n_t), acc_shape=(tkk, tkd), P=1, tnw=tkd, name=name)

    g_dn1, g_up1, dcw1, dcb1, (dr1, dr1b, dg_ln1_1, db_ln1_1) = ffn_bwd(1, dr, drb, f1, hf1, y3, xh3, rs3)
    g_wout1 = g_out(yh, dr1b, "g_od_out")
    dyh = mm_nt_plain(dr1b, wout1, tm=tm_big, tn=_tile(DC, 512), name="d_od_out")
    dh1p, d_gn, d_lbp = hgrn_bwd(h1p, o_saved, dyh, sp, lb_param, gn_full, name="hgrn_bwd")
    g_win1 = mm_tn(y2, dh1p, a_spec=pl.BlockSpec((tt, tkd), lambda kb, nb, t: (t, kb)),
                   g_spec=pl.BlockSpec((tt, 4 * HP), lambda kb, nb, t: (t, nb)),
                   o_spec=pl.BlockSpec((4, tkd, HP), lambda kb, nb, t: (0, kb, nb)), out_shape=(4, D, DC),
                   grid=(D // tkd, n_p, n_t), acc_shape=(tkd, 4 * HP), P=4, tnw=HP, name="g_od_in")
    dr, drb, dg_ln2_0, db_ln2_0 = mm_nt_res(
        dh1p, win1, dr1, (xh2, rs2, row(ln2_g, 0)), a_spec=pl.BlockSpec((tm_ln, 4 * HP), lambda i, k: (i, k)),
        w_spec=pl.BlockSpec((4, D, HP), lambda i, k: (0, 0, k)), P=4, tnw=HP, n_k=n_p, tm=tm_ln, name="d_od_in")
    g_dn0, g_up0, dcw0, dcb0, (dr1, dr1b, dg_ln1_0, db_ln1_0) = ffn_bwd(0, dr, drb, f0, hf0, y1, xh1, rs1)
    g_wout0 = g_out(cat, dr1b, "g_ev_out")
    dcat = mm_nt_plain(dr1b, wout0, tm=tm_big, tn=_tile(DA + DB, 512), name="d_ev_out")
    dh0, d_ws, d_bsT, d_lg, d_lb, d_sc, d_wp = gating_bwd(h0, dcat, ev_ln_v_g, ev_ln_v_b, ws, wsT, bsT, wpb,
                                                          ev_pool_scale, name="gating_bwd")
    g_win0 = mm_tn(xb16, dh0, a_spec=pl.BlockSpec((tt, tkd), lambda kb, nb, t: (t, kb)),
                   g_spec=pl.BlockSpec((tt, Ns0), lambda kb, nb, t: (t, nb)),
                   o_spec=pl.BlockSpec((1, tkd, Ns0), lambda kb, nb, t: (nb, kb, 0)), out_shape=(4, D, Ns0),
                   grid=(D // tkd, 4, n_t), acc_shape=(tkd, Ns0), P=1, tnw=Ns0, name="g_ev_in")
    grad_x = mm_nt_res(dh0, win0, dr1, None, a_spec=pl.BlockSpec((tm_ln, Ns0), lambda i, k: (i, k)),
                       w_spec=pl.BlockSpec((1, D, Ns0), lambda i, k: (k, 0, 0)), P=1, tnw=Ns0, n_k=4, tm=tm_ln,
                       name="d_ev_in")

    big_grads = [g_win0, g_wout0.reshape(4, (DA + DB) // 4, D), g_win1, g_wout1.reshape(4, DC // 4, D),
                 g_up0, g_up1, g_dn0.reshape(4, F // 4, D), g_dn1.reshape(4, F // 4, D)]
    groups = [[0], [1], [2], [3], [4, 5], [6, 7]]
    split = [g.reshape(4, 2, g.shape[1] // 2, g.shape[2]) for g in big_grads]
    summed = [rs_pair_add(g, ids, name=f"rs_pair_add{t}") for t, g in enumerate(split)]
    got = rs_exchange_chips([s[0] for s in summed], name="rs_exchange_chips")
    shared = [rs_finish([summed[t][1] for t in m], [got[t] for t in m], name="rs_finish_" + n)
              for n, m in zip(_BIG, groups)]
    big_g = {n: s.reshape(w[n].shape) for n, s in zip(_BIG, shared)}

    small_full = {
        'ev_ln_v_g': d_lg, 'ev_ln_v_b': d_lb, 'ev_w_s': d_ws[None], 'ev_b_s': d_bsT.T[None], 'ev_w_pool': d_wp[None],
        'ev_pool_scale': d_sc, 'od_norm_g': d_gn, 'lb_param': d_lbp,
        'ffn_conv_w': jnp.stack([jnp.concatenate([dcw0[0], dcw0[1]], axis=-1),
                                 jnp.concatenate([dcw1[0], dcw1[1]], axis=-1)]),
        'ffn_conv_b': jnp.stack([jnp.concatenate([dcb0[0, 0], dcb0[1, 0]]), jnp.concatenate([dcb1[0, 0], dcb1[1, 0]])]),
        'ln1_g': jnp.concatenate([dg_ln1_0, dg_ln1_1]), 'ln1_b': jnp.concatenate([db_ln1_0, db_ln1_1]),
        'ln2_g': jnp.concatenate([dg_ln2_0, dg_ln2_1]), 'ln2_b': jnp.concatenate([db_ln2_0, db_ln2_1])}
    packed = _pack([small_full[n] for n in _SMALL] + [loss_p[0, 0:1]])
    reduced = _unpack(all_reduce_small(packed, name="all_reduce_small"),
                      [small_full[n].shape for n in _SMALL] + [(1,)])
    small_g = dict(zip(_SMALL, reduced[:-1]))
    loss = reduced[-1][0]
    small_g['ev_w_pool'] = lax.dynamic_slice_in_dim(small_g['ev_w_pool'], chip * (CG // 4), CG // 4, axis=2)
    small_g['ffn_conv_w'] = lax.dynamic_slice_in_dim(small_g['ffn_conv_w'], chip * (F // 2), F // 2, axis=2)
    small_g['od_norm_g'] = lax.dynamic_slice_in_dim(small_g['od_norm_g'], chip * (DC // 4), DC // 4, axis=1)

    grads = {**big_g, **small_g}
    delta, new_m, new_v = {}, {}, {}
    for n in _BIG:
        delta[n], new_m[n], new_v[n] = adamw(w[n], big_g[n], mom[n], vel[n], "adamw_" + n)
    ps = [_pack([d[n] for n in _SMALL]) for d in (w, small_g, mom, vel)]
    upd = adamw(*[p[None] for p in ps], "adamw_small")
    shapes = [w[n].shape for n in _SMALL]
    for d, buf in zip((delta, new_m, new_v), upd):
        d.update(zip(_SMALL, _unpack(buf[0], shapes)))

    return (loss, grad_x[None], *[grads[n] for n in _WEIGHTS], *[delta[n] for n in _WEIGHTS],
            *[new_m[n] for n in _WEIGHTS], *[new_v[n] for n in _WEIGHTS])
```

```python
import math

import jax
import jax.numpy as jnp
from jax import lax
from jax.experimental import pallas as pl
from jax.experimental.pallas import tpu as pltpu

F32 = jnp.float32
BF16 = jnp.bfloat16
MESH = pl.DeviceIdType.MESH
ANY = pl.BlockSpec(memory_space=pl.ANY)
VMEM_SPEC = pl.BlockSpec(memory_space=pltpu.VMEM)

DEPTH = 2
ALPHA = (2 * DEPTH) ** 0.25
LN_EPS = 1e-5
A_HEAD = 128
A_CHUNK = 128
B_WINDOWS = (2, 4, 8, 16)
POOL_HALO = 16
C_HEAD = 128
C_CHUNK = 64
CONV_HALO = 8
ADAM_LR = 0.001
ADAM_B1 = 0.9
ADAM_B2 = 0.999
ADAM_EPS = 1e-08
ADAM_WD = 0.01
ADAM_STEP = 10
V7X_VMEM_LIMIT_BYTES = 56 * 1024 * 1024
LANES = 128
PACK_ALIGN = 8 * LANES


class IciCopy:
    def __init__(self, kind, arr):
        self.kind, self.arr, self.out = kind, arr, None


def _carried_copies(items, in_refs, out_refs, send, recv):
    x, y, c, chips = _mesh_pos()
    me = 2 * x + y
    sends, lands = [], []
    for q, (it, src, dst) in enumerate(zip(items, in_refs, out_refs)):
        for k, (cx, cy) in enumerate(chips):
            if it.kind == "gather":
                mine, theirs = dst.at[0, me, c], dst.at[0, 2 * cx + cy, c]
                sends.append(_remote(mine, mine, send, recv, 3 * q + k, (cx, cy, c)))
            else:
                theirs = dst.at[k]
                sends.append(_remote(src.at[k], theirs, send, recv, 3 * q + k, (cx, cy, c)))
            lands.append(_remote(theirs, theirs, send, recv, 3 * q + k, (x, y, c)))
    return sends, lands


def _call(body, *, name, out_shape, grid=(), in_specs=None, out_specs=None, scratch=(), prefetch=0, aliases=None,
          carry=()):
    single = not isinstance(out_specs, (list, tuple))
    in_specs = list(in_specs)
    out_specs = [out_specs] if single else list(out_specs)
    out_shape = [out_shape] if single else list(out_shape)
    scratch = list(scratch)
    aliases = dict(aliases or {})
    n_in, n_out, n_sc, n_c = len(in_specs), len(out_specs), len(scratch), len(carry)
    inner = body
    if n_c:
        assert grid, "a carrier needs a grid"
        for q, it in enumerate(carry):
            if it.kind == "gather":
                aliases[prefetch + n_in + q] = n_out + q
        in_specs += [ANY] * n_c
        out_specs += [ANY] * n_c
        out_shape += [jax.ShapeDtypeStruct(it.arr.shape, it.arr.dtype) for it in carry]
        scratch += [pltpu.SemaphoreType.DMA((3 * n_c,)), pltpu.SemaphoreType.DMA((3 * n_c,))]

        def inner(*refs):
            pre, refs = refs[:prefetch], refs[prefetch:]
            ins, c_in = refs[:n_in], refs[n_in:n_in + n_c]
            outs, c_out = refs[n_in + n_c:n_in + n_c + n_out], refs[n_in + n_c + n_out:n_in + 2 * n_c + n_out]
            rest = refs[n_in + 2 * n_c + n_out:]
            first = last = True
            for d, n in enumerate(grid):
                first = jnp.logical_and(first, pl.program_id(d) == 0)
                last = jnp.logical_and(last, pl.program_id(d) == n - 1)

            @pl.when(first)
            def _():
                for cp in _carried_copies(carry, c_in, c_out, rest[-2], rest[-1])[0]:
                    cp.start()

            body(*pre, *ins, *outs, *rest[:n_sc])

            @pl.when(last)
            def _():
                sends, lands = _carried_copies(carry, c_in, c_out, rest[-2], rest[-1])
                for cp in lands:
                    cp.wait_recv()
                for cp in sends:
                    cp.wait_send()

    spec = pltpu.PrefetchScalarGridSpec(num_scalar_prefetch=prefetch, grid=grid, in_specs=in_specs,
                                        out_specs=out_specs, scratch_shapes=scratch)
    fn = pl.pallas_call(inner, name=name, grid_spec=spec, out_shape=out_shape, input_output_aliases=aliases,
                        compiler_params=pltpu.CompilerParams(vmem_limit_bytes=V7X_VMEM_LIMIT_BYTES))

    def run(*args):
        res = fn(*args, *[it.arr for it in carry])
        for it, o in zip(carry, res[n_out:]):
            it.out = o
        return res[0] if single else list(res[:n_out])

    return run


def _tile(n, pref, unit=LANES):
    if n <= pref:
        return n
    t = (pref // unit) * unit
    while t > unit and n % t:
        t -= unit
    assert n % t == 0, (n, pref, unit)
    return t


def _dot(a, b):
    return jnp.dot(a, b, preferred_element_type=F32)


def _dot_nt(a, b):
    return lax.dot_general(a, b, (((1,), (1,)), ((), ())), preferred_element_type=F32)


def _dot_tn(a, b):
    return lax.dot_general(a, b, (((0,), (0,)), ((), ())), preferred_element_type=F32)


def _sigmoid(x):
    return jax.nn.sigmoid(x)


_GELU_C = math.sqrt(2.0 / math.pi)


def _gelu(x):
    return 0.5 * x * (1.0 + jnp.tanh(_GELU_C * (x + 0.044715 * x * x * x)))


def _gelu_grad(x):
    th = jnp.tanh(_GELU_C * (x + 0.044715 * x * x * x))
    return 0.5 * (1.0 + th) + 0.5 * x * (1.0 - th * th) * _GELU_C * (1.0 + 3.0 * 0.044715 * x * x)


def _ln_fwd(r, g, b):
    mu = jnp.mean(r, axis=-1, keepdims=True)
    xc = r - mu
    var = jnp.mean(xc * xc, axis=-1, keepdims=True)
    rstd = lax.rsqrt(var + LN_EPS)
    xh = xc * rstd
    return xh * g + b, xh, rstd


def _ln_bwd(dy, xh, rstd, g):
    dxh = dy * g
    m1 = jnp.mean(dxh, axis=-1, keepdims=True)
    m2 = jnp.mean(dxh * xh, axis=-1, keepdims=True)
    dr = rstd * (dxh - m1 - xh * m2)
    return dr, jnp.sum(dy * xh, axis=0, keepdims=True), jnp.sum(dy, axis=0, keepdims=True)


def _exact_tri_dot(tri, x):
    hi = x.astype(BF16)
    r1 = x - hi.astype(F32)
    mid = r1.astype(BF16)
    lo = (r1 - mid.astype(F32)).astype(BF16)
    return _dot(tri, hi) + _dot(tri, mid) + _dot(tri, lo)


def cast_bf16(a3, name):
    L, R, C = a3.shape
    br = _tile(R, max(8, (1 << 20) // C), 8)

    def body(a_ref, o_ref):
        o_ref[...] = a_ref[...].astype(BF16)

    return _call(body, name=name, grid=(L, R // br),
                 in_specs=[pl.BlockSpec((None, br, C), lambda l, r: (l, r, 0))],
                 out_specs=pl.BlockSpec((None, br, C), lambda l, r: (l, r, 0)),
                 out_shape=jax.ShapeDtypeStruct((L, R, C), BF16))(a3)


def adamw(w, g, m, v, name):
    L, R, C = w.shape
    br = _tile(R, max(8, (1 << 19) // C), 8)
    c1 = 1.0 - ADAM_B1 ** ADAM_STEP
    c2 = 1.0 - ADAM_B2 ** ADAM_STEP

    def body(w_ref, g_ref, m_ref, v_ref, d_ref, nm_ref, nv_ref):
        gg = g_ref[...]
        nm = ADAM_B1 * m_ref[...] + (1.0 - ADAM_B1) * gg
        nv = ADAM_B2 * v_ref[...] + (1.0 - ADAM_B2) * (gg * gg)
        d_ref[...] = -ADAM_LR * ((nm / c1) / (jnp.sqrt(nv / c2) + ADAM_EPS) + ADAM_WD * w_ref[...])
        nm_ref[...] = nm
        nv_ref[...] = nv

    spec = pl.BlockSpec((None, br, C), lambda l, r: (l, r, 0))
    sds = jax.ShapeDtypeStruct((L, R, C), F32)
    return _call(body, name=name, grid=(L, R // br), in_specs=[spec] * 4, out_specs=[spec] * 3,
                 out_shape=[sds] * 3)(w, g, m, v)


def mm_nn(a, w, *, w_spec, P, tnw, tm, n_j, name):
    T, K = a.shape
    bw = P * tnw

    def body(a_ref, w_ref, o_ref):
        av = a_ref[...]
        for p in range(P):
            o_ref[:, p * tnw:(p + 1) * tnw] = _dot(av, w_ref[p]).astype(BF16)

    return _call(body, name=name, grid=(T // tm, n_j),
                 in_specs=[pl.BlockSpec((tm, K), lambda i, j: (i, 0)), w_spec],
                 out_specs=pl.BlockSpec((tm, bw), lambda i, j: (i, j)),
                 out_shape=jax.ShapeDtypeStruct((T, n_j * bw), BF16))(a, w)


def mm_ln(a, w, res, rg, rb, g, b, *, w_spec, K, tk, tm, name, carry=()):
    T, N = res.shape
    n_k = K // tk

    def body(a_ref, w_ref, res_ref, rg_ref, rb_ref, g_ref, b_ref, xh_ref, y_ref, rs_ref, acc):
        k = pl.program_id(1)

        @pl.when(k == 0)
        def _():
            acc[...] = jnp.zeros_like(acc)

        acc[...] += _dot(a_ref[...], w_ref[...])

        @pl.when(k == n_k - 1)
        def _():
            r = ALPHA * (res_ref[...] * rg_ref[...] + rb_ref[...]) + acc[...]
            y, xh, rstd = _ln_fwd(r, g_ref[...], b_ref[...])
            xh_ref[...] = xh
            y_ref[...] = y.astype(BF16)
            rs_ref[...] = rstd

    row = pl.BlockSpec((tm, N), lambda i, k: (i, 0))
    vec = pl.BlockSpec((1, N), lambda i, k: (0, 0))
    return _call(body, name=name, grid=(T // tm, n_k),
                 in_specs=[pl.BlockSpec((tm, tk), lambda i, k: (i, k)), w_spec, row, vec, vec, vec, vec],
                 out_specs=[row, row, pl.BlockSpec((tm, 1), lambda i, k: (i, 0))],
                 out_shape=[jax.ShapeDtypeStruct((T, N), F32), jax.ShapeDtypeStruct((T, N), BF16),
                            jax.ShapeDtypeStruct((T, 1), F32)],
                 scratch=[pltpu.VMEM((tm, N), F32)], carry=carry)(a, w, res, rg, rb, g, b)


def mm_nt_plain(a, w, *, tm, tn, name):
    T, K = a.shape
    N = w.shape[0]

    def body(a_ref, w_ref, o_ref):
        o_ref[...] = _dot_nt(a_ref[...], w_ref[...]).astype(BF16)

    return _call(body, name=name, grid=(T // tm, N // tn),
                 in_specs=[pl.BlockSpec((tm, K), lambda i, j: (i, 0)), pl.BlockSpec((tn, K), lambda i, j: (j, 0))],
                 out_specs=pl.BlockSpec((tm, tn), lambda i, j: (i, j)),
                 out_shape=jax.ShapeDtypeStruct((T, N), BF16))(a, w)


def mm_nt_res(a, w, res, ln, *, a_spec, w_spec, P, tnw, n_k, tm, name, carry=()):
    T, N = res.shape
    n_i = T // tm

    def body(*refs):
        if ln is None:
            a_ref, w_ref, res_ref, o_ref, acc = refs
        else:
            a_ref, w_ref, res_ref, xh_ref, rs_ref, g_ref, dr_ref, drb_ref, dg_ref, db_ref, acc = refs
        i = pl.program_id(0)
        k = pl.program_id(1)

        @pl.when(k == 0)
        def _():
            acc[...] = jnp.zeros_like(acc)

        s = acc[...]
        for p in range(P):
            s = s + _dot_nt(a_ref[:, p * tnw:(p + 1) * tnw], w_ref[p])
        acc[...] = s

        @pl.when(k == n_k - 1)
        def _():
            d = ALPHA * res_ref[...] + acc[...]
            if ln is None:
                o_ref[...] = d
            else:
                dr, dg, db = _ln_bwd(d, xh_ref[...], rs_ref[...], g_ref[...])
                dr_ref[...] = dr
                drb_ref[...] = dr.astype(BF16)

                @pl.when(i == 0)
                def _():
                    dg_ref[...] = jnp.zeros_like(dg_ref)
                    db_ref[...] = jnp.zeros_like(db_ref)

                dg_ref[...] += dg
                db_ref[...] += db

    row = pl.BlockSpec((tm, N), lambda i, k: (i, 0))
    vec = pl.BlockSpec((1, N), lambda i, k: (0, 0))
    scratch = [pltpu.VMEM((tm, N), F32)]
    if ln is None:
        return _call(body, name=name, grid=(n_i, n_k), in_specs=[a_spec, w_spec, row], out_specs=row,
                     out_shape=jax.ShapeDtypeStruct((T, N), F32), scratch=scratch, carry=carry)(a, w, res)
    xh, rstd, g = ln
    return _call(body, name=name, grid=(n_i, n_k),
                 in_specs=[a_spec, w_spec, row, row, pl.BlockSpec((tm, 1), lambda i, k: (i, 0)), vec],
                 out_specs=[row, row, vec, vec],
                 out_shape=[jax.ShapeDtypeStruct((T, N), F32), jax.ShapeDtypeStruct((T, N), BF16),
                            jax.ShapeDtypeStruct((1, N), F32), jax.ShapeDtypeStruct((1, N), F32)],
                 scratch=scratch, carry=carry)(a, w, res, xh, rstd, g)


def mm_tn(a, g, *, a_spec, g_spec, o_spec, out_shape, grid, acc_shape, P, tnw, name):
    n_t = grid[2]

    def body(a_ref, g_ref, o_ref, acc):
        t = pl.program_id(2)

        @pl.when(t == 0)
        def _():
            acc[...] = jnp.zeros_like(acc)

        acc[...] += _dot_tn(a_ref[...], g_ref[...])

        @pl.when(t == n_t - 1)
        def _():
            for p in range(P):
                o_ref[p] = acc[:, p * tnw:(p + 1) * tnw].astype(BF16)

    return _call(body, name=name, grid=grid, in_specs=[a_spec, g_spec], out_specs=o_spec,
                 out_shape=jax.ShapeDtypeStruct(out_shape, BF16),
                 scratch=[pltpu.VMEM(acc_shape, F32)])(a, g)


def _causal_conv(ext, halo, w, b):
    s1 = pltpu.roll(ext, 1, 0)[halo:]
    s2 = pltpu.roll(ext, 2, 0)[halo:]
    return b + w[2:3] * ext[halo:] + w[1:2] * s1 + w[0:1] * s2, s1, s2


def ffn_up(xb, wup, cw, cb, l, *, name, carry=()):
    T, D = xb.shape
    Ns = wup.shape[-1]
    F = 2 * Ns
    tn = _tile(Ns, 256)
    nps = Ns // tn
    n_j = F // tn
    tm = _tile(T, 1024, 8)

    def body(x_ref, wa_ref, wv_ref, cwa_ref, cwv_ref, cba_ref, cbv_ref, h_ref, f_ref, carry):
        i = pl.program_id(1)

        @pl.when(i == 0)
        def _():
            carry[...] = jnp.zeros_like(carry)

        xv = x_ref[...]
        ha = _dot(xv, wa_ref[...])
        hv = _dot(xv, wv_ref[...])
        ca, _, _ = _causal_conv(jnp.concatenate([carry[0], ha], axis=0), CONV_HALO, cwa_ref[...], cba_ref[...])
        cv, _, _ = _causal_conv(jnp.concatenate([carry[1], hv], axis=0), CONV_HALO, cwv_ref[...], cbv_ref[...])
        carry[0] = ha[tm - CONV_HALO:]
        carry[1] = hv[tm - CONV_HALO:]
        h_ref[0] = ha.astype(BF16)
        h_ref[1] = hv.astype(BF16)
        f_ref[...] = (ca * _sigmoid(ca) * cv).astype(BF16)

    wspec_a = pl.BlockSpec((None, None, D, tn), lambda j, i: (0, j // nps, 0, j % nps))
    wspec_v = pl.BlockSpec((None, None, D, tn), lambda j, i: (0, 2 + j // nps, 0, j % nps))
    return _call(
        body, name=name, grid=(n_j, T // tm),
        in_specs=[pl.BlockSpec((tm, D), lambda j, i: (i, 0)), wspec_a, wspec_v,
                  pl.BlockSpec((None, 3, tn), lambda j, i: (l, 0, j)),
                  pl.BlockSpec((None, 3, tn), lambda j, i: (l, 0, n_j + j)),
                  pl.BlockSpec((None, 1, tn), lambda j, i: (l, 0, j)),
                  pl.BlockSpec((None, 1, tn), lambda j, i: (l, 0, n_j + j))],
        out_specs=[pl.BlockSpec((2, tm, tn), lambda j, i: (0, i, j)), pl.BlockSpec((tm, tn), lambda j, i: (i, j))],
        out_shape=[jax.ShapeDtypeStruct((2, T, F), BF16), jax.ShapeDtypeStruct((T, F), BF16)],
        scratch=[pltpu.VMEM((2, CONV_HALO, tn), F32)], carry=carry)(xb, wup, wup, cw, cw, cb, cb)


def ffn_dgate(db16, wdn, h, cw, cb, l, *, name, carry=()):
    T, D = db16.shape
    F = h.shape[-1]
    tn = _tile(F, 512)
    n_j = F // tn
    tm = _tile(T, 512, 16)
    n_i = T // tm
    hb = 16
    n_ext = tm + CONV_HALO

    def body(d_ref, w_ref, h_ref, hh_ref, cwa_ref, cwv_ref, cba_ref, cbv_ref, dh_ref, dcw_ref, dcb_ref, carry):
        ip = pl.program_id(1)
        i = n_i - 1 - ip

        @pl.when(ip == 0)
        def _():
            carry[...] = jnp.zeros_like(carry)
            dcw_ref[...] = jnp.zeros_like(dcw_ref)
            dcb_ref[...] = jnp.zeros_like(dcb_ref)

        df = _dot_nt(d_ref[...], w_ref[...])
        halo = jnp.where(i > 0, hh_ref[...].astype(F32), 0.0)
        ha = h_ref[0].astype(F32)
        hv = h_ref[1].astype(F32)
        cwa, cwv = cwa_ref[...], cwv_ref[...]
        ca, a1, a2 = _causal_conv(jnp.concatenate([halo[0], ha], axis=0), hb, cwa, cba_ref[...])
        cv, v1, v2 = _causal_conv(jnp.concatenate([halo[1], hv], axis=0), hb, cwv, cbv_ref[...])
        sig = _sigmoid(ca)
        da = df * cv * sig * (1.0 + ca * (1.0 - sig))
        dv = df * ca * sig
        for half, (dc, h0, h1, h2, w) in enumerate(((da, ha, a1, a2, cwa), (dv, hv, v1, v2, cwv))):
            dcb_ref[half] += jnp.sum(dc, axis=0, keepdims=True)
            dcw_ref[half] += jnp.concatenate(
                [jnp.sum(dc * h2, axis=0, keepdims=True), jnp.sum(dc * h1, axis=0, keepdims=True),
                 jnp.sum(dc * h0, axis=0, keepdims=True)], axis=0)
            ext = jnp.concatenate([dc, carry[half]], axis=0)
            n1 = pltpu.roll(ext, n_ext - 1, 0)[:tm]
            n2 = pltpu.roll(ext, n_ext - 2, 0)[:tm]
            dh_ref[half] = (w[2:3] * dc + w[1:2] * n1 + w[0:1] * n2).astype(BF16)
            carry[half] = dc[:CONV_HALO]

    rev = lambda ip: n_i - 1 - ip
    return _call(
        body, name=name, grid=(n_j, n_i),
        in_specs=[pl.BlockSpec((tm, D), lambda j, ip: (rev(ip), 0)),
                  pl.BlockSpec((None, tn, D), lambda j, ip: (0, j, 0)),
                  pl.BlockSpec((2, tm, tn), lambda j, ip: (0, rev(ip), j)),
                  pl.BlockSpec((2, hb, tn), lambda j, ip: (0, jnp.maximum(rev(ip) * (tm // hb) - 1, 0), j)),
                  pl.BlockSpec((None, 3, tn), lambda j, ip: (l, 0, j)),
                  pl.BlockSpec((None, 3, tn), lambda j, ip: (l, 0, n_j + j)),
                  pl.BlockSpec((None, 1, tn), lambda j, ip: (l, 0, j)),
                  pl.BlockSpec((None, 1, tn), lambda j, ip: (l, 0, n_j + j))],
        out_specs=[pl.BlockSpec((2, tm, tn), lambda j, ip: (0, rev(ip), j)),
                   pl.BlockSpec((2, 3, tn), lambda j, ip: (0, 0, j)),
                   pl.BlockSpec((2, 1, tn), lambda j, ip: (0, 0, j))],
        out_shape=[jax.ShapeDtypeStruct((2, T, F), BF16), jax.ShapeDtypeStruct((2, 3, F), F32),
                   jax.ShapeDtypeStruct((2, 1, F), F32)],
        scratch=[pltpu.VMEM((2, CONV_HALO, tn), F32)], carry=carry)(db16, wdn, h, h, cw, cw, cb, cb)


def _pool_fwd(ext, xb_g, t_glob, win):
    e = ext
    sft = 1
    while sft < win:
        e = e + pltpu.roll(e, sft, 0)
        sft *= 2
    cnt = jnp.minimum(t_glob + 1.0, float(win))
    return e[POOL_HALO:] / cnt - xb_g


def gating_fwd(h0, lg, lb, ws, bsT, wp, sc, *, name):
    T = h0.shape[0]
    DA = lg.shape[-1]
    DB = sc.shape[-1]
    HA = DA // A_HEAD
    G = len(B_WINDOWS)
    CG = DB // G
    tm = _tile(T, 512, A_CHUNK)
    n_c = tm // A_CHUNK

    def body(h_ref, halo_ref, lg_ref, lb_ref, ws_ref, bsT_ref, wp_ref, sc_ref, cat_ref):
        i = pl.program_id(0)
        hu = h_ref[:, 0:DA].astype(F32)
        hv = h_ref[:, DA:2 * DA].astype(F32)
        xb = h_ref[:, 2 * DA:].astype(F32)
        u = _gelu(hu)
        vn, _, _ = _ln_fwd(_gelu(hv), lg_ref[...], lb_ref[...])
        vnb = vn.astype(BF16)
        rr = lax.broadcasted_iota(jnp.int32, (A_CHUNK, A_CHUNK), 0)
        cc = lax.broadcasted_iota(jnp.int32, (A_CHUNK, A_CHUNK), 1)
        for hh in range(HA):
            wt = jnp.where(rr >= cc, ws_ref[hh], 0.0).astype(BF16)
            cs = slice(hh * A_HEAD, (hh + 1) * A_HEAD)
            for n in range(n_c):
                rs = slice(n * A_CHUNK, (n + 1) * A_CHUNK)
                s = _dot(wt, vnb[rs, cs]) + bsT_ref[:, hh:hh + 1]
                cat_ref[rs, cs] = (u[rs, cs] * s).astype(BF16)
        halo = jnp.where(i > 0, halo_ref[...].astype(F32), 0.0)
        ext = jnp.concatenate([halo, xb], axis=0)
        t_glob = (i * tm + lax.broadcasted_iota(jnp.int32, (tm, 1), 0)).astype(F32)
        for g, win in enumerate(B_WINDOWS):
            gs = slice(g * CG, (g + 1) * CG)
            p = _pool_fwd(ext[:, gs], xb[:, gs], t_glob, win)
            z = _dot(p.astype(BF16), wp_ref[g])
            cat_ref[:, DA + g * CG:DA + (g + 1) * CG] = (z * sc_ref[:, gs]).astype(BF16)

    full = lambda a: pl.BlockSpec(a.shape, lambda i: (0,) * a.ndim)
    hpb = tm // POOL_HALO
    return _call(
        body, name=name, grid=(T // tm,),
        in_specs=[pl.BlockSpec((tm, 2 * DA + DB), lambda i: (i, 0)),
                  pl.BlockSpec((POOL_HALO, DB), lambda i: (jnp.maximum(i * hpb - 1, 0), 2 * DA // DB)),
                  full(lg), full(lb), full(ws), full(bsT), full(wp), full(sc)],
        out_specs=pl.BlockSpec((tm, DA + DB), lambda i: (i, 0)),
        out_shape=jax.ShapeDtypeStruct((T, DA + DB), BF16))(h0, h0, lg, lb, ws, bsT, wp, sc)


def gating_bwd(h0, dcat, lg, lb, ws, wsT, bsT, wp, sc, *, name, carry=()):
    T = h0.shape[0]
    DA = lg.shape[-1]
    DB = sc.shape[-1]
    HA = DA // A_HEAD
    G = len(B_WINDOWS)
    CG = DB // G
    tm = _tile(T, 512, A_CHUNK)
    n_i = T // tm
    n_c = tm // A_CHUNK
    n_ext = tm + POOL_HALO

    def body(h_ref, halo_ref, dc_ref, dhalo_ref, lg_ref, lb_ref, ws_ref, wsT_ref, bsT_ref, wp_ref, sc_ref,
             dh_ref, dws_ref, dbsT_ref, dlg_ref, dlb_ref, dsc_ref, dwp_ref, dvn_sc):
        i = pl.program_id(0)

        @pl.when(i == 0)
        def _():
            for r in (dws_ref, dbsT_ref, dlg_ref, dlb_ref, dsc_ref, dwp_ref):
                r[...] = jnp.zeros_like(r)

        hu = h_ref[:, 0:DA].astype(F32)
        hv = h_ref[:, DA:2 * DA].astype(F32)
        xb = h_ref[:, 2 * DA:].astype(F32)
        u = _gelu(hu)
        gu = _gelu_grad(hu)
        lgv = lg_ref[...]
        vn, vhat, rstd = _ln_fwd(_gelu(hv), lgv, lb_ref[...])
        vnb = vn.astype(BF16)
        rr = lax.broadcasted_iota(jnp.int32, (A_CHUNK, A_CHUNK), 0)
        cc = lax.broadcasted_iota(jnp.int32, (A_CHUNK, A_CHUNK), 1)
        for hh in range(HA):
            wt = jnp.where(rr >= cc, ws_ref[hh], 0.0).astype(BF16)
            wtT = jnp.where(rr <= cc, wsT_ref[hh], 0.0).astype(BF16)
            cs = slice(hh * A_HEAD, (hh + 1) * A_HEAD)
            dws = jnp.zeros((A_CHUNK, A_CHUNK), F32)
            dbs = jnp.zeros((A_CHUNK, 1), F32)
            for n in range(n_c):
                rs = slice(n * A_CHUNK, (n + 1) * A_CHUNK)
                vb = vnb[rs, cs]
                s = _dot(wt, vb) + bsT_ref[:, hh:hh + 1]
                dya = dc_ref[rs, cs].astype(F32)
                ds = dya * u[rs, cs]
                dh_ref[rs, cs] = (dya * s * gu[rs, cs]).astype(BF16)
                dsb = ds.astype(BF16)
                dbs = dbs + jnp.sum(ds, axis=1, keepdims=True)
                dws = dws + _dot_nt(dsb, vb)
                dvn_sc[rs, cs] = _dot(wtT, dsb)
            dws_ref[hh] += jnp.where(rr >= cc, dws, 0.0)
            dbsT_ref[:, hh:hh + 1] += dbs
        dvg, dlg, dlb = _ln_bwd(dvn_sc[...], vhat, rstd, lgv)
        dlg_ref[...] += dlg
        dlb_ref[...] += dlb
        dh_ref[:, DA:2 * DA] = (dvg * _gelu_grad(hv)).astype(BF16)

        halo = jnp.where(i > 0, halo_ref[...].astype(F32), 0.0)
        ext = jnp.concatenate([halo, xb], axis=0)
        t_glob = (i * tm + lax.broadcasted_iota(jnp.int32, (tm, 1), 0)).astype(F32)
        t_ext = (i * tm + lax.broadcasted_iota(jnp.int32, (n_ext, 1), 0)).astype(F32)
        dyb = dc_ref[:, DA:].astype(F32)
        dhalo = jnp.where(i < n_i - 1, dhalo_ref[...].astype(F32), 0.0)
        dyb_ext = jnp.concatenate([dyb, dhalo], axis=0)
        for g, win in enumerate(B_WINDOWS):
            gs = slice(g * CG, (g + 1) * CG)
            pb = _pool_fwd(ext[:, gs], xb[:, gs], t_glob, win).astype(BF16)
            wpg = wp_ref[g]
            z = _dot(pb, wpg)
            dsc_ref[:, gs] += jnp.sum(dyb[:, gs] * z, axis=0, keepdims=True)
            dzb = (dyb_ext[:, gs] * sc_ref[:, gs]).astype(BF16)
            dwp_ref[g] += _dot_tn(pb, dzb[:tm])
            dp = _dot_nt(dzb, wpg)
            e = dp / jnp.minimum(t_ext + 1.0, float(win))
            sft = 1
            while sft < win:
                e = e + pltpu.roll(e, n_ext - sft, 0)
                sft *= 2
            dh_ref[:, 2 * DA + g * CG:2 * DA + (g + 1) * CG] = (e[:tm] - dp[:tm]).astype(BF16)

    full = lambda a: pl.BlockSpec(a.shape, lambda i: (0,) * a.ndim)
    hpb = tm // POOL_HALO
    n_hb = T // POOL_HALO
    outs = [jax.ShapeDtypeStruct((T, 2 * DA + DB), BF16), jax.ShapeDtypeStruct(ws.shape, F32),
            jax.ShapeDtypeStruct(bsT.shape, F32), jax.ShapeDtypeStruct(lg.shape, F32),
            jax.ShapeDtypeStruct(lb.shape, F32), jax.ShapeDtypeStruct(sc.shape, F32),
            jax.ShapeDtypeStruct(wp.shape, F32)]
    return _call(
        body, name=name, grid=(n_i,),
        in_specs=[pl.BlockSpec((tm, 2 * DA + DB), lambda i: (i, 0)),
                  pl.BlockSpec((POOL_HALO, DB), lambda i: (jnp.maximum(i * hpb - 1, 0), 2 * DA // DB)),
                  pl.BlockSpec((tm, DA + DB), lambda i: (i, 0)),
                  pl.BlockSpec((POOL_HALO, DB), lambda i: (jnp.minimum((i + 1) * hpb, n_hb - 1), DA // DB)),
                  full(lg), full(lb), full(ws), full(wsT), full(bsT), full(wp), full(sc)],
        out_specs=[pl.BlockSpec((tm, 2 * DA + DB), lambda i: (i, 0))] + [full(o) for o in outs[1:]],
        out_shape=outs,
        scratch=[pltpu.VMEM((tm, DA), F32)], carry=carry)(h0, h0, dcat, dcat, lg, lb, ws, wsT, bsT, wp, sc)


HP = 2 * C_HEAD


def _hgrn_gates(blk, lbv, tri):
    q = blk[:, 0:HP]
    sg = _sigmoid(blk[:, HP:2 * HP])
    f = lbv + (1.0 - lbv) * sg
    bcum = _exact_tri_dot(tri, jnp.log(f))
    blast = bcum[C_CHUNK - 1:C_CHUNK]
    sq = _sigmoid(q)
    k = 1.0 - f
    e_in = jnp.exp(bcum)
    e_out = jnp.exp(-bcum)
    e_end = jnp.exp(blast - bcum)
    return dict(q=q, sq=sq, sg=sg, f=f, k=k, bcum=bcum, blast=blast, e_in=e_in, e_out=e_out, e_end=e_end,
                qd=q * sq * e_in, kd=k * e_out, ke=k * e_end, dec=jnp.exp(blast), v=blk[:, 2 * HP:3 * HP],
                gg=blk[:, 3 * HP:4 * HP])


def _lower_bound(lbp_ref):
    p0, p1 = lbp_ref[0:1], lbp_ref[1:2]
    mx = jnp.maximum(p0, p1)
    e0, e1 = jnp.exp(p0 - mx), jnp.exp(p1 - mx)
    return e1 / (e0 + e1)


def hgrn_fwd(h1p, lbp, gn, *, name, carry=()):
    T = h1p.shape[0]
    DC = gn.shape[-1]
    n_p = DC // HP
    tt = _tile(T, 512, C_CHUNK)
    n_c = tt // C_CHUNK

    def body(h_ref, lbp_ref, gn_ref, y_ref, o_ref, sp_ref, st):
        i = pl.program_id(1)

        @pl.when(i == 0)
        def _():
            st[...] = jnp.zeros_like(st)

        lbv = _lower_bound(lbp_ref)
        gnv = gn_ref[...]
        rr = lax.broadcasted_iota(jnp.int32, (C_CHUNK, C_CHUNK), 0)
        cc = lax.broadcasted_iota(jnp.int32, (C_CHUNK, C_CHUNK), 1)
        causal = rr >= cc
        tri = jnp.where(causal, 1.0, 0.0).astype(BF16)

        def chunk(n, _):
            rows = pl.ds(pl.multiple_of(n * C_CHUNK, C_CHUNK), C_CHUNK)
            a = _hgrn_gates(h_ref[rows, :].astype(F32), lbv, tri)
            outs = []
            for hd in range(2):
                cs = slice(hd * C_HEAD, (hd + 1) * C_HEAD)
                qd, kd, ke = a["qd"][:, cs].astype(BF16), a["kd"][:, cs].astype(BF16), a["ke"][:, cs].astype(BF16)
                vb = a["v"][:, cs].astype(BF16)
                s_t = st[hd]
                att = jnp.where(causal, _dot_nt(qd, kd), 0.0).astype(BF16)
                outs.append(_dot(att, vb) + _dot_nt(qd, s_t.astype(BF16)))
                sp_ref[hd, n] = s_t
                st[hd] = a["dec"][:, cs] * s_t + _dot_tn(vb, ke)
            o = jnp.concatenate(outs, axis=1)
            o_ref[rows, :] = o.astype(BF16)
            ys = []
            for hd in range(2):
                oh = o[:, hd * C_HEAD:(hd + 1) * C_HEAD]
                ys.append(oh * lax.rsqrt(jnp.mean(oh * oh, axis=-1, keepdims=True) + LN_EPS))
            y_ref[rows, :] = (jnp.concatenate(ys, axis=1) * gnv * _sigmoid(a["gg"])).astype(BF16)
            return 0

        lax.fori_loop(0, n_c, chunk, 0)

    return _call(
        body, name=name, grid=(n_p, T // tt),
        in_specs=[pl.BlockSpec((tt, 4 * HP), lambda p, i: (i, p)), pl.BlockSpec((2, HP), lambda p, i: (0, p)),
                  pl.BlockSpec((1, HP), lambda p, i: (0, p))],
        out_specs=[pl.BlockSpec((tt, HP), lambda p, i: (i, p)), pl.BlockSpec((tt, HP), lambda p, i: (i, p)),
                   pl.BlockSpec((2, n_c, C_HEAD, C_HEAD), lambda p, i: (p, i, 0, 0))],
        out_shape=[jax.ShapeDtypeStruct((T, DC), BF16), jax.ShapeDtypeStruct((T, DC), BF16),
                   jax.ShapeDtypeStruct((2 * n_p, T // C_CHUNK, C_HEAD, C_HEAD), F32)],
        scratch=[pltpu.VMEM((2, C_HEAD, C_HEAD), F32)], carry=carry)(h1p, lbp, gn)


def hgrn_bwd(h1p, o_saved, dy, sp, lbp, gn, *, name, carry=()):
    T = h1p.shape[0]
    DC = gn.shape[-1]
    n_p = DC // HP
    tt = _tile(T, 512, C_CHUNK)
    n_i = T // tt
    n_c = tt // C_CHUNK

    def body(h_ref, o_ref, dy_ref, sp_ref, lbp_ref, gn_ref, dh_ref, dgn_ref, dlbp_ref, dst, dlb_acc):
        ip = pl.program_id(1)

        @pl.when(ip == 0)
        def _():
            dst[...] = jnp.zeros_like(dst)
            dlb_acc[...] = jnp.zeros_like(dlb_acc)
            dgn_ref[...] = jnp.zeros_like(dgn_ref)

        lbv = _lower_bound(lbp_ref)
        gnv = gn_ref[...]
        rr = lax.broadcasted_iota(jnp.int32, (C_CHUNK, C_CHUNK), 0)
        cc = lax.broadcasted_iota(jnp.int32, (C_CHUNK, C_CHUNK), 1)
        causal = rr >= cc
        tri = jnp.where(causal, 1.0, 0.0).astype(BF16)
        tri_t = jnp.where(rr <= cc, 1.0, 0.0).astype(BF16)
        last_row = lax.broadcasted_iota(jnp.int32, (C_CHUNK, 1), 0) == C_CHUNK - 1

        def chunk(m, _):
            n = n_c - 1 - m
            rows = pl.ds(pl.multiple_of(n * C_CHUNK, C_CHUNK), C_CHUNK)
            a = _hgrn_gates(h_ref[rows, :].astype(F32), lbv, tri)
            o = o_ref[rows, :].astype(F32)
            dyv = dy_ref[rows, :].astype(F32)
            sgg = _sigmoid(a["gg"])
            ohs, rrs = [], []
            for hd in range(2):
                oh = o[:, hd * C_HEAD:(hd + 1) * C_HEAD]
                r = lax.rsqrt(jnp.mean(oh * oh, axis=-1, keepdims=True) + LN_EPS)
                ohs.append(oh * r)
                rrs.append(r)
            ohat = jnp.concatenate(ohs, axis=1)
            dyn = dyv * sgg
            dgg = dyv * ohat * gnv * sgg * (1.0 - sgg)
            dgn_ref[...] += jnp.sum(dyn * ohat, axis=0, keepdims=True)
            dxh = dyn * gnv
            dqd_l, dkd_l, dke_l, dv_l, ddec_l = [], [], [], [], []
            for hd in range(2):
                cs = slice(hd * C_HEAD, (hd + 1) * C_HEAD)
                dxh_h, oh_h = dxh[:, cs], ohat[:, cs]
                do = rrs[hd] * (dxh_h - oh_h * jnp.mean(dxh_h * oh_h, axis=-1, keepdims=True))
                dob = do.astype(BF16)
                qd, kd, ke = a["qd"][:, cs].astype(BF16), a["kd"][:, cs].astype(BF16), a["ke"][:, cs].astype(BF16)
                vb = a["v"][:, cs].astype(BF16)
                s_t = sp_ref[hd, n]
                ds_out = dst[hd]
                ds_outb = ds_out.astype(BF16)
                att = jnp.where(causal, _dot_nt(qd, kd), 0.0).astype(BF16)
                datt = jnp.where(causal, _dot_nt(dob, vb), 0.0).astype(BF16)
                dv_l.append(_dot_tn(att, dob) + _dot_nt(ke, ds_outb))
                dqd_l.append(_dot(datt, kd) + _dot(dob, s_t.astype(BF16)))
                dkd_l.append(_dot_tn(datt, qd))
                dke_l.append(_dot(vb, ds_outb))
                ddec_l.append(jnp.sum(ds_out * s_t, axis=0, keepdims=True))
                dst[hd] = a["dec"][:, cs] * ds_out + _dot_tn(dob, qd)
            dqd = jnp.concatenate(dqd_l, axis=1)
            dkd = jnp.concatenate(dkd_l, axis=1)
            dke = jnp.concatenate(dke_l, axis=1)
            dv = jnp.concatenate(dv_l, axis=1)
            ddec = jnp.concatenate(ddec_l, axis=1)
            dqs = dqd * a["e_in"]
            kek = dke * a["ke"]
            dbcum = dqd * a["qd"] - dkd * a["kd"] - kek
            dk = dkd * a["e_out"] + dke * a["e_end"]
            dblast = jnp.sum(kek, axis=0, keepdims=True) + ddec * a["dec"]
            dbcum = dbcum + jnp.where(last_row, dblast, 0.0)
            dlf = _exact_tri_dot(tri_t, dbcum)
            df = dlf / a["f"] - dk
            dlb_acc[...] += jnp.sum(df * (1.0 - a["sg"]), axis=0, keepdims=True)
            dfl = df * (1.0 - lbv) * a["sg"] * (1.0 - a["sg"])
            dq = dqs * a["sq"] * (1.0 + a["q"] * (1.0 - a["sq"]))
            dh_ref[rows, :] = jnp.concatenate([dq, dfl, dv, dgg], axis=1).astype(BF16)
            return 0

        lax.fori_loop(0, n_c, chunk, 0)

        @pl.when(ip == n_i - 1)
        def _():
            d1 = dlb_acc[...] * lbv * (1.0 - lbv)
            dlbp_ref[...] = jnp.concatenate([-d1, d1], axis=0)

    rev = lambda ip: n_i - 1 - ip
    return _call(
        body, name=name, grid=(n_p, n_i),
        in_specs=[pl.BlockSpec((tt, 4 * HP), lambda p, ip: (rev(ip), p)),
                  pl.BlockSpec((tt, HP), lambda p, ip: (rev(ip), p)),
                  pl.BlockSpec((tt, HP), lambda p, ip: (rev(ip), p)),
                  pl.BlockSpec((2, n_c, C_HEAD, C_HEAD), lambda p, ip: (p, rev(ip), 0, 0)),
                  pl.BlockSpec((2, HP), lambda p, ip: (0, p)), pl.BlockSpec((1, HP), lambda p, ip: (0, p))],
        out_specs=[pl.BlockSpec((tt, 4 * HP), lambda p, ip: (rev(ip), p)),
                   pl.BlockSpec((1, HP), lambda p, ip: (0, p)), pl.BlockSpec((2, HP), lambda p, ip: (0, p))],
        out_shape=[jax.ShapeDtypeStruct(h1p.shape, BF16), jax.ShapeDtypeStruct((1, DC), F32),
                   jax.ShapeDtypeStruct((2, DC), F32)],
        scratch=[pltpu.VMEM((2, C_HEAD, C_HEAD), F32), pltpu.VMEM((1, HP), F32)], carry=carry)(h1p, o_saved, dy, sp, lbp, gn)


def loss_bwd(xh, rstd, g, b, target, *, name):
    T, N = xh.shape
    tm = _tile(T, 512, 8)

    def body(xh_ref, rs_ref, g_ref, b_ref, t_ref, ls_ref, dr_ref, drb_ref, dg_ref, db_ref):
        i = pl.program_id(0)

        @pl.when(i == 0)
        def _():
            for r in (ls_ref, dg_ref, db_ref):
                r[...] = jnp.zeros_like(r)

        xhv, gv = xh_ref[...], g_ref[...]
        e = xhv * gv + b_ref[...] - t_ref[...]
        ls_ref[...] += 0.5 * jnp.sum(jnp.mean(e * e, axis=-1, keepdims=True), axis=0, keepdims=True)
        dr, dg, db = _ln_bwd(e / N, xhv, rs_ref[...], gv)
        dr_ref[...] = dr
        drb_ref[...] = dr.astype(BF16)
        dg_ref[...] += dg
        db_ref[...] += db

    row = pl.BlockSpec((tm, N), lambda i: (i, 0))
    vec = pl.BlockSpec((1, N), lambda i: (0, 0))
    return _call(body, name=name, grid=(T // tm,),
                 in_specs=[row, pl.BlockSpec((tm, 1), lambda i: (i, 0)), vec, vec, row],
                 out_specs=[pl.BlockSpec((1, LANES), lambda i: (0, 0)), row, row, vec, vec],
                 out_shape=[jax.ShapeDtypeStruct((1, LANES), F32), jax.ShapeDtypeStruct((T, N), F32),
                            jax.ShapeDtypeStruct((T, N), BF16), jax.ShapeDtypeStruct((1, N), F32),
                            jax.ShapeDtypeStruct((1, N), F32)])(xh, rstd, g, b, target)


def _mesh_pos():
    x, y, c = lax.axis_index("x"), lax.axis_index("y"), lax.axis_index("c")
    chips = [(1 - x, y), (x, 1 - y), (1 - x, 1 - y)]
    return x, y, c, chips


def _remote(src, dst, send, recv, j, dev):
    return pltpu.make_async_remote_copy(src_ref=src, dst_ref=dst, send_sem=send.at[j], recv_sem=recv.at[j],
                                        device_id=dev, device_id_type=MESH)


def mesh_ids():
    x, y, c, chips = _mesh_pos()
    return jnp.stack([c] + [2 * cx + cy for cx, cy in chips] + [2 * x + y]).astype(jnp.int32)


def _sibling():
    return (lax.axis_index("x"), lax.axis_index("y"), 1 - lax.axis_index("c"))


def _swap_with_sibling(src, recv, send_sem, recv_sem, step):
    slot = step % 2
    cp = pltpu.make_async_remote_copy(src_ref=src, dst_ref=recv.at[slot], send_sem=send_sem.at[slot],
                                      recv_sem=recv_sem.at[slot], device_id=_sibling(), device_id_type=MESH)
    cp.start()
    cp.wait_recv()
    return cp, slot


def _swap_scratch(br, C, dtype):
    return [pltpu.VMEM((2, br, C), dtype), pltpu.SemaphoreType.DMA((2,)), pltpu.SemaphoreType.DMA((2,))]


def _swap_rows(Rh, C, itemsize):
    return _tile(Rh, max(16, (2 << 20) // (C * itemsize)), 16)


def cast_to_slot(a3, l, me1, name):
    _, R, C = a3.shape
    br = _tile(R, max(8, (1 << 20) // C), 16)

    def body(me_ref, a_ref, o_ref):
        o_ref[...] = a_ref[...].astype(BF16)

    return _call(body, name=name, grid=(R // br,), prefetch=1,
                 in_specs=[pl.BlockSpec((None, br, C), lambda r, me: (l, r, 0))],
                 out_specs=pl.BlockSpec((None, None, br, C), lambda r, me: (0, me[0], r, 0)),
                 out_shape=jax.ShapeDtypeStruct((1, 4, R, C), BF16))(me1, a3)


def all_gather_chips(big, small, *, name):
    nb, ns = len(big), len(small)
    layers = [(t, l) for t in range(nb) for l in range(big[t].shape[0])]
    n_big = 3 * len(layers)
    n_rem = n_big + 3 * ns

    def body(*refs):
        small_in = refs[nb:nb + ns]
        bufs, small_out = refs[nb + ns:2 * nb + ns], refs[2 * nb + ns:2 * (nb + ns)]
        send, recv, loc = refs[2 * (nb + ns):]
        x, y, c, chips = _mesh_pos()
        me = 2 * x + y
        ids = [2 * cx + cy for cx, cy in chips]
        started, sends = [], []
        for t in range(ns):
            cp = pltpu.make_async_copy(small_in[t], small_out[t].at[me], loc.at[t])
            cp.start()
            started.append(cp)
        for q, (t, l) in enumerate(layers):
            for k, chip in enumerate(chips):
                blk = bufs[t].at[l, me, c]
                cp = _remote(blk, blk, send, recv, 3 * q + k, (*chip, c))
                cp.start()
                sends.append(cp)
        for t in range(ns):
            for k, chip in enumerate(chips):
                cp = _remote(small_in[t], small_out[t].at[me], send, recv, n_big + 3 * t + k, (*chip, c))
                cp.start()
                sends.append(cp)
        for q, (t, l) in enumerate(layers):
            for k in range(3):
                blk = bufs[t].at[l, ids[k], c]
                _remote(blk, blk, send, recv, 3 * q + k, (x, y, c)).wait_recv()
        for t in range(ns):
            for k in range(3):
                blk = small_out[t].at[ids[k]]
                _remote(blk, blk, send, recv, n_big + 3 * t + k, (x, y, c)).wait_recv()
        for cp in sends:
            cp.wait_send()
        for cp in started:
            cp.wait()

    out_shape = [jax.ShapeDtypeStruct(a.shape, a.dtype) for a in big]
    out_shape += [jax.ShapeDtypeStruct((4,) + a.shape, a.dtype) for a in small]
    return _call(body, name=name, in_specs=[ANY] * (nb + ns), out_specs=[ANY] * (nb + ns), out_shape=out_shape,
                 aliases={t: t for t in range(nb)},
                 scratch=[pltpu.SemaphoreType.DMA((n_rem,)), pltpu.SemaphoreType.DMA((n_rem,)),
                          pltpu.SemaphoreType.DMA((max(ns, 1),))])(*big, *small)


def all_gather_pair(buf, ids, *, name):
    L, _, _, Rh, C = buf.shape
    br = _swap_rows(Rh, C, 2)
    n_r = Rh // br

    def body(ids_ref, in_ref, o_ref, recv, ssem, rsem):
        step = (pl.program_id(0) * 3 + pl.program_id(1)) * n_r + pl.program_id(2)
        cp, slot = _swap_with_sibling(in_ref, recv, ssem, rsem, step)
        o_ref[...] = recv[slot]
        cp.wait_send()

    at = lambda l, s, h, r: (((l * 4 + s) * 2 + h) * n_r + r, 0)
    out = _call(body, name=name, grid=(L, 3, n_r), prefetch=1,
                in_specs=[pl.BlockSpec((br, C), lambda l, k, r, ids: at(l, ids[1 + k], ids[0], r))],
                out_specs=pl.BlockSpec((br, C), lambda l, k, r, ids: at(l, ids[1 + k], 1 - ids[0], r)),
                out_shape=jax.ShapeDtypeStruct((L * 8 * Rh, C), buf.dtype), aliases={1: 0},
                scratch=_swap_scratch(br, C, BF16))(ids, buf.reshape(L * 8 * Rh, C))
    return out.reshape(buf.shape)


def rs_pair_add(grad, ids, *, name):
    _, _, Rh, C = grad.shape
    br = _swap_rows(Rh, C, 2)
    n_r = Rh // br

    def body(ids_ref, send_ref, keep_ref, pb_ref, own_ref, recv, ssem, rsem):
        ph = pl.program_id(0)
        cp, slot = _swap_with_sibling(send_ref, recv, ssem, rsem, ph * n_r + pl.program_id(1))
        s = keep_ref[...].astype(F32) + recv[slot].astype(F32)

        @pl.when(ph == 0)
        def _():
            own_ref[...] = s

        @pl.when(ph > 0)
        def _():
            pb_ref[...] = s.astype(BF16)

        cp.wait_send()

    rel = lambda ph: (ph + 3) % 4
    at = lambda s, h, r: ((s * 2 + h) * n_r + r, 0)
    grad = grad.reshape(8 * Rh, C)
    return _call(
        body, name=name, grid=(4, n_r), prefetch=1,
        in_specs=[pl.BlockSpec((br, C), lambda ph, r, ids: at(ids[1 + rel(ph)], 1 - ids[0], r)),
                  pl.BlockSpec((br, C), lambda ph, r, ids: at(ids[1 + rel(ph)], ids[0], r))],
        out_specs=[pl.BlockSpec((None, br, C), lambda ph, r, ids: (jnp.maximum(ph - 1, 0), jnp.where(ph == 0, 0, r), 0)),
                   pl.BlockSpec((br, C), lambda ph, r, ids: (jnp.where(ph == 0, r, n_r - 1), 0))],
        out_shape=[jax.ShapeDtypeStruct((3, Rh, C), BF16), jax.ShapeDtypeStruct((Rh, C), F32)],
        scratch=_swap_scratch(br, C, BF16))(ids, grad, grad)


def rs_finish(owns, gots, *, name):
    L = len(owns)
    Rh, C = owns[0].shape
    br = _swap_rows(Rh, C, 4)
    n_r = Rh // br

    def body(*refs):
        own_refs, got_refs, o_ref = refs[:L], refs[L:2 * L], refs[2 * L]
        recv, ssem, rsem = refs[2 * L + 1:]
        l = pl.program_id(0)
        c = lax.axis_index("c")
        for ll in range(L):
            @pl.when(l == ll)
            def _():
                s = own_refs[ll][...]
                for k in range(3):
                    s = s + got_refs[ll][k].astype(F32)
                o_ref[c] = s

        cp, slot = _swap_with_sibling(o_ref.at[c], recv, ssem, rsem, l * n_r + pl.program_id(1))
        o_ref[1 - c] = recv[slot]
        cp.wait_send()

    def at_layer(ll):
        return lambda l, r: jnp.where(l == ll, r, jnp.where(l < ll, 0, n_r - 1))

    in_specs = [pl.BlockSpec((br, C), lambda l, r, ll=ll: (at_layer(ll)(l, r), 0)) for ll in range(L)]
    in_specs += [pl.BlockSpec((3, br, C), lambda l, r, ll=ll: (0, at_layer(ll)(l, r), 0)) for ll in range(L)]
    out = _call(body, name=name, grid=(L, n_r), in_specs=in_specs,
                out_specs=pl.BlockSpec((2, br, C), lambda l, r: (l, r, 0)),
                out_shape=jax.ShapeDtypeStruct((L * 2, Rh, C), F32),
                scratch=_swap_scratch(br, C, F32))(*owns, *gots)
    return out.reshape(L, 2, Rh, C)


def all_reduce_small(buf, *, name):
    rows = buf.shape[0]

    def body(x_ref, o_ref, rbuf, send, recv):
        x, y, c, _ = _mesh_pos()
        o_ref[...] = x_ref[...]
        for k, dev in enumerate(((x, y, 1 - c), (1 - x, y, c), (x, 1 - y, c))):
            cp = _remote(o_ref, rbuf.at[k], send, recv, k, dev)
            cp.start()
            cp.wait()
            o_ref[...] = o_ref[...] + rbuf[k]

    return _call(body, name=name, in_specs=[VMEM_SPEC], out_specs=VMEM_SPEC,
                 out_shape=jax.ShapeDtypeStruct(buf.shape, F32),
                 scratch=[pltpu.VMEM((3, rows, LANES), F32), pltpu.SemaphoreType.DMA((3,)),
                          pltpu.SemaphoreType.DMA((3,))])(buf)


def _pack(arrs):
    parts = []
    for a in arrs:
        f = a.reshape(-1).astype(F32)
        parts.append(jnp.pad(f, (0, (-f.shape[0]) % PACK_ALIGN)))
    return jnp.concatenate(parts).reshape(-1, LANES)


def _unpack(buf, shapes):
    flat = buf.reshape(-1)
    out, off = [], 0
    for s in shapes:
        n = math.prod(s)
        out.append(flat[off:off + n].reshape(s))
        off += n + (-n) % PACK_ALIGN
    return out


_WEIGHTS = ['ev_w_in', 'ev_ln_v_g', 'ev_ln_v_b', 'ev_w_s', 'ev_b_s', 'ev_w_pool', 'ev_pool_scale', 'ev_w_out',
            'od_w_in', 'od_norm_g', 'od_w_out', 'lb_param', 'ffn_w_up', 'ffn_conv_w', 'ffn_conv_b', 'ffn_w_down',
            'ln1_g', 'ln1_b', 'ln2_g', 'ln2_b']
_BIG = ['ev_w_in', 'ev_w_out', 'od_w_in', 'od_w_out', 'ffn_w_up', 'ffn_w_down']
_SMALL = [n for n in _WEIGHTS if n not in _BIG]


def kernel(x, ev_w_in, ev_ln_v_g, ev_ln_v_b, ev_w_s, ev_b_s, ev_w_pool, ev_pool_scale, ev_w_out, od_w_in, od_norm_g, od_w_out, lb_param, ffn_w_up, ffn_conv_w, ffn_conv_b, ffn_w_down, ln1_g, ln1_b, ln2_g, ln2_b, loss_target, m_ev_w_in, m_ev_ln_v_g, m_ev_ln_v_b, m_ev_w_s, m_ev_b_s, m_ev_w_pool, m_ev_pool_scale, m_ev_w_out, m_od_w_in, m_od_norm_g, m_od_w_out, m_lb_param, m_ffn_w_up, m_ffn_conv_w, m_ffn_conv_b, m_ffn_w_down, m_ln1_g, m_ln1_b, m_ln2_g, m_ln2_b, v_ev_w_in, v_ev_ln_v_g, v_ev_ln_v_b, v_ev_w_s, v_ev_b_s, v_ev_w_pool, v_ev_pool_scale, v_ev_w_out, v_od_w_in, v_od_norm_g, v_od_w_out, v_lb_param, v_ffn_w_up, v_ffn_conv_w, v_ffn_conv_b, v_ffn_w_down, v_ln1_g, v_ln1_b, v_ln2_g, v_ln2_b):
    given = dict(locals())
    w = {n: given[n] for n in _WEIGHTS}
    mom = {n: given["m_" + n] for n in _WEIGHTS}
    vel = {n: given["v_" + n] for n in _WEIGHTS}
    x2d = x[0]
    tgt = loss_target[0]
    T, D = x2d.shape
    DA = ev_ln_v_g.shape[-1]
    DB = ev_pool_scale.shape[-1]
    HA = ev_w_s.shape[1]
    G = len(B_WINDOWS)
    CG = DB // G
    DC = 4 * od_norm_g.shape[-1]
    F = ffn_conv_b.shape[-1] // 2
    chip = 2 * lax.axis_index("x") + lax.axis_index("y")

    ids = mesh_ids()
    halves = lambda a: a.reshape(1, 4, 2, a.shape[2] // 2, a.shape[3])
    slot = {(n, l): halves(cast_to_slot(w[n], l, ids[4:5], f"cast_{n}{l}"))
            for n in _BIG for l in range(w[n].shape[0])}

    def riding(*keys):
        return [IciCopy("gather", slot[k]) for k in keys]

    def pair(buf, key):
        g = all_gather_pair(buf, ids, name=f"all_gather_pair_{key[0]}{key[1]}")
        return g.reshape(1, 4, g.shape[3] * 2, g.shape[4])

    early = [('ev_w_in', 0), ('ev_w_out', 0), ('ffn_w_up', 0)]
    gathered = all_gather_chips([slot[k] for k in early], [ev_w_pool[0], ffn_conv_w, od_norm_g],
                                name="all_gather_chips")
    wpool_full = gathered[3].transpose(1, 0, 2, 3).reshape(G, CG, CG)
    cw_full = gathered[4].transpose(1, 2, 0, 3).reshape(DEPTH, 3, 2 * F)
    gn_full = gathered[5].reshape(1, DC)
    win0 = pair(gathered[0], early[0])[0]
    wout0 = pair(gathered[1], early[1]).reshape(DA + DB, D)
    wup = {0: pair(gathered[2], early[2])}
    wdn = {}
    cb3 = ffn_conv_b.reshape(DEPTH, 1, 2 * F)
    ws = ev_w_s[0]
    wsT = jnp.swapaxes(ws, 1, 2)
    bsT = ev_b_s[0].T
    wpb = wpool_full.astype(BF16)
    ones = jnp.ones((1, D), F32)
    zeros = jnp.zeros((1, D), F32)
    row = lambda a, l: a[l:l + 1]

    Ns0 = win0.shape[-1]
    Nu = ffn_w_up.shape[-1]
    tm_big = _tile(T, 1024, 8)
    tm_ln = _tile(T, 512, 8)
    tk_ln = _tile(D, 512)
    n_p = DC // HP

    def nat_spec(Ns, tnw):
        nps = Ns // tnw
        return pl.BlockSpec((1, D, tnw), lambda i, j: (j // nps, 0, j % nps))

    perm_spec = pl.BlockSpec((4, D, HP), lambda i, j: (0, 0, j))

    xb16 = cast_bf16(x, "cast_x")[0]
    h0 = mm_nn(xb16, win0, w_spec=nat_spec(Ns0, Ns0), P=1, tnw=Ns0, tm=tm_big, n_j=4, name="ev_in")
    cat = gating_fwd(h0, ev_ln_v_g, ev_ln_v_b, ws, bsT, wpb, ev_pool_scale, name="gating_fwd")

    def mix_ln(a, wmat, res, l, name):
        K = a.shape[1]
        tk = _tile(K, 512)
        return mm_ln(a, wmat, *res, row(ln1_g, l), row(ln1_b, l), w_spec=pl.BlockSpec((tk, D), lambda i, k: (k, 0)),
                     K=K, tk=tk, tm=tm_ln, name=name)

    def ffn_down(f, res, l, carry=()):
        tk = _tile(F, 512)
        return mm_ln(f, wdn[l], *res, row(ln2_g, l), row(ln2_b, l),
                     w_spec=pl.BlockSpec((None, tk, D), lambda i, k: (0, k, 0)), K=F, tk=tk, tm=tm_ln,
                     name=f"ffn_down{l}", carry=carry)

    xh1, y1, rs1 = mix_ln(cat, wout0, (x2d, ones, zeros), 0, "ev_out")
    res1 = (xh1, row(ln1_g, 0), row(ln1_b, 0))
    ride = riding(('ffn_w_down', 0), ('od_w_in', 0))
    hf0, f0 = ffn_up(y1, wup[0], cw_full, cb3, 0, name="ffn_up0", carry=ride)
    wdn[0] = pair(ride[0].out, ('ffn_w_down', 0)).reshape(1, F, D)
    win1 = pair(ride[1].out, ('od_w_in', 0))[0]
    ride = riding(('od_w_out', 0), ('ffn_w_down', 1))
    xh2, y2, rs2 = ffn_down(f0, res1, 0, carry=ride)
    wout1 = pair(ride[0].out, ('od_w_out', 0)).reshape(DC, D)
    wdn[1] = pair(ride[1].out, ('ffn_w_down', 1)).reshape(1, F, D)
    res2 = (xh2, row(ln2_g, 0), row(ln2_b, 0))
    h1p = mm_nn(y2, win1, w_spec=perm_spec, P=4, tnw=HP, tm=tm_big, n_j=n_p, name="od_in")
    ride = riding(('ffn_w_up', 1))
    yh, o_saved, sp = hgrn_fwd(h1p, lb_param, gn_full, name="hgrn_fwd", carry=ride)
    wup[1] = pair(ride[0].out, ('ffn_w_up', 1))
    xh3, y3, rs3 = mix_ln(yh, wout1, res2, 1, "od_out")
    res3 = (xh3, row(ln1_g, 1), row(ln1_b, 1))
    hf1, f1 = ffn_up(y3, wup[1], cw_full, cb3, 1, name="ffn_up1")
    xh4, y4, rs4 = ffn_down(f1, res3, 1)
    loss_p, dr, drb, dg_ln2_1, db_ln2_1 = loss_bwd(xh4, rs4, row(ln2_g, 1), row(ln2_b, 1), tgt, name="loss_bwd")

    tt = _tile(T, 512, 16)
    n_t = T // tt
    tnu = Nu // 2 if (Nu // 2) % LANES == 0 else Nu
    upb = Nu // tnu
    tkd = _tile(D, 1024)

    def pair_sum(g4, name):
        pb, own = rs_pair_add(g4.reshape(4, 2, g4.shape[1] // 2, g4.shape[2]), ids, name="rs_pair_add_" + name)
        return IciCopy("scatter", pb), own

    def ffn_bwd(l, dr2, dr2b, f, hf, y_in, xh_in, rs_in):
        tkf = F // 4
        g_dn = mm_tn(f, dr2b, a_spec=pl.BlockSpec((tt, tkf), lambda kb, nb, t: (t, kb)),
                     g_spec=pl.BlockSpec((tt, D), lambda kb, nb, t: (t, 0)),
                     o_spec=pl.BlockSpec((1, tkf, D), lambda kb, nb, t: (0, kb, 0)), out_shape=(1, F, D),
                     grid=(4, 1, n_t), acc_shape=(tkf, D), P=1, tnw=D, name=f"g_ffn_down{l}")
        rs_dn = pair_sum(g_dn.reshape(4, F // 4, D), f"ffn_down{l}")
        dh, dcw, dcb = ffn_dgate(dr2b, wdn[l], hf, cw_full, cb3, l, name=f"ffn_dgate{l}", carry=[rs_dn[0]])
        g_up = mm_tn(y_in, dh, a_spec=pl.BlockSpec((tt, tkd), lambda kb, nb, t: (t, kb)),
                     g_spec=pl.BlockSpec((None, tt, tnu), lambda kb, nb, t: (nb // (2 * upb), t, nb % (2 * upb))),
                     o_spec=pl.BlockSpec((1, tkd, tnu), lambda kb, nb, t: (nb // upb, kb, nb % upb)),
                     out_shape=(4, D, Nu), grid=(D // tkd, 4 * upb, n_t), acc_shape=(tkd, tnu), P=1, tnw=tnu,
                     name=f"g_ffn_up{l}")
        rs_up = pair_sum(g_up, f"ffn_up{l}")
        tku = _tile(Nu, 256)
        kps = Nu // tku
        out = mm_nt_res(dh, wup[l], dr2, (xh_in, rs_in, row(ln1_g, l)),
                        a_spec=pl.BlockSpec((None, tm_ln, tku), lambda i, k: (k // (2 * kps), i, k % (2 * kps))),
                        w_spec=pl.BlockSpec((None, 1, D, tku), lambda i, k: (0, k // kps, 0, k % kps)),
                        P=1, tnw=tku, n_k=4 * kps, tm=tm_ln, name=f"d_ffn_in{l}", carry=[rs_up[0]])
        return rs_dn, rs_up, dcw, dcb, out

    def g_out(a, gb, name):
        K = a.shape[1]
        tkk = _tile(K, 1024)
        return mm_tn(a, gb, a_spec=pl.BlockSpec((tt, tkk), lambda kb, nb, t: (t, kb)),
                     g_spec=pl.BlockSpec((tt, tkd), lambda kb, nb, t: (t, nb)),
                     o_spec=pl.BlockSpec((1, tkk, tkd), lambda kb, nb, t: (0, kb, nb)), out_shape=(1, K, D),
                     grid=(K // tkk, D // tkd, n_t), acc_shape=(tkk, tkd), P=1, tnw=tkd, name=name)

    rs_dn1, rs_up1, dcw1, dcb1, (dr1, dr1b, dg_ln1_1, db_ln1_1) = ffn_bwd(1, dr, drb, f1, hf1, y3, xh3, rs3)
    rs_wout1 = pair_sum(g_out(yh, dr1b, "g_od_out").reshape(4, DC // 4, D), "od_out")
    dyh = mm_nt_plain(dr1b, wout1, tm=tm_big, tn=_tile(DC, 512), name="d_od_out")
    dh1p, d_gn, d_lbp = hgrn_bwd(h1p, o_saved, dyh, sp, lb_param, gn_full, name="hgrn_bwd", carry=[rs_wout1[0]])
    g_win1 = mm_tn(y2, dh1p, a_spec=pl.BlockSpec((tt, tkd), lambda kb, nb, t: (t, kb)),
                   g_spec=pl.BlockSpec((tt, 4 * HP), lambda kb, nb, t: (t, nb)),
                   o_spec=pl.BlockSpec((4, tkd, HP), lambda kb, nb, t: (0, kb, nb)), out_shape=(4, D, DC),
                   grid=(D // tkd, n_p, n_t), acc_shape=(tkd, 4 * HP), P=4, tnw=HP, name="g_od_in")
    rs_win1 = pair_sum(g_win1, "od_in")
    dr, drb, dg_ln2_0, db_ln2_0 = mm_nt_res(
        dh1p, win1, dr1, (xh2, rs2, row(ln2_g, 0)), a_spec=pl.BlockSpec((tm_ln, 4 * HP), lambda i, k: (i, k)),
        w_spec=pl.BlockSpec((4, D, HP), lambda i, k: (0, 0, k)), P=4, tnw=HP, n_k=n_p, tm=tm_ln, name="d_od_in",
        carry=[rs_win1[0]])
    rs_dn0, rs_up0, dcw0, dcb0, (dr1, dr1b, dg_ln1_0, db_ln1_0) = ffn_bwd(0, dr, drb, f0, hf0, y1, xh1, rs1)
    rs_wout0 = pair_sum(g_out(cat, dr1b, "g_ev_out").reshape(4, (DA + DB) // 4, D), "ev_out")
    dcat = mm_nt_plain(dr1b, wout0, tm=tm_big, tn=_tile(DA + DB, 512), name="d_ev_out")
    dh0, d_ws, d_bsT, d_lg, d_lb, d_sc, d_wp = gating_bwd(h0, dcat, ev_ln_v_g, ev_ln_v_b, ws, wsT, bsT, wpb,
                                                          ev_pool_scale, name="gating_bwd", carry=[rs_wout0[0]])
    g_win0 = mm_tn(xb16, dh0, a_spec=pl.BlockSpec((tt, tkd), lambda kb, nb, t: (t, kb)),
                   g_spec=pl.BlockSpec((tt, Ns0), lambda kb, nb, t: (t, nb)),
                   o_spec=pl.BlockSpec((1, tkd, Ns0), lambda kb, nb, t: (nb, kb, 0)), out_shape=(4, D, Ns0),
                   grid=(D // tkd, 4, n_t), acc_shape=(tkd, Ns0), P=1, tnw=Ns0, name="g_ev_in")
    rs_win0 = pair_sum(g_win0, "ev_in")
    grad_x = mm_nt_res(dh0, win0, dr1, None, a_spec=pl.BlockSpec((tm_ln, Ns0), lambda i, k: (i, k)),
                       w_spec=pl.BlockSpec((1, D, Ns0), lambda i, k: (k, 0, 0)), P=1, tnw=Ns0, n_k=4, tm=tm_ln,
                       name="d_ev_in", carry=[rs_win0[0]])

    per_weight = [[rs_win0], [rs_wout0], [rs_win1], [rs_wout1], [rs_up0, rs_up1], [rs_dn0, rs_dn1]]
    shared = [rs_finish([own for _, own in m], [cp.out for cp, _ in m], name="rs_finish_" + n)
              for n, m in zip(_BIG, per_weight)]
    big_g = {n: s.reshape(w[n].shape) for n, s in zip(_BIG, shared)}

    small_full = {
        'ev_ln_v_g': d_lg, 'ev_ln_v_b': d_lb, 'ev_w_s': d_ws[None], 'ev_b_s': d_bsT.T[None], 'ev_w_pool': d_wp[None],
        'ev_pool_scale': d_sc, 'od_norm_g': d_gn, 'lb_param': d_lbp,
        'ffn_conv_w': jnp.stack([jnp.concatenate([dcw0[0], dcw0[1]], axis=-1),
                                 jnp.concatenate([dcw1[0], dcw1[1]], axis=-1)]),
        'ffn_conv_b': jnp.stack([jnp.concatenate([dcb0[0, 0], dcb0[1, 0]]), jnp.concatenate([dcb1[0, 0], dcb1[1, 0]])]),
        'ln1_g': jnp.concatenate([dg_ln1_0, dg_ln1_1]), 'ln1_b': jnp.concatenate([db_ln1_0, db_ln1_1]),
        'ln2_g': jnp.concatenate([dg_ln2_0, dg_ln2_1]), 'ln2_b': jnp.concatenate([db_ln2_0, db_ln2_1])}
    packed = _pack([small_full[n] for n in _SMALL] + [loss_p[0, 0:1]])
    reduced = _unpack(all_reduce_small(packed, name="all_reduce_small"),
                      [small_full[n].shape for n in _SMALL] + [(1,)])
    small_g = dict(zip(_SMALL, reduced[:-1]))
    loss = reduced[-1][0]
    small_g['ev_w_pool'] = lax.dynamic_slice_in_dim(small_g['ev_w_pool'], chip * (CG // 4), CG // 4, axis=2)
    small_g['ffn_conv_w'] = lax.dynamic_slice_in_dim(small_g['ffn_conv_w'], chip * (F // 2), F // 2, axis=2)
    small_g['od_norm_g'] = lax.dynamic_slice_in_dim(small_g['od_norm_g'], chip * (DC // 4), DC // 4, axis=1)

    grads = {**big_g, **small_g}
    delta, new_m, new_v = {}, {}, {}
    for n in _BIG:
        delta[n], new_m[n], new_v[n] = adamw(w[n], big_g[n], mom[n], vel[n], "adamw_" + n)
    ps = [_pack([d[n] for n in _SMALL]) for d in (w, small_g, mom, vel)]
    upd = adamw(*[p[None] for p in ps], "adamw_small")
    shapes = [w[n].shape for n in _SMALL]
    for d, buf in zip((delta, new_m, new_v), upd):
        d.update(zip(_SMALL, _unpack(buf[0], shapes)))

    return (loss, grad_x[None], *[grads[n] for n in _WEIGHTS], *[delta[n] for n in _WEIGHTS],
            *[new_m[n] for n in _WEIGHTS], *[new_v[n] for n in _WEIGHTS])
```

```python
import math

import jax
import jax.numpy as jnp
from jax import lax
from jax.experimental import pallas as pl
from jax.experimental.pallas import tpu as pltpu

F32 = jnp.float32
BF16 = jnp.bfloat16
MESH = pl.DeviceIdType.MESH
ANY = pl.BlockSpec(memory_space=pl.ANY)
VMEM_SPEC = pl.BlockSpec(memory_space=pltpu.VMEM)

DEPTH = 2
ALPHA = (2 * DEPTH) ** 0.25
LN_EPS = 1e-5
A_HEAD = 128
A_CHUNK = 128
B_WINDOWS = (2, 4, 8, 16)
POOL_HALO = 16
C_HEAD = 128
C_CHUNK = 64
CONV_HALO = 8
ADAM_LR = 0.001
ADAM_B1 = 0.9
ADAM_B2 = 0.999
ADAM_EPS = 1e-08
ADAM_WD = 0.01
ADAM_STEP = 10
V7X_VMEM_LIMIT_BYTES = 56 * 1024 * 1024
LANES = 128
PACK_ALIGN = 8 * LANES


class IciCopy:
    def __init__(self, kind, arr, rows=None):
        self.kind, self.arr, self.out = kind, arr, None
        self.rows = rows


def _carried_copies(items, in_refs, out_refs, send, recv):
    x, y, c, chips = _mesh_pos()
    me = 2 * x + y
    sends, lands = [], []
    for q, (it, src, dst) in enumerate(zip(items, in_refs, out_refs)):
        rows = pl.ds(*(it.rows or (0, it.arr.shape[-2])))
        for k, (cx, cy) in enumerate(chips):
            if it.kind == "gather":
                mine, theirs = dst.at[0, me, c, rows], dst.at[0, 2 * cx + cy, c, rows]
                sends.append(_remote(mine, mine, send, recv, 3 * q + k, (cx, cy, c)))
            else:
                theirs = dst.at[k]
                sends.append(_remote(src.at[k], theirs, send, recv, 3 * q + k, (cx, cy, c)))
            lands.append(_remote(theirs, theirs, send, recv, 3 * q + k, (x, y, c)))
    return sends, lands


def _call(body, *, name, out_shape, grid=(), in_specs=None, out_specs=None, scratch=(), prefetch=0, aliases=None,
          carry=()):
    single = not isinstance(out_specs, (list, tuple))
    in_specs = list(in_specs)
    out_specs = [out_specs] if single else list(out_specs)
    out_shape = [out_shape] if single else list(out_shape)
    scratch = list(scratch)
    aliases = dict(aliases or {})
    n_in, n_out, n_sc, n_c = len(in_specs), len(out_specs), len(scratch), len(carry)
    inner = body
    if n_c:
        assert grid, "a carrier needs a grid"
        for q, it in enumerate(carry):
            if it.kind == "gather":
                aliases[prefetch + n_in + q] = n_out + q
        in_specs += [ANY] * n_c
        out_specs += [ANY] * n_c
        out_shape += [jax.ShapeDtypeStruct(it.arr.shape, it.arr.dtype) for it in carry]
        scratch += [pltpu.SemaphoreType.DMA((3 * n_c,)), pltpu.SemaphoreType.DMA((3 * n_c,))]

        def inner(*refs):
            pre, refs = refs[:prefetch], refs[prefetch:]
            ins, c_in = refs[:n_in], refs[n_in:n_in + n_c]
            outs, c_out = refs[n_in + n_c:n_in + n_c + n_out], refs[n_in + n_c + n_out:n_in + 2 * n_c + n_out]
            rest = refs[n_in + 2 * n_c + n_out:]
            first = last = True
            for d, n in enumerate(grid):
                first = jnp.logical_and(first, pl.program_id(d) == 0)
                last = jnp.logical_and(last, pl.program_id(d) == n - 1)

            @pl.when(first)
            def _():
                for cp in _carried_copies(carry, c_in, c_out, rest[-2], rest[-1])[0]:
                    cp.start()

            body(*pre, *ins, *outs, *rest[:n_sc])

            @pl.when(last)
            def _():
                sends, lands = _carried_copies(carry, c_in, c_out, rest[-2], rest[-1])
                for cp in lands:
                    cp.wait_recv()
                for cp in sends:
                    cp.wait_send()

    spec = pltpu.PrefetchScalarGridSpec(num_scalar_prefetch=prefetch, grid=grid, in_specs=in_specs,
                                        out_specs=out_specs, scratch_shapes=scratch)
    fn = pl.pallas_call(inner, name=name, grid_spec=spec, out_shape=out_shape, input_output_aliases=aliases,
                        compiler_params=pltpu.CompilerParams(vmem_limit_bytes=V7X_VMEM_LIMIT_BYTES))

    def run(*args):
        res = fn(*args, *[it.arr for it in carry])
        for it, o in zip(carry, res[n_out:]):
            it.out = o
        return res[0] if single else list(res[:n_out])

    return run


def _tile(n, pref, unit=LANES):
    if n <= pref:
        return n
    t = (pref // unit) * unit
    while t > unit and n % t:
        t -= unit
    assert n % t == 0, (n, pref, unit)
    return t


def _slabs(n, rows=128):
    return [slice(r, min(r + rows, n)) for r in range(0, n, rows)]


def _dot(a, b):
    return jnp.dot(a, b, preferred_element_type=F32)


def _dot_nt(a, b):
    return lax.dot_general(a, b, (((1,), (1,)), ((), ())), preferred_element_type=F32)


def _dot_tn(a, b):
    return lax.dot_general(a, b, (((0,), (0,)), ((), ())), preferred_element_type=F32)


def _sigmoid(x):
    return jax.nn.sigmoid(x)


_GELU_C = math.sqrt(2.0 / math.pi)


def _gelu(x):
    return 0.5 * x * (1.0 + jnp.tanh(_GELU_C * (x + 0.044715 * x * x * x)))


def _gelu_grad(x):
    th = jnp.tanh(_GELU_C * (x + 0.044715 * x * x * x))
    return 0.5 * (1.0 + th) + 0.5 * x * (1.0 - th * th) * _GELU_C * (1.0 + 3.0 * 0.044715 * x * x)


def _ln_fwd(r, g, b):
    mu = jnp.mean(r, axis=-1, keepdims=True)
    xc = r - mu
    var = jnp.mean(xc * xc, axis=-1, keepdims=True)
    rstd = lax.rsqrt(var + LN_EPS)
    xh = xc * rstd
    return xh * g + b, xh, rstd


def _ln_bwd(dy, xh, rstd, g):
    dxh = dy * g
    m1 = jnp.mean(dxh, axis=-1, keepdims=True)
    m2 = jnp.mean(dxh * xh, axis=-1, keepdims=True)
    dr = rstd * (dxh - m1 - xh * m2)
    return dr, jnp.sum(dy * xh, axis=0, keepdims=True), jnp.sum(dy, axis=0, keepdims=True)


def _exact_tri_dot(tri, x):
    hi = x.astype(BF16)
    r1 = x - hi.astype(F32)
    mid = r1.astype(BF16)
    lo = (r1 - mid.astype(F32)).astype(BF16)
    return _dot(tri, hi) + _dot(tri, mid) + _dot(tri, lo)


def cast_bf16(a3, name):
    L, R, C = a3.shape
    br = _tile(R, max(8, (1 << 20) // C), 8)

    def body(a_ref, o_ref):
        o_ref[...] = a_ref[...].astype(BF16)

    return _call(body, name=name, grid=(L, R // br),
                 in_specs=[pl.BlockSpec((None, br, C), lambda l, r: (l, r, 0))],
                 out_specs=pl.BlockSpec((None, br, C), lambda l, r: (l, r, 0)),
                 out_shape=jax.ShapeDtypeStruct((L, R, C), BF16))(a3)


def adamw(w, g, m, v, name):
    L, R, C = w.shape
    br = _tile(R, max(8, (1 << 19) // C), 8)
    c1 = 1.0 - ADAM_B1 ** ADAM_STEP
    c2 = 1.0 - ADAM_B2 ** ADAM_STEP

    def body(w_ref, g_ref, m_ref, v_ref, d_ref, nm_ref, nv_ref):
        gg = g_ref[...]
        nm = ADAM_B1 * m_ref[...] + (1.0 - ADAM_B1) * gg
        nv = ADAM_B2 * v_ref[...] + (1.0 - ADAM_B2) * (gg * gg)
        d_ref[...] = -ADAM_LR * ((nm / c1) / (jnp.sqrt(nv / c2) + ADAM_EPS) + ADAM_WD * w_ref[...])
        nm_ref[...] = nm
        nv_ref[...] = nv

    spec = pl.BlockSpec((None, br, C), lambda l, r: (l, r, 0))
    sds = jax.ShapeDtypeStruct((L, R, C), F32)
    return _call(body, name=name, grid=(L, R // br), in_specs=[spec] * 4, out_specs=[spec] * 3,
                 out_shape=[sds] * 3)(w, g, m, v)


def mm_nn(a, w, *, w_spec, P, tnw, tm, n_j, name, carry=()):
    T, K = a.shape
    bw = P * tnw

    def body(a_ref, w_ref, o_ref):
        av = a_ref[...]
        for p in range(P):
            o_ref[:, p * tnw:(p + 1) * tnw] = _dot(av, w_ref[p]).astype(BF16)

    return _call(body, name=name, grid=(T // tm, n_j),
                 in_specs=[pl.BlockSpec((tm, K), lambda i, j: (i, 0)), w_spec],
                 out_specs=pl.BlockSpec((tm, bw), lambda i, j: (i, j)),
                 out_shape=jax.ShapeDtypeStruct((T, n_j * bw), BF16), carry=carry)(a, w)


def mm_ln(a, w, res, rg, rb, g, b, *, w_spec, K, tk, tm, name, carry=()):
    T, N = res.shape
    n_k = K // tk

    def body(a_ref, w_ref, res_ref, rg_ref, rb_ref, g_ref, b_ref, xh_ref, y_ref, rs_ref, acc):
        k = pl.program_id(1)

        @pl.when(k == 0)
        def _():
            acc[...] = jnp.zeros_like(acc)

        acc[...] += _dot(a_ref[...], w_ref[...])

        @pl.when(k == n_k - 1)
        def _():
            for rows in _slabs(tm):
                r = ALPHA * (res_ref[rows, :] * rg_ref[...] + rb_ref[...]) + acc[rows, :]
                y, xh, rstd = _ln_fwd(r, g_ref[...], b_ref[...])
                xh_ref[rows, :] = xh
                y_ref[rows, :] = y.astype(BF16)
                rs_ref[rows, :] = rstd

    row = pl.BlockSpec((tm, N), lambda i, k: (i, 0))
    vec = pl.BlockSpec((1, N), lambda i, k: (0, 0))
    return _call(body, name=name, grid=(T // tm, n_k),
                 in_specs=[pl.BlockSpec((tm, tk), lambda i, k: (i, k)), w_spec, row, vec, vec, vec, vec],
                 out_specs=[row, row, pl.BlockSpec((tm, 1), lambda i, k: (i, 0))],
                 out_shape=[jax.ShapeDtypeStruct((T, N), F32), jax.ShapeDtypeStruct((T, N), BF16),
                            jax.ShapeDtypeStruct((T, 1), F32)],
                 scratch=[pltpu.VMEM((tm, N), F32)], carry=carry)(a, w, res, rg, rb, g, b)


def mm_nt_plain(a, w, *, tm, tn, name):
    T, K = a.shape
    N = w.shape[0]

    def body(a_ref, w_ref, o_ref):
        o_ref[...] = _dot_nt(a_ref[...], w_ref[...]).astype(BF16)

    return _call(body, name=name, grid=(T // tm, N // tn),
                 in_specs=[pl.BlockSpec((tm, K), lambda i, j: (i, 0)), pl.BlockSpec((tn, K), lambda i, j: (j, 0))],
                 out_specs=pl.BlockSpec((tm, tn), lambda i, j: (i, j)),
                 out_shape=jax.ShapeDtypeStruct((T, N), BF16))(a, w)


def mm_nt_res(a, w, res, ln, *, a_spec, w_spec, P, tnw, n_k, tm, name, carry=()):
    T, N = res.shape
    n_i = T // tm

    def body(*refs):
        if ln is None:
            a_ref, w_ref, res_ref, o_ref, acc = refs
        else:
            a_ref, w_ref, res_ref, xh_ref, rs_ref, g_ref, dr_ref, drb_ref, dg_ref, db_ref, acc = refs
        i = pl.program_id(0)
        k = pl.program_id(1)

        @pl.when(k == 0)
        def _():
            acc[...] = jnp.zeros_like(acc)

        wv = w_ref[0] if P == 1 else jnp.concatenate([w_ref[p] for p in range(P)], axis=1)
        acc[...] += _dot_nt(a_ref[...], wv)

        @pl.when(k == n_k - 1)
        def _():
            if ln is not None:
                @pl.when(i == 0)
                def _():
                    dg_ref[...] = jnp.zeros_like(dg_ref)
                    db_ref[...] = jnp.zeros_like(db_ref)

            for rows in _slabs(tm):
                d = ALPHA * res_ref[rows, :] + acc[rows, :]
                if ln is None:
                    o_ref[rows, :] = d
                else:
                    dr, dg, db = _ln_bwd(d, xh_ref[rows, :], rs_ref[rows, :], g_ref[...])
                    dr_ref[rows, :] = dr
                    drb_ref[rows, :] = dr.astype(BF16)
                    dg_ref[...] += dg
                    db_ref[...] += db

    row = pl.BlockSpec((tm, N), lambda i, k: (i, 0))
    vec = pl.BlockSpec((1, N), lambda i, k: (0, 0))
    scratch = [pltpu.VMEM((tm, N), F32)]
    if ln is None:
        return _call(body, name=name, grid=(n_i, n_k), in_specs=[a_spec, w_spec, row], out_specs=row,
                     out_shape=jax.ShapeDtypeStruct((T, N), F32), scratch=scratch, carry=carry)(a, w, res)
    xh, rstd, g = ln
    return _call(body, name=name, grid=(n_i, n_k),
                 in_specs=[a_spec, w_spec, row, row, pl.BlockSpec((tm, 1), lambda i, k: (i, 0)), vec],
                 out_specs=[row, row, vec, vec],
                 out_shape=[jax.ShapeDtypeStruct((T, N), F32), jax.ShapeDtypeStruct((T, N), BF16),
                            jax.ShapeDtypeStruct((1, N), F32), jax.ShapeDtypeStruct((1, N), F32)],
                 scratch=scratch, carry=carry)(a, w, res, xh, rstd, g)


def mm_tn(a, g, *, a_spec, g_spec, o_spec, out_shape, grid, acc_shape, P, tnw, name):
    n_t = grid[2]

    def body(a_ref, g_ref, o_ref, acc):
        t = pl.program_id(2)

        @pl.when(t == 0)
        def _():
            acc[...] = jnp.zeros_like(acc)

        acc[...] += _dot_tn(a_ref[...], g_ref[...])

        @pl.when(t == n_t - 1)
        def _():
            for p in range(P):
                o_ref[p] = acc[:, p * tnw:(p + 1) * tnw].astype(BF16)

    return _call(body, name=name, grid=grid, in_specs=[a_spec, g_spec], out_specs=o_spec,
                 out_shape=jax.ShapeDtypeStruct(out_shape, BF16),
                 scratch=[pltpu.VMEM(acc_shape, F32)])(a, g)


def _causal_conv(ext, halo, w, b):
    s1 = pltpu.roll(ext, 1, 0)[halo:]
    s2 = pltpu.roll(ext, 2, 0)[halo:]
    return b + w[2:3] * ext[halo:] + w[1:2] * s1 + w[0:1] * s2, s1, s2


def ffn_up(xb, wup, cw, cb, l, *, name, carry=()):
    T, D = xb.shape
    Ns = wup.shape[-1]
    F = 2 * Ns
    tn = _tile(Ns, 256)
    nps = Ns // tn
    n_j = F // tn
    tm = _tile(T, 1024, 8)

    def body(x_ref, wa_ref, wv_ref, cwa_ref, cwv_ref, cba_ref, cbv_ref, h_ref, f_ref, carry):
        i = pl.program_id(1)

        @pl.when(i == 0)
        def _():
            carry[...] = jnp.zeros_like(carry)

        xv = x_ref[...]
        ha = _dot(xv, wa_ref[...])
        hv = _dot(xv, wv_ref[...])
        ca, _, _ = _causal_conv(jnp.concatenate([carry[0], ha], axis=0), CONV_HALO, cwa_ref[...], cba_ref[...])
        cv, _, _ = _causal_conv(jnp.concatenate([carry[1], hv], axis=0), CONV_HALO, cwv_ref[...], cbv_ref[...])
        carry[0] = ha[tm - CONV_HALO:]
        carry[1] = hv[tm - CONV_HALO:]
        h_ref[0] = ha.astype(BF16)
        h_ref[1] = hv.astype(BF16)
        f_ref[...] = (ca * _sigmoid(ca) * cv).astype(BF16)

    wspec_a = pl.BlockSpec((None, None, D, tn), lambda j, i: (0, j // nps, 0, j % nps))
    wspec_v = pl.BlockSpec((None, None, D, tn), lambda j, i: (0, 2 + j // nps, 0, j % nps))
    return _call(
        body, name=name, grid=(n_j, T // tm),
        in_specs=[pl.BlockSpec((tm, D), lambda j, i: (i, 0)), wspec_a, wspec_v,
                  pl.BlockSpec((None, 3, tn), lambda j, i: (l, 0, j)),
                  pl.BlockSpec((None, 3, tn), lambda j, i: (l, 0, n_j + j)),
                  pl.BlockSpec((None, 1, tn), lambda j, i: (l, 0, j)),
                  pl.BlockSpec((None, 1, tn), lambda j, i: (l, 0, n_j + j))],
        out_specs=[pl.BlockSpec((2, tm, tn), lambda j, i: (0, i, j)), pl.BlockSpec((tm, tn), lambda j, i: (i, j))],
        out_shape=[jax.ShapeDtypeStruct((2, T, F), BF16), jax.ShapeDtypeStruct((T, F), BF16)],
        scratch=[pltpu.VMEM((2, CONV_HALO, tn), F32)], carry=carry)(xb, wup, wup, cw, cw, cb, cb)


def ffn_dgate(db16, wdn, h, cw, cb, l, *, name, carry=()):
    T, D = db16.shape
    F = h.shape[-1]
    tn = _tile(F, 512)
    n_j = F // tn
    tm = _tile(T, 512, 16)
    n_i = T // tm
    hb = 16
    n_ext = tm + CONV_HALO

    def body(d_ref, w_ref, h_ref, hh_ref, cwa_ref, cwv_ref, cba_ref, cbv_ref, dh_ref, dcw_ref, dcb_ref, carry):
        ip = pl.program_id(1)
        i = n_i - 1 - ip

        @pl.when(ip == 0)
        def _():
            carry[...] = jnp.zeros_like(carry)
            dcw_ref[...] = jnp.zeros_like(dcw_ref)
            dcb_ref[...] = jnp.zeros_like(dcb_ref)

        df = _dot_nt(d_ref[...], w_ref[...])
        halo = jnp.where(i > 0, hh_ref[...].astype(F32), 0.0)
        ha = h_ref[0].astype(F32)
        hv = h_ref[1].astype(F32)
        cwa, cwv = cwa_ref[...], cwv_ref[...]
        ca, a1, a2 = _causal_conv(jnp.concatenate([halo[0], ha], axis=0), hb, cwa, cba_ref[...])
        cv, v1, v2 = _causal_conv(jnp.concatenate([halo[1], hv], axis=0), hb, cwv, cbv_ref[...])
        sig = _sigmoid(ca)
        da = df * cv * sig * (1.0 + ca * (1.0 - sig))
        dv = df * ca * sig
        for half, (dc, h0, h1, h2, w) in enumerate(((da, ha, a1, a2, cwa), (dv, hv, v1, v2, cwv))):
            dcb_ref[half] += jnp.sum(dc, axis=0, keepdims=True)
            dcw_ref[half] += jnp.concatenate(
                [jnp.sum(dc * h2, axis=0, keepdims=True), jnp.sum(dc * h1, axis=0, keepdims=True),
                 jnp.sum(dc * h0, axis=0, keepdims=True)], axis=0)
            ext = jnp.concatenate([dc, carry[half]], axis=0)
            n1 = pltpu.roll(ext, n_ext - 1, 0)[:tm]
            n2 = pltpu.roll(ext, n_ext - 2, 0)[:tm]
            dh_ref[half] = (w[2:3] * dc + w[1:2] * n1 + w[0:1] * n2).astype(BF16)
            carry[half] = dc[:CONV_HALO]

    rev = lambda ip: n_i - 1 - ip
    return _call(
        body, name=name, grid=(n_j, n_i),
        in_specs=[pl.BlockSpec((tm, D), lambda j, ip: (rev(ip), 0)),
                  pl.BlockSpec((None, tn, D), lambda j, ip: (0, j, 0)),
                  pl.BlockSpec((2, tm, tn), lambda j, ip: (0, rev(ip), j)),
                  pl.BlockSpec((2, hb, tn), lambda j, ip: (0, jnp.maximum(rev(ip) * (tm // hb) - 1, 0), j)),
                  pl.BlockSpec((None, 3, tn), lambda j, ip: (l, 0, j)),
                  pl.BlockSpec((None, 3, tn), lambda j, ip: (l, 0, n_j + j)),
                  pl.BlockSpec((None, 1, tn), lambda j, ip: (l, 0, j)),
                  pl.BlockSpec((None, 1, tn), lambda j, ip: (l, 0, n_j + j))],
        out_specs=[pl.BlockSpec((2, tm, tn), lambda j, ip: (0, rev(ip), j)),
                   pl.BlockSpec((2, 3, tn), lambda j, ip: (0, 0, j)),
                   pl.BlockSpec((2, 1, tn), lambda j, ip: (0, 0, j))],
        out_shape=[jax.ShapeDtypeStruct((2, T, F), BF16), jax.ShapeDtypeStruct((2, 3, F), F32),
                   jax.ShapeDtypeStruct((2, 1, F), F32)],
        scratch=[pltpu.VMEM((2, CONV_HALO, tn), F32)], carry=carry)(db16, wdn, h, h, cw, cw, cb, cb)


def _pool_fwd(ext, xb_g, t_glob, win):
    e = ext
    sft = 1
    while sft < win:
        e = e + pltpu.roll(e, sft, 0)
        sft *= 2
    cnt = jnp.minimum(t_glob + 1.0, float(win))
    return e[POOL_HALO:] / cnt - xb_g


def gating_fwd(h0, lg, lb, ws, bsT, wp, sc, *, name, carry=()):
    T = h0.shape[0]
    DA = lg.shape[-1]
    DB = sc.shape[-1]
    HA = DA // A_HEAD
    G = len(B_WINDOWS)
    CG = DB // G
    tm = _tile(T, 512, A_CHUNK)
    n_c = tm // A_CHUNK

    def body(h_ref, halo_ref, lg_ref, lb_ref, ws_ref, bsT_ref, wp_ref, sc_ref, cat_ref):
        i = pl.program_id(0)
        hu = h_ref[:, 0:DA].astype(F32)
        hv = h_ref[:, DA:2 * DA].astype(F32)
        xb = h_ref[:, 2 * DA:].astype(F32)
        u = _gelu(hu)
        vn, _, _ = _ln_fwd(_gelu(hv), lg_ref[...], lb_ref[...])
        vnb = vn.astype(BF16)
        rr = lax.broadcasted_iota(jnp.int32, (A_CHUNK, A_CHUNK), 0)
        cc = lax.broadcasted_iota(jnp.int32, (A_CHUNK, A_CHUNK), 1)
        for hh in range(HA):
            wt = jnp.where(rr >= cc, ws_ref[hh], 0.0).astype(BF16)
            cs = slice(hh * A_HEAD, (hh + 1) * A_HEAD)
            for n in range(n_c):
                rs = slice(n * A_CHUNK, (n + 1) * A_CHUNK)
                s = _dot(wt, vnb[rs, cs]) + bsT_ref[:, hh:hh + 1]
                cat_ref[rs, cs] = (u[rs, cs] * s).astype(BF16)
        halo = jnp.where(i > 0, halo_ref[...].astype(F32), 0.0)
        ext = jnp.concatenate([halo, xb], axis=0)
        t_glob = (i * tm + lax.broadcasted_iota(jnp.int32, (tm, 1), 0)).astype(F32)
        for g, win in enumerate(B_WINDOWS):
            gs = slice(g * CG, (g + 1) * CG)
            p = _pool_fwd(ext[:, gs], xb[:, gs], t_glob, win)
            z = _dot(p.astype(BF16), wp_ref[g])
            cat_ref[:, DA + g * CG:DA + (g + 1) * CG] = (z * sc_ref[:, gs]).astype(BF16)

    full = lambda a: pl.BlockSpec(a.shape, lambda i: (0,) * a.ndim)
    hpb = tm // POOL_HALO
    return _call(
        body, name=name, grid=(T // tm,),
        in_specs=[pl.BlockSpec((tm, 2 * DA + DB), lambda i: (i, 0)),
                  pl.BlockSpec((POOL_HALO, DB), lambda i: (jnp.maximum(i * hpb - 1, 0), 2 * DA // DB)),
                  full(lg), full(lb), full(ws), full(bsT), full(wp), full(sc)],
        out_specs=pl.BlockSpec((tm, DA + DB), lambda i: (i, 0)),
        out_shape=jax.ShapeDtypeStruct((T, DA + DB), BF16), carry=carry)(h0, h0, lg, lb, ws, bsT, wp, sc)


def gating_bwd(h0, dcat, lg, lb, ws, wsT, bsT, wp, sc, *, name, carry=()):
    T = h0.shape[0]
    DA = lg.shape[-1]
    DB = sc.shape[-1]
    HA = DA // A_HEAD
    G = len(B_WINDOWS)
    CG = DB // G
    tm = _tile(T, 512, A_CHUNK)
    n_i = T // tm
    n_c = tm // A_CHUNK
    n_ext = tm + POOL_HALO

    def body(h_ref, halo_ref, dc_ref, dhalo_ref, lg_ref, lb_ref, ws_ref, wsT_ref, bsT_ref, wp_ref, sc_ref,
             dh_ref, dws_ref, dbsT_ref, dlg_ref, dlb_ref, dsc_ref, dwp_ref, dvn_sc):
        i = pl.program_id(0)

        @pl.when(i == 0)
        def _():
            for r in (dws_ref, dbsT_ref, dlg_ref, dlb_ref, dsc_ref, dwp_ref):
                r[...] = jnp.zeros_like(r)

        hu = h_ref[:, 0:DA].astype(F32)
        hv = h_ref[:, DA:2 * DA].astype(F32)
        xb = h_ref[:, 2 * DA:].astype(F32)
        u = _gelu(hu)
        gu = _gelu_grad(hu)
        lgv = lg_ref[...]
        vn, vhat, rstd = _ln_fwd(_gelu(hv), lgv, lb_ref[...])
        vnb = vn.astype(BF16)
        rr = lax.broadcasted_iota(jnp.int32, (A_CHUNK, A_CHUNK), 0)
        cc = lax.broadcasted_iota(jnp.int32, (A_CHUNK, A_CHUNK), 1)
        for hh in range(HA):
            wt = jnp.where(rr >= cc, ws_ref[hh], 0.0).astype(BF16)
            wtT = jnp.where(rr <= cc, wsT_ref[hh], 0.0).astype(BF16)
            cs = slice(hh * A_HEAD, (hh + 1) * A_HEAD)
            dws = jnp.zeros((A_CHUNK, A_CHUNK), F32)
            dbs = jnp.zeros((A_CHUNK, 1), F32)
            for n in range(n_c):
                rs = slice(n * A_CHUNK, (n + 1) * A_CHUNK)
                vb = vnb[rs, cs]
                s = _dot(wt, vb) + bsT_ref[:, hh:hh + 1]
                dya = dc_ref[rs, cs].astype(F32)
                ds = dya * u[rs, cs]
                dh_ref[rs, cs] = (dya * s * gu[rs, cs]).astype(BF16)
                dsb = ds.astype(BF16)
                dbs = dbs + jnp.sum(ds, axis=1, keepdims=True)
                dws = dws + _dot_nt(dsb, vb)
                dvn_sc[rs, cs] = _dot(wtT, dsb)
            dws_ref[hh] += jnp.where(rr >= cc, dws, 0.0)
            dbsT_ref[:, hh:hh + 1] += dbs
        dvg, dlg, dlb = _ln_bwd(dvn_sc[...], vhat, rstd, lgv)
        dlg_ref[...] += dlg
        dlb_ref[...] += dlb
        dh_ref[:, DA:2 * DA] = (dvg * _gelu_grad(hv)).astype(BF16)

        halo = jnp.where(i > 0, halo_ref[...].astype(F32), 0.0)
        ext = jnp.concatenate([halo, xb], axis=0)
        t_glob = (i * tm + lax.broadcasted_iota(jnp.int32, (tm, 1), 0)).astype(F32)
        t_ext = (i * tm + lax.broadcasted_iota(jnp.int32, (n_ext, 1), 0)).astype(F32)
        dyb = dc_ref[:, DA:].astype(F32)
        dhalo = jnp.where(i < n_i - 1, dhalo_ref[...].astype(F32), 0.0)
        dyb_ext = jnp.concatenate([dyb, dhalo], axis=0)
        for g, win in enumerate(B_WINDOWS):
            gs = slice(g * CG, (g + 1) * CG)
            pb = _pool_fwd(ext[:, gs], xb[:, gs], t_glob, win).astype(BF16)
            wpg = wp_ref[g]
            z = _dot(pb, wpg)
            dsc_ref[:, gs] += jnp.sum(dyb[:, gs] * z, axis=0, keepdims=True)
            dzb = (dyb_ext[:, gs] * sc_ref[:, gs]).astype(BF16)
            dwp_ref[g] += _dot_tn(pb, dzb[:tm])
            dp = _dot_nt(dzb, wpg)
            e = dp / jnp.minimum(t_ext + 1.0, float(win))
            sft = 1
            while sft < win:
                e = e + pltpu.roll(e, n_ext - sft, 0)
                sft *= 2
            dh_ref[:, 2 * DA + g * CG:2 * DA + (g + 1) * CG] = (e[:tm] - dp[:tm]).astype(BF16)

    full = lambda a: pl.BlockSpec(a.shape, lambda i: (0,) * a.ndim)
    hpb = tm // POOL_HALO
    n_hb = T // POOL_HALO
    outs = [jax.ShapeDtypeStruct((T, 2 * DA + DB), BF16), jax.ShapeDtypeStruct(ws.shape, F32),
            jax.ShapeDtypeStruct(bsT.shape, F32), jax.ShapeDtypeStruct(lg.shape, F32),
            jax.ShapeDtypeStruct(lb.shape, F32), jax.ShapeDtypeStruct(sc.shape, F32),
            jax.ShapeDtypeStruct(wp.shape, F32)]
    return _call(
        body, name=name, grid=(n_i,),
        in_specs=[pl.BlockSpec((tm, 2 * DA + DB), lambda i: (i, 0)),
                  pl.BlockSpec((POOL_HALO, DB), lambda i: (jnp.maximum(i * hpb - 1, 0), 2 * DA // DB)),
                  pl.BlockSpec((tm, DA + DB), lambda i: (i, 0)),
                  pl.BlockSpec((POOL_HALO, DB), lambda i: (jnp.minimum((i + 1) * hpb, n_hb - 1), DA // DB)),
                  full(lg), full(lb), full(ws), full(wsT), full(bsT), full(wp), full(sc)],
        out_specs=[pl.BlockSpec((tm, 2 * DA + DB), lambda i: (i, 0))] + [full(o) for o in outs[1:]],
        out_shape=outs,
        scratch=[pltpu.VMEM((tm, DA), F32)], carry=carry)(h0, h0, dcat, dcat, lg, lb, ws, wsT, bsT, wp, sc)


HP = 2 * C_HEAD


def _hgrn_gates(blk, lbv, tri):
    q = blk[:, 0:HP]
    sg = _sigmoid(blk[:, HP:2 * HP])
    f = lbv + (1.0 - lbv) * sg
    bcum = _exact_tri_dot(tri, jnp.log(f))
    blast = bcum[C_CHUNK - 1:C_CHUNK]
    sq = _sigmoid(q)
    k = 1.0 - f
    e_in = jnp.exp(bcum)
    e_out = jnp.exp(-bcum)
    e_end = jnp.exp(blast - bcum)
    return dict(q=q, sq=sq, sg=sg, f=f, k=k, bcum=bcum, blast=blast, e_in=e_in, e_out=e_out, e_end=e_end,
                qd=q * sq * e_in, kd=k * e_out, ke=k * e_end, dec=jnp.exp(blast), v=blk[:, 2 * HP:3 * HP],
                gg=blk[:, 3 * HP:4 * HP])


def _lower_bound(lbp_ref):
    p0, p1 = lbp_ref[0:1], lbp_ref[1:2]
    mx = jnp.maximum(p0, p1)
    e0, e1 = jnp.exp(p0 - mx), jnp.exp(p1 - mx)
    return e1 / (e0 + e1)


def hgrn_fwd(h1p, lbp, gn, *, name, carry=()):
    T = h1p.shape[0]
    DC = gn.shape[-1]
    n_p = DC // HP
    tt = _tile(T, 512, C_CHUNK)
    n_c = tt // C_CHUNK

    def body(h_ref, lbp_ref, gn_ref, y_ref, o_ref, sp_ref, st):
        i = pl.program_id(1)

        @pl.when(i == 0)
        def _():
            st[...] = jnp.zeros_like(st)

        lbv = _lower_bound(lbp_ref)
        gnv = gn_ref[...]
        rr = lax.broadcasted_iota(jnp.int32, (C_CHUNK, C_CHUNK), 0)
        cc = lax.broadcasted_iota(jnp.int32, (C_CHUNK, C_CHUNK), 1)
        causal = rr >= cc
        tri = jnp.where(causal, 1.0, 0.0).astype(BF16)

        def chunk(n, _):
            rows = pl.ds(pl.multiple_of(n * C_CHUNK, C_CHUNK), C_CHUNK)
            a = _hgrn_gates(h_ref[rows, :].astype(F32), lbv, tri)
            outs = []
            for hd in range(2):
                cs = slice(hd * C_HEAD, (hd + 1) * C_HEAD)
                qd, kd, ke = a["qd"][:, cs].astype(BF16), a["kd"][:, cs].astype(BF16), a["ke"][:, cs].astype(BF16)
                vb = a["v"][:, cs].astype(BF16)
                s_t = st[hd]
                att = jnp.where(causal, _dot_nt(qd, kd), 0.0).astype(BF16)
                outs.append(_dot(att, vb) + _dot_nt(qd, s_t.astype(BF16)))
                sp_ref[hd, n] = s_t
                st[hd] = a["dec"][:, cs] * s_t + _dot_tn(vb, ke)
            o = jnp.concatenate(outs, axis=1)
            o_ref[rows, :] = o.astype(BF16)
            ys = []
            for hd in range(2):
                oh = o[:, hd * C_HEAD:(hd + 1) * C_HEAD]
                ys.append(oh * lax.rsqrt(jnp.mean(oh * oh, axis=-1, keepdims=True) + LN_EPS))
            y_ref[rows, :] = (jnp.concatenate(ys, axis=1) * gnv * _sigmoid(a["gg"])).astype(BF16)
            return 0

        lax.fori_loop(0, n_c, chunk, 0, unroll=4)

    return _call(
        body, name=name, grid=(n_p, T // tt),
        in_specs=[pl.BlockSpec((tt, 4 * HP), lambda p, i: (i, p)), pl.BlockSpec((2, HP), lambda p, i: (0, p)),
                  pl.BlockSpec((1, HP), lambda p, i: (0, p))],
        out_specs=[pl.BlockSpec((tt, HP), lambda p, i: (i, p)), pl.BlockSpec((tt, HP), lambda p, i: (i, p)),
                   pl.BlockSpec((2, n_c, C_HEAD, C_HEAD), lambda p, i: (p, i, 0, 0))],
        out_shape=[jax.ShapeDtypeStruct((T, DC), BF16), jax.ShapeDtypeStruct((T, DC), BF16),
                   jax.ShapeDtypeStruct((2 * n_p, T // C_CHUNK, C_HEAD, C_HEAD), F32)],
        scratch=[pltpu.VMEM((2, C_HEAD, C_HEAD), F32)], carry=carry)(h1p, lbp, gn)


def hgrn_bwd(h1p, o_saved, dy, sp, lbp, gn, *, name, carry=()):
    T = h1p.shape[0]
    DC = gn.shape[-1]
    n_p = DC // HP
    tt = _tile(T, 512, C_CHUNK)
    n_i = T // tt
    n_c = tt // C_CHUNK

    def body(h_ref, o_ref, dy_ref, sp_ref, lbp_ref, gn_ref, dh_ref, dgn_ref, dlbp_ref, dst, dlb_acc):
        ip = pl.program_id(1)

        @pl.when(ip == 0)
        def _():
            dst[...] = jnp.zeros_like(dst)
            dlb_acc[...] = jnp.zeros_like(dlb_acc)
            dgn_ref[...] = jnp.zeros_like(dgn_ref)

        lbv = _lower_bound(lbp_ref)
        gnv = gn_ref[...]
        rr = lax.broadcasted_iota(jnp.int32, (C_CHUNK, C_CHUNK), 0)
        cc = lax.broadcasted_iota(jnp.int32, (C_CHUNK, C_CHUNK), 1)
        causal = rr >= cc
        tri = jnp.where(causal, 1.0, 0.0).astype(BF16)
        tri_t = jnp.where(rr <= cc, 1.0, 0.0).astype(BF16)
        last_row = lax.broadcasted_iota(jnp.int32, (C_CHUNK, 1), 0) == C_CHUNK - 1

        def chunk(m, _):
            n = n_c - 1 - m
            rows = pl.ds(pl.multiple_of(n * C_CHUNK, C_CHUNK), C_CHUNK)
            a = _hgrn_gates(h_ref[rows, :].astype(F32), lbv, tri)
            o = o_ref[rows, :].astype(F32)
            dyv = dy_ref[rows, :].astype(F32)
            sgg = _sigmoid(a["gg"])
            ohs, rrs = [], []
            for hd in range(2):
                oh = o[:, hd * C_HEAD:(hd + 1) * C_HEAD]
                r = lax.rsqrt(jnp.mean(oh * oh, axis=-1, keepdims=True) + LN_EPS)
                ohs.append(oh * r)
                rrs.append(r)
            ohat = jnp.concatenate(ohs, axis=1)
            dyn = dyv * sgg
            dgg = dyv * ohat * gnv * sgg * (1.0 - sgg)
            dgn_ref[...] += jnp.sum(dyn * ohat, axis=0, keepdims=True)
            dxh = dyn * gnv
            dqd_l, dkd_l, dke_l, dv_l, ddec_l = [], [], [], [], []
            for hd in range(2):
                cs = slice(hd * C_HEAD, (hd + 1) * C_HEAD)
                dxh_h, oh_h = dxh[:, cs], ohat[:, cs]
                do = rrs[hd] * (dxh_h - oh_h * jnp.mean(dxh_h * oh_h, axis=-1, keepdims=True))
                dob = do.astype(BF16)
                qd, kd, ke = a["qd"][:, cs].astype(BF16), a["kd"][:, cs].astype(BF16), a["ke"][:, cs].astype(BF16)
                vb = a["v"][:, cs].astype(BF16)
                s_t = sp_ref[hd, n]
                ds_out = dst[hd]
                ds_outb = ds_out.astype(BF16)
                att = jnp.where(causal, _dot_nt(qd, kd), 0.0).astype(BF16)
                datt = jnp.where(causal, _dot_nt(dob, vb), 0.0).astype(BF16)
                dv_l.append(_dot_tn(att, dob) + _dot_nt(ke, ds_outb))
                dqd_l.append(_dot(datt, kd) + _dot(dob, s_t.astype(BF16)))
                dkd_l.append(_dot_tn(datt, qd))
                dke_l.append(_dot(vb, ds_outb))
                ddec_l.append(jnp.sum(ds_out * s_t, axis=0, keepdims=True))
                dst[hd] = a["dec"][:, cs] * ds_out + _dot_tn(dob, qd)
            dqd = jnp.concatenate(dqd_l, axis=1)
            dkd = jnp.concatenate(dkd_l, axis=1)
            dke = jnp.concatenate(dke_l, axis=1)
            dv = jnp.concatenate(dv_l, axis=1)
            ddec = jnp.concatenate(ddec_l, axis=1)
            dqs = dqd * a["e_in"]
            kek = dke * a["ke"]
            dbcum = dqd * a["qd"] - dkd * a["kd"] - kek
            dk = dkd * a["e_out"] + dke * a["e_end"]
            dblast = jnp.sum(kek, axis=0, keepdims=True) + ddec * a["dec"]
            dbcum = dbcum + jnp.where(last_row, dblast, 0.0)
            dlf = _exact_tri_dot(tri_t, dbcum)
            df = dlf / a["f"] - dk
            dlb_acc[...] += jnp.sum(df * (1.0 - a["sg"]), axis=0, keepdims=True)
            dfl = df * (1.0 - lbv) * a["sg"] * (1.0 - a["sg"])
            dq = dqs * a["sq"] * (1.0 + a["q"] * (1.0 - a["sq"]))
            dh_ref[rows, :] = jnp.concatenate([dq, dfl, dv, dgg], axis=1).astype(BF16)
            return 0

        lax.fori_loop(0, n_c, chunk, 0, unroll=4)

        @pl.when(ip == n_i - 1)
        def _():
            d1 = dlb_acc[...] * lbv * (1.0 - lbv)
            dlbp_ref[...] = jnp.concatenate([-d1, d1], axis=0)

    rev = lambda ip: n_i - 1 - ip
    return _call(
        body, name=name, grid=(n_p, n_i),
        in_specs=[pl.BlockSpec((tt, 4 * HP), lambda p, ip: (rev(ip), p)),
                  pl.BlockSpec((tt, HP), lambda p, ip: (rev(ip), p)),
                  pl.BlockSpec((tt, HP), lambda p, ip: (rev(ip), p)),
                  pl.BlockSpec((2, n_c, C_HEAD, C_HEAD), lambda p, ip: (p, rev(ip), 0, 0)),
                  pl.BlockSpec((2, HP), lambda p, ip: (0, p)), pl.BlockSpec((1, HP), lambda p, ip: (0, p))],
        out_specs=[pl.BlockSpec((tt, 4 * HP), lambda p, ip: (rev(ip), p)),
                   pl.BlockSpec((1, HP), lambda p, ip: (0, p)), pl.BlockSpec((2, HP), lambda p, ip: (0, p))],
        out_shape=[jax.ShapeDtypeStruct(h1p.shape, BF16), jax.ShapeDtypeStruct((1, DC), F32),
                   jax.ShapeDtypeStruct((2, DC), F32)],
        scratch=[pltpu.VMEM((2, C_HEAD, C_HEAD), F32), pltpu.VMEM((1, HP), F32)], carry=carry)(h1p, o_saved, dy, sp, lbp, gn)


def loss_bwd(xh, rstd, g, b, target, *, name):
    T, N = xh.shape
    tm = _tile(T, 512, 8)

    def body(xh_ref, rs_ref, g_ref, b_ref, t_ref, ls_ref, dr_ref, drb_ref, dg_ref, db_ref):
        i = pl.program_id(0)

        @pl.when(i == 0)
        def _():
            for r in (ls_ref, dg_ref, db_ref):
                r[...] = jnp.zeros_like(r)

        xhv, gv = xh_ref[...], g_ref[...]
        e = xhv * gv + b_ref[...] - t_ref[...]
        ls_ref[...] += 0.5 * jnp.sum(jnp.mean(e * e, axis=-1, keepdims=True), axis=0, keepdims=True)
        dr, dg, db = _ln_bwd(e / N, xhv, rs_ref[...], gv)
        dr_ref[...] = dr
        drb_ref[...] = dr.astype(BF16)
        dg_ref[...] += dg
        db_ref[...] += db

    row = pl.BlockSpec((tm, N), lambda i: (i, 0))
    vec = pl.BlockSpec((1, N), lambda i: (0, 0))
    return _call(body, name=name, grid=(T // tm,),
                 in_specs=[row, pl.BlockSpec((tm, 1), lambda i: (i, 0)), vec, vec, row],
                 out_specs=[pl.BlockSpec((1, LANES), lambda i: (0, 0)), row, row, vec, vec],
                 out_shape=[jax.ShapeDtypeStruct((1, LANES), F32), jax.ShapeDtypeStruct((T, N), F32),
                            jax.ShapeDtypeStruct((T, N), BF16), jax.ShapeDtypeStruct((1, N), F32),
                            jax.ShapeDtypeStruct((1, N), F32)])(xh, rstd, g, b, target)


def _mesh_pos():
    x, y, c = lax.axis_index("x"), lax.axis_index("y"), lax.axis_index("c")
    chips = [(1 - x, y), (x, 1 - y), (1 - x, 1 - y)]
    return x, y, c, chips


def _remote(src, dst, send, recv, j, dev):
    return pltpu.make_async_remote_copy(src_ref=src, dst_ref=dst, send_sem=send.at[j], recv_sem=recv.at[j],
                                        device_id=dev, device_id_type=MESH)


def mesh_ids():
    x, y, c, chips = _mesh_pos()
    return jnp.stack([c] + [2 * cx + cy for cx, cy in chips] + [2 * x + y]).astype(jnp.int32)


def _sibling():
    return (lax.axis_index("x"), lax.axis_index("y"), 1 - lax.axis_index("c"))


def _swap_with_sibling(src, recv, send_sem, recv_sem, step):
    slot = step % 2
    cp = pltpu.make_async_remote_copy(src_ref=src, dst_ref=recv.at[slot], send_sem=send_sem.at[slot],
                                      recv_sem=recv_sem.at[slot], device_id=_sibling(), device_id_type=MESH)
    cp.start()
    cp.wait_recv()
    return cp, slot


def _swap_scratch(br, C, dtype):
    return [pltpu.VMEM((2, br, C), dtype), pltpu.SemaphoreType.DMA((2,)), pltpu.SemaphoreType.DMA((2,))]


def _swap_rows(Rh, C, itemsize):
    return _tile(Rh, max(16, (2 << 20) // (C * itemsize)), 16)


def cast_to_slot(a3, l, me1, name):
    _, R, C = a3.shape
    br = _tile(R, max(8, (1 << 20) // C), 16)

    def body(me_ref, a_ref, o_ref):
        o_ref[...] = a_ref[...].astype(BF16)

    return _call(body, name=name, grid=(R // br,), prefetch=1,
                 in_specs=[pl.BlockSpec((None, br, C), lambda r, me: (l, r, 0))],
                 out_specs=pl.BlockSpec((None, None, br, C), lambda r, me: (0, me[0], r, 0)),
                 out_shape=jax.ShapeDtypeStruct((1, 4, R, C), BF16))(me1, a3)


def all_gather_chips(big, small, *, name):
    nb, ns = len(big), len(small)
    layers = [(t, l) for t in range(nb) for l in range(big[t].shape[0])]
    n_big = 3 * len(layers)
    n_rem = n_big + 3 * ns

    def body(*refs):
        small_in = refs[nb:nb + ns]
        bufs, small_out = refs[nb + ns:2 * nb + ns], refs[2 * nb + ns:2 * (nb + ns)]
        send, recv, loc = refs[2 * (nb + ns):]
        x, y, c, chips = _mesh_pos()
        me = 2 * x + y
        ids = [2 * cx + cy for cx, cy in chips]
        started, sends = [], []
        for t in range(ns):
            cp = pltpu.make_async_copy(small_in[t], small_out[t].at[me], loc.at[t])
            cp.start()
            started.append(cp)
        for q, (t, l) in enumerate(layers):
            for k, chip in enumerate(chips):
                blk = bufs[t].at[l, me, c]
                cp = _remote(blk, blk, send, recv, 3 * q + k, (*chip, c))
                cp.start()
                sends.append(cp)
        for t in range(ns):
            for k, chip in enumerate(chips):
                cp = _remote(small_in[t], small_out[t].at[me], send, recv, n_big + 3 * t + k, (*chip, c))
                cp.start()
                sends.append(cp)
        for q, (t, l) in enumerate(layers):
            for k in range(3):
                blk = bufs[t].at[l, ids[k], c]
                _remote(blk, blk, send, recv, 3 * q + k, (x, y, c)).wait_recv()
        for t in range(ns):
            for k in range(3):
                blk = small_out[t].at[ids[k]]
                _remote(blk, blk, send, recv, n_big + 3 * t + k, (x, y, c)).wait_recv()
        for cp in sends:
            cp.wait_send()
        for cp in started:
            cp.wait()

    out_shape = [jax.ShapeDtypeStruct(a.shape, a.dtype) for a in big]
    out_shape += [jax.ShapeDtypeStruct((4,) + a.shape, a.dtype) for a in small]
    return _call(body, name=name, in_specs=[ANY] * (nb + ns), out_specs=[ANY] * (nb + ns), out_shape=out_shape,
                 aliases={t: t for t in range(nb)},
                 scratch=[pltpu.SemaphoreType.DMA((n_rem,)), pltpu.SemaphoreType.DMA((n_rem,)),
                          pltpu.SemaphoreType.DMA((max(ns, 1),))])(*big, *small)


def all_gather_pair(buf, ids, *, name):
    L, _, _, Rh, C = buf.shape
    br = _swap_rows(Rh, C, 2)
    n_r = Rh // br

    def body(ids_ref, in_ref, o_ref, recv, ssem, rsem):
        step = (pl.program_id(0) * 3 + pl.program_id(1)) * n_r + pl.program_id(2)
        cp, slot = _swap_with_sibling(in_ref, recv, ssem, rsem, step)
        o_ref[...] = recv[slot]
        cp.wait_send()

    at = lambda l, s, h, r: (((l * 4 + s) * 2 + h) * n_r + r, 0)
    out = _call(body, name=name, grid=(L, 3, n_r), prefetch=1,
                in_specs=[pl.BlockSpec((br, C), lambda l, k, r, ids: at(l, ids[1 + k], ids[0], r))],
                out_specs=pl.BlockSpec((br, C), lambda l, k, r, ids: at(l, ids[1 + k], 1 - ids[0], r)),
                out_shape=jax.ShapeDtypeStruct((L * 8 * Rh, C), buf.dtype), aliases={1: 0},
                scratch=_swap_scratch(br, C, BF16))(ids, buf.reshape(L * 8 * Rh, C))
    return out.reshape(buf.shape)


def rs_pair_add(grad, ids, *, name):
    _, _, Rh, C = grad.shape
    br = _swap_rows(Rh, C, 2)
    n_r = Rh // br

    def body(ids_ref, send_ref, keep_ref, pb_ref, own_ref, recv, ssem, rsem):
        ph = pl.program_id(0)
        cp, slot = _swap_with_sibling(send_ref, recv, ssem, rsem, ph * n_r + pl.program_id(1))
        s = keep_ref[...].astype(F32) + recv[slot].astype(F32)

        @pl.when(ph == 0)
        def _():
            own_ref[...] = s

        @pl.when(ph > 0)
        def _():
            pb_ref[...] = s.astype(BF16)

        cp.wait_send()

    rel = lambda ph: (ph + 3) % 4
    at = lambda s, h, r: ((s * 2 + h) * n_r + r, 0)
    grad = grad.reshape(8 * Rh, C)
    return _call(
        body, name=name, grid=(4, n_r), prefetch=1,
        in_specs=[pl.BlockSpec((br, C), lambda ph, r, ids: at(ids[1 + rel(ph)], 1 - ids[0], r)),
                  pl.BlockSpec((br, C), lambda ph, r, ids: at(ids[1 + rel(ph)], ids[0], r))],
        out_specs=[pl.BlockSpec((None, br, C), lambda ph, r, ids: (jnp.maximum(ph - 1, 0), jnp.where(ph == 0, 0, r), 0)),
                   pl.BlockSpec((br, C), lambda ph, r, ids: (jnp.where(ph == 0, r, n_r - 1), 0))],
        out_shape=[jax.ShapeDtypeStruct((3, Rh, C), BF16), jax.ShapeDtypeStruct((Rh, C), F32)],
        scratch=_swap_scratch(br, C, BF16))(ids, grad, grad)


def rs_finish(owns, gots, *, name):
    L = len(owns)
    Rh, C = owns[0].shape
    br = _swap_rows(Rh, C, 4)
    n_r = Rh // br

    def body(*refs):
        own_refs, got_refs, o_ref = refs[:L], refs[L:2 * L], refs[2 * L]
        recv, ssem, rsem = refs[2 * L + 1:]
        l = pl.program_id(0)
        c = lax.axis_index("c")
        for ll in range(L):
            @pl.when(l == ll)
            def _():
                s = own_refs[ll][...]
                for k in range(3):
                    s = s + got_refs[ll][k].astype(F32)
                o_ref[c] = s

        cp, slot = _swap_with_sibling(o_ref.at[c], recv, ssem, rsem, l * n_r + pl.program_id(1))
        o_ref[1 - c] = recv[slot]
        cp.wait_send()

    def at_layer(ll):
        return lambda l, r: jnp.where(l == ll, r, jnp.where(l < ll, 0, n_r - 1))

    in_specs = [pl.BlockSpec((br, C), lambda l, r, ll=ll: (at_layer(ll)(l, r), 0)) for ll in range(L)]
    in_specs += [pl.BlockSpec((3, br, C), lambda l, r, ll=ll: (0, at_layer(ll)(l, r), 0)) for ll in range(L)]
    out = _call(body, name=name, grid=(L, n_r), in_specs=in_specs,
                out_specs=pl.BlockSpec((2, br, C), lambda l, r: (l, r, 0)),
                out_shape=jax.ShapeDtypeStruct((L * 2, Rh, C), F32),
                scratch=_swap_scratch(br, C, F32))(*owns, *gots)
    return out.reshape(L, 2, Rh, C)


def all_reduce_small(buf, *, name):
    rows = buf.shape[0]

    def body(x_ref, o_ref, rbuf, send, recv):
        x, y, c, _ = _mesh_pos()
        o_ref[...] = x_ref[...]
        for k, dev in enumerate(((x, y, 1 - c), (1 - x, y, c), (x, 1 - y, c))):
            cp = _remote(o_ref, rbuf.at[k], send, recv, k, dev)
            cp.start()
            cp.wait()
            o_ref[...] = o_ref[...] + rbuf[k]

    return _call(body, name=name, in_specs=[VMEM_SPEC], out_specs=VMEM_SPEC,
                 out_shape=jax.ShapeDtypeStruct(buf.shape, F32),
                 scratch=[pltpu.VMEM((3, rows, LANES), F32), pltpu.SemaphoreType.DMA((3,)),
                          pltpu.SemaphoreType.DMA((3,))])(buf)


def _pack(arrs):
    parts = []
    for a in arrs:
        f = a.reshape(-1).astype(F32)
        parts.append(jnp.pad(f, (0, (-f.shape[0]) % PACK_ALIGN)))
    return jnp.concatenate(parts).reshape(-1, LANES)


def _unpack(buf, shapes):
    flat = buf.reshape(-1)
    out, off = [], 0
    for s in shapes:
        n = math.prod(s)
        out.append(flat[off:off + n].reshape(s))
        off += n + (-n) % PACK_ALIGN
    return out


_WEIGHTS = ['ev_w_in', 'ev_ln_v_g', 'ev_ln_v_b', 'ev_w_s', 'ev_b_s', 'ev_w_pool', 'ev_pool_scale', 'ev_w_out',
            'od_w_in', 'od_norm_g', 'od_w_out', 'lb_param', 'ffn_w_up', 'ffn_conv_w', 'ffn_conv_b', 'ffn_w_down',
            'ln1_g', 'ln1_b', 'ln2_g', 'ln2_b']
_BIG = ['ev_w_in', 'ev_w_out', 'od_w_in', 'od_w_out', 'ffn_w_up', 'ffn_w_down']
_SMALL = [n for n in _WEIGHTS if n not in _BIG]


def kernel(x, ev_w_in, ev_ln_v_g, ev_ln_v_b, ev_w_s, ev_b_s, ev_w_pool, ev_pool_scale, ev_w_out, od_w_in, od_norm_g, od_w_out, lb_param, ffn_w_up, ffn_conv_w, ffn_conv_b, ffn_w_down, ln1_g, ln1_b, ln2_g, ln2_b, loss_target, m_ev_w_in, m_ev_ln_v_g, m_ev_ln_v_b, m_ev_w_s, m_ev_b_s, m_ev_w_pool, m_ev_pool_scale, m_ev_w_out, m_od_w_in, m_od_norm_g, m_od_w_out, m_lb_param, m_ffn_w_up, m_ffn_conv_w, m_ffn_conv_b, m_ffn_w_down, m_ln1_g, m_ln1_b, m_ln2_g, m_ln2_b, v_ev_w_in, v_ev_ln_v_g, v_ev_ln_v_b, v_ev_w_s, v_ev_b_s, v_ev_w_pool, v_ev_pool_scale, v_ev_w_out, v_od_w_in, v_od_norm_g, v_od_w_out, v_lb_param, v_ffn_w_up, v_ffn_conv_w, v_ffn_conv_b, v_ffn_w_down, v_ln1_g, v_ln1_b, v_ln2_g, v_ln2_b):
    given = dict(locals())
    w = {n: given[n] for n in _WEIGHTS}
    mom = {n: given["m_" + n] for n in _WEIGHTS}
    vel = {n: given["v_" + n] for n in _WEIGHTS}
    x2d = x[0]
    tgt = loss_target[0]
    T, D = x2d.shape
    DA = ev_ln_v_g.shape[-1]
    DB = ev_pool_scale.shape[-1]
    HA = ev_w_s.shape[1]
    G = len(B_WINDOWS)
    CG = DB // G
    DC = 4 * od_norm_g.shape[-1]
    F = ffn_conv_b.shape[-1] // 2
    chip = 2 * lax.axis_index("x") + lax.axis_index("y")

    ids = mesh_ids()
    halves = lambda a: a.reshape(1, 4, 2, a.shape[2] // 2, a.shape[3])
    slot = {(n, l): halves(cast_to_slot(w[n], l, ids[4:5], f"cast_{n}{l}"))
            for n in _BIG for l in range(w[n].shape[0])}

    def riding(*keys):
        return [IciCopy("gather", slot[k]) for k in keys]

    def pair(buf, key):
        g = all_gather_pair(buf, ids, name=f"all_gather_pair_{key[0]}{key[1]}")
        return g.reshape(1, 4, g.shape[3] * 2, g.shape[4])

    early = [('ev_w_in', 0), ('ev_w_out', 0)]
    gathered = all_gather_chips([slot[k] for k in early], [ev_w_pool[0], ffn_conv_w, od_norm_g],
                                name="all_gather_chips")
    wpool_full = gathered[2].transpose(1, 0, 2, 3).reshape(G, CG, CG)
    cw_full = gathered[3].transpose(1, 2, 0, 3).reshape(DEPTH, 3, 2 * F)
    gn_full = gathered[4].reshape(1, DC)
    win0 = pair(gathered[0], early[0])[0]
    wout0 = pair(gathered[1], early[1]).reshape(DA + DB, D)
    wup, wdn = {}, {}
    rh_up = D // 2
    cut1 = (rh_up * 35 // 100) // 16 * 16
    cut2 = cut1 + (rh_up * 18 // 100) // 16 * 16
    cb3 = ffn_conv_b.reshape(DEPTH, 1, 2 * F)
    ws = ev_w_s[0]
    wsT = jnp.swapaxes(ws, 1, 2)
    bsT = ev_b_s[0].T
    wpb = wpool_full.astype(BF16)
    ones = jnp.ones((1, D), F32)
    zeros = jnp.zeros((1, D), F32)
    row = lambda a, l: a[l:l + 1]

    Ns0 = win0.shape[-1]
    Nu = ffn_w_up.shape[-1]
    tm_big = _tile(T, 1024, 8)
    tm_ln = _tile(T, 512, 8)
    tk_ln = _tile(D, 512)
    n_p = DC // HP

    def nat_spec(Ns, tnw):
        nps = Ns // tnw
        return pl.BlockSpec((1, D, tnw), lambda i, j: (j // nps, 0, j % nps))

    perm_spec = pl.BlockSpec((4, D, HP), lambda i, j: (0, 0, j))

    xb16 = cast_bf16(x, "cast_x")[0]
    up0 = IciCopy("gather", slot[('ffn_w_up', 0)], rows=(0, cut1))
    h0 = mm_nn(xb16, win0, w_spec=nat_spec(Ns0, Ns0), P=1, tnw=Ns0, tm=tm_big, n_j=4, name="ev_in", carry=[up0])
    up0 = IciCopy("gather", up0.out, rows=(cut1, cut2 - cut1))
    cat = gating_fwd(h0, ev_ln_v_g, ev_ln_v_b, ws, bsT, wpb, ev_pool_scale, name="gating_fwd", carry=[up0])
    up0 = IciCopy("gather", up0.out, rows=(cut2, rh_up - cut2))

    def mix_ln(a, wmat, res, l, name, carry=()):
        K = a.shape[1]
        tk = _tile(K, 2048)
        return mm_ln(a, wmat, *res, row(ln1_g, l), row(ln1_b, l), w_spec=pl.BlockSpec((tk, D), lambda i, k: (k, 0)),
                     K=K, tk=tk, tm=tm_ln, name=name, carry=carry)

    def ffn_down(f, res, l, carry=()):
        tk = F // 4 if (F // 4) % LANES == 0 else _tile(F, 512)
        return mm_ln(f, wdn[l], *res, row(ln2_g, l), row(ln2_b, l),
                     w_spec=pl.BlockSpec((None, tk, D), lambda i, k: (0, k, 0)), K=F, tk=tk, tm=tm_ln,
                     name=f"ffn_down{l}", carry=carry)

    xh1, y1, rs1 = mix_ln(cat, wout0, (x2d, ones, zeros), 0, "ev_out", carry=[up0])
    wup[0] = pair(up0.out, ('ffn_w_up', 0))
    res1 = (xh1, row(ln1_g, 0), row(ln1_b, 0))
    ride = riding(('ffn_w_down', 0), ('od_w_in', 0))
    hf0, f0 = ffn_up(y1, wup[0], cw_full, cb3, 0, name="ffn_up0", carry=ride)
    wdn[0] = pair(ride[0].out, ('ffn_w_down', 0)).reshape(1, F, D)
    win1 = pair(ride[1].out, ('od_w_in', 0))[0]
    ride = riding(('od_w_out', 0), ('ffn_w_down', 1))
    xh2, y2, rs2 = ffn_down(f0, res1, 0, carry=ride)
    wout1 = pair(ride[0].out, ('od_w_out', 0)).reshape(DC, D)
    wdn[1] = pair(ride[1].out, ('ffn_w_down', 1)).reshape(1, F, D)
    res2 = (xh2, row(ln2_g, 0), row(ln2_b, 0))
    h1p = mm_nn(y2, win1, w_spec=perm_spec, P=4, tnw=HP, tm=tm_big, n_j=n_p, name="od_in")
    ride = riding(('ffn_w_up', 1))
    yh, o_saved, sp = hgrn_fwd(h1p, lb_param, gn_full, name="hgrn_fwd", carry=ride)
    wup[1] = pair(ride[0].out, ('ffn_w_up', 1))
    xh3, y3, rs3 = mix_ln(yh, wout1, res2, 1, "od_out")
    res3 = (xh3, row(ln1_g, 1), row(ln1_b, 1))
    hf1, f1 = ffn_up(y3, wup[1], cw_full, cb3, 1, name="ffn_up1")
    xh4, y4, rs4 = ffn_down(f1, res3, 1)
    loss_p, dr, drb, dg_ln2_1, db_ln2_1 = loss_bwd(xh4, rs4, row(ln2_g, 1), row(ln2_b, 1), tgt, name="loss_bwd")

    tt = _tile(T, 512, 16)
    n_t = T // tt
    tnu = Nu // 2 if (Nu // 2) % LANES == 0 else Nu
    upb = Nu // tnu
    tkd = _tile(D, 1024)

    def pair_sum(g4, name):
        pb, own = rs_pair_add(g4.reshape(4, 2, g4.shape[1] // 2, g4.shape[2]), ids, name="rs_pair_add_" + name)
        return IciCopy("scatter", pb), own

    def ffn_bwd(l, dr2, dr2b, f, hf, y_in, xh_in, rs_in):
        tkf = F // 4
        g_dn = mm_tn(f, dr2b, a_spec=pl.BlockSpec((tt, tkf), lambda kb, nb, t: (t, kb)),
                     g_spec=pl.BlockSpec((tt, D), lambda kb, nb, t: (t, 0)),
                     o_spec=pl.BlockSpec((1, tkf, D), lambda kb, nb, t: (0, kb, 0)), out_shape=(1, F, D),
                     grid=(4, 1, n_t), acc_shape=(tkf, D), P=1, tnw=D, name=f"g_ffn_down{l}")
        rs_dn = pair_sum(g_dn.reshape(4, F // 4, D), f"ffn_down{l}")
        dh, dcw, dcb = ffn_dgate(dr2b, wdn[l], hf, cw_full, cb3, l, name=f"ffn_dgate{l}", carry=[rs_dn[0]])
        g_up = mm_tn(y_in, dh, a_spec=pl.BlockSpec((tt, tkd), lambda kb, nb, t: (t, kb)),
                     g_spec=pl.BlockSpec((None, tt, tnu), lambda kb, nb, t: (nb // (2 * upb), t, nb % (2 * upb))),
                     o_spec=pl.BlockSpec((1, tkd, tnu), lambda kb, nb, t: (nb // upb, kb, nb % upb)),
                     out_shape=(4, D, Nu), grid=(D // tkd, 4 * upb, n_t), acc_shape=(tkd, tnu), P=1, tnw=tnu,
                     name=f"g_ffn_up{l}")
        rs_up = pair_sum(g_up, f"ffn_up{l}")
        tku = tnu
        kps = Nu // tku
        out = mm_nt_res(dh, wup[l], dr2, (xh_in, rs_in, row(ln1_g, l)),
                        a_spec=pl.BlockSpec((None, tm_ln, tku), lambda i, k: (k // (2 * kps), i, k % (2 * kps))),
                        w_spec=pl.BlockSpec((None, 1, D, tku), lambda i, k: (0, k // kps, 0, k % kps)),
                        P=1, tnw=tku, n_k=4 * kps, tm=tm_ln, name=f"d_ffn_in{l}", carry=[rs_up[0]])
        return rs_dn, rs_up, dcw, dcb, out

    def g_out(a, gb, name):
        K = a.shape[1]
        tkk = _tile(K, 1024)
        return mm_tn(a, gb, a_spec=pl.BlockSpec((tt, tkk), lambda kb, nb, t: (t, kb)),
                     g_spec=pl.BlockSpec((tt, tkd), lambda kb, nb, t: (t, nb)),
                     o_spec=pl.BlockSpec((1, tkk, tkd), lambda kb, nb, t: (0, kb, nb)), out_shape=(1, K, D),
                     grid=(K // tkk, D // tkd, n_t), acc_shape=(tkk, tkd), P=1, tnw=tkd, name=name)

    rs_dn1, rs_up1, dcw1, dcb1, (dr1, dr1b, dg_ln1_1, db_ln1_1) = ffn_bwd(1, dr, drb, f1, hf1, y3, xh3, rs3)
    rs_wout1 = pair_sum(g_out(yh, dr1b, "g_od_out").reshape(4, DC // 4, D), "od_out")
    dyh = mm_nt_plain(dr1b, wout1, tm=tm_big, tn=_tile(DC, 512), name="d_od_out")
    dh1p, d_gn, d_lbp = hgrn_bwd(h1p, o_saved, dyh, sp, lb_param, gn_full, name="hgrn_bwd", carry=[rs_wout1[0]])
    g_win1 = mm_tn(y2, dh1p, a_spec=pl.BlockSpec((tt, tkd), lambda kb, nb, t: (t, kb)),
                   g_spec=pl.BlockSpec((tt, 4 * HP), lambda kb, nb, t: (t, nb)),
                   o_spec=pl.BlockSpec((4, tkd, HP), lambda kb, nb, t: (0, kb, nb)), out_shape=(4, D, DC),
                   grid=(D // tkd, n_p, n_t), acc_shape=(tkd, 4 * HP), P=4, tnw=HP, name="g_od_in")
    rs_win1 = pair_sum(g_win1, "od_in")
    dr, drb, dg_ln2_0, db_ln2_0 = mm_nt_res(
        dh1p, win1, dr1, (xh2, rs2, row(ln2_g, 0)), a_spec=pl.BlockSpec((tm_ln, 4 * HP), lambda i, k: (i, k)),
        w_spec=pl.BlockSpec((4, D, HP), lambda i, k: (0, 0, k)), P=4, tnw=HP, n_k=n_p, tm=tm_ln, name="d_od_in",
        carry=[rs_win1[0]])
    rs_dn0, rs_up0, dcw0, dcb0, (dr1, dr1b, dg_ln1_0, db_ln1_0) = ffn_bwd(0, dr, drb, f0, hf0, y1, xh1, rs1)
    rs_wout0 = pair_sum(g_out(cat, dr1b, "g_ev_out").reshape(4, (DA + DB) // 4, D), "ev_out")
    dcat = mm_nt_plain(dr1b, wout0, tm=tm_big, tn=_tile(DA + DB, 512), name="d_ev_out")
    dh0, d_ws, d_bsT, d_lg, d_lb, d_sc, d_wp = gating_bwd(h0, dcat, ev_ln_v_g, ev_ln_v_b, ws, wsT, bsT, wpb,
                                                          ev_pool_scale, name="gating_bwd", carry=[rs_wout0[0]])
    g_win0 = mm_tn(xb16, dh0, a_spec=pl.BlockSpec((tt, tkd), lambda kb, nb, t: (t, kb)),
                   g_spec=pl.BlockSpec((tt, Ns0), lambda kb, nb, t: (t, nb)),
                   o_spec=pl.BlockSpec((1, tkd, Ns0), lambda kb, nb, t: (nb, kb, 0)), out_shape=(4, D, Ns0),
                   grid=(D // tkd, 4, n_t), acc_shape=(tkd, Ns0), P=1, tnw=Ns0, name="g_ev_in")
    rs_win0 = pair_sum(g_win0, "ev_in")
    grad_x = mm_nt_res(dh0, win0, dr1, None, a_spec=pl.BlockSpec((tm_ln, Ns0), lambda i, k: (i, k)),
                       w_spec=pl.BlockSpec((1, D, Ns0), lambda i, k: (k, 0, 0)), P=1, tnw=Ns0, n_k=4, tm=tm_ln,
                       name="d_ev_in", carry=[rs_win0[0]])

    per_weight = [[rs_win0], [rs_wout0], [rs_win1], [rs_wout1], [rs_up0, rs_up1], [rs_dn0, rs_dn1]]
    shared = [rs_finish([own for _, own in m], [cp.out for cp, _ in m], name="rs_finish_" + n)
              for n, m in zip(_BIG, per_weight)]
    big_g = {n: s.reshape(w[n].shape) for n, s in zip(_BIG, shared)}

    small_full = {
        'ev_ln_v_g': d_lg, 'ev_ln_v_b': d_lb, 'ev_w_s': d_ws[None], 'ev_b_s': d_bsT.T[None], 'ev_w_pool': d_wp[None],
        'ev_pool_scale': d_sc, 'od_norm_g': d_gn, 'lb_param': d_lbp,
        'ffn_conv_w': jnp.stack([jnp.concatenate([dcw0[0], dcw0[1]], axis=-1),
                                 jnp.concatenate([dcw1[0], dcw1[1]], axis=-1)]),
        'ffn_conv_b': jnp.stack([jnp.concatenate([dcb0[0, 0], dcb0[1, 0]]), jnp.concatenate([dcb1[0, 0], dcb1[1, 0]])]),
        'ln1_g': jnp.concatenate([dg_ln1_0, dg_ln1_1]), 'ln1_b': jnp.concatenate([db_ln1_0, db_ln1_1]),
        'ln2_g': jnp.concatenate([dg_ln2_0, dg_ln2_1]), 'ln2_b': jnp.concatenate([db_ln2_0, db_ln2_1])}
    packed = _pack([small_full[n] for n in _SMALL] + [loss_p[0, 0:1]])
    reduced = _unpack(all_reduce_small(packed, name="all_reduce_small"),
                      [small_full[n].shape for n in _SMALL] + [(1,)])
    small_g = dict(zip(_SMALL, reduced[:-1]))
    loss = reduced[-1][0]
    small_g['ev_w_pool'] = lax.dynamic_slice_in_dim(small_g['ev_w_pool'], chip * (CG // 4), CG // 4, axis=2)
    small_g['ffn_conv_w'] = lax.dynamic_slice_in_dim(small_g['ffn_conv_w'], chip * (F // 2), F // 2, axis=2)
    small_g['od_norm_g'] = lax.dynamic_slice_in_dim(small_g['od_norm_g'], chip * (DC // 4), DC // 4, axis=1)

    grads = {**big_g, **small_g}
    delta, new_m, new_v = {}, {}, {}
    for n in _BIG:
        delta[n], new_m[n], new_v[n] = adamw(w[n], big_g[n], mom[n], vel[n], "adamw_" + n)
    ps = [_pack([d[n] for n in _SMALL]) for d in (w, small_g, mom, vel)]
    upd = adamw(*[p[None] for p in ps], "adamw_small")
    shapes = [w[n].shape for n in _SMALL]
    for d, buf in zip((delta, new_m, new_v), upd):
        d.update(zip(_SMALL, _unpack(buf[0], shapes)))

    return (loss, grad_x[None], *[grads[n] for n in _WEIGHTS], *[delta[n] for n in _WEIGHTS],
            *[new_m[n] for n in _WEIGHTS], *[new_v[n] for n in _WEIGHTS])
```

```python
import math

import jax
import jax.numpy as jnp
from jax import lax
from jax.experimental import pallas as pl
from jax.experimental.pallas import tpu as pltpu

F32 = jnp.float32
BF16 = jnp.bfloat16
MESH = pl.DeviceIdType.MESH
ANY = pl.BlockSpec(memory_space=pl.ANY)
VMEM_SPEC = pl.BlockSpec(memory_space=pltpu.VMEM)

DEPTH = 2
ALPHA = (2 * DEPTH) ** 0.25
LN_EPS = 1e-5
A_HEAD = 128
A_CHUNK = 128
B_WINDOWS = (2, 4, 8, 16)
POOL_HALO = 16
C_HEAD = 128
C_CHUNK = 64
CONV_HALO = 8
ADAM_LR = 0.001
ADAM_B1 = 0.9
ADAM_B2 = 0.999
ADAM_EPS = 1e-08
ADAM_WD = 0.01
ADAM_STEP = 10
V7X_VMEM_LIMIT_BYTES = 56 * 1024 * 1024
LANES = 128
PACK_ALIGN = 8 * LANES


class IciCopy:
    def __init__(self, kind, arr, rows=None):
        self.kind, self.arr, self.out = kind, arr, None
        self.rows = rows


def _carried_copies(items, in_refs, out_refs, send, recv):
    x, y, c, chips = _mesh_pos()
    me = 2 * x + y
    sends, lands = [], []
    for q, (it, src, dst) in enumerate(zip(items, in_refs, out_refs)):
        rows = pl.ds(*(it.rows or (0, it.arr.shape[-2])))
        for k, (cx, cy) in enumerate(chips):
            if it.kind == "gather":
                mine, theirs = dst.at[0, me, c, rows], dst.at[0, 2 * cx + cy, c, rows]
                sends.append(_remote(mine, mine, send, recv, 3 * q + k, (cx, cy, c)))
            else:
                theirs = dst.at[k]
                sends.append(_remote(src.at[k], theirs, send, recv, 3 * q + k, (cx, cy, c)))
            lands.append(_remote(theirs, theirs, send, recv, 3 * q + k, (x, y, c)))
    return sends, lands


def _call(body, *, name, out_shape, grid=(), in_specs=None, out_specs=None, scratch=(), prefetch=0, aliases=None,
          carry=()):
    single = not isinstance(out_specs, (list, tuple))
    in_specs = list(in_specs)
    out_specs = [out_specs] if single else list(out_specs)
    out_shape = [out_shape] if single else list(out_shape)
    scratch = list(scratch)
    aliases = dict(aliases or {})
    n_in, n_out, n_sc, n_c = len(in_specs), len(out_specs), len(scratch), len(carry)
    inner = body
    if n_c:
        assert grid, "a carrier needs a grid"
        for q, it in enumerate(carry):
            if it.kind == "gather":
                aliases[prefetch + n_in + q] = n_out + q
        in_specs += [ANY] * n_c
        out_specs += [ANY] * n_c
        out_shape += [jax.ShapeDtypeStruct(it.arr.shape, it.arr.dtype) for it in carry]
        scratch += [pltpu.SemaphoreType.DMA((3 * n_c,)), pltpu.SemaphoreType.DMA((3 * n_c,))]

        def inner(*refs):
            pre, refs = refs[:prefetch], refs[prefetch:]
            ins, c_in = refs[:n_in], refs[n_in:n_in + n_c]
            outs, c_out = refs[n_in + n_c:n_in + n_c + n_out], refs[n_in + n_c + n_out:n_in + 2 * n_c + n_out]
            rest = refs[n_in + 2 * n_c + n_out:]
            first = last = True
            for d, n in enumerate(grid):
                first = jnp.logical_and(first, pl.program_id(d) == 0)
                last = jnp.logical_and(last, pl.program_id(d) == n - 1)

            @pl.when(first)
            def _():
                for cp in _carried_copies(carry, c_in, c_out, rest[-2], rest[-1])[0]:
                    cp.start()

            body(*pre, *ins, *outs, *rest[:n_sc])

            @pl.when(last)
            def _():
                sends, lands = _carried_copies(carry, c_in, c_out, rest[-2], rest[-1])
                for cp in lands:
                    cp.wait_recv()
                for cp in sends:
                    cp.wait_send()

    spec = pltpu.PrefetchScalarGridSpec(num_scalar_prefetch=prefetch, grid=grid, in_specs=in_specs,
                                        out_specs=out_specs, scratch_shapes=scratch)
    fn = pl.pallas_call(inner, name=name, grid_spec=spec, out_shape=out_shape, input_output_aliases=aliases,
                        compiler_params=pltpu.CompilerParams(vmem_limit_bytes=V7X_VMEM_LIMIT_BYTES))

    def run(*args):
        res = fn(*args, *[it.arr for it in carry])
        for it, o in zip(carry, res[n_out:]):
            it.out = o
        return res[0] if single else list(res[:n_out])

    return run


def _tile(n, pref, unit=LANES):
    if n <= pref:
        return n
    t = (pref // unit) * unit
    while t > unit and n % t:
        t -= unit
    assert n % t == 0, (n, pref, unit)
    return t


def _slabs(n, rows=128):
    return [slice(r, min(r + rows, n)) for r in range(0, n, rows)]


def _dot(a, b):
    return jnp.dot(a, b, preferred_element_type=F32)


def _dot_nt(a, b):
    return lax.dot_general(a, b, (((1,), (1,)), ((), ())), preferred_element_type=F32)


def _dot_tn(a, b):
    return lax.dot_general(a, b, (((0,), (0,)), ((), ())), preferred_element_type=F32)


def _sigmoid(x):
    return jax.nn.sigmoid(x)


_GELU_C = math.sqrt(2.0 / math.pi)


def _gelu(x):
    return 0.5 * x * (1.0 + jnp.tanh(_GELU_C * (x + 0.044715 * x * x * x)))


def _gelu_grad(x):
    th = jnp.tanh(_GELU_C * (x + 0.044715 * x * x * x))
    return 0.5 * (1.0 + th) + 0.5 * x * (1.0 - th * th) * _GELU_C * (1.0 + 3.0 * 0.044715 * x * x)


def _ln_fwd(r, g, b):
    mu = jnp.mean(r, axis=-1, keepdims=True)
    xc = r - mu
    var = jnp.mean(xc * xc, axis=-1, keepdims=True)
    rstd = lax.rsqrt(var + LN_EPS)
    xh = xc * rstd
    return xh * g + b, xh, rstd


def _ln_bwd(dy, xh, rstd, g):
    dxh = dy * g
    m1 = jnp.mean(dxh, axis=-1, keepdims=True)
    m2 = jnp.mean(dxh * xh, axis=-1, keepdims=True)
    dr = rstd * (dxh - m1 - xh * m2)
    return dr, jnp.sum(dy * xh, axis=0, keepdims=True), jnp.sum(dy, axis=0, keepdims=True)


def _exact_tri_dot(tri, x):
    hi = x.astype(BF16)
    r1 = x - hi.astype(F32)
    mid = r1.astype(BF16)
    lo = (r1 - mid.astype(F32)).astype(BF16)
    return _dot(tri, hi) + _dot(tri, mid) + _dot(tri, lo)


def cast_bf16(a3, name):
    L, R, C = a3.shape
    br = _tile(R, max(8, (1 << 20) // C), 8)

    def body(a_ref, o_ref):
        o_ref[...] = a_ref[...].astype(BF16)

    return _call(body, name=name, grid=(L, R // br),
                 in_specs=[pl.BlockSpec((None, br, C), lambda l, r: (l, r, 0))],
                 out_specs=pl.BlockSpec((None, br, C), lambda l, r: (l, r, 0)),
                 out_shape=jax.ShapeDtypeStruct((L, R, C), BF16))(a3)


def adamw(w, g, m, v, name):
    L, R, C = w.shape
    br = _tile(R, max(8, (1 << 19) // C), 8)
    c1 = 1.0 - ADAM_B1 ** ADAM_STEP
    c2 = 1.0 - ADAM_B2 ** ADAM_STEP

    def body(w_ref, g_ref, m_ref, v_ref, go_ref, d_ref, nm_ref, nv_ref):
        gg = g_ref[...]
        nm = ADAM_B1 * m_ref[...] + (1.0 - ADAM_B1) * gg
        nv = ADAM_B2 * v_ref[...] + (1.0 - ADAM_B2) * (gg * gg)
        go_ref[...] = gg
        d_ref[...] = -ADAM_LR * ((nm / c1) / (jnp.sqrt(nv / c2) + ADAM_EPS) + ADAM_WD * w_ref[...])
        nm_ref[...] = nm
        nv_ref[...] = nv

    spec = pl.BlockSpec((None, br, C), lambda l, r: (l, r, 0))
    sds = jax.ShapeDtypeStruct((L, R, C), F32)
    return _call(body, name=name, grid=(L, R // br), in_specs=[spec] * 4, out_specs=[spec] * 4,
                 out_shape=[sds] * 4)(w, g, m, v)


def mm_nn(a, w, *, w_spec, P, tnw, tm, n_j, name, carry=()):
    T, K = a.shape
    bw = P * tnw

    def body(a_ref, w_ref, o_ref):
        av = a_ref[...]
        for p in range(P):
            o_ref[:, p * tnw:(p + 1) * tnw] = _dot(av, w_ref[p]).astype(BF16)

    return _call(body, name=name, grid=(T // tm, n_j),
                 in_specs=[pl.BlockSpec((tm, K), lambda i, j: (i, 0)), w_spec],
                 out_specs=pl.BlockSpec((tm, bw), lambda i, j: (i, j)),
                 out_shape=jax.ShapeDtypeStruct((T, n_j * bw), BF16), carry=carry)(a, w)


def mm_ln(a, w, res, rg, rb, g, b, *, w_spec, K, tk, tm, name, carry=()):
    T, N = res.shape
    n_k = K // tk

    def body(a_ref, w_ref, res_ref, rg_ref, rb_ref, g_ref, b_ref, xh_ref, y_ref, rs_ref, acc):
        k = pl.program_id(1)

        @pl.when(k == 0)
        def _():
            acc[...] = jnp.zeros_like(acc)

        acc[...] += _dot(a_ref[...], w_ref[...])

        @pl.when(k == n_k - 1)
        def _():
            for rows in _slabs(tm):
                r = ALPHA * (res_ref[rows, :] * rg_ref[...] + rb_ref[...]) + acc[rows, :]
                y, xh, rstd = _ln_fwd(r, g_ref[...], b_ref[...])
                xh_ref[rows, :] = xh
                y_ref[rows, :] = y.astype(BF16)
                rs_ref[rows, :] = rstd

    row = pl.BlockSpec((tm, N), lambda i, k: (i, 0))
    vec = pl.BlockSpec((1, N), lambda i, k: (0, 0))
    return _call(body, name=name, grid=(T // tm, n_k),
                 in_specs=[pl.BlockSpec((tm, tk), lambda i, k: (i, k)), w_spec, row, vec, vec, vec, vec],
                 out_specs=[row, row, pl.BlockSpec((tm, 1), lambda i, k: (i, 0))],
                 out_shape=[jax.ShapeDtypeStruct((T, N), F32), jax.ShapeDtypeStruct((T, N), BF16),
                            jax.ShapeDtypeStruct((T, 1), F32)],
                 scratch=[pltpu.VMEM((tm, N), F32)], carry=carry)(a, w, res, rg, rb, g, b)


def mm_nt_plain(a, w, *, tm, tn, name):
    T, K = a.shape
    N = w.shape[0]

    def body(a_ref, w_ref, o_ref):
        o_ref[...] = _dot_nt(a_ref[...], w_ref[...]).astype(BF16)

    return _call(body, name=name, grid=(T // tm, N // tn),
                 in_specs=[pl.BlockSpec((tm, K), lambda i, j: (i, 0)), pl.BlockSpec((tn, K), lambda i, j: (j, 0))],
                 out_specs=pl.BlockSpec((tm, tn), lambda i, j: (i, j)),
                 out_shape=jax.ShapeDtypeStruct((T, N), BF16))(a, w)


def mm_nt_res(a, w, res, ln, *, a_spec, w_spec, P, tnw, n_k, tm, name, carry=()):
    T, N = res.shape
    n_i = T // tm

    def body(*refs):
        if ln is None:
            a_ref, w_ref, res_ref, o_ref, acc = refs
        else:
            a_ref, w_ref, res_ref, xh_ref, rs_ref, g_ref, dr_ref, drb_ref, dg_ref, db_ref, acc = refs
        i = pl.program_id(0)
        k = pl.program_id(1)

        @pl.when(k == 0)
        def _():
            acc[...] = jnp.zeros_like(acc)

        wv = w_ref[0] if P == 1 else jnp.concatenate([w_ref[p] for p in range(P)], axis=1)
        acc[...] += _dot_nt(a_ref[...], wv)

        @pl.when(k == n_k - 1)
        def _():
            if ln is not None:
                @pl.when(i == 0)
                def _():
                    dg_ref[...] = jnp.zeros_like(dg_ref)
                    db_ref[...] = jnp.zeros_like(db_ref)

            for rows in _slabs(tm):
                d = ALPHA * res_ref[rows, :] + acc[rows, :]
                if ln is None:
                    o_ref[rows, :] = d
                else:
                    dr, dg, db = _ln_bwd(d, xh_ref[rows, :], rs_ref[rows, :], g_ref[...])
                    dr_ref[rows, :] = dr
                    drb_ref[rows, :] = dr.astype(BF16)
                    dg_ref[...] += dg
                    db_ref[...] += db

    row = pl.BlockSpec((tm, N), lambda i, k: (i, 0))
    vec = pl.BlockSpec((1, N), lambda i, k: (0, 0))
    scratch = [pltpu.VMEM((tm, N), F32)]
    if ln is None:
        return _call(body, name=name, grid=(n_i, n_k), in_specs=[a_spec, w_spec, row], out_specs=row,
                     out_shape=jax.ShapeDtypeStruct((T, N), F32), scratch=scratch, carry=carry)(a, w, res)
    xh, rstd, g = ln
    return _call(body, name=name, grid=(n_i, n_k),
                 in_specs=[a_spec, w_spec, row, row, pl.BlockSpec((tm, 1), lambda i, k: (i, 0)), vec],
                 out_specs=[row, row, vec, vec],
                 out_shape=[jax.ShapeDtypeStruct((T, N), F32), jax.ShapeDtypeStruct((T, N), BF16),
                            jax.ShapeDtypeStruct((1, N), F32), jax.ShapeDtypeStruct((1, N), F32)],
                 scratch=scratch, carry=carry)(a, w, res, xh, rstd, g)


def mm_tn(a, g, *, a_spec, g_spec, o_spec, out_shape, grid, acc_shape, P, tnw, name):
    n_t = grid[2]

    def body(a_ref, g_ref, o_ref, acc):
        t = pl.program_id(2)

        @pl.when(t == 0)
        def _():
            acc[...] = jnp.zeros_like(acc)

        acc[...] += _dot_tn(a_ref[...], g_ref[...])

        @pl.when(t == n_t - 1)
        def _():
            for p in range(P):
                o_ref[p] = acc[:, p * tnw:(p + 1) * tnw].astype(BF16)

    return _call(body, name=name, grid=grid, in_specs=[a_spec, g_spec], out_specs=o_spec,
                 out_shape=jax.ShapeDtypeStruct(out_shape, BF16),
                 scratch=[pltpu.VMEM(acc_shape, F32)])(a, g)


def _causal_conv(ext, halo, w, b):
    s1 = pltpu.roll(ext, 1, 0)[halo:]
    s2 = pltpu.roll(ext, 2, 0)[halo:]
    return b + w[2:3] * ext[halo:] + w[1:2] * s1 + w[0:1] * s2, s1, s2


def ffn_up(xb, wup, cw, cb, l, *, name, carry=()):
    T, D = xb.shape
    Ns = wup.shape[-1]
    F = 2 * Ns
    tn = _tile(Ns, 256)
    nps = Ns // tn
    n_j = F // tn
    tm = _tile(T, 1024, 8)

    def body(x_ref, wa_ref, wv_ref, cwa_ref, cwv_ref, cba_ref, cbv_ref, h_ref, hc_ref, f_ref, carry):
        i = pl.program_id(1)

        @pl.when(i == 0)
        def _():
            carry[...] = jnp.zeros_like(carry)

        xv = x_ref[...]
        ha = _dot(xv, wa_ref[...])
        hv = _dot(xv, wv_ref[...])
        ca, _, _ = _causal_conv(jnp.concatenate([carry[0], ha], axis=0), CONV_HALO, cwa_ref[...], cba_ref[...])
        cv, _, _ = _causal_conv(jnp.concatenate([carry[1], hv], axis=0), CONV_HALO, cwv_ref[...], cbv_ref[...])
        carry[0] = ha[tm - CONV_HALO:]
        carry[1] = hv[tm - CONV_HALO:]
        h_ref[0] = ha.astype(BF16)
        h_ref[1] = hv.astype(BF16)
        hc_ref[0] = ca.astype(BF16)
        hc_ref[1] = cv.astype(BF16)
        f_ref[...] = (ca * _sigmoid(ca) * cv).astype(BF16)

    wspec_a = pl.BlockSpec((None, None, D, tn), lambda j, i: (0, j // nps, 0, j % nps))
    wspec_v = pl.BlockSpec((None, None, D, tn), lambda j, i: (0, 2 + j // nps, 0, j % nps))
    return _call(
        body, name=name, grid=(n_j, T // tm),
        in_specs=[pl.BlockSpec((tm, D), lambda j, i: (i, 0)), wspec_a, wspec_v,
                  pl.BlockSpec((None, 3, tn), lambda j, i: (l, 0, j)),
                  pl.BlockSpec((None, 3, tn), lambda j, i: (l, 0, n_j + j)),
                  pl.BlockSpec((None, 1, tn), lambda j, i: (l, 0, j)),
                  pl.BlockSpec((None, 1, tn), lambda j, i: (l, 0, n_j + j))],
        out_specs=[pl.BlockSpec((2, tm, tn), lambda j, i: (0, i, j)), pl.BlockSpec((2, tm, tn), lambda j, i: (0, i, j)),
                   pl.BlockSpec((tm, tn), lambda j, i: (i, j))],
        out_shape=[jax.ShapeDtypeStruct((2, T, F), BF16), jax.ShapeDtypeStruct((2, T, F), BF16),
                   jax.ShapeDtypeStruct((T, F), BF16)],
        scratch=[pltpu.VMEM((2, CONV_HALO, tn), F32)], carry=carry)(xb, wup, wup, cw, cw, cb, cb)


def ffn_dgate(db16, wdn, h, hc, cw, l, *, name, carry=()):
    T, D = db16.shape
    F = h.shape[-1]
    tn = _tile(F, 512)
    n_j = F // tn
    tm = _tile(T, 512, 16)
    n_i = T // tm
    n_ext = tm + CONV_HALO

    def body(d_ref, w_ref, h_ref, hc_ref, cwa_ref, cwv_ref, dh_ref, dcw_ref, dcb_ref, carry):
        ip = pl.program_id(1)

        @pl.when(ip == 0)
        def _():
            carry[...] = jnp.zeros_like(carry)
            dcw_ref[...] = jnp.zeros_like(dcw_ref)
            dcb_ref[...] = jnp.zeros_like(dcb_ref)

        df = _dot_nt(d_ref[...], w_ref[...])
        ca = hc_ref[0].astype(F32)
        cv = hc_ref[1].astype(F32)
        sig = _sigmoid(ca)
        sil = ca * sig
        da = df * cv * (sig + sil * (1.0 - sig))
        dv = df * sil
        for half, (dc, w_ref_h) in enumerate(((da, cwa_ref), (dv, cwv_ref))):
            w = w_ref_h[...]
            h0 = h_ref[half].astype(F32)
            ext = jnp.concatenate([dc, carry[half]], axis=0)
            n1 = pltpu.roll(ext, n_ext - 1, 0)[:tm]
            n2 = pltpu.roll(ext, n_ext - 2, 0)[:tm]
            dcb_ref[half] += jnp.sum(dc, axis=0, keepdims=True)
            dcw_ref[half] += jnp.concatenate(
                [jnp.sum(n2 * h0, axis=0, keepdims=True), jnp.sum(n1 * h0, axis=0, keepdims=True),
                 jnp.sum(dc * h0, axis=0, keepdims=True)], axis=0)
            dh_ref[half] = (w[2:3] * dc + w[1:2] * n1 + w[0:1] * n2).astype(BF16)
            carry[half] = dc[:CONV_HALO]

    rev = lambda ip: n_i - 1 - ip
    tile = pl.BlockSpec((2, tm, tn), lambda j, ip: (0, rev(ip), j))
    return _call(
        body, name=name, grid=(n_j, n_i),
        in_specs=[pl.BlockSpec((tm, D), lambda j, ip: (rev(ip), 0)),
                  pl.BlockSpec((None, tn, D), lambda j, ip: (0, j, 0)), tile, tile,
                  pl.BlockSpec((None, 3, tn), lambda j, ip: (l, 0, j)),
                  pl.BlockSpec((None, 3, tn), lambda j, ip: (l, 0, n_j + j))],
        out_specs=[tile, pl.BlockSpec((2, 3, tn), lambda j, ip: (0, 0, j)),
                   pl.BlockSpec((2, 1, tn), lambda j, ip: (0, 0, j))],
        out_shape=[jax.ShapeDtypeStruct((2, T, F), BF16), jax.ShapeDtypeStruct((2, 3, F), F32),
                   jax.ShapeDtypeStruct((2, 1, F), F32)],
        scratch=[pltpu.VMEM((2, CONV_HALO, tn), F32)], carry=carry)(db16, wdn, h, hc, cw, cw)


def _pool_fwd(ext, xb_g, t_glob, win):
    e = ext
    sft = 1
    while sft < win:
        e = e + pltpu.roll(e, sft, 0)
        sft *= 2
    cnt = jnp.minimum(t_glob + 1.0, float(win))
    return e[POOL_HALO:] / cnt - xb_g


def gating_fwd(h0, lg, lb, ws, bsT, wp, sc, *, name, carry=()):
    T = h0.shape[0]
    DA = lg.shape[-1]
    DB = sc.shape[-1]
    HA = DA // A_HEAD
    G = len(B_WINDOWS)
    CG = DB // G
    tm = _tile(T, 512, A_CHUNK)
    n_c = tm // A_CHUNK

    def body(h_ref, halo_ref, lg_ref, lb_ref, ws_ref, bsT_ref, wp_ref, sc_ref, cat_ref):
        i = pl.program_id(0)
        hu = h_ref[:, 0:DA].astype(F32)
        hv = h_ref[:, DA:2 * DA].astype(F32)
        xb = h_ref[:, 2 * DA:].astype(F32)
        u = _gelu(hu)
        vn, _, _ = _ln_fwd(_gelu(hv), lg_ref[...], lb_ref[...])
        vnb = vn.astype(BF16)
        rr = lax.broadcasted_iota(jnp.int32, (A_CHUNK, A_CHUNK), 0)
        cc = lax.broadcasted_iota(jnp.int32, (A_CHUNK, A_CHUNK), 1)
        for hh in range(HA):
            wt = jnp.where(rr >= cc, ws_ref[hh], 0.0).astype(BF16)
            cs = slice(hh * A_HEAD, (hh + 1) * A_HEAD)
            for n in range(n_c):
                rs = slice(n * A_CHUNK, (n + 1) * A_CHUNK)
                s = _dot(wt, vnb[rs, cs]) + bsT_ref[:, hh:hh + 1]
                cat_ref[rs, cs] = (u[rs, cs] * s).astype(BF16)
        halo = jnp.where(i > 0, halo_ref[...].astype(F32), 0.0)
        ext = jnp.concatenate([halo, xb], axis=0)
        t_glob = (i * tm + lax.broadcasted_iota(jnp.int32, (tm, 1), 0)).astype(F32)
        for g, win in enumerate(B_WINDOWS):
            gs = slice(g * CG, (g + 1) * CG)
            p = _pool_fwd(ext[:, gs], xb[:, gs], t_glob, win)
            z = _dot(p.astype(BF16), wp_ref[g])
            cat_ref[:, DA + g * CG:DA + (g + 1) * CG] = (z * sc_ref[:, gs]).astype(BF16)

    full = lambda a: pl.BlockSpec(a.shape, lambda i: (0,) * a.ndim)
    hpb = tm // POOL_HALO
    return _call(
        body, name=name, grid=(T // tm,),
        in_specs=[pl.BlockSpec((tm, 2 * DA + DB), lambda i: (i, 0)),
                  pl.BlockSpec((POOL_HALO, DB), lambda i: (jnp.maximum(i * hpb - 1, 0), 2 * DA // DB)),
                  full(lg), full(lb), full(ws), full(bsT), full(wp), full(sc)],
        out_specs=pl.BlockSpec((tm, DA + DB), lambda i: (i, 0)),
        out_shape=jax.ShapeDtypeStruct((T, DA + DB), BF16), carry=carry)(h0, h0, lg, lb, ws, bsT, wp, sc)


def gating_bwd(h0, dcat, lg, lb, ws, wsT, bsT, wp, sc, *, name, carry=()):
    T = h0.shape[0]
    DA = lg.shape[-1]
    DB = sc.shape[-1]
    HA = DA // A_HEAD
    G = len(B_WINDOWS)
    CG = DB // G
    tm = _tile(T, 512, A_CHUNK)
    n_i = T // tm
    n_c = tm // A_CHUNK
    n_ext = tm + POOL_HALO

    def body(h_ref, halo_ref, dc_ref, dhalo_ref, lg_ref, lb_ref, ws_ref, wsT_ref, bsT_ref, wp_ref, sc_ref,
             dh_ref, dws_ref, dbsT_ref, dlg_ref, dlb_ref, dsc_ref, dwp_ref, dvn_sc):
        i = pl.program_id(0)

        @pl.when(i == 0)
        def _():
            for r in (dws_ref, dbsT_ref, dlg_ref, dlb_ref, dsc_ref, dwp_ref):
                r[...] = jnp.zeros_like(r)

        hu = h_ref[:, 0:DA].astype(F32)
        hv = h_ref[:, DA:2 * DA].astype(F32)
        xb = h_ref[:, 2 * DA:].astype(F32)
        u = _gelu(hu)
        gu = _gelu_grad(hu)
        lgv = lg_ref[...]
        vn, vhat, rstd = _ln_fwd(_gelu(hv), lgv, lb_ref[...])
        vnb = vn.astype(BF16)
        rr = lax.broadcasted_iota(jnp.int32, (A_CHUNK, A_CHUNK), 0)
        cc = lax.broadcasted_iota(jnp.int32, (A_CHUNK, A_CHUNK), 1)
        for hh in range(HA):
            wt = jnp.where(rr >= cc, ws_ref[hh], 0.0).astype(BF16)
            wtT = jnp.where(rr <= cc, wsT_ref[hh], 0.0).astype(BF16)
            cs = slice(hh * A_HEAD, (hh + 1) * A_HEAD)
            dws = jnp.zeros((A_CHUNK, A_CHUNK), F32)
            dbs = jnp.zeros((A_CHUNK, 1), F32)
            for n in range(n_c):
                rs = slice(n * A_CHUNK, (n + 1) * A_CHUNK)
                vb = vnb[rs, cs]
                s = _dot(wt, vb) + bsT_ref[:, hh:hh + 1]
                dya = dc_ref[rs, cs].astype(F32)
                ds = dya * u[rs, cs]
                dh_ref[rs, cs] = (dya * s * gu[rs, cs]).astype(BF16)
                dsb = ds.astype(BF16)
                dbs = dbs + jnp.sum(ds, axis=1, keepdims=True)
                dws = dws + _dot_nt(dsb, vb)
                dvn_sc[rs, cs] = _dot(wtT, dsb)
            dws_ref[hh] += jnp.where(rr >= cc, dws, 0.0)
            dbsT_ref[:, hh:hh + 1] += dbs
        dvg, dlg, dlb = _ln_bwd(dvn_sc[...], vhat, rstd, lgv)
        dlg_ref[...] += dlg
        dlb_ref[...] += dlb
        dh_ref[:, DA:2 * DA] = (dvg * _gelu_grad(hv)).astype(BF16)

        halo = jnp.where(i > 0, halo_ref[...].astype(F32), 0.0)
        ext = jnp.concatenate([halo, xb], axis=0)
        t_glob = (i * tm + lax.broadcasted_iota(jnp.int32, (tm, 1), 0)).astype(F32)
        t_ext = (i * tm + lax.broadcasted_iota(jnp.int32, (n_ext, 1), 0)).astype(F32)
        dyb = dc_ref[:, DA:].astype(F32)
        dhalo = jnp.where(i < n_i - 1, dhalo_ref[...].astype(F32), 0.0)
        dyb_ext = jnp.concatenate([dyb, dhalo], axis=0)
        for g, win in enumerate(B_WINDOWS):
            gs = slice(g * CG, (g + 1) * CG)
            pb = _pool_fwd(ext[:, gs], xb[:, gs], t_glob, win).astype(BF16)
            wpg = wp_ref[g]
            z = _dot(pb, wpg)
            dsc_ref[:, gs] += jnp.sum(dyb[:, gs] * z, axis=0, keepdims=True)
            dzb = (dyb_ext[:, gs] * sc_ref[:, gs]).astype(BF16)
            dwp_ref[g] += _dot_tn(pb, dzb[:tm])
            dp = _dot_nt(dzb, wpg)
            e = dp / jnp.minimum(t_ext + 1.0, float(win))
            sft = 1
            while sft < win:
                e = e + pltpu.roll(e, n_ext - sft, 0)
                sft *= 2
            dh_ref[:, 2 * DA + g * CG:2 * DA + (g + 1) * CG] = (e[:tm] - dp[:tm]).astype(BF16)

    full = lambda a: pl.BlockSpec(a.shape, lambda i: (0,) * a.ndim)
    hpb = tm // POOL_HALO
    n_hb = T // POOL_HALO
    outs = [jax.ShapeDtypeStruct((T, 2 * DA + DB), BF16), jax.ShapeDtypeStruct(ws.shape, F32),
            jax.ShapeDtypeStruct(bsT.shape, F32), jax.ShapeDtypeStruct(lg.shape, F32),
            jax.ShapeDtypeStruct(lb.shape, F32), jax.ShapeDtypeStruct(sc.shape, F32),
            jax.ShapeDtypeStruct(wp.shape, F32)]
    return _call(
        body, name=name, grid=(n_i,),
        in_specs=[pl.BlockSpec((tm, 2 * DA + DB), lambda i: (i, 0)),
                  pl.BlockSpec((POOL_HALO, DB), lambda i: (jnp.maximum(i * hpb - 1, 0), 2 * DA // DB)),
                  pl.BlockSpec((tm, DA + DB), lambda i: (i, 0)),
                  pl.BlockSpec((POOL_HALO, DB), lambda i: (jnp.minimum((i + 1) * hpb, n_hb - 1), DA // DB)),
                  full(lg), full(lb), full(ws), full(wsT), full(bsT), full(wp), full(sc)],
        out_specs=[pl.BlockSpec((tm, 2 * DA + DB), lambda i: (i, 0))] + [full(o) for o in outs[1:]],
        out_shape=outs,
        scratch=[pltpu.VMEM((tm, DA), F32)], carry=carry)(h0, h0, dcat, dcat, lg, lb, ws, wsT, bsT, wp, sc)


HP = 2 * C_HEAD


def _hgrn_gates(blk, lbv, tri):
    q = blk[:, 0:HP]
    sg = _sigmoid(blk[:, HP:2 * HP])
    f = lbv + (1.0 - lbv) * sg
    bcum = _exact_tri_dot(tri, jnp.log(f))
    blast = bcum[C_CHUNK - 1:C_CHUNK]
    sq = _sigmoid(q)
    k = 1.0 - f
    e_in = jnp.exp(bcum)
    e_out = jnp.exp(-bcum)
    e_end = jnp.exp(blast - bcum)
    return dict(q=q, sq=sq, sg=sg, f=f, k=k, bcum=bcum, blast=blast, e_in=e_in, e_out=e_out, e_end=e_end,
                qd=q * sq * e_in, kd=k * e_out, ke=k * e_end, dec=jnp.exp(blast), v=blk[:, 2 * HP:3 * HP],
                gg=blk[:, 3 * HP:4 * HP])


def _lower_bound(lbp_ref):
    p0, p1 = lbp_ref[0:1], lbp_ref[1:2]
    mx = jnp.maximum(p0, p1)
    e0, e1 = jnp.exp(p0 - mx), jnp.exp(p1 - mx)
    return e1 / (e0 + e1)


def hgrn_fwd(h1p, lbp, gn, *, name, carry=()):
    T = h1p.shape[0]
    DC = gn.shape[-1]
    n_p = DC // HP
    tt = _tile(T, 512, C_CHUNK)
    n_c = tt // C_CHUNK

    def body(h_ref, lbp_ref, gn_ref, y_ref, o_ref, sp_ref, st):
        i = pl.program_id(1)

        @pl.when(i == 0)
        def _():
            st[...] = jnp.zeros_like(st)

        lbv = _lower_bound(lbp_ref)
        gnv = gn_ref[...]
        rr = lax.broadcasted_iota(jnp.int32, (C_CHUNK, C_CHUNK), 0)
        cc = lax.broadcasted_iota(jnp.int32, (C_CHUNK, C_CHUNK), 1)
        causal = rr >= cc
        tri = jnp.where(causal, 1.0, 0.0).astype(BF16)

        def chunk(n, _):
            rows = pl.ds(pl.multiple_of(n * C_CHUNK, C_CHUNK), C_CHUNK)
            a = _hgrn_gates(h_ref[rows, :].astype(F32), lbv, tri)
            outs = []
            for hd in range(2):
                cs = slice(hd * C_HEAD, (hd + 1) * C_HEAD)
                qd, kd, ke = a["qd"][:, cs].astype(BF16), a["kd"][:, cs].astype(BF16), a["ke"][:, cs].astype(BF16)
                vb = a["v"][:, cs].astype(BF16)
                s_t = st[hd]
                att = jnp.where(causal, _dot_nt(qd, kd), 0.0).astype(BF16)
                outs.append(_dot(att, vb) + _dot_nt(qd, s_t.astype(BF16)))
                sp_ref[hd, n] = s_t
                st[hd] = a["dec"][:, cs] * s_t + _dot_tn(vb, ke)
            o = jnp.concatenate(outs, axis=1)
            o_ref[rows, :] = o.astype(BF16)
            ys = []
            for hd in range(2):
                oh = o[:, hd * C_HEAD:(hd + 1) * C_HEAD]
                ys.append(oh * lax.rsqrt(jnp.mean(oh * oh, axis=-1, keepdims=True) + LN_EPS))
            y_ref[rows, :] = (jnp.concatenate(ys, axis=1) * gnv * _sigmoid(a["gg"])).astype(BF16)
            return 0

        lax.fori_loop(0, n_c, chunk, 0, unroll=4)

    return _call(
        body, name=name, grid=(n_p, T // tt),
        in_specs=[pl.BlockSpec((tt, 4 * HP), lambda p, i: (i, p)), pl.BlockSpec((2, HP), lambda p, i: (0, p)),
                  pl.BlockSpec((1, HP), lambda p, i: (0, p))],
        out_specs=[pl.BlockSpec((tt, HP), lambda p, i: (i, p)), pl.BlockSpec((tt, HP), lambda p, i: (i, p)),
                   pl.BlockSpec((2, n_c, C_HEAD, C_HEAD), lambda p, i: (p, i, 0, 0))],
        out_shape=[jax.ShapeDtypeStruct((T, DC), BF16), jax.ShapeDtypeStruct((T, DC), BF16),
                   jax.ShapeDtypeStruct((2 * n_p, T // C_CHUNK, C_HEAD, C_HEAD), F32)],
        scratch=[pltpu.VMEM((2, C_HEAD, C_HEAD), F32)], carry=carry)(h1p, lbp, gn)


def hgrn_bwd(h1p, o_saved, dy, sp, lbp, gn, *, name, carry=()):
    T = h1p.shape[0]
    DC = gn.shape[-1]
    n_p = DC // HP
    tt = _tile(T, 512, C_CHUNK)
    n_i = T // tt
    n_c = tt // C_CHUNK

    def body(h_ref, o_ref, dy_ref, sp_ref, lbp_ref, gn_ref, dh_ref, dgn_ref, dlbp_ref, dst, dlb_acc):
        ip = pl.program_id(1)

        @pl.when(ip == 0)
        def _():
            dst[...] = jnp.zeros_like(dst)
            dlb_acc[...] = jnp.zeros_like(dlb_acc)
            dgn_ref[...] = jnp.zeros_like(dgn_ref)

        lbv = _lower_bound(lbp_ref)
        gnv = gn_ref[...]
        rr = lax.broadcasted_iota(jnp.int32, (C_CHUNK, C_CHUNK), 0)
        cc = lax.broadcasted_iota(jnp.int32, (C_CHUNK, C_CHUNK), 1)
        causal = rr >= cc
        tri = jnp.where(causal, 1.0, 0.0).astype(BF16)
        tri_t = jnp.where(rr <= cc, 1.0, 0.0).astype(BF16)
        last_row = lax.broadcasted_iota(jnp.int32, (C_CHUNK, 1), 0) == C_CHUNK - 1

        def chunk(m, _):
            n = n_c - 1 - m
            rows = pl.ds(pl.multiple_of(n * C_CHUNK, C_CHUNK), C_CHUNK)
            a = _hgrn_gates(h_ref[rows, :].astype(F32), lbv, tri)
            o = o_ref[rows, :].astype(F32)
            dyv = dy_ref[rows, :].astype(F32)
            sgg = _sigmoid(a["gg"])
            ohs, rrs = [], []
            for hd in range(2):
                oh = o[:, hd * C_HEAD:(hd + 1) * C_HEAD]
                r = lax.rsqrt(jnp.mean(oh * oh, axis=-1, keepdims=True) + LN_EPS)
                ohs.append(oh * r)
                rrs.append(r)
            ohat = jnp.concatenate(ohs, axis=1)
            dyn = dyv * sgg
            dgg = dyv * ohat * gnv * sgg * (1.0 - sgg)
            dgn_ref[...] += jnp.sum(dyn * ohat, axis=0, keepdims=True)
            dxh = dyn * gnv
            dqd_l, dkd_l, dke_l, dv_l, ddec_l = [], [], [], [], []
            for hd in range(2):
                cs = slice(hd * C_HEAD, (hd + 1) * C_HEAD)
                dxh_h, oh_h = dxh[:, cs], ohat[:, cs]
                do = rrs[hd] * (dxh_h - oh_h * jnp.mean(dxh_h * oh_h, axis=-1, keepdims=True))
                dob = do.astype(BF16)
                qd, kd, ke = a["qd"][:, cs].astype(BF16), a["kd"][:, cs].astype(BF16), a["ke"][:, cs].astype(BF16)
                vb = a["v"][:, cs].astype(BF16)
                s_t = sp_ref[hd, n]
                ds_out = dst[hd]
                ds_outb = ds_out.astype(BF16)
                att = jnp.where(causal, _dot_nt(qd, kd), 0.0).astype(BF16)
                datt = jnp.where(causal, _dot_nt(dob, vb), 0.0).astype(BF16)
                dv_l.append(_dot_tn(att, dob) + _dot_nt(ke, ds_outb))
                dqd_l.append(_dot(datt, kd) + _dot(dob, s_t.astype(BF16)))
                dkd_l.append(_dot_tn(datt, qd))
                dke_l.append(_dot(vb, ds_outb))
                ddec_l.append(jnp.sum(ds_out * s_t, axis=0, keepdims=True))
                dst[hd] = a["dec"][:, cs] * ds_out + _dot_tn(dob, qd)
            dqd = jnp.concatenate(dqd_l, axis=1)
            dkd = jnp.concatenate(dkd_l, axis=1)
            dke = jnp.concatenate(dke_l, axis=1)
            dv = jnp.concatenate(dv_l, axis=1)
            ddec = jnp.concatenate(ddec_l, axis=1)
            dqs = dqd * a["e_in"]
            kek = dke * a["ke"]
            dbcum = dqd * a["qd"] - dkd * a["kd"] - kek
            dk = dkd * a["e_out"] + dke * a["e_end"]
            dblast = jnp.sum(kek, axis=0, keepdims=True) + ddec * a["dec"]
            dbcum = dbcum + jnp.where(last_row, dblast, 0.0)
            dlf = _exact_tri_dot(tri_t, dbcum)
            df = dlf / a["f"] - dk
            dlb_acc[...] += jnp.sum(df * (1.0 - a["sg"]), axis=0, keepdims=True)
            dfl = df * (1.0 - lbv) * a["sg"] * (1.0 - a["sg"])
            dq = dqs * a["sq"] * (1.0 + a["q"] * (1.0 - a["sq"]))
            dh_ref[rows, :] = jnp.concatenate([dq, dfl, dv, dgg], axis=1).astype(BF16)
            return 0

        lax.fori_loop(0, n_c, chunk, 0, unroll=4)

        @pl.when(ip == n_i - 1)
        def _():
            d1 = dlb_acc[...] * lbv * (1.0 - lbv)
            dlbp_ref[...] = jnp.concatenate([-d1, d1], axis=0)

    rev = lambda ip: n_i - 1 - ip
    return _call(
        body, name=name, grid=(n_p, n_i),
        in_specs=[pl.BlockSpec((tt, 4 * HP), lambda p, ip: (rev(ip), p)),
                  pl.BlockSpec((tt, HP), lambda p, ip: (rev(ip), p)),
                  pl.BlockSpec((tt, HP), lambda p, ip: (rev(ip), p)),
                  pl.BlockSpec((2, n_c, C_HEAD, C_HEAD), lambda p, ip: (p, rev(ip), 0, 0)),
                  pl.BlockSpec((2, HP), lambda p, ip: (0, p)), pl.BlockSpec((1, HP), lambda p, ip: (0, p))],
        out_specs=[pl.BlockSpec((tt, 4 * HP), lambda p, ip: (rev(ip), p)),
                   pl.BlockSpec((1, HP), lambda p, ip: (0, p)), pl.BlockSpec((2, HP), lambda p, ip: (0, p))],
        out_shape=[jax.ShapeDtypeStruct(h1p.shape, BF16), jax.ShapeDtypeStruct((1, DC), F32),
                   jax.ShapeDtypeStruct((2, DC), F32)],
        scratch=[pltpu.VMEM((2, C_HEAD, C_HEAD), F32), pltpu.VMEM((1, HP), F32)], carry=carry)(h1p, o_saved, dy, sp, lbp, gn)


def loss_bwd(xh, rstd, g, b, target, *, name):
    T, N = xh.shape
    tm = _tile(T, 512, 8)

    def body(xh_ref, rs_ref, g_ref, b_ref, t_ref, ls_ref, dr_ref, drb_ref, dg_ref, db_ref):
        i = pl.program_id(0)

        @pl.when(i == 0)
        def _():
            for r in (ls_ref, dg_ref, db_ref):
                r[...] = jnp.zeros_like(r)

        xhv, gv = xh_ref[...], g_ref[...]
        e = xhv * gv + b_ref[...] - t_ref[...]
        ls_ref[...] += 0.5 * jnp.sum(jnp.mean(e * e, axis=-1, keepdims=True), axis=0, keepdims=True)
        dr, dg, db = _ln_bwd(e / N, xhv, rs_ref[...], gv)
        dr_ref[...] = dr
        drb_ref[...] = dr.astype(BF16)
        dg_ref[...] += dg
        db_ref[...] += db

    row = pl.BlockSpec((tm, N), lambda i: (i, 0))
    vec = pl.BlockSpec((1, N), lambda i: (0, 0))
    return _call(body, name=name, grid=(T // tm,),
                 in_specs=[row, pl.BlockSpec((tm, 1), lambda i: (i, 0)), vec, vec, row],
                 out_specs=[pl.BlockSpec((1, LANES), lambda i: (0, 0)), row, row, vec, vec],
                 out_shape=[jax.ShapeDtypeStruct((1, LANES), F32), jax.ShapeDtypeStruct((T, N), F32),
                            jax.ShapeDtypeStruct((T, N), BF16), jax.ShapeDtypeStruct((1, N), F32),
                            jax.ShapeDtypeStruct((1, N), F32)])(xh, rstd, g, b, target)


def _mesh_pos():
    x, y, c = lax.axis_index("x"), lax.axis_index("y"), lax.axis_index("c")
    chips = [(1 - x, y), (x, 1 - y), (1 - x, 1 - y)]
    return x, y, c, chips


def _remote(src, dst, send, recv, j, dev):
    return pltpu.make_async_remote_copy(src_ref=src, dst_ref=dst, send_sem=send.at[j], recv_sem=recv.at[j],
                                        device_id=dev, device_id_type=MESH)


def mesh_ids():
    x, y, c, chips = _mesh_pos()
    return jnp.stack([c] + [2 * cx + cy for cx, cy in chips] + [2 * x + y]).astype(jnp.int32)


def _sibling():
    return (lax.axis_index("x"), lax.axis_index("y"), 1 - lax.axis_index("c"))


def _swap_with_sibling(src, recv, send_sem, recv_sem, step):
    slot = step % 2
    cp = pltpu.make_async_remote_copy(src_ref=src, dst_ref=recv.at[slot], send_sem=send_sem.at[slot],
                                      recv_sem=recv_sem.at[slot], device_id=_sibling(), device_id_type=MESH)
    cp.start()
    cp.wait_recv()
    return cp, slot


def _swap_scratch(br, C, dtype):
    return [pltpu.VMEM((2, br, C), dtype), pltpu.SemaphoreType.DMA((2,)), pltpu.SemaphoreType.DMA((2,))]


def _swap_rows(Rh, C, itemsize):
    return _tile(Rh, max(16, (3 << 20) // (C * itemsize)), 16)


def cast_to_slot(a3, l, me1, name):
    _, R, C = a3.shape
    br = _tile(R, max(8, (1 << 20) // C), 16)

    def body(me_ref, a_ref, o_ref):
        o_ref[...] = a_ref[...].astype(BF16)

    return _call(body, name=name, grid=(R // br,), prefetch=1,
                 in_specs=[pl.BlockSpec((None, br, C), lambda r, me: (l, r, 0))],
                 out_specs=pl.BlockSpec((None, None, br, C), lambda r, me: (0, me[0], r, 0)),
                 out_shape=jax.ShapeDtypeStruct((1, 4, R, C), BF16))(me1, a3)


def all_gather_chips(big, small, *, name):
    nb, ns = len(big), len(small)
    layers = [(t, l) for t in range(nb) for l in range(big[t].shape[0])]
    n_big = 3 * len(layers)
    n_rem = n_big + 3 * ns

    def body(*refs):
        small_in = refs[nb:nb + ns]
        bufs, small_out = refs[nb + ns:2 * nb + ns], refs[2 * nb + ns:2 * (nb + ns)]
        send, recv, loc = refs[2 * (nb + ns):]
        x, y, c, chips = _mesh_pos()
        me = 2 * x + y
        ids = [2 * cx + cy for cx, cy in chips]
        started, sends = [], []
        for t in range(ns):
            cp = pltpu.make_async_copy(small_in[t], small_out[t].at[me], loc.at[t])
            cp.start()
            started.append(cp)
        for q, (t, l) in enumerate(layers):
            for k, chip in enumerate(chips):
                blk = bufs[t].at[l, me, c]
                cp = _remote(blk, blk, send, recv, 3 * q + k, (*chip, c))
                cp.start()
                sends.append(cp)
        for t in range(ns):
            for k, chip in enumerate(chips):
                cp = _remote(small_in[t], small_out[t].at[me], send, recv, n_big + 3 * t + k, (*chip, c))
                cp.start()
                sends.append(cp)
        for q, (t, l) in enumerate(layers):
            for k in range(3):
                blk = bufs[t].at[l, ids[k], c]
                _remote(blk, blk, send, recv, 3 * q + k, (x, y, c)).wait_recv()
        for t in range(ns):
            for k in range(3):
                blk = small_out[t].at[ids[k]]
                _remote(blk, blk, send, recv, n_big + 3 * t + k, (x, y, c)).wait_recv()
        for cp in sends:
            cp.wait_send()
        for cp in started:
            cp.wait()

    out_shape = [jax.ShapeDtypeStruct(a.shape, a.dtype) for a in big]
    out_shape += [jax.ShapeDtypeStruct((4,) + a.shape, a.dtype) for a in small]
    return _call(body, name=name, in_specs=[ANY] * (nb + ns), out_specs=[ANY] * (nb + ns), out_shape=out_shape,
                 aliases={t: t for t in range(nb)},
                 scratch=[pltpu.SemaphoreType.DMA((n_rem,)), pltpu.SemaphoreType.DMA((n_rem,)),
                          pltpu.SemaphoreType.DMA((max(ns, 1),))])(*big, *small)


def all_gather_pair(buf, ids, *, name):
    L, _, _, Rh, C = buf.shape
    br = _swap_rows(Rh, C, 2)
    n_r = Rh // br

    def body(ids_ref, in_ref, o_ref, recv, ssem, rsem):
        step = (pl.program_id(0) * 3 + pl.program_id(1)) * n_r + pl.program_id(2)
        cp, slot = _swap_with_sibling(in_ref, recv, ssem, rsem, step)
        o_ref[...] = recv[slot]
        cp.wait_send()

    at = lambda l, s, h, r: (((l * 4 + s) * 2 + h) * n_r + r, 0)
    out = _call(body, name=name, grid=(L, 3, n_r), prefetch=1,
                in_specs=[pl.BlockSpec((br, C), lambda l, k, r, ids: at(l, ids[1 + k], ids[0], r))],
                out_specs=pl.BlockSpec((br, C), lambda l, k, r, ids: at(l, ids[1 + k], 1 - ids[0], r)),
                out_shape=jax.ShapeDtypeStruct((L * 8 * Rh, C), buf.dtype), aliases={1: 0},
                scratch=_swap_scratch(br, C, BF16))(ids, buf.reshape(L * 8 * Rh, C))
    return out.reshape(buf.shape)


def rs_pair_add(grad, ids, *, name):
    _, _, Rh, C = grad.shape
    br = _swap_rows(Rh, C, 2)
    n_r = Rh // br

    def body(ids_ref, send_ref, keep_ref, pb_ref, own_ref, recv, ssem, rsem):
        ph = pl.program_id(0)
        cp, slot = _swap_with_sibling(send_ref, recv, ssem, rsem, ph * n_r + pl.program_id(1))
        s = keep_ref[...].astype(F32) + recv[slot].astype(F32)

        @pl.when(ph == 0)
        def _():
            own_ref[...] = s

        @pl.when(ph > 0)
        def _():
            pb_ref[...] = s.astype(BF16)

        cp.wait_send()

    rel = lambda ph: (ph + 3) % 4
    at = lambda s, h, r: ((s * 2 + h) * n_r + r, 0)
    grad = grad.reshape(8 * Rh, C)
    return _call(
        body, name=name, grid=(4, n_r), prefetch=1,
        in_specs=[pl.BlockSpec((br, C), lambda ph, r, ids: at(ids[1 + rel(ph)], 1 - ids[0], r)),
                  pl.BlockSpec((br, C), lambda ph, r, ids: at(ids[1 + rel(ph)], ids[0], r))],
        out_specs=[pl.BlockSpec((None, br, C), lambda ph, r, ids: (jnp.maximum(ph - 1, 0), jnp.where(ph == 0, 0, r), 0)),
                   pl.BlockSpec((br, C), lambda ph, r, ids: (jnp.where(ph == 0, r, n_r - 1), 0))],
        out_shape=[jax.ShapeDtypeStruct((3, Rh, C), BF16), jax.ShapeDtypeStruct((Rh, C), F32)],
        scratch=_swap_scratch(br, C, BF16))(ids, grad, grad)


def rs_finish(owns, gots, *, name):
    L = len(owns)
    Rh, C = owns[0].shape
    br = _swap_rows(Rh, C, 4)
    n_r = Rh // br

    def body(*refs):
        own_refs, got_refs, o_ref = refs[:L], refs[L:2 * L], refs[2 * L]
        recv, ssem, rsem = refs[2 * L + 1:]
        l = pl.program_id(0)
        c = lax.axis_index("c")
        for ll in range(L):
            @pl.when(l == ll)
            def _():
                s = own_refs[ll][...]
                for k in range(3):
                    s = s + got_refs[ll][k].astype(F32)
                o_ref[c] = s

        cp, slot = _swap_with_sibling(o_ref.at[c], recv, ssem, rsem, l * n_r + pl.program_id(1))
        o_ref[1 - c] = recv[slot]
        cp.wait_send()

    def at_layer(ll):
        return lambda l, r: jnp.where(l == ll, r, jnp.where(l < ll, 0, n_r - 1))

    in_specs = [pl.BlockSpec((br, C), lambda l, r, ll=ll: (at_layer(ll)(l, r), 0)) for ll in range(L)]
    in_specs += [pl.BlockSpec((3, br, C), lambda l, r, ll=ll: (0, at_layer(ll)(l, r), 0)) for ll in range(L)]
    out = _call(body, name=name, grid=(L, n_r), in_specs=in_specs,
                out_specs=pl.BlockSpec((2, br, C), lambda l, r: (l, r, 0)),
                out_shape=jax.ShapeDtypeStruct((L * 2, Rh, C), F32),
                scratch=_swap_scratch(br, C, F32))(*owns, *gots)
    return out.reshape(L, 2, Rh, C)


def all_reduce_small(buf, *, name):
    rows = buf.shape[0]

    def body(x_ref, o_ref, rbuf, send, recv):
        x, y, c, _ = _mesh_pos()
        o_ref[...] = x_ref[...]
        for k, dev in enumerate(((x, y, 1 - c), (1 - x, y, c), (x, 1 - y, c))):
            cp = _remote(o_ref, rbuf.at[k], send, recv, k, dev)
            cp.start()
            cp.wait()
            o_ref[...] = o_ref[...] + rbuf[k]

    return _call(body, name=name, in_specs=[VMEM_SPEC], out_specs=VMEM_SPEC,
                 out_shape=jax.ShapeDtypeStruct(buf.shape, F32),
                 scratch=[pltpu.VMEM((3, rows, LANES), F32), pltpu.SemaphoreType.DMA((3,)),
                          pltpu.SemaphoreType.DMA((3,))])(buf)


def _pack(arrs):
    parts = []
    for a in arrs:
        f = a.reshape(-1).astype(F32)
        parts.append(jnp.pad(f, (0, (-f.shape[0]) % PACK_ALIGN)))
    return jnp.concatenate(parts).reshape(-1, LANES)


def _unpack(buf, shapes):
    flat = buf.reshape(-1)
    out, off = [], 0
    for s in shapes:
        n = math.prod(s)
        out.append(flat[off:off + n].reshape(s))
        off += n + (-n) % PACK_ALIGN
    return out


_WEIGHTS = ['ev_w_in', 'ev_ln_v_g', 'ev_ln_v_b', 'ev_w_s', 'ev_b_s', 'ev_w_pool', 'ev_pool_scale', 'ev_w_out',
            'od_w_in', 'od_norm_g', 'od_w_out', 'lb_param', 'ffn_w_up', 'ffn_conv_w', 'ffn_conv_b', 'ffn_w_down',
            'ln1_g', 'ln1_b', 'ln2_g', 'ln2_b']
_BIG = ['ev_w_in', 'ev_w_out', 'od_w_in', 'od_w_out', 'ffn_w_up', 'ffn_w_down']
_SMALL = [n for n in _WEIGHTS if n not in _BIG]


def kernel(x, ev_w_in, ev_ln_v_g, ev_ln_v_b, ev_w_s, ev_b_s, ev_w_pool, ev_pool_scale, ev_w_out, od_w_in, od_norm_g, od_w_out, lb_param, ffn_w_up, ffn_conv_w, ffn_conv_b, ffn_w_down, ln1_g, ln1_b, ln2_g, ln2_b, loss_target, m_ev_w_in, m_ev_ln_v_g, m_ev_ln_v_b, m_ev_w_s, m_ev_b_s, m_ev_w_pool, m_ev_pool_scale, m_ev_w_out, m_od_w_in, m_od_norm_g, m_od_w_out, m_lb_param, m_ffn_w_up, m_ffn_conv_w, m_ffn_conv_b, m_ffn_w_down, m_ln1_g, m_ln1_b, m_ln2_g, m_ln2_b, v_ev_w_in, v_ev_ln_v_g, v_ev_ln_v_b, v_ev_w_s, v_ev_b_s, v_ev_w_pool, v_ev_pool_scale, v_ev_w_out, v_od_w_in, v_od_norm_g, v_od_w_out, v_lb_param, v_ffn_w_up, v_ffn_conv_w, v_ffn_conv_b, v_ffn_w_down, v_ln1_g, v_ln1_b, v_ln2_g, v_ln2_b):
    given = dict(locals())
    w = {n: given[n] for n in _WEIGHTS}
    mom = {n: given["m_" + n] for n in _WEIGHTS}
    vel = {n: given["v_" + n] for n in _WEIGHTS}
    x2d = x[0]
    tgt = loss_target[0]
    T, D = x2d.shape
    DA = ev_ln_v_g.shape[-1]
    DB = ev_pool_scale.shape[-1]
    HA = ev_w_s.shape[1]
    G = len(B_WINDOWS)
    CG = DB // G
    DC = 4 * od_norm_g.shape[-1]
    F = ffn_conv_b.shape[-1] // 2
    chip = 2 * lax.axis_index("x") + lax.axis_index("y")

    ids = mesh_ids()
    halves = lambda a: a.reshape(1, 4, 2, a.shape[2] // 2, a.shape[3])
    slot = {(n, l): halves(cast_to_slot(w[n], l, ids[4:5], f"cast_{n}{l}"))
            for n in _BIG for l in range(w[n].shape[0])}

    def riding(*keys):
        return [IciCopy("gather", slot[k]) for k in keys]

    def pair(buf, key):
        g = all_gather_pair(buf, ids, name=f"all_gather_pair_{key[0]}{key[1]}")
        return g.reshape(1, 4, g.shape[3] * 2, g.shape[4])

    early = [('ev_w_in', 0), ('ev_w_out', 0)]
    gathered = all_gather_chips([slot[k] for k in early], [ev_w_pool[0], ffn_conv_w, od_norm_g],
                                name="all_gather_chips")
    wpool_full = gathered[2].transpose(1, 0, 2, 3).reshape(G, CG, CG)
    cw_full = gathered[3].transpose(1, 2, 0, 3).reshape(DEPTH, 3, 2 * F)
    gn_full = gathered[4].reshape(1, DC)
    win0 = pair(gathered[0], early[0])[0]
    wout0 = pair(gathered[1], early[1]).reshape(DA + DB, D)
    wup, wdn = {}, {}
    rh_up = D // 2
    cut1 = (rh_up * 35 // 100) // 16 * 16
    cut2 = cut1 + (rh_up * 18 // 100) // 16 * 16
    cb3 = ffn_conv_b.reshape(DEPTH, 1, 2 * F)
    ws = ev_w_s[0]
    wsT = jnp.swapaxes(ws, 1, 2)
    bsT = ev_b_s[0].T
    wpb = wpool_full.astype(BF16)
    ones = jnp.ones((1, D), F32)
    zeros = jnp.zeros((1, D), F32)
    row = lambda a, l: a[l:l + 1]

    Ns0 = win0.shape[-1]
    Nu = ffn_w_up.shape[-1]
    tm_big = _tile(T, 1024, 8)
    tm_ln = _tile(T, 512, 8)
    tk_ln = _tile(D, 512)
    n_p = DC // HP

    def nat_spec(Ns, tnw):
        nps = Ns // tnw
        return pl.BlockSpec((1, D, tnw), lambda i, j: (j // nps, 0, j % nps))

    perm_spec = pl.BlockSpec((4, D, HP), lambda i, j: (0, 0, j))

    xb16 = cast_bf16(x, "cast_x")[0]
    up0 = IciCopy("gather", slot[('ffn_w_up', 0)], rows=(0, cut1))
    h0 = mm_nn(xb16, win0, w_spec=nat_spec(Ns0, Ns0), P=1, tnw=Ns0, tm=tm_big, n_j=4, name="ev_in", carry=[up0])
    up0 = IciCopy("gather", up0.out, rows=(cut1, cut2 - cut1))
    cat = gating_fwd(h0, ev_ln_v_g, ev_ln_v_b, ws, bsT, wpb, ev_pool_scale, name="gating_fwd", carry=[up0])
    up0 = IciCopy("gather", up0.out, rows=(cut2, rh_up - cut2))

    def mix_ln(a, wmat, res, l, name, carry=()):
        K = a.shape[1]
        tk = _tile(K, 2048)
        return mm_ln(a, wmat, *res, row(ln1_g, l), row(ln1_b, l), w_spec=pl.BlockSpec((tk, D), lambda i, k: (k, 0)),
                     K=K, tk=tk, tm=tm_ln, name=name, carry=carry)

    def ffn_down(f, res, l, carry=()):
        tk = F // 4 if (F // 4) % LANES == 0 else _tile(F, 512)
        return mm_ln(f, wdn[l], *res, row(ln2_g, l), row(ln2_b, l),
                     w_spec=pl.BlockSpec((None, tk, D), lambda i, k: (0, k, 0)), K=F, tk=tk, tm=tm_ln,
                     name=f"ffn_down{l}", carry=carry)

    xh1, y1, rs1 = mix_ln(cat, wout0, (x2d, ones, zeros), 0, "ev_out", carry=[up0])
    wup[0] = pair(up0.out, ('ffn_w_up', 0))
    res1 = (xh1, row(ln1_g, 0), row(ln1_b, 0))
    ride = riding(('ffn_w_down', 0), ('od_w_in', 0))
    hf0, hc0, f0 = ffn_up(y1, wup[0], cw_full, cb3, 0, name="ffn_up0", carry=ride)
    wdn[0] = pair(ride[0].out, ('ffn_w_down', 0)).reshape(1, F, D)
    win1 = pair(ride[1].out, ('od_w_in', 0))[0]
    ride = riding(('od_w_out', 0), ('ffn_w_down', 1))
    xh2, y2, rs2 = ffn_down(f0, res1, 0, carry=ride)
    wout1 = pair(ride[0].out, ('od_w_out', 0)).reshape(DC, D)
    wdn[1] = pair(ride[1].out, ('ffn_w_down', 1)).reshape(1, F, D)
    res2 = (xh2, row(ln2_g, 0), row(ln2_b, 0))
    h1p = mm_nn(y2, win1, w_spec=perm_spec, P=4, tnw=HP, tm=tm_big, n_j=n_p, name="od_in")
    ride = riding(('ffn_w_up', 1))
    yh, o_saved, sp = hgrn_fwd(h1p, lb_param, gn_full, name="hgrn_fwd", carry=ride)
    wup[1] = pair(ride[0].out, ('ffn_w_up', 1))
    xh3, y3, rs3 = mix_ln(yh, wout1, res2, 1, "od_out")
    res3 = (xh3, row(ln1_g, 1), row(ln1_b, 1))
    hf1, hc1, f1 = ffn_up(y3, wup[1], cw_full, cb3, 1, name="ffn_up1")
    xh4, y4, rs4 = ffn_down(f1, res3, 1)
    loss_p, dr, drb, dg_ln2_1, db_ln2_1 = loss_bwd(xh4, rs4, row(ln2_g, 1), row(ln2_b, 1), tgt, name="loss_bwd")

    tt = _tile(T, 2048, 16)
    n_t = T // tt
    tnu = Nu // 2 if (Nu // 2) % LANES == 0 else Nu
    upb = Nu // tnu
    tkd = _tile(D, 1024)

    def pair_sum(g4, name):
        pb, own = rs_pair_add(g4.reshape(4, 2, g4.shape[1] // 2, g4.shape[2]), ids, name="rs_pair_add_" + name)
        return IciCopy("scatter", pb), own

    def g_out(a, gb, name, tkk=None):
        K = a.shape[1]
        tkk = tkk or _tile(K, 1024)
        return mm_tn(a, gb, a_spec=pl.BlockSpec((tt, tkk), lambda kb, nb, t: (t, kb)),
                     g_spec=pl.BlockSpec((tt, tkd), lambda kb, nb, t: (t, nb)),
                     o_spec=pl.BlockSpec((1, tkk, tkd), lambda kb, nb, t: (0, kb, nb)), out_shape=(1, K, D),
                     grid=(K // tkk, D // tkd, n_t), acc_shape=(tkk, tkd), P=1, tnw=tkd, name=name)

    def ffn_bwd(l, dr2, dr2b, f, hf, hc, y_in, xh_in, rs_in):
        g_dn = g_out(f, dr2b, f"g_ffn_down{l}", tkk=F // 4 if (F // 4) % LANES == 0 else None)
        rs_dn = pair_sum(g_dn.reshape(4, F // 4, D), f"ffn_down{l}")
        dh, dcw, dcb = ffn_dgate(dr2b, wdn[l], hf, hc, cw_full, l, name=f"ffn_dgate{l}", carry=[rs_dn[0]])
        g_up = mm_tn(y_in, dh, a_spec=pl.BlockSpec((tt, tkd), lambda kb, nb, t: (t, kb)),
                     g_spec=pl.BlockSpec((None, tt, tnu), lambda kb, nb, t: (nb // (2 * upb), t, nb % (2 * upb))),
                     o_spec=pl.BlockSpec((1, tkd, tnu), lambda kb, nb, t: (nb // upb, kb, nb % upb)),
                     out_shape=(4, D, Nu), grid=(D // tkd, 4 * upb, n_t), acc_shape=(tkd, tnu), P=1, tnw=tnu,
                     name=f"g_ffn_up{l}")
        rs_up = pair_sum(g_up, f"ffn_up{l}")
        tku = tnu
        kps = Nu // tku
        out = mm_nt_res(dh, wup[l], dr2, (xh_in, rs_in, row(ln1_g, l)),
                        a_spec=pl.BlockSpec((None, tm_ln, tku), lambda i, k: (k // (2 * kps), i, k % (2 * kps))),
                        w_spec=pl.BlockSpec((None, 1, D, tku), lambda i, k: (0, k // kps, 0, k % kps)),
                        P=1, tnw=tku, n_k=4 * kps, tm=tm_ln, name=f"d_ffn_in{l}", carry=[rs_up[0]])
        return rs_dn, rs_up, dcw, dcb, out

    rs_dn1, rs_up1, dcw1, dcb1, (dr1, dr1b, dg_ln1_1, db_ln1_1) = ffn_bwd(1, dr, drb, f1, hf1, hc1, y3, xh3, rs3)
    rs_wout1 = pair_sum(g_out(yh, dr1b, "g_od_out").reshape(4, DC // 4, D), "od_out")
    dyh = mm_nt_plain(dr1b, wout1, tm=tm_big, tn=_tile(DC, 512), name="d_od_out")
    dh1p, d_gn, d_lbp = hgrn_bwd(h1p, o_saved, dyh, sp, lb_param, gn_full, name="hgrn_bwd", carry=[rs_wout1[0]])
    g_win1 = mm_tn(y2, dh1p, a_spec=pl.BlockSpec((tt, tkd), lambda kb, nb, t: (t, kb)),
                   g_spec=pl.BlockSpec((tt, 4 * HP), lambda kb, nb, t: (t, nb)),
                   o_spec=pl.BlockSpec((4, tkd, HP), lambda kb, nb, t: (0, kb, nb)), out_shape=(4, D, DC),
                   grid=(D // tkd, n_p, n_t), acc_shape=(tkd, 4 * HP), P=4, tnw=HP, name="g_od_in")
    rs_win1 = pair_sum(g_win1, "od_in")
    dr, drb, dg_ln2_0, db_ln2_0 = mm_nt_res(
        dh1p, win1, dr1, (xh2, rs2, row(ln2_g, 0)), a_spec=pl.BlockSpec((tm_ln, 4 * HP), lambda i, k: (i, k)),
        w_spec=pl.BlockSpec((4, D, HP), lambda i, k: (0, 0, k)), P=4, tnw=HP, n_k=n_p, tm=tm_ln, name="d_od_in",
        carry=[rs_win1[0]])
    rs_dn0, rs_up0, dcw0, dcb0, (dr1, dr1b, dg_ln1_0, db_ln1_0) = ffn_bwd(0, dr, drb, f0, hf0, hc0, y1, xh1, rs1)
    rs_wout0 = pair_sum(g_out(cat, dr1b, "g_ev_out").reshape(4, (DA + DB) // 4, D), "ev_out")
    dcat = mm_nt_plain(dr1b, wout0, tm=tm_big, tn=_tile(DA + DB, 512), name="d_ev_out")
    dh0, d_ws, d_bsT, d_lg, d_lb, d_sc, d_wp = gating_bwd(h0, dcat, ev_ln_v_g, ev_ln_v_b, ws, wsT, bsT, wpb,
                                                          ev_pool_scale, name="gating_bwd", carry=[rs_wout0[0]])
    g_win0 = mm_tn(xb16, dh0, a_spec=pl.BlockSpec((tt, tkd), lambda kb, nb, t: (t, kb)),
                   g_spec=pl.BlockSpec((tt, Ns0), lambda kb, nb, t: (t, nb)),
                   o_spec=pl.BlockSpec((1, tkd, Ns0), lambda kb, nb, t: (nb, kb, 0)), out_shape=(4, D, Ns0),
                   grid=(D // tkd, 4, n_t), acc_shape=(tkd, Ns0), P=1, tnw=Ns0, name="g_ev_in")
    rs_win0 = pair_sum(g_win0, "ev_in")
    grad_x = mm_nt_res(dh0, win0, dr1, None, a_spec=pl.BlockSpec((tm_ln, Ns0), lambda i, k: (i, k)),
                       w_spec=pl.BlockSpec((1, D, Ns0), lambda i, k: (k, 0, 0)), P=1, tnw=Ns0, n_k=4, tm=tm_ln,
                       name="d_ev_in", carry=[rs_win0[0]])

    per_weight = [[rs_win0], [rs_wout0], [rs_win1], [rs_wout1], [rs_up0, rs_up1], [rs_dn0, rs_dn1]]
    shared = [rs_finish([own for _, own in m], [cp.out for cp, _ in m], name="rs_finish_" + n)
              for n, m in zip(_BIG, per_weight)]
    big_g = {n: s.reshape(w[n].shape) for n, s in zip(_BIG, shared)}

    small_full = {
        'ev_ln_v_g': d_lg, 'ev_ln_v_b': d_lb, 'ev_w_s': d_ws[None], 'ev_b_s': d_bsT.T[None], 'ev_w_pool': d_wp[None],
        'ev_pool_scale': d_sc, 'od_norm_g': d_gn, 'lb_param': d_lbp,
        'ffn_conv_w': jnp.stack([jnp.concatenate([dcw0[0], dcw0[1]], axis=-1),
                                 jnp.concatenate([dcw1[0], dcw1[1]], axis=-1)]),
        'ffn_conv_b': jnp.stack([jnp.concatenate([dcb0[0, 0], dcb0[1, 0]]), jnp.concatenate([dcb1[0, 0], dcb1[1, 0]])]),
        'ln1_g': jnp.concatenate([dg_ln1_0, dg_ln1_1]), 'ln1_b': jnp.concatenate([db_ln1_0, db_ln1_1]),
        'ln2_g': jnp.concatenate([dg_ln2_0, dg_ln2_1]), 'ln2_b': jnp.concatenate([db_ln2_0, db_ln2_1])}
    packed = _pack([small_full[n] for n in _SMALL] + [loss_p[0, 0:1]])
    reduced = _unpack(all_reduce_small(packed, name="all_reduce_small"),
                      [small_full[n].shape for n in _SMALL] + [(1,)])
    small_g = dict(zip(_SMALL, reduced[:-1]))
    loss = reduced[-1][0]
    small_g['ev_w_pool'] = lax.dynamic_slice_in_dim(small_g['ev_w_pool'], chip * (CG // 4), CG // 4, axis=2)
    small_g['ffn_conv_w'] = lax.dynamic_slice_in_dim(small_g['ffn_conv_w'], chip * (F // 2), F // 2, axis=2)
    small_g['od_norm_g'] = lax.dynamic_slice_in_dim(small_g['od_norm_g'], chip * (DC // 4), DC // 4, axis=1)

    grads, delta, new_m, new_v = {}, {}, {}, {}
    for n in _BIG:
        grads[n], delta[n], new_m[n], new_v[n] = adamw(w[n], big_g[n], mom[n], vel[n], "adamw_" + n)
    ps = [_pack([d[n] for n in _SMALL]) for d in (w, small_g, mom, vel)]
    upd = adamw(*[p[None] for p in ps], "adamw_small")
    shapes = [w[n].shape for n in _SMALL]
    for d, buf in zip((grads, delta, new_m, new_v), upd):
        d.update(zip(_SMALL, _unpack(buf[0], shapes)))

    return (loss, grad_x[None], *[grads[n] for n in _WEIGHTS], *[delta[n] for n in _WEIGHTS],
            *[new_m[n] for n in _WEIGHTS], *[new_v[n] for n in _WEIGHTS])
```

```python
import math

import jax
import jax.numpy as jnp
from jax import lax
from jax.experimental import pallas as pl
from jax.experimental.pallas import tpu as pltpu

F32 = jnp.float32
BF16 = jnp.bfloat16
MESH = pl.DeviceIdType.MESH
ANY = pl.BlockSpec(memory_space=pl.ANY)
VMEM_SPEC = pl.BlockSpec(memory_space=pltpu.VMEM)

DEPTH = 2
ALPHA = (2 * DEPTH) ** 0.25
LN_EPS = 1e-5
A_HEAD = 128
A_CHUNK = 128
B_WINDOWS = (2, 4, 8, 16)
POOL_HALO = 16
C_HEAD = 128
C_CHUNK = 64
CONV_HALO = 8
ADAM_LR = 0.001
ADAM_B1 = 0.9
ADAM_B2 = 0.999
ADAM_EPS = 1e-08
ADAM_WD = 0.01
ADAM_STEP = 10
V7X_VMEM_LIMIT_BYTES = 56 * 1024 * 1024
LANES = 128
PACK_ALIGN = 8 * LANES


class IciCopy:
    def __init__(self, kind, arr, rows=None):
        self.kind, self.arr, self.out = kind, arr, None
        self.rows = rows


def _carried_copies(items, in_refs, out_refs, send, recv):
    x, y, c, chips = _mesh_pos()
    me = 2 * x + y
    sends, lands = [], []
    for q, (it, src, dst) in enumerate(zip(items, in_refs, out_refs)):
        rows = pl.ds(*(it.rows or (0, it.arr.shape[-2])))
        for k, (cx, cy) in enumerate(chips):
            if it.kind == "gather":
                mine, theirs = dst.at[0, me, c, rows], dst.at[0, 2 * cx + cy, c, rows]
                sends.append(_remote(mine, mine, send, recv, 3 * q + k, (cx, cy, c)))
            else:
                theirs = dst.at[k]
                sends.append(_remote(src.at[k], theirs, send, recv, 3 * q + k, (cx, cy, c)))
            lands.append(_remote(theirs, theirs, send, recv, 3 * q + k, (x, y, c)))
    return sends, lands


def _call(body, *, name, out_shape, grid=(), in_specs=None, out_specs=None, scratch=(), prefetch=0, aliases=None,
          carry=()):
    single = not isinstance(out_specs, (list, tuple))
    in_specs = list(in_specs)
    out_specs = [out_specs] if single else list(out_specs)
    out_shape = [out_shape] if single else list(out_shape)
    scratch = list(scratch)
    aliases = dict(aliases or {})
    n_in, n_out, n_sc, n_c = len(in_specs), len(out_specs), len(scratch), len(carry)
    inner = body
    if n_c:
        assert grid, "a carrier needs a grid"
        for q, it in enumerate(carry):
            if it.kind == "gather":
                aliases[prefetch + n_in + q] = n_out + q
        in_specs += [ANY] * n_c
        out_specs += [ANY] * n_c
        out_shape += [jax.ShapeDtypeStruct(it.arr.shape, it.arr.dtype) for it in carry]
        scratch += [pltpu.SemaphoreType.DMA((3 * n_c,)), pltpu.SemaphoreType.DMA((3 * n_c,))]

        def inner(*refs):
            pre, refs = refs[:prefetch], refs[prefetch:]
            ins, c_in = refs[:n_in], refs[n_in:n_in + n_c]
            outs, c_out = refs[n_in + n_c:n_in + n_c + n_out], refs[n_in + n_c + n_out:n_in + 2 * n_c + n_out]
            rest = refs[n_in + 2 * n_c + n_out:]
            first = last = True
            for d, n in enumerate(grid):
                first = jnp.logical_and(first, pl.program_id(d) == 0)
                last = jnp.logical_and(last, pl.program_id(d) == n - 1)

            @pl.when(first)
            def _():
                for cp in _carried_copies(carry, c_in, c_out, rest[-2], rest[-1])[0]:
                    cp.start()

            body(*pre, *ins, *outs, *rest[:n_sc])

            @pl.when(last)
            def _():
                sends, lands = _carried_copies(carry, c_in, c_out, rest[-2], rest[-1])
                for cp in lands:
                    cp.wait_recv()
                for cp in sends:
                    cp.wait_send()

    spec = pltpu.PrefetchScalarGridSpec(num_scalar_prefetch=prefetch, grid=grid, in_specs=in_specs,
                                        out_specs=out_specs, scratch_shapes=scratch)
    fn = pl.pallas_call(inner, name=name, grid_spec=spec, out_shape=out_shape, input_output_aliases=aliases,
                        compiler_params=pltpu.CompilerParams(vmem_limit_bytes=V7X_VMEM_LIMIT_BYTES))

    def run(*args):
        res = fn(*args, *[it.arr for it in carry])
        for it, o in zip(carry, res[n_out:]):
            it.out = o
        return res[0] if single else list(res[:n_out])

    return run


def _tile(n, pref, unit=LANES):
    if n <= pref:
        return n
    t = (pref // unit) * unit
    while t > unit and n % t:
        t -= unit
    assert n % t == 0, (n, pref, unit)
    return t


def _slabs(n, rows=128):
    return [slice(r, min(r + rows, n)) for r in range(0, n, rows)]


def _dot(a, b):
    return jnp.dot(a, b, preferred_element_type=F32)


def _dot_nt(a, b):
    return lax.dot_general(a, b, (((1,), (1,)), ((), ())), preferred_element_type=F32)


def _dot_tn(a, b):
    return lax.dot_general(a, b, (((0,), (0,)), ((), ())), preferred_element_type=F32)


def _sigmoid(x):
    return jax.nn.sigmoid(x)


_GELU_C = math.sqrt(2.0 / math.pi)


def _gelu(x):
    return 0.5 * x * (1.0 + jnp.tanh(_GELU_C * (x + 0.044715 * x * x * x)))


def _gelu_grad(x):
    th = jnp.tanh(_GELU_C * (x + 0.044715 * x * x * x))
    return 0.5 * (1.0 + th) + 0.5 * x * (1.0 - th * th) * _GELU_C * (1.0 + 3.0 * 0.044715 * x * x)


def _ln_fwd(r, g, b):
    mu = jnp.mean(r, axis=-1, keepdims=True)
    xc = r - mu
    var = jnp.mean(xc * xc, axis=-1, keepdims=True)
    rstd = lax.rsqrt(var + LN_EPS)
    xh = xc * rstd
    return xh * g + b, xh, rstd


def _ln_bwd(dy, xh, rstd, g):
    dxh = dy * g
    m1 = jnp.mean(dxh, axis=-1, keepdims=True)
    m2 = jnp.mean(dxh * xh, axis=-1, keepdims=True)
    dr = rstd * (dxh - m1 - xh * m2)
    return dr, jnp.sum(dy * xh, axis=0, keepdims=True), jnp.sum(dy, axis=0, keepdims=True)


def _exact_tri_dot(tri, x):
    hi = x.astype(BF16)
    r1 = x - hi.astype(F32)
    mid = r1.astype(BF16)
    lo = (r1 - mid.astype(F32)).astype(BF16)
    return _dot(tri, hi) + _dot(tri, mid) + _dot(tri, lo)


def cast_bf16(a3, name):
    L, R, C = a3.shape
    br = _tile(R, max(8, (1 << 20) // C), 8)

    def body(a_ref, o_ref):
        o_ref[...] = a_ref[...].astype(BF16)

    return _call(body, name=name, grid=(L, R // br),
                 in_specs=[pl.BlockSpec((None, br, C), lambda l, r: (l, r, 0))],
                 out_specs=pl.BlockSpec((None, br, C), lambda l, r: (l, r, 0)),
                 out_shape=jax.ShapeDtypeStruct((L, R, C), BF16))(a3)


def adamw(w, g, m, v, name):
    L, R, C = w.shape
    br = _tile(R, max(8, (1 << 19) // C), 8)
    c1 = 1.0 - ADAM_B1 ** ADAM_STEP
    c2 = 1.0 - ADAM_B2 ** ADAM_STEP

    def body(w_ref, g_ref, m_ref, v_ref, go_ref, d_ref, nm_ref, nv_ref):
        gg = g_ref[...]
        nm = ADAM_B1 * m_ref[...] + (1.0 - ADAM_B1) * gg
        nv = ADAM_B2 * v_ref[...] + (1.0 - ADAM_B2) * (gg * gg)
        go_ref[...] = gg
        d_ref[...] = -ADAM_LR * ((nm / c1) / (jnp.sqrt(nv / c2) + ADAM_EPS) + ADAM_WD * w_ref[...])
        nm_ref[...] = nm
        nv_ref[...] = nv

    spec = pl.BlockSpec((None, br, C), lambda l, r: (l, r, 0))
    sds = jax.ShapeDtypeStruct((L, R, C), F32)
    return _call(body, name=name, grid=(L, R // br), in_specs=[spec] * 4, out_specs=[spec] * 4,
                 out_shape=[sds] * 4)(w, g, m, v)


def mm_nn(a, w, *, w_spec, P, tnw, tm, n_j, name, carry=()):
    T, K = a.shape
    bw = P * tnw

    def body(a_ref, w_ref, o_ref):
        av = a_ref[...]
        for p in range(P):
            o_ref[:, p * tnw:(p + 1) * tnw] = _dot(av, w_ref[p]).astype(BF16)

    return _call(body, name=name, grid=(T // tm, n_j),
                 in_specs=[pl.BlockSpec((tm, K), lambda i, j: (i, 0)), w_spec],
                 out_specs=pl.BlockSpec((tm, bw), lambda i, j: (i, j)),
                 out_shape=jax.ShapeDtypeStruct((T, n_j * bw), BF16), carry=carry)(a, w)


def mm_ln(a, w, res, rg, rb, g, b, *, w_spec, K, tk, tm, name, carry=()):
    T, N = res.shape
    n_k = K // tk

    def body(a_ref, w_ref, res_ref, rg_ref, rb_ref, g_ref, b_ref, xh_ref, y_ref, rs_ref, acc):
        k = pl.program_id(1)

        @pl.when(k == 0)
        def _():
            acc[...] = jnp.zeros_like(acc)

        acc[...] += _dot(a_ref[...], w_ref[...])

        @pl.when(k == n_k - 1)
        def _():
            for rows in _slabs(tm):
                r = ALPHA * (res_ref[rows, :] * rg_ref[...] + rb_ref[...]) + acc[rows, :]
                y, xh, rstd = _ln_fwd(r, g_ref[...], b_ref[...])
                xh_ref[rows, :] = xh
                y_ref[rows, :] = y.astype(BF16)
                rs_ref[rows, :] = rstd

    row = pl.BlockSpec((tm, N), lambda i, k: (i, 0))
    vec = pl.BlockSpec((1, N), lambda i, k: (0, 0))
    return _call(body, name=name, grid=(T // tm, n_k),
                 in_specs=[pl.BlockSpec((tm, tk), lambda i, k: (i, k)), w_spec, row, vec, vec, vec, vec],
                 out_specs=[row, row, pl.BlockSpec((tm, 1), lambda i, k: (i, 0))],
                 out_shape=[jax.ShapeDtypeStruct((T, N), F32), jax.ShapeDtypeStruct((T, N), BF16),
                            jax.ShapeDtypeStruct((T, 1), F32)],
                 scratch=[pltpu.VMEM((tm, N), F32)], carry=carry)(a, w, res, rg, rb, g, b)


def mm_nt_plain(a, w, *, tm, tn, name):
    T, K = a.shape
    N = w.shape[0]

    def body(a_ref, w_ref, o_ref):
        o_ref[...] = _dot_nt(a_ref[...], w_ref[...]).astype(BF16)

    return _call(body, name=name, grid=(T // tm, N // tn),
                 in_specs=[pl.BlockSpec((tm, K), lambda i, j: (i, 0)), pl.BlockSpec((tn, K), lambda i, j: (j, 0))],
                 out_specs=pl.BlockSpec((tm, tn), lambda i, j: (i, j)),
                 out_shape=jax.ShapeDtypeStruct((T, N), BF16))(a, w)


def mm_nt_res(a, w, res, ln, *, a_spec, w_spec, P, tnw, n_k, tm, name, carry=()):
    T, N = res.shape
    n_i = T // tm

    def body(*refs):
        if ln is None:
            a_ref, w_ref, res_ref, o_ref, acc = refs
        else:
            a_ref, w_ref, res_ref, xh_ref, rs_ref, g_ref, dr_ref, drb_ref, dg_ref, db_ref, acc = refs
        i = pl.program_id(0)
        k = pl.program_id(1)

        @pl.when(k == 0)
        def _():
            acc[...] = jnp.zeros_like(acc)

        wv = w_ref[0] if P == 1 else jnp.concatenate([w_ref[p] for p in range(P)], axis=1)
        acc[...] += _dot_nt(a_ref[...], wv)

        @pl.when(k == n_k - 1)
        def _():
            if ln is not None:
                @pl.when(i == 0)
                def _():
                    dg_ref[...] = jnp.zeros_like(dg_ref)
                    db_ref[...] = jnp.zeros_like(db_ref)

            for rows in _slabs(tm):
                d = ALPHA * res_ref[rows, :] + acc[rows, :]
                if ln is None:
                    o_ref[rows, :] = d
                else:
                    dr, dg, db = _ln_bwd(d, xh_ref[rows, :], rs_ref[rows, :], g_ref[...])
                    dr_ref[rows, :] = dr
                    drb_ref[rows, :] = dr.astype(BF16)
                    dg_ref[...] += dg
                    db_ref[...] += db

    row = pl.BlockSpec((tm, N), lambda i, k: (i, 0))
    vec = pl.BlockSpec((1, N), lambda i, k: (0, 0))
    scratch = [pltpu.VMEM((tm, N), F32)]
    if ln is None:
        return _call(body, name=name, grid=(n_i, n_k), in_specs=[a_spec, w_spec, row], out_specs=row,
                     out_shape=jax.ShapeDtypeStruct((T, N), F32), scratch=scratch, carry=carry)(a, w, res)
    xh, rstd, g = ln
    return _call(body, name=name, grid=(n_i, n_k),
                 in_specs=[a_spec, w_spec, row, row, pl.BlockSpec((tm, 1), lambda i, k: (i, 0)), vec],
                 out_specs=[row, row, vec, vec],
                 out_shape=[jax.ShapeDtypeStruct((T, N), F32), jax.ShapeDtypeStruct((T, N), BF16),
                            jax.ShapeDtypeStruct((1, N), F32), jax.ShapeDtypeStruct((1, N), F32)],
                 scratch=scratch, carry=carry)(a, w, res, xh, rstd, g)


def mm_tn(a, g, *, a_spec, g_spec, o_spec, out_shape, grid, acc_shape, P, tnw, name):
    n_t = grid[2]

    def body(a_ref, g_ref, o_ref, acc):
        t = pl.program_id(2)

        @pl.when(t == 0)
        def _():
            acc[...] = jnp.zeros_like(acc)

        acc[...] += _dot_tn(a_ref[...], g_ref[...])

        @pl.when(t == n_t - 1)
        def _():
            for p in range(P):
                o_ref[p] = acc[:, p * tnw:(p + 1) * tnw].astype(BF16)

    return _call(body, name=name, grid=grid, in_specs=[a_spec, g_spec], out_specs=o_spec,
                 out_shape=jax.ShapeDtypeStruct(out_shape, BF16),
                 scratch=[pltpu.VMEM(acc_shape, F32)])(a, g)


def _causal_conv(ext, halo, w, b):
    s1 = pltpu.roll(ext, 1, 0)[halo:]
    s2 = pltpu.roll(ext, 2, 0)[halo:]
    return b + w[2:3] * ext[halo:] + w[1:2] * s1 + w[0:1] * s2, s1, s2


def ffn_up(xb, wup, cw, cb, l, *, name, carry=()):
    T, D = xb.shape
    Ns = wup.shape[-1]
    F = 2 * Ns
    tn = _tile(Ns, 256)
    nps = Ns // tn
    n_j = F // tn
    tm = _tile(T, 1024, 8)

    def body(x_ref, wa_ref, wv_ref, cwa_ref, cwv_ref, cba_ref, cbv_ref, h_ref, hc_ref, f_ref, carry):
        i = pl.program_id(1)

        @pl.when(i == 0)
        def _():
            carry[...] = jnp.zeros_like(carry)

        xv = x_ref[...]
        ha = _dot(xv, wa_ref[...])
        hv = _dot(xv, wv_ref[...])
        ca, _, _ = _causal_conv(jnp.concatenate([carry[0], ha], axis=0), CONV_HALO, cwa_ref[...], cba_ref[...])
        cv, _, _ = _causal_conv(jnp.concatenate([carry[1], hv], axis=0), CONV_HALO, cwv_ref[...], cbv_ref[...])
        carry[0] = ha[tm - CONV_HALO:]
        carry[1] = hv[tm - CONV_HALO:]
        h_ref[0] = ha.astype(BF16)
        h_ref[1] = hv.astype(BF16)
        hc_ref[0] = ca.astype(BF16)
        hc_ref[1] = cv.astype(BF16)
        f_ref[...] = (ca * _sigmoid(ca) * cv).astype(BF16)

    wspec_a = pl.BlockSpec((None, None, D, tn), lambda j, i: (0, j // nps, 0, j % nps))
    wspec_v = pl.BlockSpec((None, None, D, tn), lambda j, i: (0, 2 + j // nps, 0, j % nps))
    return _call(
        body, name=name, grid=(n_j, T // tm),
        in_specs=[pl.BlockSpec((tm, D), lambda j, i: (i, 0)), wspec_a, wspec_v,
                  pl.BlockSpec((None, 3, tn), lambda j, i: (l, 0, j)),
                  pl.BlockSpec((None, 3, tn), lambda j, i: (l, 0, n_j + j)),
                  pl.BlockSpec((None, 1, tn), lambda j, i: (l, 0, j)),
                  pl.BlockSpec((None, 1, tn), lambda j, i: (l, 0, n_j + j))],
        out_specs=[pl.BlockSpec((2, tm, tn), lambda j, i: (0, i, j)), pl.BlockSpec((2, tm, tn), lambda j, i: (0, i, j)),
                   pl.BlockSpec((tm, tn), lambda j, i: (i, j))],
        out_shape=[jax.ShapeDtypeStruct((2, T, F), BF16), jax.ShapeDtypeStruct((2, T, F), BF16),
                   jax.ShapeDtypeStruct((T, F), BF16)],
        scratch=[pltpu.VMEM((2, CONV_HALO, tn), F32)], carry=carry)(xb, wup, wup, cw, cw, cb, cb)


def ffn_dgate(db16, wdn, h, hc, cw, l, *, name, carry=()):
    T, D = db16.shape
    F = h.shape[-1]
    tn = _tile(F, 512)
    n_j = F // tn
    tm = _tile(T, 512, 16)
    n_i = T // tm
    n_ext = tm + CONV_HALO

    def body(d_ref, w_ref, h_ref, hc_ref, cwa_ref, cwv_ref, dh_ref, dcw_ref, dcb_ref, carry):
        ip = pl.program_id(1)

        @pl.when(ip == 0)
        def _():
            carry[...] = jnp.zeros_like(carry)
            dcw_ref[...] = jnp.zeros_like(dcw_ref)
            dcb_ref[...] = jnp.zeros_like(dcb_ref)

        df = _dot_nt(d_ref[...], w_ref[...])
        ca = hc_ref[0].astype(F32)
        cv = hc_ref[1].astype(F32)
        sig = _sigmoid(ca)
        sil = ca * sig
        da = df * cv * (sig + sil * (1.0 - sig))
        dv = df * sil
        for half, (dc, w_ref_h) in enumerate(((da, cwa_ref), (dv, cwv_ref))):
            w = w_ref_h[...]
            h0 = h_ref[half].astype(F32)
            ext = jnp.concatenate([dc, carry[half]], axis=0)
            n1 = pltpu.roll(ext, n_ext - 1, 0)[:tm]
            n2 = pltpu.roll(ext, n_ext - 2, 0)[:tm]
            dcb_ref[half] += jnp.sum(dc, axis=0, keepdims=True)
            dcw_ref[half] += jnp.concatenate(
                [jnp.sum(n2 * h0, axis=0, keepdims=True), jnp.sum(n1 * h0, axis=0, keepdims=True),
                 jnp.sum(dc * h0, axis=0, keepdims=True)], axis=0)
            dh_ref[half] = (w[2:3] * dc + w[1:2] * n1 + w[0:1] * n2).astype(BF16)
            carry[half] = dc[:CONV_HALO]

    rev = lambda ip: n_i - 1 - ip
    tile = pl.BlockSpec((2, tm, tn), lambda j, ip: (0, rev(ip), j))
    return _call(
        body, name=name, grid=(n_j, n_i),
        in_specs=[pl.BlockSpec((tm, D), lambda j, ip: (rev(ip), 0)),
                  pl.BlockSpec((None, tn, D), lambda j, ip: (0, j, 0)), tile, tile,
                  pl.BlockSpec((None, 3, tn), lambda j, ip: (l, 0, j)),
                  pl.BlockSpec((None, 3, tn), lambda j, ip: (l, 0, n_j + j))],
        out_specs=[tile, pl.BlockSpec((2, 3, tn), lambda j, ip: (0, 0, j)),
                   pl.BlockSpec((2, 1, tn), lambda j, ip: (0, 0, j))],
        out_shape=[jax.ShapeDtypeStruct((2, T, F), BF16), jax.ShapeDtypeStruct((2, 3, F), F32),
                   jax.ShapeDtypeStruct((2, 1, F), F32)],
        scratch=[pltpu.VMEM((2, CONV_HALO, tn), F32)], carry=carry)(db16, wdn, h, hc, cw, cw)


def _pool_fwd(ext, xb_g, t_glob, win):
    e = ext
    sft = 1
    while sft < win:
        e = e + pltpu.roll(e, sft, 0)
        sft *= 2
    cnt = jnp.minimum(t_glob + 1.0, float(win))
    return e[POOL_HALO:] / cnt - xb_g


def gating_fwd(h0, lg, lb, ws, bsT, wp, sc, *, name, carry=()):
    T = h0.shape[0]
    DA = lg.shape[-1]
    DB = sc.shape[-1]
    HA = DA // A_HEAD
    G = len(B_WINDOWS)
    CG = DB // G
    tm = _tile(T, 512, A_CHUNK)
    n_c = tm // A_CHUNK

    def body(h_ref, halo_ref, lg_ref, lb_ref, ws_ref, bsT_ref, wp_ref, sc_ref, cat_ref):
        i = pl.program_id(0)
        hu = h_ref[:, 0:DA].astype(F32)
        hv = h_ref[:, DA:2 * DA].astype(F32)
        xb = h_ref[:, 2 * DA:].astype(F32)
        u = _gelu(hu)
        vn, _, _ = _ln_fwd(_gelu(hv), lg_ref[...], lb_ref[...])
        vnb = vn.astype(BF16)
        rr = lax.broadcasted_iota(jnp.int32, (A_CHUNK, A_CHUNK), 0)
        cc = lax.broadcasted_iota(jnp.int32, (A_CHUNK, A_CHUNK), 1)
        for hh in range(HA):
            wt = jnp.where(rr >= cc, ws_ref[hh], 0.0).astype(BF16)
            cs = slice(hh * A_HEAD, (hh + 1) * A_HEAD)
            for n in range(n_c):
                rs = slice(n * A_CHUNK, (n + 1) * A_CHUNK)
                s = _dot(wt, vnb[rs, cs]) + bsT_ref[:, hh:hh + 1]
                cat_ref[rs, cs] = (u[rs, cs] * s).astype(BF16)
        halo = jnp.where(i > 0, halo_ref[...].astype(F32), 0.0)
        ext = jnp.concatenate([halo, xb], axis=0)
        t_glob = (i * tm + lax.broadcasted_iota(jnp.int32, (tm, 1), 0)).astype(F32)
        for g, win in enumerate(B_WINDOWS):
            gs = slice(g * CG, (g + 1) * CG)
            p = _pool_fwd(ext[:, gs], xb[:, gs], t_glob, win)
            z = _dot(p.astype(BF16), wp_ref[g])
            cat_ref[:, DA + g * CG:DA + (g + 1) * CG] = (z * sc_ref[:, gs]).astype(BF16)

    full = lambda a: pl.BlockSpec(a.shape, lambda i: (0,) * a.ndim)
    hpb = tm // POOL_HALO
    return _call(
        body, name=name, grid=(T // tm,),
        in_specs=[pl.BlockSpec((tm, 2 * DA + DB), lambda i: (i, 0)),
                  pl.BlockSpec((POOL_HALO, DB), lambda i: (jnp.maximum(i * hpb - 1, 0), 2 * DA // DB)),
                  full(lg), full(lb), full(ws), full(bsT), full(wp), full(sc)],
        out_specs=pl.BlockSpec((tm, DA + DB), lambda i: (i, 0)),
        out_shape=jax.ShapeDtypeStruct((T, DA + DB), BF16), carry=carry)(h0, h0, lg, lb, ws, bsT, wp, sc)


def gating_bwd(h0, dcat, lg, lb, ws, wsT, bsT, wp, sc, *, name, carry=()):
    T = h0.shape[0]
    DA = lg.shape[-1]
    DB = sc.shape[-1]
    HA = DA // A_HEAD
    G = len(B_WINDOWS)
    CG = DB // G
    tm = _tile(T, 512, A_CHUNK)
    n_i = T // tm
    n_c = tm // A_CHUNK
    n_ext = tm + POOL_HALO

    def body(h_ref, halo_ref, dc_ref, dhalo_ref, lg_ref, lb_ref, ws_ref, wsT_ref, bsT_ref, wp_ref, sc_ref,
             dh_ref, dws_ref, dbsT_ref, dlg_ref, dlb_ref, dsc_ref, dwp_ref, dvn_sc):
        i = pl.program_id(0)

        @pl.when(i == 0)
        def _():
            for r in (dws_ref, dbsT_ref, dlg_ref, dlb_ref, dsc_ref, dwp_ref):
                r[...] = jnp.zeros_like(r)

        hu = h_ref[:, 0:DA].astype(F32)
        hv = h_ref[:, DA:2 * DA].astype(F32)
        xb = h_ref[:, 2 * DA:].astype(F32)
        u = _gelu(hu)
        gu = _gelu_grad(hu)
        lgv = lg_ref[...]
        vn, vhat, rstd = _ln_fwd(_gelu(hv), lgv, lb_ref[...])
        vnb = vn.astype(BF16)
        rr = lax.broadcasted_iota(jnp.int32, (A_CHUNK, A_CHUNK), 0)
        cc = lax.broadcasted_iota(jnp.int32, (A_CHUNK, A_CHUNK), 1)
        for hh in range(HA):
            wt = jnp.where(rr >= cc, ws_ref[hh], 0.0).astype(BF16)
            wtT = jnp.where(rr <= cc, wsT_ref[hh], 0.0).astype(BF16)
            cs = slice(hh * A_HEAD, (hh + 1) * A_HEAD)
            dws = jnp.zeros((A_CHUNK, A_CHUNK), F32)
            dbs = jnp.zeros((A_CHUNK, 1), F32)
            for n in range(n_c):
                rs = slice(n * A_CHUNK, (n + 1) * A_CHUNK)
                vb = vnb[rs, cs]
                s = _dot(wt, vb) + bsT_ref[:, hh:hh + 1]
                dya = dc_ref[rs, cs].astype(F32)
                ds = dya * u[rs, cs]
                dh_ref[rs, cs] = (dya * s * gu[rs, cs]).astype(BF16)
                dsb = ds.astype(BF16)
                dbs = dbs + jnp.sum(ds, axis=1, keepdims=True)
                dws = dws + _dot_nt(dsb, vb)
                dvn_sc[rs, cs] = _dot(wtT, dsb)
            dws_ref[hh] += jnp.where(rr >= cc, dws, 0.0)
            dbsT_ref[:, hh:hh + 1] += dbs
        dvg, dlg, dlb = _ln_bwd(dvn_sc[...], vhat, rstd, lgv)
        dlg_ref[...] += dlg
        dlb_ref[...] += dlb
        dh_ref[:, DA:2 * DA] = (dvg * _gelu_grad(hv)).astype(BF16)

        halo = jnp.where(i > 0, halo_ref[...].astype(F32), 0.0)
        ext = jnp.concatenate([halo, xb], axis=0)
        t_glob = (i * tm + lax.broadcasted_iota(jnp.int32, (tm, 1), 0)).astype(F32)
        t_ext = (i * tm + lax.broadcasted_iota(jnp.int32, (n_ext, 1), 0)).astype(F32)
        dyb = dc_ref[:, DA:].astype(F32)
        dhalo = jnp.where(i < n_i - 1, dhalo_ref[...].astype(F32), 0.0)
        dyb_ext = jnp.concatenate([dyb, dhalo], axis=0)
        for g, win in enumerate(B_WINDOWS):
            gs = slice(g * CG, (g + 1) * CG)
            pb = _pool_fwd(ext[:, gs], xb[:, gs], t_glob, win).astype(BF16)
            wpg = wp_ref[g]
            z = _dot(pb, wpg)
            dsc_ref[:, gs] += jnp.sum(dyb[:, gs] * z, axis=0, keepdims=True)
            dzb = (dyb_ext[:, gs] * sc_ref[:, gs]).astype(BF16)
            dwp_ref[g] += _dot_tn(pb, dzb[:tm])
            dp = _dot_nt(dzb, wpg)
            e = dp / jnp.minimum(t_ext + 1.0, float(win))
            sft = 1
            while sft < win:
                e = e + pltpu.roll(e, n_ext - sft, 0)
                sft *= 2
            dh_ref[:, 2 * DA + g * CG:2 * DA + (g + 1) * CG] = (e[:tm] - dp[:tm]).astype(BF16)

    full = lambda a: pl.BlockSpec(a.shape, lambda i: (0,) * a.ndim)
    hpb = tm // POOL_HALO
    n_hb = T // POOL_HALO
    outs = [jax.ShapeDtypeStruct((T, 2 * DA + DB), BF16), jax.ShapeDtypeStruct(ws.shape, F32),
            jax.ShapeDtypeStruct(bsT.shape, F32), jax.ShapeDtypeStruct(lg.shape, F32),
            jax.ShapeDtypeStruct(lb.shape, F32), jax.ShapeDtypeStruct(sc.shape, F32),
            jax.ShapeDtypeStruct(wp.shape, F32)]
    return _call(
        body, name=name, grid=(n_i,),
        in_specs=[pl.BlockSpec((tm, 2 * DA + DB), lambda i: (i, 0)),
                  pl.BlockSpec((POOL_HALO, DB), lambda i: (jnp.maximum(i * hpb - 1, 0), 2 * DA // DB)),
                  pl.BlockSpec((tm, DA + DB), lambda i: (i, 0)),
                  pl.BlockSpec((POOL_HALO, DB), lambda i: (jnp.minimum((i + 1) * hpb, n_hb - 1), DA // DB)),
                  full(lg), full(lb), full(ws), full(wsT), full(bsT), full(wp), full(sc)],
        out_specs=[pl.BlockSpec((tm, 2 * DA + DB), lambda i: (i, 0))] + [full(o) for o in outs[1:]],
        out_shape=outs,
        scratch=[pltpu.VMEM((tm, DA), F32)], carry=carry)(h0, h0, dcat, dcat, lg, lb, ws, wsT, bsT, wp, sc)


HP = 2 * C_HEAD


def _hgrn_gates(h_ref, rows, lbv, tri):
    pre = []
    for r in rows:
        blk = h_ref[r, :].astype(F32)
        q = blk[:, 0:HP]
        sg = _sigmoid(blk[:, HP:2 * HP])
        f = lbv + (1.0 - lbv) * sg
        pre.append(dict(q=q, sq=_sigmoid(q), sg=sg, f=f, k=1.0 - f, lf=jnp.log(f), v=blk[:, 2 * HP:3 * HP],
                        gg=blk[:, 3 * HP:4 * HP]))
    bcums = [_exact_tri_dot(tri, p["lf"]) for p in pre]
    out = []
    for p, bcum in zip(pre, bcums):
        blast = bcum[C_CHUNK - 1:C_CHUNK]
        e_in = jnp.exp(bcum)
        e_out = jnp.exp(-bcum)
        e_end = jnp.exp(blast - bcum)
        out.append(dict(p, e_in=e_in, e_out=e_out, e_end=e_end, qd=p["q"] * p["sq"] * e_in, kd=p["k"] * e_out,
                        ke=p["k"] * e_end, dec=jnp.exp(blast)))
    return out


def _lower_bound(lbp_ref):
    p0, p1 = lbp_ref[0:1], lbp_ref[1:2]
    mx = jnp.maximum(p0, p1)
    e0, e1 = jnp.exp(p0 - mx), jnp.exp(p1 - mx)
    return e1 / (e0 + e1)


def hgrn_fwd(h1p, lbp, gn, *, name, carry=()):
    T = h1p.shape[0]
    DC = gn.shape[-1]
    n_p = DC // HP
    tt = _tile(T, 512, C_CHUNK)
    n_c = tt // C_CHUNK

    def body(h_ref, lbp_ref, gn_ref, y_ref, o_ref, sp_ref, st):
        i = pl.program_id(1)

        @pl.when(i == 0)
        def _():
            st[...] = jnp.zeros_like(st)

        lbv = _lower_bound(lbp_ref)
        gnv = gn_ref[...]
        rr = lax.broadcasted_iota(jnp.int32, (C_CHUNK, C_CHUNK), 0)
        cc = lax.broadcasted_iota(jnp.int32, (C_CHUNK, C_CHUNK), 1)
        causal = rr >= cc
        tri = jnp.where(causal, 1.0, 0.0).astype(BF16)

        heads = [slice(hd * C_HEAD, (hd + 1) * C_HEAD) for hd in range(2)]
        rows = [slice(n * C_CHUNK, (n + 1) * C_CHUNK) for n in range(n_c)]
        nh = [(n, hd) for n in range(n_c) for hd in range(2)]
        gates = _hgrn_gates(h_ref, rows, lbv, tri)
        b16 = lambda key: {(n, hd): gates[n][key][:, heads[hd]].astype(BF16) for n, hd in nh}
        qd, kd, ke, vb = b16("qd"), b16("kd"), b16("ke"), b16("v")
        att = {i: jnp.where(causal, _dot_nt(qd[i], kd[i]), 0.0).astype(BF16) for i in nh}
        intra = {i: _dot(att[i], vb[i]) for i in nh}
        upd = {i: _dot_tn(vb[i], ke[i]) for i in nh}
        s = [st[0], st[1]]
        entering = {}
        for n, hd in nh:
            entering[n, hd] = s[hd]
            sp_ref[hd, n] = s[hd]
            s[hd] = gates[n]["dec"][:, heads[hd]] * s[hd] + upd[n, hd]
        st[0], st[1] = s
        o = {i: intra[i] + _dot_nt(qd[i], entering[i].astype(BF16)) for i in nh}
        for n in range(n_c):
            os_ = [o[n, 0], o[n, 1]]
            o_ref[rows[n], :] = jnp.concatenate(os_, axis=1).astype(BF16)
            ys = [oh * lax.rsqrt(jnp.mean(oh * oh, axis=-1, keepdims=True) + LN_EPS) for oh in os_]
            y_ref[rows[n], :] = (jnp.concatenate(ys, axis=1) * gnv * _sigmoid(gates[n]["gg"])).astype(BF16)

    return _call(
        body, name=name, grid=(n_p, T // tt),
        in_specs=[pl.BlockSpec((tt, 4 * HP), lambda p, i: (i, p)), pl.BlockSpec((2, HP), lambda p, i: (0, p)),
                  pl.BlockSpec((1, HP), lambda p, i: (0, p))],
        out_specs=[pl.BlockSpec((tt, HP), lambda p, i: (i, p)), pl.BlockSpec((tt, HP), lambda p, i: (i, p)),
                   pl.BlockSpec((2, n_c, C_HEAD, C_HEAD), lambda p, i: (p, i, 0, 0))],
        out_shape=[jax.ShapeDtypeStruct((T, DC), BF16), jax.ShapeDtypeStruct((T, DC), BF16),
                   jax.ShapeDtypeStruct((2 * n_p, T // C_CHUNK, C_HEAD, C_HEAD), F32)],
        scratch=[pltpu.VMEM((2, C_HEAD, C_HEAD), F32)], carry=carry)(h1p, lbp, gn)


def hgrn_bwd(h1p, o_saved, dy, sp, lbp, gn, *, name, carry=()):
    T = h1p.shape[0]
    DC = gn.shape[-1]
    n_p = DC // HP
    tt = _tile(T, 512, C_CHUNK)
    n_i = T // tt
    n_c = tt // C_CHUNK

    def body(h_ref, o_ref, dy_ref, sp_ref, lbp_ref, gn_ref, dh_ref, dgn_ref, dlbp_ref, dst, dlb_acc):
        ip = pl.program_id(1)

        @pl.when(ip == 0)
        def _():
            dst[...] = jnp.zeros_like(dst)
            dlb_acc[...] = jnp.zeros_like(dlb_acc)
            dgn_ref[...] = jnp.zeros_like(dgn_ref)

        lbv = _lower_bound(lbp_ref)
        gnv = gn_ref[...]
        rr = lax.broadcasted_iota(jnp.int32, (C_CHUNK, C_CHUNK), 0)
        cc = lax.broadcasted_iota(jnp.int32, (C_CHUNK, C_CHUNK), 1)
        causal = rr >= cc
        tri = jnp.where(causal, 1.0, 0.0).astype(BF16)
        tri_t = jnp.where(rr <= cc, 1.0, 0.0).astype(BF16)
        last_row = lax.broadcasted_iota(jnp.int32, (C_CHUNK, 1), 0) == C_CHUNK - 1

        heads = [slice(hd * C_HEAD, (hd + 1) * C_HEAD) for hd in range(2)]
        rows = [slice(n * C_CHUNK, (n + 1) * C_CHUNK) for n in range(n_c)]
        cat = lambda parts: jnp.concatenate(parts, axis=1)
        nh = [(n, hd) for n in range(n_c) for hd in range(2)]
        pair = lambda d, n: cat([d[n, 0], d[n, 1]])
        gates = _hgrn_gates(h_ref, rows, lbv, tri)
        dgn = jnp.zeros((1, HP), F32)
        dgg, dob = [], {}
        for n in range(n_c):
            o = o_ref[rows[n], :].astype(F32)
            dyv = dy_ref[rows[n], :].astype(F32)
            sgg = _sigmoid(gates[n]["gg"])
            rrs = [lax.rsqrt(jnp.mean(o[:, cs] * o[:, cs], axis=-1, keepdims=True) + LN_EPS) for cs in heads]
            ohat = cat([o[:, cs] * r for cs, r in zip(heads, rrs)])
            dyn = dyv * sgg
            dgg.append(dyv * ohat * gnv * sgg * (1.0 - sgg))
            dgn = dgn + jnp.sum(dyn * ohat, axis=0, keepdims=True)
            dxh = dyn * gnv
            for hd, cs in enumerate(heads):
                dxh_h, oh_h = dxh[:, cs], ohat[:, cs]
                dob[n, hd] = (rrs[hd] * (dxh_h - oh_h * jnp.mean(dxh_h * oh_h, axis=-1, keepdims=True))).astype(BF16)
        dgn_ref[...] += dgn
        b16 = lambda key: {(n, hd): gates[n][key][:, heads[hd]].astype(BF16) for n, hd in nh}
        qd, kd, ke, vb = b16("qd"), b16("kd"), b16("ke"), b16("v")
        s_in = {(n, hd): sp_ref[hd, n] for n, hd in nh}
        att = {i: jnp.where(causal, _dot_nt(qd[i], kd[i]), 0.0).astype(BF16) for i in nh}
        datt = {i: jnp.where(causal, _dot_nt(dob[i], vb[i]), 0.0).astype(BF16) for i in nh}
        grow = {i: _dot_tn(dob[i], qd[i]) for i in nh}
        dv_i = {i: _dot_tn(att[i], dob[i]) for i in nh}
        dqd = {i: _dot(datt[i], kd[i]) + _dot(dob[i], s_in[i].astype(BF16)) for i in nh}
        dkd = {i: _dot_tn(datt[i], qd[i]) for i in nh}
        ds = [dst[0], dst[1]]
        leaving = {}
        for n in reversed(range(n_c)):
            for hd, cs in enumerate(heads):
                leaving[n, hd] = ds[hd]
                ds[hd] = gates[n]["dec"][:, cs] * ds[hd] + grow[n, hd]
        dst[0], dst[1] = ds
        dsb = {i: leaving[i].astype(BF16) for i in nh}
        dv = {i: dv_i[i] + _dot_nt(ke[i], dsb[i]) for i in nh}
        dke = {i: _dot(vb[i], dsb[i]) for i in nh}
        ddec = {i: jnp.sum(leaving[i] * s_in[i], axis=0, keepdims=True) for i in nh}
        mid = []
        for n in range(n_c):
            a = gates[n]
            dqd_n, dkd_n, dke_n = pair(dqd, n), pair(dkd, n), pair(dke, n)
            kek = dke_n * a["ke"]
            dblast = jnp.sum(kek, axis=0, keepdims=True) + pair(ddec, n) * a["dec"]
            dbcum = dqd_n * a["qd"] - dkd_n * a["kd"] - kek + jnp.where(last_row, dblast, 0.0)
            mid.append((dqd_n * a["e_in"], dkd_n * a["e_out"] + dke_n * a["e_end"], dbcum))
        dlf = [_exact_tri_dot(tri_t, m[2]) for m in mid]
        dlb = jnp.zeros((1, HP), F32)
        for n in range(n_c):
            a = gates[n]
            dqs, dk, _ = mid[n]
            df = dlf[n] / a["f"] - dk
            dlb = dlb + jnp.sum(df * (1.0 - a["sg"]), axis=0, keepdims=True)
            dfl = df * (1.0 - lbv) * a["sg"] * (1.0 - a["sg"])
            dq = dqs * a["sq"] * (1.0 + a["q"] * (1.0 - a["sq"]))
            dh_ref[rows[n], :] = cat([dq, dfl, pair(dv, n), dgg[n]]).astype(BF16)
        dlb_acc[...] += dlb

        @pl.when(ip == n_i - 1)
        def _():
            d1 = dlb_acc[...] * lbv * (1.0 - lbv)
            dlbp_ref[...] = jnp.concatenate([-d1, d1], axis=0)

    rev = lambda ip: n_i - 1 - ip
    return _call(
        body, name=name, grid=(n_p, n_i),
        in_specs=[pl.BlockSpec((tt, 4 * HP), lambda p, ip: (rev(ip), p)),
                  pl.BlockSpec((tt, HP), lambda p, ip: (rev(ip), p)),
                  pl.BlockSpec((tt, HP), lambda p, ip: (rev(ip), p)),
                  pl.BlockSpec((2, n_c, C_HEAD, C_HEAD), lambda p, ip: (p, rev(ip), 0, 0)),
                  pl.BlockSpec((2, HP), lambda p, ip: (0, p)), pl.BlockSpec((1, HP), lambda p, ip: (0, p))],
        out_specs=[pl.BlockSpec((tt, 4 * HP), lambda p, ip: (rev(ip), p)),
                   pl.BlockSpec((1, HP), lambda p, ip: (0, p)), pl.BlockSpec((2, HP), lambda p, ip: (0, p))],
        out_shape=[jax.ShapeDtypeStruct(h1p.shape, BF16), jax.ShapeDtypeStruct((1, DC), F32),
                   jax.ShapeDtypeStruct((2, DC), F32)],
        scratch=[pltpu.VMEM((2, C_HEAD, C_HEAD), F32), pltpu.VMEM((1, HP), F32)], carry=carry)(h1p, o_saved, dy, sp, lbp, gn)


def loss_bwd(xh, rstd, g, b, target, *, name):
    T, N = xh.shape
    tm = _tile(T, 512, 8)

    def body(xh_ref, rs_ref, g_ref, b_ref, t_ref, ls_ref, dr_ref, drb_ref, dg_ref, db_ref):
        i = pl.program_id(0)

        @pl.when(i == 0)
        def _():
            for r in (ls_ref, dg_ref, db_ref):
                r[...] = jnp.zeros_like(r)

        xhv, gv = xh_ref[...], g_ref[...]
        e = xhv * gv + b_ref[...] - t_ref[...]
        ls_ref[...] += 0.5 * jnp.sum(jnp.mean(e * e, axis=-1, keepdims=True), axis=0, keepdims=True)
        dr, dg, db = _ln_bwd(e / N, xhv, rs_ref[...], gv)
        dr_ref[...] = dr
        drb_ref[...] = dr.astype(BF16)
        dg_ref[...] += dg
        db_ref[...] += db

    row = pl.BlockSpec((tm, N), lambda i: (i, 0))
    vec = pl.BlockSpec((1, N), lambda i: (0, 0))
    return _call(body, name=name, grid=(T // tm,),
                 in_specs=[row, pl.BlockSpec((tm, 1), lambda i: (i, 0)), vec, vec, row],
                 out_specs=[pl.BlockSpec((1, LANES), lambda i: (0, 0)), row, row, vec, vec],
                 out_shape=[jax.ShapeDtypeStruct((1, LANES), F32), jax.ShapeDtypeStruct((T, N), F32),
                            jax.ShapeDtypeStruct((T, N), BF16), jax.ShapeDtypeStruct((1, N), F32),
                            jax.ShapeDtypeStruct((1, N), F32)])(xh, rstd, g, b, target)


def _mesh_pos():
    x, y, c = lax.axis_index("x"), lax.axis_index("y"), lax.axis_index("c")
    chips = [(1 - x, y), (x, 1 - y), (1 - x, 1 - y)]
    return x, y, c, chips


def _remote(src, dst, send, recv, j, dev):
    return pltpu.make_async_remote_copy(src_ref=src, dst_ref=dst, send_sem=send.at[j], recv_sem=recv.at[j],
                                        device_id=dev, device_id_type=MESH)


def mesh_ids():
    x, y, c, chips = _mesh_pos()
    return jnp.stack([c] + [2 * cx + cy for cx, cy in chips] + [2 * x + y]).astype(jnp.int32)


def _sibling():
    return (lax.axis_index("x"), lax.axis_index("y"), 1 - lax.axis_index("c"))


def _swap_with_sibling(src, recv, send_sem, recv_sem, step):
    slot = step % 2
    cp = pltpu.make_async_remote_copy(src_ref=src, dst_ref=recv.at[slot], send_sem=send_sem.at[slot],
                                      recv_sem=recv_sem.at[slot], device_id=_sibling(), device_id_type=MESH)
    cp.start()
    cp.wait_recv()
    return cp, slot


def _swap_scratch(br, C, dtype):
    return [pltpu.VMEM((2, br, C), dtype), pltpu.SemaphoreType.DMA((2,)), pltpu.SemaphoreType.DMA((2,))]


def _swap_rows(Rh, C, itemsize):
    return _tile(Rh, max(16, (3 << 20) // (C * itemsize)), 16)


def cast_to_slot(a3, l, me1, name):
    _, R, C = a3.shape
    br = _tile(R, max(8, (1 << 20) // C), 16)

    def body(me_ref, a_ref, o_ref):
        o_ref[...] = a_ref[...].astype(BF16)

    return _call(body, name=name, grid=(R // br,), prefetch=1,
                 in_specs=[pl.BlockSpec((None, br, C), lambda r, me: (l, r, 0))],
                 out_specs=pl.BlockSpec((None, None, br, C), lambda r, me: (0, me[0], r, 0)),
                 out_shape=jax.ShapeDtypeStruct((1, 4, R, C), BF16))(me1, a3)


def all_gather_chips(big, small, *, name):
    nb, ns = len(big), len(small)
    layers = [(t, l) for t in range(nb) for l in range(big[t].shape[0])]
    n_big = 3 * len(layers)
    n_rem = n_big + 3 * ns

    def body(*refs):
        small_in = refs[nb:nb + ns]
        bufs, small_out = refs[nb + ns:2 * nb + ns], refs[2 * nb + ns:2 * (nb + ns)]
        send, recv, loc = refs[2 * (nb + ns):]
        x, y, c, chips = _mesh_pos()
        me = 2 * x + y
        ids = [2 * cx + cy for cx, cy in chips]
        started, sends = [], []
        for t in range(ns):
            cp = pltpu.make_async_copy(small_in[t], small_out[t].at[me], loc.at[t])
            cp.start()
            started.append(cp)
        for q, (t, l) in enumerate(layers):
            for k, chip in enumerate(chips):
                blk = bufs[t].at[l, me, c]
                cp = _remote(blk, blk, send, recv, 3 * q + k, (*chip, c))
                cp.start()
                sends.append(cp)
        for t in range(ns):
            for k, chip in enumerate(chips):
                cp = _remote(small_in[t], small_out[t].at[me], send, recv, n_big + 3 * t + k, (*chip, c))
                cp.start()
                sends.append(cp)
        for q, (t, l) in enumerate(layers):
            for k in range(3):
                blk = bufs[t].at[l, ids[k], c]
                _remote(blk, blk, send, recv, 3 * q + k, (x, y, c)).wait_recv()
        for t in range(ns):
            for k in range(3):
                blk = small_out[t].at[ids[k]]
                _remote(blk, blk, send, recv, n_big + 3 * t + k, (x, y, c)).wait_recv()
        for cp in sends:
            cp.wait_send()
        for cp in started:
            cp.wait()

    out_shape = [jax.ShapeDtypeStruct(a.shape, a.dtype) for a in big]
    out_shape += [jax.ShapeDtypeStruct((4,) + a.shape, a.dtype) for a in small]
    return _call(body, name=name, in_specs=[ANY] * (nb + ns), out_specs=[ANY] * (nb + ns), out_shape=out_shape,
                 aliases={t: t for t in range(nb)},
                 scratch=[pltpu.SemaphoreType.DMA((n_rem,)), pltpu.SemaphoreType.DMA((n_rem,)),
                          pltpu.SemaphoreType.DMA((max(ns, 1),))])(*big, *small)


def all_gather_pair(buf, ids, *, name):
    L, _, _, Rh, C = buf.shape
    br = _swap_rows(Rh, C, 2)
    n_r = Rh // br

    def body(ids_ref, in_ref, o_ref, recv, ssem, rsem):
        step = (pl.program_id(0) * 3 + pl.program_id(1)) * n_r + pl.program_id(2)
        cp, slot = _swap_with_sibling(in_ref, recv, ssem, rsem, step)
        o_ref[...] = recv[slot]
        cp.wait_send()

    at = lambda l, s, h, r: (((l * 4 + s) * 2 + h) * n_r + r, 0)
    out = _call(body, name=name, grid=(L, 3, n_r), prefetch=1,
                in_specs=[pl.BlockSpec((br, C), lambda l, k, r, ids: at(l, ids[1 + k], ids[0], r))],
                out_specs=pl.BlockSpec((br, C), lambda l, k, r, ids: at(l, ids[1 + k], 1 - ids[0], r)),
                out_shape=jax.ShapeDtypeStruct((L * 8 * Rh, C), buf.dtype), aliases={1: 0},
                scratch=_swap_scratch(br, C, BF16))(ids, buf.reshape(L * 8 * Rh, C))
    return out.reshape(buf.shape)


def rs_pair_add(grad, ids, *, name):
    _, _, Rh, C = grad.shape
    br = _swap_rows(Rh, C, 2)
    n_r = Rh // br

    def body(ids_ref, send_ref, keep_ref, pb_ref, own_ref, recv, ssem, rsem):
        ph = pl.program_id(0)
        cp, slot = _swap_with_sibling(send_ref, recv, ssem, rsem, ph * n_r + pl.program_id(1))
        s = keep_ref[...].astype(F32) + recv[slot].astype(F32)

        @pl.when(ph == 0)
        def _():
            own_ref[...] = s

        @pl.when(ph > 0)
        def _():
            pb_ref[...] = s.astype(BF16)

        cp.wait_send()

    rel = lambda ph: (ph + 3) % 4
    at = lambda s, h, r: ((s * 2 + h) * n_r + r, 0)
    grad = grad.reshape(8 * Rh, C)
    return _call(
        body, name=name, grid=(4, n_r), prefetch=1,
        in_specs=[pl.BlockSpec((br, C), lambda ph, r, ids: at(ids[1 + rel(ph)], 1 - ids[0], r)),
                  pl.BlockSpec((br, C), lambda ph, r, ids: at(ids[1 + rel(ph)], ids[0], r))],
        out_specs=[pl.BlockSpec((None, br, C), lambda ph, r, ids: (jnp.maximum(ph - 1, 0), jnp.where(ph == 0, 0, r), 0)),
                   pl.BlockSpec((br, C), lambda ph, r, ids: (jnp.where(ph == 0, r, n_r - 1), 0))],
        out_shape=[jax.ShapeDtypeStruct((3, Rh, C), BF16), jax.ShapeDtypeStruct((Rh, C), F32)],
        scratch=_swap_scratch(br, C, BF16))(ids, grad, grad)


def rs_finish(owns, gots, *, name):
    L = len(owns)
    Rh, C = owns[0].shape
    br = _swap_rows(Rh, C, 4)
    n_r = Rh // br

    def body(*refs):
        own_refs, got_refs, o_ref = refs[:L], refs[L:2 * L], refs[2 * L]
        recv, ssem, rsem = refs[2 * L + 1:]
        l = pl.program_id(0)
        c = lax.axis_index("c")
        for ll in range(L):
            @pl.when(l == ll)
            def _():
                s = own_refs[ll][...]
                for k in range(3):
                    s = s + got_refs[ll][k].astype(F32)
                o_ref[c] = s

        cp, slot = _swap_with_sibling(o_ref.at[c], recv, ssem, rsem, l * n_r + pl.program_id(1))
        o_ref[1 - c] = recv[slot]
        cp.wait_send()

    def at_layer(ll):
        return lambda l, r: jnp.where(l == ll, r, jnp.where(l < ll, 0, n_r - 1))

    in_specs = [pl.BlockSpec((br, C), lambda l, r, ll=ll: (at_layer(ll)(l, r), 0)) for ll in range(L)]
    in_specs += [pl.BlockSpec((3, br, C), lambda l, r, ll=ll: (0, at_layer(ll)(l, r), 0)) for ll in range(L)]
    out = _call(body, name=name, grid=(L, n_r), in_specs=in_specs,
                out_specs=pl.BlockSpec((2, br, C), lambda l, r: (l, r, 0)),
                out_shape=jax.ShapeDtypeStruct((L * 2, Rh, C), F32),
                scratch=_swap_scratch(br, C, F32))(*owns, *gots)
    return out.reshape(L, 2, Rh, C)


def all_reduce_small(buf, *, name):
    rows = buf.shape[0]

    def body(x_ref, o_ref, rbuf, send, recv):
        x, y, c, _ = _mesh_pos()
        o_ref[...] = x_ref[...]
        for k, dev in enumerate(((x, y, 1 - c), (1 - x, y, c), (x, 1 - y, c))):
            cp = _remote(o_ref, rbuf.at[k], send, recv, k, dev)
            cp.start()
            cp.wait()
            o_ref[...] = o_ref[...] + rbuf[k]

    return _call(body, name=name, in_specs=[VMEM_SPEC], out_specs=VMEM_SPEC,
                 out_shape=jax.ShapeDtypeStruct(buf.shape, F32),
                 scratch=[pltpu.VMEM((3, rows, LANES), F32), pltpu.SemaphoreType.DMA((3,)),
                          pltpu.SemaphoreType.DMA((3,))])(buf)


def _pack(arrs):
    parts = []
    for a in arrs:
        f = a.reshape(-1).astype(F32)
        parts.append(jnp.pad(f, (0, (-f.shape[0]) % PACK_ALIGN)))
    return jnp.concatenate(parts).reshape(-1, LANES)


def _unpack(buf, shapes):
    flat = buf.reshape(-1)
    out, off = [], 0
    for s in shapes:
        n = math.prod(s)
        out.append(flat[off:off + n].reshape(s))
        off += n + (-n) % PACK_ALIGN
    return out


_WEIGHTS = ['ev_w_in', 'ev_ln_v_g', 'ev_ln_v_b', 'ev_w_s', 'ev_b_s', 'ev_w_pool', 'ev_pool_scale', 'ev_w_out',
            'od_w_in', 'od_norm_g', 'od_w_out', 'lb_param', 'ffn_w_up', 'ffn_conv_w', 'ffn_conv_b', 'ffn_w_down',
            'ln1_g', 'ln1_b', 'ln2_g', 'ln2_b']
_BIG = ['ev_w_in', 'ev_w_out', 'od_w_in', 'od_w_out', 'ffn_w_up', 'ffn_w_down']
_SMALL = [n for n in _WEIGHTS if n not in _BIG]


def kernel(x, ev_w_in, ev_ln_v_g, ev_ln_v_b, ev_w_s, ev_b_s, ev_w_pool, ev_pool_scale, ev_w_out, od_w_in, od_norm_g, od_w_out, lb_param, ffn_w_up, ffn_conv_w, ffn_conv_b, ffn_w_down, ln1_g, ln1_b, ln2_g, ln2_b, loss_target, m_ev_w_in, m_ev_ln_v_g, m_ev_ln_v_b, m_ev_w_s, m_ev_b_s, m_ev_w_pool, m_ev_pool_scale, m_ev_w_out, m_od_w_in, m_od_norm_g, m_od_w_out, m_lb_param, m_ffn_w_up, m_ffn_conv_w, m_ffn_conv_b, m_ffn_w_down, m_ln1_g, m_ln1_b, m_ln2_g, m_ln2_b, v_ev_w_in, v_ev_ln_v_g, v_ev_ln_v_b, v_ev_w_s, v_ev_b_s, v_ev_w_pool, v_ev_pool_scale, v_ev_w_out, v_od_w_in, v_od_norm_g, v_od_w_out, v_lb_param, v_ffn_w_up, v_ffn_conv_w, v_ffn_conv_b, v_ffn_w_down, v_ln1_g, v_ln1_b, v_ln2_g, v_ln2_b):
    given = dict(locals())
    w = {n: given[n] for n in _WEIGHTS}
    mom = {n: given["m_" + n] for n in _WEIGHTS}
    vel = {n: given["v_" + n] for n in _WEIGHTS}
    x2d = x[0]
    tgt = loss_target[0]
    T, D = x2d.shape
    DA = ev_ln_v_g.shape[-1]
    DB = ev_pool_scale.shape[-1]
    HA = ev_w_s.shape[1]
    G = len(B_WINDOWS)
    CG = DB // G
    DC = 4 * od_norm_g.shape[-1]
    F = ffn_conv_b.shape[-1] // 2
    chip = 2 * lax.axis_index("x") + lax.axis_index("y")

    ids = mesh_ids()
    halves = lambda a: a.reshape(1, 4, 2, a.shape[2] // 2, a.shape[3])
    slot = {(n, l): halves(cast_to_slot(w[n], l, ids[4:5], f"cast_{n}{l}"))
            for n in _BIG for l in range(w[n].shape[0])}

    def riding(*keys):
        return [IciCopy("gather", slot[k]) for k in keys]

    def pair(buf, key):
        g = all_gather_pair(buf, ids, name=f"all_gather_pair_{key[0]}{key[1]}")
        return g.reshape(1, 4, g.shape[3] * 2, g.shape[4])

    early = [('ev_w_in', 0), ('ev_w_out', 0)]
    gathered = all_gather_chips([slot[k] for k in early], [ev_w_pool[0], ffn_conv_w, od_norm_g],
                                name="all_gather_chips")
    wpool_full = gathered[2].transpose(1, 0, 2, 3).reshape(G, CG, CG)
    cw_full = gathered[3].transpose(1, 2, 0, 3).reshape(DEPTH, 3, 2 * F)
    gn_full = gathered[4].reshape(1, DC)
    win0 = pair(gathered[0], early[0])[0]
    wout0 = pair(gathered[1], early[1]).reshape(DA + DB, D)
    wup, wdn = {}, {}
    rh_up = D // 2
    cut1 = (rh_up * 35 // 100) // 16 * 16
    cut2 = cut1 + (rh_up * 18 // 100) // 16 * 16
    cb3 = ffn_conv_b.reshape(DEPTH, 1, 2 * F)
    ws = ev_w_s[0]
    wsT = jnp.swapaxes(ws, 1, 2)
    bsT = ev_b_s[0].T
    wpb = wpool_full.astype(BF16)
    ones = jnp.ones((1, D), F32)
    zeros = jnp.zeros((1, D), F32)
    row = lambda a, l: a[l:l + 1]

    Ns0 = win0.shape[-1]
    Nu = ffn_w_up.shape[-1]
    tm_big = _tile(T, 1024, 8)
    tm_ln = _tile(T, 512, 8)
    tk_ln = _tile(D, 512)
    n_p = DC // HP

    def nat_spec(Ns, tnw):
        nps = Ns // tnw
        return pl.BlockSpec((1, D, tnw), lambda i, j: (j // nps, 0, j % nps))

    perm_spec = pl.BlockSpec((4, D, HP), lambda i, j: (0, 0, j))

    xb16 = cast_bf16(x, "cast_x")[0]
    up0 = IciCopy("gather", slot[('ffn_w_up', 0)], rows=(0, cut1))
    h0 = mm_nn(xb16, win0, w_spec=nat_spec(Ns0, Ns0), P=1, tnw=Ns0, tm=tm_big, n_j=4, name="ev_in", carry=[up0])
    up0 = IciCopy("gather", up0.out, rows=(cut1, cut2 - cut1))
    cat = gating_fwd(h0, ev_ln_v_g, ev_ln_v_b, ws, bsT, wpb, ev_pool_scale, name="gating_fwd", carry=[up0])
    up0 = IciCopy("gather", up0.out, rows=(cut2, rh_up - cut2))

    def mix_ln(a, wmat, res, l, name, carry=()):
        K = a.shape[1]
        tk = _tile(K, 2048)
        return mm_ln(a, wmat, *res, row(ln1_g, l), row(ln1_b, l), w_spec=pl.BlockSpec((tk, D), lambda i, k: (k, 0)),
                     K=K, tk=tk, tm=tm_ln, name=name, carry=carry)

    def ffn_down(f, res, l, carry=()):
        tk = F // 4 if (F // 4) % LANES == 0 else _tile(F, 512)
        return mm_ln(f, wdn[l], *res, row(ln2_g, l), row(ln2_b, l),
                     w_spec=pl.BlockSpec((None, tk, D), lambda i, k: (0, k, 0)), K=F, tk=tk, tm=tm_ln,
                     name=f"ffn_down{l}", carry=carry)

    xh1, y1, rs1 = mix_ln(cat, wout0, (x2d, ones, zeros), 0, "ev_out", carry=[up0])
    wup[0] = pair(up0.out, ('ffn_w_up', 0))
    res1 = (xh1, row(ln1_g, 0), row(ln1_b, 0))
    ride = riding(('ffn_w_down', 0), ('od_w_in', 0))
    hf0, hc0, f0 = ffn_up(y1, wup[0], cw_full, cb3, 0, name="ffn_up0", carry=ride)
    wdn[0] = pair(ride[0].out, ('ffn_w_down', 0)).reshape(1, F, D)
    win1 = pair(ride[1].out, ('od_w_in', 0))[0]
    cut1 = (rh_up * 40 // 100) // 16 * 16
    cut2 = 2 * cut1
    up1 = IciCopy("gather", slot[('ffn_w_up', 1)], rows=(0, cut1))
    ride = riding(('od_w_out', 0))
    xh2, y2, rs2 = ffn_down(f0, res1, 0, carry=ride + [up1])
    wout1 = pair(ride[0].out, ('od_w_out', 0)).reshape(DC, D)
    res2 = (xh2, row(ln2_g, 0), row(ln2_b, 0))
    up1 = IciCopy("gather", up1.out, rows=(cut1, cut2 - cut1))
    h1p = mm_nn(y2, win1, w_spec=perm_spec, P=4, tnw=HP, tm=tm_big, n_j=n_p, name="od_in", carry=[up1])
    up1 = IciCopy("gather", up1.out, rows=(cut2, rh_up - cut2))
    yh, o_saved, sp = hgrn_fwd(h1p, lb_param, gn_full, name="hgrn_fwd", carry=[up1])
    wup[1] = pair(up1.out, ('ffn_w_up', 1))
    xh3, y3, rs3 = mix_ln(yh, wout1, res2, 1, "od_out")
    res3 = (xh3, row(ln1_g, 1), row(ln1_b, 1))
    ride = riding(('ffn_w_down', 1))
    hf1, hc1, f1 = ffn_up(y3, wup[1], cw_full, cb3, 1, name="ffn_up1", carry=ride)
    wdn[1] = pair(ride[0].out, ('ffn_w_down', 1)).reshape(1, F, D)
    xh4, y4, rs4 = ffn_down(f1, res3, 1)
    loss_p, dr, drb, dg_ln2_1, db_ln2_1 = loss_bwd(xh4, rs4, row(ln2_g, 1), row(ln2_b, 1), tgt, name="loss_bwd")

    tt = _tile(T, 2048, 16)
    n_t = T // tt
    tnu = Nu // 2 if (Nu // 2) % LANES == 0 else Nu
    upb = Nu // tnu
    tkd = _tile(D, 1024)

    def pair_sum(g4, name):
        pb, own = rs_pair_add(g4.reshape(4, 2, g4.shape[1] // 2, g4.shape[2]), ids, name="rs_pair_add_" + name)
        return IciCopy("scatter", pb), own

    def g_out(a, gb, name, tkk=None):
        K = a.shape[1]
        tkk = tkk or _tile(K, 1024)
        return mm_tn(a, gb, a_spec=pl.BlockSpec((tt, tkk), lambda kb, nb, t: (t, kb)),
                     g_spec=pl.BlockSpec((tt, tkd), lambda kb, nb, t: (t, nb)),
                     o_spec=pl.BlockSpec((1, tkk, tkd), lambda kb, nb, t: (0, kb, nb)), out_shape=(1, K, D),
                     grid=(K // tkk, D // tkd, n_t), acc_shape=(tkk, tkd), P=1, tnw=tkd, name=name)

    def ffn_bwd(l, dr2, dr2b, f, hf, hc, y_in, xh_in, rs_in):
        g_dn = g_out(f, dr2b, f"g_ffn_down{l}", tkk=F // 4 if (F // 4) % LANES == 0 else None)
        rs_dn = pair_sum(g_dn.reshape(4, F // 4, D), f"ffn_down{l}")
        dh, dcw, dcb = ffn_dgate(dr2b, wdn[l], hf, hc, cw_full, l, name=f"ffn_dgate{l}", carry=[rs_dn[0]])
        g_up = mm_tn(y_in, dh, a_spec=pl.BlockSpec((tt, tkd), lambda kb, nb, t: (t, kb)),
                     g_spec=pl.BlockSpec((None, tt, tnu), lambda kb, nb, t: (nb // (2 * upb), t, nb % (2 * upb))),
                     o_spec=pl.BlockSpec((1, tkd, tnu), lambda kb, nb, t: (nb // upb, kb, nb % upb)),
                     out_shape=(4, D, Nu), grid=(D // tkd, 4 * upb, n_t), acc_shape=(tkd, tnu), P=1, tnw=tnu,
                     name=f"g_ffn_up{l}")
        rs_up = pair_sum(g_up, f"ffn_up{l}")
        tku = tnu
        kps = Nu // tku
        out = mm_nt_res(dh, wup[l], dr2, (xh_in, rs_in, row(ln1_g, l)),
                        a_spec=pl.BlockSpec((None, tm_ln, tku), lambda i, k: (k // (2 * kps), i, k % (2 * kps))),
                        w_spec=pl.BlockSpec((None, 1, D, tku), lambda i, k: (0, k // kps, 0, k % kps)),
                        P=1, tnw=tku, n_k=4 * kps, tm=tm_ln, name=f"d_ffn_in{l}", carry=[rs_up[0]])
        return rs_dn, rs_up, dcw, dcb, out

    rs_dn1, rs_up1, dcw1, dcb1, (dr1, dr1b, dg_ln1_1, db_ln1_1) = ffn_bwd(1, dr, drb, f1, hf1, hc1, y3, xh3, rs3)
    rs_wout1 = pair_sum(g_out(yh, dr1b, "g_od_out").reshape(4, DC // 4, D), "od_out")
    dyh = mm_nt_plain(dr1b, wout1, tm=tm_big, tn=_tile(DC, 512), name="d_od_out")
    dh1p, d_gn, d_lbp = hgrn_bwd(h1p, o_saved, dyh, sp, lb_param, gn_full, name="hgrn_bwd", carry=[rs_wout1[0]])
    g_win1 = mm_tn(y2, dh1p, a_spec=pl.BlockSpec((tt, tkd), lambda kb, nb, t: (t, kb)),
                   g_spec=pl.BlockSpec((tt, 4 * HP), lambda kb, nb, t: (t, nb)),
                   o_spec=pl.BlockSpec((4, tkd, HP), lambda kb, nb, t: (0, kb, nb)), out_shape=(4, D, DC),
                   grid=(D // tkd, n_p, n_t), acc_shape=(tkd, 4 * HP), P=4, tnw=HP, name="g_od_in")
    rs_win1 = pair_sum(g_win1, "od_in")
    dr, drb, dg_ln2_0, db_ln2_0 = mm_nt_res(
        dh1p, win1, dr1, (xh2, rs2, row(ln2_g, 0)), a_spec=pl.BlockSpec((tm_ln, 4 * HP), lambda i, k: (i, k)),
        w_spec=pl.BlockSpec((4, D, HP), lambda i, k: (0, 0, k)), P=4, tnw=HP, n_k=n_p, tm=tm_ln, name="d_od_in",
        carry=[rs_win1[0]])
    rs_dn0, rs_up0, dcw0, dcb0, (dr1, dr1b, dg_ln1_0, db_ln1_0) = ffn_bwd(0, dr, drb, f0, hf0, hc0, y1, xh1, rs1)
    rs_wout0 = pair_sum(g_out(cat, dr1b, "g_ev_out").reshape(4, (DA + DB) // 4, D), "ev_out")
    dcat = mm_nt_plain(dr1b, wout0, tm=tm_big, tn=_tile(DA + DB, 512), name="d_ev_out")
    dh0, d_ws, d_bsT, d_lg, d_lb, d_sc, d_wp = gating_bwd(h0, dcat, ev_ln_v_g, ev_ln_v_b, ws, wsT, bsT, wpb,
                                                          ev_pool_scale, name="gating_bwd", carry=[rs_wout0[0]])
    g_win0 = mm_tn(xb16, dh0, a_spec=pl.BlockSpec((tt, tkd), lambda kb, nb, t: (t, kb)),
                   g_spec=pl.BlockSpec((tt, Ns0), lambda kb, nb, t: (t, nb)),
                   o_spec=pl.BlockSpec((1, tkd, Ns0), lambda kb, nb, t: (nb, kb, 0)), out_shape=(4, D, Ns0),
                   grid=(D // tkd, 4, n_t), acc_shape=(tkd, Ns0), P=1, tnw=Ns0, name="g_ev_in")
    rs_win0 = pair_sum(g_win0, "ev_in")
    grad_x = mm_nt_res(dh0, win0, dr1, None, a_spec=pl.BlockSpec((tm_ln, Ns0), lambda i, k: (i, k)),
                       w_spec=pl.BlockSpec((1, D, Ns0), lambda i, k: (k, 0, 0)), P=1, tnw=Ns0, n_k=4, tm=tm_ln,
                       name="d_ev_in", carry=[rs_win0[0]])

    per_weight = [[rs_win0], [rs_wout0], [rs_win1], [rs_wout1], [rs_up0, rs_up1], [rs_dn0, rs_dn1]]
    shared = [rs_finish([own for _, own in m], [cp.out for cp, _ in m], name="rs_finish_" + n)
              for n, m in zip(_BIG, per_weight)]
    big_g = {n: s.reshape(w[n].shape) for n, s in zip(_BIG, shared)}

    small_full = {
        'ev_ln_v_g': d_lg, 'ev_ln_v_b': d_lb, 'ev_w_s': d_ws[None], 'ev_b_s': d_bsT.T[None], 'ev_w_pool': d_wp[None],
        'ev_pool_scale': d_sc, 'od_norm_g': d_gn, 'lb_param': d_lbp,
        'ffn_conv_w': jnp.stack([jnp.concatenate([dcw0[0], dcw0[1]], axis=-1),
                                 jnp.concatenate([dcw1[0], dcw1[1]], axis=-1)]),
        'ffn_conv_b': jnp.stack([jnp.concatenate([dcb0[0, 0], dcb0[1, 0]]), jnp.concatenate([dcb1[0, 0], dcb1[1, 0]])]),
        'ln1_g': jnp.concatenate([dg_ln1_0, dg_ln1_1]), 'ln1_b': jnp.concatenate([db_ln1_0, db_ln1_1]),
        'ln2_g': jnp.concatenate([dg_ln2_0, dg_ln2_1]), 'ln2_b': jnp.concatenate([db_ln2_0, db_ln2_1])}
    packed = _pack([small_full[n] for n in _SMALL] + [loss_p[0, 0:1]])
    reduced = _unpack(all_reduce_small(packed, name="all_reduce_small"),
                      [small_full[n].shape for n in _SMALL] + [(1,)])
    small_g = dict(zip(_SMALL, reduced[:-1]))
    loss = reduced[-1][0]
    small_g['ev_w_pool'] = lax.dynamic_slice_in_dim(small_g['ev_w_pool'], chip * (CG // 4), CG // 4, axis=2)
    small_g['ffn_conv_w'] = lax.dynamic_slice_in_dim(small_g['ffn_conv_w'], chip * (F // 2), F // 2, axis=2)
    small_g['od_norm_g'] = lax.dynamic_slice_in_dim(small_g['od_norm_g'], chip * (DC // 4), DC // 4, axis=1)

    grads, delta, new_m, new_v = {}, {}, {}, {}
    for n in _BIG:
        grads[n], delta[n], new_m[n], new_v[n] = adamw(w[n], big_g[n], mom[n], vel[n], "adamw_" + n)
    ps = [_pack([d[n] for n in _SMALL]) for d in (w, small_g, mom, vel)]
    upd = adamw(*[p[None] for p in ps], "adamw_small")
    shapes = [w[n].shape for n in _SMALL]
    for d, buf in zip((grads, delta, new_m, new_v), upd):
        d.update(zip(_SMALL, _unpack(buf[0], shapes)))

    return (loss, grad_x[None], *[grads[n] for n in _WEIGHTS], *[delta[n] for n in _WEIGHTS],
            *[new_m[n] for n in _WEIGHTS], *[new_v[n] for n in _WEIGHTS])
```

```python
import math

import jax
import jax.numpy as jnp
from jax import lax
from jax.experimental import pallas as pl
from jax.experimental.pallas import tpu as pltpu

F32 = jnp.float32
BF16 = jnp.bfloat16
MESH = pl.DeviceIdType.MESH
ANY = pl.BlockSpec(memory_space=pl.ANY)
VMEM_SPEC = pl.BlockSpec(memory_space=pltpu.VMEM)

DEPTH = 2
ALPHA = (2 * DEPTH) ** 0.25
LN_EPS = 1e-5
A_HEAD = 128
A_CHUNK = 128
B_WINDOWS = (2, 4, 8, 16)
POOL_HALO = 16
C_HEAD = 128
C_CHUNK = 64
CONV_HALO = 8
ADAM_LR = 0.001
ADAM_B1 = 0.9
ADAM_B2 = 0.999
ADAM_EPS = 1e-08
ADAM_WD = 0.01
ADAM_STEP = 10
V7X_VMEM_LIMIT_BYTES = 56 * 1024 * 1024
LANES = 128
PACK_ALIGN = 8 * LANES


class IciCopy:
    def __init__(self, kind, arr, rows=None):
        self.kind, self.arr, self.out = kind, arr, None
        self.rows = rows


def _carried_copies(items, in_refs, out_refs, send, recv):
    x, y, c, chips = _mesh_pos()
    me = 2 * x + y
    sends, lands = [], []
    for q, (it, src, dst) in enumerate(zip(items, in_refs, out_refs)):
        rows = pl.ds(*(it.rows or (0, it.arr.shape[-2])))
        for k, (cx, cy) in enumerate(chips):
            if it.kind == "gather":
                mine, theirs = dst.at[0, me, c, rows], dst.at[0, 2 * cx + cy, c, rows]
                sends.append(_remote(mine, mine, send, recv, 3 * q + k, (cx, cy, c)))
            else:
                theirs = dst.at[k]
                sends.append(_remote(src.at[k], theirs, send, recv, 3 * q + k, (cx, cy, c)))
            lands.append(_remote(theirs, theirs, send, recv, 3 * q + k, (x, y, c)))
    return sends, lands


def _call(body, *, name, out_shape, grid=(), in_specs=None, out_specs=None, scratch=(), prefetch=0, aliases=None,
          carry=()):
    single = not isinstance(out_specs, (list, tuple))
    in_specs = list(in_specs)
    out_specs = [out_specs] if single else list(out_specs)
    out_shape = [out_shape] if single else list(out_shape)
    scratch = list(scratch)
    aliases = dict(aliases or {})
    n_in, n_out, n_sc, n_c = len(in_specs), len(out_specs), len(scratch), len(carry)
    inner = body
    if n_c:
        assert grid, "a carrier needs a grid"
        for q, it in enumerate(carry):
            if it.kind == "gather":
                aliases[prefetch + n_in + q] = n_out + q
        in_specs += [ANY] * n_c
        out_specs += [ANY] * n_c
        out_shape += [jax.ShapeDtypeStruct(it.arr.shape, it.arr.dtype) for it in carry]
        scratch += [pltpu.SemaphoreType.DMA((3 * n_c,)), pltpu.SemaphoreType.DMA((3 * n_c,))]

        def inner(*refs):
            pre, refs = refs[:prefetch], refs[prefetch:]
            ins, c_in = refs[:n_in], refs[n_in:n_in + n_c]
            outs, c_out = refs[n_in + n_c:n_in + n_c + n_out], refs[n_in + n_c + n_out:n_in + 2 * n_c + n_out]
            rest = refs[n_in + 2 * n_c + n_out:]
            first = last = True
            for d, n in enumerate(grid):
                first = jnp.logical_and(first, pl.program_id(d) == 0)
                last = jnp.logical_and(last, pl.program_id(d) == n - 1)

            @pl.when(first)
            def _():
                for cp in _carried_copies(carry, c_in, c_out, rest[-2], rest[-1])[0]:
                    cp.start()

            body(*pre, *ins, *outs, *rest[:n_sc])

            @pl.when(last)
            def _():
                sends, lands = _carried_copies(carry, c_in, c_out, rest[-2], rest[-1])
                for cp in lands:
                    cp.wait_recv()
                for cp in sends:
                    cp.wait_send()

    spec = pltpu.PrefetchScalarGridSpec(num_scalar_prefetch=prefetch, grid=grid, in_specs=in_specs,
                                        out_specs=out_specs, scratch_shapes=scratch)
    fn = pl.pallas_call(inner, name=name, grid_spec=spec, out_shape=out_shape, input_output_aliases=aliases,
                        compiler_params=pltpu.CompilerParams(vmem_limit_bytes=V7X_VMEM_LIMIT_BYTES))

    def run(*args):
        res = fn(*args, *[it.arr for it in carry])
        for it, o in zip(carry, res[n_out:]):
            it.out = o
        return res[0] if single else list(res[:n_out])

    return run


def _tile(n, pref, unit=LANES):
    if n <= pref:
        return n
    t = (pref // unit) * unit
    while t > unit and n % t:
        t -= unit
    assert n % t == 0, (n, pref, unit)
    return t


def _slabs(n, rows=128):
    return [slice(r, min(r + rows, n)) for r in range(0, n, rows)]


def _dot(a, b):
    return jnp.dot(a, b, preferred_element_type=F32)


def _dot_nt(a, b):
    return lax.dot_general(a, b, (((1,), (1,)), ((), ())), preferred_element_type=F32)


def _dot_tn(a, b):
    return lax.dot_general(a, b, (((0,), (0,)), ((), ())), preferred_element_type=F32)


def _sigmoid(x):
    return jax.nn.sigmoid(x)


_GELU_C = math.sqrt(2.0 / math.pi)


def _gelu(x):
    return 0.5 * x * (1.0 + jnp.tanh(_GELU_C * (x + 0.044715 * x * x * x)))


def _gelu_grad(x):
    th = jnp.tanh(_GELU_C * (x + 0.044715 * x * x * x))
    return 0.5 * (1.0 + th) + 0.5 * x * (1.0 - th * th) * _GELU_C * (1.0 + 3.0 * 0.044715 * x * x)


def _ln_fwd(r, g, b):
    mu = jnp.mean(r, axis=-1, keepdims=True)
    xc = r - mu
    var = jnp.mean(xc * xc, axis=-1, keepdims=True)
    rstd = lax.rsqrt(var + LN_EPS)
    xh = xc * rstd
    return xh * g + b, xh, rstd


def _ln_bwd(dy, xh, rstd, g):
    dxh = dy * g
    m1 = jnp.mean(dxh, axis=-1, keepdims=True)
    m2 = jnp.mean(dxh * xh, axis=-1, keepdims=True)
    dr = rstd * (dxh - m1 - xh * m2)
    return dr, jnp.sum(dy * xh, axis=0, keepdims=True), jnp.sum(dy, axis=0, keepdims=True)


def _exact_tri_dot(tri, x):
    hi = x.astype(BF16)
    r1 = x - hi.astype(F32)
    mid = r1.astype(BF16)
    lo = (r1 - mid.astype(F32)).astype(BF16)
    return _dot(tri, hi) + _dot(tri, mid) + _dot(tri, lo)


def cast_bf16(a3, name):
    L, R, C = a3.shape
    br = _tile(R, max(8, (1 << 20) // C), 8)

    def body(a_ref, o_ref):
        o_ref[...] = a_ref[...].astype(BF16)

    return _call(body, name=name, grid=(L, R // br),
                 in_specs=[pl.BlockSpec((None, br, C), lambda l, r: (l, r, 0))],
                 out_specs=pl.BlockSpec((None, br, C), lambda l, r: (l, r, 0)),
                 out_shape=jax.ShapeDtypeStruct((L, R, C), BF16))(a3)


def adamw(w, g, m, v, name):
    L, R, C = w.shape
    br = _tile(R, max(8, (1 << 19) // C), 8)
    c1 = 1.0 - ADAM_B1 ** ADAM_STEP
    c2 = 1.0 - ADAM_B2 ** ADAM_STEP

    def body(w_ref, g_ref, m_ref, v_ref, go_ref, d_ref, nm_ref, nv_ref):
        gg = g_ref[...]
        nm = ADAM_B1 * m_ref[...] + (1.0 - ADAM_B1) * gg
        nv = ADAM_B2 * v_ref[...] + (1.0 - ADAM_B2) * (gg * gg)
        go_ref[...] = gg
        d_ref[...] = -ADAM_LR * ((nm / c1) / (jnp.sqrt(nv / c2) + ADAM_EPS) + ADAM_WD * w_ref[...])
        nm_ref[...] = nm
        nv_ref[...] = nv

    spec = pl.BlockSpec((None, br, C), lambda l, r: (l, r, 0))
    sds = jax.ShapeDtypeStruct((L, R, C), F32)
    return _call(body, name=name, grid=(L, R // br), in_specs=[spec] * 4, out_specs=[spec] * 4,
                 out_shape=[sds] * 4)(w, g, m, v)


def mm_nn(a, w, *, w_spec, P, tnw, tm, n_j, name, carry=()):
    T, K = a.shape
    bw = P * tnw

    def body(a_ref, w_ref, o_ref):
        av = a_ref[...]
        for p in range(P):
            o_ref[:, p * tnw:(p + 1) * tnw] = _dot(av, w_ref[p]).astype(BF16)

    return _call(body, name=name, grid=(T // tm, n_j),
                 in_specs=[pl.BlockSpec((tm, K), lambda i, j: (i, 0)), w_spec],
                 out_specs=pl.BlockSpec((tm, bw), lambda i, j: (i, j)),
                 out_shape=jax.ShapeDtypeStruct((T, n_j * bw), BF16), carry=carry)(a, w)


def mm_ln(a, w, res, rg, rb, g, b, *, w_spec, K, tk, tm, name, carry=()):
    T, N = res.shape
    n_k = K // tk

    def body(a_ref, w_ref, res_ref, rg_ref, rb_ref, g_ref, b_ref, xh_ref, y_ref, rs_ref, acc):
        k = pl.program_id(1)

        @pl.when(k == 0)
        def _():
            acc[...] = jnp.zeros_like(acc)

        acc[...] += _dot(a_ref[...], w_ref[...])

        @pl.when(k == n_k - 1)
        def _():
            for rows in _slabs(tm):
                r = ALPHA * (res_ref[rows, :] * rg_ref[...] + rb_ref[...]) + acc[rows, :]
                y, xh, rstd = _ln_fwd(r, g_ref[...], b_ref[...])
                xh_ref[rows, :] = xh
                y_ref[rows, :] = y.astype(BF16)
                rs_ref[rows, :] = rstd

    row = pl.BlockSpec((tm, N), lambda i, k: (i, 0))
    vec = pl.BlockSpec((1, N), lambda i, k: (0, 0))
    return _call(body, name=name, grid=(T // tm, n_k),
                 in_specs=[pl.BlockSpec((tm, tk), lambda i, k: (i, k)), w_spec, row, vec, vec, vec, vec],
                 out_specs=[row, row, pl.BlockSpec((tm, 1), lambda i, k: (i, 0))],
                 out_shape=[jax.ShapeDtypeStruct((T, N), F32), jax.ShapeDtypeStruct((T, N), BF16),
                            jax.ShapeDtypeStruct((T, 1), F32)],
                 scratch=[pltpu.VMEM((tm, N), F32)], carry=carry)(a, w, res, rg, rb, g, b)


def mm_nt_plain(a, w, *, tm, tn, name):
    T, K = a.shape
    N = w.shape[0]

    def body(a_ref, w_ref, o_ref):
        o_ref[...] = _dot_nt(a_ref[...], w_ref[...]).astype(BF16)

    return _call(body, name=name, grid=(T // tm, N // tn),
                 in_specs=[pl.BlockSpec((tm, K), lambda i, j: (i, 0)), pl.BlockSpec((tn, K), lambda i, j: (j, 0))],
                 out_specs=pl.BlockSpec((tm, tn), lambda i, j: (i, j)),
                 out_shape=jax.ShapeDtypeStruct((T, N), BF16))(a, w)


def mm_nt_res(a, w, res, ln, *, a_spec, w_spec, P, tnw, n_k, tm, name, carry=()):
    T, N = res.shape
    n_i = T // tm

    def body(*refs):
        if ln is None:
            a_ref, w_ref, res_ref, o_ref, acc = refs
        else:
            a_ref, w_ref, res_ref, xh_ref, rs_ref, g_ref, dr_ref, drb_ref, dg_ref, db_ref, acc = refs
        i = pl.program_id(0)
        k = pl.program_id(1)

        @pl.when(k == 0)
        def _():
            acc[...] = jnp.zeros_like(acc)

        wv = w_ref[0] if P == 1 else jnp.concatenate([w_ref[p] for p in range(P)], axis=1)
        acc[...] += _dot_nt(a_ref[...], wv)

        @pl.when(k == n_k - 1)
        def _():
            if ln is not None:
                @pl.when(i == 0)
                def _():
                    dg_ref[...] = jnp.zeros_like(dg_ref)
                    db_ref[...] = jnp.zeros_like(db_ref)

            for rows in _slabs(tm):
                d = ALPHA * res_ref[rows, :] + acc[rows, :]
                if ln is None:
                    o_ref[rows, :] = d
                else:
                    dr, dg, db = _ln_bwd(d, xh_ref[rows, :], rs_ref[rows, :], g_ref[...])
                    dr_ref[rows, :] = dr
                    drb_ref[rows, :] = dr.astype(BF16)
                    dg_ref[...] += dg
                    db_ref[...] += db

    row = pl.BlockSpec((tm, N), lambda i, k: (i, 0))
    vec = pl.BlockSpec((1, N), lambda i, k: (0, 0))
    scratch = [pltpu.VMEM((tm, N), F32)]
    if ln is None:
        return _call(body, name=name, grid=(n_i, n_k), in_specs=[a_spec, w_spec, row], out_specs=row,
                     out_shape=jax.ShapeDtypeStruct((T, N), F32), scratch=scratch, carry=carry)(a, w, res)
    xh, rstd, g = ln
    return _call(body, name=name, grid=(n_i, n_k),
                 in_specs=[a_spec, w_spec, row, row, pl.BlockSpec((tm, 1), lambda i, k: (i, 0)), vec],
                 out_specs=[row, row, vec, vec],
                 out_shape=[jax.ShapeDtypeStruct((T, N), F32), jax.ShapeDtypeStruct((T, N), BF16),
                            jax.ShapeDtypeStruct((1, N), F32), jax.ShapeDtypeStruct((1, N), F32)],
                 scratch=scratch, carry=carry)(a, w, res, xh, rstd, g)


def mm_tn(a, g, *, a_spec, g_spec, o_spec, out_shape, grid, acc_shape, P, tnw, name):
    n_t = grid[2]

    def body(a_ref, g_ref, o_ref, acc):
        t = pl.program_id(2)

        @pl.when(t == 0)
        def _():
            acc[...] = jnp.zeros_like(acc)

        acc[...] += _dot_tn(a_ref[...], g_ref[...])

        @pl.when(t == n_t - 1)
        def _():
            for p in range(P):
                o_ref[p] = acc[:, p * tnw:(p + 1) * tnw].astype(BF16)

    return _call(body, name=name, grid=grid, in_specs=[a_spec, g_spec], out_specs=o_spec,
                 out_shape=jax.ShapeDtypeStruct(out_shape, BF16),
                 scratch=[pltpu.VMEM(acc_shape, F32)])(a, g)


def _causal_conv(ext, halo, w, b):
    s1 = pltpu.roll(ext, 1, 0)[halo:]
    s2 = pltpu.roll(ext, 2, 0)[halo:]
    return b + w[2:3] * ext[halo:] + w[1:2] * s1 + w[0:1] * s2, s1, s2


def ffn_up(xb, wup, cw, cb, l, *, name, carry=()):
    T, D = xb.shape
    Ns = wup.shape[-1]
    F = 2 * Ns
    tn = _tile(Ns, 256)
    nps = Ns // tn
    n_j = F // tn
    tm = _tile(T, 1024, 8)

    def body(x_ref, wa_ref, wv_ref, cwa_ref, cwv_ref, cba_ref, cbv_ref, h_ref, hc_ref, f_ref, carry):
        i = pl.program_id(1)

        @pl.when(i == 0)
        def _():
            carry[...] = jnp.zeros_like(carry)

        xv = x_ref[...]
        ha = _dot(xv, wa_ref[...])
        hv = _dot(xv, wv_ref[...])
        ca, _, _ = _causal_conv(jnp.concatenate([carry[0], ha], axis=0), CONV_HALO, cwa_ref[...], cba_ref[...])
        cv, _, _ = _causal_conv(jnp.concatenate([carry[1], hv], axis=0), CONV_HALO, cwv_ref[...], cbv_ref[...])
        carry[0] = ha[tm - CONV_HALO:]
        carry[1] = hv[tm - CONV_HALO:]
        h_ref[0] = ha.astype(BF16)
        h_ref[1] = hv.astype(BF16)
        hc_ref[0] = ca.astype(BF16)
        hc_ref[1] = cv.astype(BF16)
        f_ref[...] = (ca * _sigmoid(ca) * cv).astype(BF16)

    wspec_a = pl.BlockSpec((None, None, D, tn), lambda j, i: (0, j // nps, 0, j % nps))
    wspec_v = pl.BlockSpec((None, None, D, tn), lambda j, i: (0, 2 + j // nps, 0, j % nps))
    pair_tile = pl.BlockSpec((2, tm, tn), lambda j, i: (0, i, j))
    return _call(
        body, name=name, grid=(n_j, T // tm),
        in_specs=[pl.BlockSpec((tm, D), lambda j, i: (i, 0)), wspec_a, wspec_v,
                  pl.BlockSpec((None, 3, tn), lambda j, i: (l, 0, j)),
                  pl.BlockSpec((None, 3, tn), lambda j, i: (l, 0, n_j + j)),
                  pl.BlockSpec((None, 1, tn), lambda j, i: (l, 0, j)),
                  pl.BlockSpec((None, 1, tn), lambda j, i: (l, 0, n_j + j))],
        out_specs=[pair_tile, pair_tile, pl.BlockSpec((tm, tn), lambda j, i: (i, j))],
        out_shape=[jax.ShapeDtypeStruct((2, T, F), BF16), jax.ShapeDtypeStruct((2, T, F), BF16),
                   jax.ShapeDtypeStruct((T, F), BF16)],
        scratch=[pltpu.VMEM((2, CONV_HALO, tn), F32)], carry=carry)(xb, wup, wup, cw, cw, cb, cb)


def ffn_dgate(db16, wdn, h, hc, cw, l, *, name, carry=()):
    T, D = db16.shape
    F = h.shape[-1]
    tn = _tile(F, 512)
    n_j = F // tn
    tm = _tile(T, 512, 16)
    n_i = T // tm
    n_ext = tm + CONV_HALO

    def body(d_ref, w_ref, h_ref, hc_ref, cwa_ref, cwv_ref, dh_ref, dcw_ref, dcb_ref, carry):
        ip = pl.program_id(1)

        @pl.when(ip == 0)
        def _():
            carry[...] = jnp.zeros_like(carry)
            dcw_ref[...] = jnp.zeros_like(dcw_ref)
            dcb_ref[...] = jnp.zeros_like(dcb_ref)

        df = _dot_nt(d_ref[...], w_ref[...])
        ca = hc_ref[0].astype(F32)
        cv = hc_ref[1].astype(F32)
        sig = _sigmoid(ca)
        sil = ca * sig
        da = df * cv * (sig + sil * (1.0 - sig))
        dv = df * sil
        for half, (dc, w_ref_h) in enumerate(((da, cwa_ref), (dv, cwv_ref))):
            w = w_ref_h[...]
            h0 = h_ref[half].astype(F32)
            ext = jnp.concatenate([dc, carry[half]], axis=0)
            n1 = pltpu.roll(ext, n_ext - 1, 0)[:tm]
            n2 = pltpu.roll(ext, n_ext - 2, 0)[:tm]
            dcb_ref[half] += jnp.sum(dc, axis=0, keepdims=True)
            dcw_ref[half] += jnp.concatenate(
                [jnp.sum(n2 * h0, axis=0, keepdims=True), jnp.sum(n1 * h0, axis=0, keepdims=True),
                 jnp.sum(dc * h0, axis=0, keepdims=True)], axis=0)
            dh_ref[half] = (w[2:3] * dc + w[1:2] * n1 + w[0:1] * n2).astype(BF16)
            carry[half] = dc[:CONV_HALO]

    rev = lambda ip: n_i - 1 - ip
    tile = pl.BlockSpec((2, tm, tn), lambda j, ip: (0, rev(ip), j))
    return _call(
        body, name=name, grid=(n_j, n_i),
        in_specs=[pl.BlockSpec((tm, D), lambda j, ip: (rev(ip), 0)),
                  pl.BlockSpec((None, tn, D), lambda j, ip: (0, j, 0)), tile, tile,
                  pl.BlockSpec((None, 3, tn), lambda j, ip: (l, 0, j)),
                  pl.BlockSpec((None, 3, tn), lambda j, ip: (l, 0, n_j + j))],
        out_specs=[tile, pl.BlockSpec((2, 3, tn), lambda j, ip: (0, 0, j)),
                   pl.BlockSpec((2, 1, tn), lambda j, ip: (0, 0, j))],
        out_shape=[jax.ShapeDtypeStruct((2, T, F), BF16), jax.ShapeDtypeStruct((2, 3, F), F32),
                   jax.ShapeDtypeStruct((2, 1, F), F32)],
        scratch=[pltpu.VMEM((2, CONV_HALO, tn), F32)], carry=carry)(db16, wdn, h, hc, cw, cw)


def _pool_fwd(ext, xb_g, t_glob, win):
    e = ext
    sft = 1
    while sft < win:
        e = e + pltpu.roll(e, sft, 0)
        sft *= 2
    cnt = jnp.minimum(t_glob + 1.0, float(win))
    return e[POOL_HALO:] / cnt - xb_g


def gating_fwd(h0, lg, lb, ws, bsT, wp, sc, *, name, carry=()):
    T = h0.shape[0]
    DA = lg.shape[-1]
    DB = sc.shape[-1]
    HA = DA // A_HEAD
    G = len(B_WINDOWS)
    CG = DB // G
    tm = _tile(T, 512, A_CHUNK)
    n_c = tm // A_CHUNK

    def body(h_ref, halo_ref, lg_ref, lb_ref, ws_ref, bsT_ref, wp_ref, sc_ref, cat_ref):
        i = pl.program_id(0)
        hu = h_ref[:, 0:DA].astype(F32)
        hv = h_ref[:, DA:2 * DA].astype(F32)
        xb = h_ref[:, 2 * DA:].astype(F32)
        u = _gelu(hu)
        vn, _, _ = _ln_fwd(_gelu(hv), lg_ref[...], lb_ref[...])
        vnb = vn.astype(BF16)
        rr = lax.broadcasted_iota(jnp.int32, (A_CHUNK, A_CHUNK), 0)
        cc = lax.broadcasted_iota(jnp.int32, (A_CHUNK, A_CHUNK), 1)
        for hh in range(HA):
            wt = jnp.where(rr >= cc, ws_ref[hh], 0.0).astype(BF16)
            cs = slice(hh * A_HEAD, (hh + 1) * A_HEAD)
            for n in range(n_c):
                rs = slice(n * A_CHUNK, (n + 1) * A_CHUNK)
                s = _dot(wt, vnb[rs, cs]) + bsT_ref[:, hh:hh + 1]
                cat_ref[rs, cs] = (u[rs, cs] * s).astype(BF16)
        halo = jnp.where(i > 0, halo_ref[...].astype(F32), 0.0)
        ext = jnp.concatenate([halo, xb], axis=0)
        t_glob = (i * tm + lax.broadcasted_iota(jnp.int32, (tm, 1), 0)).astype(F32)
        for g, win in enumerate(B_WINDOWS):
            gs = slice(g * CG, (g + 1) * CG)
            p = _pool_fwd(ext[:, gs], xb[:, gs], t_glob, win)
            z = _dot(p.astype(BF16), wp_ref[g])
            cat_ref[:, DA + g * CG:DA + (g + 1) * CG] = (z * sc_ref[:, gs]).astype(BF16)

    full = lambda a: pl.BlockSpec(a.shape, lambda i: (0,) * a.ndim)
    hpb = tm // POOL_HALO
    return _call(
        body, name=name, grid=(T // tm,),
        in_specs=[pl.BlockSpec((tm, 2 * DA + DB), lambda i: (i, 0)),
                  pl.BlockSpec((POOL_HALO, DB), lambda i: (jnp.maximum(i * hpb - 1, 0), 2 * DA // DB)),
                  full(lg), full(lb), full(ws), full(bsT), full(wp), full(sc)],
        out_specs=pl.BlockSpec((tm, DA + DB), lambda i: (i, 0)),
        out_shape=jax.ShapeDtypeStruct((T, DA + DB), BF16), carry=carry)(h0, h0, lg, lb, ws, bsT, wp, sc)


def gating_bwd(h0, dcat, lg, lb, ws, wsT, bsT, wp, sc, *, name, carry=()):
    T = h0.shape[0]
    DA = lg.shape[-1]
    DB = sc.shape[-1]
    HA = DA // A_HEAD
    G = len(B_WINDOWS)
    CG = DB // G
    tm = _tile(T, 512, A_CHUNK)
    n_i = T // tm
    n_c = tm // A_CHUNK
    n_ext = tm + POOL_HALO

    def body(h_ref, halo_ref, dc_ref, dhalo_ref, lg_ref, lb_ref, ws_ref, wsT_ref, bsT_ref, wp_ref, sc_ref,
             dh_ref, dws_ref, dbsT_ref, dlg_ref, dlb_ref, dsc_ref, dwp_ref, dvn_sc):
        i = pl.program_id(0)

        @pl.when(i == 0)
        def _():
            for r in (dws_ref, dbsT_ref, dlg_ref, dlb_ref, dsc_ref, dwp_ref):
                r[...] = jnp.zeros_like(r)

        hu = h_ref[:, 0:DA].astype(F32)
        hv = h_ref[:, DA:2 * DA].astype(F32)
        xb = h_ref[:, 2 * DA:].astype(F32)
        u = _gelu(hu)
        gu = _gelu_grad(hu)
        lgv = lg_ref[...]
        vn, vhat, rstd = _ln_fwd(_gelu(hv), lgv, lb_ref[...])
        vnb = vn.astype(BF16)
        rr = lax.broadcasted_iota(jnp.int32, (A_CHUNK, A_CHUNK), 0)
        cc = lax.broadcasted_iota(jnp.int32, (A_CHUNK, A_CHUNK), 1)
        for hh in range(HA):
            wt = jnp.where(rr >= cc, ws_ref[hh], 0.0).astype(BF16)
            wtT = jnp.where(rr <= cc, wsT_ref[hh], 0.0).astype(BF16)
            cs = slice(hh * A_HEAD, (hh + 1) * A_HEAD)
            dws = jnp.zeros((A_CHUNK, A_CHUNK), F32)
            dbs = jnp.zeros((A_CHUNK, 1), F32)
            for n in range(n_c):
                rs = slice(n * A_CHUNK, (n + 1) * A_CHUNK)
                vb = vnb[rs, cs]
                s = _dot(wt, vb) + bsT_ref[:, hh:hh + 1]
                dya = dc_ref[rs, cs].astype(F32)
                ds = dya * u[rs, cs]
                dh_ref[rs, cs] = (dya * s * gu[rs, cs]).astype(BF16)
                dsb = ds.astype(BF16)
                dbs = dbs + jnp.sum(ds, axis=1, keepdims=True)
                dws = dws + _dot_nt(dsb, vb)
                dvn_sc[rs, cs] = _dot(wtT, dsb)
            dws_ref[hh] += jnp.where(rr >= cc, dws, 0.0)
            dbsT_ref[:, hh:hh + 1] += dbs
        dvg, dlg, dlb = _ln_bwd(dvn_sc[...], vhat, rstd, lgv)
        dlg_ref[...] += dlg
        dlb_ref[...] += dlb
        dh_ref[:, DA:2 * DA] = (dvg * _gelu_grad(hv)).astype(BF16)

        halo = jnp.where(i > 0, halo_ref[...].astype(F32), 0.0)
        ext = jnp.concatenate([halo, xb], axis=0)
        t_glob = (i * tm + lax.broadcasted_iota(jnp.int32, (tm, 1), 0)).astype(F32)
        t_ext = (i * tm + lax.broadcasted_iota(jnp.int32, (n_ext, 1), 0)).astype(F32)
        dyb = dc_ref[:, DA:].astype(F32)
        dhalo = jnp.where(i < n_i - 1, dhalo_ref[...].astype(F32), 0.0)
        dyb_ext = jnp.concatenate([dyb, dhalo], axis=0)
        for g, win in enumerate(B_WINDOWS):
            gs = slice(g * CG, (g + 1) * CG)
            pb = _pool_fwd(ext[:, gs], xb[:, gs], t_glob, win).astype(BF16)
            wpg = wp_ref[g]
            z = _dot(pb, wpg)
            dsc_ref[:, gs] += jnp.sum(dyb[:, gs] * z, axis=0, keepdims=True)
            dzb = (dyb_ext[:, gs] * sc_ref[:, gs]).astype(BF16)
            dwp_ref[g] += _dot_tn(pb, dzb[:tm])
            dp = _dot_nt(dzb, wpg)
            e = dp / jnp.minimum(t_ext + 1.0, float(win))
            sft = 1
            while sft < win:
                e = e + pltpu.roll(e, n_ext - sft, 0)
                sft *= 2
            dh_ref[:, 2 * DA + g * CG:2 * DA + (g + 1) * CG] = (e[:tm] - dp[:tm]).astype(BF16)

    full = lambda a: pl.BlockSpec(a.shape, lambda i: (0,) * a.ndim)
    hpb = tm // POOL_HALO
    n_hb = T // POOL_HALO
    outs = [jax.ShapeDtypeStruct((T, 2 * DA + DB), BF16), jax.ShapeDtypeStruct(ws.shape, F32),
            jax.ShapeDtypeStruct(bsT.shape, F32), jax.ShapeDtypeStruct(lg.shape, F32),
            jax.ShapeDtypeStruct(lb.shape, F32), jax.ShapeDtypeStruct(sc.shape, F32),
            jax.ShapeDtypeStruct(wp.shape, F32)]
    return _call(
        body, name=name, grid=(n_i,),
        in_specs=[pl.BlockSpec((tm, 2 * DA + DB), lambda i: (i, 0)),
                  pl.BlockSpec((POOL_HALO, DB), lambda i: (jnp.maximum(i * hpb - 1, 0), 2 * DA // DB)),
                  pl.BlockSpec((tm, DA + DB), lambda i: (i, 0)),
                  pl.BlockSpec((POOL_HALO, DB), lambda i: (jnp.minimum((i + 1) * hpb, n_hb - 1), DA // DB)),
                  full(lg), full(lb), full(ws), full(wsT), full(bsT), full(wp), full(sc)],
        out_specs=[pl.BlockSpec((tm, 2 * DA + DB), lambda i: (i, 0))] + [full(o) for o in outs[1:]],
        out_shape=outs,
        scratch=[pltpu.VMEM((tm, DA), F32)], carry=carry)(h0, h0, dcat, dcat, lg, lb, ws, wsT, bsT, wp, sc)


HP = 2 * C_HEAD


def _hgrn_gates(h_ref, rows, lbv, tri):
    pre = []
    for r in rows:
        blk = h_ref[r, :].astype(F32)
        q = blk[:, 0:HP]
        sg = _sigmoid(blk[:, HP:2 * HP])
        f = lbv + (1.0 - lbv) * sg
        pre.append(dict(q=q, sq=_sigmoid(q), sg=sg, f=f, k=1.0 - f, lf=jnp.log(f), v=blk[:, 2 * HP:3 * HP],
                        gg=blk[:, 3 * HP:4 * HP]))
    bcums = [_exact_tri_dot(tri, p["lf"]) for p in pre]
    out = []
    for p, bcum in zip(pre, bcums):
        blast = bcum[C_CHUNK - 1:C_CHUNK]
        e_in = jnp.exp(bcum)
        e_out = jnp.exp(-bcum)
        e_end = jnp.exp(blast - bcum)
        out.append(dict(p, e_in=e_in, e_out=e_out, e_end=e_end, qd=p["q"] * p["sq"] * e_in, kd=p["k"] * e_out,
                        ke=p["k"] * e_end, dec=jnp.exp(blast)))
    return out


def _lower_bound(lbp_ref):
    p0, p1 = lbp_ref[0:1], lbp_ref[1:2]
    mx = jnp.maximum(p0, p1)
    e0, e1 = jnp.exp(p0 - mx), jnp.exp(p1 - mx)
    return e1 / (e0 + e1)


def hgrn_fwd(h1p, lbp, gn, *, name, carry=()):
    T = h1p.shape[0]
    DC = gn.shape[-1]
    n_p = DC // HP
    tt = _tile(T, 512, C_CHUNK)
    n_c = tt // C_CHUNK

    def body(h_ref, lbp_ref, gn_ref, y_ref, o_ref, sp_ref, st):
        i = pl.program_id(1)

        @pl.when(i == 0)
        def _():
            st[...] = jnp.zeros_like(st)

        lbv = _lower_bound(lbp_ref)
        gnv = gn_ref[...]
        rr = lax.broadcasted_iota(jnp.int32, (C_CHUNK, C_CHUNK), 0)
        cc = lax.broadcasted_iota(jnp.int32, (C_CHUNK, C_CHUNK), 1)
        causal = rr >= cc
        tri = jnp.where(causal, 1.0, 0.0).astype(BF16)

        heads = [slice(hd * C_HEAD, (hd + 1) * C_HEAD) for hd in range(2)]
        rows = [slice(n * C_CHUNK, (n + 1) * C_CHUNK) for n in range(n_c)]
        nh = [(n, hd) for n in range(n_c) for hd in range(2)]
        gates = _hgrn_gates(h_ref, rows, lbv, tri)
        b16 = lambda key: {(n, hd): gates[n][key][:, heads[hd]].astype(BF16) for n, hd in nh}
        qd, kd, ke, vb = b16("qd"), b16("kd"), b16("ke"), b16("v")
        att = {i: jnp.where(causal, _dot_nt(qd[i], kd[i]), 0.0).astype(BF16) for i in nh}
        intra = {i: _dot(att[i], vb[i]) for i in nh}
        upd = {i: _dot_tn(vb[i], ke[i]) for i in nh}
        s = [st[0], st[1]]
        entering = {}
        for n, hd in nh:
            entering[n, hd] = s[hd]
            sp_ref[hd, n] = s[hd]
            s[hd] = gates[n]["dec"][:, heads[hd]] * s[hd] + upd[n, hd]
        st[0], st[1] = s
        o = {i: intra[i] + _dot_nt(qd[i], entering[i].astype(BF16)) for i in nh}
        for n in range(n_c):
            os_ = [o[n, 0], o[n, 1]]
            o_ref[rows[n], :] = jnp.concatenate(os_, axis=1).astype(BF16)
            ys = [oh * lax.rsqrt(jnp.mean(oh * oh, axis=-1, keepdims=True) + LN_EPS) for oh in os_]
            y_ref[rows[n], :] = (jnp.concatenate(ys, axis=1) * gnv * _sigmoid(gates[n]["gg"])).astype(BF16)

    return _call(
        body, name=name, grid=(n_p, T // tt),
        in_specs=[pl.BlockSpec((tt, 4 * HP), lambda p, i: (i, p)), pl.BlockSpec((2, HP), lambda p, i: (0, p)),
                  pl.BlockSpec((1, HP), lambda p, i: (0, p))],
        out_specs=[pl.BlockSpec((tt, HP), lambda p, i: (i, p)), pl.BlockSpec((tt, HP), lambda p, i: (i, p)),
                   pl.BlockSpec((2, n_c, C_HEAD, C_HEAD), lambda p, i: (p, i, 0, 0))],
        out_shape=[jax.ShapeDtypeStruct((T, DC), BF16), jax.ShapeDtypeStruct((T, DC), BF16),
                   jax.ShapeDtypeStruct((2 * n_p, T // C_CHUNK, C_HEAD, C_HEAD), F32)],
        scratch=[pltpu.VMEM((2, C_HEAD, C_HEAD), F32)], carry=carry)(h1p, lbp, gn)


def hgrn_bwd(h1p, o_saved, dy, sp, lbp, gn, *, name, carry=()):
    T = h1p.shape[0]
    DC = gn.shape[-1]
    n_p = DC // HP
    tt = _tile(T, 512, C_CHUNK)
    n_i = T // tt
    n_c = tt // C_CHUNK

    def body(h_ref, o_ref, dy_ref, sp_ref, lbp_ref, gn_ref, dh_ref, dgn_ref, dlbp_ref, dst, dlb_acc):
        ip = pl.program_id(1)

        @pl.when(ip == 0)
        def _():
            dst[...] = jnp.zeros_like(dst)
            dlb_acc[...] = jnp.zeros_like(dlb_acc)
            dgn_ref[...] = jnp.zeros_like(dgn_ref)

        lbv = _lower_bound(lbp_ref)
        gnv = gn_ref[...]
        rr = lax.broadcasted_iota(jnp.int32, (C_CHUNK, C_CHUNK), 0)
        cc = lax.broadcasted_iota(jnp.int32, (C_CHUNK, C_CHUNK), 1)
        causal = rr >= cc
        tri = jnp.where(causal, 1.0, 0.0).astype(BF16)
        tri_t = jnp.where(rr <= cc, 1.0, 0.0).astype(BF16)
        last_row = lax.broadcasted_iota(jnp.int32, (C_CHUNK, 1), 0) == C_CHUNK - 1

        heads = [slice(hd * C_HEAD, (hd + 1) * C_HEAD) for hd in range(2)]
        rows = [slice(n * C_CHUNK, (n + 1) * C_CHUNK) for n in range(n_c)]
        cat = lambda parts: jnp.concatenate(parts, axis=1)
        nh = [(n, hd) for n in range(n_c) for hd in range(2)]
        pair = lambda d, n: cat([d[n, 0], d[n, 1]])
        gates = _hgrn_gates(h_ref, rows, lbv, tri)
        dgn = jnp.zeros((1, HP), F32)
        dgg, dob = [], {}
        for n in range(n_c):
            o = o_ref[rows[n], :].astype(F32)
            dyv = dy_ref[rows[n], :].astype(F32)
            sgg = _sigmoid(gates[n]["gg"])
            rrs = [lax.rsqrt(jnp.mean(o[:, cs] * o[:, cs], axis=-1, keepdims=True) + LN_EPS) for cs in heads]
            ohat = cat([o[:, cs] * r for cs, r in zip(heads, rrs)])
            dyn = dyv * sgg
            dgg.append(dyv * ohat * gnv * sgg * (1.0 - sgg))
            dgn = dgn + jnp.sum(dyn * ohat, axis=0, keepdims=True)
            dxh = dyn * gnv
            for hd, cs in enumerate(heads):
                dxh_h, oh_h = dxh[:, cs], ohat[:, cs]
                dob[n, hd] = (rrs[hd] * (dxh_h - oh_h * jnp.mean(dxh_h * oh_h, axis=-1, keepdims=True))).astype(BF16)
        dgn_ref[...] += dgn
        b16 = lambda key: {(n, hd): gates[n][key][:, heads[hd]].astype(BF16) for n, hd in nh}
        qd, kd, ke, vb = b16("qd"), b16("kd"), b16("ke"), b16("v")
        s_in = {(n, hd): sp_ref[hd, n] for n, hd in nh}
        att = {i: jnp.where(causal, _dot_nt(qd[i], kd[i]), 0.0).astype(BF16) for i in nh}
        datt = {i: jnp.where(causal, _dot_nt(dob[i], vb[i]), 0.0).astype(BF16) for i in nh}
        grow = {i: _dot_tn(dob[i], qd[i]) for i in nh}
        dv_i = {i: _dot_tn(att[i], dob[i]) for i in nh}
        dqd = {i: _dot(datt[i], kd[i]) + _dot(dob[i], s_in[i].astype(BF16)) for i in nh}
        dkd = {i: _dot_tn(datt[i], qd[i]) for i in nh}
        ds = [dst[0], dst[1]]
        leaving = {}
        for n in reversed(range(n_c)):
            for hd, cs in enumerate(heads):
                leaving[n, hd] = ds[hd]
                ds[hd] = gates[n]["dec"][:, cs] * ds[hd] + grow[n, hd]
        dst[0], dst[1] = ds
        dsb = {i: leaving[i].astype(BF16) for i in nh}
        dv = {i: dv_i[i] + _dot_nt(ke[i], dsb[i]) for i in nh}
        dke = {i: _dot(vb[i], dsb[i]) for i in nh}
        ddec = {i: jnp.sum(leaving[i] * s_in[i], axis=0, keepdims=True) for i in nh}
        mid = []
        for n in range(n_c):
            a = gates[n]
            dqd_n, dkd_n, dke_n = pair(dqd, n), pair(dkd, n), pair(dke, n)
            kek = dke_n * a["ke"]
            dblast = jnp.sum(kek, axis=0, keepdims=True) + pair(ddec, n) * a["dec"]
            dbcum = dqd_n * a["qd"] - dkd_n * a["kd"] - kek + jnp.where(last_row, dblast, 0.0)
            mid.append((dqd_n * a["e_in"], dkd_n * a["e_out"] + dke_n * a["e_end"], dbcum))
        dlf = [_exact_tri_dot(tri_t, m[2]) for m in mid]
        dlb = jnp.zeros((1, HP), F32)
        for n in range(n_c):
            a = gates[n]
            dqs, dk, _ = mid[n]
            df = dlf[n] / a["f"] - dk
            dlb = dlb + jnp.sum(df * (1.0 - a["sg"]), axis=0, keepdims=True)
            dfl = df * (1.0 - lbv) * a["sg"] * (1.0 - a["sg"])
            dq = dqs * a["sq"] * (1.0 + a["q"] * (1.0 - a["sq"]))
            dh_ref[rows[n], :] = cat([dq, dfl, pair(dv, n), dgg[n]]).astype(BF16)
        dlb_acc[...] += dlb

        @pl.when(ip == n_i - 1)
        def _():
            d1 = dlb_acc[...] * lbv * (1.0 - lbv)
            dlbp_ref[...] = jnp.concatenate([-d1, d1], axis=0)

    rev = lambda ip: n_i - 1 - ip
    return _call(
        body, name=name, grid=(n_p, n_i),
        in_specs=[pl.BlockSpec((tt, 4 * HP), lambda p, ip: (rev(ip), p)),
                  pl.BlockSpec((tt, HP), lambda p, ip: (rev(ip), p)),
                  pl.BlockSpec((tt, HP), lambda p, ip: (rev(ip), p)),
                  pl.BlockSpec((2, n_c, C_HEAD, C_HEAD), lambda p, ip: (p, rev(ip), 0, 0)),
                  pl.BlockSpec((2, HP), lambda p, ip: (0, p)), pl.BlockSpec((1, HP), lambda p, ip: (0, p))],
        out_specs=[pl.BlockSpec((tt, 4 * HP), lambda p, ip: (rev(ip), p)),
                   pl.BlockSpec((1, HP), lambda p, ip: (0, p)), pl.BlockSpec((2, HP), lambda p, ip: (0, p))],
        out_shape=[jax.ShapeDtypeStruct(h1p.shape, BF16), jax.ShapeDtypeStruct((1, DC), F32),
                   jax.ShapeDtypeStruct((2, DC), F32)],
        scratch=[pltpu.VMEM((2, C_HEAD, C_HEAD), F32), pltpu.VMEM((1, HP), F32)], carry=carry)(h1p, o_saved, dy, sp, lbp, gn)


def loss_bwd(xh, rstd, g, b, target, *, name):
    T, N = xh.shape
    tm = _tile(T, 512, 8)

    def body(xh_ref, rs_ref, g_ref, b_ref, t_ref, ls_ref, dr_ref, drb_ref, dg_ref, db_ref):
        i = pl.program_id(0)

        @pl.when(i == 0)
        def _():
            for r in (ls_ref, dg_ref, db_ref):
                r[...] = jnp.zeros_like(r)

        xhv, gv = xh_ref[...], g_ref[...]
        e = xhv * gv + b_ref[...] - t_ref[...]
        ls_ref[...] += 0.5 * jnp.sum(jnp.mean(e * e, axis=-1, keepdims=True), axis=0, keepdims=True)
        dr, dg, db = _ln_bwd(e / N, xhv, rs_ref[...], gv)
        dr_ref[...] = dr
        drb_ref[...] = dr.astype(BF16)
        dg_ref[...] += dg
        db_ref[...] += db

    row = pl.BlockSpec((tm, N), lambda i: (i, 0))
    vec = pl.BlockSpec((1, N), lambda i: (0, 0))
    return _call(body, name=name, grid=(T // tm,),
                 in_specs=[row, pl.BlockSpec((tm, 1), lambda i: (i, 0)), vec, vec, row],
                 out_specs=[pl.BlockSpec((1, LANES), lambda i: (0, 0)), row, row, vec, vec],
                 out_shape=[jax.ShapeDtypeStruct((1, LANES), F32), jax.ShapeDtypeStruct((T, N), F32),
                            jax.ShapeDtypeStruct((T, N), BF16), jax.ShapeDtypeStruct((1, N), F32),
                            jax.ShapeDtypeStruct((1, N), F32)])(xh, rstd, g, b, target)


def _mesh_pos():
    x, y, c = lax.axis_index("x"), lax.axis_index("y"), lax.axis_index("c")
    chips = [(1 - x, y), (x, 1 - y), (1 - x, 1 - y)]
    return x, y, c, chips


def _remote(src, dst, send, recv, j, dev):
    return pltpu.make_async_remote_copy(src_ref=src, dst_ref=dst, send_sem=send.at[j], recv_sem=recv.at[j],
                                        device_id=dev, device_id_type=MESH)


def mesh_ids():
    x, y, c, chips = _mesh_pos()
    return jnp.stack([c] + [2 * cx + cy for cx, cy in chips] + [2 * x + y]).astype(jnp.int32)


def _sibling():
    return (lax.axis_index("x"), lax.axis_index("y"), 1 - lax.axis_index("c"))


SWAP_PARTS = 2


def _swap_with_sibling(src, recv, send_sem, recv_sem, step):
    slot = step % 2
    br = src.shape[0]
    n = SWAP_PARTS if br % (16 * SWAP_PARTS) == 0 else 1
    parts = []
    for k in range(n):
        rows = pl.ds(k * (br // n), br // n)
        cp = pltpu.make_async_remote_copy(
            src_ref=src.at[rows], dst_ref=recv.at[slot, rows], send_sem=send_sem.at[slot * SWAP_PARTS + k],
            recv_sem=recv_sem.at[slot * SWAP_PARTS + k], device_id=_sibling(), device_id_type=MESH)
        cp.start()
        parts.append((cp, slot, rows))
    return parts


def _swap_scratch(br, C, dtype):
    return [pltpu.VMEM((2, br, C), dtype), pltpu.SemaphoreType.DMA((2 * SWAP_PARTS,)),
            pltpu.SemaphoreType.DMA((2 * SWAP_PARTS,))]


def _swap_rows(Rh, C, itemsize):
    return _tile(Rh, max(16, (3 << 20) // (C * itemsize)), 16)


def cast_to_slot(a3, l, me1, name):
    _, R, C = a3.shape
    br = _tile(R, max(8, (1 << 20) // C), 16)

    def body(me_ref, a_ref, o_ref):
        o_ref[...] = a_ref[...].astype(BF16)

    return _call(body, name=name, grid=(R // br,), prefetch=1,
                 in_specs=[pl.BlockSpec((None, br, C), lambda r, me: (l, r, 0))],
                 out_specs=pl.BlockSpec((None, None, br, C), lambda r, me: (0, me[0], r, 0)),
                 out_shape=jax.ShapeDtypeStruct((1, 4, R, C), BF16))(me1, a3)


def all_gather_chips(big, small, *, name):
    nb, ns = len(big), len(small)
    layers = [(t, l) for t in range(nb) for l in range(big[t].shape[0])]
    n_big = 3 * len(layers)
    n_rem = n_big + 3 * ns

    def body(*refs):
        small_in = refs[nb:nb + ns]
        bufs, small_out = refs[nb + ns:2 * nb + ns], refs[2 * nb + ns:2 * (nb + ns)]
        send, recv, loc = refs[2 * (nb + ns):]
        x, y, c, chips = _mesh_pos()
        me = 2 * x + y
        ids = [2 * cx + cy for cx, cy in chips]
        started, sends = [], []
        for t in range(ns):
            cp = pltpu.make_async_copy(small_in[t], small_out[t].at[me], loc.at[t])
            cp.start()
            started.append(cp)
        for q, (t, l) in enumerate(layers):
            for k, chip in enumerate(chips):
                blk = bufs[t].at[l, me, c]
                cp = _remote(blk, blk, send, recv, 3 * q + k, (*chip, c))
                cp.start()
                sends.append(cp)
        for t in range(ns):
            for k, chip in enumerate(chips):
                cp = _remote(small_in[t], small_out[t].at[me], send, recv, n_big + 3 * t + k, (*chip, c))
                cp.start()
                sends.append(cp)
        for q, (t, l) in enumerate(layers):
            for k in range(3):
                blk = bufs[t].at[l, ids[k], c]
                _remote(blk, blk, send, recv, 3 * q + k, (x, y, c)).wait_recv()
        for t in range(ns):
            for k in range(3):
                blk = small_out[t].at[ids[k]]
                _remote(blk, blk, send, recv, n_big + 3 * t + k, (x, y, c)).wait_recv()
        for cp in sends:
            cp.wait_send()
        for cp in started:
            cp.wait()

    out_shape = [jax.ShapeDtypeStruct(a.shape, a.dtype) for a in big]
    out_shape += [jax.ShapeDtypeStruct((4,) + a.shape, a.dtype) for a in small]
    return _call(body, name=name, in_specs=[ANY] * (nb + ns), out_specs=[ANY] * (nb + ns), out_shape=out_shape,
                 aliases={t: t for t in range(nb)},
                 scratch=[pltpu.SemaphoreType.DMA((n_rem,)), pltpu.SemaphoreType.DMA((n_rem,)),
                          pltpu.SemaphoreType.DMA((max(ns, 1),))])(*big, *small)


def all_gather_pair(buf, ids, *, name):
    L, _, _, Rh, C = buf.shape
    br = _swap_rows(Rh, C, 2)
    n_r = Rh // br

    def body(ids_ref, in_ref, o_ref, recv, ssem, rsem):
        step = (pl.program_id(0) * 3 + pl.program_id(1)) * n_r + pl.program_id(2)
        parts = _swap_with_sibling(in_ref, recv, ssem, rsem, step)
        for cp, slot, rows in parts:
            cp.wait_recv()
            o_ref[rows, :] = recv[slot, rows, :]
        for cp, _, _ in parts:
            cp.wait_send()

    at = lambda l, s, h, r: (((l * 4 + s) * 2 + h) * n_r + r, 0)
    out = _call(body, name=name, grid=(L, 3, n_r), prefetch=1,
                in_specs=[pl.BlockSpec((br, C), lambda l, k, r, ids: at(l, ids[1 + k], ids[0], r))],
                out_specs=pl.BlockSpec((br, C), lambda l, k, r, ids: at(l, ids[1 + k], 1 - ids[0], r)),
                out_shape=jax.ShapeDtypeStruct((L * 8 * Rh, C), buf.dtype), aliases={1: 0},
                scratch=_swap_scratch(br, C, BF16))(ids, buf.reshape(L * 8 * Rh, C))
    return out.reshape(buf.shape)


def rs_pair_add(grad, ids, *, name):
    _, _, Rh, C = grad.shape
    br = _swap_rows(Rh, C, 2)
    n_r = Rh // br

    def body(ids_ref, send_ref, keep_ref, pb_ref, own_ref, recv, ssem, rsem):
        ph = pl.program_id(0)
        parts = _swap_with_sibling(send_ref, recv, ssem, rsem, ph * n_r + pl.program_id(1))
        for cp, slot, rows in parts:
            cp.wait_recv()
            s = keep_ref[rows, :].astype(F32) + recv[slot, rows, :].astype(F32)

            @pl.when(ph == 0)
            def _():
                own_ref[rows, :] = s

            @pl.when(ph > 0)
            def _():
                pb_ref[rows, :] = s.astype(BF16)

        for cp, _, _ in parts:
            cp.wait_send()

    rel = lambda ph: (ph + 3) % 4
    at = lambda s, h, r: ((s * 2 + h) * n_r + r, 0)
    grad = grad.reshape(8 * Rh, C)
    return _call(
        body, name=name, grid=(4, n_r), prefetch=1,
        in_specs=[pl.BlockSpec((br, C), lambda ph, r, ids: at(ids[1 + rel(ph)], 1 - ids[0], r)),
                  pl.BlockSpec((br, C), lambda ph, r, ids: at(ids[1 + rel(ph)], ids[0], r))],
        out_specs=[pl.BlockSpec((None, br, C), lambda ph, r, ids: (jnp.maximum(ph - 1, 0), jnp.where(ph == 0, 0, r), 0)),
                   pl.BlockSpec((br, C), lambda ph, r, ids: (jnp.where(ph == 0, r, n_r - 1), 0))],
        out_shape=[jax.ShapeDtypeStruct((3, Rh, C), BF16), jax.ShapeDtypeStruct((Rh, C), F32)],
        scratch=_swap_scratch(br, C, BF16))(ids, grad, grad)


def rs_finish(owns, gots, *, name):
    L = len(owns)
    Rh, C = owns[0].shape
    br = _swap_rows(Rh, C, 4)
    n_r = Rh // br

    def body(*refs):
        own_refs, got_refs, o_ref = refs[:L], refs[L:2 * L], refs[2 * L]
        recv, ssem, rsem = refs[2 * L + 1:]
        l = pl.program_id(0)
        c = lax.axis_index("c")
        for ll in range(L):
            @pl.when(l == ll)
            def _():
                s = own_refs[ll][...]
                for k in range(3):
                    s = s + got_refs[ll][k].astype(F32)
                o_ref[c] = s

        parts = _swap_with_sibling(o_ref.at[c], recv, ssem, rsem, l * n_r + pl.program_id(1))
        for cp, slot, rows in parts:
            cp.wait_recv()
            o_ref[1 - c, rows, :] = recv[slot, rows, :]
        for cp, _, _ in parts:
            cp.wait_send()

    def at_layer(ll):
        return lambda l, r: jnp.where(l == ll, r, jnp.where(l < ll, 0, n_r - 1))

    in_specs = [pl.BlockSpec((br, C), lambda l, r, ll=ll: (at_layer(ll)(l, r), 0)) for ll in range(L)]
    in_specs += [pl.BlockSpec((3, br, C), lambda l, r, ll=ll: (0, at_layer(ll)(l, r), 0)) for ll in range(L)]
    out = _call(body, name=name, grid=(L, n_r), in_specs=in_specs,
                out_specs=pl.BlockSpec((2, br, C), lambda l, r: (l, r, 0)),
                out_shape=jax.ShapeDtypeStruct((L * 2, Rh, C), F32),
                scratch=_swap_scratch(br, C, F32))(*owns, *gots)
    return out.reshape(L, 2, Rh, C)


def all_reduce_small(buf, *, name):
    rows = buf.shape[0]

    def body(x_ref, o_ref, rbuf, send, recv):
        x, y, c, _ = _mesh_pos()
        o_ref[...] = x_ref[...]
        for k, dev in enumerate(((x, y, 1 - c), (1 - x, y, c), (x, 1 - y, c))):
            cp = _remote(o_ref, rbuf.at[k], send, recv, k, dev)
            cp.start()
            cp.wait()
            o_ref[...] = o_ref[...] + rbuf[k]

    return _call(body, name=name, in_specs=[VMEM_SPEC], out_specs=VMEM_SPEC,
                 out_shape=jax.ShapeDtypeStruct(buf.shape, F32),
                 scratch=[pltpu.VMEM((3, rows, LANES), F32), pltpu.SemaphoreType.DMA((3,)),
                          pltpu.SemaphoreType.DMA((3,))])(buf)


def _pack(arrs):
    parts = []
    for a in arrs:
        f = a.reshape(-1).astype(F32)
        parts.append(jnp.pad(f, (0, (-f.shape[0]) % PACK_ALIGN)))
    return jnp.concatenate(parts).reshape(-1, LANES)


def _unpack(buf, shapes):
    flat = buf.reshape(-1)
    out, off = [], 0
    for s in shapes:
        n = math.prod(s)
        out.append(flat[off:off + n].reshape(s))
        off += n + (-n) % PACK_ALIGN
    return out


_WEIGHTS = ['ev_w_in', 'ev_ln_v_g', 'ev_ln_v_b', 'ev_w_s', 'ev_b_s', 'ev_w_pool', 'ev_pool_scale', 'ev_w_out',
            'od_w_in', 'od_norm_g', 'od_w_out', 'lb_param', 'ffn_w_up', 'ffn_conv_w', 'ffn_conv_b', 'ffn_w_down',
            'ln1_g', 'ln1_b', 'ln2_g', 'ln2_b']
_BIG = ['ev_w_in', 'ev_w_out', 'od_w_in', 'od_w_out', 'ffn_w_up', 'ffn_w_down']
_SMALL = [n for n in _WEIGHTS if n not in _BIG]


def kernel(x, ev_w_in, ev_ln_v_g, ev_ln_v_b, ev_w_s, ev_b_s, ev_w_pool, ev_pool_scale, ev_w_out, od_w_in, od_norm_g, od_w_out, lb_param, ffn_w_up, ffn_conv_w, ffn_conv_b, ffn_w_down, ln1_g, ln1_b, ln2_g, ln2_b, loss_target, m_ev_w_in, m_ev_ln_v_g, m_ev_ln_v_b, m_ev_w_s, m_ev_b_s, m_ev_w_pool, m_ev_pool_scale, m_ev_w_out, m_od_w_in, m_od_norm_g, m_od_w_out, m_lb_param, m_ffn_w_up, m_ffn_conv_w, m_ffn_conv_b, m_ffn_w_down, m_ln1_g, m_ln1_b, m_ln2_g, m_ln2_b, v_ev_w_in, v_ev_ln_v_g, v_ev_ln_v_b, v_ev_w_s, v_ev_b_s, v_ev_w_pool, v_ev_pool_scale, v_ev_w_out, v_od_w_in, v_od_norm_g, v_od_w_out, v_lb_param, v_ffn_w_up, v_ffn_conv_w, v_ffn_conv_b, v_ffn_w_down, v_ln1_g, v_ln1_b, v_ln2_g, v_ln2_b):
    given = dict(locals())
    w = {n: given[n] for n in _WEIGHTS}
    mom = {n: given["m_" + n] for n in _WEIGHTS}
    vel = {n: given["v_" + n] for n in _WEIGHTS}
    x2d = x[0]
    tgt = loss_target[0]
    T, D = x2d.shape
    DA = ev_ln_v_g.shape[-1]
    DB = ev_pool_scale.shape[-1]
    HA = ev_w_s.shape[1]
    G = len(B_WINDOWS)
    CG = DB // G
    DC = 4 * od_norm_g.shape[-1]
    F = ffn_conv_b.shape[-1] // 2
    chip = 2 * lax.axis_index("x") + lax.axis_index("y")

    ids = mesh_ids()
    halves = lambda a: a.reshape(1, 4, 2, a.shape[2] // 2, a.shape[3])
    slot = {(n, l): halves(cast_to_slot(w[n], l, ids[4:5], f"cast_{n}{l}"))
            for n in _BIG for l in range(w[n].shape[0])}

    def riding(*keys):
        return [IciCopy("gather", slot[k]) for k in keys]

    def pair(buf, key):
        g = all_gather_pair(buf, ids, name=f"all_gather_pair_{key[0]}{key[1]}")
        return g.reshape(1, 4, g.shape[3] * 2, g.shape[4])

    early = [('ev_w_in', 0), ('ev_w_out', 0)]
    gathered = all_gather_chips([slot[k] for k in early], [ev_w_pool[0], ffn_conv_w, od_norm_g],
                                name="all_gather_chips")
    wpool_full = gathered[2].transpose(1, 0, 2, 3).reshape(G, CG, CG)
    cw_full = gathered[3].transpose(1, 2, 0, 3).reshape(DEPTH, 3, 2 * F)
    gn_full = gathered[4].reshape(1, DC)
    win0 = pair(gathered[0], early[0])[0]
    wout0 = pair(gathered[1], early[1]).reshape(DA + DB, D)
    wup, wdn = {}, {}
    rh_up = D // 2
    cut1 = (rh_up * 35 // 100) // 16 * 16
    cut2 = cut1 + (rh_up * 18 // 100) // 16 * 16
    cb3 = ffn_conv_b.reshape(DEPTH, 1, 2 * F)
    ws = ev_w_s[0]
    wsT = jnp.swapaxes(ws, 1, 2)
    bsT = ev_b_s[0].T
    wpb = wpool_full.astype(BF16)
    ones = jnp.ones((1, D), F32)
    zeros = jnp.zeros((1, D), F32)
    row = lambda a, l: a[l:l + 1]

    Ns0 = win0.shape[-1]
    Nu = ffn_w_up.shape[-1]
    tm_big = _tile(T, 1024, 8)
    tm_ln = _tile(T, 512, 8)
    tk_ln = _tile(D, 512)
    n_p = DC // HP

    def nat_spec(Ns, tnw):
        nps = Ns // tnw
        return pl.BlockSpec((1, D, tnw), lambda i, j: (j // nps, 0, j % nps))

    perm_spec = pl.BlockSpec((4, D, HP), lambda i, j: (0, 0, j))

    xb16 = cast_bf16(x, "cast_x")[0]
    up0 = IciCopy("gather", slot[('ffn_w_up', 0)], rows=(0, cut1))
    h0 = mm_nn(xb16, win0, w_spec=nat_spec(Ns0, Ns0), P=1, tnw=Ns0, tm=tm_big, n_j=4, name="ev_in", carry=[up0])
    up0 = IciCopy("gather", up0.out, rows=(cut1, cut2 - cut1))
    cat = gating_fwd(h0, ev_ln_v_g, ev_ln_v_b, ws, bsT, wpb, ev_pool_scale, name="gating_fwd", carry=[up0])
    up0 = IciCopy("gather", up0.out, rows=(cut2, rh_up - cut2))

    def mix_ln(a, wmat, res, l, name, carry=()):
        K = a.shape[1]
        tk = _tile(K, 2048)
        return mm_ln(a, wmat, *res, row(ln1_g, l), row(ln1_b, l), w_spec=pl.BlockSpec((tk, D), lambda i, k: (k, 0)),
                     K=K, tk=tk, tm=tm_ln, name=name, carry=carry)

    def ffn_down(f, res, l, carry=()):
        tk = F // 4 if (F // 4) % LANES == 0 else _tile(F, 512)
        return mm_ln(f, wdn[l], *res, row(ln2_g, l), row(ln2_b, l),
                     w_spec=pl.BlockSpec((None, tk, D), lambda i, k: (0, k, 0)), K=F, tk=tk, tm=tm_ln,
                     name=f"ffn_down{l}", carry=carry)

    xh1, y1, rs1 = mix_ln(cat, wout0, (x2d, ones, zeros), 0, "ev_out", carry=[up0])
    wup[0] = pair(up0.out, ('ffn_w_up', 0))
    res1 = (xh1, row(ln1_g, 0), row(ln1_b, 0))
    ride = riding(('ffn_w_down', 0), ('od_w_in', 0))
    hf0, hc0, f0 = ffn_up(y1, wup[0], cw_full, cb3, 0, name="ffn_up0", carry=ride)
    wdn[0] = pair(ride[0].out, ('ffn_w_down', 0)).reshape(1, F, D)
    win1 = pair(ride[1].out, ('od_w_in', 0))[0]
    cut1 = (rh_up * 40 // 100) // 16 * 16
    cut2 = 2 * cut1
    up1 = IciCopy("gather", slot[('ffn_w_up', 1)], rows=(0, cut1))
    ride = riding(('od_w_out', 0))
    xh2, y2, rs2 = ffn_down(f0, res1, 0, carry=ride + [up1])
    wout1 = pair(ride[0].out, ('od_w_out', 0)).reshape(DC, D)
    res2 = (xh2, row(ln2_g, 0), row(ln2_b, 0))
    up1 = IciCopy("gather", up1.out, rows=(cut1, cut2 - cut1))
    h1p = mm_nn(y2, win1, w_spec=perm_spec, P=4, tnw=HP, tm=tm_big, n_j=n_p, name="od_in", carry=[up1])
    up1 = IciCopy("gather", up1.out, rows=(cut2, rh_up - cut2))
    yh, o_saved, sp = hgrn_fwd(h1p, lb_param, gn_full, name="hgrn_fwd", carry=[up1])
    wup[1] = pair(up1.out, ('ffn_w_up', 1))
    xh3, y3, rs3 = mix_ln(yh, wout1, res2, 1, "od_out")
    res3 = (xh3, row(ln1_g, 1), row(ln1_b, 1))
    ride = riding(('ffn_w_down', 1))
    hf1, hc1, f1 = ffn_up(y3, wup[1], cw_full, cb3, 1, name="ffn_up1", carry=ride)
    wdn[1] = pair(ride[0].out, ('ffn_w_down', 1)).reshape(1, F, D)
    xh4, y4, rs4 = ffn_down(f1, res3, 1)
    loss_p, dr, drb, dg_ln2_1, db_ln2_1 = loss_bwd(xh4, rs4, row(ln2_g, 1), row(ln2_b, 1), tgt, name="loss_bwd")

    tt = _tile(T, 2048, 16)
    n_t = T // tt
    tnu = Nu // 2 if (Nu // 2) % LANES == 0 else Nu
    upb = Nu // tnu
    tkd = _tile(D, 1024)

    def pair_sum(g4, name):
        pb, own = rs_pair_add(g4.reshape(4, 2, g4.shape[1] // 2, g4.shape[2]), ids, name="rs_pair_add_" + name)
        return IciCopy("scatter", pb), own

    def g_out(a, gb, name, tkk=None):
        K = a.shape[1]
        tkk = tkk or _tile(K, 1024)
        return mm_tn(a, gb, a_spec=pl.BlockSpec((tt, tkk), lambda kb, nb, t: (t, kb)),
                     g_spec=pl.BlockSpec((tt, tkd), lambda kb, nb, t: (t, nb)),
                     o_spec=pl.BlockSpec((1, tkk, tkd), lambda kb, nb, t: (0, kb, nb)), out_shape=(1, K, D),
                     grid=(K // tkk, D // tkd, n_t), acc_shape=(tkk, tkd), P=1, tnw=tkd, name=name)

    def ffn_bwd(l, dr2, dr2b, f, hf, hc, y_in, xh_in, rs_in):
        g_dn = g_out(f, dr2b, f"g_ffn_down{l}", tkk=F // 4 if (F // 4) % LANES == 0 else None)
        rs_dn = pair_sum(g_dn.reshape(4, F // 4, D), f"ffn_down{l}")
        dh, dcw, dcb = ffn_dgate(dr2b, wdn[l], hf, hc, cw_full, l, name=f"ffn_dgate{l}", carry=[rs_dn[0]])
        g_up = mm_tn(y_in, dh, a_spec=pl.BlockSpec((tt, tkd), lambda kb, nb, t: (t, kb)),
                     g_spec=pl.BlockSpec((None, tt, tnu), lambda kb, nb, t: (nb // (2 * upb), t, nb % (2 * upb))),
                     o_spec=pl.BlockSpec((1, tkd, tnu), lambda kb, nb, t: (nb // upb, kb, nb % upb)),
                     out_shape=(4, D, Nu), grid=(D // tkd, 4 * upb, n_t), acc_shape=(tkd, tnu), P=1, tnw=tnu,
                     name=f"g_ffn_up{l}")
        rs_up = pair_sum(g_up, f"ffn_up{l}")
        tku = tnu
        kps = Nu // tku
        out = mm_nt_res(dh, wup[l], dr2, (xh_in, rs_in, row(ln1_g, l)),
                        a_spec=pl.BlockSpec((None, tm_ln, tku), lambda i, k: (k // (2 * kps), i, k % (2 * kps))),
                        w_spec=pl.BlockSpec((None, 1, D, tku), lambda i, k: (0, k // kps, 0, k % kps)),
                        P=1, tnw=tku, n_k=4 * kps, tm=tm_ln, name=f"d_ffn_in{l}", carry=[rs_up[0]])
        return rs_dn, rs_up, dcw, dcb, out

    rs_dn1, rs_up1, dcw1, dcb1, (dr1, dr1b, dg_ln1_1, db_ln1_1) = ffn_bwd(1, dr, drb, f1, hf1, hc1, y3, xh3, rs3)
    rs_wout1 = pair_sum(g_out(yh, dr1b, "g_od_out").reshape(4, DC // 4, D), "od_out")
    dyh = mm_nt_plain(dr1b, wout1, tm=tm_big, tn=_tile(DC, 512), name="d_od_out")
    dh1p, d_gn, d_lbp = hgrn_bwd(h1p, o_saved, dyh, sp, lb_param, gn_full, name="hgrn_bwd", carry=[rs_wout1[0]])
    g_win1 = mm_tn(y2, dh1p, a_spec=pl.BlockSpec((tt, tkd), lambda kb, nb, t: (t, kb)),
                   g_spec=pl.BlockSpec((tt, 4 * HP), lambda kb, nb, t: (t, nb)),
                   o_spec=pl.BlockSpec((4, tkd, HP), lambda kb, nb, t: (0, kb, nb)), out_shape=(4, D, DC),
                   grid=(D // tkd, n_p, n_t), acc_shape=(tkd, 4 * HP), P=4, tnw=HP, name="g_od_in")
    rs_win1 = pair_sum(g_win1, "od_in")
    dr, drb, dg_ln2_0, db_ln2_0 = mm_nt_res(
        dh1p, win1, dr1, (xh2, rs2, row(ln2_g, 0)), a_spec=pl.BlockSpec((tm_ln, 4 * HP), lambda i, k: (i, k)),
        w_spec=pl.BlockSpec((4, D, HP), lambda i, k: (0, 0, k)), P=4, tnw=HP, n_k=n_p, tm=tm_ln, name="d_od_in",
        carry=[rs_win1[0]])
    rs_dn0, rs_up0, dcw0, dcb0, (dr1, dr1b, dg_ln1_0, db_ln1_0) = ffn_bwd(0, dr, drb, f0, hf0, hc0, y1, xh1, rs1)
    rs_wout0 = pair_sum(g_out(cat, dr1b, "g_ev_out").reshape(4, (DA + DB) // 4, D), "ev_out")
    dcat = mm_nt_plain(dr1b, wout0, tm=tm_big, tn=_tile(DA + DB, 512), name="d_ev_out")
    dh0, d_ws, d_bsT, d_lg, d_lb, d_sc, d_wp = gating_bwd(h0, dcat, ev_ln_v_g, ev_ln_v_b, ws, wsT, bsT, wpb,
                                                          ev_pool_scale, name="gating_bwd", carry=[rs_wout0[0]])
    g_win0 = mm_tn(xb16, dh0, a_spec=pl.BlockSpec((tt, tkd), lambda kb, nb, t: (t, kb)),
                   g_spec=pl.BlockSpec((tt, Ns0), lambda kb, nb, t: (t, nb)),
                   o_spec=pl.BlockSpec((1, tkd, Ns0), lambda kb, nb, t: (nb, kb, 0)), out_shape=(4, D, Ns0),
                   grid=(D // tkd, 4, n_t), acc_shape=(tkd, Ns0), P=1, tnw=Ns0, name="g_ev_in")
    rs_win0 = pair_sum(g_win0, "ev_in")
    grad_x = mm_nt_res(dh0, win0, dr1, None, a_spec=pl.BlockSpec((tm_ln, Ns0), lambda i, k: (i, k)),
                       w_spec=pl.BlockSpec((1, D, Ns0), lambda i, k: (k, 0, 0)), P=1, tnw=Ns0, n_k=4, tm=tm_ln,
                       name="d_ev_in", carry=[rs_win0[0]])

    per_weight = [[rs_win0], [rs_wout0], [rs_win1], [rs_wout1], [rs_up0, rs_up1], [rs_dn0, rs_dn1]]
    shared = [rs_finish([own for _, own in m], [cp.out for cp, _ in m], name="rs_finish_" + n)
              for n, m in zip(_BIG, per_weight)]
    big_g = {n: s.reshape(w[n].shape) for n, s in zip(_BIG, shared)}

    small_full = {
        'ev_ln_v_g': d_lg, 'ev_ln_v_b': d_lb, 'ev_w_s': d_ws[None], 'ev_b_s': d_bsT.T[None], 'ev_w_pool': d_wp[None],
        'ev_pool_scale': d_sc, 'od_norm_g': d_gn, 'lb_param': d_lbp,
        'ffn_conv_w': jnp.stack([jnp.concatenate([dcw0[0], dcw0[1]], axis=-1),
                                 jnp.concatenate([dcw1[0], dcw1[1]], axis=-1)]),
        'ffn_conv_b': jnp.stack([jnp.concatenate([dcb0[0, 0], dcb0[1, 0]]), jnp.concatenate([dcb1[0, 0], dcb1[1, 0]])]),
        'ln1_g': jnp.concatenate([dg_ln1_0, dg_ln1_1]), 'ln1_b': jnp.concatenate([db_ln1_0, db_ln1_1]),
        'ln2_g': jnp.concatenate([dg_ln2_0, dg_ln2_1]), 'ln2_b': jnp.concatenate([db_ln2_0, db_ln2_1])}
    packed = _pack([small_full[n] for n in _SMALL] + [loss_p[0, 0:1]])
    reduced = _unpack(all_reduce_small(packed, name="all_reduce_small"),
                      [small_full[n].shape for n in _SMALL] + [(1,)])
    small_g = dict(zip(_SMALL, reduced[:-1]))
    loss = reduced[-1][0]
    small_g['ev_w_pool'] = lax.dynamic_slice_in_dim(small_g['ev_w_pool'], chip * (CG // 4), CG // 4, axis=2)
    small_g['ffn_conv_w'] = lax.dynamic_slice_in_dim(small_g['ffn_conv_w'], chip * (F // 2), F // 2, axis=2)
    small_g['od_norm_g'] = lax.dynamic_slice_in_dim(small_g['od_norm_g'], chip * (DC // 4), DC // 4, axis=1)

    grads, delta, new_m, new_v = {}, {}, {}, {}
    for n in _BIG:
        grads[n], delta[n], new_m[n], new_v[n] = adamw(w[n], big_g[n], mom[n], vel[n], "adamw_" + n)
    ps = [_pack([d[n] for n in _SMALL]) for d in (w, small_g, mom, vel)]
    upd = adamw(*[p[None] for p in ps], "adamw_small")
    shapes = [w[n].shape for n in _SMALL]
    for d, buf in zip((grads, delta, new_m, new_v), upd):
        d.update(zip(_SMALL, _unpack(buf[0], shapes)))

    return (loss, grad_x[None], *[grads[n] for n in _WEIGHTS], *[delta[n] for n in _WEIGHTS],
            *[new_m[n] for n in _WEIGHTS], *[new_v[n] for n in _WEIGHTS])
```

```python
import math

import jax
import jax.numpy as jnp
from jax import lax
from jax.experimental import pallas as pl
from jax.experimental.pallas import tpu as pltpu

F32 = jnp.float32
BF16 = jnp.bfloat16
MESH = pl.DeviceIdType.MESH
ANY = pl.BlockSpec(memory_space=pl.ANY)
VMEM_SPEC = pl.BlockSpec(memory_space=pltpu.VMEM)

DEPTH = 2
ALPHA = (2 * DEPTH) ** 0.25
LN_EPS = 1e-5
A_HEAD = 128
A_CHUNK = 128
B_WINDOWS = (2, 4, 8, 16)
POOL_HALO = 16
C_HEAD = 128
C_CHUNK = 64
CONV_HALO = 8
ADAM_LR = 0.001
ADAM_B1 = 0.9
ADAM_B2 = 0.999
ADAM_EPS = 1e-08
ADAM_WD = 0.01
ADAM_STEP = 10
V7X_VMEM_LIMIT_BYTES = 56 * 1024 * 1024
LANES = 128
PACK_ALIGN = 8 * LANES


class IciCopy:
    def __init__(self, kind, arr, rows=None):
        self.kind, self.arr, self.out = kind, arr, None
        self.rows = rows


def _carried_copies(items, in_refs, out_refs, send, recv):
    x, y, c, chips = _mesh_pos()
    me = 2 * x + y
    sends, lands = [], []
    for q, (it, src, dst) in enumerate(zip(items, in_refs, out_refs)):
        rows = pl.ds(*(it.rows or (0, it.arr.shape[-2])))
        for k, (cx, cy) in enumerate(chips):
            if it.kind == "gather":
                mine, theirs = dst.at[0, me, c, rows], dst.at[0, 2 * cx + cy, c, rows]
                sends.append(_remote(mine, mine, send, recv, 3 * q + k, (cx, cy, c)))
            else:
                theirs = dst.at[k]
                sends.append(_remote(src.at[k], theirs, send, recv, 3 * q + k, (cx, cy, c)))
            lands.append(_remote(theirs, theirs, send, recv, 3 * q + k, (x, y, c)))
    return sends, lands


def _call(body, *, name, out_shape, grid=(), in_specs=None, out_specs=None, scratch=(), prefetch=0, aliases=None,
          carry=()):
    single = not isinstance(out_specs, (list, tuple))
    in_specs = list(in_specs)
    out_specs = [out_specs] if single else list(out_specs)
    out_shape = [out_shape] if single else list(out_shape)
    scratch = list(scratch)
    aliases = dict(aliases or {})
    n_in, n_out, n_sc, n_c = len(in_specs), len(out_specs), len(scratch), len(carry)
    inner = body
    if n_c:
        assert grid, "a carrier needs a grid"
        for q, it in enumerate(carry):
            if it.kind == "gather":
                aliases[prefetch + n_in + q] = n_out + q
        in_specs += [ANY] * n_c
        out_specs += [ANY] * n_c
        out_shape += [jax.ShapeDtypeStruct(it.arr.shape, it.arr.dtype) for it in carry]
        scratch += [pltpu.SemaphoreType.DMA((3 * n_c,)), pltpu.SemaphoreType.DMA((3 * n_c,))]

        def inner(*refs):
            pre, refs = refs[:prefetch], refs[prefetch:]
            ins, c_in = refs[:n_in], refs[n_in:n_in + n_c]
            outs, c_out = refs[n_in + n_c:n_in + n_c + n_out], refs[n_in + n_c + n_out:n_in + 2 * n_c + n_out]
            rest = refs[n_in + 2 * n_c + n_out:]
            first = last = True
            for d, n in enumerate(grid):
                first = jnp.logical_and(first, pl.program_id(d) == 0)
                last = jnp.logical_and(last, pl.program_id(d) == n - 1)

            @pl.when(first)
            def _():
                for cp in _carried_copies(carry, c_in, c_out, rest[-2], rest[-1])[0]:
                    cp.start()

            body(*pre, *ins, *outs, *rest[:n_sc])

            @pl.when(last)
            def _():
                sends, lands = _carried_copies(carry, c_in, c_out, rest[-2], rest[-1])
                for cp in lands:
                    cp.wait_recv()
                for cp in sends:
                    cp.wait_send()

    spec = pltpu.PrefetchScalarGridSpec(num_scalar_prefetch=prefetch, grid=grid, in_specs=in_specs,
                                        out_specs=out_specs, scratch_shapes=scratch)
    fn = pl.pallas_call(inner, name=name, grid_spec=spec, out_shape=out_shape, input_output_aliases=aliases,
                        compiler_params=pltpu.CompilerParams(vmem_limit_bytes=V7X_VMEM_LIMIT_BYTES))

    def run(*args):
        res = fn(*args, *[it.arr for it in carry])
        for it, o in zip(carry, res[n_out:]):
            it.out = o
        return res[0] if single else list(res[:n_out])

    return run


def _tile(n, pref, unit=LANES):
    if n <= pref:
        return n
    t = (pref // unit) * unit
    while t > unit and n % t:
        t -= unit
    assert n % t == 0, (n, pref, unit)
    return t


def _slabs(n, rows=128):
    return [slice(r, min(r + rows, n)) for r in range(0, n, rows)]


def _dot(a, b):
    return jnp.dot(a, b, preferred_element_type=F32)


def _dot_nt(a, b):
    return lax.dot_general(a, b, (((1,), (1,)), ((), ())), preferred_element_type=F32)


def _dot_tn(a, b):
    return lax.dot_general(a, b, (((0,), (0,)), ((), ())), preferred_element_type=F32)


def _sigmoid(x):
    return jax.nn.sigmoid(x)


_GELU_C = math.sqrt(2.0 / math.pi)


def _gelu(x):
    return 0.5 * x * (1.0 + jnp.tanh(_GELU_C * (x + 0.044715 * x * x * x)))


def _gelu_grad(x):
    th = jnp.tanh(_GELU_C * (x + 0.044715 * x * x * x))
    return 0.5 * (1.0 + th) + 0.5 * x * (1.0 - th * th) * _GELU_C * (1.0 + 3.0 * 0.044715 * x * x)


def _ln_fwd(r, g, b):
    mu = jnp.mean(r, axis=-1, keepdims=True)
    xc = r - mu
    var = jnp.mean(xc * xc, axis=-1, keepdims=True)
    rstd = lax.rsqrt(var + LN_EPS)
    xh = xc * rstd
    return xh * g + b, xh, rstd


def _ln_bwd(dy, xh, rstd, g):
    dxh = dy * g
    m1 = jnp.mean(dxh, axis=-1, keepdims=True)
    m2 = jnp.mean(dxh * xh, axis=-1, keepdims=True)
    dr = rstd * (dxh - m1 - xh * m2)
    return dr, jnp.sum(dy * xh, axis=0, keepdims=True), jnp.sum(dy, axis=0, keepdims=True)


def _exact_tri_dot(tri, x):
    hi = x.astype(BF16)
    r1 = x - hi.astype(F32)
    mid = r1.astype(BF16)
    lo = (r1 - mid.astype(F32)).astype(BF16)
    return _dot(tri, hi) + _dot(tri, mid) + _dot(tri, lo)


def cast_bf16(a3, name):
    L, R, C = a3.shape
    br = _tile(R, max(8, (1 << 20) // C), 8)

    def body(a_ref, o_ref):
        o_ref[...] = a_ref[...].astype(BF16)

    return _call(body, name=name, grid=(L, R // br),
                 in_specs=[pl.BlockSpec((None, br, C), lambda l, r: (l, r, 0))],
                 out_specs=pl.BlockSpec((None, br, C), lambda l, r: (l, r, 0)),
                 out_shape=jax.ShapeDtypeStruct((L, R, C), BF16))(a3)


def adamw(w, g, m, v, name):
    L, R, C = w.shape
    br = _tile(R, max(8, (1 << 19) // C), 8)
    c1 = 1.0 - ADAM_B1 ** ADAM_STEP
    c2 = 1.0 - ADAM_B2 ** ADAM_STEP

    def body(w_ref, g_ref, m_ref, v_ref, go_ref, d_ref, nm_ref, nv_ref):
        gg = g_ref[...]
        nm = ADAM_B1 * m_ref[...] + (1.0 - ADAM_B1) * gg
        nv = ADAM_B2 * v_ref[...] + (1.0 - ADAM_B2) * (gg * gg)
        go_ref[...] = gg
        d_ref[...] = -ADAM_LR * ((nm / c1) / (jnp.sqrt(nv / c2) + ADAM_EPS) + ADAM_WD * w_ref[...])
        nm_ref[...] = nm
        nv_ref[...] = nv

    spec = pl.BlockSpec((None, br, C), lambda l, r: (l, r, 0))
    sds = jax.ShapeDtypeStruct((L, R, C), F32)
    return _call(body, name=name, grid=(L, R // br), in_specs=[spec] * 4, out_specs=[spec] * 4,
                 out_shape=[sds] * 4)(w, g, m, v)


def mm_nn(a, w, *, w_spec, P, tnw, tm, n_j, name, carry=()):
    T, K = a.shape
    bw = P * tnw

    def body(a_ref, w_ref, o_ref):
        av = a_ref[...]
        for p in range(P):
            o_ref[:, p * tnw:(p + 1) * tnw] = _dot(av, w_ref[p]).astype(BF16)

    return _call(body, name=name, grid=(T // tm, n_j),
                 in_specs=[pl.BlockSpec((tm, K), lambda i, j: (i, 0)), w_spec],
                 out_specs=pl.BlockSpec((tm, bw), lambda i, j: (i, j)),
                 out_shape=jax.ShapeDtypeStruct((T, n_j * bw), BF16), carry=carry)(a, w)


def mm_ln(a, w, res, rg, rb, g, b, *, w_spec, K, tk, tm, name, carry=()):
    T, N = res.shape
    n_k = K // tk

    def body(a_ref, w_ref, res_ref, rg_ref, rb_ref, g_ref, b_ref, xh_ref, y_ref, rs_ref, acc):
        k = pl.program_id(1)

        @pl.when(k == 0)
        def _():
            acc[...] = jnp.zeros_like(acc)

        acc[...] += _dot(a_ref[...], w_ref[...])

        @pl.when(k == n_k - 1)
        def _():
            for rows in _slabs(tm):
                r = ALPHA * (res_ref[rows, :] * rg_ref[...] + rb_ref[...]) + acc[rows, :]
                y, xh, rstd = _ln_fwd(r, g_ref[...], b_ref[...])
                xh_ref[rows, :] = xh
                y_ref[rows, :] = y.astype(BF16)
                rs_ref[rows, :] = rstd

    row = pl.BlockSpec((tm, N), lambda i, k: (i, 0))
    vec = pl.BlockSpec((1, N), lambda i, k: (0, 0))
    return _call(body, name=name, grid=(T // tm, n_k),
                 in_specs=[pl.BlockSpec((tm, tk), lambda i, k: (i, k)), w_spec, row, vec, vec, vec, vec],
                 out_specs=[row, row, pl.BlockSpec((tm, 1), lambda i, k: (i, 0))],
                 out_shape=[jax.ShapeDtypeStruct((T, N), F32), jax.ShapeDtypeStruct((T, N), BF16),
                            jax.ShapeDtypeStruct((T, 1), F32)],
                 scratch=[pltpu.VMEM((tm, N), F32)], carry=carry)(a, w, res, rg, rb, g, b)


def mm_nt_plain(a, w, *, tm, tn, name):
    T, K = a.shape
    N = w.shape[0]

    def body(a_ref, w_ref, o_ref):
        o_ref[...] = _dot_nt(a_ref[...], w_ref[...]).astype(BF16)

    return _call(body, name=name, grid=(T // tm, N // tn),
                 in_specs=[pl.BlockSpec((tm, K), lambda i, j: (i, 0)), pl.BlockSpec((tn, K), lambda i, j: (j, 0))],
                 out_specs=pl.BlockSpec((tm, tn), lambda i, j: (i, j)),
                 out_shape=jax.ShapeDtypeStruct((T, N), BF16))(a, w)


def mm_nt_res(a, w, res, ln, *, a_spec, w_spec, P, tnw, n_k, tm, name, carry=()):
    T, N = res.shape
    n_i = T // tm

    def body(*refs):
        if ln is None:
            a_ref, w_ref, res_ref, o_ref, acc = refs
        else:
            a_ref, w_ref, res_ref, xh_ref, rs_ref, g_ref, dr_ref, drb_ref, dg_ref, db_ref, acc = refs
        i = pl.program_id(0)
        k = pl.program_id(1)

        @pl.when(k == 0)
        def _():
            acc[...] = jnp.zeros_like(acc)

        wv = w_ref[0] if P == 1 else jnp.concatenate([w_ref[p] for p in range(P)], axis=1)
        acc[...] += _dot_nt(a_ref[...], wv)

        @pl.when(k == n_k - 1)
        def _():
            if ln is not None:
                @pl.when(i == 0)
                def _():
                    dg_ref[...] = jnp.zeros_like(dg_ref)
                    db_ref[...] = jnp.zeros_like(db_ref)

            for rows in _slabs(tm):
                d = ALPHA * res_ref[rows, :] + acc[rows, :]
                if ln is None:
                    o_ref[rows, :] = d
                else:
                    dr, dg, db = _ln_bwd(d, xh_ref[rows, :], rs_ref[rows, :], g_ref[...])
                    dr_ref[rows, :] = dr
                    drb_ref[rows, :] = dr.astype(BF16)
                    dg_ref[...] += dg
                    db_ref[...] += db

    row = pl.BlockSpec((tm, N), lambda i, k: (i, 0))
    vec = pl.BlockSpec((1, N), lambda i, k: (0, 0))
    scratch = [pltpu.VMEM((tm, N), F32)]
    if ln is None:
        return _call(body, name=name, grid=(n_i, n_k), in_specs=[a_spec, w_spec, row], out_specs=row,
                     out_shape=jax.ShapeDtypeStruct((T, N), F32), scratch=scratch, carry=carry)(a, w, res)
    xh, rstd, g = ln
    return _call(body, name=name, grid=(n_i, n_k),
                 in_specs=[a_spec, w_spec, row, row, pl.BlockSpec((tm, 1), lambda i, k: (i, 0)), vec],
                 out_specs=[row, row, vec, vec],
                 out_shape=[jax.ShapeDtypeStruct((T, N), F32), jax.ShapeDtypeStruct((T, N), BF16),
                            jax.ShapeDtypeStruct((1, N), F32), jax.ShapeDtypeStruct((1, N), F32)],
                 scratch=scratch, carry=carry)(a, w, res, xh, rstd, g)


def mm_tn(a, g, *, a_spec, g_spec, o_spec, out_shape, grid, acc_shape, P, tnw, name):
    n_t = grid[2]

    def body(a_ref, g_ref, o_ref, acc):
        t = pl.program_id(2)

        @pl.when(t == 0)
        def _():
            acc[...] = jnp.zeros_like(acc)

        acc[...] += _dot_tn(a_ref[...], g_ref[...])

        @pl.when(t == n_t - 1)
        def _():
            for p in range(P):
                o_ref[p] = acc[:, p * tnw:(p + 1) * tnw].astype(BF16)

    return _call(body, name=name, grid=grid, in_specs=[a_spec, g_spec], out_specs=o_spec,
                 out_shape=jax.ShapeDtypeStruct(out_shape, BF16),
                 scratch=[pltpu.VMEM(acc_shape, F32)])(a, g)


def _causal_conv(ext, halo, w, b):
    s1 = pltpu.roll(ext, 1, 0)[halo:]
    s2 = pltpu.roll(ext, 2, 0)[halo:]
    return b + w[2:3] * ext[halo:] + w[1:2] * s1 + w[0:1] * s2, s1, s2


def ffn_up(xb, wup, cw, cb, l, *, name, carry=()):
    T, D = xb.shape
    Ns = wup.shape[-1]
    F = 2 * Ns
    tn = _tile(Ns, 256)
    nps = Ns // tn
    n_j = F // tn
    tm = _tile(T, 1024, 8)

    def body(x_ref, wa_ref, wv_ref, cwa_ref, cwv_ref, cba_ref, cbv_ref, h_ref, hc_ref, f_ref, carry):
        i = pl.program_id(1)

        @pl.when(i == 0)
        def _():
            carry[...] = jnp.zeros_like(carry)

        xv = x_ref[...]
        ha = _dot(xv, wa_ref[...])
        hv = _dot(xv, wv_ref[...])
        ca, _, _ = _causal_conv(jnp.concatenate([carry[0], ha], axis=0), CONV_HALO, cwa_ref[...], cba_ref[...])
        cv, _, _ = _causal_conv(jnp.concatenate([carry[1], hv], axis=0), CONV_HALO, cwv_ref[...], cbv_ref[...])
        carry[0] = ha[tm - CONV_HALO:]
        carry[1] = hv[tm - CONV_HALO:]
        h_ref[0] = ha.astype(BF16)
        h_ref[1] = hv.astype(BF16)
        hc_ref[0] = ca.astype(BF16)
        hc_ref[1] = cv.astype(BF16)
        f_ref[...] = (ca * _sigmoid(ca) * cv).astype(BF16)

    wspec_a = pl.BlockSpec((None, None, D, tn), lambda j, i: (0, j // nps, 0, j % nps))
    wspec_v = pl.BlockSpec((None, None, D, tn), lambda j, i: (0, 2 + j // nps, 0, j % nps))
    pair_tile = pl.BlockSpec((2, tm, tn), lambda j, i: (0, i, j))
    return _call(
        body, name=name, grid=(n_j, T // tm),
        in_specs=[pl.BlockSpec((tm, D), lambda j, i: (i, 0)), wspec_a, wspec_v,
                  pl.BlockSpec((None, 3, tn), lambda j, i: (l, 0, j)),
                  pl.BlockSpec((None, 3, tn), lambda j, i: (l, 0, n_j + j)),
                  pl.BlockSpec((None, 1, tn), lambda j, i: (l, 0, j)),
                  pl.BlockSpec((None, 1, tn), lambda j, i: (l, 0, n_j + j))],
        out_specs=[pair_tile, pair_tile, pl.BlockSpec((tm, tn), lambda j, i: (i, j))],
        out_shape=[jax.ShapeDtypeStruct((2, T, F), BF16), jax.ShapeDtypeStruct((2, T, F), BF16),
                   jax.ShapeDtypeStruct((T, F), BF16)],
        scratch=[pltpu.VMEM((2, CONV_HALO, tn), F32)], carry=carry)(xb, wup, wup, cw, cw, cb, cb)


def ffn_dgate(db16, wdn, h, hc, cw, l, *, name, carry=()):
    T, D = db16.shape
    F = h.shape[-1]
    tn = _tile(F, 512)
    n_j = F // tn
    tm = _tile(T, 512, 16)
    n_i = T // tm
    n_ext = tm + CONV_HALO

    def body(d_ref, w_ref, h_ref, hc_ref, cwa_ref, cwv_ref, dh_ref, dcw_ref, dcb_ref, carry):
        ip = pl.program_id(1)

        @pl.when(ip == 0)
        def _():
            carry[...] = jnp.zeros_like(carry)
            dcw_ref[...] = jnp.zeros_like(dcw_ref)
            dcb_ref[...] = jnp.zeros_like(dcb_ref)

        df = _dot_nt(d_ref[...], w_ref[...])
        ca = hc_ref[0].astype(F32)
        cv = hc_ref[1].astype(F32)
        sig = _sigmoid(ca)
        sil = ca * sig
        da = df * cv * (sig + sil * (1.0 - sig))
        dv = df * sil
        for half, (dc, w_ref_h) in enumerate(((da, cwa_ref), (dv, cwv_ref))):
            w = w_ref_h[...]
            h0 = h_ref[half].astype(F32)
            ext = jnp.concatenate([dc, carry[half]], axis=0)
            n1 = pltpu.roll(ext, n_ext - 1, 0)[:tm]
            n2 = pltpu.roll(ext, n_ext - 2, 0)[:tm]
            dcb_ref[half] += jnp.sum(dc, axis=0, keepdims=True)
            dcw_ref[half] += jnp.concatenate(
                [jnp.sum(n2 * h0, axis=0, keepdims=True), jnp.sum(n1 * h0, axis=0, keepdims=True),
                 jnp.sum(dc * h0, axis=0, keepdims=True)], axis=0)
            dh_ref[half] = (w[2:3] * dc + w[1:2] * n1 + w[0:1] * n2).astype(BF16)
            carry[half] = dc[:CONV_HALO]

    rev = lambda ip: n_i - 1 - ip
    tile = pl.BlockSpec((2, tm, tn), lambda j, ip: (0, rev(ip), j))
    return _call(
        body, name=name, grid=(n_j, n_i),
        in_specs=[pl.BlockSpec((tm, D), lambda j, ip: (rev(ip), 0)),
                  pl.BlockSpec((None, tn, D), lambda j, ip: (0, j, 0)), tile, tile,
                  pl.BlockSpec((None, 3, tn), lambda j, ip: (l, 0, j)),
                  pl.BlockSpec((None, 3, tn), lambda j, ip: (l, 0, n_j + j))],
        out_specs=[tile, pl.BlockSpec((2, 3, tn), lambda j, ip: (0, 0, j)),
                   pl.BlockSpec((2, 1, tn), lambda j, ip: (0, 0, j))],
        out_shape=[jax.ShapeDtypeStruct((2, T, F), BF16), jax.ShapeDtypeStruct((2, 3, F), F32),
                   jax.ShapeDtypeStruct((2, 1, F), F32)],
        scratch=[pltpu.VMEM((2, CONV_HALO, tn), F32)], carry=carry)(db16, wdn, h, hc, cw, cw)


def _pool_fwd(ext, xb_g, t_glob, win):
    e = ext
    sft = 1
    while sft < win:
        e = e + pltpu.roll(e, sft, 0)
        sft *= 2
    cnt = jnp.minimum(t_glob + 1.0, float(win))
    return e[POOL_HALO:] / cnt - xb_g


def gating_fwd(h0, lg, lb, ws, bsT, wp, sc, *, name, carry=()):
    T = h0.shape[0]
    DA = lg.shape[-1]
    DB = sc.shape[-1]
    HA = DA // A_HEAD
    G = len(B_WINDOWS)
    CG = DB // G
    tm = _tile(T, 512, A_CHUNK)
    n_c = tm // A_CHUNK

    def body(h_ref, halo_ref, lg_ref, lb_ref, ws_ref, bsT_ref, wp_ref, sc_ref, cat_ref):
        i = pl.program_id(0)
        hu = h_ref[:, 0:DA].astype(F32)
        hv = h_ref[:, DA:2 * DA].astype(F32)
        xb = h_ref[:, 2 * DA:].astype(F32)
        u = _gelu(hu)
        vn, _, _ = _ln_fwd(_gelu(hv), lg_ref[...], lb_ref[...])
        vnb = vn.astype(BF16)
        rr = lax.broadcasted_iota(jnp.int32, (A_CHUNK, A_CHUNK), 0)
        cc = lax.broadcasted_iota(jnp.int32, (A_CHUNK, A_CHUNK), 1)
        for hh in range(HA):
            wt = jnp.where(rr >= cc, ws_ref[hh], 0.0).astype(BF16)
            cs = slice(hh * A_HEAD, (hh + 1) * A_HEAD)
            for n in range(n_c):
                rs = slice(n * A_CHUNK, (n + 1) * A_CHUNK)
                s = _dot(wt, vnb[rs, cs]) + bsT_ref[:, hh:hh + 1]
                cat_ref[rs, cs] = (u[rs, cs] * s).astype(BF16)
        halo = jnp.where(i > 0, halo_ref[...].astype(F32), 0.0)
        ext = jnp.concatenate([halo, xb], axis=0)
        t_glob = (i * tm + lax.broadcasted_iota(jnp.int32, (tm, 1), 0)).astype(F32)
        for g, win in enumerate(B_WINDOWS):
            gs = slice(g * CG, (g + 1) * CG)
            p = _pool_fwd(ext[:, gs], xb[:, gs], t_glob, win)
            z = _dot(p.astype(BF16), wp_ref[g])
            cat_ref[:, DA + g * CG:DA + (g + 1) * CG] = (z * sc_ref[:, gs]).astype(BF16)

    full = lambda a: pl.BlockSpec(a.shape, lambda i: (0,) * a.ndim)
    hpb = tm // POOL_HALO
    return _call(
        body, name=name, grid=(T // tm,),
        in_specs=[pl.BlockSpec((tm, 2 * DA + DB), lambda i: (i, 0)),
                  pl.BlockSpec((POOL_HALO, DB), lambda i: (jnp.maximum(i * hpb - 1, 0), 2 * DA // DB)),
                  full(lg), full(lb), full(ws), full(bsT), full(wp), full(sc)],
        out_specs=pl.BlockSpec((tm, DA + DB), lambda i: (i, 0)),
        out_shape=jax.ShapeDtypeStruct((T, DA + DB), BF16), carry=carry)(h0, h0, lg, lb, ws, bsT, wp, sc)


def gating_bwd(h0, dcat, lg, lb, ws, wsT, bsT, wp, sc, *, name, carry=()):
    T = h0.shape[0]
    DA = lg.shape[-1]
    DB = sc.shape[-1]
    HA = DA // A_HEAD
    G = len(B_WINDOWS)
    CG = DB // G
    tm = _tile(T, 512, A_CHUNK)
    n_i = T // tm
    n_c = tm // A_CHUNK
    n_ext = tm + POOL_HALO

    def body(h_ref, halo_ref, dc_ref, dhalo_ref, lg_ref, lb_ref, ws_ref, wsT_ref, bsT_ref, wp_ref, sc_ref,
             dh_ref, dws_ref, dbsT_ref, dlg_ref, dlb_ref, dsc_ref, dwp_ref, dvn_sc):
        i = pl.program_id(0)

        @pl.when(i == 0)
        def _():
            for r in (dws_ref, dbsT_ref, dlg_ref, dlb_ref, dsc_ref, dwp_ref):
                r[...] = jnp.zeros_like(r)

        hu = h_ref[:, 0:DA].astype(F32)
        hv = h_ref[:, DA:2 * DA].astype(F32)
        xb = h_ref[:, 2 * DA:].astype(F32)
        u = _gelu(hu)
        gu = _gelu_grad(hu)
        lgv = lg_ref[...]
        vn, vhat, rstd = _ln_fwd(_gelu(hv), lgv, lb_ref[...])
        vnb = vn.astype(BF16)
        rr = lax.broadcasted_iota(jnp.int32, (A_CHUNK, A_CHUNK), 0)
        cc = lax.broadcasted_iota(jnp.int32, (A_CHUNK, A_CHUNK), 1)
        for hh in range(HA):
            wt = jnp.where(rr >= cc, ws_ref[hh], 0.0).astype(BF16)
            wtT = jnp.where(rr <= cc, wsT_ref[hh], 0.0).astype(BF16)
            cs = slice(hh * A_HEAD, (hh + 1) * A_HEAD)
            dws = jnp.zeros((A_CHUNK, A_CHUNK), F32)
            dbs = jnp.zeros((A_CHUNK, 1), F32)
            for n in range(n_c):
                rs = slice(n * A_CHUNK, (n + 1) * A_CHUNK)
                vb = vnb[rs, cs]
                s = _dot(wt, vb) + bsT_ref[:, hh:hh + 1]
                dya = dc_ref[rs, cs].astype(F32)
                ds = dya * u[rs, cs]
                dh_ref[rs, cs] = (dya * s * gu[rs, cs]).astype(BF16)
                dsb = ds.astype(BF16)
                dbs = dbs + jnp.sum(ds, axis=1, keepdims=True)
                dws = dws + _dot_nt(dsb, vb)
                dvn_sc[rs, cs] = _dot(wtT, dsb)
            dws_ref[hh] += jnp.where(rr >= cc, dws, 0.0)
            dbsT_ref[:, hh:hh + 1] += dbs
        dvg, dlg, dlb = _ln_bwd(dvn_sc[...], vhat, rstd, lgv)
        dlg_ref[...] += dlg
        dlb_ref[...] += dlb
        dh_ref[:, DA:2 * DA] = (dvg * _gelu_grad(hv)).astype(BF16)

        halo = jnp.where(i > 0, halo_ref[...].astype(F32), 0.0)
        ext = jnp.concatenate([halo, xb], axis=0)
        t_glob = (i * tm + lax.broadcasted_iota(jnp.int32, (tm, 1), 0)).astype(F32)
        t_ext = (i * tm + lax.broadcasted_iota(jnp.int32, (n_ext, 1), 0)).astype(F32)
        dyb = dc_ref[:, DA:].astype(F32)
        dhalo = jnp.where(i < n_i - 1, dhalo_ref[...].astype(F32), 0.0)
        dyb_ext = jnp.concatenate([dyb, dhalo], axis=0)
        for g, win in enumerate(B_WINDOWS):
            gs = slice(g * CG, (g + 1) * CG)
            pb = _pool_fwd(ext[:, gs], xb[:, gs], t_glob, win).astype(BF16)
            wpg = wp_ref[g]
            z = _dot(pb, wpg)
            dsc_ref[:, gs] += jnp.sum(dyb[:, gs] * z, axis=0, keepdims=True)
            dzb = (dyb_ext[:, gs] * sc_ref[:, gs]).astype(BF16)
            dwp_ref[g] += _dot_tn(pb, dzb[:tm])
            dp = _dot_nt(dzb, wpg)
            e = dp / jnp.minimum(t_ext + 1.0, float(win))
            sft = 1
            while sft < win:
                e = e + pltpu.roll(e, n_ext - sft, 0)
                sft *= 2
            dh_ref[:, 2 * DA + g * CG:2 * DA + (g + 1) * CG] = (e[:tm] - dp[:tm]).astype(BF16)

    full = lambda a: pl.BlockSpec(a.shape, lambda i: (0,) * a.ndim)
    hpb = tm // POOL_HALO
    n_hb = T // POOL_HALO
    outs = [jax.ShapeDtypeStruct((T, 2 * DA + DB), BF16), jax.ShapeDtypeStruct(ws.shape, F32),
            jax.ShapeDtypeStruct(bsT.shape, F32), jax.ShapeDtypeStruct(lg.shape, F32),
            jax.ShapeDtypeStruct(lb.shape, F32), jax.ShapeDtypeStruct(sc.shape, F32),
            jax.ShapeDtypeStruct(wp.shape, F32)]
    return _call(
        body, name=name, grid=(n_i,),
        in_specs=[pl.BlockSpec((tm, 2 * DA + DB), lambda i: (i, 0)),
                  pl.BlockSpec((POOL_HALO, DB), lambda i: (jnp.maximum(i * hpb - 1, 0), 2 * DA // DB)),
                  pl.BlockSpec((tm, DA + DB), lambda i: (i, 0)),
                  pl.BlockSpec((POOL_HALO, DB), lambda i: (jnp.minimum((i + 1) * hpb, n_hb - 1), DA // DB)),
                  full(lg), full(lb), full(ws), full(wsT), full(bsT), full(wp), full(sc)],
        out_specs=[pl.BlockSpec((tm, 2 * DA + DB), lambda i: (i, 0))] + [full(o) for o in outs[1:]],
        out_shape=outs,
        scratch=[pltpu.VMEM((tm, DA), F32)], carry=carry)(h0, h0, dcat, dcat, lg, lb, ws, wsT, bsT, wp, sc)


HP = 2 * C_HEAD


def _hgrn_gates(h_ref, rows, lbv, tri):
    pre = []
    for r in rows:
        blk = h_ref[r, :].astype(F32)
        q = blk[:, 0:HP]
        sg = _sigmoid(blk[:, HP:2 * HP])
        f = lbv + (1.0 - lbv) * sg
        pre.append(dict(q=q, sq=_sigmoid(q), sg=sg, f=f, k=1.0 - f, lf=jnp.log(f), v=blk[:, 2 * HP:3 * HP],
                        gg=blk[:, 3 * HP:4 * HP]))
    bcums = [_exact_tri_dot(tri, p["lf"]) for p in pre]
    out = []
    for p, bcum in zip(pre, bcums):
        blast = bcum[C_CHUNK - 1:C_CHUNK]
        e_in = jnp.exp(bcum)
        e_out = jnp.exp(-bcum)
        e_end = jnp.exp(blast - bcum)
        out.append(dict(p, e_in=e_in, e_out=e_out, e_end=e_end, qd=p["q"] * p["sq"] * e_in, kd=p["k"] * e_out,
                        ke=p["k"] * e_end, dec=jnp.exp(blast)))
    return out


def _lower_bound(lbp_ref):
    p0, p1 = lbp_ref[0:1], lbp_ref[1:2]
    mx = jnp.maximum(p0, p1)
    e0, e1 = jnp.exp(p0 - mx), jnp.exp(p1 - mx)
    return e1 / (e0 + e1)


def hgrn_fwd(h1p, lbp, gn, *, name, carry=()):
    T = h1p.shape[0]
    DC = gn.shape[-1]
    n_p = DC // HP
    tt = _tile(T, 512, C_CHUNK)
    n_c = tt // C_CHUNK

    def body(h_ref, lbp_ref, gn_ref, y_ref, o_ref, sp_ref, st):
        i = pl.program_id(1)

        @pl.when(i == 0)
        def _():
            st[...] = jnp.zeros_like(st)

        lbv = _lower_bound(lbp_ref)
        gnv = gn_ref[...]
        rr = lax.broadcasted_iota(jnp.int32, (C_CHUNK, C_CHUNK), 0)
        cc = lax.broadcasted_iota(jnp.int32, (C_CHUNK, C_CHUNK), 1)
        causal = rr >= cc
        tri = jnp.where(causal, 1.0, 0.0).astype(BF16)

        heads = [slice(hd * C_HEAD, (hd + 1) * C_HEAD) for hd in range(2)]
        rows = [slice(n * C_CHUNK, (n + 1) * C_CHUNK) for n in range(n_c)]
        nh = [(n, hd) for n in range(n_c) for hd in range(2)]
        gates = _hgrn_gates(h_ref, rows, lbv, tri)
        b16 = lambda key: {(n, hd): gates[n][key][:, heads[hd]].astype(BF16) for n, hd in nh}
        qd, kd, ke, vb = b16("qd"), b16("kd"), b16("ke"), b16("v")
        att = {i: jnp.where(causal, _dot_nt(qd[i], kd[i]), 0.0).astype(BF16) for i in nh}
        intra = {i: _dot(att[i], vb[i]) for i in nh}
        upd = {i: _dot_tn(vb[i], ke[i]) for i in nh}
        s = [st[0], st[1]]
        entering = {}
        for n, hd in nh:
            entering[n, hd] = s[hd]
            sp_ref[hd, n] = s[hd]
            s[hd] = gates[n]["dec"][:, heads[hd]] * s[hd] + upd[n, hd]
        st[0], st[1] = s
        o = {i: intra[i] + _dot_nt(qd[i], entering[i].astype(BF16)) for i in nh}
        for n in range(n_c):
            os_ = [o[n, 0], o[n, 1]]
            o_ref[rows[n], :] = jnp.concatenate(os_, axis=1).astype(BF16)
            ys = [oh * lax.rsqrt(jnp.mean(oh * oh, axis=-1, keepdims=True) + LN_EPS) for oh in os_]
            y_ref[rows[n], :] = (jnp.concatenate(ys, axis=1) * gnv * _sigmoid(gates[n]["gg"])).astype(BF16)

    return _call(
        body, name=name, grid=(n_p, T // tt),
        in_specs=[pl.BlockSpec((tt, 4 * HP), lambda p, i: (i, p)), pl.BlockSpec((2, HP), lambda p, i: (0, p)),
                  pl.BlockSpec((1, HP), lambda p, i: (0, p))],
        out_specs=[pl.BlockSpec((tt, HP), lambda p, i: (i, p)), pl.BlockSpec((tt, HP), lambda p, i: (i, p)),
                   pl.BlockSpec((2, n_c, C_HEAD, C_HEAD), lambda p, i: (p, i, 0, 0))],
        out_shape=[jax.ShapeDtypeStruct((T, DC), BF16), jax.ShapeDtypeStruct((T, DC), BF16),
                   jax.ShapeDtypeStruct((2 * n_p, T // C_CHUNK, C_HEAD, C_HEAD), F32)],
        scratch=[pltpu.VMEM((2, C_HEAD, C_HEAD), F32)], carry=carry)(h1p, lbp, gn)


def hgrn_bwd(h1p, o_saved, dy, sp, lbp, gn, *, name, carry=()):
    T = h1p.shape[0]
    DC = gn.shape[-1]
    n_p = DC // HP
    tt = _tile(T, 512, C_CHUNK)
    n_i = T // tt
    n_c = tt // C_CHUNK

    def body(h_ref, o_ref, dy_ref, sp_ref, lbp_ref, gn_ref, dh_ref, dgn_ref, dlbp_ref, dst, dlb_acc):
        ip = pl.program_id(1)

        @pl.when(ip == 0)
        def _():
            dst[...] = jnp.zeros_like(dst)
            dlb_acc[...] = jnp.zeros_like(dlb_acc)
            dgn_ref[...] = jnp.zeros_like(dgn_ref)

        lbv = _lower_bound(lbp_ref)
        gnv = gn_ref[...]
        rr = lax.broadcasted_iota(jnp.int32, (C_CHUNK, C_CHUNK), 0)
        cc = lax.broadcasted_iota(jnp.int32, (C_CHUNK, C_CHUNK), 1)
        causal = rr >= cc
        tri = jnp.where(causal, 1.0, 0.0).astype(BF16)
        tri_t = jnp.where(rr <= cc, 1.0, 0.0).astype(BF16)
        last_row = lax.broadcasted_iota(jnp.int32, (C_CHUNK, 1), 0) == C_CHUNK - 1

        heads = [slice(hd * C_HEAD, (hd + 1) * C_HEAD) for hd in range(2)]
        rows = [slice(n * C_CHUNK, (n + 1) * C_CHUNK) for n in range(n_c)]
        cat = lambda parts: jnp.concatenate(parts, axis=1)
        nh = [(n, hd) for n in range(n_c) for hd in range(2)]
        pair = lambda d, n: cat([d[n, 0], d[n, 1]])
        gates = _hgrn_gates(h_ref, rows, lbv, tri)
        dgn = jnp.zeros((1, HP), F32)
        dgg, dob = [], {}
        for n in range(n_c):
            o = o_ref[rows[n], :].astype(F32)
            dyv = dy_ref[rows[n], :].astype(F32)
            sgg = _sigmoid(gates[n]["gg"])
            rrs = [lax.rsqrt(jnp.mean(o[:, cs] * o[:, cs], axis=-1, keepdims=True) + LN_EPS) for cs in heads]
            ohat = cat([o[:, cs] * r for cs, r in zip(heads, rrs)])
            dyn = dyv * sgg
            dgg.append(dyv * ohat * gnv * sgg * (1.0 - sgg))
            dgn = dgn + jnp.sum(dyn * ohat, axis=0, keepdims=True)
            dxh = dyn * gnv
            for hd, cs in enumerate(heads):
                dxh_h, oh_h = dxh[:, cs], ohat[:, cs]
                dob[n, hd] = (rrs[hd] * (dxh_h - oh_h * jnp.mean(dxh_h * oh_h, axis=-1, keepdims=True))).astype(BF16)
        dgn_ref[...] += dgn
        b16 = lambda key: {(n, hd): gates[n][key][:, heads[hd]].astype(BF16) for n, hd in nh}
        qd, kd, ke, vb = b16("qd"), b16("kd"), b16("ke"), b16("v")
        s_in = {(n, hd): sp_ref[hd, n] for n, hd in nh}
        att = {i: jnp.where(causal, _dot_nt(qd[i], kd[i]), 0.0).astype(BF16) for i in nh}
        datt = {i: jnp.where(causal, _dot_nt(dob[i], vb[i]), 0.0).astype(BF16) for i in nh}
        grow = {i: _dot_tn(dob[i], qd[i]) for i in nh}
        dv_i = {i: _dot_tn(att[i], dob[i]) for i in nh}
        dqd = {i: _dot(datt[i], kd[i]) + _dot(dob[i], s_in[i].astype(BF16)) for i in nh}
        dkd = {i: _dot_tn(datt[i], qd[i]) for i in nh}
        ds = [dst[0], dst[1]]
        leaving = {}
        for n in reversed(range(n_c)):
            for hd, cs in enumerate(heads):
                leaving[n, hd] = ds[hd]
                ds[hd] = gates[n]["dec"][:, cs] * ds[hd] + grow[n, hd]
        dst[0], dst[1] = ds
        dsb = {i: leaving[i].astype(BF16) for i in nh}
        dv = {i: dv_i[i] + _dot_nt(ke[i], dsb[i]) for i in nh}
        dke = {i: _dot(vb[i], dsb[i]) for i in nh}
        ddec = {i: jnp.sum(leaving[i] * s_in[i], axis=0, keepdims=True) for i in nh}
        mid = []
        for n in range(n_c):
            a = gates[n]
            dqd_n, dkd_n, dke_n = pair(dqd, n), pair(dkd, n), pair(dke, n)
            kek = dke_n * a["ke"]
            dblast = jnp.sum(kek, axis=0, keepdims=True) + pair(ddec, n) * a["dec"]
            dbcum = dqd_n * a["qd"] - dkd_n * a["kd"] - kek + jnp.where(last_row, dblast, 0.0)
            mid.append((dqd_n * a["e_in"], dkd_n * a["e_out"] + dke_n * a["e_end"], dbcum))
        dlf = [_exact_tri_dot(tri_t, m[2]) for m in mid]
        dlb = jnp.zeros((1, HP), F32)
        for n in range(n_c):
            a = gates[n]
            dqs, dk, _ = mid[n]
            df = dlf[n] / a["f"] - dk
            dlb = dlb + jnp.sum(df * (1.0 - a["sg"]), axis=0, keepdims=True)
            dfl = df * (1.0 - lbv) * a["sg"] * (1.0 - a["sg"])
            dq = dqs * a["sq"] * (1.0 + a["q"] * (1.0 - a["sq"]))
            dh_ref[rows[n], :] = cat([dq, dfl, pair(dv, n), dgg[n]]).astype(BF16)
        dlb_acc[...] += dlb

        @pl.when(ip == n_i - 1)
        def _():
            d1 = dlb_acc[...] * lbv * (1.0 - lbv)
            dlbp_ref[...] = jnp.concatenate([-d1, d1], axis=0)

    rev = lambda ip: n_i - 1 - ip
    return _call(
        body, name=name, grid=(n_p, n_i),
        in_specs=[pl.BlockSpec((tt, 4 * HP), lambda p, ip: (rev(ip), p)),
                  pl.BlockSpec((tt, HP), lambda p, ip: (rev(ip), p)),
                  pl.BlockSpec((tt, HP), lambda p, ip: (rev(ip), p)),
                  pl.BlockSpec((2, n_c, C_HEAD, C_HEAD), lambda p, ip: (p, rev(ip), 0, 0)),
                  pl.BlockSpec((2, HP), lambda p, ip: (0, p)), pl.BlockSpec((1, HP), lambda p, ip: (0, p))],
        out_specs=[pl.BlockSpec((tt, 4 * HP), lambda p, ip: (rev(ip), p)),
                   pl.BlockSpec((1, HP), lambda p, ip: (0, p)), pl.BlockSpec((2, HP), lambda p, ip: (0, p))],
        out_shape=[jax.ShapeDtypeStruct(h1p.shape, BF16), jax.ShapeDtypeStruct((1, DC), F32),
                   jax.ShapeDtypeStruct((2, DC), F32)],
        scratch=[pltpu.VMEM((2, C_HEAD, C_HEAD), F32), pltpu.VMEM((1, HP), F32)], carry=carry)(h1p, o_saved, dy, sp, lbp, gn)


def loss_bwd(xh, rstd, g, b, target, *, name):
    T, N = xh.shape
    tm = _tile(T, 512, 8)

    def body(xh_ref, rs_ref, g_ref, b_ref, t_ref, ls_ref, dr_ref, drb_ref, dg_ref, db_ref):
        i = pl.program_id(0)

        @pl.when(i == 0)
        def _():
            for r in (ls_ref, dg_ref, db_ref):
                r[...] = jnp.zeros_like(r)

        xhv, gv = xh_ref[...], g_ref[...]
        e = xhv * gv + b_ref[...] - t_ref[...]
        ls_ref[...] += 0.5 * jnp.sum(jnp.mean(e * e, axis=-1, keepdims=True), axis=0, keepdims=True)
        dr, dg, db = _ln_bwd(e / N, xhv, rs_ref[...], gv)
        dr_ref[...] = dr
        drb_ref[...] = dr.astype(BF16)
        dg_ref[...] += dg
        db_ref[...] += db

    row = pl.BlockSpec((tm, N), lambda i: (i, 0))
    vec = pl.BlockSpec((1, N), lambda i: (0, 0))
    return _call(body, name=name, grid=(T // tm,),
                 in_specs=[row, pl.BlockSpec((tm, 1), lambda i: (i, 0)), vec, vec, row],
                 out_specs=[pl.BlockSpec((1, LANES), lambda i: (0, 0)), row, row, vec, vec],
                 out_shape=[jax.ShapeDtypeStruct((1, LANES), F32), jax.ShapeDtypeStruct((T, N), F32),
                            jax.ShapeDtypeStruct((T, N), BF16), jax.ShapeDtypeStruct((1, N), F32),
                            jax.ShapeDtypeStruct((1, N), F32)])(xh, rstd, g, b, target)


def _mesh_pos():
    x, y, c = lax.axis_index("x"), lax.axis_index("y"), lax.axis_index("c")
    chips = [(1 - x, y), (x, 1 - y), (1 - x, 1 - y)]
    return x, y, c, chips


def _remote(src, dst, send, recv, j, dev):
    return pltpu.make_async_remote_copy(src_ref=src, dst_ref=dst, send_sem=send.at[j], recv_sem=recv.at[j],
                                        device_id=dev, device_id_type=MESH)


def mesh_ids():
    x, y, c, chips = _mesh_pos()
    return jnp.stack([c] + [2 * cx + cy for cx, cy in chips] + [2 * x + y]).astype(jnp.int32)


def _sibling():
    return (lax.axis_index("x"), lax.axis_index("y"), 1 - lax.axis_index("c"))


SWAP_PARTS = 2


def _swap_with_sibling(src, recv, send_sem, recv_sem, step):
    slot = step % 2
    br = src.shape[0]
    n = SWAP_PARTS if br % (16 * SWAP_PARTS) == 0 else 1
    parts = []
    for k in range(n):
        rows = pl.ds(k * (br // n), br // n)
        cp = pltpu.make_async_remote_copy(
            src_ref=src.at[rows], dst_ref=recv.at[slot, rows], send_sem=send_sem.at[slot * SWAP_PARTS + k],
            recv_sem=recv_sem.at[slot * SWAP_PARTS + k], device_id=_sibling(), device_id_type=MESH)
        cp.start()
        parts.append((cp, slot, rows))
    return parts


def _swap_scratch(br, C, dtype):
    return [pltpu.VMEM((2, br, C), dtype), pltpu.SemaphoreType.DMA((2 * SWAP_PARTS,)),
            pltpu.SemaphoreType.DMA((2 * SWAP_PARTS,))]


def _swap_rows(Rh, C, itemsize):
    return _tile(Rh, max(16, (3 << 20) // (C * itemsize)), 16)


def cast_to_slot(a3, l, me1, name):
    _, R, C = a3.shape
    br = _tile(R, max(8, (1 << 20) // C), 16)

    def body(me_ref, a_ref, o_ref):
        o_ref[...] = a_ref[...].astype(BF16)

    return _call(body, name=name, grid=(R // br,), prefetch=1,
                 in_specs=[pl.BlockSpec((None, br, C), lambda r, me: (l, r, 0))],
                 out_specs=pl.BlockSpec((None, None, br, C), lambda r, me: (0, me[0], r, 0)),
                 out_shape=jax.ShapeDtypeStruct((1, 4, R, C), BF16))(me1, a3)


def all_gather_chips(big, small, *, name):
    nb, ns = len(big), len(small)
    layers = [(t, l) for t in range(nb) for l in range(big[t].shape[0])]
    n_big = 3 * len(layers)
    n_rem = n_big + 3 * ns

    def body(*refs):
        small_in = refs[nb:nb + ns]
        bufs, small_out = refs[nb + ns:2 * nb + ns], refs[2 * nb + ns:2 * (nb + ns)]
        send, recv, loc = refs[2 * (nb + ns):]
        x, y, c, chips = _mesh_pos()
        me = 2 * x + y
        ids = [2 * cx + cy for cx, cy in chips]
        started, sends = [], []
        for t in range(ns):
            cp = pltpu.make_async_copy(small_in[t], small_out[t].at[me], loc.at[t])
            cp.start()
            started.append(cp)
        for q, (t, l) in enumerate(layers):
            for k, chip in enumerate(chips):
                blk = bufs[t].at[l, me, c]
                cp = _remote(blk, blk, send, recv, 3 * q + k, (*chip, c))
                cp.start()
                sends.append(cp)
        for t in range(ns):
            for k, chip in enumerate(chips):
                cp = _remote(small_in[t], small_out[t].at[me], send, recv, n_big + 3 * t + k, (*chip, c))
                cp.start()
                sends.append(cp)
        for q, (t, l) in enumerate(layers):
            for k in range(3):
                blk = bufs[t].at[l, ids[k], c]
                _remote(blk, blk, send, recv, 3 * q + k, (x, y, c)).wait_recv()
        for t in range(ns):
            for k in range(3):
                blk = small_out[t].at[ids[k]]
                _remote(blk, blk, send, recv, n_big + 3 * t + k, (x, y, c)).wait_recv()
        for cp in sends:
            cp.wait_send()
        for cp in started:
            cp.wait()

    out_shape = [jax.ShapeDtypeStruct(a.shape, a.dtype) for a in big]
    out_shape += [jax.ShapeDtypeStruct((4,) + a.shape, a.dtype) for a in small]
    return _call(body, name=name, in_specs=[ANY] * (nb + ns), out_specs=[ANY] * (nb + ns), out_shape=out_shape,
                 aliases={t: t for t in range(nb)},
                 scratch=[pltpu.SemaphoreType.DMA((n_rem,)), pltpu.SemaphoreType.DMA((n_rem,)),
                          pltpu.SemaphoreType.DMA((max(ns, 1),))])(*big, *small)


def all_gather_pair(buf, ids, *, name):
    L, _, _, Rh, C = buf.shape
    br = _swap_rows(Rh, C, 2)
    n_r = Rh // br

    def body(ids_ref, in_ref, o_ref, recv, ssem, rsem):
        step = (pl.program_id(0) * 3 + pl.program_id(1)) * n_r + pl.program_id(2)
        parts = _swap_with_sibling(in_ref, recv, ssem, rsem, step)
        for cp, slot, rows in parts:
            cp.wait_recv()
            o_ref[rows, :] = recv[slot, rows, :]
        for cp, _, _ in parts:
            cp.wait_send()

    at = lambda l, s, h, r: (((l * 4 + s) * 2 + h) * n_r + r, 0)
    out = _call(body, name=name, grid=(L, 3, n_r), prefetch=1,
                in_specs=[pl.BlockSpec((br, C), lambda l, k, r, ids: at(l, ids[1 + k], ids[0], r))],
                out_specs=pl.BlockSpec((br, C), lambda l, k, r, ids: at(l, ids[1 + k], 1 - ids[0], r)),
                out_shape=jax.ShapeDtypeStruct((L * 8 * Rh, C), buf.dtype), aliases={1: 0},
                scratch=_swap_scratch(br, C, BF16))(ids, buf.reshape(L * 8 * Rh, C))
    return out.reshape(buf.shape)


def rs_pair_add(grad, ids, *, name):
    _, _, Rh, C = grad.shape
    br = _swap_rows(Rh, C, 2)
    n_r = Rh // br

    def body(ids_ref, send_ref, keep_ref, pb_ref, own_ref, recv, ssem, rsem):
        ph = pl.program_id(0)
        parts = _swap_with_sibling(send_ref, recv, ssem, rsem, ph * n_r + pl.program_id(1))
        for cp, slot, rows in parts:
            cp.wait_recv()
            s = keep_ref[rows, :].astype(F32) + recv[slot, rows, :].astype(F32)

            @pl.when(ph == 0)
            def _():
                own_ref[rows, :] = s

            @pl.when(ph > 0)
            def _():
                pb_ref[rows, :] = s.astype(BF16)

        for cp, _, _ in parts:
            cp.wait_send()

    rel = lambda ph: (ph + 3) % 4
    at = lambda s, h, r: ((s * 2 + h) * n_r + r, 0)
    grad = grad.reshape(8 * Rh, C)
    return _call(
        body, name=name, grid=(4, n_r), prefetch=1,
        in_specs=[pl.BlockSpec((br, C), lambda ph, r, ids: at(ids[1 + rel(ph)], 1 - ids[0], r)),
                  pl.BlockSpec((br, C), lambda ph, r, ids: at(ids[1 + rel(ph)], ids[0], r))],
        out_specs=[pl.BlockSpec((None, br, C), lambda ph, r, ids: (jnp.maximum(ph - 1, 0), jnp.where(ph == 0, 0, r), 0)),
                   pl.BlockSpec((br, C), lambda ph, r, ids: (jnp.where(ph == 0, r, n_r - 1), 0))],
        out_shape=[jax.ShapeDtypeStruct((3, Rh, C), BF16), jax.ShapeDtypeStruct((Rh, C), F32)],
        scratch=_swap_scratch(br, C, BF16))(ids, grad, grad)


def rs_finish(owns, gots, *, name):
    L = len(owns)
    Rh, C = owns[0].shape
    br = _swap_rows(Rh, C, 4)
    n_r = Rh // br

    def body(*refs):
        own_refs, got_refs, o_ref = refs[:L], refs[L:2 * L], refs[2 * L]
        recv, ssem, rsem = refs[2 * L + 1:]
        l = pl.program_id(0)
        c = lax.axis_index("c")
        for ll in range(L):
            @pl.when(l == ll)
            def _():
                s = own_refs[ll][...]
                for k in range(3):
                    s = s + got_refs[ll][k].astype(F32)
                o_ref[c] = s

        parts = _swap_with_sibling(o_ref.at[c], recv, ssem, rsem, l * n_r + pl.program_id(1))
        for cp, slot, rows in parts:
            cp.wait_recv()
            o_ref[1 - c, rows, :] = recv[slot, rows, :]
        for cp, _, _ in parts:
            cp.wait_send()

    def at_layer(ll):
        return lambda l, r: jnp.where(l == ll, r, jnp.where(l < ll, 0, n_r - 1))

    in_specs = [pl.BlockSpec((br, C), lambda l, r, ll=ll: (at_layer(ll)(l, r), 0)) for ll in range(L)]
    in_specs += [pl.BlockSpec((3, br, C), lambda l, r, ll=ll: (0, at_layer(ll)(l, r), 0)) for ll in range(L)]
    out = _call(body, name=name, grid=(L, n_r), in_specs=in_specs,
                out_specs=pl.BlockSpec((2, br, C), lambda l, r: (l, r, 0)),
                out_shape=jax.ShapeDtypeStruct((L * 2, Rh, C), F32),
                scratch=_swap_scratch(br, C, F32))(*owns, *gots)
    return out.reshape(L, 2, Rh, C)


SMALL_ROW_MULTIPLE = 64


def all_reduce_small(buf, *, name):
    rows = buf.shape[0]
    h, q, e = rows // 2, rows // 4, rows // 8

    def body(x_ref, o_ref, s1, r1, s2, r2, s3, r3, send, recv):
        x, y, c, _ = _mesh_pos()
        sib, xn, yn = (x, y, 1 - c), (1 - x, y, c), (x, 1 - y, c)
        at = lambda off, n: pl.ds(pl.multiple_of(off, 8), n)
        cp = _remote(x_ref.at[at((1 - c) * h, h)], r1, send, recv, 0, sib)
        cp.start()
        cp.wait()
        s1[...] = x_ref[at(c * h, h), :] + r1[...]
        cp = _remote(s1.at[at((1 - x) * q, q)], r2, send, recv, 1, xn)
        cp.start()
        cp.wait()
        s2[...] = s1[at(x * q, q), :] + r2[...]
        cp = _remote(s2.at[at((1 - y) * e, e)], r3, send, recv, 2, yn)
        cp.start()
        cp.wait()
        s3[...] = s2[at(y * e, e), :] + r3[...]
        mine = c * h + x * q + y * e
        o_ref[at(mine, e), :] = s3[...]
        theirs = o_ref.at[at(c * h + x * q + (1 - y) * e, e)]
        cp = _remote(s3, o_ref.at[at(mine, e)], send, recv, 3, yn)
        cp.start()
        cp.wait_send()
        _remote(theirs, theirs, send, recv, 3, yn).wait_recv()
        quarter = o_ref.at[at(c * h + x * q, q)]
        theirs = o_ref.at[at(c * h + (1 - x) * q, q)]
        cp = _remote(quarter, quarter, send, recv, 4, xn)
        cp.start()
        cp.wait_send()
        _remote(theirs, theirs, send, recv, 4, xn).wait_recv()
        half = o_ref.at[at(c * h, h)]
        theirs = o_ref.at[at((1 - c) * h, h)]
        cp = _remote(half, half, send, recv, 5, sib)
        cp.start()
        cp.wait_send()
        _remote(theirs, theirs, send, recv, 5, sib).wait_recv()

    part = lambda n: pltpu.VMEM((n, LANES), F32)
    return _call(body, name=name, in_specs=[VMEM_SPEC], out_specs=VMEM_SPEC,
                 out_shape=jax.ShapeDtypeStruct(buf.shape, F32),
                 scratch=[part(h), part(h), part(q), part(q), part(e), part(e), pltpu.SemaphoreType.DMA((6,)),
                          pltpu.SemaphoreType.DMA((6,))])(buf)


def _pack(arrs, row_multiple=8):
    parts = []
    for a in arrs:
        f = a.reshape(-1).astype(F32)
        parts.append(jnp.pad(f, (0, (-f.shape[0]) % PACK_ALIGN)))
    total = sum(p.shape[0] for p in parts)
    parts.append(jnp.zeros(((-total) % (row_multiple * LANES),), F32))
    return jnp.concatenate(parts).reshape(-1, LANES)


def _unpack(buf, shapes):
    flat = buf.reshape(-1)
    out, off = [], 0
    for s in shapes:
        n = math.prod(s)
        out.append(flat[off:off + n].reshape(s))
        off += n + (-n) % PACK_ALIGN
    return out


_WEIGHTS = ['ev_w_in', 'ev_ln_v_g', 'ev_ln_v_b', 'ev_w_s', 'ev_b_s', 'ev_w_pool', 'ev_pool_scale', 'ev_w_out',
            'od_w_in', 'od_norm_g', 'od_w_out', 'lb_param', 'ffn_w_up', 'ffn_conv_w', 'ffn_conv_b', 'ffn_w_down',
            'ln1_g', 'ln1_b', 'ln2_g', 'ln2_b']
_BIG = ['ev_w_in', 'ev_w_out', 'od_w_in', 'od_w_out', 'ffn_w_up', 'ffn_w_down']
_SMALL = [n for n in _WEIGHTS if n not in _BIG]


def kernel(x, ev_w_in, ev_ln_v_g, ev_ln_v_b, ev_w_s, ev_b_s, ev_w_pool, ev_pool_scale, ev_w_out, od_w_in, od_norm_g, od_w_out, lb_param, ffn_w_up, ffn_conv_w, ffn_conv_b, ffn_w_down, ln1_g, ln1_b, ln2_g, ln2_b, loss_target, m_ev_w_in, m_ev_ln_v_g, m_ev_ln_v_b, m_ev_w_s, m_ev_b_s, m_ev_w_pool, m_ev_pool_scale, m_ev_w_out, m_od_w_in, m_od_norm_g, m_od_w_out, m_lb_param, m_ffn_w_up, m_ffn_conv_w, m_ffn_conv_b, m_ffn_w_down, m_ln1_g, m_ln1_b, m_ln2_g, m_ln2_b, v_ev_w_in, v_ev_ln_v_g, v_ev_ln_v_b, v_ev_w_s, v_ev_b_s, v_ev_w_pool, v_ev_pool_scale, v_ev_w_out, v_od_w_in, v_od_norm_g, v_od_w_out, v_lb_param, v_ffn_w_up, v_ffn_conv_w, v_ffn_conv_b, v_ffn_w_down, v_ln1_g, v_ln1_b, v_ln2_g, v_ln2_b):
    given = dict(locals())
    w = {n: given[n] for n in _WEIGHTS}
    mom = {n: given["m_" + n] for n in _WEIGHTS}
    vel = {n: given["v_" + n] for n in _WEIGHTS}
    x2d = x[0]
    tgt = loss_target[0]
    T, D = x2d.shape
    DA = ev_ln_v_g.shape[-1]
    DB = ev_pool_scale.shape[-1]
    HA = ev_w_s.shape[1]
    G = len(B_WINDOWS)
    CG = DB // G
    DC = 4 * od_norm_g.shape[-1]
    F = ffn_conv_b.shape[-1] // 2
    chip = 2 * lax.axis_index("x") + lax.axis_index("y")

    ids = mesh_ids()
    halves = lambda a: a.reshape(1, 4, 2, a.shape[2] // 2, a.shape[3])
    slot = {(n, l): halves(cast_to_slot(w[n], l, ids[4:5], f"cast_{n}{l}"))
            for n in _BIG for l in range(w[n].shape[0])}

    def riding(*keys):
        return [IciCopy("gather", slot[k]) for k in keys]

    def pair(buf, key):
        g = all_gather_pair(buf, ids, name=f"all_gather_pair_{key[0]}{key[1]}")
        return g.reshape(1, 4, g.shape[3] * 2, g.shape[4])

    early = [('ev_w_in', 0), ('ev_w_out', 0)]
    gathered = all_gather_chips([slot[k] for k in early], [ev_w_pool[0], ffn_conv_w, od_norm_g],
                                name="all_gather_chips")
    wpool_full = gathered[2].transpose(1, 0, 2, 3).reshape(G, CG, CG)
    cw_full = gathered[3].transpose(1, 2, 0, 3).reshape(DEPTH, 3, 2 * F)
    gn_full = gathered[4].reshape(1, DC)
    win0 = pair(gathered[0], early[0])[0]
    wout0 = pair(gathered[1], early[1]).reshape(DA + DB, D)
    wup, wdn = {}, {}
    rh_up = D // 2
    cut1 = (rh_up * 35 // 100) // 16 * 16
    cut2 = cut1 + (rh_up * 18 // 100) // 16 * 16
    cb3 = ffn_conv_b.reshape(DEPTH, 1, 2 * F)
    ws = ev_w_s[0]
    wsT = jnp.swapaxes(ws, 1, 2)
    bsT = ev_b_s[0].T
    wpb = wpool_full.astype(BF16)
    ones = jnp.ones((1, D), F32)
    zeros = jnp.zeros((1, D), F32)
    row = lambda a, l: a[l:l + 1]

    Ns0 = win0.shape[-1]
    Nu = ffn_w_up.shape[-1]
    tm_big = _tile(T, 1024, 8)
    tm_ln = _tile(T, 512, 8)
    tk_ln = _tile(D, 512)
    n_p = DC // HP

    def nat_spec(Ns, tnw):
        nps = Ns // tnw
        return pl.BlockSpec((1, D, tnw), lambda i, j: (j // nps, 0, j % nps))

    perm_spec = pl.BlockSpec((4, D, HP), lambda i, j: (0, 0, j))

    xb16 = cast_bf16(x, "cast_x")[0]
    up0 = IciCopy("gather", slot[('ffn_w_up', 0)], rows=(0, cut1))
    h0 = mm_nn(xb16, win0, w_spec=nat_spec(Ns0, Ns0), P=1, tnw=Ns0, tm=tm_big, n_j=4, name="ev_in", carry=[up0])
    up0 = IciCopy("gather", up0.out, rows=(cut1, cut2 - cut1))
    cat = gating_fwd(h0, ev_ln_v_g, ev_ln_v_b, ws, bsT, wpb, ev_pool_scale, name="gating_fwd", carry=[up0])
    up0 = IciCopy("gather", up0.out, rows=(cut2, rh_up - cut2))

    def mix_ln(a, wmat, res, l, name, carry=()):
        K = a.shape[1]
        tk = _tile(K, 2048)
        return mm_ln(a, wmat, *res, row(ln1_g, l), row(ln1_b, l), w_spec=pl.BlockSpec((tk, D), lambda i, k: (k, 0)),
                     K=K, tk=tk, tm=tm_ln, name=name, carry=carry)

    def ffn_down(f, res, l, carry=()):
        tk = F // 4 if (F // 4) % LANES == 0 else _tile(F, 512)
        return mm_ln(f, wdn[l], *res, row(ln2_g, l), row(ln2_b, l),
                     w_spec=pl.BlockSpec((None, tk, D), lambda i, k: (0, k, 0)), K=F, tk=tk, tm=tm_ln,
                     name=f"ffn_down{l}", carry=carry)

    xh1, y1, rs1 = mix_ln(cat, wout0, (x2d, ones, zeros), 0, "ev_out", carry=[up0])
    wup[0] = pair(up0.out, ('ffn_w_up', 0))
    res1 = (xh1, row(ln1_g, 0), row(ln1_b, 0))
    ride = riding(('ffn_w_down', 0), ('od_w_in', 0))
    hf0, hc0, f0 = ffn_up(y1, wup[0], cw_full, cb3, 0, name="ffn_up0", carry=ride)
    wdn[0] = pair(ride[0].out, ('ffn_w_down', 0)).reshape(1, F, D)
    win1 = pair(ride[1].out, ('od_w_in', 0))[0]
    cut1 = (rh_up * 40 // 100) // 16 * 16
    cut2 = 2 * cut1
    up1 = IciCopy("gather", slot[('ffn_w_up', 1)], rows=(0, cut1))
    ride = riding(('od_w_out', 0))
    xh2, y2, rs2 = ffn_down(f0, res1, 0, carry=ride + [up1])
    wout1 = pair(ride[0].out, ('od_w_out', 0)).reshape(DC, D)
    res2 = (xh2, row(ln2_g, 0), row(ln2_b, 0))
    up1 = IciCopy("gather", up1.out, rows=(cut1, cut2 - cut1))
    h1p = mm_nn(y2, win1, w_spec=perm_spec, P=4, tnw=HP, tm=tm_big, n_j=n_p, name="od_in", carry=[up1])
    up1 = IciCopy("gather", up1.out, rows=(cut2, rh_up - cut2))
    yh, o_saved, sp = hgrn_fwd(h1p, lb_param, gn_full, name="hgrn_fwd", carry=[up1])
    wup[1] = pair(up1.out, ('ffn_w_up', 1))
    xh3, y3, rs3 = mix_ln(yh, wout1, res2, 1, "od_out")
    res3 = (xh3, row(ln1_g, 1), row(ln1_b, 1))
    ride = riding(('ffn_w_down', 1))
    hf1, hc1, f1 = ffn_up(y3, wup[1], cw_full, cb3, 1, name="ffn_up1", carry=ride)
    wdn[1] = pair(ride[0].out, ('ffn_w_down', 1)).reshape(1, F, D)
    xh4, y4, rs4 = ffn_down(f1, res3, 1)
    loss_p, dr, drb, dg_ln2_1, db_ln2_1 = loss_bwd(xh4, rs4, row(ln2_g, 1), row(ln2_b, 1), tgt, name="loss_bwd")

    tt = _tile(T, 2048, 16)
    n_t = T // tt
    tnu = Nu // 2 if (Nu // 2) % LANES == 0 else Nu
    upb = Nu // tnu
    tkd = _tile(D, 1024)

    def pair_sum(g4, name):
        pb, own = rs_pair_add(g4.reshape(4, 2, g4.shape[1] // 2, g4.shape[2]), ids, name="rs_pair_add_" + name)
        return IciCopy("scatter", pb), own

    def g_out(a, gb, name, tkk=None):
        K = a.shape[1]
        tkk = tkk or _tile(K, 1024)
        return mm_tn(a, gb, a_spec=pl.BlockSpec((tt, tkk), lambda kb, nb, t: (t, kb)),
                     g_spec=pl.BlockSpec((tt, tkd), lambda kb, nb, t: (t, nb)),
                     o_spec=pl.BlockSpec((1, tkk, tkd), lambda kb, nb, t: (0, kb, nb)), out_shape=(1, K, D),
                     grid=(K // tkk, D // tkd, n_t), acc_shape=(tkk, tkd), P=1, tnw=tkd, name=name)

    def ffn_bwd(l, dr2, dr2b, f, hf, hc, y_in, xh_in, rs_in):
        g_dn = g_out(f, dr2b, f"g_ffn_down{l}", tkk=F // 4 if (F // 4) % LANES == 0 else None)
        rs_dn = pair_sum(g_dn.reshape(4, F // 4, D), f"ffn_down{l}")
        dh, dcw, dcb = ffn_dgate(dr2b, wdn[l], hf, hc, cw_full, l, name=f"ffn_dgate{l}", carry=[rs_dn[0]])
        g_up = mm_tn(y_in, dh, a_spec=pl.BlockSpec((tt, tkd), lambda kb, nb, t: (t, kb)),
                     g_spec=pl.BlockSpec((None, tt, tnu), lambda kb, nb, t: (nb // (2 * upb), t, nb % (2 * upb))),
                     o_spec=pl.BlockSpec((1, tkd, tnu), lambda kb, nb, t: (nb // upb, kb, nb % upb)),
                     out_shape=(4, D, Nu), grid=(D // tkd, 4 * upb, n_t), acc_shape=(tkd, tnu), P=1, tnw=tnu,
                     name=f"g_ffn_up{l}")
        rs_up = pair_sum(g_up, f"ffn_up{l}")
        tku = tnu
        kps = Nu // tku
        out = mm_nt_res(dh, wup[l], dr2, (xh_in, rs_in, row(ln1_g, l)),
                        a_spec=pl.BlockSpec((None, tm_ln, tku), lambda i, k: (k // (2 * kps), i, k % (2 * kps))),
                        w_spec=pl.BlockSpec((None, 1, D, tku), lambda i, k: (0, k // kps, 0, k % kps)),
                        P=1, tnw=tku, n_k=4 * kps, tm=tm_ln, name=f"d_ffn_in{l}", carry=[rs_up[0]])
        return rs_dn, rs_up, dcw, dcb, out

    rs_dn1, rs_up1, dcw1, dcb1, (dr1, dr1b, dg_ln1_1, db_ln1_1) = ffn_bwd(1, dr, drb, f1, hf1, hc1, y3, xh3, rs3)
    rs_wout1 = pair_sum(g_out(yh, dr1b, "g_od_out").reshape(4, DC // 4, D), "od_out")
    dyh = mm_nt_plain(dr1b, wout1, tm=tm_big, tn=_tile(DC, 512), name="d_od_out")
    dh1p, d_gn, d_lbp = hgrn_bwd(h1p, o_saved, dyh, sp, lb_param, gn_full, name="hgrn_bwd", carry=[rs_wout1[0]])
    g_win1 = mm_tn(y2, dh1p, a_spec=pl.BlockSpec((tt, tkd), lambda kb, nb, t: (t, kb)),
                   g_spec=pl.BlockSpec((tt, 4 * HP), lambda kb, nb, t: (t, nb)),
                   o_spec=pl.BlockSpec((4, tkd, HP), lambda kb, nb, t: (0, kb, nb)), out_shape=(4, D, DC),
                   grid=(D // tkd, n_p, n_t), acc_shape=(tkd, 4 * HP), P=4, tnw=HP, name="g_od_in")
    rs_win1 = pair_sum(g_win1, "od_in")
    dr, drb, dg_ln2_0, db_ln2_0 = mm_nt_res(
        dh1p, win1, dr1, (xh2, rs2, row(ln2_g, 0)), a_spec=pl.BlockSpec((tm_ln, 4 * HP), lambda i, k: (i, k)),
        w_spec=pl.BlockSpec((4, D, HP), lambda i, k: (0, 0, k)), P=4, tnw=HP, n_k=n_p, tm=tm_ln, name="d_od_in",
        carry=[rs_win1[0]])
    rs_dn0, rs_up0, dcw0, dcb0, (dr1, dr1b, dg_ln1_0, db_ln1_0) = ffn_bwd(0, dr, drb, f0, hf0, hc0, y1, xh1, rs1)
    rs_wout0 = pair_sum(g_out(cat, dr1b, "g_ev_out").reshape(4, (DA + DB) // 4, D), "ev_out")
    dcat = mm_nt_plain(dr1b, wout0, tm=tm_big, tn=_tile(DA + DB, 512), name="d_ev_out")
    dh0, d_ws, d_bsT, d_lg, d_lb, d_sc, d_wp = gating_bwd(h0, dcat, ev_ln_v_g, ev_ln_v_b, ws, wsT, bsT, wpb,
                                                          ev_pool_scale, name="gating_bwd", carry=[rs_wout0[0]])
    g_win0 = mm_tn(xb16, dh0, a_spec=pl.BlockSpec((tt, tkd), lambda kb, nb, t: (t, kb)),
                   g_spec=pl.BlockSpec((tt, Ns0), lambda kb, nb, t: (t, nb)),
                   o_spec=pl.BlockSpec((1, tkd, Ns0), lambda kb, nb, t: (nb, kb, 0)), out_shape=(4, D, Ns0),
                   grid=(D // tkd, 4, n_t), acc_shape=(tkd, Ns0), P=1, tnw=Ns0, name="g_ev_in")
    rs_win0 = pair_sum(g_win0, "ev_in")
    grad_x = mm_nt_res(dh0, win0, dr1, None, a_spec=pl.BlockSpec((tm_ln, Ns0), lambda i, k: (i, k)),
                       w_spec=pl.BlockSpec((1, D, Ns0), lambda i, k: (k, 0, 0)), P=1, tnw=Ns0, n_k=4, tm=tm_ln,
                       name="d_ev_in", carry=[rs_win0[0]])

    per_weight = [[rs_win0], [rs_wout0], [rs_win1], [rs_wout1], [rs_up0, rs_up1], [rs_dn0, rs_dn1]]
    shared = [rs_finish([own for _, own in m], [cp.out for cp, _ in m], name="rs_finish_" + n)
              for n, m in zip(_BIG, per_weight)]
    big_g = {n: s.reshape(w[n].shape) for n, s in zip(_BIG, shared)}

    small_full = {
        'ev_ln_v_g': d_lg, 'ev_ln_v_b': d_lb, 'ev_w_s': d_ws[None], 'ev_b_s': d_bsT.T[None], 'ev_w_pool': d_wp[None],
        'ev_pool_scale': d_sc, 'od_norm_g': d_gn, 'lb_param': d_lbp,
        'ffn_conv_w': jnp.stack([jnp.concatenate([dcw0[0], dcw0[1]], axis=-1),
                                 jnp.concatenate([dcw1[0], dcw1[1]], axis=-1)]),
        'ffn_conv_b': jnp.stack([jnp.concatenate([dcb0[0, 0], dcb0[1, 0]]), jnp.concatenate([dcb1[0, 0], dcb1[1, 0]])]),
        'ln1_g': jnp.concatenate([dg_ln1_0, dg_ln1_1]), 'ln1_b': jnp.concatenate([db_ln1_0, db_ln1_1]),
        'ln2_g': jnp.concatenate([dg_ln2_0, dg_ln2_1]), 'ln2_b': jnp.concatenate([db_ln2_0, db_ln2_1])}
    packed = _pack([small_full[n] for n in _SMALL] + [loss_p[0, 0:1]], SMALL_ROW_MULTIPLE)
    reduced = _unpack(all_reduce_small(packed, name="all_reduce_small"),
                      [small_full[n].shape for n in _SMALL] + [(1,)])
    small_g = dict(zip(_SMALL, reduced[:-1]))
    loss = reduced[-1][0]
    small_g['ev_w_pool'] = lax.dynamic_slice_in_dim(small_g['ev_w_pool'], chip * (CG // 4), CG // 4, axis=2)
    small_g['ffn_conv_w'] = lax.dynamic_slice_in_dim(small_g['ffn_conv_w'], chip * (F // 2), F // 2, axis=2)
    small_g['od_norm_g'] = lax.dynamic_slice_in_dim(small_g['od_norm_g'], chip * (DC // 4), DC // 4, axis=1)

    grads, delta, new_m, new_v = {}, {}, {}, {}
    for n in _BIG:
        grads[n], delta[n], new_m[n], new_v[n] = adamw(w[n], big_g[n], mom[n], vel[n], "adamw_" + n)
    ps = [_pack([d[n] for n in _SMALL]) for d in (w, small_g, mom, vel)]
    upd = adamw(*[p[None] for p in ps], "adamw_small")
    shapes = [w[n].shape for n in _SMALL]
    for d, buf in zip((grads, delta, new_m, new_v), upd):
        d.update(zip(_SMALL, _unpack(buf[0], shapes)))

    return (loss, grad_x[None], *[grads[n] for n in _WEIGHTS], *[delta[n] for n in _WEIGHTS],
            *[new_m[n] for n in _WEIGHTS], *[new_v[n] for n in _WEIGHTS])
```

```python
import math

import jax
import jax.numpy as jnp
from jax import lax
from jax.experimental import pallas as pl
from jax.experimental.pallas import tpu as pltpu

F32 = jnp.float32
BF16 = jnp.bfloat16
MESH = pl.DeviceIdType.MESH
ANY = pl.BlockSpec(memory_space=pl.ANY)
VMEM_SPEC = pl.BlockSpec(memory_space=pltpu.VMEM)

DEPTH = 2
ALPHA = (2 * DEPTH) ** 0.25
LN_EPS = 1e-5
A_HEAD = 128
A_CHUNK = 128
B_WINDOWS = (2, 4, 8, 16)
POOL_HALO = 16
C_HEAD = 128
C_CHUNK = 64
CONV_HALO = 8
ADAM_LR = 0.001
ADAM_B1 = 0.9
ADAM_B2 = 0.999
ADAM_EPS = 1e-08
ADAM_WD = 0.01
ADAM_STEP = 10
V7X_VMEM_LIMIT_BYTES = 56 * 1024 * 1024
LANES = 128
PACK_ALIGN = 8 * LANES


class IciCopy:
    def __init__(self, kind, arr, rows=None):
        self.kind, self.arr, self.out = kind, arr, None
        self.rows = rows


def _carried_copies(items, in_refs, out_refs, send, recv):
    x, y, c, chips = _mesh_pos()
    me = 2 * x + y
    sends, lands = [], []
    for q, (it, src, dst) in enumerate(zip(items, in_refs, out_refs)):
        rows = pl.ds(*(it.rows or (0, it.arr.shape[-2])))
        for k, (cx, cy) in enumerate(chips):
            if it.kind == "gather":
                mine, theirs = dst.at[0, me, c, rows], dst.at[0, 2 * cx + cy, c, rows]
                sends.append(_remote(mine, mine, send, recv, 3 * q + k, (cx, cy, c)))
            else:
                theirs = dst.at[k]
                sends.append(_remote(src.at[k], theirs, send, recv, 3 * q + k, (cx, cy, c)))
            lands.append(_remote(theirs, theirs, send, recv, 3 * q + k, (x, y, c)))
    return sends, lands


def _call(body, *, name, out_shape, grid=(), in_specs=None, out_specs=None, scratch=(), prefetch=0, aliases=None,
          carry=()):
    single = not isinstance(out_specs, (list, tuple))
    in_specs = list(in_specs)
    out_specs = [out_specs] if single else list(out_specs)
    out_shape = [out_shape] if single else list(out_shape)
    scratch = list(scratch)
    aliases = dict(aliases or {})
    n_in, n_out, n_sc, n_c = len(in_specs), len(out_specs), len(scratch), len(carry)
    inner = body
    if n_c:
        assert grid, "a carrier needs a grid"
        for q, it in enumerate(carry):
            if it.kind == "gather":
                aliases[prefetch + n_in + q] = n_out + q
        in_specs += [ANY] * n_c
        out_specs += [ANY] * n_c
        out_shape += [jax.ShapeDtypeStruct(it.arr.shape, it.arr.dtype) for it in carry]
        scratch += [pltpu.SemaphoreType.DMA((3 * n_c,)), pltpu.SemaphoreType.DMA((3 * n_c,))]

        def inner(*refs):
            pre, refs = refs[:prefetch], refs[prefetch:]
            ins, c_in = refs[:n_in], refs[n_in:n_in + n_c]
            outs, c_out = refs[n_in + n_c:n_in + n_c + n_out], refs[n_in + n_c + n_out:n_in + 2 * n_c + n_out]
            rest = refs[n_in + 2 * n_c + n_out:]
            first = last = True
            for d, n in enumerate(grid):
                first = jnp.logical_and(first, pl.program_id(d) == 0)
                last = jnp.logical_and(last, pl.program_id(d) == n - 1)

            @pl.when(first)
            def _():
                for cp in _carried_copies(carry, c_in, c_out, rest[-2], rest[-1])[0]:
                    cp.start()

            body(*pre, *ins, *outs, *rest[:n_sc])

            @pl.when(last)
            def _():
                sends, lands = _carried_copies(carry, c_in, c_out, rest[-2], rest[-1])
                for cp in lands:
                    cp.wait_recv()
                for cp in sends:
                    cp.wait_send()

    spec = pltpu.PrefetchScalarGridSpec(num_scalar_prefetch=prefetch, grid=grid, in_specs=in_specs,
                                        out_specs=out_specs, scratch_shapes=scratch)
    fn = pl.pallas_call(inner, name=name, grid_spec=spec, out_shape=out_shape, input_output_aliases=aliases,
                        compiler_params=pltpu.CompilerParams(vmem_limit_bytes=V7X_VMEM_LIMIT_BYTES))

    def run(*args):
        res = fn(*args, *[it.arr for it in carry])
        for it, o in zip(carry, res[n_out:]):
            it.out = o
        return res[0] if single else list(res[:n_out])

    return run


def _tile(n, pref, unit=LANES):
    if n <= pref:
        return n
    t = (pref // unit) * unit
    while t > unit and n % t:
        t -= unit
    assert n % t == 0, (n, pref, unit)
    return t


def _slabs(n, rows=128):
    return [slice(r, min(r + rows, n)) for r in range(0, n, rows)]


def _dot(a, b):
    return jnp.dot(a, b, preferred_element_type=F32)


def _dot_nt(a, b):
    return lax.dot_general(a, b, (((1,), (1,)), ((), ())), preferred_element_type=F32)


def _dot_tn(a, b):
    return lax.dot_general(a, b, (((0,), (0,)), ((), ())), preferred_element_type=F32)


def _sigmoid(x):
    return jax.nn.sigmoid(x)


_GELU_C = math.sqrt(2.0 / math.pi)


def _gelu(x):
    return 0.5 * x * (1.0 + jnp.tanh(_GELU_C * (x + 0.044715 * x * x * x)))


def _gelu_grad(x):
    th = jnp.tanh(_GELU_C * (x + 0.044715 * x * x * x))
    return 0.5 * (1.0 + th) + 0.5 * x * (1.0 - th * th) * _GELU_C * (1.0 + 3.0 * 0.044715 * x * x)


def _ln_fwd(r, g, b):
    mu = jnp.mean(r, axis=-1, keepdims=True)
    xc = r - mu
    var = jnp.mean(xc * xc, axis=-1, keepdims=True)
    rstd = lax.rsqrt(var + LN_EPS)
    xh = xc * rstd
    return xh * g + b, xh, rstd


def _ln_bwd(dy, xh, rstd, g):
    dxh = dy * g
    m1 = jnp.mean(dxh, axis=-1, keepdims=True)
    m2 = jnp.mean(dxh * xh, axis=-1, keepdims=True)
    dr = rstd * (dxh - m1 - xh * m2)
    return dr, jnp.sum(dy * xh, axis=0, keepdims=True), jnp.sum(dy, axis=0, keepdims=True)


def _exact_tri_dot(tri, x):
    hi = x.astype(BF16)
    r1 = x - hi.astype(F32)
    mid = r1.astype(BF16)
    lo = (r1 - mid.astype(F32)).astype(BF16)
    return _dot(tri, hi) + _dot(tri, mid) + _dot(tri, lo)


def cast_bf16(a3, name):
    L, R, C = a3.shape
    br = _tile(R, max(8, (1 << 20) // C), 8)

    def body(a_ref, o_ref):
        o_ref[...] = a_ref[...].astype(BF16)

    return _call(body, name=name, grid=(L, R // br),
                 in_specs=[pl.BlockSpec((None, br, C), lambda l, r: (l, r, 0))],
                 out_specs=pl.BlockSpec((None, br, C), lambda l, r: (l, r, 0)),
                 out_shape=jax.ShapeDtypeStruct((L, R, C), BF16))(a3)


def adamw(w, g, m, v, name):
    L, R, C = w.shape
    br = _tile(R, max(8, (1 << 19) // C), 8)

    def body(w_ref, g_ref, m_ref, v_ref, go_ref, d_ref, nm_ref, nv_ref):
        _adamw_update(w_ref, g_ref, m_ref, v_ref, go_ref, d_ref, nm_ref, nv_ref)

    spec = pl.BlockSpec((None, br, C), lambda l, r: (l, r, 0))
    sds = jax.ShapeDtypeStruct((L, R, C), F32)
    return _call(body, name=name, grid=(L, R // br), in_specs=[spec] * 4, out_specs=[spec] * 4,
                 out_shape=[sds] * 4)(w, g, m, v)


def _adamw_update(w_ref, g_ref, m_ref, v_ref, go_ref, d_ref, nm_ref, nv_ref):
    gg = g_ref[...]
    nm = ADAM_B1 * m_ref[...] + (1.0 - ADAM_B1) * gg
    nv = ADAM_B2 * v_ref[...] + (1.0 - ADAM_B2) * (gg * gg)
    go_ref[...] = gg
    d_ref[...] = -ADAM_LR * ((nm / (1.0 - ADAM_B1 ** ADAM_STEP))
                             / (jnp.sqrt(nv / (1.0 - ADAM_B2 ** ADAM_STEP)) + ADAM_EPS) + ADAM_WD * w_ref[...])
    nm_ref[...] = nm
    nv_ref[...] = nv


def adamw_small(ws, gs, ms, vs, name):
    n = len(ws)

    def body(*refs):
        for t in range(n):
            _adamw_update(*[refs[k * n + t] for k in range(8)])

    args = [a for group in (ws, gs, ms, vs) for a in group]
    out = _call(body, name=name, in_specs=[VMEM_SPEC] * (4 * n), out_specs=[VMEM_SPEC] * (4 * n),
                out_shape=[jax.ShapeDtypeStruct(a.shape, F32) for a in args])(*args)
    return [[out[k * n + t] for t in range(n)] for k in range(4)]


def mm_nn(a, w, *, w_spec, P, tnw, tm, n_j, name, carry=()):
    T, K = a.shape
    bw = P * tnw

    def body(a_ref, w_ref, o_ref):
        av = a_ref[...]
        for p in range(P):
            o_ref[:, p * tnw:(p + 1) * tnw] = _dot(av, w_ref[p]).astype(BF16)

    return _call(body, name=name, grid=(T // tm, n_j),
                 in_specs=[pl.BlockSpec((tm, K), lambda i, j: (i, 0)), w_spec],
                 out_specs=pl.BlockSpec((tm, bw), lambda i, j: (i, j)),
                 out_shape=jax.ShapeDtypeStruct((T, n_j * bw), BF16), carry=carry)(a, w)


def mm_ln(a, w, res, rg, rb, g, b, *, w_spec, K, tk, tm, name, carry=()):
    T, N = res.shape
    n_k = K // tk

    def body(a_ref, w_ref, res_ref, rg_ref, rb_ref, g_ref, b_ref, xh_ref, y_ref, rs_ref, acc):
        k = pl.program_id(1)

        @pl.when(k == 0)
        def _():
            acc[...] = jnp.zeros_like(acc)

        acc[...] += _dot(a_ref[...], w_ref[...])

        @pl.when(k == n_k - 1)
        def _():
            for rows in _slabs(tm):
                r = ALPHA * (res_ref[rows, :] * rg_ref[...] + rb_ref[...]) + acc[rows, :]
                y, xh, rstd = _ln_fwd(r, g_ref[...], b_ref[...])
                xh_ref[rows, :] = xh
                y_ref[rows, :] = y.astype(BF16)
                rs_ref[rows, :] = rstd

    row = pl.BlockSpec((tm, N), lambda i, k: (i, 0))
    vec = pl.BlockSpec((1, N), lambda i, k: (0, 0))
    return _call(body, name=name, grid=(T // tm, n_k),
                 in_specs=[pl.BlockSpec((tm, tk), lambda i, k: (i, k)), w_spec, row, vec, vec, vec, vec],
                 out_specs=[row, row, pl.BlockSpec((tm, 1), lambda i, k: (i, 0))],
                 out_shape=[jax.ShapeDtypeStruct((T, N), F32), jax.ShapeDtypeStruct((T, N), BF16),
                            jax.ShapeDtypeStruct((T, 1), F32)],
                 scratch=[pltpu.VMEM((tm, N), F32)], carry=carry)(a, w, res, rg, rb, g, b)


def mm_nt_plain(a, w, *, tm, tn, name):
    T, K = a.shape
    N = w.shape[0]

    def body(a_ref, w_ref, o_ref):
        o_ref[...] = _dot_nt(a_ref[...], w_ref[...]).astype(BF16)

    return _call(body, name=name, grid=(T // tm, N // tn),
                 in_specs=[pl.BlockSpec((tm, K), lambda i, j: (i, 0)), pl.BlockSpec((tn, K), lambda i, j: (j, 0))],
                 out_specs=pl.BlockSpec((tm, tn), lambda i, j: (i, j)),
                 out_shape=jax.ShapeDtypeStruct((T, N), BF16))(a, w)


def mm_nt_res(a, w, res, ln, *, a_spec, w_spec, P, tnw, n_k, tm, name, carry=()):
    T, N = res.shape
    n_i = T // tm

    def body(*refs):
        if ln is None:
            a_ref, w_ref, res_ref, o_ref, acc = refs
        else:
            a_ref, w_ref, res_ref, xh_ref, rs_ref, g_ref, dr_ref, drb_ref, dg_ref, db_ref, acc = refs
        i = pl.program_id(0)
        k = pl.program_id(1)

        @pl.when(k == 0)
        def _():
            acc[...] = jnp.zeros_like(acc)

        wv = w_ref[0] if P == 1 else jnp.concatenate([w_ref[p] for p in range(P)], axis=1)
        acc[...] += _dot_nt(a_ref[...], wv)

        @pl.when(k == n_k - 1)
        def _():
            if ln is not None:
                @pl.when(i == 0)
                def _():
                    dg_ref[...] = jnp.zeros_like(dg_ref)
                    db_ref[...] = jnp.zeros_like(db_ref)

            for rows in _slabs(tm):
                d = ALPHA * res_ref[rows, :] + acc[rows, :]
                if ln is None:
                    o_ref[rows, :] = d
                else:
                    dr, dg, db = _ln_bwd(d, xh_ref[rows, :], rs_ref[rows, :], g_ref[...])
                    dr_ref[rows, :] = dr
                    drb_ref[rows, :] = dr.astype(BF16)
                    dg_ref[...] += dg
                    db_ref[...] += db

    row = pl.BlockSpec((tm, N), lambda i, k: (i, 0))
    vec = pl.BlockSpec((1, N), lambda i, k: (0, 0))
    scratch = [pltpu.VMEM((tm, N), F32)]
    if ln is None:
        return _call(body, name=name, grid=(n_i, n_k), in_specs=[a_spec, w_spec, row], out_specs=row,
                     out_shape=jax.ShapeDtypeStruct((T, N), F32), scratch=scratch, carry=carry)(a, w, res)
    xh, rstd, g = ln
    return _call(body, name=name, grid=(n_i, n_k),
                 in_specs=[a_spec, w_spec, row, row, pl.BlockSpec((tm, 1), lambda i, k: (i, 0)), vec],
                 out_specs=[row, row, vec, vec],
                 out_shape=[jax.ShapeDtypeStruct((T, N), F32), jax.ShapeDtypeStruct((T, N), BF16),
                            jax.ShapeDtypeStruct((1, N), F32), jax.ShapeDtypeStruct((1, N), F32)],
                 scratch=scratch, carry=carry)(a, w, res, xh, rstd, g)


def mm_tn(a, g, *, a_spec, g_spec, o_spec, out_shape, grid, acc_shape, P, tnw, name):
    n_t = grid[2]

    def body(a_ref, g_ref, o_ref, acc):
        t = pl.program_id(2)

        @pl.when(t == 0)
        def _():
            acc[...] = jnp.zeros_like(acc)

        acc[...] += _dot_tn(a_ref[...], g_ref[...])

        @pl.when(t == n_t - 1)
        def _():
            for p in range(P):
                o_ref[p] = acc[:, p * tnw:(p + 1) * tnw].astype(BF16)

    return _call(body, name=name, grid=grid, in_specs=[a_spec, g_spec], out_specs=o_spec,
                 out_shape=jax.ShapeDtypeStruct(out_shape, BF16),
                 scratch=[pltpu.VMEM(acc_shape, F32)])(a, g)


def _causal_conv(ext, halo, w, b):
    s1 = pltpu.roll(ext, 1, 0)[halo:]
    s2 = pltpu.roll(ext, 2, 0)[halo:]
    return b + w[2:3] * ext[halo:] + w[1:2] * s1 + w[0:1] * s2, s1, s2


def ffn_up(xb, wup, cw, cb, l, *, name, carry=()):
    T, D = xb.shape
    Ns = wup.shape[-1]
    F = 2 * Ns
    tn = _tile(Ns, 256)
    nps = Ns // tn
    n_j = F // tn
    tm = _tile(T, 1024, 8)

    def body(x_ref, wa_ref, wv_ref, cwa_ref, cwv_ref, cba_ref, cbv_ref, h_ref, hc_ref, f_ref, carry):
        i = pl.program_id(1)

        @pl.when(i == 0)
        def _():
            carry[...] = jnp.zeros_like(carry)

        xv = x_ref[...]
        ha = _dot(xv, wa_ref[...])
        hv = _dot(xv, wv_ref[...])
        ca, _, _ = _causal_conv(jnp.concatenate([carry[0], ha], axis=0), CONV_HALO, cwa_ref[...], cba_ref[...])
        cv, _, _ = _causal_conv(jnp.concatenate([carry[1], hv], axis=0), CONV_HALO, cwv_ref[...], cbv_ref[...])
        carry[0] = ha[tm - CONV_HALO:]
        carry[1] = hv[tm - CONV_HALO:]
        h_ref[0] = ha.astype(BF16)
        h_ref[1] = hv.astype(BF16)
        hc_ref[0] = ca.astype(BF16)
        hc_ref[1] = cv.astype(BF16)
        f_ref[...] = (ca * _sigmoid(ca) * cv).astype(BF16)

    wspec_a = pl.BlockSpec((None, None, D, tn), lambda j, i: (0, j // nps, 0, j % nps))
    wspec_v = pl.BlockSpec((None, None, D, tn), lambda j, i: (0, 2 + j // nps, 0, j % nps))
    pair_tile = pl.BlockSpec((2, tm, tn), lambda j, i: (0, i, j))
    return _call(
        body, name=name, grid=(n_j, T // tm),
        in_specs=[pl.BlockSpec((tm, D), lambda j, i: (i, 0)), wspec_a, wspec_v,
                  pl.BlockSpec((None, 3, tn), lambda j, i: (l, 0, j)),
                  pl.BlockSpec((None, 3, tn), lambda j, i: (l, 0, n_j + j)),
                  pl.BlockSpec((None, 1, tn), lambda j, i: (l, 0, j)),
                  pl.BlockSpec((None, 1, tn), lambda j, i: (l, 0, n_j + j))],
        out_specs=[pair_tile, pair_tile, pl.BlockSpec((tm, tn), lambda j, i: (i, j))],
        out_shape=[jax.ShapeDtypeStruct((2, T, F), BF16), jax.ShapeDtypeStruct((2, T, F), BF16),
                   jax.ShapeDtypeStruct((T, F), BF16)],
        scratch=[pltpu.VMEM((2, CONV_HALO, tn), F32)], carry=carry)(xb, wup, wup, cw, cw, cb, cb)


def ffn_dgate(db16, wdn, h, hc, cw, l, *, name, carry=()):
    T, D = db16.shape
    F = h.shape[-1]
    tn = _tile(F, 512)
    n_j = F // tn
    tm = _tile(T, 512, 16)
    n_i = T // tm
    n_ext = tm + CONV_HALO

    def body(d_ref, w_ref, h_ref, hc_ref, cwa_ref, cwv_ref, dh_ref, dcw_ref, dcb_ref, carry):
        ip = pl.program_id(1)

        @pl.when(ip == 0)
        def _():
            carry[...] = jnp.zeros_like(carry)
            dcw_ref[...] = jnp.zeros_like(dcw_ref)
            dcb_ref[...] = jnp.zeros_like(dcb_ref)

        df = _dot_nt(d_ref[...], w_ref[...])
        ca = hc_ref[0].astype(F32)
        cv = hc_ref[1].astype(F32)
        sig = _sigmoid(ca)
        sil = ca * sig
        da = df * cv * (sig + sil * (1.0 - sig))
        dv = df * sil
        for half, (dc, w_ref_h) in enumerate(((da, cwa_ref), (dv, cwv_ref))):
            w = w_ref_h[...]
            h0 = h_ref[half].astype(F32)
            ext = jnp.concatenate([dc, carry[half]], axis=0)
            n1 = pltpu.roll(ext, n_ext - 1, 0)[:tm]
            n2 = pltpu.roll(ext, n_ext - 2, 0)[:tm]
            dcb_ref[half] += jnp.sum(dc, axis=0, keepdims=True)
            dcw_ref[half] += jnp.concatenate(
                [jnp.sum(n2 * h0, axis=0, keepdims=True), jnp.sum(n1 * h0, axis=0, keepdims=True),
                 jnp.sum(dc * h0, axis=0, keepdims=True)], axis=0)
            dh_ref[half] = (w[2:3] * dc + w[1:2] * n1 + w[0:1] * n2).astype(BF16)
            carry[half] = dc[:CONV_HALO]

    rev = lambda ip: n_i - 1 - ip
    tile = pl.BlockSpec((2, tm, tn), lambda j, ip: (0, rev(ip), j))
    return _call(
        body, name=name, grid=(n_j, n_i),
        in_specs=[pl.BlockSpec((tm, D), lambda j, ip: (rev(ip), 0)),
                  pl.BlockSpec((None, tn, D), lambda j, ip: (0, j, 0)), tile, tile,
                  pl.BlockSpec((None, 3, tn), lambda j, ip: (l, 0, j)),
                  pl.BlockSpec((None, 3, tn), lambda j, ip: (l, 0, n_j + j))],
        out_specs=[tile, pl.BlockSpec((2, 3, tn), lambda j, ip: (0, 0, j)),
                   pl.BlockSpec((2, 1, tn), lambda j, ip: (0, 0, j))],
        out_shape=[jax.ShapeDtypeStruct((2, T, F), BF16), jax.ShapeDtypeStruct((2, 3, F), F32),
                   jax.ShapeDtypeStruct((2, 1, F), F32)],
        scratch=[pltpu.VMEM((2, CONV_HALO, tn), F32)], carry=carry)(db16, wdn, h, hc, cw, cw)


def _pool_fwd(ext, xb_g, t_glob, win):
    e = ext
    sft = 1
    while sft < win:
        e = e + pltpu.roll(e, sft, 0)
        sft *= 2
    cnt = jnp.minimum(t_glob + 1.0, float(win))
    return e[POOL_HALO:] / cnt - xb_g


def gating_fwd(h0, lg, lb, ws, bsT, wp, sc, *, name, carry=()):
    T = h0.shape[0]
    DA = lg.shape[-1]
    DB = sc.shape[-1]
    HA = DA // A_HEAD
    G = len(B_WINDOWS)
    CG = DB // G
    tm = _tile(T, 512, A_CHUNK)
    n_c = tm // A_CHUNK

    def body(h_ref, halo_ref, lg_ref, lb_ref, ws_ref, bsT_ref, wp_ref, sc_ref, cat_ref):
        i = pl.program_id(0)
        hu = h_ref[:, 0:DA].astype(F32)
        hv = h_ref[:, DA:2 * DA].astype(F32)
        xb = h_ref[:, 2 * DA:].astype(F32)
        u = _gelu(hu)
        vn, _, _ = _ln_fwd(_gelu(hv), lg_ref[...], lb_ref[...])
        vnb = vn.astype(BF16)
        rr = lax.broadcasted_iota(jnp.int32, (A_CHUNK, A_CHUNK), 0)
        cc = lax.broadcasted_iota(jnp.int32, (A_CHUNK, A_CHUNK), 1)
        for hh in range(HA):
            wt = jnp.where(rr >= cc, ws_ref[hh], 0.0).astype(BF16)
            cs = slice(hh * A_HEAD, (hh + 1) * A_HEAD)
            for n in range(n_c):
                rs = slice(n * A_CHUNK, (n + 1) * A_CHUNK)
                s = _dot(wt, vnb[rs, cs]) + bsT_ref[:, hh:hh + 1]
                cat_ref[rs, cs] = (u[rs, cs] * s).astype(BF16)
        halo = jnp.where(i > 0, halo_ref[...].astype(F32), 0.0)
        ext = jnp.concatenate([halo, xb], axis=0)
        t_glob = (i * tm + lax.broadcasted_iota(jnp.int32, (tm, 1), 0)).astype(F32)
        for g, win in enumerate(B_WINDOWS):
            gs = slice(g * CG, (g + 1) * CG)
            p = _pool_fwd(ext[:, gs], xb[:, gs], t_glob, win)
            z = _dot(p.astype(BF16), wp_ref[g])
            cat_ref[:, DA + g * CG:DA + (g + 1) * CG] = (z * sc_ref[:, gs]).astype(BF16)

    full = lambda a: pl.BlockSpec(a.shape, lambda i: (0,) * a.ndim)
    hpb = tm // POOL_HALO
    return _call(
        body, name=name, grid=(T // tm,),
        in_specs=[pl.BlockSpec((tm, 2 * DA + DB), lambda i: (i, 0)),
                  pl.BlockSpec((POOL_HALO, DB), lambda i: (jnp.maximum(i * hpb - 1, 0), 2 * DA // DB)),
                  full(lg), full(lb), full(ws), full(bsT), full(wp), full(sc)],
        out_specs=pl.BlockSpec((tm, DA + DB), lambda i: (i, 0)),
        out_shape=jax.ShapeDtypeStruct((T, DA + DB), BF16), carry=carry)(h0, h0, lg, lb, ws, bsT, wp, sc)


def gating_bwd(h0, dcat, lg, lb, ws, wsT, bsT, wp, sc, *, name, carry=()):
    T = h0.shape[0]
    DA = lg.shape[-1]
    DB = sc.shape[-1]
    HA = DA // A_HEAD
    G = len(B_WINDOWS)
    CG = DB // G
    tm = _tile(T, 512, A_CHUNK)
    n_i = T // tm
    n_c = tm // A_CHUNK
    n_ext = tm + POOL_HALO

    def body(h_ref, halo_ref, dc_ref, dhalo_ref, lg_ref, lb_ref, ws_ref, wsT_ref, bsT_ref, wp_ref, sc_ref,
             dh_ref, dws_ref, dbsT_ref, dlg_ref, dlb_ref, dsc_ref, dwp_ref, dvn_sc):
        i = pl.program_id(0)

        @pl.when(i == 0)
        def _():
            for r in (dws_ref, dbsT_ref, dlg_ref, dlb_ref, dsc_ref, dwp_ref):
                r[...] = jnp.zeros_like(r)

        hu = h_ref[:, 0:DA].astype(F32)
        hv = h_ref[:, DA:2 * DA].astype(F32)
        xb = h_ref[:, 2 * DA:].astype(F32)
        u = _gelu(hu)
        gu = _gelu_grad(hu)
        lgv = lg_ref[...]
        vn, vhat, rstd = _ln_fwd(_gelu(hv), lgv, lb_ref[...])
        vnb = vn.astype(BF16)
        rr = lax.broadcasted_iota(jnp.int32, (A_CHUNK, A_CHUNK), 0)
        cc = lax.broadcasted_iota(jnp.int32, (A_CHUNK, A_CHUNK), 1)
        for hh in range(HA):
            wt = jnp.where(rr >= cc, ws_ref[hh], 0.0).astype(BF16)
            wtT = jnp.where(rr <= cc, wsT_ref[hh], 0.0).astype(BF16)
            cs = slice(hh * A_HEAD, (hh + 1) * A_HEAD)
            dws = jnp.zeros((A_CHUNK, A_CHUNK), F32)
            dbs = jnp.zeros((A_CHUNK, 1), F32)
            for n in range(n_c):
                rs = slice(n * A_CHUNK, (n + 1) * A_CHUNK)
                vb = vnb[rs, cs]
                s = _dot(wt, vb) + bsT_ref[:, hh:hh + 1]
                dya = dc_ref[rs, cs].astype(F32)
                ds = dya * u[rs, cs]
                dh_ref[rs, cs] = (dya * s * gu[rs, cs]).astype(BF16)
                dsb = ds.astype(BF16)
                dbs = dbs + jnp.sum(ds, axis=1, keepdims=True)
                dws = dws + _dot_nt(dsb, vb)
                dvn_sc[rs, cs] = _dot(wtT, dsb)
            dws_ref[hh] += jnp.where(rr >= cc, dws, 0.0)
            dbsT_ref[:, hh:hh + 1] += dbs
        dvg, dlg, dlb = _ln_bwd(dvn_sc[...], vhat, rstd, lgv)
        dlg_ref[...] += dlg
        dlb_ref[...] += dlb
        dh_ref[:, DA:2 * DA] = (dvg * _gelu_grad(hv)).astype(BF16)

        halo = jnp.where(i > 0, halo_ref[...].astype(F32), 0.0)
        ext = jnp.concatenate([halo, xb], axis=0)
        t_glob = (i * tm + lax.broadcasted_iota(jnp.int32, (tm, 1), 0)).astype(F32)
        t_ext = (i * tm + lax.broadcasted_iota(jnp.int32, (n_ext, 1), 0)).astype(F32)
        dyb = dc_ref[:, DA:].astype(F32)
        dhalo = jnp.where(i < n_i - 1, dhalo_ref[...].astype(F32), 0.0)
        dyb_ext = jnp.concatenate([dyb, dhalo], axis=0)
        for g, win in enumerate(B_WINDOWS):
            gs = slice(g * CG, (g + 1) * CG)
            pb = _pool_fwd(ext[:, gs], xb[:, gs], t_glob, win).astype(BF16)
            wpg = wp_ref[g]
            z = _dot(pb, wpg)
            dsc_ref[:, gs] += jnp.sum(dyb[:, gs] * z, axis=0, keepdims=True)
            dzb = (dyb_ext[:, gs] * sc_ref[:, gs]).astype(BF16)
            dwp_ref[g] += _dot_tn(pb, dzb[:tm])
            dp = _dot_nt(dzb, wpg)
            e = dp / jnp.minimum(t_ext + 1.0, float(win))
            sft = 1
            while sft < win:
                e = e + pltpu.roll(e, n_ext - sft, 0)
                sft *= 2
            dh_ref[:, 2 * DA + g * CG:2 * DA + (g + 1) * CG] = (e[:tm] - dp[:tm]).astype(BF16)

    full = lambda a: pl.BlockSpec(a.shape, lambda i: (0,) * a.ndim)
    hpb = tm // POOL_HALO
    n_hb = T // POOL_HALO
    outs = [jax.ShapeDtypeStruct((T, 2 * DA + DB), BF16), jax.ShapeDtypeStruct(ws.shape, F32),
            jax.ShapeDtypeStruct(bsT.shape, F32), jax.ShapeDtypeStruct(lg.shape, F32),
            jax.ShapeDtypeStruct(lb.shape, F32), jax.ShapeDtypeStruct(sc.shape, F32),
            jax.ShapeDtypeStruct(wp.shape, F32)]
    return _call(
        body, name=name, grid=(n_i,),
        in_specs=[pl.BlockSpec((tm, 2 * DA + DB), lambda i: (i, 0)),
                  pl.BlockSpec((POOL_HALO, DB), lambda i: (jnp.maximum(i * hpb - 1, 0), 2 * DA // DB)),
                  pl.BlockSpec((tm, DA + DB), lambda i: (i, 0)),
                  pl.BlockSpec((POOL_HALO, DB), lambda i: (jnp.minimum((i + 1) * hpb, n_hb - 1), DA // DB)),
                  full(lg), full(lb), full(ws), full(wsT), full(bsT), full(wp), full(sc)],
        out_specs=[pl.BlockSpec((tm, 2 * DA + DB), lambda i: (i, 0))] + [full(o) for o in outs[1:]],
        out_shape=outs,
        scratch=[pltpu.VMEM((tm, DA), F32)], carry=carry)(h0, h0, dcat, dcat, lg, lb, ws, wsT, bsT, wp, sc)


HP = 2 * C_HEAD


def _hgrn_gates(h_ref, rows, lbv, tri):
    pre = []
    for r in rows:
        blk = h_ref[r, :].astype(F32)
        q = blk[:, 0:HP]
        sg = _sigmoid(blk[:, HP:2 * HP])
        f = lbv + (1.0 - lbv) * sg
        pre.append(dict(q=q, sq=_sigmoid(q), sg=sg, f=f, k=1.0 - f, lf=jnp.log(f), v=blk[:, 2 * HP:3 * HP],
                        gg=blk[:, 3 * HP:4 * HP]))
    bcums = [_exact_tri_dot(tri, p["lf"]) for p in pre]
    out = []
    for p, bcum in zip(pre, bcums):
        blast = bcum[C_CHUNK - 1:C_CHUNK]
        e_in = jnp.exp(bcum)
        e_out = jnp.exp(-bcum)
        e_end = jnp.exp(blast - bcum)
        out.append(dict(p, e_in=e_in, e_out=e_out, e_end=e_end, qd=p["q"] * p["sq"] * e_in, kd=p["k"] * e_out,
                        ke=p["k"] * e_end, dec=jnp.exp(blast)))
    return out


def _lower_bound(lbp_ref):
    p0, p1 = lbp_ref[0:1], lbp_ref[1:2]
    mx = jnp.maximum(p0, p1)
    e0, e1 = jnp.exp(p0 - mx), jnp.exp(p1 - mx)
    return e1 / (e0 + e1)


def hgrn_fwd(h1p, lbp, gn, *, name, carry=()):
    T = h1p.shape[0]
    DC = gn.shape[-1]
    n_p = DC // HP
    tt = _tile(T, 1024, C_CHUNK)
    n_c = tt // C_CHUNK

    def body(h_ref, lbp_ref, gn_ref, y_ref, o_ref, sp_ref, st):
        i = pl.program_id(1)

        @pl.when(i == 0)
        def _():
            st[...] = jnp.zeros_like(st)

        lbv = _lower_bound(lbp_ref)
        gnv = gn_ref[...]
        rr = lax.broadcasted_iota(jnp.int32, (C_CHUNK, C_CHUNK), 0)
        cc = lax.broadcasted_iota(jnp.int32, (C_CHUNK, C_CHUNK), 1)
        causal = rr >= cc
        tri = jnp.where(causal, 1.0, 0.0).astype(BF16)

        heads = [slice(hd * C_HEAD, (hd + 1) * C_HEAD) for hd in range(2)]
        rows = [slice(n * C_CHUNK, (n + 1) * C_CHUNK) for n in range(n_c)]
        nh = [(n, hd) for n in range(n_c) for hd in range(2)]
        gates = _hgrn_gates(h_ref, rows, lbv, tri)
        b16 = lambda key: {(n, hd): gates[n][key][:, heads[hd]].astype(BF16) for n, hd in nh}
        qd, kd, ke, vb = b16("qd"), b16("kd"), b16("ke"), b16("v")
        att = {i: jnp.where(causal, _dot_nt(qd[i], kd[i]), 0.0).astype(BF16) for i in nh}
        intra = {i: _dot(att[i], vb[i]) for i in nh}
        upd = {i: _dot_tn(vb[i], ke[i]) for i in nh}
        s = [st[0], st[1]]
        entering = {}
        for n, hd in nh:
            entering[n, hd] = s[hd]
            sp_ref[hd, n] = s[hd]
            s[hd] = gates[n]["dec"][:, heads[hd]] * s[hd] + upd[n, hd]
        st[0], st[1] = s
        o = {i: intra[i] + _dot_nt(qd[i], entering[i].astype(BF16)) for i in nh}
        for n in range(n_c):
            os_ = [o[n, 0], o[n, 1]]
            o_ref[rows[n], :] = jnp.concatenate(os_, axis=1).astype(BF16)
            ys = [oh * lax.rsqrt(jnp.mean(oh * oh, axis=-1, keepdims=True) + LN_EPS) for oh in os_]
            y_ref[rows[n], :] = (jnp.concatenate(ys, axis=1) * gnv * _sigmoid(gates[n]["gg"])).astype(BF16)

    return _call(
        body, name=name, grid=(n_p, T // tt),
        in_specs=[pl.BlockSpec((tt, 4 * HP), lambda p, i: (i, p)), pl.BlockSpec((2, HP), lambda p, i: (0, p)),
                  pl.BlockSpec((1, HP), lambda p, i: (0, p))],
        out_specs=[pl.BlockSpec((tt, HP), lambda p, i: (i, p)), pl.BlockSpec((tt, HP), lambda p, i: (i, p)),
                   pl.BlockSpec((2, n_c, C_HEAD, C_HEAD), lambda p, i: (p, i, 0, 0))],
        out_shape=[jax.ShapeDtypeStruct((T, DC), BF16), jax.ShapeDtypeStruct((T, DC), BF16),
                   jax.ShapeDtypeStruct((2 * n_p, T // C_CHUNK, C_HEAD, C_HEAD), F32)],
        scratch=[pltpu.VMEM((2, C_HEAD, C_HEAD), F32)], carry=carry)(h1p, lbp, gn)


def hgrn_bwd(h1p, o_saved, dy, sp, lbp, gn, *, name, carry=()):
    T = h1p.shape[0]
    DC = gn.shape[-1]
    n_p = DC // HP
    tt = _tile(T, 1024, C_CHUNK)
    n_i = T // tt
    n_c = tt // C_CHUNK

    def body(h_ref, o_ref, dy_ref, sp_ref, lbp_ref, gn_ref, dh_ref, dgn_ref, dlbp_ref, dst, dlb_acc):
        ip = pl.program_id(1)

        @pl.when(ip == 0)
        def _():
            dst[...] = jnp.zeros_like(dst)
            dlb_acc[...] = jnp.zeros_like(dlb_acc)
            dgn_ref[...] = jnp.zeros_like(dgn_ref)

        lbv = _lower_bound(lbp_ref)
        gnv = gn_ref[...]
        rr = lax.broadcasted_iota(jnp.int32, (C_CHUNK, C_CHUNK), 0)
        cc = lax.broadcasted_iota(jnp.int32, (C_CHUNK, C_CHUNK), 1)
        causal = rr >= cc
        tri = jnp.where(causal, 1.0, 0.0).astype(BF16)
        tri_t = jnp.where(rr <= cc, 1.0, 0.0).astype(BF16)
        last_row = lax.broadcasted_iota(jnp.int32, (C_CHUNK, 1), 0) == C_CHUNK - 1

        heads = [slice(hd * C_HEAD, (hd + 1) * C_HEAD) for hd in range(2)]
        rows = [slice(n * C_CHUNK, (n + 1) * C_CHUNK) for n in range(n_c)]
        cat = lambda parts: jnp.concatenate(parts, axis=1)
        nh = [(n, hd) for n in range(n_c) for hd in range(2)]
        pair = lambda d, n: cat([d[n, 0], d[n, 1]])
        gates = _hgrn_gates(h_ref, rows, lbv, tri)
        dgn = jnp.zeros((1, HP), F32)
        dgg, dob = [], {}
        for n in range(n_c):
            o = o_ref[rows[n], :].astype(F32)
            dyv = dy_ref[rows[n], :].astype(F32)
            sgg = _sigmoid(gates[n]["gg"])
            rrs = [lax.rsqrt(jnp.mean(o[:, cs] * o[:, cs], axis=-1, keepdims=True) + LN_EPS) for cs in heads]
            ohat = cat([o[:, cs] * r for cs, r in zip(heads, rrs)])
            dyn = dyv * sgg
            dgg.append(dyv * ohat * gnv * sgg * (1.0 - sgg))
            dgn = dgn + jnp.sum(dyn * ohat, axis=0, keepdims=True)
            dxh = dyn * gnv
            for hd, cs in enumerate(heads):
                dxh_h, oh_h = dxh[:, cs], ohat[:, cs]
                dob[n, hd] = (rrs[hd] * (dxh_h - oh_h * jnp.mean(dxh_h * oh_h, axis=-1, keepdims=True))).astype(BF16)
        dgn_ref[...] += dgn
        b16 = lambda key: {(n, hd): gates[n][key][:, heads[hd]].astype(BF16) for n, hd in nh}
        qd, kd, ke, vb = b16("qd"), b16("kd"), b16("ke"), b16("v")
        s_in = {(n, hd): sp_ref[hd, n] for n, hd in nh}
        att = {i: jnp.where(causal, _dot_nt(qd[i], kd[i]), 0.0).astype(BF16) for i in nh}
        datt = {i: jnp.where(causal, _dot_nt(dob[i], vb[i]), 0.0).astype(BF16) for i in nh}
        grow = {i: _dot_tn(dob[i], qd[i]) for i in nh}
        dv_i = {i: _dot_tn(att[i], dob[i]) for i in nh}
        dqd = {i: _dot(datt[i], kd[i]) + _dot(dob[i], s_in[i].astype(BF16)) for i in nh}
        dkd = {i: _dot_tn(datt[i], qd[i]) for i in nh}
        ds = [dst[0], dst[1]]
        leaving = {}
        for n in reversed(range(n_c)):
            for hd, cs in enumerate(heads):
                leaving[n, hd] = ds[hd]
                ds[hd] = gates[n]["dec"][:, cs] * ds[hd] + grow[n, hd]
        dst[0], dst[1] = ds
        dsb = {i: leaving[i].astype(BF16) for i in nh}
        dv = {i: dv_i[i] + _dot_nt(ke[i], dsb[i]) for i in nh}
        dke = {i: _dot(vb[i], dsb[i]) for i in nh}
        ddec = {i: jnp.sum(leaving[i] * s_in[i], axis=0, keepdims=True) for i in nh}
        mid = []
        for n in range(n_c):
            a = gates[n]
            dqd_n, dkd_n, dke_n = pair(dqd, n), pair(dkd, n), pair(dke, n)
            kek = dke_n * a["ke"]
            dblast = jnp.sum(kek, axis=0, keepdims=True) + pair(ddec, n) * a["dec"]
            dbcum = dqd_n * a["qd"] - dkd_n * a["kd"] - kek + jnp.where(last_row, dblast, 0.0)
            mid.append((dqd_n * a["e_in"], dkd_n * a["e_out"] + dke_n * a["e_end"], dbcum))
        dlf = [_exact_tri_dot(tri_t, m[2]) for m in mid]
        dlb = jnp.zeros((1, HP), F32)
        for n in range(n_c):
            a = gates[n]
            dqs, dk, _ = mid[n]
            df = dlf[n] / a["f"] - dk
            dlb = dlb + jnp.sum(df * (1.0 - a["sg"]), axis=0, keepdims=True)
            dfl = df * (1.0 - lbv) * a["sg"] * (1.0 - a["sg"])
            dq = dqs * a["sq"] * (1.0 + a["q"] * (1.0 - a["sq"]))
            dh_ref[rows[n], :] = cat([dq, dfl, pair(dv, n), dgg[n]]).astype(BF16)
        dlb_acc[...] += dlb

        @pl.when(ip == n_i - 1)
        def _():
            d1 = dlb_acc[...] * lbv * (1.0 - lbv)
            dlbp_ref[...] = jnp.concatenate([-d1, d1], axis=0)

    rev = lambda ip: n_i - 1 - ip
    return _call(
        body, name=name, grid=(n_p, n_i),
        in_specs=[pl.BlockSpec((tt, 4 * HP), lambda p, ip: (rev(ip), p)),
                  pl.BlockSpec((tt, HP), lambda p, ip: (rev(ip), p)),
                  pl.BlockSpec((tt, HP), lambda p, ip: (rev(ip), p)),
                  pl.BlockSpec((2, n_c, C_HEAD, C_HEAD), lambda p, ip: (p, rev(ip), 0, 0)),
                  pl.BlockSpec((2, HP), lambda p, ip: (0, p)), pl.BlockSpec((1, HP), lambda p, ip: (0, p))],
        out_specs=[pl.BlockSpec((tt, 4 * HP), lambda p, ip: (rev(ip), p)),
                   pl.BlockSpec((1, HP), lambda p, ip: (0, p)), pl.BlockSpec((2, HP), lambda p, ip: (0, p))],
        out_shape=[jax.ShapeDtypeStruct(h1p.shape, BF16), jax.ShapeDtypeStruct((1, DC), F32),
                   jax.ShapeDtypeStruct((2, DC), F32)],
        scratch=[pltpu.VMEM((2, C_HEAD, C_HEAD), F32), pltpu.VMEM((1, HP), F32)], carry=carry)(h1p, o_saved, dy, sp, lbp, gn)


def loss_bwd(xh, rstd, g, b, target, *, name):
    T, N = xh.shape
    tm = _tile(T, 512, 8)

    def body(xh_ref, rs_ref, g_ref, b_ref, t_ref, ls_ref, dr_ref, drb_ref, dg_ref, db_ref):
        i = pl.program_id(0)

        @pl.when(i == 0)
        def _():
            for r in (ls_ref, dg_ref, db_ref):
                r[...] = jnp.zeros_like(r)

        xhv, gv = xh_ref[...], g_ref[...]
        e = xhv * gv + b_ref[...] - t_ref[...]
        ls_ref[...] += 0.5 * jnp.sum(jnp.mean(e * e, axis=-1, keepdims=True), axis=0, keepdims=True)
        dr, dg, db = _ln_bwd(e / N, xhv, rs_ref[...], gv)
        dr_ref[...] = dr
        drb_ref[...] = dr.astype(BF16)
        dg_ref[...] += dg
        db_ref[...] += db

    row = pl.BlockSpec((tm, N), lambda i: (i, 0))
    vec = pl.BlockSpec((1, N), lambda i: (0, 0))
    return _call(body, name=name, grid=(T // tm,),
                 in_specs=[row, pl.BlockSpec((tm, 1), lambda i: (i, 0)), vec, vec, row],
                 out_specs=[pl.BlockSpec((1, LANES), lambda i: (0, 0)), row, row, vec, vec],
                 out_shape=[jax.ShapeDtypeStruct((1, LANES), F32), jax.ShapeDtypeStruct((T, N), F32),
                            jax.ShapeDtypeStruct((T, N), BF16), jax.ShapeDtypeStruct((1, N), F32),
                            jax.ShapeDtypeStruct((1, N), F32)])(xh, rstd, g, b, target)


def _mesh_pos():
    x, y, c = lax.axis_index("x"), lax.axis_index("y"), lax.axis_index("c")
    chips = [(1 - x, y), (x, 1 - y), (1 - x, 1 - y)]
    return x, y, c, chips


def _remote(src, dst, send, recv, j, dev):
    return pltpu.make_async_remote_copy(src_ref=src, dst_ref=dst, send_sem=send.at[j], recv_sem=recv.at[j],
                                        device_id=dev, device_id_type=MESH)


def mesh_ids():
    x, y, c, chips = _mesh_pos()
    return jnp.stack([c] + [2 * cx + cy for cx, cy in chips] + [2 * x + y]).astype(jnp.int32)


def _sibling():
    return (lax.axis_index("x"), lax.axis_index("y"), 1 - lax.axis_index("c"))


SWAP_PARTS = 2


def _swap_with_sibling(src, recv, send_sem, recv_sem, step):
    slot = step % 2
    br = src.shape[0]
    n = SWAP_PARTS if br % (16 * SWAP_PARTS) == 0 else 1
    parts = []
    for k in range(n):
        rows = pl.ds(k * (br // n), br // n)
        cp = pltpu.make_async_remote_copy(
            src_ref=src.at[rows], dst_ref=recv.at[slot, rows], send_sem=send_sem.at[slot * SWAP_PARTS + k],
            recv_sem=recv_sem.at[slot * SWAP_PARTS + k], device_id=_sibling(), device_id_type=MESH)
        cp.start()
        parts.append((cp, slot, rows))
    return parts


def _swap_scratch(br, C, dtype):
    return [pltpu.VMEM((2, br, C), dtype), pltpu.SemaphoreType.DMA((2 * SWAP_PARTS,)),
            pltpu.SemaphoreType.DMA((2 * SWAP_PARTS,))]


def _swap_rows(Rh, C, itemsize):
    return _tile(Rh, max(16, (3 << 20) // (C * itemsize)), 16)


def cast_to_slot(a3, l, me1, name):
    _, R, C = a3.shape
    br = _tile(R, max(8, (1 << 20) // C), 16)

    def body(me_ref, a_ref, o_ref):
        o_ref[...] = a_ref[...].astype(BF16)

    return _call(body, name=name, grid=(R // br,), prefetch=1,
                 in_specs=[pl.BlockSpec((None, br, C), lambda r, me: (l, r, 0))],
                 out_specs=pl.BlockSpec((None, None, br, C), lambda r, me: (0, me[0], r, 0)),
                 out_shape=jax.ShapeDtypeStruct((1, 4, R, C), BF16))(me1, a3)


def all_gather_chips(big, small, *, name):
    nb, ns = len(big), len(small)
    layers = [(t, l) for t in range(nb) for l in range(big[t].shape[0])]
    n_big = 3 * len(layers)
    n_rem = n_big + 3 * ns

    def body(*refs):
        small_in = refs[nb:nb + ns]
        bufs, small_out = refs[nb + ns:2 * nb + ns], refs[2 * nb + ns:2 * (nb + ns)]
        send, recv, loc = refs[2 * (nb + ns):]
        x, y, c, chips = _mesh_pos()
        me = 2 * x + y
        ids = [2 * cx + cy for cx, cy in chips]
        started, sends = [], []
        for t in range(ns):
            cp = pltpu.make_async_copy(small_in[t], small_out[t].at[me], loc.at[t])
            cp.start()
            started.append(cp)
        for q, (t, l) in enumerate(layers):
            for k, chip in enumerate(chips):
                blk = bufs[t].at[l, me, c]
                cp = _remote(blk, blk, send, recv, 3 * q + k, (*chip, c))
                cp.start()
                sends.append(cp)
        for t in range(ns):
            for k, chip in enumerate(chips):
                cp = _remote(small_in[t], small_out[t].at[me], send, recv, n_big + 3 * t + k, (*chip, c))
                cp.start()
                sends.append(cp)
        for q, (t, l) in enumerate(layers):
            for k in range(3):
                blk = bufs[t].at[l, ids[k], c]
                _remote(blk, blk, send, recv, 3 * q + k, (x, y, c)).wait_recv()
        for t in range(ns):
            for k in range(3):
                blk = small_out[t].at[ids[k]]
                _remote(blk, blk, send, recv, n_big + 3 * t + k, (x, y, c)).wait_recv()
        for cp in sends:
            cp.wait_send()
        for cp in started:
            cp.wait()

    out_shape = [jax.ShapeDtypeStruct(a.shape, a.dtype) for a in big]
    out_shape += [jax.ShapeDtypeStruct((4,) + a.shape, a.dtype) for a in small]
    return _call(body, name=name, in_specs=[ANY] * (nb + ns), out_specs=[ANY] * (nb + ns), out_shape=out_shape,
                 aliases={t: t for t in range(nb)},
                 scratch=[pltpu.SemaphoreType.DMA((n_rem,)), pltpu.SemaphoreType.DMA((n_rem,)),
                          pltpu.SemaphoreType.DMA((max(ns, 1),))])(*big, *small)


def all_gather_pair(buf, ids, *, name):
    L, _, _, Rh, C = buf.shape
    br = _swap_rows(Rh, C, 2)
    n_r = Rh // br

    def body(ids_ref, in_ref, o_ref, recv, ssem, rsem):
        step = (pl.program_id(0) * 3 + pl.program_id(1)) * n_r + pl.program_id(2)
        parts = _swap_with_sibling(in_ref, recv, ssem, rsem, step)
        for cp, slot, rows in parts:
            cp.wait_recv()
            o_ref[rows, :] = recv[slot, rows, :]
        for cp, _, _ in parts:
            cp.wait_send()

    at = lambda l, s, h, r: (((l * 4 + s) * 2 + h) * n_r + r, 0)
    out = _call(body, name=name, grid=(L, 3, n_r), prefetch=1,
                in_specs=[pl.BlockSpec((br, C), lambda l, k, r, ids: at(l, ids[1 + k], ids[0], r))],
                out_specs=pl.BlockSpec((br, C), lambda l, k, r, ids: at(l, ids[1 + k], 1 - ids[0], r)),
                out_shape=jax.ShapeDtypeStruct((L * 8 * Rh, C), buf.dtype), aliases={1: 0},
                scratch=_swap_scratch(br, C, BF16))(ids, buf.reshape(L * 8 * Rh, C))
    return out.reshape(buf.shape)


def rs_pair_add(grad, ids, *, name):
    _, _, Rh, C = grad.shape
    br = _swap_rows(Rh, C, 2)
    n_r = Rh // br

    def body(ids_ref, send_ref, keep_ref, pb_ref, own_ref, recv, ssem, rsem):
        ph = pl.program_id(0)
        parts = _swap_with_sibling(send_ref, recv, ssem, rsem, ph * n_r + pl.program_id(1))
        for cp, slot, rows in parts:
            cp.wait_recv()
            s = keep_ref[rows, :].astype(F32) + recv[slot, rows, :].astype(F32)

            @pl.when(ph == 0)
            def _():
                own_ref[rows, :] = s

            @pl.when(ph > 0)
            def _():
                pb_ref[rows, :] = s.astype(BF16)

        for cp, _, _ in parts:
            cp.wait_send()

    rel = lambda ph: (ph + 3) % 4
    at = lambda s, h, r: ((s * 2 + h) * n_r + r, 0)
    grad = grad.reshape(8 * Rh, C)
    return _call(
        body, name=name, grid=(4, n_r), prefetch=1,
        in_specs=[pl.BlockSpec((br, C), lambda ph, r, ids: at(ids[1 + rel(ph)], 1 - ids[0], r)),
                  pl.BlockSpec((br, C), lambda ph, r, ids: at(ids[1 + rel(ph)], ids[0], r))],
        out_specs=[pl.BlockSpec((None, br, C), lambda ph, r, ids: (jnp.maximum(ph - 1, 0), jnp.where(ph == 0, 0, r), 0)),
                   pl.BlockSpec((br, C), lambda ph, r, ids: (jnp.where(ph == 0, r, n_r - 1), 0))],
        out_shape=[jax.ShapeDtypeStruct((3, Rh, C), BF16), jax.ShapeDtypeStruct((Rh, C), F32)],
        scratch=_swap_scratch(br, C, BF16))(ids, grad, grad)


def rs_finish(owns, gots, *, name):
    L = len(owns)
    Rh, C = owns[0].shape
    br = _swap_rows(Rh, C, 4)
    n_r = Rh // br

    def body(*refs):
        own_refs, got_refs, o_ref = refs[:L], refs[L:2 * L], refs[2 * L]
        recv, ssem, rsem = refs[2 * L + 1:]
        l = pl.program_id(0)
        c = lax.axis_index("c")
        for ll in range(L):
            @pl.when(l == ll)
            def _():
                s = own_refs[ll][...]
                for k in range(3):
                    s = s + got_refs[ll][k].astype(F32)
                o_ref[c] = s

        parts = _swap_with_sibling(o_ref.at[c], recv, ssem, rsem, l * n_r + pl.program_id(1))
        for cp, slot, rows in parts:
            cp.wait_recv()
            o_ref[1 - c, rows, :] = recv[slot, rows, :]
        for cp, _, _ in parts:
            cp.wait_send()

    def at_layer(ll):
        return lambda l, r: jnp.where(l == ll, r, jnp.where(l < ll, 0, n_r - 1))

    in_specs = [pl.BlockSpec((br, C), lambda l, r, ll=ll: (at_layer(ll)(l, r), 0)) for ll in range(L)]
    in_specs += [pl.BlockSpec((3, br, C), lambda l, r, ll=ll: (0, at_layer(ll)(l, r), 0)) for ll in range(L)]
    out = _call(body, name=name, grid=(L, n_r), in_specs=in_specs,
                out_specs=pl.BlockSpec((2, br, C), lambda l, r: (l, r, 0)),
                out_shape=jax.ShapeDtypeStruct((L * 2, Rh, C), F32),
                scratch=_swap_scratch(br, C, F32))(*owns, *gots)
    return out.reshape(L, 2, Rh, C)


SMALL_ROW_MULTIPLE = 64


def all_reduce_small(buf, *, name):
    rows = buf.shape[0]
    h, q, e = rows // 2, rows // 4, rows // 8

    def body(x_ref, o_ref, s1, r1, s2, r2, s3, r3, send, recv):
        x, y, c, _ = _mesh_pos()
        sib, xn, yn = (x, y, 1 - c), (1 - x, y, c), (x, 1 - y, c)
        at = lambda off, n: pl.ds(pl.multiple_of(off, 8), n)
        cp = _remote(x_ref.at[at((1 - c) * h, h)], r1, send, recv, 0, sib)
        cp.start()
        cp.wait()
        s1[...] = x_ref[at(c * h, h), :] + r1[...]
        cp = _remote(s1.at[at((1 - x) * q, q)], r2, send, recv, 1, xn)
        cp.start()
        cp.wait()
        s2[...] = s1[at(x * q, q), :] + r2[...]
        cp = _remote(s2.at[at((1 - y) * e, e)], r3, send, recv, 2, yn)
        cp.start()
        cp.wait()
        s3[...] = s2[at(y * e, e), :] + r3[...]
        mine = c * h + x * q + y * e
        o_ref[at(mine, e), :] = s3[...]
        theirs = o_ref.at[at(c * h + x * q + (1 - y) * e, e)]
        cp = _remote(s3, o_ref.at[at(mine, e)], send, recv, 3, yn)
        cp.start()
        cp.wait_send()
        _remote(theirs, theirs, send, recv, 3, yn).wait_recv()
        quarter = o_ref.at[at(c * h + x * q, q)]
        theirs = o_ref.at[at(c * h + (1 - x) * q, q)]
        cp = _remote(quarter, quarter, send, recv, 4, xn)
        cp.start()
        cp.wait_send()
        _remote(theirs, theirs, send, recv, 4, xn).wait_recv()
        half = o_ref.at[at(c * h, h)]
        theirs = o_ref.at[at((1 - c) * h, h)]
        cp = _remote(half, half, send, recv, 5, sib)
        cp.start()
        cp.wait_send()
        _remote(theirs, theirs, send, recv, 5, sib).wait_recv()

    part = lambda n: pltpu.VMEM((n, LANES), F32)
    return _call(body, name=name, in_specs=[VMEM_SPEC], out_specs=VMEM_SPEC,
                 out_shape=jax.ShapeDtypeStruct(buf.shape, F32),
                 scratch=[part(h), part(h), part(q), part(q), part(e), part(e), pltpu.SemaphoreType.DMA((6,)),
                          pltpu.SemaphoreType.DMA((6,))])(buf)


def _pack(arrs, row_multiple=8):
    parts = []
    for a in arrs:
        f = a.reshape(-1).astype(F32)
        parts.append(jnp.pad(f, (0, (-f.shape[0]) % PACK_ALIGN)))
    total = sum(p.shape[0] for p in parts)
    parts.append(jnp.zeros(((-total) % (row_multiple * LANES),), F32))
    return jnp.concatenate(parts).reshape(-1, LANES)


def _unpack(buf, shapes):
    flat = buf.reshape(-1)
    out, off = [], 0
    for s in shapes:
        n = math.prod(s)
        out.append(flat[off:off + n].reshape(s))
        off += n + (-n) % PACK_ALIGN
    return out


_WEIGHTS = ['ev_w_in', 'ev_ln_v_g', 'ev_ln_v_b', 'ev_w_s', 'ev_b_s', 'ev_w_pool', 'ev_pool_scale', 'ev_w_out',
            'od_w_in', 'od_norm_g', 'od_w_out', 'lb_param', 'ffn_w_up', 'ffn_conv_w', 'ffn_conv_b', 'ffn_w_down',
            'ln1_g', 'ln1_b', 'ln2_g', 'ln2_b']
_BIG = ['ev_w_in', 'ev_w_out', 'od_w_in', 'od_w_out', 'ffn_w_up', 'ffn_w_down']
_SMALL = [n for n in _WEIGHTS if n not in _BIG]


def kernel(x, ev_w_in, ev_ln_v_g, ev_ln_v_b, ev_w_s, ev_b_s, ev_w_pool, ev_pool_scale, ev_w_out, od_w_in, od_norm_g, od_w_out, lb_param, ffn_w_up, ffn_conv_w, ffn_conv_b, ffn_w_down, ln1_g, ln1_b, ln2_g, ln2_b, loss_target, m_ev_w_in, m_ev_ln_v_g, m_ev_ln_v_b, m_ev_w_s, m_ev_b_s, m_ev_w_pool, m_ev_pool_scale, m_ev_w_out, m_od_w_in, m_od_norm_g, m_od_w_out, m_lb_param, m_ffn_w_up, m_ffn_conv_w, m_ffn_conv_b, m_ffn_w_down, m_ln1_g, m_ln1_b, m_ln2_g, m_ln2_b, v_ev_w_in, v_ev_ln_v_g, v_ev_ln_v_b, v_ev_w_s, v_ev_b_s, v_ev_w_pool, v_ev_pool_scale, v_ev_w_out, v_od_w_in, v_od_norm_g, v_od_w_out, v_lb_param, v_ffn_w_up, v_ffn_conv_w, v_ffn_conv_b, v_ffn_w_down, v_ln1_g, v_ln1_b, v_ln2_g, v_ln2_b):
    given = dict(locals())
    w = {n: given[n] for n in _WEIGHTS}
    mom = {n: given["m_" + n] for n in _WEIGHTS}
    vel = {n: given["v_" + n] for n in _WEIGHTS}
    x2d = x[0]
    tgt = loss_target[0]
    T, D = x2d.shape
    DA = ev_ln_v_g.shape[-1]
    DB = ev_pool_scale.shape[-1]
    HA = ev_w_s.shape[1]
    G = len(B_WINDOWS)
    CG = DB // G
    DC = 4 * od_norm_g.shape[-1]
    F = ffn_conv_b.shape[-1] // 2
    chip = 2 * lax.axis_index("x") + lax.axis_index("y")

    ids = mesh_ids()
    halves = lambda a: a.reshape(1, 4, 2, a.shape[2] // 2, a.shape[3])
    slot = {(n, l): halves(cast_to_slot(w[n], l, ids[4:5], f"cast_{n}{l}"))
            for n in _BIG for l in range(w[n].shape[0])}

    def riding(*keys):
        return [IciCopy("gather", slot[k]) for k in keys]

    def pair(buf, key):
        g = all_gather_pair(buf, ids, name=f"all_gather_pair_{key[0]}{key[1]}")
        return g.reshape(1, 4, g.shape[3] * 2, g.shape[4])

    early = [('ev_w_in', 0), ('ev_w_out', 0)]
    gathered = all_gather_chips([slot[k] for k in early], [ev_w_pool[0], ffn_conv_w, od_norm_g],
                                name="all_gather_chips")
    wpool_full = gathered[2].transpose(1, 0, 2, 3).reshape(G, CG, CG)
    cw_full = gathered[3].transpose(1, 2, 0, 3).reshape(DEPTH, 3, 2 * F)
    gn_full = gathered[4].reshape(1, DC)
    win0 = pair(gathered[0], early[0])[0]
    wout0 = pair(gathered[1], early[1]).reshape(DA + DB, D)
    wup, wdn = {}, {}
    rh_up = D // 2
    cut1 = (rh_up * 35 // 100) // 16 * 16
    cut2 = cut1 + (rh_up * 18 // 100) // 16 * 16
    cb3 = ffn_conv_b.reshape(DEPTH, 1, 2 * F)
    ws = ev_w_s[0]
    wsT = jnp.swapaxes(ws, 1, 2)
    bsT = ev_b_s[0].T
    wpb = wpool_full.astype(BF16)
    ones = jnp.ones((1, D), F32)
    zeros = jnp.zeros((1, D), F32)
    row = lambda a, l: a[l:l + 1]

    Ns0 = win0.shape[-1]
    Nu = ffn_w_up.shape[-1]
    tm_big = _tile(T, 1024, 8)
    tm_ln = _tile(T, 512, 8)
    tk_ln = _tile(D, 512)
    n_p = DC // HP

    def nat_spec(Ns, tnw):
        nps = Ns // tnw
        return pl.BlockSpec((1, D, tnw), lambda i, j: (j // nps, 0, j % nps))

    perm_spec = pl.BlockSpec((4, D, HP), lambda i, j: (0, 0, j))

    xb16 = cast_bf16(x, "cast_x")[0]
    up0 = IciCopy("gather", slot[('ffn_w_up', 0)], rows=(0, cut1))
    h0 = mm_nn(xb16, win0, w_spec=nat_spec(Ns0, Ns0), P=1, tnw=Ns0, tm=tm_big, n_j=4, name="ev_in", carry=[up0])
    up0 = IciCopy("gather", up0.out, rows=(cut1, cut2 - cut1))
    cat = gating_fwd(h0, ev_ln_v_g, ev_ln_v_b, ws, bsT, wpb, ev_pool_scale, name="gating_fwd", carry=[up0])
    up0 = IciCopy("gather", up0.out, rows=(cut2, rh_up - cut2))

    def mix_ln(a, wmat, res, l, name, carry=()):
        K = a.shape[1]
        tk = _tile(K, 2048)
        return mm_ln(a, wmat, *res, row(ln1_g, l), row(ln1_b, l), w_spec=pl.BlockSpec((tk, D), lambda i, k: (k, 0)),
                     K=K, tk=tk, tm=tm_ln, name=name, carry=carry)

    def ffn_down(f, res, l, carry=()):
        tk = F // 4 if (F // 4) % LANES == 0 else _tile(F, 512)
        return mm_ln(f, wdn[l], *res, row(ln2_g, l), row(ln2_b, l),
                     w_spec=pl.BlockSpec((None, tk, D), lambda i, k: (0, k, 0)), K=F, tk=tk, tm=tm_ln,
                     name=f"ffn_down{l}", carry=carry)

    xh1, y1, rs1 = mix_ln(cat, wout0, (x2d, ones, zeros), 0, "ev_out", carry=[up0])
    wup[0] = pair(up0.out, ('ffn_w_up', 0))
    res1 = (xh1, row(ln1_g, 0), row(ln1_b, 0))
    ride = riding(('ffn_w_down', 0), ('od_w_in', 0))
    hf0, hc0, f0 = ffn_up(y1, wup[0], cw_full, cb3, 0, name="ffn_up0", carry=ride)
    wdn[0] = pair(ride[0].out, ('ffn_w_down', 0)).reshape(1, F, D)
    win1 = pair(ride[1].out, ('od_w_in', 0))[0]
    cut1 = (rh_up * 40 // 100) // 16 * 16
    cut2 = 2 * cut1
    up1 = IciCopy("gather", slot[('ffn_w_up', 1)], rows=(0, cut1))
    ride = riding(('od_w_out', 0))
    xh2, y2, rs2 = ffn_down(f0, res1, 0, carry=ride + [up1])
    wout1 = pair(ride[0].out, ('od_w_out', 0)).reshape(DC, D)
    res2 = (xh2, row(ln2_g, 0), row(ln2_b, 0))
    up1 = IciCopy("gather", up1.out, rows=(cut1, cut2 - cut1))
    h1p = mm_nn(y2, win1, w_spec=perm_spec, P=4, tnw=HP, tm=tm_big, n_j=n_p, name="od_in", carry=[up1])
    up1 = IciCopy("gather", up1.out, rows=(cut2, rh_up - cut2))
    yh, o_saved, sp = hgrn_fwd(h1p, lb_param, gn_full, name="hgrn_fwd", carry=[up1])
    wup[1] = pair(up1.out, ('ffn_w_up', 1))
    xh3, y3, rs3 = mix_ln(yh, wout1, res2, 1, "od_out")
    res3 = (xh3, row(ln1_g, 1), row(ln1_b, 1))
    ride = riding(('ffn_w_down', 1))
    hf1, hc1, f1 = ffn_up(y3, wup[1], cw_full, cb3, 1, name="ffn_up1", carry=ride)
    wdn[1] = pair(ride[0].out, ('ffn_w_down', 1)).reshape(1, F, D)
    xh4, y4, rs4 = ffn_down(f1, res3, 1)
    loss_p, dr, drb, dg_ln2_1, db_ln2_1 = loss_bwd(xh4, rs4, row(ln2_g, 1), row(ln2_b, 1), tgt, name="loss_bwd")

    tt = _tile(T, 2048, 16)
    n_t = T // tt
    tnu = Nu // 2 if (Nu // 2) % LANES == 0 else Nu
    upb = Nu // tnu
    tkd = _tile(D, 1024)

    def pair_sum(g4, name):
        pb, own = rs_pair_add(g4.reshape(4, 2, g4.shape[1] // 2, g4.shape[2]), ids, name="rs_pair_add_" + name)
        return IciCopy("scatter", pb), own

    def g_out(a, gb, name, tkk=None):
        K = a.shape[1]
        tkk = tkk or _tile(K, 1024)
        return mm_tn(a, gb, a_spec=pl.BlockSpec((tt, tkk), lambda kb, nb, t: (t, kb)),
                     g_spec=pl.BlockSpec((tt, tkd), lambda kb, nb, t: (t, nb)),
                     o_spec=pl.BlockSpec((1, tkk, tkd), lambda kb, nb, t: (0, kb, nb)), out_shape=(1, K, D),
                     grid=(K // tkk, D // tkd, n_t), acc_shape=(tkk, tkd), P=1, tnw=tkd, name=name)

    def ffn_bwd(l, dr2, dr2b, f, hf, hc, y_in, xh_in, rs_in):
        g_dn = g_out(f, dr2b, f"g_ffn_down{l}", tkk=F // 4 if (F // 4) % LANES == 0 else None)
        rs_dn = pair_sum(g_dn.reshape(4, F // 4, D), f"ffn_down{l}")
        dh, dcw, dcb = ffn_dgate(dr2b, wdn[l], hf, hc, cw_full, l, name=f"ffn_dgate{l}", carry=[rs_dn[0]])
        g_up = mm_tn(y_in, dh, a_spec=pl.BlockSpec((tt, tkd), lambda kb, nb, t: (t, kb)),
                     g_spec=pl.BlockSpec((None, tt, tnu), lambda kb, nb, t: (nb // (2 * upb), t, nb % (2 * upb))),
                     o_spec=pl.BlockSpec((1, tkd, tnu), lambda kb, nb, t: (nb // upb, kb, nb % upb)),
                     out_shape=(4, D, Nu), grid=(D // tkd, 4 * upb, n_t), acc_shape=(tkd, tnu), P=1, tnw=tnu,
                     name=f"g_ffn_up{l}")
        rs_up = pair_sum(g_up, f"ffn_up{l}")
        tku = tnu
        kps = Nu // tku
        out = mm_nt_res(dh, wup[l], dr2, (xh_in, rs_in, row(ln1_g, l)),
                        a_spec=pl.BlockSpec((None, tm_ln, tku), lambda i, k: (k // (2 * kps), i, k % (2 * kps))),
                        w_spec=pl.BlockSpec((None, 1, D, tku), lambda i, k: (0, k // kps, 0, k % kps)),
                        P=1, tnw=tku, n_k=4 * kps, tm=tm_ln, name=f"d_ffn_in{l}", carry=[rs_up[0]])
        return rs_dn, rs_up, dcw, dcb, out

    rs_dn1, rs_up1, dcw1, dcb1, (dr1, dr1b, dg_ln1_1, db_ln1_1) = ffn_bwd(1, dr, drb, f1, hf1, hc1, y3, xh3, rs3)
    rs_wout1 = pair_sum(g_out(yh, dr1b, "g_od_out").reshape(4, DC // 4, D), "od_out")
    dyh = mm_nt_plain(dr1b, wout1, tm=tm_big, tn=_tile(DC, 512), name="d_od_out")
    dh1p, d_gn, d_lbp = hgrn_bwd(h1p, o_saved, dyh, sp, lb_param, gn_full, name="hgrn_bwd", carry=[rs_wout1[0]])
    g_win1 = mm_tn(y2, dh1p, a_spec=pl.BlockSpec((tt, tkd), lambda kb, nb, t: (t, kb)),
                   g_spec=pl.BlockSpec((tt, 4 * HP), lambda kb, nb, t: (t, nb)),
                   o_spec=pl.BlockSpec((4, tkd, HP), lambda kb, nb, t: (0, kb, nb)), out_shape=(4, D, DC),
                   grid=(D // tkd, n_p, n_t), acc_shape=(tkd, 4 * HP), P=4, tnw=HP, name="g_od_in")
    rs_win1 = pair_sum(g_win1, "od_in")
    dr, drb, dg_ln2_0, db_ln2_0 = mm_nt_res(
        dh1p, win1, dr1, (xh2, rs2, row(ln2_g, 0)), a_spec=pl.BlockSpec((tm_ln, 4 * HP), lambda i, k: (i, k)),
        w_spec=pl.BlockSpec((4, D, HP), lambda i, k: (0, 0, k)), P=4, tnw=HP, n_k=n_p, tm=tm_ln, name="d_od_in",
        carry=[rs_win1[0]])
    rs_dn0, rs_up0, dcw0, dcb0, (dr1, dr1b, dg_ln1_0, db_ln1_0) = ffn_bwd(0, dr, drb, f0, hf0, hc0, y1, xh1, rs1)
    rs_wout0 = pair_sum(g_out(cat, dr1b, "g_ev_out").reshape(4, (DA + DB) // 4, D), "ev_out")
    dcat = mm_nt_plain(dr1b, wout0, tm=tm_big, tn=_tile(DA + DB, 512), name="d_ev_out")
    dh0, d_ws, d_bsT, d_lg, d_lb, d_sc, d_wp = gating_bwd(h0, dcat, ev_ln_v_g, ev_ln_v_b, ws, wsT, bsT, wpb,
                                                          ev_pool_scale, name="gating_bwd", carry=[rs_wout0[0]])
    g_win0 = mm_tn(xb16, dh0, a_spec=pl.BlockSpec((tt, tkd), lambda kb, nb, t: (t, kb)),
                   g_spec=pl.BlockSpec((tt, Ns0), lambda kb, nb, t: (t, nb)),
                   o_spec=pl.BlockSpec((1, tkd, Ns0), lambda kb, nb, t: (nb, kb, 0)), out_shape=(4, D, Ns0),
                   grid=(D // tkd, 4, n_t), acc_shape=(tkd, Ns0), P=1, tnw=Ns0, name="g_ev_in")
    rs_win0 = pair_sum(g_win0, "ev_in")
    grad_x = mm_nt_res(dh0, win0, dr1, None, a_spec=pl.BlockSpec((tm_ln, Ns0), lambda i, k: (i, k)),
                       w_spec=pl.BlockSpec((1, D, Ns0), lambda i, k: (k, 0, 0)), P=1, tnw=Ns0, n_k=4, tm=tm_ln,
                       name="d_ev_in", carry=[rs_win0[0]])

    per_weight = [[rs_win0], [rs_wout0], [rs_win1], [rs_wout1], [rs_up0, rs_up1], [rs_dn0, rs_dn1]]
    shared = [rs_finish([own for _, own in m], [cp.out for cp, _ in m], name="rs_finish_" + n)
              for n, m in zip(_BIG, per_weight)]
    big_g = {n: s.reshape(w[n].shape) for n, s in zip(_BIG, shared)}

    small_full = {
        'ev_ln_v_g': d_lg, 'ev_ln_v_b': d_lb, 'ev_w_s': d_ws[None], 'ev_b_s': d_bsT.T[None], 'ev_w_pool': d_wp[None],
        'ev_pool_scale': d_sc, 'od_norm_g': d_gn, 'lb_param': d_lbp,
        'ffn_conv_w': jnp.stack([jnp.concatenate([dcw0[0], dcw0[1]], axis=-1),
                                 jnp.concatenate([dcw1[0], dcw1[1]], axis=-1)]),
        'ffn_conv_b': jnp.stack([jnp.concatenate([dcb0[0, 0], dcb0[1, 0]]), jnp.concatenate([dcb1[0, 0], dcb1[1, 0]])]),
        'ln1_g': jnp.concatenate([dg_ln1_0, dg_ln1_1]), 'ln1_b': jnp.concatenate([db_ln1_0, db_ln1_1]),
        'ln2_g': jnp.concatenate([dg_ln2_0, dg_ln2_1]), 'ln2_b': jnp.concatenate([db_ln2_0, db_ln2_1])}
    packed = _pack([small_full[n] for n in _SMALL] + [loss_p[0, 0:1]], SMALL_ROW_MULTIPLE)
    reduced = _unpack(all_reduce_small(packed, name="all_reduce_small"),
                      [small_full[n].shape for n in _SMALL] + [(1,)])
    small_g = dict(zip(_SMALL, reduced[:-1]))
    loss = reduced[-1][0]
    small_g['ev_w_pool'] = lax.dynamic_slice_in_dim(small_g['ev_w_pool'], chip * (CG // 4), CG // 4, axis=2)
    small_g['ffn_conv_w'] = lax.dynamic_slice_in_dim(small_g['ffn_conv_w'], chip * (F // 2), F // 2, axis=2)
    small_g['od_norm_g'] = lax.dynamic_slice_in_dim(small_g['od_norm_g'], chip * (DC // 4), DC // 4, axis=1)

    grads, delta, new_m, new_v = {}, {}, {}, {}
    for n in _BIG:
        grads[n], delta[n], new_m[n], new_v[n] = adamw(w[n], big_g[n], mom[n], vel[n], "adamw_" + n)
    upd = adamw_small(*[[d[n] for n in _SMALL] for d in (w, small_g, mom, vel)], "adamw_small")
    for d, outs in zip((grads, delta, new_m, new_v), upd):
        d.update(zip(_SMALL, outs))

    return (loss, grad_x[None], *[grads[n] for n in _WEIGHTS], *[delta[n] for n in _WEIGHTS],
            *[new_m[n] for n in _WEIGHTS], *[new_v[n] for n in _WEIGHTS])
```

```python
import math

import jax
import jax.numpy as jnp
from jax import lax
from jax.experimental import pallas as pl
from jax.experimental.pallas import tpu as pltpu

F32 = jnp.float32
BF16 = jnp.bfloat16
MESH = pl.DeviceIdType.MESH
ANY = pl.BlockSpec(memory_space=pl.ANY)
VMEM_SPEC = pl.BlockSpec(memory_space=pltpu.VMEM)

DEPTH = 2
ALPHA = (2 * DEPTH) ** 0.25
LN_EPS = 1e-5
A_HEAD = 128
A_CHUNK = 128
B_WINDOWS = (2, 4, 8, 16)
POOL_HALO = 16
C_HEAD = 128
C_CHUNK = 64
CONV_HALO = 8
ADAM_LR = 0.001
ADAM_B1 = 0.9
ADAM_B2 = 0.999
ADAM_EPS = 1e-08
ADAM_WD = 0.01
ADAM_STEP = 10
V7X_VMEM_LIMIT_BYTES = 56 * 1024 * 1024
LANES = 128
PACK_ALIGN = 8 * LANES


class IciCopy:
    def __init__(self, kind, arr, rows=None):
        self.kind, self.arr, self.out = kind, arr, None
        self.rows = rows


def _carried_copies(items, in_refs, out_refs, send, recv):
    x, y, c, chips = _mesh_pos()
    me = 2 * x + y
    sends, lands = [], []
    for q, (it, src, dst) in enumerate(zip(items, in_refs, out_refs)):
        rows = pl.ds(*(it.rows or (0, it.arr.shape[-2])))
        for k, (cx, cy) in enumerate(chips):
            if it.kind == "gather":
                mine, theirs = dst.at[0, me, c, rows], dst.at[0, 2 * cx + cy, c, rows]
                sends.append(_remote(mine, mine, send, recv, 3 * q + k, (cx, cy, c)))
            else:
                theirs = dst.at[k]
                sends.append(_remote(src.at[k], theirs, send, recv, 3 * q + k, (cx, cy, c)))
            lands.append(_remote(theirs, theirs, send, recv, 3 * q + k, (x, y, c)))
    return sends, lands


def _call(body, *, name, out_shape, grid=(), in_specs=None, out_specs=None, scratch=(), prefetch=0, aliases=None,
          carry=()):
    single = not isinstance(out_specs, (list, tuple))
    in_specs = list(in_specs)
    out_specs = [out_specs] if single else list(out_specs)
    out_shape = [out_shape] if single else list(out_shape)
    scratch = list(scratch)
    aliases = dict(aliases or {})
    n_in, n_out, n_sc, n_c = len(in_specs), len(out_specs), len(scratch), len(carry)
    inner = body
    if n_c:
        assert grid, "a carrier needs a grid"
        for q, it in enumerate(carry):
            if it.kind == "gather":
                aliases[prefetch + n_in + q] = n_out + q
        in_specs += [ANY] * n_c
        out_specs += [ANY] * n_c
        out_shape += [jax.ShapeDtypeStruct(it.arr.shape, it.arr.dtype) for it in carry]
        scratch += [pltpu.SemaphoreType.DMA((3 * n_c,)), pltpu.SemaphoreType.DMA((3 * n_c,))]

        def inner(*refs):
            pre, refs = refs[:prefetch], refs[prefetch:]
            ins, c_in = refs[:n_in], refs[n_in:n_in + n_c]
            outs, c_out = refs[n_in + n_c:n_in + n_c + n_out], refs[n_in + n_c + n_out:n_in + 2 * n_c + n_out]
            rest = refs[n_in + 2 * n_c + n_out:]
            first = last = True
            for d, n in enumerate(grid):
                first = jnp.logical_and(first, pl.program_id(d) == 0)
                last = jnp.logical_and(last, pl.program_id(d) == n - 1)

            @pl.when(first)
            def _():
                for cp in _carried_copies(carry, c_in, c_out, rest[-2], rest[-1])[0]:
                    cp.start()

            body(*pre, *ins, *outs, *rest[:n_sc])

            @pl.when(last)
            def _():
                sends, lands = _carried_copies(carry, c_in, c_out, rest[-2], rest[-1])
                for cp in lands:
                    cp.wait_recv()
                for cp in sends:
                    cp.wait_send()

    spec = pltpu.PrefetchScalarGridSpec(num_scalar_prefetch=prefetch, grid=grid, in_specs=in_specs,
                                        out_specs=out_specs, scratch_shapes=scratch)
    fn = pl.pallas_call(inner, name=name, grid_spec=spec, out_shape=out_shape, input_output_aliases=aliases,
                        compiler_params=pltpu.CompilerParams(vmem_limit_bytes=V7X_VMEM_LIMIT_BYTES))

    def run(*args):
        res = fn(*args, *[it.arr for it in carry])
        for it, o in zip(carry, res[n_out:]):
            it.out = o
        return res[0] if single else list(res[:n_out])

    return run


def _tile(n, pref, unit=LANES):
    if n <= pref:
        return n
    t = (pref // unit) * unit
    while t > unit and n % t:
        t -= unit
    assert n % t == 0, (n, pref, unit)
    return t


def _slabs(n, rows=128):
    return [slice(r, min(r + rows, n)) for r in range(0, n, rows)]


def _dot(a, b):
    return jnp.dot(a, b, preferred_element_type=F32)


def _dot_nt(a, b):
    return lax.dot_general(a, b, (((1,), (1,)), ((), ())), preferred_element_type=F32)


def _dot_tn(a, b):
    return lax.dot_general(a, b, (((0,), (0,)), ((), ())), preferred_element_type=F32)


def _sigmoid(x):
    return jax.nn.sigmoid(x)


_GELU_C = math.sqrt(2.0 / math.pi)


def _gelu(x):
    return 0.5 * x * (1.0 + jnp.tanh(_GELU_C * (x + 0.044715 * x * x * x)))


def _gelu_grad(x):
    th = jnp.tanh(_GELU_C * (x + 0.044715 * x * x * x))
    return 0.5 * (1.0 + th) + 0.5 * x * (1.0 - th * th) * _GELU_C * (1.0 + 3.0 * 0.044715 * x * x)


def _ln_fwd(r, g, b):
    mu = jnp.mean(r, axis=-1, keepdims=True)
    xc = r - mu
    var = jnp.mean(xc * xc, axis=-1, keepdims=True)
    rstd = lax.rsqrt(var + LN_EPS)
    xh = xc * rstd
    return xh * g + b, xh, rstd


def _ln_bwd(dy, xh, rstd, g):
    dxh = dy * g
    m1 = jnp.mean(dxh, axis=-1, keepdims=True)
    m2 = jnp.mean(dxh * xh, axis=-1, keepdims=True)
    dr = rstd * (dxh - m1 - xh * m2)
    return dr, jnp.sum(dy * xh, axis=0, keepdims=True), jnp.sum(dy, axis=0, keepdims=True)


def _exact_tri_dot(tri, x):
    hi = x.astype(BF16)
    r1 = x - hi.astype(F32)
    mid = r1.astype(BF16)
    lo = (r1 - mid.astype(F32)).astype(BF16)
    return _dot(tri, hi) + _dot(tri, mid) + _dot(tri, lo)


def cast_bf16(a3, name):
    L, R, C = a3.shape
    br = _tile(R, max(8, (1 << 20) // C), 8)

    def body(a_ref, o_ref):
        o_ref[...] = a_ref[...].astype(BF16)

    return _call(body, name=name, grid=(L, R // br),
                 in_specs=[pl.BlockSpec((None, br, C), lambda l, r: (l, r, 0))],
                 out_specs=pl.BlockSpec((None, br, C), lambda l, r: (l, r, 0)),
                 out_shape=jax.ShapeDtypeStruct((L, R, C), BF16))(a3)


def adamw(w, g, m, v, name):
    L, R, C = w.shape
    br = _tile(R, max(8, (1 << 19) // C), 8)

    def body(w_ref, g_ref, m_ref, v_ref, go_ref, d_ref, nm_ref, nv_ref):
        _adamw_update(w_ref, g_ref, m_ref, v_ref, go_ref, d_ref, nm_ref, nv_ref)

    spec = pl.BlockSpec((None, br, C), lambda l, r: (l, r, 0))
    sds = jax.ShapeDtypeStruct((L, R, C), F32)
    return _call(body, name=name, grid=(L, R // br), in_specs=[spec] * 4, out_specs=[spec] * 4,
                 out_shape=[sds] * 4)(w, g, m, v)


def _adamw_update(w_ref, g_ref, m_ref, v_ref, go_ref, d_ref, nm_ref, nv_ref):
    gg = g_ref[...]
    nm = ADAM_B1 * m_ref[...] + (1.0 - ADAM_B1) * gg
    nv = ADAM_B2 * v_ref[...] + (1.0 - ADAM_B2) * (gg * gg)
    go_ref[...] = gg
    d_ref[...] = -ADAM_LR * ((nm / (1.0 - ADAM_B1 ** ADAM_STEP))
                             / (jnp.sqrt(nv / (1.0 - ADAM_B2 ** ADAM_STEP)) + ADAM_EPS) + ADAM_WD * w_ref[...])
    nm_ref[...] = nm
    nv_ref[...] = nv


def adamw_small(ws, gs, ms, vs, name):
    n = len(ws)

    def body(*refs):
        for t in range(n):
            _adamw_update(*[refs[k * n + t] for k in range(8)])

    args = [a for group in (ws, gs, ms, vs) for a in group]
    out = _call(body, name=name, in_specs=[VMEM_SPEC] * (4 * n), out_specs=[VMEM_SPEC] * (4 * n),
                out_shape=[jax.ShapeDtypeStruct(a.shape, F32) for a in args])(*args)
    return [[out[k * n + t] for t in range(n)] for k in range(4)]


def mm_nn(a, w, *, w_spec, P, tnw, tm, n_j, name, carry=()):
    T, K = a.shape
    bw = P * tnw

    def body(a_ref, w_ref, o_ref):
        av = a_ref[...]
        for p in range(P):
            o_ref[:, p * tnw:(p + 1) * tnw] = _dot(av, w_ref[p]).astype(BF16)

    return _call(body, name=name, grid=(T // tm, n_j),
                 in_specs=[pl.BlockSpec((tm, K), lambda i, j: (i, 0)), w_spec],
                 out_specs=pl.BlockSpec((tm, bw), lambda i, j: (i, j)),
                 out_shape=jax.ShapeDtypeStruct((T, n_j * bw), BF16), carry=carry)(a, w)


def mm_ln(a, w, res, rg, rb, g, b, *, w_spec, K, tk, tm, name, carry=()):
    T, N = res.shape
    n_k = K // tk

    def body(a_ref, w_ref, res_ref, rg_ref, rb_ref, g_ref, b_ref, xh_ref, y_ref, rs_ref, acc):
        k = pl.program_id(1)

        @pl.when(k == 0)
        def _():
            acc[...] = jnp.zeros_like(acc)

        acc[...] += _dot(a_ref[...], w_ref[...])

        @pl.when(k == n_k - 1)
        def _():
            for rows in _slabs(tm):
                r = ALPHA * (res_ref[rows, :] * rg_ref[...] + rb_ref[...]) + acc[rows, :]
                y, xh, rstd = _ln_fwd(r, g_ref[...], b_ref[...])
                xh_ref[rows, :] = xh
                y_ref[rows, :] = y.astype(BF16)
                rs_ref[rows, :] = rstd

    row = pl.BlockSpec((tm, N), lambda i, k: (i, 0))
    vec = pl.BlockSpec((1, N), lambda i, k: (0, 0))
    return _call(body, name=name, grid=(T // tm, n_k),
                 in_specs=[pl.BlockSpec((tm, tk), lambda i, k: (i, k)), w_spec, row, vec, vec, vec, vec],
                 out_specs=[row, row, pl.BlockSpec((tm, 1), lambda i, k: (i, 0))],
                 out_shape=[jax.ShapeDtypeStruct((T, N), F32), jax.ShapeDtypeStruct((T, N), BF16),
                            jax.ShapeDtypeStruct((T, 1), F32)],
                 scratch=[pltpu.VMEM((tm, N), F32)], carry=carry)(a, w, res, rg, rb, g, b)


def mm_nt_plain(a, w, *, tm, tn, name):
    T, K = a.shape
    N = w.shape[0]

    def body(a_ref, w_ref, o_ref):
        o_ref[...] = _dot_nt(a_ref[...], w_ref[...]).astype(BF16)

    return _call(body, name=name, grid=(T // tm, N // tn),
                 in_specs=[pl.BlockSpec((tm, K), lambda i, j: (i, 0)), pl.BlockSpec((tn, K), lambda i, j: (j, 0))],
                 out_specs=pl.BlockSpec((tm, tn), lambda i, j: (i, j)),
                 out_shape=jax.ShapeDtypeStruct((T, N), BF16))(a, w)


def mm_nt_res(a, w, res, ln, *, a_spec, w_spec, P, tnw, n_k, tm, name, carry=()):
    T, N = res.shape
    n_i = T // tm

    def body(*refs):
        if ln is None:
            a_ref, w_ref, res_ref, o_ref, acc = refs
        else:
            a_ref, w_ref, res_ref, xh_ref, rs_ref, g_ref, dr_ref, drb_ref, dg_ref, db_ref, acc = refs
        i = pl.program_id(0)
        k = pl.program_id(1)

        @pl.when(k == 0)
        def _():
            acc[...] = jnp.zeros_like(acc)

        wv = w_ref[0] if P == 1 else jnp.concatenate([w_ref[p] for p in range(P)], axis=1)
        acc[...] += _dot_nt(a_ref[...], wv)

        @pl.when(k == n_k - 1)
        def _():
            if ln is not None:
                @pl.when(i == 0)
                def _():
                    dg_ref[...] = jnp.zeros_like(dg_ref)
                    db_ref[...] = jnp.zeros_like(db_ref)

            for rows in _slabs(tm):
                d = ALPHA * res_ref[rows, :] + acc[rows, :]
                if ln is None:
                    o_ref[rows, :] = d
                else:
                    dr, dg, db = _ln_bwd(d, xh_ref[rows, :], rs_ref[rows, :], g_ref[...])
                    dr_ref[rows, :] = dr
                    drb_ref[rows, :] = dr.astype(BF16)
                    dg_ref[...] += dg
                    db_ref[...] += db

    row = pl.BlockSpec((tm, N), lambda i, k: (i, 0))
    vec = pl.BlockSpec((1, N), lambda i, k: (0, 0))
    scratch = [pltpu.VMEM((tm, N), F32)]
    if ln is None:
        return _call(body, name=name, grid=(n_i, n_k), in_specs=[a_spec, w_spec, row], out_specs=row,
                     out_shape=jax.ShapeDtypeStruct((T, N), F32), scratch=scratch, carry=carry)(a, w, res)
    xh, rstd, g = ln
    return _call(body, name=name, grid=(n_i, n_k),
                 in_specs=[a_spec, w_spec, row, row, pl.BlockSpec((tm, 1), lambda i, k: (i, 0)), vec],
                 out_specs=[row, row, vec, vec],
                 out_shape=[jax.ShapeDtypeStruct((T, N), F32), jax.ShapeDtypeStruct((T, N), BF16),
                            jax.ShapeDtypeStruct((1, N), F32), jax.ShapeDtypeStruct((1, N), F32)],
                 scratch=scratch, carry=carry)(a, w, res, xh, rstd, g)


def mm_tn(a, g, *, a_spec, g_spec, o_spec, out_shape, grid, acc_shape, P, tnw, name):
    n_t = grid[2]

    def body(a_ref, g_ref, o_ref, acc):
        t = pl.program_id(2)

        @pl.when(t == 0)
        def _():
            acc[...] = jnp.zeros_like(acc)

        acc[...] += _dot_tn(a_ref[...], g_ref[...])

        @pl.when(t == n_t - 1)
        def _():
            for p in range(P):
                o_ref[p] = acc[:, p * tnw:(p + 1) * tnw].astype(BF16)

    return _call(body, name=name, grid=grid, in_specs=[a_spec, g_spec], out_specs=o_spec,
                 out_shape=jax.ShapeDtypeStruct(out_shape, BF16),
                 scratch=[pltpu.VMEM(acc_shape, F32)])(a, g)


def _causal_conv(ext, halo, w, b):
    s1 = pltpu.roll(ext, 1, 0)[halo:]
    s2 = pltpu.roll(ext, 2, 0)[halo:]
    return b + w[2:3] * ext[halo:] + w[1:2] * s1 + w[0:1] * s2, s1, s2


def ffn_up(xb, wup, cw, cb, l, *, name, carry=()):
    T, D = xb.shape
    Ns = wup.shape[-1]
    F = 2 * Ns
    tn = _tile(Ns, 256)
    nps = Ns // tn
    n_j = F // tn
    tm = _tile(T, 1024, 8)

    def body(x_ref, wa_ref, wv_ref, cwa_ref, cwv_ref, cba_ref, cbv_ref, h_ref, hc_ref, f_ref, carry):
        i = pl.program_id(1)

        @pl.when(i == 0)
        def _():
            carry[...] = jnp.zeros_like(carry)

        xv = x_ref[...]
        ha = _dot(xv, wa_ref[...])
        hv = _dot(xv, wv_ref[...])
        ca, _, _ = _causal_conv(jnp.concatenate([carry[0], ha], axis=0), CONV_HALO, cwa_ref[...], cba_ref[...])
        cv, _, _ = _causal_conv(jnp.concatenate([carry[1], hv], axis=0), CONV_HALO, cwv_ref[...], cbv_ref[...])
        carry[0] = ha[tm - CONV_HALO:]
        carry[1] = hv[tm - CONV_HALO:]
        h_ref[0] = ha.astype(BF16)
        h_ref[1] = hv.astype(BF16)
        hc_ref[0] = ca.astype(BF16)
        hc_ref[1] = cv.astype(BF16)
        f_ref[...] = (ca * _sigmoid(ca) * cv).astype(BF16)

    wspec_a = pl.BlockSpec((None, None, D, tn), lambda j, i: (0, j // nps, 0, j % nps))
    wspec_v = pl.BlockSpec((None, None, D, tn), lambda j, i: (0, 2 + j // nps, 0, j % nps))
    pair_tile = pl.BlockSpec((2, tm, tn), lambda j, i: (0, i, j))
    return _call(
        body, name=name, grid=(n_j, T // tm),
        in_specs=[pl.BlockSpec((tm, D), lambda j, i: (i, 0)), wspec_a, wspec_v,
                  pl.BlockSpec((None, 3, tn), lambda j, i: (l, 0, j)),
                  pl.BlockSpec((None, 3, tn), lambda j, i: (l, 0, n_j + j)),
                  pl.BlockSpec((None, 1, tn), lambda j, i: (l, 0, j)),
                  pl.BlockSpec((None, 1, tn), lambda j, i: (l, 0, n_j + j))],
        out_specs=[pair_tile, pair_tile, pl.BlockSpec((tm, tn), lambda j, i: (i, j))],
        out_shape=[jax.ShapeDtypeStruct((2, T, F), BF16), jax.ShapeDtypeStruct((2, T, F), BF16),
                   jax.ShapeDtypeStruct((T, F), BF16)],
        scratch=[pltpu.VMEM((2, CONV_HALO, tn), F32)], carry=carry)(xb, wup, wup, cw, cw, cb, cb)


def ffn_dgate(db16, wdn, h, hc, cw, l, *, name, carry=()):
    T, D = db16.shape
    F = h.shape[-1]
    tn = _tile(F, 512)
    n_j = F // tn
    tm = _tile(T, 512, 16)
    n_i = T // tm
    n_ext = tm + CONV_HALO

    def body(d_ref, w_ref, h_ref, hc_ref, cwa_ref, cwv_ref, dh_ref, dcw_ref, dcb_ref, carry):
        ip = pl.program_id(1)

        @pl.when(ip == 0)
        def _():
            carry[...] = jnp.zeros_like(carry)
            dcw_ref[...] = jnp.zeros_like(dcw_ref)
            dcb_ref[...] = jnp.zeros_like(dcb_ref)

        df = _dot_nt(d_ref[...], w_ref[...])
        ca = hc_ref[0].astype(F32)
        cv = hc_ref[1].astype(F32)
        sig = _sigmoid(ca)
        sil = ca * sig
        da = df * cv * (sig + sil * (1.0 - sig))
        dv = df * sil
        for half, (dc, w_ref_h) in enumerate(((da, cwa_ref), (dv, cwv_ref))):
            w = w_ref_h[...]
            h0 = h_ref[half].astype(F32)
            ext = jnp.concatenate([dc, carry[half]], axis=0)
            n1 = pltpu.roll(ext, n_ext - 1, 0)[:tm]
            n2 = pltpu.roll(ext, n_ext - 2, 0)[:tm]
            dcb_ref[half] += jnp.sum(dc, axis=0, keepdims=True)
            dcw_ref[half] += jnp.concatenate(
                [jnp.sum(n2 * h0, axis=0, keepdims=True), jnp.sum(n1 * h0, axis=0, keepdims=True),
                 jnp.sum(dc * h0, axis=0, keepdims=True)], axis=0)
            dh_ref[half] = (w[2:3] * dc + w[1:2] * n1 + w[0:1] * n2).astype(BF16)
            carry[half] = dc[:CONV_HALO]

    rev = lambda ip: n_i - 1 - ip
    tile = pl.BlockSpec((2, tm, tn), lambda j, ip: (0, rev(ip), j))
    return _call(
        body, name=name, grid=(n_j, n_i),
        in_specs=[pl.BlockSpec((tm, D), lambda j, ip: (rev(ip), 0)),
                  pl.BlockSpec((None, tn, D), lambda j, ip: (0, j, 0)), tile, tile,
                  pl.BlockSpec((None, 3, tn), lambda j, ip: (l, 0, j)),
                  pl.BlockSpec((None, 3, tn), lambda j, ip: (l, 0, n_j + j))],
        out_specs=[tile, pl.BlockSpec((2, 3, tn), lambda j, ip: (0, 0, j)),
                   pl.BlockSpec((2, 1, tn), lambda j, ip: (0, 0, j))],
        out_shape=[jax.ShapeDtypeStruct((2, T, F), BF16), jax.ShapeDtypeStruct((2, 3, F), F32),
                   jax.ShapeDtypeStruct((2, 1, F), F32)],
        scratch=[pltpu.VMEM((2, CONV_HALO, tn), F32)], carry=carry)(db16, wdn, h, hc, cw, cw)


def _pool_fwd(ext, xb_g, t_glob, win):
    e = ext
    sft = 1
    while sft < win:
        e = e + pltpu.roll(e, sft, 0)
        sft *= 2
    cnt = jnp.minimum(t_glob + 1.0, float(win))
    return e[POOL_HALO:] / cnt - xb_g


def gating_fwd(h0, lg, lb, ws, bsT, wp, sc, *, name, carry=()):
    T = h0.shape[0]
    DA = lg.shape[-1]
    DB = sc.shape[-1]
    HA = DA // A_HEAD
    G = len(B_WINDOWS)
    CG = DB // G
    tm = _tile(T, 512, A_CHUNK)
    n_c = tm // A_CHUNK

    def body(h_ref, halo_ref, lg_ref, lb_ref, ws_ref, bsT_ref, wp_ref, sc_ref, cat_ref):
        i = pl.program_id(0)
        hu = h_ref[:, 0:DA].astype(F32)
        hv = h_ref[:, DA:2 * DA].astype(F32)
        xb = h_ref[:, 2 * DA:].astype(F32)
        u = _gelu(hu)
        vn, _, _ = _ln_fwd(_gelu(hv), lg_ref[...], lb_ref[...])
        vnb = vn.astype(BF16)
        rr = lax.broadcasted_iota(jnp.int32, (A_CHUNK, A_CHUNK), 0)
        cc = lax.broadcasted_iota(jnp.int32, (A_CHUNK, A_CHUNK), 1)
        for hh in range(HA):
            wt = jnp.where(rr >= cc, ws_ref[hh], 0.0).astype(BF16)
            cs = slice(hh * A_HEAD, (hh + 1) * A_HEAD)
            for n in range(n_c):
                rs = slice(n * A_CHUNK, (n + 1) * A_CHUNK)
                s = _dot(wt, vnb[rs, cs]) + bsT_ref[:, hh:hh + 1]
                cat_ref[rs, cs] = (u[rs, cs] * s).astype(BF16)
        halo = jnp.where(i > 0, halo_ref[...].astype(F32), 0.0)
        ext = jnp.concatenate([halo, xb], axis=0)
        t_glob = (i * tm + lax.broadcasted_iota(jnp.int32, (tm, 1), 0)).astype(F32)
        for g, win in enumerate(B_WINDOWS):
            gs = slice(g * CG, (g + 1) * CG)
            p = _pool_fwd(ext[:, gs], xb[:, gs], t_glob, win)
            z = _dot(p.astype(BF16), wp_ref[g])
            cat_ref[:, DA + g * CG:DA + (g + 1) * CG] = (z * sc_ref[:, gs]).astype(BF16)

    full = lambda a: pl.BlockSpec(a.shape, lambda i: (0,) * a.ndim)
    hpb = tm // POOL_HALO
    return _call(
        body, name=name, grid=(T // tm,),
        in_specs=[pl.BlockSpec((tm, 2 * DA + DB), lambda i: (i, 0)),
                  pl.BlockSpec((POOL_HALO, DB), lambda i: (jnp.maximum(i * hpb - 1, 0), 2 * DA // DB)),
                  full(lg), full(lb), full(ws), full(bsT), full(wp), full(sc)],
        out_specs=pl.BlockSpec((tm, DA + DB), lambda i: (i, 0)),
        out_shape=jax.ShapeDtypeStruct((T, DA + DB), BF16), carry=carry)(h0, h0, lg, lb, ws, bsT, wp, sc)


def gating_bwd(h0, dcat, lg, lb, ws, wsT, bsT, wp, sc, *, name, carry=()):
    T = h0.shape[0]
    DA = lg.shape[-1]
    DB = sc.shape[-1]
    HA = DA // A_HEAD
    G = len(B_WINDOWS)
    CG = DB // G
    tm = _tile(T, 512, A_CHUNK)
    n_i = T // tm
    n_c = tm // A_CHUNK
    n_ext = tm + POOL_HALO

    def body(h_ref, halo_ref, dc_ref, dhalo_ref, lg_ref, lb_ref, ws_ref, wsT_ref, bsT_ref, wp_ref, sc_ref,
             dh_ref, dws_ref, dbsT_ref, dlg_ref, dlb_ref, dsc_ref, dwp_ref, dvn_sc):
        i = pl.program_id(0)

        @pl.when(i == 0)
        def _():
            for r in (dws_ref, dbsT_ref, dlg_ref, dlb_ref, dsc_ref, dwp_ref):
                r[...] = jnp.zeros_like(r)

        hu = h_ref[:, 0:DA].astype(F32)
        hv = h_ref[:, DA:2 * DA].astype(F32)
        xb = h_ref[:, 2 * DA:].astype(F32)
        u = _gelu(hu)
        gu = _gelu_grad(hu)
        lgv = lg_ref[...]
        vn, vhat, rstd = _ln_fwd(_gelu(hv), lgv, lb_ref[...])
        vnb = vn.astype(BF16)
        rr = lax.broadcasted_iota(jnp.int32, (A_CHUNK, A_CHUNK), 0)
        cc = lax.broadcasted_iota(jnp.int32, (A_CHUNK, A_CHUNK), 1)
        for hh in range(HA):
            wt = jnp.where(rr >= cc, ws_ref[hh], 0.0).astype(BF16)
            wtT = jnp.where(rr <= cc, wsT_ref[hh], 0.0).astype(BF16)
            cs = slice(hh * A_HEAD, (hh + 1) * A_HEAD)
            dws = jnp.zeros((A_CHUNK, A_CHUNK), F32)
            dbs = jnp.zeros((A_CHUNK, 1), F32)
            for n in range(n_c):
                rs = slice(n * A_CHUNK, (n + 1) * A_CHUNK)
                vb = vnb[rs, cs]
                s = _dot(wt, vb) + bsT_ref[:, hh:hh + 1]
                dya = dc_ref[rs, cs].astype(F32)
                ds = dya * u[rs, cs]
                dh_ref[rs, cs] = (dya * s * gu[rs, cs]).astype(BF16)
                dsb = ds.astype(BF16)
                dbs = dbs + jnp.sum(ds, axis=1, keepdims=True)
                dws = dws + _dot_nt(dsb, vb)
                dvn_sc[rs, cs] = _dot(wtT, dsb)
            dws_ref[hh] += jnp.where(rr >= cc, dws, 0.0)
            dbsT_ref[:, hh:hh + 1] += dbs
        dvg, dlg, dlb = _ln_bwd(dvn_sc[...], vhat, rstd, lgv)
        dlg_ref[...] += dlg
        dlb_ref[...] += dlb
        dh_ref[:, DA:2 * DA] = (dvg * _gelu_grad(hv)).astype(BF16)

        halo = jnp.where(i > 0, halo_ref[...].astype(F32), 0.0)
        ext = jnp.concatenate([halo, xb], axis=0)
        t_glob = (i * tm + lax.broadcasted_iota(jnp.int32, (tm, 1), 0)).astype(F32)
        t_ext = (i * tm + lax.broadcasted_iota(jnp.int32, (n_ext, 1), 0)).astype(F32)
        dyb = dc_ref[:, DA:].astype(F32)
        dhalo = jnp.where(i < n_i - 1, dhalo_ref[...].astype(F32), 0.0)
        dyb_ext = jnp.concatenate([dyb, dhalo], axis=0)
        for g, win in enumerate(B_WINDOWS):
            gs = slice(g * CG, (g + 1) * CG)
            pb = _pool_fwd(ext[:, gs], xb[:, gs], t_glob, win).astype(BF16)
            wpg = wp_ref[g]
            z = _dot(pb, wpg)
            dsc_ref[:, gs] += jnp.sum(dyb[:, gs] * z, axis=0, keepdims=True)
            dzb = (dyb_ext[:, gs] * sc_ref[:, gs]).astype(BF16)
            dwp_ref[g] += _dot_tn(pb, dzb[:tm])
            dp = _dot_nt(dzb, wpg)
            e = dp / jnp.minimum(t_ext + 1.0, float(win))
            sft = 1
            while sft < win:
                e = e + pltpu.roll(e, n_ext - sft, 0)
                sft *= 2
            dh_ref[:, 2 * DA + g * CG:2 * DA + (g + 1) * CG] = (e[:tm] - dp[:tm]).astype(BF16)

    full = lambda a: pl.BlockSpec(a.shape, lambda i: (0,) * a.ndim)
    hpb = tm // POOL_HALO
    n_hb = T // POOL_HALO
    outs = [jax.ShapeDtypeStruct((T, 2 * DA + DB), BF16), jax.ShapeDtypeStruct(ws.shape, F32),
            jax.ShapeDtypeStruct(bsT.shape, F32), jax.ShapeDtypeStruct(lg.shape, F32),
            jax.ShapeDtypeStruct(lb.shape, F32), jax.ShapeDtypeStruct(sc.shape, F32),
            jax.ShapeDtypeStruct(wp.shape, F32)]
    return _call(
        body, name=name, grid=(n_i,),
        in_specs=[pl.BlockSpec((tm, 2 * DA + DB), lambda i: (i, 0)),
                  pl.BlockSpec((POOL_HALO, DB), lambda i: (jnp.maximum(i * hpb - 1, 0), 2 * DA // DB)),
                  pl.BlockSpec((tm, DA + DB), lambda i: (i, 0)),
                  pl.BlockSpec((POOL_HALO, DB), lambda i: (jnp.minimum((i + 1) * hpb, n_hb - 1), DA // DB)),
                  full(lg), full(lb), full(ws), full(wsT), full(bsT), full(wp), full(sc)],
        out_specs=[pl.BlockSpec((tm, 2 * DA + DB), lambda i: (i, 0))] + [full(o) for o in outs[1:]],
        out_shape=outs,
        scratch=[pltpu.VMEM((tm, DA), F32)], carry=carry)(h0, h0, dcat, dcat, lg, lb, ws, wsT, bsT, wp, sc)


HP = 2 * C_HEAD


def _hgrn_gates(h_ref, rows, lbv, tri):
    pre = []
    for r in rows:
        blk = h_ref[r, :].astype(F32)
        q = blk[:, 0:HP]
        sg = _sigmoid(blk[:, HP:2 * HP])
        f = lbv + (1.0 - lbv) * sg
        pre.append(dict(q=q, sq=_sigmoid(q), sg=sg, f=f, k=1.0 - f, lf=jnp.log(f), v=blk[:, 2 * HP:3 * HP],
                        gg=blk[:, 3 * HP:4 * HP]))
    bcums = [_exact_tri_dot(tri, p["lf"]) for p in pre]
    out = []
    for p, bcum in zip(pre, bcums):
        blast = bcum[C_CHUNK - 1:C_CHUNK]
        e_in = jnp.exp(bcum)
        e_out = jnp.exp(-bcum)
        e_end = jnp.exp(blast - bcum)
        out.append(dict(p, e_in=e_in, e_out=e_out, e_end=e_end, qd=p["q"] * p["sq"] * e_in, kd=p["k"] * e_out,
                        ke=p["k"] * e_end, dec=jnp.exp(blast)))
    return out


def _lower_bound(lbp_ref):
    p0, p1 = lbp_ref[0:1], lbp_ref[1:2]
    mx = jnp.maximum(p0, p1)
    e0, e1 = jnp.exp(p0 - mx), jnp.exp(p1 - mx)
    return e1 / (e0 + e1)


def hgrn_fwd(h1p, lbp, gn, *, name, carry=()):
    T = h1p.shape[0]
    DC = gn.shape[-1]
    n_p = DC // HP
    tt = _tile(T, 1024, C_CHUNK)
    n_c = tt // C_CHUNK

    def body(h_ref, lbp_ref, gn_ref, y_ref, o_ref, sp_ref, st):
        i = pl.program_id(1)

        @pl.when(i == 0)
        def _():
            st[...] = jnp.zeros_like(st)

        lbv = _lower_bound(lbp_ref)
        gnv = gn_ref[...]
        rr = lax.broadcasted_iota(jnp.int32, (C_CHUNK, C_CHUNK), 0)
        cc = lax.broadcasted_iota(jnp.int32, (C_CHUNK, C_CHUNK), 1)
        causal = rr >= cc
        tri = jnp.where(causal, 1.0, 0.0).astype(BF16)

        heads = [slice(hd * C_HEAD, (hd + 1) * C_HEAD) for hd in range(2)]
        rows = [slice(n * C_CHUNK, (n + 1) * C_CHUNK) for n in range(n_c)]
        nh = [(n, hd) for n in range(n_c) for hd in range(2)]
        gates = _hgrn_gates(h_ref, rows, lbv, tri)
        b16 = lambda key: {(n, hd): gates[n][key][:, heads[hd]].astype(BF16) for n, hd in nh}
        qd, kd, ke, vb = b16("qd"), b16("kd"), b16("ke"), b16("v")
        att = {i: jnp.where(causal, _dot_nt(qd[i], kd[i]), 0.0).astype(BF16) for i in nh}
        intra = {i: _dot(att[i], vb[i]) for i in nh}
        upd = {i: _dot_tn(vb[i], ke[i]) for i in nh}
        s = [st[0], st[1]]
        entering = {}
        for n, hd in nh:
            entering[n, hd] = s[hd]
            sp_ref[hd, n] = s[hd]
            s[hd] = gates[n]["dec"][:, heads[hd]] * s[hd] + upd[n, hd]
        st[0], st[1] = s
        o = {i: intra[i] + _dot_nt(qd[i], entering[i].astype(BF16)) for i in nh}
        for n in range(n_c):
            os_ = [o[n, 0], o[n, 1]]
            o_ref[rows[n], :] = jnp.concatenate(os_, axis=1).astype(BF16)
            ys = [oh * lax.rsqrt(jnp.mean(oh * oh, axis=-1, keepdims=True) + LN_EPS) for oh in os_]
            y_ref[rows[n], :] = (jnp.concatenate(ys, axis=1) * gnv * _sigmoid(gates[n]["gg"])).astype(BF16)

    return _call(
        body, name=name, grid=(n_p, T // tt),
        in_specs=[pl.BlockSpec((tt, 4 * HP), lambda p, i: (i, p)), pl.BlockSpec((2, HP), lambda p, i: (0, p)),
                  pl.BlockSpec((1, HP), lambda p, i: (0, p))],
        out_specs=[pl.BlockSpec((tt, HP), lambda p, i: (i, p)), pl.BlockSpec((tt, HP), lambda p, i: (i, p)),
                   pl.BlockSpec((2, n_c, C_HEAD, C_HEAD), lambda p, i: (p, i, 0, 0))],
        out_shape=[jax.ShapeDtypeStruct((T, DC), BF16), jax.ShapeDtypeStruct((T, DC), BF16),
                   jax.ShapeDtypeStruct((2 * n_p, T // C_CHUNK, C_HEAD, C_HEAD), F32)],
        scratch=[pltpu.VMEM((2, C_HEAD, C_HEAD), F32)], carry=carry)(h1p, lbp, gn)


def hgrn_bwd(h1p, o_saved, dy, sp, lbp, gn, *, name, carry=()):
    T = h1p.shape[0]
    DC = gn.shape[-1]
    n_p = DC // HP
    tt = _tile(T, 512, C_CHUNK)
    n_i = T // tt
    n_c = tt // C_CHUNK

    def body(h_ref, o_ref, dy_ref, sp_ref, lbp_ref, gn_ref, dh_ref, dgn_ref, dlbp_ref, dst, dlb_acc):
        ip = pl.program_id(1)

        @pl.when(ip == 0)
        def _():
            dst[...] = jnp.zeros_like(dst)
            dlb_acc[...] = jnp.zeros_like(dlb_acc)
            dgn_ref[...] = jnp.zeros_like(dgn_ref)

        lbv = _lower_bound(lbp_ref)
        gnv = gn_ref[...]
        rr = lax.broadcasted_iota(jnp.int32, (C_CHUNK, C_CHUNK), 0)
        cc = lax.broadcasted_iota(jnp.int32, (C_CHUNK, C_CHUNK), 1)
        causal = rr >= cc
        tri = jnp.where(causal, 1.0, 0.0).astype(BF16)
        tri_t = jnp.where(rr <= cc, 1.0, 0.0).astype(BF16)
        last_row = lax.broadcasted_iota(jnp.int32, (C_CHUNK, 1), 0) == C_CHUNK - 1

        heads = [slice(hd * C_HEAD, (hd + 1) * C_HEAD) for hd in range(2)]
        rows = [slice(n * C_CHUNK, (n + 1) * C_CHUNK) for n in range(n_c)]
        cat = lambda parts: jnp.concatenate(parts, axis=1)
        nh = [(n, hd) for n in range(n_c) for hd in range(2)]
        pair = lambda d, n: cat([d[n, 0], d[n, 1]])
        gates = _hgrn_gates(h_ref, rows, lbv, tri)
        dgn = jnp.zeros((1, HP), F32)
        dgg, dob = [], {}
        for n in range(n_c):
            o = o_ref[rows[n], :].astype(F32)
            dyv = dy_ref[rows[n], :].astype(F32)
            sgg = _sigmoid(gates[n]["gg"])
            rrs = [lax.rsqrt(jnp.mean(o[:, cs] * o[:, cs], axis=-1, keepdims=True) + LN_EPS) for cs in heads]
            ohat = cat([o[:, cs] * r for cs, r in zip(heads, rrs)])
            dyn = dyv * sgg
            dgg.append(dyv * ohat * gnv * sgg * (1.0 - sgg))
            dgn = dgn + jnp.sum(dyn * ohat, axis=0, keepdims=True)
            dxh = dyn * gnv
            for hd, cs in enumerate(heads):
                dxh_h, oh_h = dxh[:, cs], ohat[:, cs]
                dob[n, hd] = (rrs[hd] * (dxh_h - oh_h * jnp.mean(dxh_h * oh_h, axis=-1, keepdims=True))).astype(BF16)
        dgn_ref[...] += dgn
        b16 = lambda key: {(n, hd): gates[n][key][:, heads[hd]].astype(BF16) for n, hd in nh}
        qd, kd, ke, vb = b16("qd"), b16("kd"), b16("ke"), b16("v")
        s_in = {(n, hd): sp_ref[hd, n] for n, hd in nh}
        att = {i: jnp.where(causal, _dot_nt(qd[i], kd[i]), 0.0).astype(BF16) for i in nh}
        datt = {i: jnp.where(causal, _dot_nt(dob[i], vb[i]), 0.0).astype(BF16) for i in nh}
        grow = {i: _dot_tn(dob[i], qd[i]) for i in nh}
        dv_i = {i: _dot_tn(att[i], dob[i]) for i in nh}
        dqd = {i: _dot(datt[i], kd[i]) + _dot(dob[i], s_in[i].astype(BF16)) for i in nh}
        dkd = {i: _dot_tn(datt[i], qd[i]) for i in nh}
        ds = [dst[0], dst[1]]
        leaving = {}
        for n in reversed(range(n_c)):
            for hd, cs in enumerate(heads):
                leaving[n, hd] = ds[hd]
                ds[hd] = gates[n]["dec"][:, cs] * ds[hd] + grow[n, hd]
        dst[0], dst[1] = ds
        dsb = {i: leaving[i].astype(BF16) for i in nh}
        dv = {i: dv_i[i] + _dot_nt(ke[i], dsb[i]) for i in nh}
        dke = {i: _dot(vb[i], dsb[i]) for i in nh}
        ddec = {i: jnp.sum(leaving[i] * s_in[i], axis=0, keepdims=True) for i in nh}
        mid = []
        for n in range(n_c):
            a = gates[n]
            dqd_n, dkd_n, dke_n = pair(dqd, n), pair(dkd, n), pair(dke, n)
            kek = dke_n * a["ke"]
            dblast = jnp.sum(kek, axis=0, keepdims=True) + pair(ddec, n) * a["dec"]
            dbcum = dqd_n * a["qd"] - dkd_n * a["kd"] - kek + jnp.where(last_row, dblast, 0.0)
            mid.append((dqd_n * a["e_in"], dkd_n * a["e_out"] + dke_n * a["e_end"], dbcum))
        dlf = [_exact_tri_dot(tri_t, m[2]) for m in mid]
        dlb = jnp.zeros((1, HP), F32)
        for n in range(n_c):
            a = gates[n]
            dqs, dk, _ = mid[n]
            df = dlf[n] / a["f"] - dk
            dlb = dlb + jnp.sum(df * (1.0 - a["sg"]), axis=0, keepdims=True)
            dfl = df * (1.0 - lbv) * a["sg"] * (1.0 - a["sg"])
            dq = dqs * a["sq"] * (1.0 + a["q"] * (1.0 - a["sq"]))
            dh_ref[rows[n], :] = cat([dq, dfl, pair(dv, n), dgg[n]]).astype(BF16)
        dlb_acc[...] += dlb

        @pl.when(ip == n_i - 1)
        def _():
            d1 = dlb_acc[...] * lbv * (1.0 - lbv)
            dlbp_ref[...] = jnp.concatenate([-d1, d1], axis=0)

    rev = lambda ip: n_i - 1 - ip
    return _call(
        body, name=name, grid=(n_p, n_i),
        in_specs=[pl.BlockSpec((tt, 4 * HP), lambda p, ip: (rev(ip), p)),
                  pl.BlockSpec((tt, HP), lambda p, ip: (rev(ip), p)),
                  pl.BlockSpec((tt, HP), lambda p, ip: (rev(ip), p)),
                  pl.BlockSpec((2, n_c, C_HEAD, C_HEAD), lambda p, ip: (p, rev(ip), 0, 0)),
                  pl.BlockSpec((2, HP), lambda p, ip: (0, p)), pl.BlockSpec((1, HP), lambda p, ip: (0, p))],
        out_specs=[pl.BlockSpec((tt, 4 * HP), lambda p, ip: (rev(ip), p)),
                   pl.BlockSpec((1, HP), lambda p, ip: (0, p)), pl.BlockSpec((2, HP), lambda p, ip: (0, p))],
        out_shape=[jax.ShapeDtypeStruct(h1p.shape, BF16), jax.ShapeDtypeStruct((1, DC), F32),
                   jax.ShapeDtypeStruct((2, DC), F32)],
        scratch=[pltpu.VMEM((2, C_HEAD, C_HEAD), F32), pltpu.VMEM((1, HP), F32)], carry=carry)(h1p, o_saved, dy, sp, lbp, gn)


def loss_bwd(xh, rstd, g, b, target, *, name):
    T, N = xh.shape
    tm = _tile(T, 512, 8)

    def body(xh_ref, rs_ref, g_ref, b_ref, t_ref, ls_ref, dr_ref, drb_ref, dg_ref, db_ref):
        i = pl.program_id(0)

        @pl.when(i == 0)
        def _():
            for r in (ls_ref, dg_ref, db_ref):
                r[...] = jnp.zeros_like(r)

        xhv, gv = xh_ref[...], g_ref[...]
        e = xhv * gv + b_ref[...] - t_ref[...]
        ls_ref[...] += 0.5 * jnp.sum(jnp.mean(e * e, axis=-1, keepdims=True), axis=0, keepdims=True)
        dr, dg, db = _ln_bwd(e / N, xhv, rs_ref[...], gv)
        dr_ref[...] = dr
        drb_ref[...] = dr.astype(BF16)
        dg_ref[...] += dg
        db_ref[...] += db

    row = pl.BlockSpec((tm, N), lambda i: (i, 0))
    vec = pl.BlockSpec((1, N), lambda i: (0, 0))
    return _call(body, name=name, grid=(T // tm,),
                 in_specs=[row, pl.BlockSpec((tm, 1), lambda i: (i, 0)), vec, vec, row],
                 out_specs=[pl.BlockSpec((1, LANES), lambda i: (0, 0)), row, row, vec, vec],
                 out_shape=[jax.ShapeDtypeStruct((1, LANES), F32), jax.ShapeDtypeStruct((T, N), F32),
                            jax.ShapeDtypeStruct((T, N), BF16), jax.ShapeDtypeStruct((1, N), F32),
                            jax.ShapeDtypeStruct((1, N), F32)])(xh, rstd, g, b, target)


def _mesh_pos():
    x, y, c = lax.axis_index("x"), lax.axis_index("y"), lax.axis_index("c")
    chips = [(1 - x, y), (x, 1 - y), (1 - x, 1 - y)]
    return x, y, c, chips


def _remote(src, dst, send, recv, j, dev):
    return pltpu.make_async_remote_copy(src_ref=src, dst_ref=dst, send_sem=send.at[j], recv_sem=recv.at[j],
                                        device_id=dev, device_id_type=MESH)


def mesh_ids():
    x, y, c, chips = _mesh_pos()
    return jnp.stack([c] + [2 * cx + cy for cx, cy in chips] + [2 * x + y]).astype(jnp.int32)


def _sibling():
    return (lax.axis_index("x"), lax.axis_index("y"), 1 - lax.axis_index("c"))


SWAP_PARTS = 2


def _swap_with_sibling(src, recv, send_sem, recv_sem, step):
    slot = step % 2
    br = src.shape[0]
    n = SWAP_PARTS if br % (16 * SWAP_PARTS) == 0 else 1
    parts = []
    for k in range(n):
        rows = pl.ds(k * (br // n), br // n)
        cp = pltpu.make_async_remote_copy(
            src_ref=src.at[rows], dst_ref=recv.at[slot, rows], send_sem=send_sem.at[slot * SWAP_PARTS + k],
            recv_sem=recv_sem.at[slot * SWAP_PARTS + k], device_id=_sibling(), device_id_type=MESH)
        cp.start()
        parts.append((cp, slot, rows))
    return parts


def _swap_scratch(br, C, dtype):
    return [pltpu.VMEM((2, br, C), dtype), pltpu.SemaphoreType.DMA((2 * SWAP_PARTS,)),
            pltpu.SemaphoreType.DMA((2 * SWAP_PARTS,))]


def _swap_rows(Rh, C, itemsize):
    return _tile(Rh, max(16, (3 << 20) // (C * itemsize)), 16)


def cast_to_slot(a3, l, me1, name):
    _, R, C = a3.shape
    br = _tile(R, max(8, (1 << 20) // C), 16)

    def body(me_ref, a_ref, o_ref):
        o_ref[...] = a_ref[...].astype(BF16)

    return _call(body, name=name, grid=(R // br,), prefetch=1,
                 in_specs=[pl.BlockSpec((None, br, C), lambda r, me: (l, r, 0))],
                 out_specs=pl.BlockSpec((None, None, br, C), lambda r, me: (0, me[0], r, 0)),
                 out_shape=jax.ShapeDtypeStruct((1, 4, R, C), BF16))(me1, a3)


def all_gather_chips(big, small, *, name):
    nb, ns = len(big), len(small)
    layers = [(t, l) for t in range(nb) for l in range(big[t].shape[0])]
    n_big = 3 * len(layers)
    n_rem = n_big + 3 * ns

    def body(*refs):
        small_in = refs[nb:nb + ns]
        bufs, small_out = refs[nb + ns:2 * nb + ns], refs[2 * nb + ns:2 * (nb + ns)]
        send, recv, loc = refs[2 * (nb + ns):]
        x, y, c, chips = _mesh_pos()
        me = 2 * x + y
        ids = [2 * cx + cy for cx, cy in chips]
        started, sends = [], []
        for t in range(ns):
            cp = pltpu.make_async_copy(small_in[t], small_out[t].at[me], loc.at[t])
            cp.start()
            started.append(cp)
        for q, (t, l) in enumerate(layers):
            for k, chip in enumerate(chips):
                blk = bufs[t].at[l, me, c]
                cp = _remote(blk, blk, send, recv, 3 * q + k, (*chip, c))
                cp.start()
                sends.append(cp)
        for t in range(ns):
            for k, chip in enumerate(chips):
                cp = _remote(small_in[t], small_out[t].at[me], send, recv, n_big + 3 * t + k, (*chip, c))
                cp.start()
                sends.append(cp)
        for q, (t, l) in enumerate(layers):
            for k in range(3):
                blk = bufs[t].at[l, ids[k], c]
                _remote(blk, blk, send, recv, 3 * q + k, (x, y, c)).wait_recv()
        for t in range(ns):
            for k in range(3):
                blk = small_out[t].at[ids[k]]
                _remote(blk, blk, send, recv, n_big + 3 * t + k, (x, y, c)).wait_recv()
        for cp in sends:
            cp.wait_send()
        for cp in started:
            cp.wait()

    out_shape = [jax.ShapeDtypeStruct(a.shape, a.dtype) for a in big]
    out_shape += [jax.ShapeDtypeStruct((4,) + a.shape, a.dtype) for a in small]
    return _call(body, name=name, in_specs=[ANY] * (nb + ns), out_specs=[ANY] * (nb + ns), out_shape=out_shape,
                 aliases={t: t for t in range(nb)},
                 scratch=[pltpu.SemaphoreType.DMA((n_rem,)), pltpu.SemaphoreType.DMA((n_rem,)),
                          pltpu.SemaphoreType.DMA((max(ns, 1),))])(*big, *small)


def all_gather_pair(buf, ids, *, name):
    L, _, _, Rh, C = buf.shape
    br = _swap_rows(Rh, C, 2)
    n_r = Rh // br

    def body(ids_ref, in_ref, o_ref, recv, ssem, rsem):
        step = (pl.program_id(0) * 3 + pl.program_id(1)) * n_r + pl.program_id(2)
        parts = _swap_with_sibling(in_ref, recv, ssem, rsem, step)
        for cp, slot, rows in parts:
            cp.wait_recv()
            o_ref[rows, :] = recv[slot, rows, :]
        for cp, _, _ in parts:
            cp.wait_send()

    at = lambda l, s, h, r: (((l * 4 + s) * 2 + h) * n_r + r, 0)
    out = _call(body, name=name, grid=(L, 3, n_r), prefetch=1,
                in_specs=[pl.BlockSpec((br, C), lambda l, k, r, ids: at(l, ids[1 + k], ids[0], r))],
                out_specs=pl.BlockSpec((br, C), lambda l, k, r, ids: at(l, ids[1 + k], 1 - ids[0], r)),
                out_shape=jax.ShapeDtypeStruct((L * 8 * Rh, C), buf.dtype), aliases={1: 0},
                scratch=_swap_scratch(br, C, BF16))(ids, buf.reshape(L * 8 * Rh, C))
    return out.reshape(buf.shape)


def rs_pair_add(grad, ids, *, name):
    _, _, Rh, C = grad.shape
    br = _swap_rows(Rh, C, 2)
    n_r = Rh // br

    def body(ids_ref, send_ref, keep_ref, pb_ref, own_ref, recv, ssem, rsem):
        ph = pl.program_id(0)
        parts = _swap_with_sibling(send_ref, recv, ssem, rsem, ph * n_r + pl.program_id(1))
        for cp, slot, rows in parts:
            cp.wait_recv()
            s = keep_ref[rows, :].astype(F32) + recv[slot, rows, :].astype(F32)

            @pl.when(ph == 0)
            def _():
                own_ref[rows, :] = s

            @pl.when(ph > 0)
            def _():
                pb_ref[rows, :] = s.astype(BF16)

        for cp, _, _ in parts:
            cp.wait_send()

    rel = lambda ph: (ph + 3) % 4
    at = lambda s, h, r: ((s * 2 + h) * n_r + r, 0)
    grad = grad.reshape(8 * Rh, C)
    return _call(
        body, name=name, grid=(4, n_r), prefetch=1,
        in_specs=[pl.BlockSpec((br, C), lambda ph, r, ids: at(ids[1 + rel(ph)], 1 - ids[0], r)),
                  pl.BlockSpec((br, C), lambda ph, r, ids: at(ids[1 + rel(ph)], ids[0], r))],
        out_specs=[pl.BlockSpec((None, br, C), lambda ph, r, ids: (jnp.maximum(ph - 1, 0), jnp.where(ph == 0, 0, r), 0)),
                   pl.BlockSpec((br, C), lambda ph, r, ids: (jnp.where(ph == 0, r, n_r - 1), 0))],
        out_shape=[jax.ShapeDtypeStruct((3, Rh, C), BF16), jax.ShapeDtypeStruct((Rh, C), F32)],
        scratch=_swap_scratch(br, C, BF16))(ids, grad, grad)


def rs_finish(owns, gots, *, name):
    L = len(owns)
    Rh, C = owns[0].shape
    br = _swap_rows(Rh, C, 4)
    n_r = Rh // br

    def body(*refs):
        own_refs, got_refs, o_ref = refs[:L], refs[L:2 * L], refs[2 * L]
        recv, ssem, rsem = refs[2 * L + 1:]
        l = pl.program_id(0)
        c = lax.axis_index("c")
        for ll in range(L):
            @pl.when(l == ll)
            def _():
                s = own_refs[ll][...]
                for k in range(3):
                    s = s + got_refs[ll][k].astype(F32)
                o_ref[c] = s

        parts = _swap_with_sibling(o_ref.at[c], recv, ssem, rsem, l * n_r + pl.program_id(1))
        for cp, slot, rows in parts:
            cp.wait_recv()
            o_ref[1 - c, rows, :] = recv[slot, rows, :]
        for cp, _, _ in parts:
            cp.wait_send()

    def at_layer(ll):
        return lambda l, r: jnp.where(l == ll, r, jnp.where(l < ll, 0, n_r - 1))

    in_specs = [pl.BlockSpec((br, C), lambda l, r, ll=ll: (at_layer(ll)(l, r), 0)) for ll in range(L)]
    in_specs += [pl.BlockSpec((3, br, C), lambda l, r, ll=ll: (0, at_layer(ll)(l, r), 0)) for ll in range(L)]
    out = _call(body, name=name, grid=(L, n_r), in_specs=in_specs,
                out_specs=pl.BlockSpec((2, br, C), lambda l, r: (l, r, 0)),
                out_shape=jax.ShapeDtypeStruct((L * 2, Rh, C), F32),
                scratch=_swap_scratch(br, C, F32))(*owns, *gots)
    return out.reshape(L, 2, Rh, C)


SMALL_ROW_MULTIPLE = 64


def all_reduce_small(buf, *, name):
    rows = buf.shape[0]
    h, q, e = rows // 2, rows // 4, rows // 8

    def body(x_ref, o_ref, s1, r1, s2, r2, s3, r3, send, recv):
        x, y, c, _ = _mesh_pos()
        sib, xn, yn = (x, y, 1 - c), (1 - x, y, c), (x, 1 - y, c)
        at = lambda off, n: pl.ds(pl.multiple_of(off, 8), n)
        cp = _remote(x_ref.at[at((1 - c) * h, h)], r1, send, recv, 0, sib)
        cp.start()
        cp.wait()
        s1[...] = x_ref[at(c * h, h), :] + r1[...]
        cp = _remote(s1.at[at((1 - x) * q, q)], r2, send, recv, 1, xn)
        cp.start()
        cp.wait()
        s2[...] = s1[at(x * q, q), :] + r2[...]
        cp = _remote(s2.at[at((1 - y) * e, e)], r3, send, recv, 2, yn)
        cp.start()
        cp.wait()
        s3[...] = s2[at(y * e, e), :] + r3[...]
        mine = c * h + x * q + y * e
        o_ref[at(mine, e), :] = s3[...]
        theirs = o_ref.at[at(c * h + x * q + (1 - y) * e, e)]
        cp = _remote(s3, o_ref.at[at(mine, e)], send, recv, 3, yn)
        cp.start()
        cp.wait_send()
        _remote(theirs, theirs, send, recv, 3, yn).wait_recv()
        quarter = o_ref.at[at(c * h + x * q, q)]
        theirs = o_ref.at[at(c * h + (1 - x) * q, q)]
        cp = _remote(quarter, quarter, send, recv, 4, xn)
        cp.start()
        cp.wait_send()
        _remote(theirs, theirs, send, recv, 4, xn).wait_recv()
        half = o_ref.at[at(c * h, h)]
        theirs = o_ref.at[at((1 - c) * h, h)]
        cp = _remote(half, half, send, recv, 5, sib)
        cp.start()
        cp.wait_send()
        _remote(theirs, theirs, send, recv, 5, sib).wait_recv()

    part = lambda n: pltpu.VMEM((n, LANES), F32)
    return _call(body, name=name, in_specs=[VMEM_SPEC], out_specs=VMEM_SPEC,
                 out_shape=jax.ShapeDtypeStruct(buf.shape, F32),
                 scratch=[part(h), part(h), part(q), part(q), part(e), part(e), pltpu.SemaphoreType.DMA((6,)),
                          pltpu.SemaphoreType.DMA((6,))])(buf)


def _pack(arrs, row_multiple=8):
    parts = []
    for a in arrs:
        f = a.reshape(-1).astype(F32)
        parts.append(jnp.pad(f, (0, (-f.shape[0]) % PACK_ALIGN)))
    total = sum(p.shape[0] for p in parts)
    parts.append(jnp.zeros(((-total) % (row_multiple * LANES),), F32))
    return jnp.concatenate(parts).reshape(-1, LANES)


def _unpack(buf, shapes):
    flat = buf.reshape(-1)
    out, off = [], 0
    for s in shapes:
        n = math.prod(s)
        out.append(flat[off:off + n].reshape(s))
        off += n + (-n) % PACK_ALIGN
    return out


_WEIGHTS = ['ev_w_in', 'ev_ln_v_g', 'ev_ln_v_b', 'ev_w_s', 'ev_b_s', 'ev_w_pool', 'ev_pool_scale', 'ev_w_out',
            'od_w_in', 'od_norm_g', 'od_w_out', 'lb_param', 'ffn_w_up', 'ffn_conv_w', 'ffn_conv_b', 'ffn_w_down',
            'ln1_g', 'ln1_b', 'ln2_g', 'ln2_b']
_BIG = ['ev_w_in', 'ev_w_out', 'od_w_in', 'od_w_out', 'ffn_w_up', 'ffn_w_down']
_SMALL = [n for n in _WEIGHTS if n not in _BIG]


def kernel(x, ev_w_in, ev_ln_v_g, ev_ln_v_b, ev_w_s, ev_b_s, ev_w_pool, ev_pool_scale, ev_w_out, od_w_in, od_norm_g, od_w_out, lb_param, ffn_w_up, ffn_conv_w, ffn_conv_b, ffn_w_down, ln1_g, ln1_b, ln2_g, ln2_b, loss_target, m_ev_w_in, m_ev_ln_v_g, m_ev_ln_v_b, m_ev_w_s, m_ev_b_s, m_ev_w_pool, m_ev_pool_scale, m_ev_w_out, m_od_w_in, m_od_norm_g, m_od_w_out, m_lb_param, m_ffn_w_up, m_ffn_conv_w, m_ffn_conv_b, m_ffn_w_down, m_ln1_g, m_ln1_b, m_ln2_g, m_ln2_b, v_ev_w_in, v_ev_ln_v_g, v_ev_ln_v_b, v_ev_w_s, v_ev_b_s, v_ev_w_pool, v_ev_pool_scale, v_ev_w_out, v_od_w_in, v_od_norm_g, v_od_w_out, v_lb_param, v_ffn_w_up, v_ffn_conv_w, v_ffn_conv_b, v_ffn_w_down, v_ln1_g, v_ln1_b, v_ln2_g, v_ln2_b):
    given = dict(locals())
    w = {n: given[n] for n in _WEIGHTS}
    mom = {n: given["m_" + n] for n in _WEIGHTS}
    vel = {n: given["v_" + n] for n in _WEIGHTS}
    x2d = x[0]
    tgt = loss_target[0]
    T, D = x2d.shape
    DA = ev_ln_v_g.shape[-1]
    DB = ev_pool_scale.shape[-1]
    HA = ev_w_s.shape[1]
    G = len(B_WINDOWS)
    CG = DB // G
    DC = 4 * od_norm_g.shape[-1]
    F = ffn_conv_b.shape[-1] // 2
    chip = 2 * lax.axis_index("x") + lax.axis_index("y")

    ids = mesh_ids()
    halves = lambda a: a.reshape(1, 4, 2, a.shape[2] // 2, a.shape[3])
    slot = {(n, l): halves(cast_to_slot(w[n], l, ids[4:5], f"cast_{n}{l}"))
            for n in _BIG for l in range(w[n].shape[0])}

    def riding(*keys):
        return [IciCopy("gather", slot[k]) for k in keys]

    def pair(buf, key):
        g = all_gather_pair(buf, ids, name=f"all_gather_pair_{key[0]}{key[1]}")
        return g.reshape(1, 4, g.shape[3] * 2, g.shape[4])

    early = [('ev_w_in', 0)]
    gathered = all_gather_chips([slot[k] for k in early], [ev_w_pool[0], ffn_conv_w, od_norm_g],
                                name="all_gather_chips")
    wpool_full = gathered[1].transpose(1, 0, 2, 3).reshape(G, CG, CG)
    cw_full = gathered[2].transpose(1, 2, 0, 3).reshape(DEPTH, 3, 2 * F)
    gn_full = gathered[3].reshape(1, DC)
    win0 = pair(gathered[0], early[0])[0]
    wup, wdn = {}, {}
    rh_up = D // 2
    cut1 = (rh_up * 7 // 100) // 16 * 16
    cut2 = cut1 + (rh_up * 13 // 100) // 16 * 16
    cb3 = ffn_conv_b.reshape(DEPTH, 1, 2 * F)
    ws = ev_w_s[0]
    wsT = jnp.swapaxes(ws, 1, 2)
    bsT = ev_b_s[0].T
    wpb = wpool_full.astype(BF16)
    ones = jnp.ones((1, D), F32)
    zeros = jnp.zeros((1, D), F32)
    row = lambda a, l: a[l:l + 1]

    Ns0 = win0.shape[-1]
    Nu = ffn_w_up.shape[-1]
    tm_big = _tile(T, 1024, 8)
    tm_ln = _tile(T, 512, 8)
    tk_ln = _tile(D, 512)
    n_p = DC // HP

    def nat_spec(Ns, tnw):
        nps = Ns // tnw
        return pl.BlockSpec((1, D, tnw), lambda i, j: (j // nps, 0, j % nps))

    perm_spec = pl.BlockSpec((4, D, HP), lambda i, j: (0, 0, j))

    xb16 = cast_bf16(x, "cast_x")[0]
    up0 = IciCopy("gather", slot[('ffn_w_up', 0)], rows=(0, cut1))
    ride = riding(('ev_w_out', 0))
    h0 = mm_nn(xb16, win0, w_spec=nat_spec(Ns0, Ns0), P=1, tnw=Ns0, tm=tm_big, n_j=4, name="ev_in",
               carry=ride + [up0])
    wout0 = pair(ride[0].out, ('ev_w_out', 0)).reshape(DA + DB, D)
    up0 = IciCopy("gather", up0.out, rows=(cut1, cut2 - cut1))
    cat = gating_fwd(h0, ev_ln_v_g, ev_ln_v_b, ws, bsT, wpb, ev_pool_scale, name="gating_fwd", carry=[up0])
    up0 = IciCopy("gather", up0.out, rows=(cut2, rh_up - cut2))

    def mix_ln(a, wmat, res, l, name, carry=()):
        K = a.shape[1]
        tk = _tile(K, 2048)
        return mm_ln(a, wmat, *res, row(ln1_g, l), row(ln1_b, l), w_spec=pl.BlockSpec((tk, D), lambda i, k: (k, 0)),
                     K=K, tk=tk, tm=tm_ln, name=name, carry=carry)

    def ffn_down(f, res, l, carry=()):
        tk = F // 4 if (F // 4) % LANES == 0 else _tile(F, 512)
        return mm_ln(f, wdn[l], *res, row(ln2_g, l), row(ln2_b, l),
                     w_spec=pl.BlockSpec((None, tk, D), lambda i, k: (0, k, 0)), K=F, tk=tk, tm=tm_ln,
                     name=f"ffn_down{l}", carry=carry)

    xh1, y1, rs1 = mix_ln(cat, wout0, (x2d, ones, zeros), 0, "ev_out", carry=[up0])
    wup[0] = pair(up0.out, ('ffn_w_up', 0))
    res1 = (xh1, row(ln1_g, 0), row(ln1_b, 0))
    ride = riding(('ffn_w_down', 0), ('od_w_in', 0))
    hf0, hc0, f0 = ffn_up(y1, wup[0], cw_full, cb3, 0, name="ffn_up0", carry=ride)
    wdn[0] = pair(ride[0].out, ('ffn_w_down', 0)).reshape(1, F, D)
    win1 = pair(ride[1].out, ('od_w_in', 0))[0]
    cut1 = (rh_up * 40 // 100) // 16 * 16
    cut2 = 2 * cut1
    up1 = IciCopy("gather", slot[('ffn_w_up', 1)], rows=(0, cut1))
    ride = riding(('od_w_out', 0))
    xh2, y2, rs2 = ffn_down(f0, res1, 0, carry=ride + [up1])
    wout1 = pair(ride[0].out, ('od_w_out', 0)).reshape(DC, D)
    res2 = (xh2, row(ln2_g, 0), row(ln2_b, 0))
    up1 = IciCopy("gather", up1.out, rows=(cut1, cut2 - cut1))
    h1p = mm_nn(y2, win1, w_spec=perm_spec, P=4, tnw=HP, tm=tm_big, n_j=n_p, name="od_in", carry=[up1])
    up1 = IciCopy("gather", up1.out, rows=(cut2, rh_up - cut2))
    yh, o_saved, sp = hgrn_fwd(h1p, lb_param, gn_full, name="hgrn_fwd", carry=[up1])
    wup[1] = pair(up1.out, ('ffn_w_up', 1))
    xh3, y3, rs3 = mix_ln(yh, wout1, res2, 1, "od_out")
    res3 = (xh3, row(ln1_g, 1), row(ln1_b, 1))
    ride = riding(('ffn_w_down', 1))
    hf1, hc1, f1 = ffn_up(y3, wup[1], cw_full, cb3, 1, name="ffn_up1", carry=ride)
    wdn[1] = pair(ride[0].out, ('ffn_w_down', 1)).reshape(1, F, D)
    xh4, y4, rs4 = ffn_down(f1, res3, 1)
    loss_p, dr, drb, dg_ln2_1, db_ln2_1 = loss_bwd(xh4, rs4, row(ln2_g, 1), row(ln2_b, 1), tgt, name="loss_bwd")

    tt = _tile(T, 2048, 16)
    n_t = T // tt
    tnu = Nu // 2 if (Nu // 2) % LANES == 0 else Nu
    upb = Nu // tnu
    tkd = _tile(D, 1024)

    def pair_sum(g4, name):
        pb, own = rs_pair_add(g4.reshape(4, 2, g4.shape[1] // 2, g4.shape[2]), ids, name="rs_pair_add_" + name)
        return IciCopy("scatter", pb), own

    def g_out(a, gb, name, tkk=None):
        K = a.shape[1]
        tkk = tkk or _tile(K, 1024)
        return mm_tn(a, gb, a_spec=pl.BlockSpec((tt, tkk), lambda kb, nb, t: (t, kb)),
                     g_spec=pl.BlockSpec((tt, tkd), lambda kb, nb, t: (t, nb)),
                     o_spec=pl.BlockSpec((1, tkk, tkd), lambda kb, nb, t: (0, kb, nb)), out_shape=(1, K, D),
                     grid=(K // tkk, D // tkd, n_t), acc_shape=(tkk, tkd), P=1, tnw=tkd, name=name)

    def ffn_bwd(l, dr2, dr2b, f, hf, hc, y_in, xh_in, rs_in):
        g_dn = g_out(f, dr2b, f"g_ffn_down{l}", tkk=F // 4 if (F // 4) % LANES == 0 else None)
        rs_dn = pair_sum(g_dn.reshape(4, F // 4, D), f"ffn_down{l}")
        dh, dcw, dcb = ffn_dgate(dr2b, wdn[l], hf, hc, cw_full, l, name=f"ffn_dgate{l}", carry=[rs_dn[0]])
        g_up = mm_tn(y_in, dh, a_spec=pl.BlockSpec((tt, tkd), lambda kb, nb, t: (t, kb)),
                     g_spec=pl.BlockSpec((None, tt, tnu), lambda kb, nb, t: (nb // (2 * upb), t, nb % (2 * upb))),
                     o_spec=pl.BlockSpec((1, tkd, tnu), lambda kb, nb, t: (nb // upb, kb, nb % upb)),
                     out_shape=(4, D, Nu), grid=(D // tkd, 4 * upb, n_t), acc_shape=(tkd, tnu), P=1, tnw=tnu,
                     name=f"g_ffn_up{l}")
        rs_up = pair_sum(g_up, f"ffn_up{l}")
        tku = tnu
        kps = Nu // tku
        out = mm_nt_res(dh, wup[l], dr2, (xh_in, rs_in, row(ln1_g, l)),
                        a_spec=pl.BlockSpec((None, tm_ln, tku), lambda i, k: (k // (2 * kps), i, k % (2 * kps))),
                        w_spec=pl.BlockSpec((None, 1, D, tku), lambda i, k: (0, k // kps, 0, k % kps)),
                        P=1, tnw=tku, n_k=4 * kps, tm=tm_ln, name=f"d_ffn_in{l}", carry=[rs_up[0]])
        return rs_dn, rs_up, dcw, dcb, out

    rs_dn1, rs_up1, dcw1, dcb1, (dr1, dr1b, dg_ln1_1, db_ln1_1) = ffn_bwd(1, dr, drb, f1, hf1, hc1, y3, xh3, rs3)
    rs_wout1 = pair_sum(g_out(yh, dr1b, "g_od_out").reshape(4, DC // 4, D), "od_out")
    dyh = mm_nt_plain(dr1b, wout1, tm=tm_big, tn=_tile(DC, 512), name="d_od_out")
    dh1p, d_gn, d_lbp = hgrn_bwd(h1p, o_saved, dyh, sp, lb_param, gn_full, name="hgrn_bwd", carry=[rs_wout1[0]])
    g_win1 = mm_tn(y2, dh1p, a_spec=pl.BlockSpec((tt, tkd), lambda kb, nb, t: (t, kb)),
                   g_spec=pl.BlockSpec((tt, 4 * HP), lambda kb, nb, t: (t, nb)),
                   o_spec=pl.BlockSpec((4, tkd, HP), lambda kb, nb, t: (0, kb, nb)), out_shape=(4, D, DC),
                   grid=(D // tkd, n_p, n_t), acc_shape=(tkd, 4 * HP), P=4, tnw=HP, name="g_od_in")
    rs_win1 = pair_sum(g_win1, "od_in")
    dr, drb, dg_ln2_0, db_ln2_0 = mm_nt_res(
        dh1p, win1, dr1, (xh2, rs2, row(ln2_g, 0)), a_spec=pl.BlockSpec((tm_ln, 4 * HP), lambda i, k: (i, k)),
        w_spec=pl.BlockSpec((4, D, HP), lambda i, k: (0, 0, k)), P=4, tnw=HP, n_k=n_p, tm=tm_ln, name="d_od_in",
        carry=[rs_win1[0]])
    rs_dn0, rs_up0, dcw0, dcb0, (dr1, dr1b, dg_ln1_0, db_ln1_0) = ffn_bwd(0, dr, drb, f0, hf0, hc0, y1, xh1, rs1)
    rs_wout0 = pair_sum(g_out(cat, dr1b, "g_ev_out").reshape(4, (DA + DB) // 4, D), "ev_out")
    dcat = mm_nt_plain(dr1b, wout0, tm=tm_big, tn=_tile(DA + DB, 512), name="d_ev_out")
    dh0, d_ws, d_bsT, d_lg, d_lb, d_sc, d_wp = gating_bwd(h0, dcat, ev_ln_v_g, ev_ln_v_b, ws, wsT, bsT, wpb,
                                                          ev_pool_scale, name="gating_bwd", carry=[rs_wout0[0]])
    g_win0 = mm_tn(xb16, dh0, a_spec=pl.BlockSpec((tt, tkd), lambda kb, nb, t: (t, kb)),
                   g_spec=pl.BlockSpec((tt, Ns0), lambda kb, nb, t: (t, nb)),
                   o_spec=pl.BlockSpec((1, tkd, Ns0), lambda kb, nb, t: (nb, kb, 0)), out_shape=(4, D, Ns0),
                   grid=(D // tkd, 4, n_t), acc_shape=(tkd, Ns0), P=1, tnw=Ns0, name="g_ev_in")
    rs_win0 = pair_sum(g_win0, "ev_in")
    grad_x = mm_nt_res(dh0, win0, dr1, None, a_spec=pl.BlockSpec((tm_ln, Ns0), lambda i, k: (i, k)),
                       w_spec=pl.BlockSpec((1, D, Ns0), lambda i, k: (k, 0, 0)), P=1, tnw=Ns0, n_k=4, tm=tm_ln,
                       name="d_ev_in", carry=[rs_win0[0]])

    per_weight = [[rs_win0], [rs_wout0], [rs_win1], [rs_wout1], [rs_up0, rs_up1], [rs_dn0, rs_dn1]]
    shared = [rs_finish([own for _, own in m], [cp.out for cp, _ in m], name="rs_finish_" + n)
              for n, m in zip(_BIG, per_weight)]
    big_g = {n: s.reshape(w[n].shape) for n, s in zip(_BIG, shared)}

    small_full = {
        'ev_ln_v_g': d_lg, 'ev_ln_v_b': d_lb, 'ev_w_s': d_ws[None], 'ev_b_s': d_bsT.T[None], 'ev_w_pool': d_wp[None],
        'ev_pool_scale': d_sc, 'od_norm_g': d_gn, 'lb_param': d_lbp,
        'ffn_conv_w': jnp.stack([jnp.concatenate([dcw0[0], dcw0[1]], axis=-1),
                                 jnp.concatenate([dcw1[0], dcw1[1]], axis=-1)]),
        'ffn_conv_b': jnp.stack([jnp.concatenate([dcb0[0, 0], dcb0[1, 0]]), jnp.concatenate([dcb1[0, 0], dcb1[1, 0]])]),
        'ln1_g': jnp.concatenate([dg_ln1_0, dg_ln1_1]), 'ln1_b': jnp.concatenate([db_ln1_0, db_ln1_1]),
        'ln2_g': jnp.concatenate([dg_ln2_0, dg_ln2_1]), 'ln2_b': jnp.concatenate([db_ln2_0, db_ln2_1])}
    packed = _pack([small_full[n] for n in _SMALL] + [loss_p[0, 0:1]], SMALL_ROW_MULTIPLE)
    reduced = _unpack(all_reduce_small(packed, name="all_reduce_small"),
                      [small_full[n].shape for n in _SMALL] + [(1,)])
    small_g = dict(zip(_SMALL, reduced[:-1]))
    loss = reduced[-1][0]
    small_g['ev_w_pool'] = lax.dynamic_slice_in_dim(small_g['ev_w_pool'], chip * (CG // 4), CG // 4, axis=2)
    small_g['ffn_conv_w'] = lax.dynamic_slice_in_dim(small_g['ffn_conv_w'], chip * (F // 2), F // 2, axis=2)
    small_g['od_norm_g'] = lax.dynamic_slice_in_dim(small_g['od_norm_g'], chip * (DC // 4), DC // 4, axis=1)

    grads, delta, new_m, new_v = {}, {}, {}, {}
    for n in _BIG:
        grads[n], delta[n], new_m[n], new_v[n] = adamw(w[n], big_g[n], mom[n], vel[n], "adamw_" + n)
    upd = adamw_small(*[[d[n] for n in _SMALL] for d in (w, small_g, mom, vel)], "adamw_small")
    for d, outs in zip((grads, delta, new_m, new_v), upd):
        d.update(zip(_SMALL, outs))

    return (loss, grad_x[None], *[grads[n] for n in _WEIGHTS], *[delta[n] for n in _WEIGHTS],
            *[new_m[n] for n in _WEIGHTS], *[new_v[n] for n in _WEIGHTS])
```

```python
import math

import jax
import jax.numpy as jnp
from jax import lax
from jax.experimental import pallas as pl
from jax.experimental.pallas import tpu as pltpu

F32 = jnp.float32
BF16 = jnp.bfloat16
MESH = pl.DeviceIdType.MESH
ANY = pl.BlockSpec(memory_space=pl.ANY)
VMEM_SPEC = pl.BlockSpec(memory_space=pltpu.VMEM)

DEPTH = 2
ALPHA = (2 * DEPTH) ** 0.25
LN_EPS = 1e-5
A_HEAD = 128
A_CHUNK = 128
B_WINDOWS = (2, 4, 8, 16)
POOL_HALO = 16
C_HEAD = 128
C_CHUNK = 64
CONV_HALO = 8
ADAM_LR = 0.001
ADAM_B1 = 0.9
ADAM_B2 = 0.999
ADAM_EPS = 1e-08
ADAM_WD = 0.01
ADAM_STEP = 10
V7X_VMEM_LIMIT_BYTES = 56 * 1024 * 1024
LANES = 128
PACK_ALIGN = 8 * LANES


class IciCopy:
    def __init__(self, kind, arr, rows=None):
        self.kind, self.arr, self.out = kind, arr, None
        self.rows = rows


def _carried_copies(items, in_refs, out_refs, send, recv):
    x, y, c, chips = _mesh_pos()
    me = 2 * x + y
    sends, lands = [], []
    for q, (it, src, dst) in enumerate(zip(items, in_refs, out_refs)):
        rows = pl.ds(*(it.rows or (0, it.arr.shape[-2])))
        for k, (cx, cy) in enumerate(chips):
            if it.kind == "gather":
                mine, theirs = dst.at[0, me, c, rows], dst.at[0, 2 * cx + cy, c, rows]
                sends.append(_remote(mine, mine, send, recv, 3 * q + k, (cx, cy, c)))
            else:
                theirs = dst.at[k]
                sends.append(_remote(src.at[k], theirs, send, recv, 3 * q + k, (cx, cy, c)))
            lands.append(_remote(theirs, theirs, send, recv, 3 * q + k, (x, y, c)))
    return sends, lands


def _call(body, *, name, out_shape, grid=(), in_specs=None, out_specs=None, scratch=(), prefetch=0, aliases=None,
          carry=()):
    single = not isinstance(out_specs, (list, tuple))
    in_specs = list(in_specs)
    out_specs = [out_specs] if single else list(out_specs)
    out_shape = [out_shape] if single else list(out_shape)
    scratch = list(scratch)
    aliases = dict(aliases or {})
    n_in, n_out, n_sc, n_c = len(in_specs), len(out_specs), len(scratch), len(carry)
    inner = body
    if n_c:
        assert grid, "a carrier needs a grid"
        for q, it in enumerate(carry):
            if it.kind == "gather":
                aliases[prefetch + n_in + q] = n_out + q
        in_specs += [ANY] * n_c
        out_specs += [ANY] * n_c
        out_shape += [jax.ShapeDtypeStruct(it.arr.shape, it.arr.dtype) for it in carry]
        scratch += [pltpu.SemaphoreType.DMA((3 * n_c,)), pltpu.SemaphoreType.DMA((3 * n_c,))]

        def inner(*refs):
            pre, refs = refs[:prefetch], refs[prefetch:]
            ins, c_in = refs[:n_in], refs[n_in:n_in + n_c]
            outs, c_out = refs[n_in + n_c:n_in + n_c + n_out], refs[n_in + n_c + n_out:n_in + 2 * n_c + n_out]
            rest = refs[n_in + 2 * n_c + n_out:]
            first = last = True
            for d, n in enumerate(grid):
                first = jnp.logical_and(first, pl.program_id(d) == 0)
                last = jnp.logical_and(last, pl.program_id(d) == n - 1)

            @pl.when(first)
            def _():
                for cp in _carried_copies(carry, c_in, c_out, rest[-2], rest[-1])[0]:
                    cp.start()

            body(*pre, *ins, *outs, *rest[:n_sc])

            @pl.when(last)
            def _():
                sends, lands = _carried_copies(carry, c_in, c_out, rest[-2], rest[-1])
                for cp in lands:
                    cp.wait_recv()
                for cp in sends:
                    cp.wait_send()

    spec = pltpu.PrefetchScalarGridSpec(num_scalar_prefetch=prefetch, grid=grid, in_specs=in_specs,
                                        out_specs=out_specs, scratch_shapes=scratch)
    fn = pl.pallas_call(inner, name=name, grid_spec=spec, out_shape=out_shape, input_output_aliases=aliases,
                        compiler_params=pltpu.CompilerParams(vmem_limit_bytes=V7X_VMEM_LIMIT_BYTES))

    def run(*args):
        res = fn(*args, *[it.arr for it in carry])
        for it, o in zip(carry, res[n_out:]):
            it.out = o
        return res[0] if single else list(res[:n_out])

    return run


def _tile(n, pref, unit=LANES):
    if n <= pref:
        return n
    t = (pref // unit) * unit
    while t > unit and n % t:
        t -= unit
    assert n % t == 0, (n, pref, unit)
    return t


def _slabs(n, rows=128):
    return [slice(r, min(r + rows, n)) for r in range(0, n, rows)]


def _dot(a, b):
    return jnp.dot(a, b, preferred_element_type=F32)


def _dot_nt(a, b):
    return lax.dot_general(a, b, (((1,), (1,)), ((), ())), preferred_element_type=F32)


def _dot_tn(a, b):
    return lax.dot_general(a, b, (((0,), (0,)), ((), ())), preferred_element_type=F32)


def _sigmoid(x):
    return jax.nn.sigmoid(x)


_GELU_C = math.sqrt(2.0 / math.pi)


def _gelu(x):
    return 0.5 * x * (1.0 + jnp.tanh(_GELU_C * (x + 0.044715 * x * x * x)))


def _gelu_grad(x):
    th = jnp.tanh(_GELU_C * (x + 0.044715 * x * x * x))
    return 0.5 * (1.0 + th) + 0.5 * x * (1.0 - th * th) * _GELU_C * (1.0 + 3.0 * 0.044715 * x * x)


def _ln_fwd(r, g, b):
    mu = jnp.mean(r, axis=-1, keepdims=True)
    xc = r - mu
    var = jnp.mean(xc * xc, axis=-1, keepdims=True)
    rstd = lax.rsqrt(var + LN_EPS)
    xh = xc * rstd
    return xh * g + b, xh, rstd


def _ln_bwd(dy, xh, rstd, g):
    dxh = dy * g
    m1 = jnp.mean(dxh, axis=-1, keepdims=True)
    m2 = jnp.mean(dxh * xh, axis=-1, keepdims=True)
    dr = rstd * (dxh - m1 - xh * m2)
    return dr, jnp.sum(dy * xh, axis=0, keepdims=True), jnp.sum(dy, axis=0, keepdims=True)


def _exact_tri_dot(tri, x):
    hi = x.astype(BF16)
    r1 = x - hi.astype(F32)
    mid = r1.astype(BF16)
    lo = (r1 - mid.astype(F32)).astype(BF16)
    return _dot(tri, hi) + _dot(tri, mid) + _dot(tri, lo)


def cast_bf16(a3, name):
    L, R, C = a3.shape
    br = _tile(R, max(8, (1 << 20) // C), 8)

    def body(a_ref, o_ref):
        o_ref[...] = a_ref[...].astype(BF16)

    return _call(body, name=name, grid=(L, R // br),
                 in_specs=[pl.BlockSpec((None, br, C), lambda l, r: (l, r, 0))],
                 out_specs=pl.BlockSpec((None, br, C), lambda l, r: (l, r, 0)),
                 out_shape=jax.ShapeDtypeStruct((L, R, C), BF16))(a3)


def adamw(w, g, m, v, name):
    L, R, C = w.shape
    br = _tile(R, max(8, (1 << 19) // C), 8)

    def body(w_ref, g_ref, m_ref, v_ref, go_ref, d_ref, nm_ref, nv_ref):
        _adamw_update(w_ref, g_ref, m_ref, v_ref, go_ref, d_ref, nm_ref, nv_ref)

    spec = pl.BlockSpec((None, br, C), lambda l, r: (l, r, 0))
    sds = jax.ShapeDtypeStruct((L, R, C), F32)
    return _call(body, name=name, grid=(L, R // br), in_specs=[spec] * 4, out_specs=[spec] * 4,
                 out_shape=[sds] * 4)(w, g, m, v)


def _adamw_update(w_ref, g_ref, m_ref, v_ref, go_ref, d_ref, nm_ref, nv_ref):
    gg = g_ref[...]
    nm = ADAM_B1 * m_ref[...] + (1.0 - ADAM_B1) * gg
    nv = ADAM_B2 * v_ref[...] + (1.0 - ADAM_B2) * (gg * gg)
    go_ref[...] = gg
    d_ref[...] = -ADAM_LR * ((nm / (1.0 - ADAM_B1 ** ADAM_STEP))
                             / (jnp.sqrt(nv / (1.0 - ADAM_B2 ** ADAM_STEP)) + ADAM_EPS) + ADAM_WD * w_ref[...])
    nm_ref[...] = nm
    nv_ref[...] = nv


def adamw_small(ws, gs, ms, vs, name):
    n = len(ws)

    def body(*refs):
        for t in range(n):
            _adamw_update(*[refs[k * n + t] for k in range(8)])

    args = [a for group in (ws, gs, ms, vs) for a in group]
    out = _call(body, name=name, in_specs=[VMEM_SPEC] * (4 * n), out_specs=[VMEM_SPEC] * (4 * n),
                out_shape=[jax.ShapeDtypeStruct(a.shape, F32) for a in args])(*args)
    return [[out[k * n + t] for t in range(n)] for k in range(4)]


def mm_nn(a, w, *, w_spec, P, tnw, tm, n_j, name, carry=()):
    T, K = a.shape
    bw = P * tnw

    def body(a_ref, w_ref, o_ref):
        av = a_ref[...]
        for p in range(P):
            o_ref[:, p * tnw:(p + 1) * tnw] = _dot(av, w_ref[p]).astype(BF16)

    return _call(body, name=name, grid=(T // tm, n_j),
                 in_specs=[pl.BlockSpec((tm, K), lambda i, j: (i, 0)), w_spec],
                 out_specs=pl.BlockSpec((tm, bw), lambda i, j: (i, j)),
                 out_shape=jax.ShapeDtypeStruct((T, n_j * bw), BF16), carry=carry)(a, w)


def mm_ln(a, w, res, rg, rb, g, b, *, w_spec, K, tk, tm, name, carry=()):
    T, N = res.shape
    n_k = K // tk

    def body(a_ref, w_ref, res_ref, rg_ref, rb_ref, g_ref, b_ref, xh_ref, y_ref, rs_ref, acc):
        k = pl.program_id(1)

        @pl.when(k == 0)
        def _():
            acc[...] = jnp.zeros_like(acc)

        acc[...] += _dot(a_ref[...], w_ref[...])

        @pl.when(k == n_k - 1)
        def _():
            for rows in _slabs(tm):
                r = ALPHA * (res_ref[rows, :] * rg_ref[...] + rb_ref[...]) + acc[rows, :]
                y, xh, rstd = _ln_fwd(r, g_ref[...], b_ref[...])
                xh_ref[rows, :] = xh
                y_ref[rows, :] = y.astype(BF16)
                rs_ref[rows, :] = rstd

    row = pl.BlockSpec((tm, N), lambda i, k: (i, 0))
    vec = pl.BlockSpec((1, N), lambda i, k: (0, 0))
    return _call(body, name=name, grid=(T // tm, n_k),
                 in_specs=[pl.BlockSpec((tm, tk), lambda i, k: (i, k)), w_spec, row, vec, vec, vec, vec],
                 out_specs=[row, row, pl.BlockSpec((tm, 1), lambda i, k: (i, 0))],
                 out_shape=[jax.ShapeDtypeStruct((T, N), F32), jax.ShapeDtypeStruct((T, N), BF16),
                            jax.ShapeDtypeStruct((T, 1), F32)],
                 scratch=[pltpu.VMEM((tm, N), F32)], carry=carry)(a, w, res, rg, rb, g, b)


def mm_ln_loss(a, w, res, rg, rb, g, b, target, *, w_spec, K, tk, tm, name):
    T, N = res.shape
    n_k = K // tk

    def body(a_ref, w_ref, res_ref, rg_ref, rb_ref, g_ref, b_ref, t_ref, ls_ref, dr_ref, drb_ref, dg_ref, db_ref, acc):
        i = pl.program_id(0)
        k = pl.program_id(1)

        @pl.when(k == 0)
        def _():
            acc[...] = jnp.zeros_like(acc)

        acc[...] += _dot(a_ref[...], w_ref[...])

        @pl.when(k == n_k - 1)
        def _():
            @pl.when(i == 0)
            def _():
                for r in (ls_ref, dg_ref, db_ref):
                    r[...] = jnp.zeros_like(r)

            for rows in _slabs(tm):
                r = ALPHA * (res_ref[rows, :] * rg_ref[...] + rb_ref[...]) + acc[rows, :]
                y, xh, rstd = _ln_fwd(r, g_ref[...], b_ref[...])
                e = y - t_ref[rows, :]
                ls_ref[...] += 0.5 * jnp.sum(jnp.mean(e * e, axis=-1, keepdims=True), axis=0, keepdims=True)
                dr, dg, db = _ln_bwd(e / N, xh, rstd, g_ref[...])
                dr_ref[rows, :] = dr
                drb_ref[rows, :] = dr.astype(BF16)
                dg_ref[...] += dg
                db_ref[...] += db

    row = pl.BlockSpec((tm, N), lambda i, k: (i, 0))
    vec = pl.BlockSpec((1, N), lambda i, k: (0, 0))
    return _call(body, name=name, grid=(T // tm, n_k),
                 in_specs=[pl.BlockSpec((tm, tk), lambda i, k: (i, k)), w_spec, row, vec, vec, vec, vec, row],
                 out_specs=[pl.BlockSpec((1, LANES), lambda i, k: (0, 0)), row, row, vec, vec],
                 out_shape=[jax.ShapeDtypeStruct((1, LANES), F32), jax.ShapeDtypeStruct((T, N), F32),
                            jax.ShapeDtypeStruct((T, N), BF16), jax.ShapeDtypeStruct((1, N), F32),
                            jax.ShapeDtypeStruct((1, N), F32)],
                 scratch=[pltpu.VMEM((tm, N), F32)])(a, w, res, rg, rb, g, b, target)


def mm_nt_plain(a, w, *, tm, tn, name):
    T, K = a.shape
    N = w.shape[0]

    def body(a_ref, w_ref, o_ref):
        o_ref[...] = _dot_nt(a_ref[...], w_ref[...]).astype(BF16)

    return _call(body, name=name, grid=(T // tm, N // tn),
                 in_specs=[pl.BlockSpec((tm, K), lambda i, j: (i, 0)), pl.BlockSpec((tn, K), lambda i, j: (j, 0))],
                 out_specs=pl.BlockSpec((tm, tn), lambda i, j: (i, j)),
                 out_shape=jax.ShapeDtypeStruct((T, N), BF16))(a, w)


def mm_nt_res(a, w, res, ln, *, a_spec, w_spec, P, tnw, n_k, tm, name, carry=()):
    T, N = res.shape
    n_i = T // tm

    def body(*refs):
        if ln is None:
            a_ref, w_ref, res_ref, o_ref, acc = refs
        else:
            a_ref, w_ref, res_ref, xh_ref, rs_ref, g_ref, dr_ref, drb_ref, dg_ref, db_ref, acc = refs
        i = pl.program_id(0)
        k = pl.program_id(1)

        @pl.when(k == 0)
        def _():
            acc[...] = jnp.zeros_like(acc)

        wv = w_ref[0] if P == 1 else jnp.concatenate([w_ref[p] for p in range(P)], axis=1)
        acc[...] += _dot_nt(a_ref[...], wv)

        @pl.when(k == n_k - 1)
        def _():
            if ln is not None:
                @pl.when(i == 0)
                def _():
                    dg_ref[...] = jnp.zeros_like(dg_ref)
                    db_ref[...] = jnp.zeros_like(db_ref)

            for rows in _slabs(tm):
                d = ALPHA * res_ref[rows, :] + acc[rows, :]
                if ln is None:
                    o_ref[rows, :] = d
                else:
                    dr, dg, db = _ln_bwd(d, xh_ref[rows, :], rs_ref[rows, :], g_ref[...])
                    dr_ref[rows, :] = dr
                    drb_ref[rows, :] = dr.astype(BF16)
                    dg_ref[...] += dg
                    db_ref[...] += db

    row = pl.BlockSpec((tm, N), lambda i, k: (i, 0))
    vec = pl.BlockSpec((1, N), lambda i, k: (0, 0))
    scratch = [pltpu.VMEM((tm, N), F32)]
    if ln is None:
        return _call(body, name=name, grid=(n_i, n_k), in_specs=[a_spec, w_spec, row], out_specs=row,
                     out_shape=jax.ShapeDtypeStruct((T, N), F32), scratch=scratch, carry=carry)(a, w, res)
    xh, rstd, g = ln
    return _call(body, name=name, grid=(n_i, n_k),
                 in_specs=[a_spec, w_spec, row, row, pl.BlockSpec((tm, 1), lambda i, k: (i, 0)), vec],
                 out_specs=[row, row, vec, vec],
                 out_shape=[jax.ShapeDtypeStruct((T, N), F32), jax.ShapeDtypeStruct((T, N), BF16),
                            jax.ShapeDtypeStruct((1, N), F32), jax.ShapeDtypeStruct((1, N), F32)],
                 scratch=scratch, carry=carry)(a, w, res, xh, rstd, g)


def mm_tn(a, g, *, a_spec, g_spec, o_spec, out_shape, grid, acc_shape, P, tnw, name):
    n_t = grid[2]

    def body(a_ref, g_ref, o_ref, acc):
        t = pl.program_id(2)

        @pl.when(t == 0)
        def _():
            acc[...] = jnp.zeros_like(acc)

        acc[...] += _dot_tn(a_ref[...], g_ref[...])

        @pl.when(t == n_t - 1)
        def _():
            for p in range(P):
                o_ref[p] = acc[:, p * tnw:(p + 1) * tnw].astype(BF16)

    return _call(body, name=name, grid=grid, in_specs=[a_spec, g_spec], out_specs=o_spec,
                 out_shape=jax.ShapeDtypeStruct(out_shape, BF16),
                 scratch=[pltpu.VMEM(acc_shape, F32)])(a, g)


def _causal_conv(ext, halo, w, b):
    s1 = pltpu.roll(ext, 1, 0)[halo:]
    s2 = pltpu.roll(ext, 2, 0)[halo:]
    return b + w[2:3] * ext[halo:] + w[1:2] * s1 + w[0:1] * s2, s1, s2


def ffn_up(xb, wup, cw, cb, l, *, name, carry=()):
    T, D = xb.shape
    Ns = wup.shape[-1]
    F = 2 * Ns
    tn = _tile(Ns, 256)
    nps = Ns // tn
    n_j = F // tn
    tm = _tile(T, 1024, 8)

    def body(x_ref, wa_ref, wv_ref, cwa_ref, cwv_ref, cba_ref, cbv_ref, h_ref, hc_ref, f_ref, carry):
        i = pl.program_id(1)

        @pl.when(i == 0)
        def _():
            carry[...] = jnp.zeros_like(carry)

        xv = x_ref[...]
        ha = _dot(xv, wa_ref[...])
        hv = _dot(xv, wv_ref[...])
        ca, _, _ = _causal_conv(jnp.concatenate([carry[0], ha], axis=0), CONV_HALO, cwa_ref[...], cba_ref[...])
        cv, _, _ = _causal_conv(jnp.concatenate([carry[1], hv], axis=0), CONV_HALO, cwv_ref[...], cbv_ref[...])
        carry[0] = ha[tm - CONV_HALO:]
        carry[1] = hv[tm - CONV_HALO:]
        h_ref[0] = ha.astype(BF16)
        h_ref[1] = hv.astype(BF16)
        hc_ref[0] = ca.astype(BF16)
        hc_ref[1] = cv.astype(BF16)
        f_ref[...] = (ca * _sigmoid(ca) * cv).astype(BF16)

    wspec_a = pl.BlockSpec((None, None, D, tn), lambda j, i: (0, j // nps, 0, j % nps))
    wspec_v = pl.BlockSpec((None, None, D, tn), lambda j, i: (0, 2 + j // nps, 0, j % nps))
    pair_tile = pl.BlockSpec((2, tm, tn), lambda j, i: (0, i, j))
    return _call(
        body, name=name, grid=(n_j, T // tm),
        in_specs=[pl.BlockSpec((tm, D), lambda j, i: (i, 0)), wspec_a, wspec_v,
                  pl.BlockSpec((None, 3, tn), lambda j, i: (l, 0, j)),
                  pl.BlockSpec((None, 3, tn), lambda j, i: (l, 0, n_j + j)),
                  pl.BlockSpec((None, 1, tn), lambda j, i: (l, 0, j)),
                  pl.BlockSpec((None, 1, tn), lambda j, i: (l, 0, n_j + j))],
        out_specs=[pair_tile, pair_tile, pl.BlockSpec((tm, tn), lambda j, i: (i, j))],
        out_shape=[jax.ShapeDtypeStruct((2, T, F), BF16), jax.ShapeDtypeStruct((2, T, F), BF16),
                   jax.ShapeDtypeStruct((T, F), BF16)],
        scratch=[pltpu.VMEM((2, CONV_HALO, tn), F32)], carry=carry)(xb, wup, wup, cw, cw, cb, cb)


def ffn_dgate(db16, wdn, h, hc, cw, l, *, name, carry=()):
    T, D = db16.shape
    F = h.shape[-1]
    tn = _tile(F, 512)
    n_j = F // tn
    tm = _tile(T, 512, 16)
    n_i = T // tm
    n_ext = tm + CONV_HALO

    def body(d_ref, w_ref, h_ref, hc_ref, cwa_ref, cwv_ref, dh_ref, dcw_ref, dcb_ref, carry):
        ip = pl.program_id(1)

        @pl.when(ip == 0)
        def _():
            carry[...] = jnp.zeros_like(carry)
            dcw_ref[...] = jnp.zeros_like(dcw_ref)
            dcb_ref[...] = jnp.zeros_like(dcb_ref)

        df = _dot_nt(d_ref[...], w_ref[...])
        ca = hc_ref[0].astype(F32)
        cv = hc_ref[1].astype(F32)
        sig = _sigmoid(ca)
        sil = ca * sig
        da = df * cv * (sig + sil * (1.0 - sig))
        dv = df * sil
        for half, (dc, w_ref_h) in enumerate(((da, cwa_ref), (dv, cwv_ref))):
            w = w_ref_h[...]
            h0 = h_ref[half].astype(F32)
            ext = jnp.concatenate([dc, carry[half]], axis=0)
            n1 = pltpu.roll(ext, n_ext - 1, 0)[:tm]
            n2 = pltpu.roll(ext, n_ext - 2, 0)[:tm]
            dcb_ref[half] += jnp.sum(dc, axis=0, keepdims=True)
            dcw_ref[half] += jnp.concatenate(
                [jnp.sum(n2 * h0, axis=0, keepdims=True), jnp.sum(n1 * h0, axis=0, keepdims=True),
                 jnp.sum(dc * h0, axis=0, keepdims=True)], axis=0)
            dh_ref[half] = (w[2:3] * dc + w[1:2] * n1 + w[0:1] * n2).astype(BF16)
            carry[half] = dc[:CONV_HALO]

    rev = lambda ip: n_i - 1 - ip
    tile = pl.BlockSpec((2, tm, tn), lambda j, ip: (0, rev(ip), j))
    return _call(
        body, name=name, grid=(n_j, n_i),
        in_specs=[pl.BlockSpec((tm, D), lambda j, ip: (rev(ip), 0)),
                  pl.BlockSpec((None, tn, D), lambda j, ip: (0, j, 0)), tile, tile,
                  pl.BlockSpec((None, 3, tn), lambda j, ip: (l, 0, j)),
                  pl.BlockSpec((None, 3, tn), lambda j, ip: (l, 0, n_j + j))],
        out_specs=[tile, pl.BlockSpec((2, 3, tn), lambda j, ip: (0, 0, j)),
                   pl.BlockSpec((2, 1, tn), lambda j, ip: (0, 0, j))],
        out_shape=[jax.ShapeDtypeStruct((2, T, F), BF16), jax.ShapeDtypeStruct((2, 3, F), F32),
                   jax.ShapeDtypeStruct((2, 1, F), F32)],
        scratch=[pltpu.VMEM((2, CONV_HALO, tn), F32)], carry=carry)(db16, wdn, h, hc, cw, cw)


def _pool_fwd(ext, xb_g, t_glob, win):
    e = ext
    sft = 1
    while sft < win:
        e = e + pltpu.roll(e, sft, 0)
        sft *= 2
    cnt = jnp.minimum(t_glob + 1.0, float(win))
    return e[POOL_HALO:] / cnt - xb_g


def gating_fwd(h0, lg, lb, ws, bsT, wp, sc, *, name, carry=()):
    T = h0.shape[0]
    DA = lg.shape[-1]
    DB = sc.shape[-1]
    HA = DA // A_HEAD
    G = len(B_WINDOWS)
    CG = DB // G
    tm = _tile(T, 512, A_CHUNK)
    n_c = tm // A_CHUNK

    def body(h_ref, halo_ref, lg_ref, lb_ref, ws_ref, bsT_ref, wp_ref, sc_ref, cat_ref):
        i = pl.program_id(0)
        hu = h_ref[:, 0:DA].astype(F32)
        hv = h_ref[:, DA:2 * DA].astype(F32)
        xb = h_ref[:, 2 * DA:].astype(F32)
        u = _gelu(hu)
        vn, _, _ = _ln_fwd(_gelu(hv), lg_ref[...], lb_ref[...])
        vnb = vn.astype(BF16)
        rr = lax.broadcasted_iota(jnp.int32, (A_CHUNK, A_CHUNK), 0)
        cc = lax.broadcasted_iota(jnp.int32, (A_CHUNK, A_CHUNK), 1)
        for hh in range(HA):
            wt = jnp.where(rr >= cc, ws_ref[hh], 0.0).astype(BF16)
            cs = slice(hh * A_HEAD, (hh + 1) * A_HEAD)
            for n in range(n_c):
                rs = slice(n * A_CHUNK, (n + 1) * A_CHUNK)
                s = _dot(wt, vnb[rs, cs]) + bsT_ref[:, hh:hh + 1]
                cat_ref[rs, cs] = (u[rs, cs] * s).astype(BF16)
        halo = jnp.where(i > 0, halo_ref[...].astype(F32), 0.0)
        ext = jnp.concatenate([halo, xb], axis=0)
        t_glob = (i * tm + lax.broadcasted_iota(jnp.int32, (tm, 1), 0)).astype(F32)
        for g, win in enumerate(B_WINDOWS):
            gs = slice(g * CG, (g + 1) * CG)
            p = _pool_fwd(ext[:, gs], xb[:, gs], t_glob, win)
            z = _dot(p.astype(BF16), wp_ref[g])
            cat_ref[:, DA + g * CG:DA + (g + 1) * CG] = (z * sc_ref[:, gs]).astype(BF16)

    full = lambda a: pl.BlockSpec(a.shape, lambda i: (0,) * a.ndim)
    hpb = tm // POOL_HALO
    return _call(
        body, name=name, grid=(T // tm,),
        in_specs=[pl.BlockSpec((tm, 2 * DA + DB), lambda i: (i, 0)),
                  pl.BlockSpec((POOL_HALO, DB), lambda i: (jnp.maximum(i * hpb - 1, 0), 2 * DA // DB)),
                  full(lg), full(lb), full(ws), full(bsT), full(wp), full(sc)],
        out_specs=pl.BlockSpec((tm, DA + DB), lambda i: (i, 0)),
        out_shape=jax.ShapeDtypeStruct((T, DA + DB), BF16), carry=carry)(h0, h0, lg, lb, ws, bsT, wp, sc)


def gating_bwd(h0, dcat, lg, lb, ws, wsT, bsT, wp, sc, *, name, carry=()):
    T = h0.shape[0]
    DA = lg.shape[-1]
    DB = sc.shape[-1]
    HA = DA // A_HEAD
    G = len(B_WINDOWS)
    CG = DB // G
    tm = _tile(T, 512, A_CHUNK)
    n_i = T // tm
    n_c = tm // A_CHUNK
    n_ext = tm + POOL_HALO

    def body(h_ref, halo_ref, dc_ref, dhalo_ref, lg_ref, lb_ref, ws_ref, wsT_ref, bsT_ref, wp_ref, sc_ref,
             dh_ref, dws_ref, dbsT_ref, dlg_ref, dlb_ref, dsc_ref, dwp_ref, dvn_sc):
        i = pl.program_id(0)

        @pl.when(i == 0)
        def _():
            for r in (dws_ref, dbsT_ref, dlg_ref, dlb_ref, dsc_ref, dwp_ref):
                r[...] = jnp.zeros_like(r)

        hu = h_ref[:, 0:DA].astype(F32)
        hv = h_ref[:, DA:2 * DA].astype(F32)
        xb = h_ref[:, 2 * DA:].astype(F32)
        u = _gelu(hu)
        gu = _gelu_grad(hu)
        lgv = lg_ref[...]
        vn, vhat, rstd = _ln_fwd(_gelu(hv), lgv, lb_ref[...])
        vnb = vn.astype(BF16)
        rr = lax.broadcasted_iota(jnp.int32, (A_CHUNK, A_CHUNK), 0)
        cc = lax.broadcasted_iota(jnp.int32, (A_CHUNK, A_CHUNK), 1)
        for hh in range(HA):
            wt = jnp.where(rr >= cc, ws_ref[hh], 0.0).astype(BF16)
            wtT = jnp.where(rr <= cc, wsT_ref[hh], 0.0).astype(BF16)
            cs = slice(hh * A_HEAD, (hh + 1) * A_HEAD)
            dws = jnp.zeros((A_CHUNK, A_CHUNK), F32)
            dbs = jnp.zeros((A_CHUNK, 1), F32)
            for n in range(n_c):
                rs = slice(n * A_CHUNK, (n + 1) * A_CHUNK)
                vb = vnb[rs, cs]
                s = _dot(wt, vb) + bsT_ref[:, hh:hh + 1]
                dya = dc_ref[rs, cs].astype(F32)
                ds = dya * u[rs, cs]
                dh_ref[rs, cs] = (dya * s * gu[rs, cs]).astype(BF16)
                dsb = ds.astype(BF16)
                dbs = dbs + jnp.sum(ds, axis=1, keepdims=True)
                dws = dws + _dot_nt(dsb, vb)
                dvn_sc[rs, cs] = _dot(wtT, dsb)
            dws_ref[hh] += jnp.where(rr >= cc, dws, 0.0)
            dbsT_ref[:, hh:hh + 1] += dbs
        dvg, dlg, dlb = _ln_bwd(dvn_sc[...], vhat, rstd, lgv)
        dlg_ref[...] += dlg
        dlb_ref[...] += dlb
        dh_ref[:, DA:2 * DA] = (dvg * _gelu_grad(hv)).astype(BF16)

        halo = jnp.where(i > 0, halo_ref[...].astype(F32), 0.0)
        ext = jnp.concatenate([halo, xb], axis=0)
        t_glob = (i * tm + lax.broadcasted_iota(jnp.int32, (tm, 1), 0)).astype(F32)
        t_ext = (i * tm + lax.broadcasted_iota(jnp.int32, (n_ext, 1), 0)).astype(F32)
        dyb = dc_ref[:, DA:].astype(F32)
        dhalo = jnp.where(i < n_i - 1, dhalo_ref[...].astype(F32), 0.0)
        dyb_ext = jnp.concatenate([dyb, dhalo], axis=0)
        for g, win in enumerate(B_WINDOWS):
            gs = slice(g * CG, (g + 1) * CG)
            pb = _pool_fwd(ext[:, gs], xb[:, gs], t_glob, win).astype(BF16)
            wpg = wp_ref[g]
            z = _dot(pb, wpg)
            dsc_ref[:, gs] += jnp.sum(dyb[:, gs] * z, axis=0, keepdims=True)
            dzb = (dyb_ext[:, gs] * sc_ref[:, gs]).astype(BF16)
            dwp_ref[g] += _dot_tn(pb, dzb[:tm])
            dp = _dot_nt(dzb, wpg)
            e = dp / jnp.minimum(t_ext + 1.0, float(win))
            sft = 1
            while sft < win:
                e = e + pltpu.roll(e, n_ext - sft, 0)
                sft *= 2
            dh_ref[:, 2 * DA + g * CG:2 * DA + (g + 1) * CG] = (e[:tm] - dp[:tm]).astype(BF16)

    full = lambda a: pl.BlockSpec(a.shape, lambda i: (0,) * a.ndim)
    hpb = tm // POOL_HALO
    n_hb = T // POOL_HALO
    outs = [jax.ShapeDtypeStruct((T, 2 * DA + DB), BF16), jax.ShapeDtypeStruct(ws.shape, F32),
            jax.ShapeDtypeStruct(bsT.shape, F32), jax.ShapeDtypeStruct(lg.shape, F32),
            jax.ShapeDtypeStruct(lb.shape, F32), jax.ShapeDtypeStruct(sc.shape, F32),
            jax.ShapeDtypeStruct(wp.shape, F32)]
    return _call(
        body, name=name, grid=(n_i,),
        in_specs=[pl.BlockSpec((tm, 2 * DA + DB), lambda i: (i, 0)),
                  pl.BlockSpec((POOL_HALO, DB), lambda i: (jnp.maximum(i * hpb - 1, 0), 2 * DA // DB)),
                  pl.BlockSpec((tm, DA + DB), lambda i: (i, 0)),
                  pl.BlockSpec((POOL_HALO, DB), lambda i: (jnp.minimum((i + 1) * hpb, n_hb - 1), DA // DB)),
                  full(lg), full(lb), full(ws), full(wsT), full(bsT), full(wp), full(sc)],
        out_specs=[pl.BlockSpec((tm, 2 * DA + DB), lambda i: (i, 0))] + [full(o) for o in outs[1:]],
        out_shape=outs,
        scratch=[pltpu.VMEM((tm, DA), F32)], carry=carry)(h0, h0, dcat, dcat, lg, lb, ws, wsT, bsT, wp, sc)


HP = 2 * C_HEAD


def _hgrn_gates(h_ref, rows, lbv, tri):
    pre = []
    for r in rows:
        blk = h_ref[r, :].astype(F32)
        q = blk[:, 0:HP]
        sg = _sigmoid(blk[:, HP:2 * HP])
        f = lbv + (1.0 - lbv) * sg
        pre.append(dict(q=q, sq=_sigmoid(q), sg=sg, f=f, k=1.0 - f, lf=jnp.log(f), v=blk[:, 2 * HP:3 * HP],
                        gg=blk[:, 3 * HP:4 * HP]))
    bcums = [_exact_tri_dot(tri, p["lf"]) for p in pre]
    out = []
    for p, bcum in zip(pre, bcums):
        blast = bcum[C_CHUNK - 1:C_CHUNK]
        e_in = jnp.exp(bcum)
        e_out = jnp.exp(-bcum)
        e_end = jnp.exp(blast - bcum)
        out.append(dict(p, e_in=e_in, e_out=e_out, e_end=e_end, qd=p["q"] * p["sq"] * e_in, kd=p["k"] * e_out,
                        ke=p["k"] * e_end, dec=jnp.exp(blast)))
    return out


def _lower_bound(lbp_ref):
    p0, p1 = lbp_ref[0:1], lbp_ref[1:2]
    mx = jnp.maximum(p0, p1)
    e0, e1 = jnp.exp(p0 - mx), jnp.exp(p1 - mx)
    return e1 / (e0 + e1)


def hgrn_fwd(h1p, lbp, gn, *, name, carry=()):
    T = h1p.shape[0]
    DC = gn.shape[-1]
    n_p = DC // HP
    tt = _tile(T, 1024, C_CHUNK)
    n_c = tt // C_CHUNK

    def body(h_ref, lbp_ref, gn_ref, y_ref, o_ref, sp_ref, st):
        i = pl.program_id(1)

        @pl.when(i == 0)
        def _():
            st[...] = jnp.zeros_like(st)

        lbv = _lower_bound(lbp_ref)
        gnv = gn_ref[...]
        rr = lax.broadcasted_iota(jnp.int32, (C_CHUNK, C_CHUNK), 0)
        cc = lax.broadcasted_iota(jnp.int32, (C_CHUNK, C_CHUNK), 1)
        causal = rr >= cc
        tri = jnp.where(causal, 1.0, 0.0).astype(BF16)

        heads = [slice(hd * C_HEAD, (hd + 1) * C_HEAD) for hd in range(2)]
        rows = [slice(n * C_CHUNK, (n + 1) * C_CHUNK) for n in range(n_c)]
        nh = [(n, hd) for n in range(n_c) for hd in range(2)]
        gates = _hgrn_gates(h_ref, rows, lbv, tri)
        b16 = lambda key: {(n, hd): gates[n][key][:, heads[hd]].astype(BF16) for n, hd in nh}
        qd, kd, ke, vb = b16("qd"), b16("kd"), b16("ke"), b16("v")
        att = {i: jnp.where(causal, _dot_nt(qd[i], kd[i]), 0.0).astype(BF16) for i in nh}
        intra = {i: _dot(att[i], vb[i]) for i in nh}
        upd = {i: _dot_tn(vb[i], ke[i]) for i in nh}
        s = [st[0], st[1]]
        entering = {}
        for n, hd in nh:
            entering[n, hd] = s[hd]
            sp_ref[hd, n] = s[hd]
            s[hd] = gates[n]["dec"][:, heads[hd]] * s[hd] + upd[n, hd]
        st[0], st[1] = s
        o = {i: intra[i] + _dot_nt(qd[i], entering[i].astype(BF16)) for i in nh}
        for n in range(n_c):
            os_ = [o[n, 0], o[n, 1]]
            o_ref[rows[n], :] = jnp.concatenate(os_, axis=1).astype(BF16)
            ys = [oh * lax.rsqrt(jnp.mean(oh * oh, axis=-1, keepdims=True) + LN_EPS) for oh in os_]
            y_ref[rows[n], :] = (jnp.concatenate(ys, axis=1) * gnv * _sigmoid(gates[n]["gg"])).astype(BF16)

    return _call(
        body, name=name, grid=(n_p, T // tt),
        in_specs=[pl.BlockSpec((tt, 4 * HP), lambda p, i: (i, p)), pl.BlockSpec((2, HP), lambda p, i: (0, p)),
                  pl.BlockSpec((1, HP), lambda p, i: (0, p))],
        out_specs=[pl.BlockSpec((tt, HP), lambda p, i: (i, p)), pl.BlockSpec((tt, HP), lambda p, i: (i, p)),
                   pl.BlockSpec((2, n_c, C_HEAD, C_HEAD), lambda p, i: (p, i, 0, 0))],
        out_shape=[jax.ShapeDtypeStruct((T, DC), BF16), jax.ShapeDtypeStruct((T, DC), BF16),
                   jax.ShapeDtypeStruct((2 * n_p, T // C_CHUNK, C_HEAD, C_HEAD), F32)],
        scratch=[pltpu.VMEM((2, C_HEAD, C_HEAD), F32)], carry=carry)(h1p, lbp, gn)


def hgrn_bwd(h1p, o_saved, dy, sp, lbp, gn, *, name, carry=()):
    T = h1p.shape[0]
    DC = gn.shape[-1]
    n_p = DC // HP
    tt = _tile(T, 512, C_CHUNK)
    n_i = T // tt
    n_c = tt // C_CHUNK

    def body(h_ref, o_ref, dy_ref, sp_ref, lbp_ref, gn_ref, dh_ref, dgn_ref, dlbp_ref, dst, dlb_acc):
        ip = pl.program_id(1)

        @pl.when(ip == 0)
        def _():
            dst[...] = jnp.zeros_like(dst)
            dlb_acc[...] = jnp.zeros_like(dlb_acc)
            dgn_ref[...] = jnp.zeros_like(dgn_ref)

        lbv = _lower_bound(lbp_ref)
        gnv = gn_ref[...]
        rr = lax.broadcasted_iota(jnp.int32, (C_CHUNK, C_CHUNK), 0)
        cc = lax.broadcasted_iota(jnp.int32, (C_CHUNK, C_CHUNK), 1)
        causal = rr >= cc
        tri = jnp.where(causal, 1.0, 0.0).astype(BF16)
        tri_t = jnp.where(rr <= cc, 1.0, 0.0).astype(BF16)
        last_row = lax.broadcasted_iota(jnp.int32, (C_CHUNK, 1), 0) == C_CHUNK - 1

        heads = [slice(hd * C_HEAD, (hd + 1) * C_HEAD) for hd in range(2)]
        rows = [slice(n * C_CHUNK, (n + 1) * C_CHUNK) for n in range(n_c)]
        cat = lambda parts: jnp.concatenate(parts, axis=1)
        nh = [(n, hd) for n in range(n_c) for hd in range(2)]
        pair = lambda d, n: cat([d[n, 0], d[n, 1]])
        gates = _hgrn_gates(h_ref, rows, lbv, tri)
        dgn = jnp.zeros((1, HP), F32)
        dgg, dob = [], {}
        for n in range(n_c):
            o = o_ref[rows[n], :].astype(F32)
            dyv = dy_ref[rows[n], :].astype(F32)
            sgg = _sigmoid(gates[n]["gg"])
            rrs = [lax.rsqrt(jnp.mean(o[:, cs] * o[:, cs], axis=-1, keepdims=True) + LN_EPS) for cs in heads]
            ohat = cat([o[:, cs] * r for cs, r in zip(heads, rrs)])
            dyn = dyv * sgg
            dgg.append(dyv * ohat * gnv * sgg * (1.0 - sgg))
            dgn = dgn + jnp.sum(dyn * ohat, axis=0, keepdims=True)
            dxh = dyn * gnv
            for hd, cs in enumerate(heads):
                dxh_h, oh_h = dxh[:, cs], ohat[:, cs]
                dob[n, hd] = (rrs[hd] * (dxh_h - oh_h * jnp.mean(dxh_h * oh_h, axis=-1, keepdims=True))).astype(BF16)
        dgn_ref[...] += dgn
        b16 = lambda key: {(n, hd): gates[n][key][:, heads[hd]].astype(BF16) for n, hd in nh}
        qd, kd, ke, vb = b16("qd"), b16("kd"), b16("ke"), b16("v")
        s_in = {(n, hd): sp_ref[hd, n] for n, hd in nh}
        att = {i: jnp.where(causal, _dot_nt(qd[i], kd[i]), 0.0).astype(BF16) for i in nh}
        datt = {i: jnp.where(causal, _dot_nt(dob[i], vb[i]), 0.0).astype(BF16) for i in nh}
        grow = {i: _dot_tn(dob[i], qd[i]) for i in nh}
        dv_i = {i: _dot_tn(att[i], dob[i]) for i in nh}
        dqd = {i: _dot(datt[i], kd[i]) + _dot(dob[i], s_in[i].astype(BF16)) for i in nh}
        dkd = {i: _dot_tn(datt[i], qd[i]) for i in nh}
        ds = [dst[0], dst[1]]
        leaving = {}
        for n in reversed(range(n_c)):
            for hd, cs in enumerate(heads):
                leaving[n, hd] = ds[hd]
                ds[hd] = gates[n]["dec"][:, cs] * ds[hd] + grow[n, hd]
        dst[0], dst[1] = ds
        dsb = {i: leaving[i].astype(BF16) for i in nh}
        dv = {i: dv_i[i] + _dot_nt(ke[i], dsb[i]) for i in nh}
        dke = {i: _dot(vb[i], dsb[i]) for i in nh}
        ddec = {i: jnp.sum(leaving[i] * s_in[i], axis=0, keepdims=True) for i in nh}
        mid = []
        for n in range(n_c):
            a = gates[n]
            dqd_n, dkd_n, dke_n = pair(dqd, n), pair(dkd, n), pair(dke, n)
            kek = dke_n * a["ke"]
            dblast = jnp.sum(kek, axis=0, keepdims=True) + pair(ddec, n) * a["dec"]
            dbcum = dqd_n * a["qd"] - dkd_n * a["kd"] - kek + jnp.where(last_row, dblast, 0.0)
            mid.append((dqd_n * a["e_in"], dkd_n * a["e_out"] + dke_n * a["e_end"], dbcum))
        dlf = [_exact_tri_dot(tri_t, m[2]) for m in mid]
        dlb = jnp.zeros((1, HP), F32)
        for n in range(n_c):
            a = gates[n]
            dqs, dk, _ = mid[n]
            df = dlf[n] / a["f"] - dk
            dlb = dlb + jnp.sum(df * (1.0 - a["sg"]), axis=0, keepdims=True)
            dfl = df * (1.0 - lbv) * a["sg"] * (1.0 - a["sg"])
            dq = dqs * a["sq"] * (1.0 + a["q"] * (1.0 - a["sq"]))
            dh_ref[rows[n], :] = cat([dq, dfl, pair(dv, n), dgg[n]]).astype(BF16)
        dlb_acc[...] += dlb

        @pl.when(ip == n_i - 1)
        def _():
            d1 = dlb_acc[...] * lbv * (1.0 - lbv)
            dlbp_ref[...] = jnp.concatenate([-d1, d1], axis=0)

    rev = lambda ip: n_i - 1 - ip
    return _call(
        body, name=name, grid=(n_p, n_i),
        in_specs=[pl.BlockSpec((tt, 4 * HP), lambda p, ip: (rev(ip), p)),
                  pl.BlockSpec((tt, HP), lambda p, ip: (rev(ip), p)),
                  pl.BlockSpec((tt, HP), lambda p, ip: (rev(ip), p)),
                  pl.BlockSpec((2, n_c, C_HEAD, C_HEAD), lambda p, ip: (p, rev(ip), 0, 0)),
                  pl.BlockSpec((2, HP), lambda p, ip: (0, p)), pl.BlockSpec((1, HP), lambda p, ip: (0, p))],
        out_specs=[pl.BlockSpec((tt, 4 * HP), lambda p, ip: (rev(ip), p)),
                   pl.BlockSpec((1, HP), lambda p, ip: (0, p)), pl.BlockSpec((2, HP), lambda p, ip: (0, p))],
        out_shape=[jax.ShapeDtypeStruct(h1p.shape, BF16), jax.ShapeDtypeStruct((1, DC), F32),
                   jax.ShapeDtypeStruct((2, DC), F32)],
        scratch=[pltpu.VMEM((2, C_HEAD, C_HEAD), F32), pltpu.VMEM((1, HP), F32)], carry=carry)(h1p, o_saved, dy, sp, lbp, gn)


def _mesh_pos():
    x, y, c = lax.axis_index("x"), lax.axis_index("y"), lax.axis_index("c")
    chips = [(1 - x, y), (x, 1 - y), (1 - x, 1 - y)]
    return x, y, c, chips


def _remote(src, dst, send, recv, j, dev):
    return pltpu.make_async_remote_copy(src_ref=src, dst_ref=dst, send_sem=send.at[j], recv_sem=recv.at[j],
                                        device_id=dev, device_id_type=MESH)


def mesh_ids():
    x, y, c, chips = _mesh_pos()
    return jnp.stack([c] + [2 * cx + cy for cx, cy in chips] + [2 * x + y]).astype(jnp.int32)


def _sibling():
    return (lax.axis_index("x"), lax.axis_index("y"), 1 - lax.axis_index("c"))


SWAP_PARTS = 2


def _swap_with_sibling(src, recv, send_sem, recv_sem, step):
    slot = step % 2
    br = src.shape[0]
    n = SWAP_PARTS if br % (16 * SWAP_PARTS) == 0 else 1
    parts = []
    for k in range(n):
        rows = pl.ds(k * (br // n), br // n)
        cp = pltpu.make_async_remote_copy(
            src_ref=src.at[rows], dst_ref=recv.at[slot, rows], send_sem=send_sem.at[slot * SWAP_PARTS + k],
            recv_sem=recv_sem.at[slot * SWAP_PARTS + k], device_id=_sibling(), device_id_type=MESH)
        cp.start()
        parts.append((cp, slot, rows))
    return parts


def _swap_scratch(br, C, dtype):
    return [pltpu.VMEM((2, br, C), dtype), pltpu.SemaphoreType.DMA((2 * SWAP_PARTS,)),
            pltpu.SemaphoreType.DMA((2 * SWAP_PARTS,))]


def _swap_rows(Rh, C, itemsize):
    return _tile(Rh, max(16, (3 << 20) // (C * itemsize)), 16)


def cast_to_slot(a3, l, me1, name):
    _, R, C = a3.shape
    br = _tile(R, max(8, (1 << 20) // C), 16)

    def body(me_ref, a_ref, o_ref):
        o_ref[...] = a_ref[...].astype(BF16)

    return _call(body, name=name, grid=(R // br,), prefetch=1,
                 in_specs=[pl.BlockSpec((None, br, C), lambda r, me: (l, r, 0))],
                 out_specs=pl.BlockSpec((None, None, br, C), lambda r, me: (0, me[0], r, 0)),
                 out_shape=jax.ShapeDtypeStruct((1, 4, R, C), BF16))(me1, a3)


def all_gather_chips(big, small, *, name):
    nb, ns = len(big), len(small)
    layers = [(t, l) for t in range(nb) for l in range(big[t].shape[0])]
    n_big = 3 * len(layers)
    n_rem = n_big + 3 * ns

    def body(*refs):
        small_in = refs[nb:nb + ns]
        bufs, small_out = refs[nb + ns:2 * nb + ns], refs[2 * nb + ns:2 * (nb + ns)]
        send, recv, loc = refs[2 * (nb + ns):]
        x, y, c, chips = _mesh_pos()
        me = 2 * x + y
        ids = [2 * cx + cy for cx, cy in chips]
        started, sends = [], []
        for t in range(ns):
            cp = pltpu.make_async_copy(small_in[t], small_out[t].at[me], loc.at[t])
            cp.start()
            started.append(cp)
        for q, (t, l) in enumerate(layers):
            for k, chip in enumerate(chips):
                blk = bufs[t].at[l, me, c]
                cp = _remote(blk, blk, send, recv, 3 * q + k, (*chip, c))
                cp.start()
                sends.append(cp)
        for t in range(ns):
            for k, chip in enumerate(chips):
                cp = _remote(small_in[t], small_out[t].at[me], send, recv, n_big + 3 * t + k, (*chip, c))
                cp.start()
                sends.append(cp)
        for q, (t, l) in enumerate(layers):
            for k in range(3):
                blk = bufs[t].at[l, ids[k], c]
                _remote(blk, blk, send, recv, 3 * q + k, (x, y, c)).wait_recv()
        for t in range(ns):
            for k in range(3):
                blk = small_out[t].at[ids[k]]
                _remote(blk, blk, send, recv, n_big + 3 * t + k, (x, y, c)).wait_recv()
        for cp in sends:
            cp.wait_send()
        for cp in started:
            cp.wait()

    out_shape = [jax.ShapeDtypeStruct(a.shape, a.dtype) for a in big]
    out_shape += [jax.ShapeDtypeStruct((4,) + a.shape, a.dtype) for a in small]
    return _call(body, name=name, in_specs=[ANY] * (nb + ns), out_specs=[ANY] * (nb + ns), out_shape=out_shape,
                 aliases={t: t for t in range(nb)},
                 scratch=[pltpu.SemaphoreType.DMA((n_rem,)), pltpu.SemaphoreType.DMA((n_rem,)),
                          pltpu.SemaphoreType.DMA((max(ns, 1),))])(*big, *small)


def all_gather_pair(buf, ids, *, name):
    L, _, _, Rh, C = buf.shape
    br = _swap_rows(Rh, C, 2)
    n_r = Rh // br

    def body(ids_ref, in_ref, o_ref, recv, ssem, rsem):
        step = (pl.program_id(0) * 3 + pl.program_id(1)) * n_r + pl.program_id(2)
        parts = _swap_with_sibling(in_ref, recv, ssem, rsem, step)
        for cp, slot, rows in parts:
            cp.wait_recv()
            o_ref[rows, :] = recv[slot, rows, :]
        for cp, _, _ in parts:
            cp.wait_send()

    at = lambda l, s, h, r: (((l * 4 + s) * 2 + h) * n_r + r, 0)
    out = _call(body, name=name, grid=(L, 3, n_r), prefetch=1,
                in_specs=[pl.BlockSpec((br, C), lambda l, k, r, ids: at(l, ids[1 + k], ids[0], r))],
                out_specs=pl.BlockSpec((br, C), lambda l, k, r, ids: at(l, ids[1 + k], 1 - ids[0], r)),
                out_shape=jax.ShapeDtypeStruct((L * 8 * Rh, C), buf.dtype), aliases={1: 0},
                scratch=_swap_scratch(br, C, BF16))(ids, buf.reshape(L * 8 * Rh, C))
    return out.reshape(buf.shape)


def rs_pair_add(grad, ids, *, name):
    _, _, Rh, C = grad.shape
    br = _swap_rows(Rh, C, 2)
    n_r = Rh // br

    def body(ids_ref, send_ref, keep_ref, pb_ref, own_ref, recv, ssem, rsem):
        ph = pl.program_id(0)
        parts = _swap_with_sibling(send_ref, recv, ssem, rsem, ph * n_r + pl.program_id(1))
        for cp, slot, rows in parts:
            cp.wait_recv()
            s = keep_ref[rows, :].astype(F32) + recv[slot, rows, :].astype(F32)

            @pl.when(ph == 0)
            def _():
                own_ref[rows, :] = s

            @pl.when(ph > 0)
            def _():
                pb_ref[rows, :] = s.astype(BF16)

        for cp, _, _ in parts:
            cp.wait_send()

    rel = lambda ph: (ph + 3) % 4
    at = lambda s, h, r: ((s * 2 + h) * n_r + r, 0)
    grad = grad.reshape(8 * Rh, C)
    return _call(
        body, name=name, grid=(4, n_r), prefetch=1,
        in_specs=[pl.BlockSpec((br, C), lambda ph, r, ids: at(ids[1 + rel(ph)], 1 - ids[0], r)),
                  pl.BlockSpec((br, C), lambda ph, r, ids: at(ids[1 + rel(ph)], ids[0], r))],
        out_specs=[pl.BlockSpec((None, br, C), lambda ph, r, ids: (jnp.maximum(ph - 1, 0), jnp.where(ph == 0, 0, r), 0)),
                   pl.BlockSpec((br, C), lambda ph, r, ids: (jnp.where(ph == 0, r, n_r - 1), 0))],
        out_shape=[jax.ShapeDtypeStruct((3, Rh, C), BF16), jax.ShapeDtypeStruct((Rh, C), F32)],
        scratch=_swap_scratch(br, C, BF16))(ids, grad, grad)


def rs_finish(owns, gots, *, name):
    L = len(owns)
    Rh, C = owns[0].shape
    br = _swap_rows(Rh, C, 4)
    n_r = Rh // br

    def body(*refs):
        own_refs, got_refs, o_ref = refs[:L], refs[L:2 * L], refs[2 * L]
        recv, ssem, rsem = refs[2 * L + 1:]
        l = pl.program_id(0)
        c = lax.axis_index("c")
        for ll in range(L):
            @pl.when(l == ll)
            def _():
                s = own_refs[ll][...]
                for k in range(3):
                    s = s + got_refs[ll][k].astype(F32)
                o_ref[c] = s

        parts = _swap_with_sibling(o_ref.at[c], recv, ssem, rsem, l * n_r + pl.program_id(1))
        for cp, slot, rows in parts:
            cp.wait_recv()
            o_ref[1 - c, rows, :] = recv[slot, rows, :]
        for cp, _, _ in parts:
            cp.wait_send()

    def at_layer(ll):
        return lambda l, r: jnp.where(l == ll, r, jnp.where(l < ll, 0, n_r - 1))

    in_specs = [pl.BlockSpec((br, C), lambda l, r, ll=ll: (at_layer(ll)(l, r), 0)) for ll in range(L)]
    in_specs += [pl.BlockSpec((3, br, C), lambda l, r, ll=ll: (0, at_layer(ll)(l, r), 0)) for ll in range(L)]
    out = _call(body, name=name, grid=(L, n_r), in_specs=in_specs,
                out_specs=pl.BlockSpec((2, br, C), lambda l, r: (l, r, 0)),
                out_shape=jax.ShapeDtypeStruct((L * 2, Rh, C), F32),
                scratch=_swap_scratch(br, C, F32))(*owns, *gots)
    return out.reshape(L, 2, Rh, C)


SMALL_ROW_MULTIPLE = 64


def all_reduce_small(buf, *, name):
    rows = buf.shape[0]
    h, q, e = rows // 2, rows // 4, rows // 8

    def body(x_ref, o_ref, s1, r1, s2, r2, s3, r3, send, recv):
        x, y, c, _ = _mesh_pos()
        sib, xn, yn = (x, y, 1 - c), (1 - x, y, c), (x, 1 - y, c)
        at = lambda off, n: pl.ds(pl.multiple_of(off, 8), n)
        cp = _remote(x_ref.at[at((1 - c) * h, h)], r1, send, recv, 0, sib)
        cp.start()
        cp.wait()
        s1[...] = x_ref[at(c * h, h), :] + r1[...]
        cp = _remote(s1.at[at((1 - x) * q, q)], r2, send, recv, 1, xn)
        cp.start()
        cp.wait()
        s2[...] = s1[at(x * q, q), :] + r2[...]
        cp = _remote(s2.at[at((1 - y) * e, e)], r3, send, recv, 2, yn)
        cp.start()
        cp.wait()
        s3[...] = s2[at(y * e, e), :] + r3[...]
        mine = c * h + x * q + y * e
        o_ref[at(mine, e), :] = s3[...]
        theirs = o_ref.at[at(c * h + x * q + (1 - y) * e, e)]
        cp = _remote(s3, o_ref.at[at(mine, e)], send, recv, 3, yn)
        cp.start()
        cp.wait_send()
        _remote(theirs, theirs, send, recv, 3, yn).wait_recv()
        quarter = o_ref.at[at(c * h + x * q, q)]
        theirs = o_ref.at[at(c * h + (1 - x) * q, q)]
        cp = _remote(quarter, quarter, send, recv, 4, xn)
        cp.start()
        cp.wait_send()
        _remote(theirs, theirs, send, recv, 4, xn).wait_recv()
        half = o_ref.at[at(c * h, h)]
        theirs = o_ref.at[at((1 - c) * h, h)]
        cp = _remote(half, half, send, recv, 5, sib)
        cp.start()
        cp.wait_send()
        _remote(theirs, theirs, send, recv, 5, sib).wait_recv()

    part = lambda n: pltpu.VMEM((n, LANES), F32)
    return _call(body, name=name, in_specs=[VMEM_SPEC], out_specs=VMEM_SPEC,
                 out_shape=jax.ShapeDtypeStruct(buf.shape, F32),
                 scratch=[part(h), part(h), part(q), part(q), part(e), part(e), pltpu.SemaphoreType.DMA((6,)),
                          pltpu.SemaphoreType.DMA((6,))])(buf)


def _pack(arrs, row_multiple=8):
    parts = []
    for a in arrs:
        f = a.reshape(-1).astype(F32)
        parts.append(jnp.pad(f, (0, (-f.shape[0]) % PACK_ALIGN)))
    total = sum(p.shape[0] for p in parts)
    parts.append(jnp.zeros(((-total) % (row_multiple * LANES),), F32))
    return jnp.concatenate(parts).reshape(-1, LANES)


def _unpack(buf, shapes):
    flat = buf.reshape(-1)
    out, off = [], 0
    for s in shapes:
        n = math.prod(s)
        out.append(flat[off:off + n].reshape(s))
        off += n + (-n) % PACK_ALIGN
    return out


_WEIGHTS = ['ev_w_in', 'ev_ln_v_g', 'ev_ln_v_b', 'ev_w_s', 'ev_b_s', 'ev_w_pool', 'ev_pool_scale', 'ev_w_out',
            'od_w_in', 'od_norm_g', 'od_w_out', 'lb_param', 'ffn_w_up', 'ffn_conv_w', 'ffn_conv_b', 'ffn_w_down',
            'ln1_g', 'ln1_b', 'ln2_g', 'ln2_b']
_BIG = ['ev_w_in', 'ev_w_out', 'od_w_in', 'od_w_out', 'ffn_w_up', 'ffn_w_down']
_SMALL = [n for n in _WEIGHTS if n not in _BIG]


def kernel(x, ev_w_in, ev_ln_v_g, ev_ln_v_b, ev_w_s, ev_b_s, ev_w_pool, ev_pool_scale, ev_w_out, od_w_in, od_norm_g, od_w_out, lb_param, ffn_w_up, ffn_conv_w, ffn_conv_b, ffn_w_down, ln1_g, ln1_b, ln2_g, ln2_b, loss_target, m_ev_w_in, m_ev_ln_v_g, m_ev_ln_v_b, m_ev_w_s, m_ev_b_s, m_ev_w_pool, m_ev_pool_scale, m_ev_w_out, m_od_w_in, m_od_norm_g, m_od_w_out, m_lb_param, m_ffn_w_up, m_ffn_conv_w, m_ffn_conv_b, m_ffn_w_down, m_ln1_g, m_ln1_b, m_ln2_g, m_ln2_b, v_ev_w_in, v_ev_ln_v_g, v_ev_ln_v_b, v_ev_w_s, v_ev_b_s, v_ev_w_pool, v_ev_pool_scale, v_ev_w_out, v_od_w_in, v_od_norm_g, v_od_w_out, v_lb_param, v_ffn_w_up, v_ffn_conv_w, v_ffn_conv_b, v_ffn_w_down, v_ln1_g, v_ln1_b, v_ln2_g, v_ln2_b):
    given = dict(locals())
    w = {n: given[n] for n in _WEIGHTS}
    mom = {n: given["m_" + n] for n in _WEIGHTS}
    vel = {n: given["v_" + n] for n in _WEIGHTS}
    x2d = x[0]
    tgt = loss_target[0]
    T, D = x2d.shape
    DA = ev_ln_v_g.shape[-1]
    DB = ev_pool_scale.shape[-1]
    HA = ev_w_s.shape[1]
    G = len(B_WINDOWS)
    CG = DB // G
    DC = 4 * od_norm_g.shape[-1]
    F = ffn_conv_b.shape[-1] // 2
    chip = 2 * lax.axis_index("x") + lax.axis_index("y")

    ids = mesh_ids()
    halves = lambda a: a.reshape(1, 4, 2, a.shape[2] // 2, a.shape[3])
    slot = {(n, l): halves(cast_to_slot(w[n], l, ids[4:5], f"cast_{n}{l}"))
            for n in _BIG for l in range(w[n].shape[0])}

    def riding(*keys):
        return [IciCopy("gather", slot[k]) for k in keys]

    def pair(buf, key):
        g = all_gather_pair(buf, ids, name=f"all_gather_pair_{key[0]}{key[1]}")
        return g.reshape(1, 4, g.shape[3] * 2, g.shape[4])

    early = [('ev_w_in', 0)]
    gathered = all_gather_chips([slot[k] for k in early], [ev_w_pool[0], ffn_conv_w, od_norm_g],
                                name="all_gather_chips")
    wpool_full = gathered[1].transpose(1, 0, 2, 3).reshape(G, CG, CG)
    cw_full = gathered[2].transpose(1, 2, 0, 3).reshape(DEPTH, 3, 2 * F)
    gn_full = gathered[3].reshape(1, DC)
    win0 = pair(gathered[0], early[0])[0]
    wup, wdn = {}, {}
    rh_up = D // 2
    cut1 = (rh_up * 7 // 100) // 16 * 16
    cut2 = cut1 + (rh_up * 13 // 100) // 16 * 16
    cb3 = ffn_conv_b.reshape(DEPTH, 1, 2 * F)
    ws = ev_w_s[0]
    wsT = jnp.swapaxes(ws, 1, 2)
    bsT = ev_b_s[0].T
    wpb = wpool_full.astype(BF16)
    ones = jnp.ones((1, D), F32)
    zeros = jnp.zeros((1, D), F32)
    row = lambda a, l: a[l:l + 1]

    Ns0 = win0.shape[-1]
    Nu = ffn_w_up.shape[-1]
    tm_big = _tile(T, 1024, 8)
    tm_ln = _tile(T, 512, 8)
    tk_ln = _tile(D, 512)
    n_p = DC // HP

    def nat_spec(Ns, tnw):
        nps = Ns // tnw
        return pl.BlockSpec((1, D, tnw), lambda i, j: (j // nps, 0, j % nps))

    perm_spec = pl.BlockSpec((4, D, HP), lambda i, j: (0, 0, j))

    xb16 = cast_bf16(x, "cast_x")[0]
    up0 = IciCopy("gather", slot[('ffn_w_up', 0)], rows=(0, cut1))
    ride = riding(('ev_w_out', 0))
    h0 = mm_nn(xb16, win0, w_spec=nat_spec(Ns0, Ns0), P=1, tnw=Ns0, tm=tm_big, n_j=4, name="ev_in",
               carry=ride + [up0])
    wout0 = pair(ride[0].out, ('ev_w_out', 0)).reshape(DA + DB, D)
    up0 = IciCopy("gather", up0.out, rows=(cut1, cut2 - cut1))
    cat = gating_fwd(h0, ev_ln_v_g, ev_ln_v_b, ws, bsT, wpb, ev_pool_scale, name="gating_fwd", carry=[up0])
    up0 = IciCopy("gather", up0.out, rows=(cut2, rh_up - cut2))

    def mix_ln(a, wmat, res, l, name, carry=()):
        K = a.shape[1]
        tk = _tile(K, 2048)
        return mm_ln(a, wmat, *res, row(ln1_g, l), row(ln1_b, l), w_spec=pl.BlockSpec((tk, D), lambda i, k: (k, 0)),
                     K=K, tk=tk, tm=tm_ln, name=name, carry=carry)

    def ffn_down(f, res, l, carry=()):
        tk = F // 4 if (F // 4) % LANES == 0 else _tile(F, 512)
        return mm_ln(f, wdn[l], *res, row(ln2_g, l), row(ln2_b, l),
                     w_spec=pl.BlockSpec((None, tk, D), lambda i, k: (0, k, 0)), K=F, tk=tk, tm=tm_ln,
                     name=f"ffn_down{l}", carry=carry)

    xh1, y1, rs1 = mix_ln(cat, wout0, (x2d, ones, zeros), 0, "ev_out", carry=[up0])
    wup[0] = pair(up0.out, ('ffn_w_up', 0))
    res1 = (xh1, row(ln1_g, 0), row(ln1_b, 0))
    ride = riding(('ffn_w_down', 0), ('od_w_in', 0))
    hf0, hc0, f0 = ffn_up(y1, wup[0], cw_full, cb3, 0, name="ffn_up0", carry=ride)
    wdn[0] = pair(ride[0].out, ('ffn_w_down', 0)).reshape(1, F, D)
    win1 = pair(ride[1].out, ('od_w_in', 0))[0]
    cut1 = (rh_up * 40 // 100) // 16 * 16
    cut2 = 2 * cut1
    up1 = IciCopy("gather", slot[('ffn_w_up', 1)], rows=(0, cut1))
    ride = riding(('od_w_out', 0))
    xh2, y2, rs2 = ffn_down(f0, res1, 0, carry=ride + [up1])
    wout1 = pair(ride[0].out, ('od_w_out', 0)).reshape(DC, D)
    res2 = (xh2, row(ln2_g, 0), row(ln2_b, 0))
    up1 = IciCopy("gather", up1.out, rows=(cut1, cut2 - cut1))
    h1p = mm_nn(y2, win1, w_spec=perm_spec, P=4, tnw=HP, tm=tm_big, n_j=n_p, name="od_in", carry=[up1])
    up1 = IciCopy("gather", up1.out, rows=(cut2, rh_up - cut2))
    yh, o_saved, sp = hgrn_fwd(h1p, lb_param, gn_full, name="hgrn_fwd", carry=[up1])
    wup[1] = pair(up1.out, ('ffn_w_up', 1))
    xh3, y3, rs3 = mix_ln(yh, wout1, res2, 1, "od_out")
    res3 = (xh3, row(ln1_g, 1), row(ln1_b, 1))
    ride = riding(('ffn_w_down', 1))
    hf1, hc1, f1 = ffn_up(y3, wup[1], cw_full, cb3, 1, name="ffn_up1", carry=ride)
    wdn[1] = pair(ride[0].out, ('ffn_w_down', 1)).reshape(1, F, D)
    tk_dn = F // 4 if (F // 4) % LANES == 0 else _tile(F, 512)
    loss_p, dr, drb, dg_ln2_1, db_ln2_1 = mm_ln_loss(
        f1, wdn[1], *res3, row(ln2_g, 1), row(ln2_b, 1), tgt,
        w_spec=pl.BlockSpec((None, tk_dn, D), lambda i, k: (0, k, 0)), K=F, tk=tk_dn, tm=tm_ln, name="ffn_down1_loss")

    tt = _tile(T, 2048, 16)
    n_t = T // tt
    tnu = Nu // 2 if (Nu // 2) % LANES == 0 else Nu
    upb = Nu // tnu
    tkd = _tile(D, 1024)

    def pair_sum(g4, name):
        pb, own = rs_pair_add(g4.reshape(4, 2, g4.shape[1] // 2, g4.shape[2]), ids, name="rs_pair_add_" + name)
        return IciCopy("scatter", pb), own

    def g_out(a, gb, name, tkk=None):
        K = a.shape[1]
        tkk = tkk or _tile(K, 1024)
        return mm_tn(a, gb, a_spec=pl.BlockSpec((tt, tkk), lambda kb, nb, t: (t, kb)),
                     g_spec=pl.BlockSpec((tt, tkd), lambda kb, nb, t: (t, nb)),
                     o_spec=pl.BlockSpec((1, tkk, tkd), lambda kb, nb, t: (0, kb, nb)), out_shape=(1, K, D),
                     grid=(K // tkk, D // tkd, n_t), acc_shape=(tkk, tkd), P=1, tnw=tkd, name=name)

    def ffn_bwd(l, dr2, dr2b, f, hf, hc, y_in, xh_in, rs_in):
        g_dn = g_out(f, dr2b, f"g_ffn_down{l}", tkk=F // 4 if (F // 4) % LANES == 0 else None)
        rs_dn = pair_sum(g_dn.reshape(4, F // 4, D), f"ffn_down{l}")
        dh, dcw, dcb = ffn_dgate(dr2b, wdn[l], hf, hc, cw_full, l, name=f"ffn_dgate{l}", carry=[rs_dn[0]])
        g_up = mm_tn(y_in, dh, a_spec=pl.BlockSpec((tt, tkd), lambda kb, nb, t: (t, kb)),
                     g_spec=pl.BlockSpec((None, tt, tnu), lambda kb, nb, t: (nb // (2 * upb), t, nb % (2 * upb))),
                     o_spec=pl.BlockSpec((1, tkd, tnu), lambda kb, nb, t: (nb // upb, kb, nb % upb)),
                     out_shape=(4, D, Nu), grid=(D // tkd, 4 * upb, n_t), acc_shape=(tkd, tnu), P=1, tnw=tnu,
                     name=f"g_ffn_up{l}")
        rs_up = pair_sum(g_up, f"ffn_up{l}")
        tku = tnu
        kps = Nu // tku
        out = mm_nt_res(dh, wup[l], dr2, (xh_in, rs_in, row(ln1_g, l)),
                        a_spec=pl.BlockSpec((None, tm_ln, tku), lambda i, k: (k // (2 * kps), i, k % (2 * kps))),
                        w_spec=pl.BlockSpec((None, 1, D, tku), lambda i, k: (0, k // kps, 0, k % kps)),
                        P=1, tnw=tku, n_k=4 * kps, tm=tm_ln, name=f"d_ffn_in{l}", carry=[rs_up[0]])
        return rs_dn, rs_up, dcw, dcb, out

    rs_dn1, rs_up1, dcw1, dcb1, (dr1, dr1b, dg_ln1_1, db_ln1_1) = ffn_bwd(1, dr, drb, f1, hf1, hc1, y3, xh3, rs3)
    rs_wout1 = pair_sum(g_out(yh, dr1b, "g_od_out").reshape(4, DC // 4, D), "od_out")
    dyh = mm_nt_plain(dr1b, wout1, tm=tm_big, tn=_tile(DC, 512), name="d_od_out")
    dh1p, d_gn, d_lbp = hgrn_bwd(h1p, o_saved, dyh, sp, lb_param, gn_full, name="hgrn_bwd", carry=[rs_wout1[0]])
    g_win1 = mm_tn(y2, dh1p, a_spec=pl.BlockSpec((tt, tkd), lambda kb, nb, t: (t, kb)),
                   g_spec=pl.BlockSpec((tt, 4 * HP), lambda kb, nb, t: (t, nb)),
                   o_spec=pl.BlockSpec((4, tkd, HP), lambda kb, nb, t: (0, kb, nb)), out_shape=(4, D, DC),
                   grid=(D // tkd, n_p, n_t), acc_shape=(tkd, 4 * HP), P=4, tnw=HP, name="g_od_in")
    rs_win1 = pair_sum(g_win1, "od_in")
    dr, drb, dg_ln2_0, db_ln2_0 = mm_nt_res(
        dh1p, win1, dr1, (xh2, rs2, row(ln2_g, 0)), a_spec=pl.BlockSpec((tm_ln, 4 * HP), lambda i, k: (i, k)),
        w_spec=pl.BlockSpec((4, D, HP), lambda i, k: (0, 0, k)), P=4, tnw=HP, n_k=n_p, tm=tm_ln, name="d_od_in",
        carry=[rs_win1[0]])
    rs_dn0, rs_up0, dcw0, dcb0, (dr1, dr1b, dg_ln1_0, db_ln1_0) = ffn_bwd(0, dr, drb, f0, hf0, hc0, y1, xh1, rs1)
    rs_wout0 = pair_sum(g_out(cat, dr1b, "g_ev_out").reshape(4, (DA + DB) // 4, D), "ev_out")
    dcat = mm_nt_plain(dr1b, wout0, tm=tm_big, tn=_tile(DA + DB, 512), name="d_ev_out")
    dh0, d_ws, d_bsT, d_lg, d_lb, d_sc, d_wp = gating_bwd(h0, dcat, ev_ln_v_g, ev_ln_v_b, ws, wsT, bsT, wpb,
                                                          ev_pool_scale, name="gating_bwd", carry=[rs_wout0[0]])
    g_win0 = mm_tn(xb16, dh0, a_spec=pl.BlockSpec((tt, tkd), lambda kb, nb, t: (t, kb)),
                   g_spec=pl.BlockSpec((tt, Ns0), lambda kb, nb, t: (t, nb)),
                   o_spec=pl.BlockSpec((1, tkd, Ns0), lambda kb, nb, t: (nb, kb, 0)), out_shape=(4, D, Ns0),
                   grid=(D // tkd, 4, n_t), acc_shape=(tkd, Ns0), P=1, tnw=Ns0, name="g_ev_in")
    rs_win0 = pair_sum(g_win0, "ev_in")
    grad_x = mm_nt_res(dh0, win0, dr1, None, a_spec=pl.BlockSpec((tm_ln, Ns0), lambda i, k: (i, k)),
                       w_spec=pl.BlockSpec((1, D, Ns0), lambda i, k: (k, 0, 0)), P=1, tnw=Ns0, n_k=4, tm=tm_ln,
                       name="d_ev_in", carry=[rs_win0[0]])

    per_weight = [[rs_win0], [rs_wout0], [rs_win1], [rs_wout1], [rs_up0, rs_up1], [rs_dn0, rs_dn1]]
    shared = [rs_finish([own for _, own in m], [cp.out for cp, _ in m], name="rs_finish_" + n)
              for n, m in zip(_BIG, per_weight)]
    big_g = {n: s.reshape(w[n].shape) for n, s in zip(_BIG, shared)}

    small_full = {
        'ev_ln_v_g': d_lg, 'ev_ln_v_b': d_lb, 'ev_w_s': d_ws[None], 'ev_b_s': d_bsT.T[None], 'ev_w_pool': d_wp[None],
        'ev_pool_scale': d_sc, 'od_norm_g': d_gn, 'lb_param': d_lbp,
        'ffn_conv_w': jnp.stack([jnp.concatenate([dcw0[0], dcw0[1]], axis=-1),
                                 jnp.concatenate([dcw1[0], dcw1[1]], axis=-1)]),
        'ffn_conv_b': jnp.stack([jnp.concatenate([dcb0[0, 0], dcb0[1, 0]]), jnp.concatenate([dcb1[0, 0], dcb1[1, 0]])]),
        'ln1_g': jnp.concatenate([dg_ln1_0, dg_ln1_1]), 'ln1_b': jnp.concatenate([db_ln1_0, db_ln1_1]),
        'ln2_g': jnp.concatenate([dg_ln2_0, dg_ln2_1]), 'ln2_b': jnp.concatenate([db_ln2_0, db_ln2_1])}
    packed = _pack([small_full[n] for n in _SMALL] + [loss_p[0, 0:1]], SMALL_ROW_MULTIPLE)
    reduced = _unpack(all_reduce_small(packed, name="all_reduce_small"),
                      [small_full[n].shape for n in _SMALL] + [(1,)])
    small_g = dict(zip(_SMALL, reduced[:-1]))
    loss = reduced[-1][0]
    small_g['ev_w_pool'] = lax.dynamic_slice_in_dim(small_g['ev_w_pool'], chip * (CG // 4), CG // 4, axis=2)
    small_g['ffn_conv_w'] = lax.dynamic_slice_in_dim(small_g['ffn_conv_w'], chip * (F // 2), F // 2, axis=2)
    small_g['od_norm_g'] = lax.dynamic_slice_in_dim(small_g['od_norm_g'], chip * (DC // 4), DC // 4, axis=1)

    grads, delta, new_m, new_v = {}, {}, {}, {}
    for n in _BIG:
        grads[n], delta[n], new_m[n], new_v[n] = adamw(w[n], big_g[n], mom[n], vel[n], "adamw_" + n)
    upd = adamw_small(*[[d[n] for n in _SMALL] for d in (w, small_g, mom, vel)], "adamw_small")
    for d, outs in zip((grads, delta, new_m, new_v), upd):
        d.update(zip(_SMALL, outs))

    return (loss, grad_x[None], *[grads[n] for n in _WEIGHTS], *[delta[n] for n in _WEIGHTS],
            *[new_m[n] for n in _WEIGHTS], *[new_v[n] for n in _WEIGHTS])
```

```python
import math

import jax
import jax.numpy as jnp
from jax import lax
from jax.experimental import pallas as pl
from jax.experimental.pallas import tpu as pltpu

F32 = jnp.float32
BF16 = jnp.bfloat16
MESH = pl.DeviceIdType.MESH
ANY = pl.BlockSpec(memory_space=pl.ANY)
VMEM_SPEC = pl.BlockSpec(memory_space=pltpu.VMEM)

DEPTH = 2
ALPHA = (2 * DEPTH) ** 0.25
LN_EPS = 1e-5
A_HEAD = 128
A_CHUNK = 128
B_WINDOWS = (2, 4, 8, 16)
POOL_HALO = 16
C_HEAD = 128
C_CHUNK = 64
CONV_HALO = 8
ADAM_LR = 0.001
ADAM_B1 = 0.9
ADAM_B2 = 0.999
ADAM_EPS = 1e-08
ADAM_WD = 0.01
ADAM_STEP = 10
V7X_VMEM_LIMIT_BYTES = 56 * 1024 * 1024
LANES = 128
PACK_ALIGN = 8 * LANES


class IciCopy:
    def __init__(self, kind, arr, rows=None):
        self.kind, self.arr, self.out = kind, arr, None
        self.rows = rows


def _carried_copies(items, in_refs, out_refs, send, recv):
    x, y, c, chips = _mesh_pos()
    me = 2 * x + y
    sends, lands = [], []
    for q, (it, src, dst) in enumerate(zip(items, in_refs, out_refs)):
        rows = pl.ds(*(it.rows or (0, it.arr.shape[-2])))
        for k, (cx, cy) in enumerate(chips):
            if it.kind == "gather":
                mine, theirs = dst.at[0, me, c, rows], dst.at[0, 2 * cx + cy, c, rows]
                sends.append(_remote(mine, mine, send, recv, 3 * q + k, (cx, cy, c)))
            else:
                theirs = dst.at[k]
                sends.append(_remote(src.at[k], theirs, send, recv, 3 * q + k, (cx, cy, c)))
            lands.append(_remote(theirs, theirs, send, recv, 3 * q + k, (x, y, c)))
    return sends, lands


def _call(body, *, name, out_shape, grid=(), in_specs=None, out_specs=None, scratch=(), prefetch=0, aliases=None,
          carry=()):
    single = not isinstance(out_specs, (list, tuple))
    in_specs = list(in_specs)
    out_specs = [out_specs] if single else list(out_specs)
    out_shape = [out_shape] if single else list(out_shape)
    scratch = list(scratch)
    aliases = dict(aliases or {})
    n_in, n_out, n_sc, n_c = len(in_specs), len(out_specs), len(scratch), len(carry)
    inner = body
    if n_c:
        assert grid, "a carrier needs a grid"
        for q, it in enumerate(carry):
            if it.kind == "gather":
                aliases[prefetch + n_in + q] = n_out + q
        in_specs += [ANY] * n_c
        out_specs += [ANY] * n_c
        out_shape += [jax.ShapeDtypeStruct(it.arr.shape, it.arr.dtype) for it in carry]
        scratch += [pltpu.SemaphoreType.DMA((3 * n_c,)), pltpu.SemaphoreType.DMA((3 * n_c,))]

        def inner(*refs):
            pre, refs = refs[:prefetch], refs[prefetch:]
            ins, c_in = refs[:n_in], refs[n_in:n_in + n_c]
            outs, c_out = refs[n_in + n_c:n_in + n_c + n_out], refs[n_in + n_c + n_out:n_in + 2 * n_c + n_out]
            rest = refs[n_in + 2 * n_c + n_out:]
            first = last = True
            for d, n in enumerate(grid):
                first = jnp.logical_and(first, pl.program_id(d) == 0)
                last = jnp.logical_and(last, pl.program_id(d) == n - 1)

            @pl.when(first)
            def _():
                for cp in _carried_copies(carry, c_in, c_out, rest[-2], rest[-1])[0]:
                    cp.start()

            body(*pre, *ins, *outs, *rest[:n_sc])

            @pl.when(last)
            def _():
                sends, lands = _carried_copies(carry, c_in, c_out, rest[-2], rest[-1])
                for cp in lands:
                    cp.wait_recv()
                for cp in sends:
                    cp.wait_send()

    spec = pltpu.PrefetchScalarGridSpec(num_scalar_prefetch=prefetch, grid=grid, in_specs=in_specs,
                                        out_specs=out_specs, scratch_shapes=scratch)
    fn = pl.pallas_call(inner, name=name, grid_spec=spec, out_shape=out_shape, input_output_aliases=aliases,
                        compiler_params=pltpu.CompilerParams(vmem_limit_bytes=V7X_VMEM_LIMIT_BYTES))

    def run(*args):
        res = fn(*args, *[it.arr for it in carry])
        for it, o in zip(carry, res[n_out:]):
            it.out = o
        return res[0] if single else list(res[:n_out])

    return run


def _tile(n, pref, unit=LANES):
    if n <= pref:
        return n
    t = (pref // unit) * unit
    while t > unit and n % t:
        t -= unit
    assert n % t == 0, (n, pref, unit)
    return t


def _slabs(n, rows=128):
    return [slice(r, min(r + rows, n)) for r in range(0, n, rows)]


def _dot(a, b):
    return jnp.dot(a, b, preferred_element_type=F32)


def _dot_nt(a, b):
    return lax.dot_general(a, b, (((1,), (1,)), ((), ())), preferred_element_type=F32)


def _dot_tn(a, b):
    return lax.dot_general(a, b, (((0,), (0,)), ((), ())), preferred_element_type=F32)


def _sigmoid(x):
    return jax.nn.sigmoid(x)


_GELU_C = math.sqrt(2.0 / math.pi)


def _gelu(x):
    return 0.5 * x * (1.0 + jnp.tanh(_GELU_C * (x + 0.044715 * x * x * x)))


def _gelu_grad(x):
    th = jnp.tanh(_GELU_C * (x + 0.044715 * x * x * x))
    return 0.5 * (1.0 + th) + 0.5 * x * (1.0 - th * th) * _GELU_C * (1.0 + 3.0 * 0.044715 * x * x)


def _ln_fwd(r, g, b):
    mu = jnp.mean(r, axis=-1, keepdims=True)
    xc = r - mu
    var = jnp.mean(xc * xc, axis=-1, keepdims=True)
    rstd = lax.rsqrt(var + LN_EPS)
    xh = xc * rstd
    return xh * g + b, xh, rstd


def _ln_bwd(dy, xh, rstd, g):
    dxh = dy * g
    m1 = jnp.mean(dxh, axis=-1, keepdims=True)
    m2 = jnp.mean(dxh * xh, axis=-1, keepdims=True)
    dr = rstd * (dxh - m1 - xh * m2)
    return dr, jnp.sum(dy * xh, axis=0, keepdims=True), jnp.sum(dy, axis=0, keepdims=True)


def _exact_tri_dot(tri, x):
    hi = x.astype(BF16)
    r1 = x - hi.astype(F32)
    mid = r1.astype(BF16)
    lo = (r1 - mid.astype(F32)).astype(BF16)
    return _dot(tri, hi) + _dot(tri, mid) + _dot(tri, lo)


def cast_bf16(a3, name):
    L, R, C = a3.shape
    br = _tile(R, max(8, (1 << 20) // C), 8)

    def body(a_ref, o_ref):
        o_ref[...] = a_ref[...].astype(BF16)

    return _call(body, name=name, grid=(L, R // br),
                 in_specs=[pl.BlockSpec((None, br, C), lambda l, r: (l, r, 0))],
                 out_specs=pl.BlockSpec((None, br, C), lambda l, r: (l, r, 0)),
                 out_shape=jax.ShapeDtypeStruct((L, R, C), BF16))(a3)


def adamw(w, g, m, v, name):
    L, R, C = w.shape
    br = _tile(R, max(8, (1 << 19) // C), 8)

    def body(w_ref, g_ref, m_ref, v_ref, go_ref, d_ref, nm_ref, nv_ref):
        _adamw_update(w_ref, g_ref, m_ref, v_ref, go_ref, d_ref, nm_ref, nv_ref)

    spec = pl.BlockSpec((None, br, C), lambda l, r: (l, r, 0))
    sds = jax.ShapeDtypeStruct((L, R, C), F32)
    return _call(body, name=name, grid=(L, R // br), in_specs=[spec] * 4, out_specs=[spec] * 4,
                 out_shape=[sds] * 4)(w, g, m, v)


def _adamw_update(w_ref, g_ref, m_ref, v_ref, go_ref, d_ref, nm_ref, nv_ref):
    gg = g_ref[...]
    nm = ADAM_B1 * m_ref[...] + (1.0 - ADAM_B1) * gg
    nv = ADAM_B2 * v_ref[...] + (1.0 - ADAM_B2) * (gg * gg)
    go_ref[...] = gg
    d_ref[...] = -ADAM_LR * ((nm / (1.0 - ADAM_B1 ** ADAM_STEP))
                             / (jnp.sqrt(nv / (1.0 - ADAM_B2 ** ADAM_STEP)) + ADAM_EPS) + ADAM_WD * w_ref[...])
    nm_ref[...] = nm
    nv_ref[...] = nv


def adamw_small(ws, gs, ms, vs, name):
    n = len(ws)

    def body(*refs):
        for t in range(n):
            _adamw_update(*[refs[k * n + t] for k in range(8)])

    args = [a for group in (ws, gs, ms, vs) for a in group]
    out = _call(body, name=name, in_specs=[VMEM_SPEC] * (4 * n), out_specs=[VMEM_SPEC] * (4 * n),
                out_shape=[jax.ShapeDtypeStruct(a.shape, F32) for a in args])(*args)
    return [[out[k * n + t] for t in range(n)] for k in range(4)]


def mm_nn(a, w, *, w_spec, P, tnw, tm, n_j, name, carry=()):
    T, K = a.shape
    bw = P * tnw

    def body(a_ref, w_ref, o_ref):
        av = a_ref[...]
        for p in range(P):
            o_ref[:, p * tnw:(p + 1) * tnw] = _dot(av, w_ref[p]).astype(BF16)

    return _call(body, name=name, grid=(T // tm, n_j),
                 in_specs=[pl.BlockSpec((tm, K), lambda i, j: (i, 0)), w_spec],
                 out_specs=pl.BlockSpec((tm, bw), lambda i, j: (i, j)),
                 out_shape=jax.ShapeDtypeStruct((T, n_j * bw), BF16), carry=carry)(a, w)


def mm_ln(a, w, res, rg, rb, g, b, *, w_spec, K, tk, tm, name, carry=()):
    T, N = res.shape
    n_k = K // tk

    def body(a_ref, w_ref, res_ref, rg_ref, rb_ref, g_ref, b_ref, xh_ref, y_ref, rs_ref, acc):
        k = pl.program_id(1)

        @pl.when(k == 0)
        def _():
            acc[...] = jnp.zeros_like(acc)

        acc[...] += _dot(a_ref[...], w_ref[...])

        @pl.when(k == n_k - 1)
        def _():
            for rows in _slabs(tm):
                r = ALPHA * (res_ref[rows, :] * rg_ref[...] + rb_ref[...]) + acc[rows, :]
                y, xh, rstd = _ln_fwd(r, g_ref[...], b_ref[...])
                xh_ref[rows, :] = xh
                y_ref[rows, :] = y.astype(BF16)
                rs_ref[rows, :] = rstd

    row = pl.BlockSpec((tm, N), lambda i, k: (i, 0))
    vec = pl.BlockSpec((1, N), lambda i, k: (0, 0))
    return _call(body, name=name, grid=(T // tm, n_k),
                 in_specs=[pl.BlockSpec((tm, tk), lambda i, k: (i, k)), w_spec, row, vec, vec, vec, vec],
                 out_specs=[row, row, pl.BlockSpec((tm, 1), lambda i, k: (i, 0))],
                 out_shape=[jax.ShapeDtypeStruct((T, N), F32), jax.ShapeDtypeStruct((T, N), BF16),
                            jax.ShapeDtypeStruct((T, 1), F32)],
                 scratch=[pltpu.VMEM((tm, N), F32)], carry=carry)(a, w, res, rg, rb, g, b)


def mm_ln_loss(a, w, res, rg, rb, g, b, target, *, w_spec, K, tk, tm, name):
    T, N = res.shape
    n_k = K // tk

    def body(a_ref, w_ref, res_ref, rg_ref, rb_ref, g_ref, b_ref, t_ref, ls_ref, dr_ref, drb_ref, dg_ref, db_ref, acc):
        i = pl.program_id(0)
        k = pl.program_id(1)

        @pl.when(k == 0)
        def _():
            acc[...] = jnp.zeros_like(acc)

        acc[...] += _dot(a_ref[...], w_ref[...])

        @pl.when(k == n_k - 1)
        def _():
            @pl.when(i == 0)
            def _():
                for r in (ls_ref, dg_ref, db_ref):
                    r[...] = jnp.zeros_like(r)

            for rows in _slabs(tm):
                r = ALPHA * (res_ref[rows, :] * rg_ref[...] + rb_ref[...]) + acc[rows, :]
                y, xh, rstd = _ln_fwd(r, g_ref[...], b_ref[...])
                e = y - t_ref[rows, :]
                ls_ref[...] += 0.5 * jnp.sum(jnp.mean(e * e, axis=-1, keepdims=True), axis=0, keepdims=True)
                dr, dg, db = _ln_bwd(e / N, xh, rstd, g_ref[...])
                dr_ref[rows, :] = dr
                drb_ref[rows, :] = dr.astype(BF16)
                dg_ref[...] += dg
                db_ref[...] += db

    row = pl.BlockSpec((tm, N), lambda i, k: (i, 0))
    vec = pl.BlockSpec((1, N), lambda i, k: (0, 0))
    return _call(body, name=name, grid=(T // tm, n_k),
                 in_specs=[pl.BlockSpec((tm, tk), lambda i, k: (i, k)), w_spec, row, vec, vec, vec, vec, row],
                 out_specs=[pl.BlockSpec((1, LANES), lambda i, k: (0, 0)), row, row, vec, vec],
                 out_shape=[jax.ShapeDtypeStruct((1, LANES), F32), jax.ShapeDtypeStruct((T, N), F32),
                            jax.ShapeDtypeStruct((T, N), BF16), jax.ShapeDtypeStruct((1, N), F32),
                            jax.ShapeDtypeStruct((1, N), F32)],
                 scratch=[pltpu.VMEM((tm, N), F32)])(a, w, res, rg, rb, g, b, target)


def mm_nt_plain(a, w, *, tm, tn, name):
    T, K = a.shape
    N = w.shape[0]

    def body(a_ref, w_ref, o_ref):
        o_ref[...] = _dot_nt(a_ref[...], w_ref[...]).astype(BF16)

    return _call(body, name=name, grid=(T // tm, N // tn),
                 in_specs=[pl.BlockSpec((tm, K), lambda i, j: (i, 0)), pl.BlockSpec((tn, K), lambda i, j: (j, 0))],
                 out_specs=pl.BlockSpec((tm, tn), lambda i, j: (i, j)),
                 out_shape=jax.ShapeDtypeStruct((T, N), BF16))(a, w)


def mm_nt_res(a, w, res, ln, *, a_spec, w_spec, P, tnw, n_k, tm, name, carry=()):
    T, N = res.shape
    n_i = T // tm

    def body(*refs):
        if ln is None:
            a_ref, w_ref, res_ref, o_ref, acc = refs
        else:
            a_ref, w_ref, res_ref, xh_ref, rs_ref, g_ref, dr_ref, drb_ref, dg_ref, db_ref, acc = refs
        i = pl.program_id(0)
        k = pl.program_id(1)

        @pl.when(k == 0)
        def _():
            acc[...] = jnp.zeros_like(acc)

        wv = w_ref[0] if P == 1 else jnp.concatenate([w_ref[p] for p in range(P)], axis=1)
        acc[...] += _dot_nt(a_ref[...], wv)

        @pl.when(k == n_k - 1)
        def _():
            if ln is not None:
                @pl.when(i == 0)
                def _():
                    dg_ref[...] = jnp.zeros_like(dg_ref)
                    db_ref[...] = jnp.zeros_like(db_ref)

            for rows in _slabs(tm):
                d = ALPHA * res_ref[rows, :] + acc[rows, :]
                if ln is None:
                    o_ref[rows, :] = d
                else:
                    dr, dg, db = _ln_bwd(d, xh_ref[rows, :], rs_ref[rows, :], g_ref[...])
                    dr_ref[rows, :] = dr
                    drb_ref[rows, :] = dr.astype(BF16)
                    dg_ref[...] += dg
                    db_ref[...] += db

    row = pl.BlockSpec((tm, N), lambda i, k: (i, 0))
    vec = pl.BlockSpec((1, N), lambda i, k: (0, 0))
    scratch = [pltpu.VMEM((tm, N), F32)]
    if ln is None:
        return _call(body, name=name, grid=(n_i, n_k), in_specs=[a_spec, w_spec, row], out_specs=row,
                     out_shape=jax.ShapeDtypeStruct((T, N), F32), scratch=scratch, carry=carry)(a, w, res)
    xh, rstd, g = ln
    return _call(body, name=name, grid=(n_i, n_k),
                 in_specs=[a_spec, w_spec, row, row, pl.BlockSpec((tm, 1), lambda i, k: (i, 0)), vec],
                 out_specs=[row, row, vec, vec],
                 out_shape=[jax.ShapeDtypeStruct((T, N), F32), jax.ShapeDtypeStruct((T, N), BF16),
                            jax.ShapeDtypeStruct((1, N), F32), jax.ShapeDtypeStruct((1, N), F32)],
                 scratch=scratch, carry=carry)(a, w, res, xh, rstd, g)


def mm_tn(a, g, *, a_spec, g_spec, o_spec, out_shape, grid, acc_shape, P, tnw, name):
    n_t = grid[2]

    def body(a_ref, g_ref, o_ref, acc):
        t = pl.program_id(2)

        @pl.when(t == 0)
        def _():
            acc[...] = jnp.zeros_like(acc)

        acc[...] += _dot_tn(a_ref[...], g_ref[...])

        @pl.when(t == n_t - 1)
        def _():
            for p in range(P):
                o_ref[p] = acc[:, p * tnw:(p + 1) * tnw].astype(BF16)

    return _call(body, name=name, grid=grid, in_specs=[a_spec, g_spec], out_specs=o_spec,
                 out_shape=jax.ShapeDtypeStruct(out_shape, BF16),
                 scratch=[pltpu.VMEM(acc_shape, F32)])(a, g)


def _causal_conv(ext, halo, w, b):
    s1 = pltpu.roll(ext, 1, 0)[halo:]
    s2 = pltpu.roll(ext, 2, 0)[halo:]
    return b + w[2:3] * ext[halo:] + w[1:2] * s1 + w[0:1] * s2, s1, s2


def ffn_up(xb, wup, cw, cb, l, *, name, carry=()):
    T, D = xb.shape
    Ns = wup.shape[-1]
    F = 2 * Ns
    tn = _tile(Ns, 256)
    nps = Ns // tn
    n_j = F // tn
    tm = _tile(T, 1024, 8)

    def body(x_ref, wa_ref, wv_ref, cwa_ref, cwv_ref, cba_ref, cbv_ref, h_ref, hc_ref, f_ref, carry):
        i = pl.program_id(1)

        @pl.when(i == 0)
        def _():
            carry[...] = jnp.zeros_like(carry)

        xv = x_ref[...]
        ha = _dot(xv, wa_ref[...])
        hv = _dot(xv, wv_ref[...])
        ca, _, _ = _causal_conv(jnp.concatenate([carry[0], ha], axis=0), CONV_HALO, cwa_ref[...], cba_ref[...])
        cv, _, _ = _causal_conv(jnp.concatenate([carry[1], hv], axis=0), CONV_HALO, cwv_ref[...], cbv_ref[...])
        carry[0] = ha[tm - CONV_HALO:]
        carry[1] = hv[tm - CONV_HALO:]
        h_ref[0] = ha.astype(BF16)
        h_ref[1] = hv.astype(BF16)
        hc_ref[0] = ca.astype(BF16)
        hc_ref[1] = cv.astype(BF16)
        f_ref[...] = (ca * _sigmoid(ca) * cv).astype(BF16)

    wspec_a = pl.BlockSpec((None, None, D, tn), lambda j, i: (0, j // nps, 0, j % nps))
    wspec_v = pl.BlockSpec((None, None, D, tn), lambda j, i: (0, 2 + j // nps, 0, j % nps))
    pair_tile = pl.BlockSpec((2, tm, tn), lambda j, i: (0, i, j))
    return _call(
        body, name=name, grid=(n_j, T // tm),
        in_specs=[pl.BlockSpec((tm, D), lambda j, i: (i, 0)), wspec_a, wspec_v,
                  pl.BlockSpec((None, 3, tn), lambda j, i: (l, 0, j)),
                  pl.BlockSpec((None, 3, tn), lambda j, i: (l, 0, n_j + j)),
                  pl.BlockSpec((None, 1, tn), lambda j, i: (l, 0, j)),
                  pl.BlockSpec((None, 1, tn), lambda j, i: (l, 0, n_j + j))],
        out_specs=[pair_tile, pair_tile, pl.BlockSpec((tm, tn), lambda j, i: (i, j))],
        out_shape=[jax.ShapeDtypeStruct((2, T, F), BF16), jax.ShapeDtypeStruct((2, T, F), BF16),
                   jax.ShapeDtypeStruct((T, F), BF16)],
        scratch=[pltpu.VMEM((2, CONV_HALO, tn), F32)], carry=carry)(xb, wup, wup, cw, cw, cb, cb)


def ffn_dgate(db16, wdn, h, hc, cw, l, *, name, carry=()):
    T, D = db16.shape
    F = h.shape[-1]
    tn = _tile(F, 512)
    n_j = F // tn
    tm = _tile(T, 512, 16)
    n_i = T // tm
    n_ext = tm + CONV_HALO

    def body(d_ref, w_ref, h_ref, hc_ref, cwa_ref, cwv_ref, dh_ref, dcw_ref, dcb_ref, carry):
        ip = pl.program_id(1)

        @pl.when(ip == 0)
        def _():
            carry[...] = jnp.zeros_like(carry)
            dcw_ref[...] = jnp.zeros_like(dcw_ref)
            dcb_ref[...] = jnp.zeros_like(dcb_ref)

        df_all = _dot_nt(d_ref[...], w_ref[...])
        for cs in _slabs(tn, max(LANES, tn // 2)):
            df = df_all[:, cs]
            ca = hc_ref[0, :, cs].astype(F32)
            cv = hc_ref[1, :, cs].astype(F32)
            sig = _sigmoid(ca)
            sil = ca * sig
            da = df * cv * (sig + sil * (1.0 - sig))
            dv = df * sil
            for half, (dc, w_ref_h) in enumerate(((da, cwa_ref), (dv, cwv_ref))):
                w = w_ref_h[:, cs]
                h0 = h_ref[half, :, cs].astype(F32)
                ext = jnp.concatenate([dc, carry[half, :, cs]], axis=0)
                n1 = pltpu.roll(ext, n_ext - 1, 0)[:tm]
                n2 = pltpu.roll(ext, n_ext - 2, 0)[:tm]
                dcb_ref[half, :, cs] += jnp.sum(dc, axis=0, keepdims=True)
                dcw_ref[half, :, cs] += jnp.concatenate(
                    [jnp.sum(n2 * h0, axis=0, keepdims=True), jnp.sum(n1 * h0, axis=0, keepdims=True),
                     jnp.sum(dc * h0, axis=0, keepdims=True)], axis=0)
                dh_ref[half, :, cs] = (w[2:3] * dc + w[1:2] * n1 + w[0:1] * n2).astype(BF16)
                carry[half, :, cs] = dc[:CONV_HALO]

    rev = lambda ip: n_i - 1 - ip
    tile = pl.BlockSpec((2, tm, tn), lambda j, ip: (0, rev(ip), j))
    return _call(
        body, name=name, grid=(n_j, n_i),
        in_specs=[pl.BlockSpec((tm, D), lambda j, ip: (rev(ip), 0)),
                  pl.BlockSpec((None, tn, D), lambda j, ip: (0, j, 0)), tile, tile,
                  pl.BlockSpec((None, 3, tn), lambda j, ip: (l, 0, j)),
                  pl.BlockSpec((None, 3, tn), lambda j, ip: (l, 0, n_j + j))],
        out_specs=[tile, pl.BlockSpec((2, 3, tn), lambda j, ip: (0, 0, j)),
                   pl.BlockSpec((2, 1, tn), lambda j, ip: (0, 0, j))],
        out_shape=[jax.ShapeDtypeStruct((2, T, F), BF16), jax.ShapeDtypeStruct((2, 3, F), F32),
                   jax.ShapeDtypeStruct((2, 1, F), F32)],
        scratch=[pltpu.VMEM((2, CONV_HALO, tn), F32)], carry=carry)(db16, wdn, h, hc, cw, cw)


def _pool_fwd(ext, xb_g, t_glob, win):
    e = ext
    sft = 1
    while sft < win:
        e = e + pltpu.roll(e, sft, 0)
        sft *= 2
    cnt = jnp.minimum(t_glob + 1.0, float(win))
    return e[POOL_HALO:] / cnt - xb_g


def gating_fwd(h0, lg, lb, ws, bsT, wp, sc, *, name, carry=()):
    T = h0.shape[0]
    DA = lg.shape[-1]
    DB = sc.shape[-1]
    HA = DA // A_HEAD
    G = len(B_WINDOWS)
    CG = DB // G
    tm = _tile(T, 512, A_CHUNK)
    n_c = tm // A_CHUNK

    def body(h_ref, halo_ref, lg_ref, lb_ref, ws_ref, bsT_ref, wp_ref, sc_ref, cat_ref):
        i = pl.program_id(0)
        hu = h_ref[:, 0:DA].astype(F32)
        hv = h_ref[:, DA:2 * DA].astype(F32)
        xb = h_ref[:, 2 * DA:].astype(F32)
        u = _gelu(hu)
        vn, _, _ = _ln_fwd(_gelu(hv), lg_ref[...], lb_ref[...])
        vnb = vn.astype(BF16)
        rr = lax.broadcasted_iota(jnp.int32, (A_CHUNK, A_CHUNK), 0)
        cc = lax.broadcasted_iota(jnp.int32, (A_CHUNK, A_CHUNK), 1)
        for hh in range(HA):
            wt = jnp.where(rr >= cc, ws_ref[hh], 0.0).astype(BF16)
            cs = slice(hh * A_HEAD, (hh + 1) * A_HEAD)
            for n in range(n_c):
                rs = slice(n * A_CHUNK, (n + 1) * A_CHUNK)
                s = _dot(wt, vnb[rs, cs]) + bsT_ref[:, hh:hh + 1]
                cat_ref[rs, cs] = (u[rs, cs] * s).astype(BF16)
        halo = jnp.where(i > 0, halo_ref[...].astype(F32), 0.0)
        ext = jnp.concatenate([halo, xb], axis=0)
        t_glob = (i * tm + lax.broadcasted_iota(jnp.int32, (tm, 1), 0)).astype(F32)
        for g, win in enumerate(B_WINDOWS):
            gs = slice(g * CG, (g + 1) * CG)
            p = _pool_fwd(ext[:, gs], xb[:, gs], t_glob, win)
            z = _dot(p.astype(BF16), wp_ref[g])
            cat_ref[:, DA + g * CG:DA + (g + 1) * CG] = (z * sc_ref[:, gs]).astype(BF16)

    full = lambda a: pl.BlockSpec(a.shape, lambda i: (0,) * a.ndim)
    hpb = tm // POOL_HALO
    return _call(
        body, name=name, grid=(T // tm,),
        in_specs=[pl.BlockSpec((tm, 2 * DA + DB), lambda i: (i, 0)),
                  pl.BlockSpec((POOL_HALO, DB), lambda i: (jnp.maximum(i * hpb - 1, 0), 2 * DA // DB)),
                  full(lg), full(lb), full(ws), full(bsT), full(wp), full(sc)],
        out_specs=pl.BlockSpec((tm, DA + DB), lambda i: (i, 0)),
        out_shape=jax.ShapeDtypeStruct((T, DA + DB), BF16), carry=carry)(h0, h0, lg, lb, ws, bsT, wp, sc)


def gating_bwd(h0, dcat, lg, lb, ws, wsT, bsT, wp, sc, *, name, carry=()):
    T = h0.shape[0]
    DA = lg.shape[-1]
    DB = sc.shape[-1]
    HA = DA // A_HEAD
    G = len(B_WINDOWS)
    CG = DB // G
    tm = _tile(T, 512, A_CHUNK)
    n_i = T // tm
    n_c = tm // A_CHUNK
    n_ext = tm + POOL_HALO

    def body(h_ref, halo_ref, dc_ref, dhalo_ref, lg_ref, lb_ref, ws_ref, wsT_ref, bsT_ref, wp_ref, sc_ref,
             dh_ref, dws_ref, dbsT_ref, dlg_ref, dlb_ref, dsc_ref, dwp_ref, dvn_sc):
        i = pl.program_id(0)

        @pl.when(i == 0)
        def _():
            for r in (dws_ref, dbsT_ref, dlg_ref, dlb_ref, dsc_ref, dwp_ref):
                r[...] = jnp.zeros_like(r)

        hu = h_ref[:, 0:DA].astype(F32)
        hv = h_ref[:, DA:2 * DA].astype(F32)
        xb = h_ref[:, 2 * DA:].astype(F32)
        u = _gelu(hu)
        gu = _gelu_grad(hu)
        lgv = lg_ref[...]
        vn, vhat, rstd = _ln_fwd(_gelu(hv), lgv, lb_ref[...])
        vnb = vn.astype(BF16)
        rr = lax.broadcasted_iota(jnp.int32, (A_CHUNK, A_CHUNK), 0)
        cc = lax.broadcasted_iota(jnp.int32, (A_CHUNK, A_CHUNK), 1)
        for hh in range(HA):
            wt = jnp.where(rr >= cc, ws_ref[hh], 0.0).astype(BF16)
            wtT = jnp.where(rr <= cc, wsT_ref[hh], 0.0).astype(BF16)
            cs = slice(hh * A_HEAD, (hh + 1) * A_HEAD)
            dws = jnp.zeros((A_CHUNK, A_CHUNK), F32)
            dbs = jnp.zeros((A_CHUNK, 1), F32)
            for n in range(n_c):
                rs = slice(n * A_CHUNK, (n + 1) * A_CHUNK)
                vb = vnb[rs, cs]
                s = _dot(wt, vb) + bsT_ref[:, hh:hh + 1]
                dya = dc_ref[rs, cs].astype(F32)
                ds = dya * u[rs, cs]
                dh_ref[rs, cs] = (dya * s * gu[rs, cs]).astype(BF16)
                dsb = ds.astype(BF16)
                dbs = dbs + jnp.sum(ds, axis=1, keepdims=True)
                dws = dws + _dot_nt(dsb, vb)
                dvn_sc[rs, cs] = _dot(wtT, dsb)
            dws_ref[hh] += jnp.where(rr >= cc, dws, 0.0)
            dbsT_ref[:, hh:hh + 1] += dbs
        dvg, dlg, dlb = _ln_bwd(dvn_sc[...], vhat, rstd, lgv)
        dlg_ref[...] += dlg
        dlb_ref[...] += dlb
        dh_ref[:, DA:2 * DA] = (dvg * _gelu_grad(hv)).astype(BF16)

        halo = jnp.where(i > 0, halo_ref[...].astype(F32), 0.0)
        ext = jnp.concatenate([halo, xb], axis=0)
        t_glob = (i * tm + lax.broadcasted_iota(jnp.int32, (tm, 1), 0)).astype(F32)
        t_ext = (i * tm + lax.broadcasted_iota(jnp.int32, (n_ext, 1), 0)).astype(F32)
        dyb = dc_ref[:, DA:].astype(F32)
        dhalo = jnp.where(i < n_i - 1, dhalo_ref[...].astype(F32), 0.0)
        dyb_ext = jnp.concatenate([dyb, dhalo], axis=0)
        for g, win in enumerate(B_WINDOWS):
            gs = slice(g * CG, (g + 1) * CG)
            pb = _pool_fwd(ext[:, gs], xb[:, gs], t_glob, win).astype(BF16)
            wpg = wp_ref[g]
            z = _dot(pb, wpg)
            dsc_ref[:, gs] += jnp.sum(dyb[:, gs] * z, axis=0, keepdims=True)
            dzb = (dyb_ext[:, gs] * sc_ref[:, gs]).astype(BF16)
            dwp_ref[g] += _dot_tn(pb, dzb[:tm])
            dp = _dot_nt(dzb, wpg)
            e = dp / jnp.minimum(t_ext + 1.0, float(win))
            sft = 1
            while sft < win:
                e = e + pltpu.roll(e, n_ext - sft, 0)
                sft *= 2
            dh_ref[:, 2 * DA + g * CG:2 * DA + (g + 1) * CG] = (e[:tm] - dp[:tm]).astype(BF16)

    full = lambda a: pl.BlockSpec(a.shape, lambda i: (0,) * a.ndim)
    hpb = tm // POOL_HALO
    n_hb = T // POOL_HALO
    outs = [jax.ShapeDtypeStruct((T, 2 * DA + DB), BF16), jax.ShapeDtypeStruct(ws.shape, F32),
            jax.ShapeDtypeStruct(bsT.shape, F32), jax.ShapeDtypeStruct(lg.shape, F32),
            jax.ShapeDtypeStruct(lb.shape, F32), jax.ShapeDtypeStruct(sc.shape, F32),
            jax.ShapeDtypeStruct(wp.shape, F32)]
    return _call(
        body, name=name, grid=(n_i,),
        in_specs=[pl.BlockSpec((tm, 2 * DA + DB), lambda i: (i, 0)),
                  pl.BlockSpec((POOL_HALO, DB), lambda i: (jnp.maximum(i * hpb - 1, 0), 2 * DA // DB)),
                  pl.BlockSpec((tm, DA + DB), lambda i: (i, 0)),
                  pl.BlockSpec((POOL_HALO, DB), lambda i: (jnp.minimum((i + 1) * hpb, n_hb - 1), DA // DB)),
                  full(lg), full(lb), full(ws), full(wsT), full(bsT), full(wp), full(sc)],
        out_specs=[pl.BlockSpec((tm, 2 * DA + DB), lambda i: (i, 0))] + [full(o) for o in outs[1:]],
        out_shape=outs,
        scratch=[pltpu.VMEM((tm, DA), F32)], carry=carry)(h0, h0, dcat, dcat, lg, lb, ws, wsT, bsT, wp, sc)


HP = 2 * C_HEAD


def _hgrn_gates(h_ref, rows, lbv, tri):
    pre = []
    for r in rows:
        blk = h_ref[r, :].astype(F32)
        q = blk[:, 0:HP]
        sg = _sigmoid(blk[:, HP:2 * HP])
        f = lbv + (1.0 - lbv) * sg
        pre.append(dict(q=q, sq=_sigmoid(q), sg=sg, f=f, k=1.0 - f, lf=jnp.log(f), v=blk[:, 2 * HP:3 * HP],
                        gg=blk[:, 3 * HP:4 * HP]))
    bcums = [_exact_tri_dot(tri, p["lf"]) for p in pre]
    out = []
    for p, bcum in zip(pre, bcums):
        blast = bcum[C_CHUNK - 1:C_CHUNK]
        e_in = jnp.exp(bcum)
        e_out = jnp.exp(-bcum)
        e_end = jnp.exp(blast - bcum)
        out.append(dict(p, e_in=e_in, e_out=e_out, e_end=e_end, qd=p["q"] * p["sq"] * e_in, kd=p["k"] * e_out,
                        ke=p["k"] * e_end, dec=jnp.exp(blast)))
    return out


def _lower_bound(lbp_ref):
    p0, p1 = lbp_ref[0:1], lbp_ref[1:2]
    mx = jnp.maximum(p0, p1)
    e0, e1 = jnp.exp(p0 - mx), jnp.exp(p1 - mx)
    return e1 / (e0 + e1)


def hgrn_fwd(h1p, lbp, gn, *, name, carry=()):
    T = h1p.shape[0]
    DC = gn.shape[-1]
    n_p = DC // HP
    tt = _tile(T, 1024, C_CHUNK)
    n_c = tt // C_CHUNK

    def body(h_ref, lbp_ref, gn_ref, y_ref, o_ref, sp_ref, st):
        i = pl.program_id(1)

        @pl.when(i == 0)
        def _():
            st[...] = jnp.zeros_like(st)

        lbv = _lower_bound(lbp_ref)
        gnv = gn_ref[...]
        rr = lax.broadcasted_iota(jnp.int32, (C_CHUNK, C_CHUNK), 0)
        cc = lax.broadcasted_iota(jnp.int32, (C_CHUNK, C_CHUNK), 1)
        causal = rr >= cc
        tri = jnp.where(causal, 1.0, 0.0).astype(BF16)

        heads = [slice(hd * C_HEAD, (hd + 1) * C_HEAD) for hd in range(2)]
        rows = [slice(n * C_CHUNK, (n + 1) * C_CHUNK) for n in range(n_c)]
        nh = [(n, hd) for n in range(n_c) for hd in range(2)]
        gates = _hgrn_gates(h_ref, rows, lbv, tri)
        b16 = lambda key: {(n, hd): gates[n][key][:, heads[hd]].astype(BF16) for n, hd in nh}
        qd, kd, ke, vb = b16("qd"), b16("kd"), b16("ke"), b16("v")
        att = {i: jnp.where(causal, _dot_nt(qd[i], kd[i]), 0.0).astype(BF16) for i in nh}
        intra = {i: _dot(att[i], vb[i]) for i in nh}
        upd = {i: _dot_tn(vb[i], ke[i]) for i in nh}
        s = [st[0], st[1]]
        entering = {}
        for n, hd in nh:
            entering[n, hd] = s[hd]
            sp_ref[hd, n] = s[hd]
            s[hd] = gates[n]["dec"][:, heads[hd]] * s[hd] + upd[n, hd]
        st[0], st[1] = s
        o = {i: intra[i] + _dot_nt(qd[i], entering[i].astype(BF16)) for i in nh}
        for n in range(n_c):
            os_ = [o[n, 0], o[n, 1]]
            o_ref[rows[n], :] = jnp.concatenate(os_, axis=1).astype(BF16)
            ys = [oh * lax.rsqrt(jnp.mean(oh * oh, axis=-1, keepdims=True) + LN_EPS) for oh in os_]
            y_ref[rows[n], :] = (jnp.concatenate(ys, axis=1) * gnv * _sigmoid(gates[n]["gg"])).astype(BF16)

    return _call(
        body, name=name, grid=(n_p, T // tt),
        in_specs=[pl.BlockSpec((tt, 4 * HP), lambda p, i: (i, p)), pl.BlockSpec((2, HP), lambda p, i: (0, p)),
                  pl.BlockSpec((1, HP), lambda p, i: (0, p))],
        out_specs=[pl.BlockSpec((tt, HP), lambda p, i: (i, p)), pl.BlockSpec((tt, HP), lambda p, i: (i, p)),
                   pl.BlockSpec((2, n_c, C_HEAD, C_HEAD), lambda p, i: (p, i, 0, 0))],
        out_shape=[jax.ShapeDtypeStruct((T, DC), BF16), jax.ShapeDtypeStruct((T, DC), BF16),
                   jax.ShapeDtypeStruct((2 * n_p, T // C_CHUNK, C_HEAD, C_HEAD), F32)],
        scratch=[pltpu.VMEM((2, C_HEAD, C_HEAD), F32)], carry=carry)(h1p, lbp, gn)


def hgrn_bwd(h1p, o_saved, dy, sp, lbp, gn, *, name, carry=()):
    T = h1p.shape[0]
    DC = gn.shape[-1]
    n_p = DC // HP
    tt = _tile(T, 512, C_CHUNK)
    n_i = T // tt
    n_c = tt // C_CHUNK

    def body(h_ref, o_ref, dy_ref, sp_ref, lbp_ref, gn_ref, dh_ref, dgn_ref, dlbp_ref, dst, dlb_acc):
        ip = pl.program_id(1)

        @pl.when(ip == 0)
        def _():
            dst[...] = jnp.zeros_like(dst)
            dlb_acc[...] = jnp.zeros_like(dlb_acc)
            dgn_ref[...] = jnp.zeros_like(dgn_ref)

        lbv = _lower_bound(lbp_ref)
        gnv = gn_ref[...]
        rr = lax.broadcasted_iota(jnp.int32, (C_CHUNK, C_CHUNK), 0)
        cc = lax.broadcasted_iota(jnp.int32, (C_CHUNK, C_CHUNK), 1)
        causal = rr >= cc
        tri = jnp.where(causal, 1.0, 0.0).astype(BF16)
        tri_t = jnp.where(rr <= cc, 1.0, 0.0).astype(BF16)
        last_row = lax.broadcasted_iota(jnp.int32, (C_CHUNK, 1), 0) == C_CHUNK - 1

        heads = [slice(hd * C_HEAD, (hd + 1) * C_HEAD) for hd in range(2)]
        rows = [slice(n * C_CHUNK, (n + 1) * C_CHUNK) for n in range(n_c)]
        cat = lambda parts: jnp.concatenate(parts, axis=1)
        nh = [(n, hd) for n in range(n_c) for hd in range(2)]
        pair = lambda d, n: cat([d[n, 0], d[n, 1]])
        gates = _hgrn_gates(h_ref, rows, lbv, tri)
        dgn = jnp.zeros((1, HP), F32)
        dgg, dob = [], {}
        for n in range(n_c):
            o = o_ref[rows[n], :].astype(F32)
            dyv = dy_ref[rows[n], :].astype(F32)
            sgg = _sigmoid(gates[n]["gg"])
            rrs = [lax.rsqrt(jnp.mean(o[:, cs] * o[:, cs], axis=-1, keepdims=True) + LN_EPS) for cs in heads]
            ohat = cat([o[:, cs] * r for cs, r in zip(heads, rrs)])
            dyn = dyv * sgg
            dgg.append(dyv * ohat * gnv * sgg * (1.0 - sgg))
            dgn = dgn + jnp.sum(dyn * ohat, axis=0, keepdims=True)
            dxh = dyn * gnv
            for hd, cs in enumerate(heads):
                dxh_h, oh_h = dxh[:, cs], ohat[:, cs]
                dob[n, hd] = (rrs[hd] * (dxh_h - oh_h * jnp.mean(dxh_h * oh_h, axis=-1, keepdims=True))).astype(BF16)
        dgn_ref[...] += dgn
        b16 = lambda key: {(n, hd): gates[n][key][:, heads[hd]].astype(BF16) for n, hd in nh}
        qd, kd, ke, vb = b16("qd"), b16("kd"), b16("ke"), b16("v")
        s_in = {(n, hd): sp_ref[hd, n] for n, hd in nh}
        att = {i: jnp.where(causal, _dot_nt(qd[i], kd[i]), 0.0).astype(BF16) for i in nh}
        datt = {i: jnp.where(causal, _dot_nt(dob[i], vb[i]), 0.0).astype(BF16) for i in nh}
        grow = {i: _dot_tn(dob[i], qd[i]) for i in nh}
        dv_i = {i: _dot_tn(att[i], dob[i]) for i in nh}
        dqd = {i: _dot(datt[i], kd[i]) + _dot(dob[i], s_in[i].astype(BF16)) for i in nh}
        dkd = {i: _dot_tn(datt[i], qd[i]) for i in nh}
        ds = [dst[0], dst[1]]
        leaving = {}
        for n in reversed(range(n_c)):
            for hd, cs in enumerate(heads):
                leaving[n, hd] = ds[hd]
                ds[hd] = gates[n]["dec"][:, cs] * ds[hd] + grow[n, hd]
        dst[0], dst[1] = ds
        dsb = {i: leaving[i].astype(BF16) for i in nh}
        dv = {i: dv_i[i] + _dot_nt(ke[i], dsb[i]) for i in nh}
        dke = {i: _dot(vb[i], dsb[i]) for i in nh}
        ddec = {i: jnp.sum(leaving[i] * s_in[i], axis=0, keepdims=True) for i in nh}
        mid = []
        for n in range(n_c):
            a = gates[n]
            dqd_n, dkd_n, dke_n = pair(dqd, n), pair(dkd, n), pair(dke, n)
            kek = dke_n * a["ke"]
            dblast = jnp.sum(kek, axis=0, keepdims=True) + pair(ddec, n) * a["dec"]
            dbcum = dqd_n * a["qd"] - dkd_n * a["kd"] - kek + jnp.where(last_row, dblast, 0.0)
            mid.append((dqd_n * a["e_in"], dkd_n * a["e_out"] + dke_n * a["e_end"], dbcum))
        dlf = [_exact_tri_dot(tri_t, m[2]) for m in mid]
        dlb = jnp.zeros((1, HP), F32)
        for n in range(n_c):
            a = gates[n]
            dqs, dk, _ = mid[n]
            df = dlf[n] / a["f"] - dk
            dlb = dlb + jnp.sum(df * (1.0 - a["sg"]), axis=0, keepdims=True)
            dfl = df * (1.0 - lbv) * a["sg"] * (1.0 - a["sg"])
            dq = dqs * a["sq"] * (1.0 + a["q"] * (1.0 - a["sq"]))
            dh_ref[rows[n], :] = cat([dq, dfl, pair(dv, n), dgg[n]]).astype(BF16)
        dlb_acc[...] += dlb

        @pl.when(ip == n_i - 1)
        def _():
            d1 = dlb_acc[...] * lbv * (1.0 - lbv)
            dlbp_ref[...] = jnp.concatenate([-d1, d1], axis=0)

    rev = lambda ip: n_i - 1 - ip
    return _call(
        body, name=name, grid=(n_p, n_i),
        in_specs=[pl.BlockSpec((tt, 4 * HP), lambda p, ip: (rev(ip), p)),
                  pl.BlockSpec((tt, HP), lambda p, ip: (rev(ip), p)),
                  pl.BlockSpec((tt, HP), lambda p, ip: (rev(ip), p)),
                  pl.BlockSpec((2, n_c, C_HEAD, C_HEAD), lambda p, ip: (p, rev(ip), 0, 0)),
                  pl.BlockSpec((2, HP), lambda p, ip: (0, p)), pl.BlockSpec((1, HP), lambda p, ip: (0, p))],
        out_specs=[pl.BlockSpec((tt, 4 * HP), lambda p, ip: (rev(ip), p)),
                   pl.BlockSpec((1, HP), lambda p, ip: (0, p)), pl.BlockSpec((2, HP), lambda p, ip: (0, p))],
        out_shape=[jax.ShapeDtypeStruct(h1p.shape, BF16), jax.ShapeDtypeStruct((1, DC), F32),
                   jax.ShapeDtypeStruct((2, DC), F32)],
        scratch=[pltpu.VMEM((2, C_HEAD, C_HEAD), F32), pltpu.VMEM((1, HP), F32)], carry=carry)(h1p, o_saved, dy, sp, lbp, gn)


def _mesh_pos():
    x, y, c = lax.axis_index("x"), lax.axis_index("y"), lax.axis_index("c")
    chips = [(1 - x, y), (x, 1 - y), (1 - x, 1 - y)]
    return x, y, c, chips


def _remote(src, dst, send, recv, j, dev):
    return pltpu.make_async_remote_copy(src_ref=src, dst_ref=dst, send_sem=send.at[j], recv_sem=recv.at[j],
                                        device_id=dev, device_id_type=MESH)


def mesh_ids():
    x, y, c, chips = _mesh_pos()
    return jnp.stack([c] + [2 * cx + cy for cx, cy in chips] + [2 * x + y]).astype(jnp.int32)


def _sibling():
    return (lax.axis_index("x"), lax.axis_index("y"), 1 - lax.axis_index("c"))


SWAP_PARTS = 2


def _swap_with_sibling(src, recv, send_sem, recv_sem, step):
    slot = step % 2
    br = src.shape[0]
    n = SWAP_PARTS if br % (16 * SWAP_PARTS) == 0 else 1
    parts = []
    for k in range(n):
        rows = pl.ds(k * (br // n), br // n)
        cp = pltpu.make_async_remote_copy(
            src_ref=src.at[rows], dst_ref=recv.at[slot, rows], send_sem=send_sem.at[slot * SWAP_PARTS + k],
            recv_sem=recv_sem.at[slot * SWAP_PARTS + k], device_id=_sibling(), device_id_type=MESH)
        cp.start()
        parts.append((cp, slot, rows))
    return parts


def _swap_scratch(br, C, dtype):
    return [pltpu.VMEM((2, br, C), dtype), pltpu.SemaphoreType.DMA((2 * SWAP_PARTS,)),
            pltpu.SemaphoreType.DMA((2 * SWAP_PARTS,))]


def _swap_rows(Rh, C, itemsize):
    return _tile(Rh, max(16, (3 << 20) // (C * itemsize)), 16)


def cast_to_slot(a3, l, me1, name):
    _, R, C = a3.shape
    br = _tile(R, max(8, (1 << 20) // C), 16)

    def body(me_ref, a_ref, o_ref):
        o_ref[...] = a_ref[...].astype(BF16)

    return _call(body, name=name, grid=(R // br,), prefetch=1,
                 in_specs=[pl.BlockSpec((None, br, C), lambda r, me: (l, r, 0))],
                 out_specs=pl.BlockSpec((None, None, br, C), lambda r, me: (0, me[0], r, 0)),
                 out_shape=jax.ShapeDtypeStruct((1, 4, R, C), BF16))(me1, a3)


def all_gather_chips(big, small, *, name):
    nb, ns = len(big), len(small)
    layers = [(t, l) for t in range(nb) for l in range(big[t].shape[0])]
    n_big = 3 * len(layers)
    n_rem = n_big + 3 * ns

    def body(*refs):
        small_in = refs[nb:nb + ns]
        bufs, small_out = refs[nb + ns:2 * nb + ns], refs[2 * nb + ns:2 * (nb + ns)]
        send, recv, loc = refs[2 * (nb + ns):]
        x, y, c, chips = _mesh_pos()
        me = 2 * x + y
        ids = [2 * cx + cy for cx, cy in chips]
        started, sends = [], []
        for t in range(ns):
            cp = pltpu.make_async_copy(small_in[t], small_out[t].at[me], loc.at[t])
            cp.start()
            started.append(cp)
        for q, (t, l) in enumerate(layers):
            for k, chip in enumerate(chips):
                blk = bufs[t].at[l, me, c]
                cp = _remote(blk, blk, send, recv, 3 * q + k, (*chip, c))
                cp.start()
                sends.append(cp)
        for t in range(ns):
            for k, chip in enumerate(chips):
                cp = _remote(small_in[t], small_out[t].at[me], send, recv, n_big + 3 * t + k, (*chip, c))
                cp.start()
                sends.append(cp)
        for q, (t, l) in enumerate(layers):
            for k in range(3):
                blk = bufs[t].at[l, ids[k], c]
                _remote(blk, blk, send, recv, 3 * q + k, (x, y, c)).wait_recv()
        for t in range(ns):
            for k in range(3):
                blk = small_out[t].at[ids[k]]
                _remote(blk, blk, send, recv, n_big + 3 * t + k, (x, y, c)).wait_recv()
        for cp in sends:
            cp.wait_send()
        for cp in started:
            cp.wait()

    out_shape = [jax.ShapeDtypeStruct(a.shape, a.dtype) for a in big]
    out_shape += [jax.ShapeDtypeStruct((4,) + a.shape, a.dtype) for a in small]
    return _call(body, name=name, in_specs=[ANY] * (nb + ns), out_specs=[ANY] * (nb + ns), out_shape=out_shape,
                 aliases={t: t for t in range(nb)},
                 scratch=[pltpu.SemaphoreType.DMA((n_rem,)), pltpu.SemaphoreType.DMA((n_rem,)),
                          pltpu.SemaphoreType.DMA((max(ns, 1),))])(*big, *small)


def all_gather_pair(buf, ids, *, name):
    L, _, _, Rh, C = buf.shape
    br = _swap_rows(Rh, C, 2)
    n_r = Rh // br

    def body(ids_ref, in_ref, o_ref, recv, ssem, rsem):
        step = (pl.program_id(0) * 3 + pl.program_id(1)) * n_r + pl.program_id(2)
        parts = _swap_with_sibling(in_ref, recv, ssem, rsem, step)
        for cp, slot, rows in parts:
            cp.wait_recv()
            o_ref[rows, :] = recv[slot, rows, :]
        for cp, _, _ in parts:
            cp.wait_send()

    at = lambda l, s, h, r: (((l * 4 + s) * 2 + h) * n_r + r, 0)
    out = _call(body, name=name, grid=(L, 3, n_r), prefetch=1,
                in_specs=[pl.BlockSpec((br, C), lambda l, k, r, ids: at(l, ids[1 + k], ids[0], r))],
                out_specs=pl.BlockSpec((br, C), lambda l, k, r, ids: at(l, ids[1 + k], 1 - ids[0], r)),
                out_shape=jax.ShapeDtypeStruct((L * 8 * Rh, C), buf.dtype), aliases={1: 0},
                scratch=_swap_scratch(br, C, BF16))(ids, buf.reshape(L * 8 * Rh, C))
    return out.reshape(buf.shape)


def rs_pair_add(grad, ids, *, name):
    _, _, Rh, C = grad.shape
    br = _swap_rows(Rh, C, 2)
    n_r = Rh // br

    def body(ids_ref, send_ref, keep_ref, pb_ref, own_ref, recv, ssem, rsem):
        ph = pl.program_id(0)
        parts = _swap_with_sibling(send_ref, recv, ssem, rsem, ph * n_r + pl.program_id(1))
        for cp, slot, rows in parts:
            cp.wait_recv()
            s = keep_ref[rows, :].astype(F32) + recv[slot, rows, :].astype(F32)

            @pl.when(ph == 0)
            def _():
                own_ref[rows, :] = s

            @pl.when(ph > 0)
            def _():
                pb_ref[rows, :] = s.astype(BF16)

        for cp, _, _ in parts:
            cp.wait_send()

    rel = lambda ph: (ph + 3) % 4
    at = lambda s, h, r: ((s * 2 + h) * n_r + r, 0)
    grad = grad.reshape(8 * Rh, C)
    return _call(
        body, name=name, grid=(4, n_r), prefetch=1,
        in_specs=[pl.BlockSpec((br, C), lambda ph, r, ids: at(ids[1 + rel(ph)], 1 - ids[0], r)),
                  pl.BlockSpec((br, C), lambda ph, r, ids: at(ids[1 + rel(ph)], ids[0], r))],
        out_specs=[pl.BlockSpec((None, br, C), lambda ph, r, ids: (jnp.maximum(ph - 1, 0), jnp.where(ph == 0, 0, r), 0)),
                   pl.BlockSpec((br, C), lambda ph, r, ids: (jnp.where(ph == 0, r, n_r - 1), 0))],
        out_shape=[jax.ShapeDtypeStruct((3, Rh, C), BF16), jax.ShapeDtypeStruct((Rh, C), F32)],
        scratch=_swap_scratch(br, C, BF16))(ids, grad, grad)


def rs_finish(owns, gots, *, name):
    L = len(owns)
    Rh, C = owns[0].shape
    br = _swap_rows(Rh, C, 4)
    n_r = Rh // br

    def body(*refs):
        own_refs, got_refs, o_ref = refs[:L], refs[L:2 * L], refs[2 * L]
        recv, ssem, rsem = refs[2 * L + 1:]
        l = pl.program_id(0)
        c = lax.axis_index("c")
        for ll in range(L):
            @pl.when(l == ll)
            def _():
                s = own_refs[ll][...]
                for k in range(3):
                    s = s + got_refs[ll][k].astype(F32)
                o_ref[c] = s

        parts = _swap_with_sibling(o_ref.at[c], recv, ssem, rsem, l * n_r + pl.program_id(1))
        for cp, slot, rows in parts:
            cp.wait_recv()
            o_ref[1 - c, rows, :] = recv[slot, rows, :]
        for cp, _, _ in parts:
            cp.wait_send()

    def at_layer(ll):
        return lambda l, r: jnp.where(l == ll, r, jnp.where(l < ll, 0, n_r - 1))

    in_specs = [pl.BlockSpec((br, C), lambda l, r, ll=ll: (at_layer(ll)(l, r), 0)) for ll in range(L)]
    in_specs += [pl.BlockSpec((3, br, C), lambda l, r, ll=ll: (0, at_layer(ll)(l, r), 0)) for ll in range(L)]
    out = _call(body, name=name, grid=(L, n_r), in_specs=in_specs,
                out_specs=pl.BlockSpec((2, br, C), lambda l, r: (l, r, 0)),
                out_shape=jax.ShapeDtypeStruct((L * 2, Rh, C), F32),
                scratch=_swap_scratch(br, C, F32))(*owns, *gots)
    return out.reshape(L, 2, Rh, C)


SMALL_ROW_MULTIPLE = 64


def all_reduce_small(buf, *, name):
    rows = buf.shape[0]
    h, q, e = rows // 2, rows // 4, rows // 8

    def body(x_ref, o_ref, s1, r1, s2, r2, s3, r3, send, recv):
        x, y, c, _ = _mesh_pos()
        sib, xn, yn = (x, y, 1 - c), (1 - x, y, c), (x, 1 - y, c)
        at = lambda off, n: pl.ds(pl.multiple_of(off, 8), n)
        cp = _remote(x_ref.at[at((1 - c) * h, h)], r1, send, recv, 0, sib)
        cp.start()
        cp.wait()
        s1[...] = x_ref[at(c * h, h), :] + r1[...]
        cp = _remote(s1.at[at((1 - x) * q, q)], r2, send, recv, 1, xn)
        cp.start()
        cp.wait()
        s2[...] = s1[at(x * q, q), :] + r2[...]
        cp = _remote(s2.at[at((1 - y) * e, e)], r3, send, recv, 2, yn)
        cp.start()
        cp.wait()
        s3[...] = s2[at(y * e, e), :] + r3[...]
        mine = c * h + x * q + y * e
        o_ref[at(mine, e), :] = s3[...]
        theirs = o_ref.at[at(c * h + x * q + (1 - y) * e, e)]
        cp = _remote(s3, o_ref.at[at(mine, e)], send, recv, 3, yn)
        cp.start()
        cp.wait_send()
        _remote(theirs, theirs, send, recv, 3, yn).wait_recv()
        quarter = o_ref.at[at(c * h + x * q, q)]
        theirs = o_ref.at[at(c * h + (1 - x) * q, q)]
        cp = _remote(quarter, quarter, send, recv, 4, xn)
        cp.start()
        cp.wait_send()
        _remote(theirs, theirs, send, recv, 4, xn).wait_recv()
        half = o_ref.at[at(c * h, h)]
        theirs = o_ref.at[at((1 - c) * h, h)]
        cp = _remote(half, half, send, recv, 5, sib)
        cp.start()
        cp.wait_send()
        _remote(theirs, theirs, send, recv, 5, sib).wait_recv()

    part = lambda n: pltpu.VMEM((n, LANES), F32)
    return _call(body, name=name, in_specs=[VMEM_SPEC], out_specs=VMEM_SPEC,
                 out_shape=jax.ShapeDtypeStruct(buf.shape, F32),
                 scratch=[part(h), part(h), part(q), part(q), part(e), part(e), pltpu.SemaphoreType.DMA((6,)),
                          pltpu.SemaphoreType.DMA((6,))])(buf)


def _pack(arrs, row_multiple=8):
    parts = []
    for a in arrs:
        f = a.reshape(-1).astype(F32)
        parts.append(jnp.pad(f, (0, (-f.shape[0]) % PACK_ALIGN)))
    total = sum(p.shape[0] for p in parts)
    parts.append(jnp.zeros(((-total) % (row_multiple * LANES),), F32))
    return jnp.concatenate(parts).reshape(-1, LANES)


def _unpack(buf, shapes):
    flat = buf.reshape(-1)
    out, off = [], 0
    for s in shapes:
        n = math.prod(s)
        out.append(flat[off:off + n].reshape(s))
        off += n + (-n) % PACK_ALIGN
    return out


_WEIGHTS = ['ev_w_in', 'ev_ln_v_g', 'ev_ln_v_b', 'ev_w_s', 'ev_b_s', 'ev_w_pool', 'ev_pool_scale', 'ev_w_out',
            'od_w_in', 'od_norm_g', 'od_w_out', 'lb_param', 'ffn_w_up', 'ffn_conv_w', 'ffn_conv_b', 'ffn_w_down',
            'ln1_g', 'ln1_b', 'ln2_g', 'ln2_b']
_BIG = ['ev_w_in', 'ev_w_out', 'od_w_in', 'od_w_out', 'ffn_w_up', 'ffn_w_down']
_SMALL = [n for n in _WEIGHTS if n not in _BIG]


def kernel(x, ev_w_in, ev_ln_v_g, ev_ln_v_b, ev_w_s, ev_b_s, ev_w_pool, ev_pool_scale, ev_w_out, od_w_in, od_norm_g, od_w_out, lb_param, ffn_w_up, ffn_conv_w, ffn_conv_b, ffn_w_down, ln1_g, ln1_b, ln2_g, ln2_b, loss_target, m_ev_w_in, m_ev_ln_v_g, m_ev_ln_v_b, m_ev_w_s, m_ev_b_s, m_ev_w_pool, m_ev_pool_scale, m_ev_w_out, m_od_w_in, m_od_norm_g, m_od_w_out, m_lb_param, m_ffn_w_up, m_ffn_conv_w, m_ffn_conv_b, m_ffn_w_down, m_ln1_g, m_ln1_b, m_ln2_g, m_ln2_b, v_ev_w_in, v_ev_ln_v_g, v_ev_ln_v_b, v_ev_w_s, v_ev_b_s, v_ev_w_pool, v_ev_pool_scale, v_ev_w_out, v_od_w_in, v_od_norm_g, v_od_w_out, v_lb_param, v_ffn_w_up, v_ffn_conv_w, v_ffn_conv_b, v_ffn_w_down, v_ln1_g, v_ln1_b, v_ln2_g, v_ln2_b):
    given = dict(locals())
    w = {n: given[n] for n in _WEIGHTS}
    mom = {n: given["m_" + n] for n in _WEIGHTS}
    vel = {n: given["v_" + n] for n in _WEIGHTS}
    x2d = x[0]
    tgt = loss_target[0]
    T, D = x2d.shape
    DA = ev_ln_v_g.shape[-1]
    DB = ev_pool_scale.shape[-1]
    HA = ev_w_s.shape[1]
    G = len(B_WINDOWS)
    CG = DB // G
    DC = 4 * od_norm_g.shape[-1]
    F = ffn_conv_b.shape[-1] // 2
    chip = 2 * lax.axis_index("x") + lax.axis_index("y")

    ids = mesh_ids()
    halves = lambda a: a.reshape(1, 4, 2, a.shape[2] // 2, a.shape[3])
    slot = {(n, l): halves(cast_to_slot(w[n], l, ids[4:5], f"cast_{n}{l}"))
            for n in _BIG for l in range(w[n].shape[0])}

    def riding(*keys):
        return [IciCopy("gather", slot[k]) for k in keys]

    def pair(buf, key):
        g = all_gather_pair(buf, ids, name=f"all_gather_pair_{key[0]}{key[1]}")
        return g.reshape(1, 4, g.shape[3] * 2, g.shape[4])

    early = [('ev_w_in', 0)]
    gathered = all_gather_chips([slot[k] for k in early], [ev_w_pool[0], ffn_conv_w, od_norm_g],
                                name="all_gather_chips")
    wpool_full = gathered[1].transpose(1, 0, 2, 3).reshape(G, CG, CG)
    cw_full = gathered[2].transpose(1, 2, 0, 3).reshape(DEPTH, 3, 2 * F)
    gn_full = gathered[3].reshape(1, DC)
    win0 = pair(gathered[0], early[0])[0]
    wup, wdn = {}, {}
    rh_up = D // 2
    cut1 = (rh_up * 7 // 100) // 16 * 16
    cut2 = cut1 + (rh_up * 13 // 100) // 16 * 16
    cb3 = ffn_conv_b.reshape(DEPTH, 1, 2 * F)
    ws = ev_w_s[0]
    wsT = jnp.swapaxes(ws, 1, 2)
    bsT = ev_b_s[0].T
    wpb = wpool_full.astype(BF16)
    ones = jnp.ones((1, D), F32)
    zeros = jnp.zeros((1, D), F32)
    row = lambda a, l: a[l:l + 1]

    Ns0 = win0.shape[-1]
    Nu = ffn_w_up.shape[-1]
    tm_big = _tile(T, 1024, 8)
    tm_ln = _tile(T, 512, 8)
    tk_ln = _tile(D, 512)
    n_p = DC // HP

    def nat_spec(Ns, tnw):
        nps = Ns // tnw
        return pl.BlockSpec((1, D, tnw), lambda i, j: (j // nps, 0, j % nps))

    perm_spec = pl.BlockSpec((4, D, HP), lambda i, j: (0, 0, j))

    xb16 = cast_bf16(x, "cast_x")[0]
    up0 = IciCopy("gather", slot[('ffn_w_up', 0)], rows=(0, cut1))
    ride = riding(('ev_w_out', 0))
    h0 = mm_nn(xb16, win0, w_spec=nat_spec(Ns0, Ns0), P=1, tnw=Ns0, tm=tm_big, n_j=4, name="ev_in",
               carry=ride + [up0])
    wout0 = pair(ride[0].out, ('ev_w_out', 0)).reshape(DA + DB, D)
    up0 = IciCopy("gather", up0.out, rows=(cut1, cut2 - cut1))
    cat = gating_fwd(h0, ev_ln_v_g, ev_ln_v_b, ws, bsT, wpb, ev_pool_scale, name="gating_fwd", carry=[up0])
    up0 = IciCopy("gather", up0.out, rows=(cut2, rh_up - cut2))

    def mix_ln(a, wmat, res, l, name, carry=()):
        K = a.shape[1]
        tk = _tile(K, 2048)
        return mm_ln(a, wmat, *res, row(ln1_g, l), row(ln1_b, l), w_spec=pl.BlockSpec((tk, D), lambda i, k: (k, 0)),
                     K=K, tk=tk, tm=tm_ln, name=name, carry=carry)

    def ffn_down(f, res, l, carry=()):
        tk = F // 4 if (F // 4) % LANES == 0 else _tile(F, 512)
        return mm_ln(f, wdn[l], *res, row(ln2_g, l), row(ln2_b, l),
                     w_spec=pl.BlockSpec((None, tk, D), lambda i, k: (0, k, 0)), K=F, tk=tk, tm=tm_ln,
                     name=f"ffn_down{l}", carry=carry)

    xh1, y1, rs1 = mix_ln(cat, wout0, (x2d, ones, zeros), 0, "ev_out", carry=[up0])
    wup[0] = pair(up0.out, ('ffn_w_up', 0))
    res1 = (xh1, row(ln1_g, 0), row(ln1_b, 0))
    ride = riding(('ffn_w_down', 0), ('od_w_in', 0))
    hf0, hc0, f0 = ffn_up(y1, wup[0], cw_full, cb3, 0, name="ffn_up0", carry=ride)
    wdn[0] = pair(ride[0].out, ('ffn_w_down', 0)).reshape(1, F, D)
    win1 = pair(ride[1].out, ('od_w_in', 0))[0]
    cut1 = (rh_up * 40 // 100) // 16 * 16
    cut2 = 2 * cut1
    up1 = IciCopy("gather", slot[('ffn_w_up', 1)], rows=(0, cut1))
    ride = riding(('od_w_out', 0))
    xh2, y2, rs2 = ffn_down(f0, res1, 0, carry=ride + [up1])
    wout1 = pair(ride[0].out, ('od_w_out', 0)).reshape(DC, D)
    res2 = (xh2, row(ln2_g, 0), row(ln2_b, 0))
    up1 = IciCopy("gather", up1.out, rows=(cut1, cut2 - cut1))
    h1p = mm_nn(y2, win1, w_spec=perm_spec, P=4, tnw=HP, tm=tm_big, n_j=n_p, name="od_in", carry=[up1])
    up1 = IciCopy("gather", up1.out, rows=(cut2, rh_up - cut2))
    yh, o_saved, sp = hgrn_fwd(h1p, lb_param, gn_full, name="hgrn_fwd", carry=[up1])
    wup[1] = pair(up1.out, ('ffn_w_up', 1))
    xh3, y3, rs3 = mix_ln(yh, wout1, res2, 1, "od_out")
    res3 = (xh3, row(ln1_g, 1), row(ln1_b, 1))
    ride = riding(('ffn_w_down', 1))
    hf1, hc1, f1 = ffn_up(y3, wup[1], cw_full, cb3, 1, name="ffn_up1", carry=ride)
    wdn[1] = pair(ride[0].out, ('ffn_w_down', 1)).reshape(1, F, D)
    tk_dn = F // 4 if (F // 4) % LANES == 0 else _tile(F, 512)
    loss_p, dr, drb, dg_ln2_1, db_ln2_1 = mm_ln_loss(
        f1, wdn[1], *res3, row(ln2_g, 1), row(ln2_b, 1), tgt,
        w_spec=pl.BlockSpec((None, tk_dn, D), lambda i, k: (0, k, 0)), K=F, tk=tk_dn, tm=tm_ln, name="ffn_down1_loss")

    tt = _tile(T, 2048, 16)
    n_t = T // tt
    tnu = Nu // 2 if (Nu // 2) % LANES == 0 else Nu
    upb = Nu // tnu
    tkd = _tile(D, 1024)

    def pair_sum(g4, name):
        pb, own = rs_pair_add(g4.reshape(4, 2, g4.shape[1] // 2, g4.shape[2]), ids, name="rs_pair_add_" + name)
        return IciCopy("scatter", pb), own

    def g_out(a, gb, name, tkk=None):
        K = a.shape[1]
        tkk = tkk or _tile(K, 1024)
        return mm_tn(a, gb, a_spec=pl.BlockSpec((tt, tkk), lambda kb, nb, t: (t, kb)),
                     g_spec=pl.BlockSpec((tt, tkd), lambda kb, nb, t: (t, nb)),
                     o_spec=pl.BlockSpec((1, tkk, tkd), lambda kb, nb, t: (0, kb, nb)), out_shape=(1, K, D),
                     grid=(K // tkk, D // tkd, n_t), acc_shape=(tkk, tkd), P=1, tnw=tkd, name=name)

    def ffn_bwd(l, dr2, dr2b, f, hf, hc, y_in, xh_in, rs_in):
        g_dn = g_out(f, dr2b, f"g_ffn_down{l}", tkk=F // 4 if (F // 4) % LANES == 0 else None)
        rs_dn = pair_sum(g_dn.reshape(4, F // 4, D), f"ffn_down{l}")
        dh, dcw, dcb = ffn_dgate(dr2b, wdn[l], hf, hc, cw_full, l, name=f"ffn_dgate{l}", carry=[rs_dn[0]])
        g_up = mm_tn(y_in, dh, a_spec=pl.BlockSpec((tt, tkd), lambda kb, nb, t: (t, kb)),
                     g_spec=pl.BlockSpec((None, tt, tnu), lambda kb, nb, t: (nb // (2 * upb), t, nb % (2 * upb))),
                     o_spec=pl.BlockSpec((1, tkd, tnu), lambda kb, nb, t: (nb // upb, kb, nb % upb)),
                     out_shape=(4, D, Nu), grid=(D // tkd, 4 * upb, n_t), acc_shape=(tkd, tnu), P=1, tnw=tnu,
                     name=f"g_ffn_up{l}")
        rs_up = pair_sum(g_up, f"ffn_up{l}")
        tku = tnu
        kps = Nu // tku
        out = mm_nt_res(dh, wup[l], dr2, (xh_in, rs_in, row(ln1_g, l)),
                        a_spec=pl.BlockSpec((None, tm_ln, tku), lambda i, k: (k // (2 * kps), i, k % (2 * kps))),
                        w_spec=pl.BlockSpec((None, 1, D, tku), lambda i, k: (0, k // kps, 0, k % kps)),
                        P=1, tnw=tku, n_k=4 * kps, tm=tm_ln, name=f"d_ffn_in{l}", carry=[rs_up[0]])
        return rs_dn, rs_up, dcw, dcb, out

    rs_dn1, rs_up1, dcw1, dcb1, (dr1, dr1b, dg_ln1_1, db_ln1_1) = ffn_bwd(1, dr, drb, f1, hf1, hc1, y3, xh3, rs3)
    rs_wout1 = pair_sum(g_out(yh, dr1b, "g_od_out").reshape(4, DC // 4, D), "od_out")
    dyh = mm_nt_plain(dr1b, wout1, tm=tm_big, tn=_tile(DC, 512), name="d_od_out")
    dh1p, d_gn, d_lbp = hgrn_bwd(h1p, o_saved, dyh, sp, lb_param, gn_full, name="hgrn_bwd", carry=[rs_wout1[0]])
    g_win1 = mm_tn(y2, dh1p, a_spec=pl.BlockSpec((tt, tkd), lambda kb, nb, t: (t, kb)),
                   g_spec=pl.BlockSpec((tt, 4 * HP), lambda kb, nb, t: (t, nb)),
                   o_spec=pl.BlockSpec((4, tkd, HP), lambda kb, nb, t: (0, kb, nb)), out_shape=(4, D, DC),
                   grid=(D // tkd, n_p, n_t), acc_shape=(tkd, 4 * HP), P=4, tnw=HP, name="g_od_in")
    rs_win1 = pair_sum(g_win1, "od_in")
    dr, drb, dg_ln2_0, db_ln2_0 = mm_nt_res(
        dh1p, win1, dr1, (xh2, rs2, row(ln2_g, 0)), a_spec=pl.BlockSpec((tm_ln, 4 * HP), lambda i, k: (i, k)),
        w_spec=pl.BlockSpec((4, D, HP), lambda i, k: (0, 0, k)), P=4, tnw=HP, n_k=n_p, tm=tm_ln, name="d_od_in",
        carry=[rs_win1[0]])
    rs_dn0, rs_up0, dcw0, dcb0, (dr1, dr1b, dg_ln1_0, db_ln1_0) = ffn_bwd(0, dr, drb, f0, hf0, hc0, y1, xh1, rs1)
    rs_wout0 = pair_sum(g_out(cat, dr1b, "g_ev_out").reshape(4, (DA + DB) // 4, D), "ev_out")
    dcat = mm_nt_plain(dr1b, wout0, tm=tm_big, tn=_tile(DA + DB, 512), name="d_ev_out")
    dh0, d_ws, d_bsT, d_lg, d_lb, d_sc, d_wp = gating_bwd(h0, dcat, ev_ln_v_g, ev_ln_v_b, ws, wsT, bsT, wpb,
                                                          ev_pool_scale, name="gating_bwd", carry=[rs_wout0[0]])
    g_win0 = mm_tn(xb16, dh0, a_spec=pl.BlockSpec((tt, tkd), lambda kb, nb, t: (t, kb)),
                   g_spec=pl.BlockSpec((tt, Ns0), lambda kb, nb, t: (t, nb)),
                   o_spec=pl.BlockSpec((1, tkd, Ns0), lambda kb, nb, t: (nb, kb, 0)), out_shape=(4, D, Ns0),
                   grid=(D // tkd, 4, n_t), acc_shape=(tkd, Ns0), P=1, tnw=Ns0, name="g_ev_in")
    rs_win0 = pair_sum(g_win0, "ev_in")
    grad_x = mm_nt_res(dh0, win0, dr1, None, a_spec=pl.BlockSpec((tm_ln, Ns0), lambda i, k: (i, k)),
                       w_spec=pl.BlockSpec((1, D, Ns0), lambda i, k: (k, 0, 0)), P=1, tnw=Ns0, n_k=4, tm=tm_ln,
                       name="d_ev_in", carry=[rs_win0[0]])

    per_weight = [[rs_win0], [rs_wout0], [rs_win1], [rs_wout1], [rs_up0, rs_up1], [rs_dn0, rs_dn1]]
    shared = [rs_finish([own for _, own in m], [cp.out for cp, _ in m], name="rs_finish_" + n)
              for n, m in zip(_BIG, per_weight)]
    big_g = {n: s.reshape(w[n].shape) for n, s in zip(_BIG, shared)}

    small_full = {
        'ev_ln_v_g': d_lg, 'ev_ln_v_b': d_lb, 'ev_w_s': d_ws[None], 'ev_b_s': d_bsT.T[None], 'ev_w_pool': d_wp[None],
        'ev_pool_scale': d_sc, 'od_norm_g': d_gn, 'lb_param': d_lbp,
        'ffn_conv_w': jnp.stack([jnp.concatenate([dcw0[0], dcw0[1]], axis=-1),
                                 jnp.concatenate([dcw1[0], dcw1[1]], axis=-1)]),
        'ffn_conv_b': jnp.stack([jnp.concatenate([dcb0[0, 0], dcb0[1, 0]]), jnp.concatenate([dcb1[0, 0], dcb1[1, 0]])]),
        'ln1_g': jnp.concatenate([dg_ln1_0, dg_ln1_1]), 'ln1_b': jnp.concatenate([db_ln1_0, db_ln1_1]),
        'ln2_g': jnp.concatenate([dg_ln2_0, dg_ln2_1]), 'ln2_b': jnp.concatenate([db_ln2_0, db_ln2_1])}
    packed = _pack([small_full[n] for n in _SMALL] + [loss_p[0, 0:1]], SMALL_ROW_MULTIPLE)
    reduced = _unpack(all_reduce_small(packed, name="all_reduce_small"),
                      [small_full[n].shape for n in _SMALL] + [(1,)])
    small_g = dict(zip(_SMALL, reduced[:-1]))
    loss = reduced[-1][0]
    small_g['ev_w_pool'] = lax.dynamic_slice_in_dim(small_g['ev_w_pool'], chip * (CG // 4), CG // 4, axis=2)
    small_g['ffn_conv_w'] = lax.dynamic_slice_in_dim(small_g['ffn_conv_w'], chip * (F // 2), F // 2, axis=2)
    small_g['od_norm_g'] = lax.dynamic_slice_in_dim(small_g['od_norm_g'], chip * (DC // 4), DC // 4, axis=1)

    grads, delta, new_m, new_v = {}, {}, {}, {}
    for n in _BIG:
        grads[n], delta[n], new_m[n], new_v[n] = adamw(w[n], big_g[n], mom[n], vel[n], "adamw_" + n)
    upd = adamw_small(*[[d[n] for n in _SMALL] for d in (w, small_g, mom, vel)], "adamw_small")
    for d, outs in zip((grads, delta, new_m, new_v), upd):
        d.update(zip(_SMALL, outs))

    return (loss, grad_x[None], *[grads[n] for n in _WEIGHTS], *[delta[n] for n in _WEIGHTS],
            *[new_m[n] for n in _WEIGHTS], *[new_v[n] for n in _WEIGHTS])
```
